```python
import math
import jax, jax.numpy as jnp
from jax import lax
import numpy as np

D_MODEL = 2048
BATCH = 2
SEQ = 4096
DEPTH = 1

REC_HEAD_DIM = 128
REC_HEADS = D_MODEL // 256
D_REC = REC_HEADS * REC_HEAD_DIM
REC_CHUNK = 64
ATT_HEAD_DIM = 128
ATT_HEADS_PER_GROUP = D_MODEL // 512
DIL_GROUPS = ((128, 1), (512, 4), (2048, 16))
N_GROUPS = 3
ATT_HEADS = N_GROUPS * ATT_HEADS_PER_GROUP
D_ATT = ATT_HEADS * ATT_HEAD_DIM
D_ATT_OUT = ATT_HEADS_PER_GROUP * ATT_HEAD_DIM
ATT_BLOCK = 64
NUM_BUCKETS = 32
MAX_DISTANCE = 1024
N_EXPERTS = 32
TOP_K = 4
D_FF = D_MODEL
SWIGLU_LIMIT = 7.0
SWIGLU_ALPHA = 1.702
EXPERT_BLOCK = 128
RMS_EPS = 1e-6
NEG_INF = -1e30
IN_WIDTHS = (D_REC, D_REC, D_REC, D_REC, D_REC, D_ATT, D_ATT, D_ATT, D_MODEL, D_MODEL)
D_IN = 5 * D_REC + 3 * D_ATT + 2 * D_MODEL

kernel_name = 'hybrid_hgrn2_dilated_attn_moe_block'


def rms_norm(x, g):
    xf = x.astype(jnp.float32)
    y = xf * lax.rsqrt(jnp.mean(xf * xf, axis=-1, keepdims=True) + RMS_EPS)
    return (y * g.astype(jnp.float32)).astype(x.dtype)


def modulate(h, shift, scale):
    return h * (1 + scale) + shift


def gla_scan(q, k, v, logf):
    Bn, S, H, Dk = q.shape
    Dv = v.shape[-1]
    n_chunks = S // REC_CHUNK

    def chunks(t):
        return t.reshape(Bn, n_chunks, REC_CHUNK, H, t.shape[-1]).transpose(1, 0, 3, 2, 4)

    tril = jnp.tril(jnp.ones((REC_CHUNK, REC_CHUNK), dtype=bool))[..., None]

    def step(state, inp):
        qc, kc, vc, gc = inp
        b = jnp.cumsum(gc, axis=2)
        b_last = b[:, :, -1:, :]
        diff = b[:, :, :, None, :] - b[:, :, None, :, :]
        decay = jnp.where(tril, jnp.exp(jnp.where(tril, diff, 0.0)), 0.0)
        scores = jnp.einsum('bhid,bhijd,bhjd->bhij', qc, decay, kc)
        o = (jnp.einsum('bhij,bhjv->bhiv', scores, vc)
             + jnp.einsum('bhid,bhdv->bhiv', qc * jnp.exp(b), state))
        state = (jnp.exp(b_last)[:, :, 0, :, None] * state
                 + jnp.einsum('bhcd,bhcv->bhdv', kc * jnp.exp(b_last - b), vc))
        return state, o

    state0 = jnp.zeros((Bn, H, Dk, Dv), jnp.float32)
    _, o = lax.scan(step, state0, (chunks(q), chunks(k), chunks(v), chunks(logf)))
    return o.transpose(1, 0, 3, 2, 4).reshape(Bn, S, H, Dv)


def hgrn2_branch(q_in, i_in, zf_fwd, zf_bwd, z_out, lb, g_out):
    Bn, S, _ = q_in.shape

    def heads(t):
        return t.astype(jnp.float32).reshape(Bn, S, REC_HEADS, REC_HEAD_DIM)

    q = heads(jax.nn.silu(q_in.astype(jnp.float32)))
    v = heads(i_in)

    def gates(z, lbd):
        f = lbd + (1.0 - lbd) * jax.nn.sigmoid(z.astype(jnp.float32))
        return heads(1.0 - f), heads(jnp.log(f))

    k_f, lf_f = gates(zf_fwd, lb[0])
    k_b, lf_b = gates(zf_bwd, lb[1])

    def flip(t):
        return jnp.flip(t, axis=1)

    o = gla_scan(q, k_f, v, lf_f) + flip(gla_scan(flip(q), flip(k_b), flip(v), flip(lf_b)))
    o = o * lax.rsqrt(jnp.mean(o * o, axis=-1, keepdims=True) + RMS_EPS)
    o = o.reshape(Bn, S, D_REC) * g_out.astype(jnp.float32) * jax.nn.sigmoid(z_out.astype(jnp.float32))
    return o.astype(q_in.dtype)


def t5_bucket(rel):
    half_buckets = NUM_BUCKETS // 2
    ret = jnp.where(rel > 0, half_buckets, 0)
    n = jnp.abs(rel)
    max_exact = half_buckets // 2
    nf = jnp.maximum(n, 1).astype(jnp.float32)
    large = max_exact + (jnp.log(nf / max_exact) / math.log(MAX_DISTANCE / max_exact)
                         * (half_buckets - max_exact)).astype(jnp.int32)
    large = jnp.minimum(large, half_buckets - 1)
    return ret + jnp.where(n < max_exact, n, large)


def dilated_group(q, k, v, bias_table, window, dil):
    Bn, S, H, Dh = q.shape
    half = window // (2 * dil)
    L = S // dil
    nb = -(-L // ATT_BLOCK)
    Lp = nb * ATT_BLOCK

    def to_sub(t):
        return t.reshape(Bn, L, dil, H, Dh).transpose(0, 2, 3, 1, 4)

    def pad(t, lo, hi):
        return jnp.pad(t, ((0, 0), (0, 0), (0, 0), (lo, hi), (0, 0)))

    qb = pad(to_sub(q), 0, Lp - L).reshape(Bn, dil, H, nb, ATT_BLOCK, Dh)

    def band(t):
        tp = pad(to_sub(t), ATT_BLOCK, Lp - L + ATT_BLOCK).reshape(Bn, dil, H, nb + 2, ATT_BLOCK, Dh)
        return jnp.concatenate([tp[:, :, :, :-2], tp[:, :, :, 1:-1], tp[:, :, :, 2:]], axis=4)

    kb, vb = band(k), band(v)
    s = jnp.einsum('bdhnqe,bdhnke->bdhnqk', qb, kb,
                   preferred_element_type=jnp.float32) * (Dh ** -0.5)
    q_off = jnp.arange(ATT_BLOCK)[:, None]
    rel = jnp.arange(3 * ATT_BLOCK)[None, :] - ATT_BLOCK - q_off
    bias = bias_table[t5_bucket(rel * dil)].transpose(2, 0, 1).astype(jnp.float32)
    k_idx = (jnp.arange(nb) * ATT_BLOCK)[:, None, None] + q_off[None] + rel[None]
    valid = (jnp.abs(rel) <= half)[None] & (k_idx >= 0) & (k_idx < L)
    s = jnp.where(valid, s + bias[:, None], NEG_INF)
    m = jnp.max(s, axis=-1)
    p = jnp.exp(s - m[..., None])
    l = jnp.sum(p, axis=-1)
    num = jnp.einsum('bdhnqk,bdhnke->bdhnqe', p, vb.astype(jnp.float32))

    def from_sub(t):
        t = t.reshape((Bn, dil, H, Lp) + t.shape[5:])[:, :, :, :L]
        t = jnp.moveaxis(t, 3, 1)
        return t.reshape((Bn, S, H) + t.shape[4:])

    return from_sub(num), from_sub(m), from_sub(l)


def dilated_attention(q_in, k_in, v_in, rel_bias):
    Bn, S, _ = q_in.shape

    def heads(t):
        return t.reshape(Bn, S, ATT_HEADS, ATT_HEAD_DIM)

    q, k, v = heads(q_in), heads(k_in), heads(v_in)
    nums, ms, ls = [], [], []
    for g, (window, dil) in enumerate(DIL_GROUPS):
        hs = slice(g * ATT_HEADS_PER_GROUP, (g + 1) * ATT_HEADS_PER_GROUP)
        num, m, l = dilated_group(q[:, :, hs], k[:, :, hs], v[:, :, hs], rel_bias[:, hs], window, dil)
        nums.append(num)
        ms.append(m)
        ls.append(l)
    m_all = jnp.stack(ms)
    w = jnp.exp(m_all - jnp.max(m_all, axis=0))
    num = jnp.sum(w[..., None] * jnp.stack(nums), axis=0)
    den = jnp.sum(w * jnp.stack(ls), axis=0)
    out = num / den[..., None]
    return out.reshape(Bn, S, D_ATT_OUT).astype(q_in.dtype)


def hybrid_mixer(h, w_in, lb, g_rec_out, rel_bias, w_branch_rec, w_branch_att, w_o):
    splits = np.cumsum(IN_WIDTHS)[:-1].tolist()
    proj = h @ w_in
    q_r, i_r, zf_f, zf_b, z_o, q_a, k_a, v_a, zg_rec, zg_att = jnp.split(proj, splits, axis=-1)
    y_rec = hgrn2_branch(q_r, i_r, zf_f, zf_b, z_o, lb, g_rec_out) @ w_branch_rec
    y_att = dilated_attention(q_a, k_a, v_a, rel_bias) @ w_branch_att
    merged = jax.nn.sigmoid(zg_rec) * y_rec + jax.nn.sigmoid(zg_att) * y_att
    return merged @ w_o


def moe_ffn(h, w_router, b_router, w_gate_up, b_gate_up, w_down, b_down):
    Bn, S, D = h.shape
    T = Bn * S
    hf = h.reshape(T, D)
    logits = (hf @ w_router + b_router).astype(jnp.float32)
    top_val, top_idx = lax.top_k(logits, TOP_K)
    top_w = jax.nn.softmax(top_val, axis=-1)
    e_flat = top_idx.reshape(-1)
    tok_flat = jnp.repeat(jnp.arange(T, dtype=jnp.int32), TOP_K)
    w_flat = top_w.reshape(-1)
    order = jnp.argsort(e_flat)
    e_sorted = e_flat[order]
    counts = jnp.bincount(e_flat, length=N_EXPERTS)
    starts = jnp.cumsum(counts) - counts
    padded = (counts + EXPERT_BLOCK - 1) // EXPERT_BLOCK * EXPERT_BLOCK
    pends = jnp.cumsum(padded)
    pstarts = pends - padded
    dest = pstarts[e_sorted] + (jnp.arange(T * TOP_K) - starts[e_sorted])
    P = T * TOP_K + N_EXPERTS * EXPERT_BLOCK
    n_blocks = P // EXPERT_BLOCK
    buf_tok = jnp.zeros((P,), jnp.int32).at[dest].set(tok_flat[order])
    buf_w = jnp.zeros((P,), jnp.float32).at[dest].set(w_flat[order])
    blk_e = jnp.minimum(jnp.searchsorted(pends, jnp.arange(n_blocks) * EXPERT_BLOCK, side='right'),
                        N_EXPERTS - 1)
    xb = hf[buf_tok].reshape(n_blocks, EXPERT_BLOCK, D)

    def expert_block(args):
        xblk, e = args
        gu = xblk @ w_gate_up[e] + b_gate_up[e]
        gate, up = gu[:, :D_FF], gu[:, D_FF:]
        gate = jnp.minimum(gate, SWIGLU_LIMIT)
        up = jnp.clip(up, -SWIGLU_LIMIT, SWIGLU_LIMIT)
        act = gate * jax.nn.sigmoid(SWIGLU_ALPHA * gate) * (up + 1)
        return act @ w_down[e] + b_down[e]

    yb = lax.map(expert_block, (xb, blk_e)).reshape(P, D)
    y = jax.ops.segment_sum(yb.astype(jnp.float32) * buf_w[:, None], buf_tok, num_segments=T)
    return y.astype(h.dtype).reshape(Bn, S, D)


def setup_inputs(seed: int = 0) -> dict:
    key = jax.random.key(seed)
    ks = jax.random.split(key, 21)

    def nrm(k, shape, scale):
        return jax.random.normal(k, shape, jnp.float32) * scale

    def gain(k, shape):
        return 1.0 + 0.1 * jax.random.normal(k, shape, jnp.float32)

    L = DEPTH
    return {
        'x': nrm(ks[0], (BATCH, SEQ, D_MODEL), 1.0),
        'c': nrm(ks[1], (BATCH, D_MODEL), 1.0),
        'w_ada': nrm(ks[2], (L, D_MODEL, 6 * D_MODEL), D_MODEL ** -0.5),
        'b_ada': nrm(ks[3], (L, 6 * D_MODEL), 0.02),
        'g_mix_pre': gain(ks[4], (L, D_MODEL)),
        'g_mix_post': gain(ks[5], (L, D_MODEL)),
        'g_ffn_pre': gain(ks[6], (L, D_MODEL)),
        'g_ffn_post': gain(ks[7], (L, D_MODEL)),
        'w_in': nrm(ks[8], (L, D_MODEL, D_IN), D_MODEL ** -0.5),
        'g_rec_out': gain(ks[9], (L, D_REC)),
        'w_branch_rec': nrm(ks[10], (L, D_REC, D_MODEL), D_REC ** -0.5),
        'w_branch_att': nrm(ks[11], (L, D_ATT_OUT, D_MODEL), D_ATT_OUT ** -0.5),
        'w_o': nrm(ks[12], (L, D_MODEL, D_MODEL), D_MODEL ** -0.5),
        'w_router': nrm(ks[13], (L, D_MODEL, N_EXPERTS), D_MODEL ** -0.5),
        'b_router': nrm(ks[14], (L, N_EXPERTS), 0.01),
        'w_gate_up': nrm(ks[15], (L, N_EXPERTS, D_MODEL, 2 * D_FF), D_MODEL ** -0.5),
        'b_gate_up': nrm(ks[16], (L, N_EXPERTS, 2 * D_FF), 0.02),
        'w_down': nrm(ks[17], (L, N_EXPERTS, D_FF, D_MODEL), D_FF ** -0.5),
        'b_down': nrm(ks[18], (L, N_EXPERTS, D_MODEL), 0.02),
        'rec_lb_table': nrm(ks[19], (2, DEPTH + 1, D_REC), 1.0),
        'rel_bias': nrm(ks[20], (NUM_BUCKETS, ATT_HEADS), 0.5),
    }


def reference(x, c, w_ada, b_ada, g_mix_pre, g_mix_post, g_ffn_pre, g_ffn_post, w_in, g_rec_out,
              w_branch_rec, w_branch_att, w_o, w_router, b_router, w_gate_up, b_gate_up, w_down,
              b_down, rec_lb_table, rel_bias):
    lb_all = jnp.cumsum(jax.nn.softmax(rec_lb_table.astype(jnp.float32), axis=1), axis=1)
    cond = jax.nn.silu(c)
    for layer in range(DEPTH):
        ada = (cond @ w_ada[layer] + b_ada[layer])[:, None, :]
        sh_m, sc_m, gt_m, sh_f, sc_f, gt_f = jnp.split(ada, 6, axis=-1)
        h = modulate(rms_norm(x, g_mix_pre[layer]), sh_m, sc_m)
        y = hybrid_mixer(h, w_in[layer], lb_all[:, layer], g_rec_out[layer], rel_bias,
                         w_branch_rec[layer], w_branch_att[layer], w_o[layer])
        x = x + gt_m * rms_norm(y, g_mix_post[layer])
        h = modulate(rms_norm(x, g_ffn_pre[layer]), sh_f, sc_f)
        y = moe_ffn(h, w_router[layer], b_router[layer], w_gate_up[layer], b_gate_up[layer],
                    w_down[layer], b_down[layer])
        x = x + gt_f * rms_norm(y, g_ffn_post[layer])
    return x
```

```python
import functools
import math

import numpy as np
import jax
import jax.numpy as jnp
from jax import lax
from jax.experimental import pallas as pl
from jax.experimental.pallas import tpu as pltpu

F32 = jnp.float32
BF16 = jnp.bfloat16

LANE = 128
SUBLANE = 8
SLAB_W = 256

REC_HEAD_DIM = 128
REC_CHUNK = 64
ATT_HEAD_DIM = 128
ATT_HEADS_PER_GROUP = 4
ATT_BLOCK = 64
DIL_GROUPS = ((128, 1), (512, 4), (2048, 16))
NUM_BUCKETS = 32
MAX_DISTANCE = 1024
N_EXPERTS = 32
TOP_K = 4
SWIGLU_LIMIT = 7.0
SWIGLU_ALPHA = 1.702
RMS_EPS = 1e-6
NEG_INF = -1e30

N_LEVELS = 6
BM = 256

_NT = (((1,), (1,)), ((), ()))
_TN = (((0,), (0,)), ((), ()))


def _cparams(sem, vmem_mb):
    return pltpu.CompilerParams(dimension_semantics=sem, vmem_limit_bytes=vmem_mb * 1024 * 1024)


def _rms(x):
    return x * lax.rsqrt(jnp.mean(x * x, axis=-1, keepdims=True) + RMS_EPS)


def _ada_body(c_ref, w_ref, b_ref, o_ref):
    c = c_ref[...]
    cond = (c * jax.nn.sigmoid(c)).astype(BF16)
    o_ref[...] = jnp.dot(cond, w_ref[...].astype(BF16), preferred_element_type=F32) + b_ref[...]


def _ada(c, w, b):
    bsz, d = c.shape
    n = w.shape[1]
    tn = 1024
    cp = jnp.zeros((SUBLANE, d), F32).at[:bsz].set(c)
    out = pl.pallas_call(
        _ada_body,
        grid=(n // tn,),
        in_specs=[pl.BlockSpec((SUBLANE, d), lambda j: (0, 0)),
                  pl.BlockSpec((d, tn), lambda j: (0, j)),
                  pl.BlockSpec((1, tn), lambda j: (0, j))],
        out_specs=pl.BlockSpec((SUBLANE, tn), lambda j: (0, j)),
        out_shape=jax.ShapeDtypeStruct((SUBLANE, n), F32),
        compiler_params=_cparams(("arbitrary",), 40),
        name="ada",
    )(cp, w, b.reshape(1, n))
    return out[:bsz]


def _inproj_body(x_ref, g_ref, ada_ref, w_ref, o_ref, h_ref):
    @pl.when(pl.program_id(1) == 0)
    def _():
        y = _rms(x_ref[...]) * g_ref[...]
        sh = ada_ref[0, 0:1, :]
        sc = ada_ref[0, 1:2, :]
        h_ref[...] = (y * (1.0 + sc) + sh).astype(BF16)

    o_ref[...] = jnp.dot(h_ref[...], w_ref[...], preferred_element_type=F32)


def _inproj(x2, g, ada3, w_bf, seq):
    t, d = x2.shape
    n = w_bf.shape[1]
    tm, tn = 1024, 768
    per_b = seq // tm
    return pl.pallas_call(
        _inproj_body,
        grid=(t // tm, n // tn),
        in_specs=[pl.BlockSpec((tm, d), lambda i, j: (i, 0)),
                  pl.BlockSpec((1, d), lambda i, j: (0, 0)),
                  pl.BlockSpec((1, 6, d), lambda i, j: (i // per_b, 0, 0)),
                  pl.BlockSpec((d, tn), lambda i, j: (0, j))],
        out_specs=pl.BlockSpec((tm, tn), lambda i, j: (i, j)),
        out_shape=jax.ShapeDtypeStruct((t, n), F32),
        scratch_shapes=[pltpu.VMEM((tm, d), BF16)],
        compiler_params=_cparams(("arbitrary", "arbitrary"), 48),
        name="inproj",
    )(x2, g.reshape(1, d), ada3, w_bf)


def _hgrn_consts():
    c = REC_CHUNK
    r = np.arange(c)[:, None]
    m = np.arange(c)[None, :]
    wf = np.zeros((8 * c, c), np.float32)
    wb = np.zeros((8 * c, c), np.float32)
    mf = np.zeros((N_LEVELS + 1, c, c), np.float32)
    for lvl in range(N_LEVELS):
        s = 32 >> lvl
        m0 = (r // (2 * s)) * (2 * s) + s
        up = r >= m0
        wf[lvl * c:(lvl + 1) * c] = np.where(up, (m >= m0) & (m <= r), (m > r) & (m <= m0 - 1))
        wb[lvl * c:(lvl + 1) * c] = np.where(up, (m >= m0) & (m <= r - 1), (m >= r) & (m <= m0 - 1))
        i = np.arange(c)[:, None]
        j = np.arange(c)[None, :]
        mf[lvl] = (i // (2 * s) == j // (2 * s)) & (i % (2 * s) >= s) & (j % (2 * s) < s)
    mf[N_LEVELS] = np.eye(c)
    wf[6 * c:7 * c] = m <= r
    wf[7 * c:8 * c] = m > r
    wb[6 * c:7 * c] = m >= r
    wb[7 * c:8 * c] = m < r
    mb = np.transpose(mf, (0, 2, 1)).copy()
    return (jnp.asarray(wf, BF16), jnp.asarray(wb, BF16), jnp.asarray(mf, F32), jnp.asarray(mb, F32))


def _split3(g):
    hi = g.astype(BF16)
    r1 = g - hi.astype(F32)
    mid = r1.astype(BF16)
    lo = (r1 - mid.astype(F32)).astype(BF16)
    return jnp.concatenate([hi, mid, lo], axis=1)


def _hgrn_a_body(q_ref, i_ref, zf_ref, zb_ref, lb_ref, wf_ref, wb_ref, mf_ref, mb_ref,
                 oi_ref, qtf_ref, qtb_ref, utf_ref, utb_ref, df_ref, db_ref, *, cpb):
    c = REC_CHUNK
    dirs = ((zf_ref, wf_ref, mf_ref, qtf_ref, utf_ref, df_ref, 0, c - 1),
            (zb_ref, wb_ref, mb_ref, qtb_ref, utb_ref, db_ref, 1, 0))

    def chunk(ci, carry):
        rows = pl.ds(pl.multiple_of(ci * c, c), c)
        zq = q_ref[rows, :]
        q = zq * jax.nn.sigmoid(zq)
        vb = i_ref[rows, :].astype(BF16)
        qb16 = q.astype(BF16)
        a = jnp.zeros((c, c), F32)
        for z_ref, w_ref, m_ref, qt_ref, ut_ref, d_ref, di, drow in dirs:
            lb = lb_ref[di:di + 1, :]
            f = lb + (1.0 - lb) * jax.nn.sigmoid(z_ref[rows, :])
            k = 1.0 - f
            g = jnp.log(f)
            r3 = jnp.dot(w_ref[...], _split3(g), preferred_element_type=F32)
            e = jnp.exp(r3[:, 0:LANE] + r3[:, LANE:2 * LANE] + r3[:, 2 * LANE:3 * LANE])
            for lvl in range(N_LEVELS):
                el = e[lvl * c:(lvl + 1) * c]
                p = lax.dot_general((q * el).astype(BF16), (k * el).astype(BF16), _NT,
                                    preferred_element_type=F32)
                a = a + p * m_ref[lvl]
            p = lax.dot_general(qb16, k.astype(BF16), _NT, preferred_element_type=F32)
            a = a + p * m_ref[N_LEVELS]
            qt_ref[rows, :] = (q * e[6 * c:7 * c]).astype(BF16)
            kt = (k * e[7 * c:8 * c]).astype(BF16)
            ut_ref[ci] = lax.dot_general(vb, kt, _TN, preferred_element_type=F32)
            d_ref[pl.ds(ci, 1), :] = e[6 * c + drow:6 * c + drow + 1]
        oi_ref[rows, :] = jnp.dot(a.astype(BF16), vb, preferred_element_type=F32)
        return carry

    lax.fori_loop(0, cpb, chunk, 0)


def _hgrn_a(proj, lb, col, t, d_rec):
    heads = d_rec // REC_HEAD_DIM
    tq = 512
    cpb = tq // REC_CHUNK
    nchunks = t // REC_CHUNK
    wf, wb, mf, mb = _hgrn_consts()
    hd = REC_HEAD_DIM

    def colspec(off):
        return pl.BlockSpec((tq, hd), lambda i, h: (i, off + h))

    full2 = lambda i, h: (0, 0)
    full3 = lambda i, h: (0, 0, 0)
    row_spec = pl.BlockSpec((tq, hd), lambda i, h: (i, h))
    u_spec = pl.BlockSpec((cpb, hd, hd), lambda i, h: (i, 0, h))
    d_spec = pl.BlockSpec((cpb, hd), lambda i, h: (i, h))
    return pl.pallas_call(
        functools.partial(_hgrn_a_body, cpb=cpb),
        grid=(t // tq, heads),
        in_specs=[colspec(col["q_r"]), colspec(col["i_r"]), colspec(col["zf_f"]), colspec(col["zf_b"]),
                  pl.BlockSpec((2, hd), lambda i, h: (0, h)),
                  pl.BlockSpec(wf.shape, full2), pl.BlockSpec(wb.shape, full2),
                  pl.BlockSpec(mf.shape, full3), pl.BlockSpec(mb.shape, full3)],
        out_specs=[row_spec, row_spec, row_spec, u_spec, u_spec, d_spec, d_spec],
        out_shape=[jax.ShapeDtypeStruct((t, d_rec), F32),
                   jax.ShapeDtypeStruct((t, d_rec), BF16),
                   jax.ShapeDtypeStruct((t, d_rec), BF16),
                   jax.ShapeDtypeStruct((nchunks, hd, d_rec), F32),
                   jax.ShapeDtypeStruct((nchunks, hd, d_rec), F32),
                   jax.ShapeDtypeStruct((nchunks, d_rec), F32),
                   jax.ShapeDtypeStruct((nchunks, d_rec), F32)],
        compiler_params=_cparams(("arbitrary", "arbitrary"), 32),
        name="hgrn_a",
    )(proj, proj, proj, proj, lb, wf, wb, mf, mb)


def _hgrn_c_body(oi_ref, qtf_ref, qtb_ref, utf_ref, utb_ref, df_ref, db_ref, z_ref, g_ref,
                 out_ref, acc_ref, *, nchunks):
    c = REC_CHUNK
    hd = REC_HEAD_DIM

    def fwd(n, st):
        rows = pl.ds(pl.multiple_of(n * c, c), c)
        o = lax.dot_general(qtf_ref[rows, :], st.astype(BF16), _NT, preferred_element_type=F32)
        acc_ref[rows, :] = oi_ref[rows, :] + o
        return df_ref[pl.ds(n, 1), :] * st + utf_ref[n]

    lax.fori_loop(0, nchunks, fwd, jnp.zeros((hd, hd), F32), unroll=4)

    def bwd(i, st):
        n = nchunks - 1 - i
        rows = pl.ds(pl.multiple_of(n * c, c), c)
        o = lax.dot_general(qtb_ref[rows, :], st.astype(BF16), _NT, preferred_element_type=F32)
        acc_ref[rows, :] = acc_ref[rows, :] + o
        return db_ref[pl.ds(n, 1), :] * st + utb_ref[n]

    lax.fori_loop(0, nchunks, bwd, jnp.zeros((hd, hd), F32), unroll=4)

    o = _rms(acc_ref[...])
    out_ref[...] = (o * g_ref[...] * jax.nn.sigmoid(z_ref[...])).astype(BF16)


def _hgrn_c(oi, qtf, qtb, utf, utb, df, db, proj, g_out, col, bsz, seq, d_rec):
    heads = d_rec // REC_HEAD_DIM
    hd = REC_HEAD_DIM
    nchunks = seq // REC_CHUNK
    row_spec = pl.BlockSpec((seq, hd), lambda b, h: (b, h))
    u_spec = pl.BlockSpec((nchunks, hd, hd), lambda b, h: (b, 0, h))
    d_spec = pl.BlockSpec((nchunks, hd), lambda b, h: (b, h))
    zo = col["z_o"]
    return pl.pallas_call(
        functools.partial(_hgrn_c_body, nchunks=nchunks),
        grid=(bsz, heads),
        in_specs=[row_spec, row_spec, row_spec, u_spec, u_spec, d_spec, d_spec,
                  pl.BlockSpec((seq, hd), lambda b, h: (b, zo + h)),
                  pl.BlockSpec((1, hd), lambda b, h: (0, h))],
        out_specs=row_spec,
        out_shape=jax.ShapeDtypeStruct((bsz * seq, d_rec), BF16),
        scratch_shapes=[pltpu.VMEM((seq, hd), F32)],
        compiler_params=_cparams(("arbitrary", "arbitrary"), 48),
        name="hgrn_c",
    )(oi, qtf, qtb, utf, utb, df, db, proj, g_out.reshape(1, d_rec))


def _t5_bucket(rel):
    half_buckets = NUM_BUCKETS // 2
    ret = jnp.where(rel > 0, half_buckets, 0)
    n = jnp.abs(rel)
    max_exact = half_buckets // 2
    nf = jnp.maximum(n, 1).astype(F32)
    large = max_exact + (jnp.log(nf / max_exact) / math.log(MAX_DISTANCE / max_exact)
                         * (half_buckets - max_exact)).astype(jnp.int32)
    large = jnp.minimum(large, half_buckets - 1)
    return ret + jnp.where(n < max_exact, n, large)


def _band_bias(rel_bias_g, window, dil):
    half = window // (2 * dil)
    q_off = jnp.arange(ATT_BLOCK)[:, None]
    rel = jnp.arange(3 * ATT_BLOCK)[None, :] - ATT_BLOCK - q_off
    bias = rel_bias_g[_t5_bucket(rel * dil)].transpose(2, 0, 1).astype(F32)
    return jnp.where((jnp.abs(rel) <= half)[None], bias, NEG_INF)


def _attn_body(q_ref, kp_ref, k_ref, kn_ref, vp_ref, v_ref, vn_ref, bias_ref,
               num_ref, st_ref, kc_ref, vc_ref, *, dil, tq, sub_len):
    blk = ATT_BLOCK
    nqb = tq // blk
    n = pl.program_id(1)
    h = pl.program_id(2)
    scale = ATT_HEAD_DIM ** -0.5

    def sds(start, size):
        if dil == 1:
            return pl.ds(start, size)
        return pl.ds(start, size, stride=dil)

    @pl.when(h == 0)
    def _():
        st_ref[...] = jnp.zeros_like(st_ref)

    def one_class(r, carry):
        kc_ref[0:blk, :] = kp_ref[sds(r, blk), :].astype(BF16)
        kc_ref[blk:blk + tq, :] = k_ref[sds(r, tq), :].astype(BF16)
        kc_ref[blk + tq:2 * blk + tq, :] = kn_ref[sds(r, blk), :].astype(BF16)
        vc_ref[0:blk, :] = vp_ref[sds(r, blk), :].astype(BF16)
        vc_ref[blk:blk + tq, :] = v_ref[sds(r, tq), :].astype(BF16)
        vc_ref[blk + tq:2 * blk + tq, :] = vn_ref[sds(r, blk), :].astype(BF16)

        def one_qblock(qb, carry2):
            q0 = pl.multiple_of(qb * blk, blk)
            rows = sds(r + dil * q0, blk)
            kpos = n * tq + q0 - blk + lax.broadcasted_iota(jnp.int32, (1, 3 * blk), 1)
            valid = (kpos >= 0) & (kpos < sub_len)
            lane = lax.broadcasted_iota(jnp.int32, (blk, LANE), 1)
            q = q_ref[rows, :].astype(BF16)
            kw = kc_ref[pl.ds(q0, 3 * blk), :]
            vw = vc_ref[pl.ds(q0, 3 * blk), :]
            s = lax.dot_general(q, kw, _NT, preferred_element_type=F32) * scale
            s = jnp.where(valid, s + bias_ref[0], NEG_INF)
            m = jnp.max(s, axis=-1, keepdims=True)
            p = jnp.exp(s - m)
            l = jnp.sum(p, axis=-1, keepdims=True)
            num_ref[rows, :] = jnp.dot(p.astype(BF16), vw, preferred_element_type=F32)
            st = st_ref[rows, :]
            st = jnp.where(lane == h, m, st)
            st_ref[rows, :] = jnp.where(lane == ATT_HEADS_PER_GROUP + h, l, st)
            return carry2

        lax.fori_loop(0, nqb, one_qblock, 0)
        return carry

    lax.fori_loop(0, dil, one_class, 0)


def _attn_group(proj, bias, col, g, dil, bsz, seq):
    tile = 1024
    tq = tile // dil
    halo = ATT_BLOCK * dil
    sub_len = seq // dil
    nh = ATT_HEADS_PER_GROUP
    hd = ATT_HEAD_DIM
    qc = col["q_a"] + g * nh
    kc = col["k_a"] + g * nh
    vc = col["v_a"] + g * nh
    tiles_b = seq // tile
    halos_b = seq // halo
    hpt = tile // halo

    own = lambda c: pl.BlockSpec((tile, hd), lambda b, n, h: (b * tiles_b + n, c + h))
    prev = lambda c: pl.BlockSpec(
        (halo, hd), lambda b, n, h: (b * halos_b + jnp.maximum(n * hpt - 1, 0), c + h))
    nxt = lambda c: pl.BlockSpec(
        (halo, hd), lambda b, n, h: (b * halos_b + jnp.minimum((n + 1) * hpt, halos_b - 1), c + h))
    t = bsz * seq
    return pl.pallas_call(
        functools.partial(_attn_body, dil=dil, tq=tq, sub_len=sub_len),
        grid=(bsz, tiles_b, nh),
        in_specs=[own(qc), prev(kc), own(kc), nxt(kc), prev(vc), own(vc), nxt(vc),
                  pl.BlockSpec((1,) + bias.shape[1:], lambda b, n, h: (h, 0, 0))],
        out_specs=[pl.BlockSpec((tile, hd), lambda b, n, h: (b * tiles_b + n, h)),
                   pl.BlockSpec((tile, LANE), lambda b, n, h: (b * tiles_b + n, 0))],
        out_shape=[jax.ShapeDtypeStruct((t, nh * hd), F32), jax.ShapeDtypeStruct((t, LANE), F32)],
        scratch_shapes=[pltpu.VMEM((tq + 2 * ATT_BLOCK, hd), BF16),
                        pltpu.VMEM((tq + 2 * ATT_BLOCK, hd), BF16)],
        compiler_params=_cparams(("arbitrary", "arbitrary", "arbitrary"), 32),
        name=f"attn_d{dil}",
    )(proj, proj, proj, proj, proj, proj, proj, bias)


def _merge_body(rec_ref, n0_ref, n1_ref, n2_ref, s0_ref, s1_ref, s2_ref, zgr_ref, zga_ref, x_ref,
                ada_ref, gpost_ref, gpre_ref, wbr_ref, wba_ref, wo_ref, wr_ref, br_ref,
                x1_ref, h2_ref, lg_ref):
    nh = ATT_HEADS_PER_GROUP
    hd = ATT_HEAD_DIM
    stats = [s[...] for s in (s0_ref, s1_ref, s2_ref)]
    ms = [s[:, 0:nh] for s in stats]
    ls = [s[:, nh:2 * nh] for s in stats]
    m_all = jnp.maximum(jnp.maximum(ms[0], ms[1]), ms[2])
    ws = [jnp.exp(m - m_all) for m in ms]
    den = ws[0] * ls[0] + ws[1] * ls[1] + ws[2] * ls[2]
    heads = []
    for h in range(nh):
        cols = slice(h * hd, (h + 1) * hd)
        num = (ws[0][:, h:h + 1] * n0_ref[:, cols] + ws[1][:, h:h + 1] * n1_ref[:, cols]
               + ws[2][:, h:h + 1] * n2_ref[:, cols])
        heads.append((num / den[:, h:h + 1]).astype(BF16))
    att = jnp.concatenate(heads, axis=1)
    y_rec = jnp.dot(rec_ref[...], wbr_ref[...], preferred_element_type=F32)
    y_att = jnp.dot(att, wba_ref[...], preferred_element_type=F32)
    merged = jax.nn.sigmoid(zgr_ref[...]) * y_rec + jax.nn.sigmoid(zga_ref[...]) * y_att
    y = jnp.dot(merged.astype(BF16), wo_ref[...], preferred_element_type=F32)
    gt_m = ada_ref[0, 2:3, :]
    sh_f = ada_ref[0, 3:4, :]
    sc_f = ada_ref[0, 4:5, :]
    x1 = x_ref[...] + gt_m * (_rms(y) * gpost_ref[...])
    x1_ref[...] = x1
    h2 = _rms(x1) * gpre_ref[...] * (1.0 + sc_f) + sh_f
    h2_ref[...] = h2
    lg_ref[...] = jnp.dot(h2, wr_ref[...], preferred_element_type=F32,
                          precision=lax.Precision.HIGHEST) + br_ref[...]


def _merge(rec_o, nums, stats, proj, x2, ada3, g_post, g_pre, wbr, wba, wo, w_router, b_router, col, seq):
    t, d = x2.shape
    tm = 256
    per_b = seq // tm
    d_rec = rec_o.shape[1]
    w_att = nums[0].shape[1]
    ne = w_router.shape[1]
    dl = d // LANE
    row = lambda w: pl.BlockSpec((tm, w), lambda i: (i, 0))
    const = lambda shape: pl.BlockSpec(shape, lambda i: (0,) * len(shape), pipeline_mode=pl.Buffered(1))
    zgr = col["zg_rec"] // dl
    zga = col["zg_att"] // dl
    return pl.pallas_call(
        _merge_body,
        grid=(t // tm,),
        in_specs=[row(d_rec), row(w_att), row(w_att), row(w_att), row(LANE), row(LANE), row(LANE),
                  pl.BlockSpec((tm, d), lambda i: (i, zgr)),
                  pl.BlockSpec((tm, d), lambda i: (i, zga)),
                  row(d),
                  pl.BlockSpec((1, 6, d), lambda i: (i // per_b, 0, 0)),
                  const((1, d)), const((1, d)),
                  const(wbr.shape), const(wba.shape), const(wo.shape), const(w_router.shape),
                  const((1, ne))],
        out_specs=[row(d), row(d), pl.BlockSpec((tm, ne), lambda i: (i, 0))],
        out_shape=[jax.ShapeDtypeStruct((t, d), F32), jax.ShapeDtypeStruct((t, d), F32),
                   jax.ShapeDtypeStruct((t, ne), F32)],
        compiler_params=_cparams(("arbitrary",), 56),
        name="merge",
    )(rec_o, nums[0], nums[1], nums[2], stats[0], stats[1], stats[2], proj, proj, x2, ada3,
      g_post.reshape(1, d), g_pre.reshape(1, d), wbr, wba, wo, w_router, b_router.reshape(1, ne))


def _route_body(lg_ref, tri_ref, rt_ref, cnt_ref, carry_ref):
    i = pl.program_id(0)
    tr, ne = lg_ref.shape

    @pl.when(i == 0)
    def _():
        carry_ref[...] = jnp.zeros_like(carry_ref)

    l = lg_ref[...]
    lane = lax.broadcasted_iota(jnp.int32, (tr, ne), 1).astype(F32)
    vals, sels, idxs = [], [], []
    for _ in range(TOP_K):
        m = jnp.max(l, axis=-1, keepdims=True)
        idx = jnp.min(jnp.where(l == m, lane, float(ne)), axis=-1, keepdims=True)
        sel = lane == idx
        vals.append(m)
        idxs.append(idx)
        sels.append(sel)
        l = jnp.where(sel, -jnp.inf, l)
    es = [jnp.exp(v - vals[0]) for v in vals]
    tot = es[0] + es[1] + es[2] + es[3]
    chosen = (sels[0] | sels[1] | sels[2] | sels[3]).astype(F32)
    prefix = jnp.dot(tri_ref[...], chosen.astype(BF16), preferred_element_type=F32) + carry_ref[0:1, :]
    out_lane = lax.broadcasted_iota(jnp.int32, (tr, LANE), 1)
    rt = jnp.zeros((tr, LANE), F32)
    for k in range(TOP_K):
        rank = jnp.sum(jnp.where(sels[k], prefix, 0.0), axis=-1, keepdims=True)
        rt = jnp.where(out_lane == k, idxs[k], rt)
        rt = jnp.where(out_lane == TOP_K + k, es[k] / tot, rt)
        rt = jnp.where(out_lane == 2 * TOP_K + k, rank, rt)
    rt_ref[...] = rt
    new = carry_ref[0:1, :] + jnp.sum(chosen, axis=0, keepdims=True)
    carry_ref[...] = jnp.broadcast_to(new, carry_ref.shape)
    cnt_ref[...] = carry_ref[...]


def _route(logits):
    t, ne = logits.shape
    tr = 512
    tri = jnp.asarray(np.tril(np.ones((tr, tr), np.float32), -1), BF16)
    return pl.pallas_call(
        _route_body,
        grid=(t // tr,),
        in_specs=[pl.BlockSpec((tr, ne), lambda i: (i, 0)),
                  pl.BlockSpec((tr, tr), lambda i: (0, 0))],
        out_specs=[pl.BlockSpec((tr, LANE), lambda i: (i, 0)),
                   pl.BlockSpec((SUBLANE, ne), lambda i: (0, 0))],
        out_shape=[jax.ShapeDtypeStruct((t, LANE), F32), jax.ShapeDtypeStruct((SUBLANE, ne), F32)],
        scratch_shapes=[pltpu.VMEM((SUBLANE, ne), F32)],
        compiler_params=_cparams(("arbitrary",), 32),
        name="route",
    )(logits, tri)


def _slab_move_body(src_idx_ref, dst_idx_ref, src_ref, init_ref, out_ref, sem, *, chunk):
    del init_ref
    i = pl.program_id(0)

    def copy(j):
        s = src_idx_ref[0, 0, j]
        d = dst_idx_ref[0, 0, j]
        return pltpu.make_async_copy(src_ref.at[pl.ds(pl.multiple_of(s * SUBLANE, SUBLANE), SUBLANE), :],
                                     out_ref.at[pl.ds(pl.multiple_of(d * SUBLANE, SUBLANE), SUBLANE), :],
                                     sem)

    def start(j, c):
        copy(j).start()
        return c

    def wait(j, c):
        copy(j).wait()
        return c

    lax.fori_loop(0, chunk, start, 0)
    lax.fori_loop(0, chunk, wait, 0)


def _slab_move(src, src_idx, dst_idx, init):
    m = src_idx.shape[0]
    chunk = 512
    nch = m // chunk
    idx_spec = pl.BlockSpec((1, 1, chunk), lambda i: (i, 0, 0), memory_space=pltpu.SMEM)
    return pl.pallas_call(
        functools.partial(_slab_move_body, chunk=chunk),
        grid=(nch,),
        in_specs=[idx_spec, idx_spec,
                  pl.BlockSpec(memory_space=pl.ANY), pl.BlockSpec(memory_space=pl.ANY)],
        out_specs=pl.BlockSpec(memory_space=pl.ANY),
        out_shape=jax.ShapeDtypeStruct(init.shape, init.dtype),
        scratch_shapes=[pltpu.SemaphoreType.DMA(())],
        input_output_aliases={3: 0},
        compiler_params=pltpu.CompilerParams(dimension_semantics=("arbitrary",), has_side_effects=True),
        name="slab_move",
    )(src_idx.reshape(nch, 1, chunk), dst_idx.reshape(nch, 1, chunk), src, init)


def _moe_up_body(be_ref, nu_ref, x_ref, wg_ref, wu_ref, bg_ref, bu_ref, o_ref):
    b = pl.program_id(1)

    @pl.when(b < nu_ref[0])
    def _():
        x = x_ref[...].astype(BF16)
        gate = jnp.dot(x, wg_ref[0], preferred_element_type=F32) + bg_ref[0]
        up = jnp.dot(x, wu_ref[0], preferred_element_type=F32) + bu_ref[0]
        gate = jnp.minimum(gate, SWIGLU_LIMIT)
        up = jnp.clip(up, -SWIGLU_LIMIT, SWIGLU_LIMIT)
        o_ref[...] = (gate * jax.nn.sigmoid(SWIGLU_ALPHA * gate) * (up + 1.0)).astype(BF16)

    @pl.when(b >= nu_ref[0])
    def _():
        o_ref[...] = jnp.zeros_like(o_ref)


def _moe_up(xb, wgu_bf, bgu, blk_e, n_used):
    p, d = xb.shape
    ne, _, f2 = wgu_bf.shape
    f = f2 // 2
    tn = 1024
    nj = f // tn
    nb = p // BM
    grid_spec = pltpu.PrefetchScalarGridSpec(
        num_scalar_prefetch=2,
        grid=(nj, nb),
        in_specs=[pl.BlockSpec((BM, d), lambda j, b, be, nu: (b, 0)),
                  pl.BlockSpec((1, d, tn), lambda j, b, be, nu: (be[b], 0, j)),
                  pl.BlockSpec((1, d, tn), lambda j, b, be, nu: (be[b], 0, nj + j)),
                  pl.BlockSpec((1, 1, tn), lambda j, b, be, nu: (be[b], 0, j)),
                  pl.BlockSpec((1, 1, tn), lambda j, b, be, nu: (be[b], 0, nj + j))],
        out_specs=pl.BlockSpec((BM, tn), lambda j, b, be, nu: (b, j)),
    )
    return pl.pallas_call(
        _moe_up_body,
        grid_spec=grid_spec,
        out_shape=jax.ShapeDtypeStruct((p, f), BF16),
        compiler_params=_cparams(("arbitrary", "arbitrary"), 48),
        name="moe_up",
    )(blk_e, n_used, xb, wgu_bf, wgu_bf, bgu.reshape(ne, 1, f2), bgu.reshape(ne, 1, f2))


def _moe_down_body(be_ref, nu_ref, a_ref, w_ref, b_ref, o_ref):
    b = pl.program_id(0)

    @pl.when(b < nu_ref[0])
    def _():
        o_ref[...] = jnp.dot(a_ref[...], w_ref[0], preferred_element_type=F32) + b_ref[0]

    @pl.when(b >= nu_ref[0])
    def _():
        o_ref[...] = jnp.zeros_like(o_ref)


def _moe_down(act, wd_bf, bd, blk_e, n_used):
    p, f = act.shape
    ne, _, d = wd_bf.shape
    nb = p // BM
    grid_spec = pltpu.PrefetchScalarGridSpec(
        num_scalar_prefetch=2,
        grid=(nb,),
        in_specs=[pl.BlockSpec((BM, f), lambda b, be, nu: (b, 0)),
                  pl.BlockSpec((1, f, d), lambda b, be, nu: (be[b], 0, 0)),
                  pl.BlockSpec((1, 1, d), lambda b, be, nu: (be[b], 0, 0))],
        out_specs=pl.BlockSpec((BM, d), lambda b, be, nu: (b, 0)),
    )
    return pl.pallas_call(
        _moe_down_body,
        grid_spec=grid_spec,
        out_shape=jax.ShapeDtypeStruct((p, d), F32),
        compiler_params=_cparams(("arbitrary",), 48),
        name="moe_down",
    )(blk_e, n_used, act, wd_bf, bd.reshape(ne, 1, d))


def _final_body(yg_ref, rt_ref, x1_ref, ada_ref, g_ref, o_ref):
    d = x1_ref.shape[1]
    rt = rt_ref[...]
    y = rt[:, TOP_K:TOP_K + 1] * yg_ref[:, 0:d]
    for k in range(1, TOP_K):
        y = y + rt[:, TOP_K + k:TOP_K + k + 1] * yg_ref[:, k * d:(k + 1) * d]
    gt_f = ada_ref[0, 5:6, :]
    o_ref[...] = x1_ref[...] + gt_f * (_rms(y) * g_ref[...])


def _final(yg, rt, x1, ada3, g_post, seq):
    t, d = x1.shape
    tm = 256
    per_b = seq // tm
    return pl.pallas_call(
        _final_body,
        grid=(t // tm,),
        in_specs=[pl.BlockSpec((tm, TOP_K * d), lambda i: (i, 0)),
                  pl.BlockSpec((tm, LANE), lambda i: (i, 0)),
                  pl.BlockSpec((tm, d), lambda i: (i, 0)),
                  pl.BlockSpec((1, 6, d), lambda i: (i // per_b, 0, 0)),
                  pl.BlockSpec((1, d), lambda i: (0, 0))],
        out_specs=pl.BlockSpec((tm, d), lambda i: (i, 0)),
        out_shape=jax.ShapeDtypeStruct((t, d), F32),
        compiler_params=_cparams(("arbitrary",), 48),
        name="final",
    )(yg, rt, x1, ada3, g_post.reshape(1, d))


def _mixer_ffn_layer(x2, ada3, bsz, seq, g_mix_pre, g_mix_post, g_ffn_pre, g_ffn_post, w_in, lb, g_rec_out,
                     rel_bias, w_branch_rec, w_branch_att, w_o, w_router, b_router, w_gate_up, b_gate_up,
                     w_down, b_down):
    t, d = x2.shape
    d_rec = w_branch_rec.shape[0]
    w_att = w_branch_att.shape[0]
    d_att = 3 * w_att
    widths = dict(q_r=d_rec, i_r=d_rec, zf_f=d_rec, zf_b=d_rec, z_o=d_rec, q_a=d_att, k_a=d_att, v_a=d_att,
                  zg_rec=d, zg_att=d)
    ref_order = ("q_r", "i_r", "zf_f", "zf_b", "z_o", "q_a", "k_a", "v_a", "zg_rec", "zg_att")
    my_order = ("zg_rec", "zg_att", "q_r", "i_r", "zf_f", "zf_b", "z_o", "q_a", "k_a", "v_a")
    ref_off, acc = {}, 0
    for name in ref_order:
        ref_off[name] = acc
        acc += widths[name]
    col, acc = {}, 0
    for name in my_order:
        col[name] = acc // LANE
        acc += widths[name]
    w_in_bf = jnp.concatenate([w_in[:, ref_off[nm]:ref_off[nm] + widths[nm]].astype(BF16) for nm in my_order],
                              axis=1)

    proj = _inproj(x2, g_mix_pre, ada3, w_in_bf, seq)

    oi, qtf, qtb, utf, utb, df, db = _hgrn_a(proj, lb, col, t, d_rec)
    rec_o = _hgrn_c(oi, qtf, qtb, utf, utb, df, db, proj, g_rec_out, col, bsz, seq, d_rec)

    nums, stats = [], []
    for g, (window, dil) in enumerate(DIL_GROUPS):
        hs = slice(g * ATT_HEADS_PER_GROUP, (g + 1) * ATT_HEADS_PER_GROUP)
        bias = _band_bias(rel_bias[:, hs], window, dil)
        num, st = _attn_group(proj, bias, col, g, dil, bsz, seq)
        nums.append(num)
        stats.append(st)

    x1, h2, logits = _merge(rec_o, nums, stats, proj, x2, ada3, g_mix_post, g_ffn_pre,
                            w_branch_rec.astype(BF16), w_branch_att.astype(BF16), w_o.astype(BF16),
                            w_router, b_router, col, seq)

    rt, cnt = _route(logits)
    ne = logits.shape[1]
    counts = cnt[0].astype(jnp.int32)
    top_idx = rt[:, 0:TOP_K].astype(jnp.int32)
    rank = rt[:, 2 * TOP_K:3 * TOP_K].astype(jnp.int32)
    padded = (counts + BM - 1) // BM * BM
    pends = jnp.cumsum(padded)
    pstarts = pends - padded
    dest = (pstarts[top_idx] + rank).reshape(-1)
    p_rows = t * TOP_K + ne * BM
    nb = p_rows // BM
    blk_e = jnp.minimum(jnp.searchsorted(pends, jnp.arange(nb, dtype=jnp.int32) * BM, side="right"),
                        ne - 1).astype(jnp.int32)
    n_used = (pends[-1:] // BM).astype(jnp.int32)

    spr = d // SLAB_W
    tok = jnp.arange(t * TOP_K, dtype=jnp.int32) // TOP_K
    xb = _slab_move(h2.reshape(t * spr, SLAB_W), tok, dest,
                    jnp.zeros((p_rows * spr, SLAB_W), F32)).reshape(p_rows, d)
    act = _moe_up(xb, w_gate_up.astype(BF16), b_gate_up, blk_e, n_used)
    yb = _moe_down(act, w_down.astype(BF16), b_down, blk_e, n_used)
    yg = _slab_move(yb.reshape(p_rows * spr, SLAB_W), dest, jnp.arange(t * TOP_K, dtype=jnp.int32),
                    jnp.zeros((t * TOP_K * spr, SLAB_W), F32)).reshape(t, TOP_K * d)
    return _final(yg, rt, x1, ada3, g_ffn_post, seq)


def kernel(x, c, w_ada, b_ada, g_mix_pre, g_mix_post, g_ffn_pre, g_ffn_post, w_in, g_rec_out, w_branch_rec,
           w_branch_att, w_o, w_router, b_router, w_gate_up, b_gate_up, w_down, b_down, rec_lb_table, rel_bias):
    bsz, seq, d = x.shape
    depth = w_in.shape[0]
    lb_all = jnp.cumsum(jax.nn.softmax(rec_lb_table.astype(F32), axis=1), axis=1)
    x2 = x.reshape(bsz * seq, d)
    for layer in range(depth):
        ada3 = _ada(c, w_ada[layer], b_ada[layer]).reshape(bsz, 6, d)
        x2 = _mixer_ffn_layer(x2, ada3, bsz, seq, g_mix_pre[layer], g_mix_post[layer], g_ffn_pre[layer],
                              g_ffn_post[layer], w_in[layer], lb_all[:, layer], g_rec_out[layer], rel_bias,
                              w_branch_rec[layer], w_branch_att[layer], w_o[layer], w_router[layer],
                              b_router[layer], w_gate_up[layer], b_gate_up[layer], w_down[layer],
                              b_down[layer])
    return x2.reshape(bsz, seq, d)
```

```python
import functools
import math

import numpy as np
import jax
import jax.numpy as jnp
from jax import lax
from jax.experimental import pallas as pl
from jax.experimental.pallas import tpu as pltpu

F32 = jnp.float32
BF16 = jnp.bfloat16

LANE = 128
SUBLANE = 8
SLAB_ROWS = 16

REC_HEAD_DIM = 128
REC_CHUNK = 64
ATT_HEAD_DIM = 128
ATT_HEADS_PER_GROUP = 4
ATT_BLOCK = 64
DIL_GROUPS = ((128, 1), (512, 4), (2048, 16))
NUM_BUCKETS = 32
MAX_DISTANCE = 1024
N_EXPERTS = 32
TOP_K = 4
SWIGLU_LIMIT = 7.0
SWIGLU_ALPHA = 1.702
RMS_EPS = 1e-6
NEG_INF = -1e30

N_LEVELS = 6
BM = 256

_NT = (((1,), (1,)), ((), ()))
_TN = (((0,), (0,)), ((), ()))


def _cparams(sem, vmem_mb):
    return pltpu.CompilerParams(dimension_semantics=sem, vmem_limit_bytes=vmem_mb * 1024 * 1024)


def _rms(x):
    return x * lax.rsqrt(jnp.mean(x * x, axis=-1, keepdims=True) + RMS_EPS)


def _ada_body(c_ref, w_ref, b_ref, o_ref):
    c = c_ref[...]
    cond = (c * jax.nn.sigmoid(c)).astype(BF16)
    o_ref[...] = jnp.dot(cond, w_ref[...].astype(BF16), preferred_element_type=F32) + b_ref[...]


def _ada(c, w, b):
    bsz, d = c.shape
    n = w.shape[1]
    tn = 1024
    cp = jnp.zeros((SUBLANE, d), F32).at[:bsz].set(c)
    out = pl.pallas_call(
        _ada_body,
        grid=(n // tn,),
        in_specs=[pl.BlockSpec((SUBLANE, d), lambda j: (0, 0)),
                  pl.BlockSpec((d, tn), lambda j: (0, j)),
                  pl.BlockSpec((1, tn), lambda j: (0, j))],
        out_specs=pl.BlockSpec((SUBLANE, tn), lambda j: (0, j)),
        out_shape=jax.ShapeDtypeStruct((SUBLANE, n), F32),
        compiler_params=_cparams(("arbitrary",), 40),
        name="ada",
    )(cp, w, b.reshape(1, n))
    return out[:bsz]


def _inproj_body(x_ref, g_ref, ada_ref, w_ref, o_ref, h_ref):
    @pl.when(pl.program_id(1) == 0)
    def _():
        y = _rms(x_ref[...]) * g_ref[...]
        sh = ada_ref[0, 0:1, :]
        sc = ada_ref[0, 1:2, :]
        h_ref[...] = (y * (1.0 + sc) + sh).astype(BF16)

    o_ref[...] = jnp.dot(h_ref[...], w_ref[...], preferred_element_type=F32)


def _inproj(x2, g, ada3, w_bf, seq):
    t, d = x2.shape
    n = w_bf.shape[1]
    tm, tn = 1024, 768
    per_b = seq // tm
    return pl.pallas_call(
        _inproj_body,
        grid=(t // tm, n // tn),
        in_specs=[pl.BlockSpec((tm, d), lambda i, j: (i, 0)),
                  pl.BlockSpec((1, d), lambda i, j: (0, 0)),
                  pl.BlockSpec((1, 6, d), lambda i, j: (i // per_b, 0, 0)),
                  pl.BlockSpec((d, tn), lambda i, j: (0, j))],
        out_specs=pl.BlockSpec((tm, tn), lambda i, j: (i, j)),
        out_shape=jax.ShapeDtypeStruct((t, n), F32),
        scratch_shapes=[pltpu.VMEM((tm, d), BF16)],
        compiler_params=_cparams(("arbitrary", "arbitrary"), 48),
        name="inproj",
    )(x2, g.reshape(1, d), ada3, w_bf)


def _hgrn_consts():
    c = REC_CHUNK
    r = np.arange(c)[:, None]
    m = np.arange(c)[None, :]
    wf = np.zeros((8 * c, c), np.float32)
    wb = np.zeros((8 * c, c), np.float32)
    mf = np.zeros((N_LEVELS + 1, c, c), np.float32)
    for lvl in range(N_LEVELS):
        s = 32 >> lvl
        m0 = (r // (2 * s)) * (2 * s) + s
        up = r >= m0
        wf[lvl * c:(lvl + 1) * c] = np.where(up, (m >= m0) & (m <= r), (m > r) & (m <= m0 - 1))
        wb[lvl * c:(lvl + 1) * c] = np.where(up, (m >= m0) & (m <= r - 1), (m >= r) & (m <= m0 - 1))
        i = np.arange(c)[:, None]
        j = np.arange(c)[None, :]
        mf[lvl] = (i // (2 * s) == j // (2 * s)) & (i % (2 * s) >= s) & (j % (2 * s) < s)
    mf[N_LEVELS] = np.eye(c)
    wf[6 * c:7 * c] = m <= r
    wf[7 * c:8 * c] = m > r
    wb[6 * c:7 * c] = m >= r
    wb[7 * c:8 * c] = m < r
    mb = np.transpose(mf, (0, 2, 1)).copy()
    return (jnp.asarray(wf, BF16), jnp.asarray(wb, BF16), jnp.asarray(mf, F32), jnp.asarray(mb, F32))


def _split3(g):
    hi = g.astype(BF16)
    r1 = g - hi.astype(F32)
    mid = r1.astype(BF16)
    lo = (r1 - mid.astype(F32)).astype(BF16)
    return jnp.concatenate([hi, mid, lo], axis=1)


def _hgrn_a_body(q_ref, i_ref, zf_ref, zb_ref, lb_ref, wf_ref, wb_ref, mf_ref, mb_ref,
                 oi_ref, qtf_ref, qtb_ref, utf_ref, utb_ref, df_ref, db_ref, *, cpb):
    c = REC_CHUNK
    dirs = ((zf_ref, wf_ref, mf_ref, qtf_ref, utf_ref, df_ref, 0, c - 1),
            (zb_ref, wb_ref, mb_ref, qtb_ref, utb_ref, db_ref, 1, 0))

    def chunk(ci, carry):
        rows = pl.ds(pl.multiple_of(ci * c, c), c)
        zq = q_ref[rows, :]
        q = zq * jax.nn.sigmoid(zq)
        vb = i_ref[rows, :].astype(BF16)
        qb16 = q.astype(BF16)
        a = jnp.zeros((c, c), F32)
        for z_ref, w_ref, m_ref, qt_ref, ut_ref, d_ref, di, drow in dirs:
            lb = lb_ref[di:di + 1, :]
            f = lb + (1.0 - lb) * jax.nn.sigmoid(z_ref[rows, :])
            k = 1.0 - f
            g = jnp.log(f)
            r3 = jnp.dot(w_ref[...], _split3(g), preferred_element_type=F32)
            e = jnp.exp(r3[:, 0:LANE] + r3[:, LANE:2 * LANE] + r3[:, 2 * LANE:3 * LANE])
            for lvl in range(N_LEVELS):
                el = e[lvl * c:(lvl + 1) * c]
                p = lax.dot_general((q * el).astype(BF16), (k * el).astype(BF16), _NT,
                                    preferred_element_type=F32)
                a = a + p * m_ref[lvl]
            p = lax.dot_general(qb16, k.astype(BF16), _NT, preferred_element_type=F32)
            a = a + p * m_ref[N_LEVELS]
            qt_ref[rows, :] = (q * e[6 * c:7 * c]).astype(BF16)
            kt = (k * e[7 * c:8 * c]).astype(BF16)
            ut_ref[ci] = lax.dot_general(vb, kt, _TN, preferred_element_type=F32)
            d_ref[pl.ds(ci, 1), :] = e[6 * c + drow:6 * c + drow + 1]
        oi_ref[rows, :] = jnp.dot(a.astype(BF16), vb, preferred_element_type=F32)
        return carry

    lax.fori_loop(0, cpb, chunk, 0)


def _hgrn_a(proj, lb, col, t, d_rec):
    heads = d_rec // REC_HEAD_DIM
    tq = 512
    cpb = tq // REC_CHUNK
    nchunks = t // REC_CHUNK
    wf, wb, mf, mb = _hgrn_consts()
    hd = REC_HEAD_DIM

    def colspec(off):
        return pl.BlockSpec((tq, hd), lambda i, h: (i, off + h))

    full2 = lambda i, h: (0, 0)
    full3 = lambda i, h: (0, 0, 0)
    row_spec = pl.BlockSpec((tq, hd), lambda i, h: (i, h))
    u_spec = pl.BlockSpec((cpb, hd, hd), lambda i, h: (i, 0, h))
    d_spec = pl.BlockSpec((cpb, hd), lambda i, h: (i, h))
    return pl.pallas_call(
        functools.partial(_hgrn_a_body, cpb=cpb),
        grid=(t // tq, heads),
        in_specs=[colspec(col["q_r"]), colspec(col["i_r"]), colspec(col["zf_f"]), colspec(col["zf_b"]),
                  pl.BlockSpec((2, hd), lambda i, h: (0, h)),
                  pl.BlockSpec(wf.shape, full2), pl.BlockSpec(wb.shape, full2),
                  pl.BlockSpec(mf.shape, full3), pl.BlockSpec(mb.shape, full3)],
        out_specs=[row_spec, row_spec, row_spec, u_spec, u_spec, d_spec, d_spec],
        out_shape=[jax.ShapeDtypeStruct((t, d_rec), F32),
                   jax.ShapeDtypeStruct((t, d_rec), BF16),
                   jax.ShapeDtypeStruct((t, d_rec), BF16),
                   jax.ShapeDtypeStruct((nchunks, hd, d_rec), F32),
                   jax.ShapeDtypeStruct((nchunks, hd, d_rec), F32),
                   jax.ShapeDtypeStruct((nchunks, d_rec), F32),
                   jax.ShapeDtypeStruct((nchunks, d_rec), F32)],
        compiler_params=_cparams(("arbitrary", "arbitrary"), 32),
        name="hgrn_a",
    )(proj, proj, proj, proj, lb, wf, wb, mf, mb)


def _hgrn_c_body(oi_ref, qtf_ref, qtb_ref, utf_ref, utb_ref, df_ref, db_ref, z_ref, g_ref,
                 out_ref, acc_ref, *, nchunks):
    c = REC_CHUNK
    hd = REC_HEAD_DIM

    def fwd(n, st):
        rows = pl.ds(pl.multiple_of(n * c, c), c)
        o = lax.dot_general(qtf_ref[rows, :], st.astype(BF16), _NT, preferred_element_type=F32)
        acc_ref[rows, :] = oi_ref[rows, :] + o
        return df_ref[pl.ds(n, 1), :] * st + utf_ref[n]

    lax.fori_loop(0, nchunks, fwd, jnp.zeros((hd, hd), F32), unroll=4)

    def bwd(i, st):
        n = nchunks - 1 - i
        rows = pl.ds(pl.multiple_of(n * c, c), c)
        o = lax.dot_general(qtb_ref[rows, :], st.astype(BF16), _NT, preferred_element_type=F32)
        acc_ref[rows, :] = acc_ref[rows, :] + o
        return db_ref[pl.ds(n, 1), :] * st + utb_ref[n]

    lax.fori_loop(0, nchunks, bwd, jnp.zeros((hd, hd), F32), unroll=4)

    o = _rms(acc_ref[...])
    out_ref[...] = (o * g_ref[...] * jax.nn.sigmoid(z_ref[...])).astype(BF16)


def _hgrn_c(oi, qtf, qtb, utf, utb, df, db, proj, g_out, col, bsz, seq, d_rec):
    heads = d_rec // REC_HEAD_DIM
    hd = REC_HEAD_DIM
    nchunks = seq // REC_CHUNK
    row_spec = pl.BlockSpec((seq, hd), lambda b, h: (b, h))
    u_spec = pl.BlockSpec((nchunks, hd, hd), lambda b, h: (b, 0, h))
    d_spec = pl.BlockSpec((nchunks, hd), lambda b, h: (b, h))
    zo = col["z_o"]
    return pl.pallas_call(
        functools.partial(_hgrn_c_body, nchunks=nchunks),
        grid=(bsz, heads),
        in_specs=[row_spec, row_spec, row_spec, u_spec, u_spec, d_spec, d_spec,
                  pl.BlockSpec((seq, hd), lambda b, h: (b, zo + h)),
                  pl.BlockSpec((1, hd), lambda b, h: (0, h))],
        out_specs=row_spec,
        out_shape=jax.ShapeDtypeStruct((bsz * seq, d_rec), BF16),
        scratch_shapes=[pltpu.VMEM((seq, hd), F32)],
        compiler_params=_cparams(("arbitrary", "arbitrary"), 48),
        name="hgrn_c",
    )(oi, qtf, qtb, utf, utb, df, db, proj, g_out.reshape(1, d_rec))


def _t5_bucket(rel):
    half_buckets = NUM_BUCKETS // 2
    ret = jnp.where(rel > 0, half_buckets, 0)
    n = jnp.abs(rel)
    max_exact = half_buckets // 2
    nf = jnp.maximum(n, 1).astype(F32)
    large = max_exact + (jnp.log(nf / max_exact) / math.log(MAX_DISTANCE / max_exact)
                         * (half_buckets - max_exact)).astype(jnp.int32)
    large = jnp.minimum(large, half_buckets - 1)
    return ret + jnp.where(n < max_exact, n, large)


def _band_bias(rel_bias_g, window, dil):
    half = window // (2 * dil)
    q_off = jnp.arange(ATT_BLOCK)[:, None]
    rel = jnp.arange(3 * ATT_BLOCK)[None, :] - ATT_BLOCK - q_off
    bias = rel_bias_g[_t5_bucket(rel * dil)].transpose(2, 0, 1).astype(F32)
    return jnp.where((jnp.abs(rel) <= half)[None], bias, NEG_INF)


def _attn_body(q_ref, kp_ref, k_ref, kn_ref, vp_ref, v_ref, vn_ref, bias_ref,
               num_ref, st_ref, kc_ref, vc_ref, *, dil, tq, sub_len):
    blk = ATT_BLOCK
    nqb = tq // blk
    n = pl.program_id(1)
    h = pl.program_id(2)
    scale = ATT_HEAD_DIM ** -0.5

    def sds(start, size):
        if dil == 1:
            return pl.ds(start, size)
        return pl.ds(start, size, stride=dil)

    @pl.when(h == 0)
    def _():
        st_ref[...] = jnp.zeros_like(st_ref)

    def one_class(r, carry):
        kc_ref[0:blk, :] = kp_ref[sds(r, blk), :].astype(BF16)
        kc_ref[blk:blk + tq, :] = k_ref[sds(r, tq), :].astype(BF16)
        kc_ref[blk + tq:2 * blk + tq, :] = kn_ref[sds(r, blk), :].astype(BF16)
        vc_ref[0:blk, :] = vp_ref[sds(r, blk), :].astype(BF16)
        vc_ref[blk:blk + tq, :] = v_ref[sds(r, tq), :].astype(BF16)
        vc_ref[blk + tq:2 * blk + tq, :] = vn_ref[sds(r, blk), :].astype(BF16)

        def one_qblock(qb, carry2):
            q0 = pl.multiple_of(qb * blk, blk)
            rows = sds(r + dil * q0, blk)
            kpos = n * tq + q0 - blk + lax.broadcasted_iota(jnp.int32, (1, 3 * blk), 1)
            valid = (kpos >= 0) & (kpos < sub_len)
            lane = lax.broadcasted_iota(jnp.int32, (blk, LANE), 1)
            q = q_ref[rows, :].astype(BF16)
            kw = kc_ref[pl.ds(q0, 3 * blk), :]
            vw = vc_ref[pl.ds(q0, 3 * blk), :]
            s = lax.dot_general(q, kw, _NT, preferred_element_type=F32) * scale
            s = jnp.where(valid, s + bias_ref[0], NEG_INF)
            m = jnp.max(s, axis=-1, keepdims=True)
            p = jnp.exp(s - m)
            l = jnp.sum(p, axis=-1, keepdims=True)
            num_ref[rows, :] = jnp.dot(p.astype(BF16), vw, preferred_element_type=F32)
            st = st_ref[rows, :]
            st = jnp.where(lane == h, m, st)
            st_ref[rows, :] = jnp.where(lane == ATT_HEADS_PER_GROUP + h, l, st)
            return carry2

        lax.fori_loop(0, nqb, one_qblock, 0)
        return carry

    lax.fori_loop(0, dil, one_class, 0)


def _attn_group(proj, bias, col, g, dil, bsz, seq):
    tile = 1024
    tq = tile // dil
    halo = ATT_BLOCK * dil
    sub_len = seq // dil
    nh = ATT_HEADS_PER_GROUP
    hd = ATT_HEAD_DIM
    qc = col["q_a"] + g * nh
    kc = col["k_a"] + g * nh
    vc = col["v_a"] + g * nh
    tiles_b = seq // tile
    halos_b = seq // halo
    hpt = tile // halo

    own = lambda c: pl.BlockSpec((tile, hd), lambda b, n, h: (b * tiles_b + n, c + h))
    prev = lambda c: pl.BlockSpec(
        (halo, hd), lambda b, n, h: (b * halos_b + jnp.maximum(n * hpt - 1, 0), c + h))
    nxt = lambda c: pl.BlockSpec(
        (halo, hd), lambda b, n, h: (b * halos_b + jnp.minimum((n + 1) * hpt, halos_b - 1), c + h))
    t = bsz * seq
    return pl.pallas_call(
        functools.partial(_attn_body, dil=dil, tq=tq, sub_len=sub_len),
        grid=(bsz, tiles_b, nh),
        in_specs=[own(qc), prev(kc), own(kc), nxt(kc), prev(vc), own(vc), nxt(vc),
                  pl.BlockSpec((1,) + bias.shape[1:], lambda b, n, h: (h, 0, 0))],
        out_specs=[pl.BlockSpec((tile, hd), lambda b, n, h: (b * tiles_b + n, h)),
                   pl.BlockSpec((tile, LANE), lambda b, n, h: (b * tiles_b + n, 0))],
        out_shape=[jax.ShapeDtypeStruct((t, nh * hd), F32), jax.ShapeDtypeStruct((t, LANE), F32)],
        scratch_shapes=[pltpu.VMEM((tq + 2 * ATT_BLOCK, hd), BF16),
                        pltpu.VMEM((tq + 2 * ATT_BLOCK, hd), BF16)],
        compiler_params=_cparams(("arbitrary", "arbitrary", "arbitrary"), 32),
        name=f"attn_d{dil}",
    )(proj, proj, proj, proj, proj, proj, proj, bias)


def _merge_body(rec_ref, n0_ref, n1_ref, n2_ref, s0_ref, s1_ref, s2_ref, zgr_ref, zga_ref, x_ref,
                ada_ref, gpost_ref, gpre_ref, wbr_ref, wba_ref, wo_ref, wr_ref, br_ref,
                x1_ref, h2_ref, lg_ref):
    nh = ATT_HEADS_PER_GROUP
    hd = ATT_HEAD_DIM
    stats = [s[...] for s in (s0_ref, s1_ref, s2_ref)]
    ms = [s[:, 0:nh] for s in stats]
    ls = [s[:, nh:2 * nh] for s in stats]
    m_all = jnp.maximum(jnp.maximum(ms[0], ms[1]), ms[2])
    ws = [jnp.exp(m - m_all) for m in ms]
    den = ws[0] * ls[0] + ws[1] * ls[1] + ws[2] * ls[2]
    heads = []
    for h in range(nh):
        cols = slice(h * hd, (h + 1) * hd)
        num = (ws[0][:, h:h + 1] * n0_ref[:, cols] + ws[1][:, h:h + 1] * n1_ref[:, cols]
               + ws[2][:, h:h + 1] * n2_ref[:, cols])
        heads.append((num / den[:, h:h + 1]).astype(BF16))
    att = jnp.concatenate(heads, axis=1)
    y_rec = jnp.dot(rec_ref[...], wbr_ref[...], preferred_element_type=F32)
    y_att = jnp.dot(att, wba_ref[...], preferred_element_type=F32)
    merged = jax.nn.sigmoid(zgr_ref[...]) * y_rec + jax.nn.sigmoid(zga_ref[...]) * y_att
    y = jnp.dot(merged.astype(BF16), wo_ref[...], preferred_element_type=F32)
    gt_m = ada_ref[0, 2:3, :]
    sh_f = ada_ref[0, 3:4, :]
    sc_f = ada_ref[0, 4:5, :]
    x1 = x_ref[...] + gt_m * (_rms(y) * gpost_ref[...])
    x1_ref[...] = x1
    h2 = _rms(x1) * gpre_ref[...] * (1.0 + sc_f) + sh_f
    tm = h2.shape[0]
    for s in range(SLAB_ROWS):
        h2_ref[pl.ds(s, tm, stride=SLAB_ROWS), :] = h2[:, s * LANE:(s + 1) * LANE]
    lg_ref[...] = jnp.dot(h2, wr_ref[...], preferred_element_type=F32,
                          precision=lax.Precision.HIGHEST) + br_ref[...]


def _merge(rec_o, nums, stats, proj, x2, ada3, g_post, g_pre, wbr, wba, wo, w_router, b_router, col, seq):
    t, d = x2.shape
    tm = 256
    per_b = seq // tm
    d_rec = rec_o.shape[1]
    w_att = nums[0].shape[1]
    ne = w_router.shape[1]
    dl = d // LANE
    row = lambda w: pl.BlockSpec((tm, w), lambda i: (i, 0))
    const = lambda shape: pl.BlockSpec(shape, lambda i: (0,) * len(shape), pipeline_mode=pl.Buffered(1))
    zgr = col["zg_rec"] // dl
    zga = col["zg_att"] // dl
    return pl.pallas_call(
        _merge_body,
        grid=(t // tm,),
        in_specs=[row(d_rec), row(w_att), row(w_att), row(w_att), row(LANE), row(LANE), row(LANE),
                  pl.BlockSpec((tm, d), lambda i: (i, zgr)),
                  pl.BlockSpec((tm, d), lambda i: (i, zga)),
                  row(d),
                  pl.BlockSpec((1, 6, d), lambda i: (i // per_b, 0, 0)),
                  const((1, d)), const((1, d)),
                  const(wbr.shape), const(wba.shape), const(wo.shape), const(w_router.shape),
                  const((1, ne))],
        out_specs=[row(d), pl.BlockSpec((tm * SLAB_ROWS, LANE), lambda i: (i, 0)),
                   pl.BlockSpec((tm, ne), lambda i: (i, 0))],
        out_shape=[jax.ShapeDtypeStruct((t, d), F32), jax.ShapeDtypeStruct((t * SLAB_ROWS, LANE), F32),
                   jax.ShapeDtypeStruct((t, ne), F32)],
        compiler_params=_cparams(("arbitrary",), 56),
        name="merge",
    )(rec_o, nums[0], nums[1], nums[2], stats[0], stats[1], stats[2], proj, proj, x2, ada3,
      g_post.reshape(1, d), g_pre.reshape(1, d), wbr, wba, wo, w_router, b_router.reshape(1, ne))


def _route_body(lg_ref, tri_ref, rt_ref, cnt_ref, carry_ref):
    i = pl.program_id(0)
    tr, ne = lg_ref.shape

    @pl.when(i == 0)
    def _():
        carry_ref[...] = jnp.zeros_like(carry_ref)

    l = lg_ref[...]
    lane = lax.broadcasted_iota(jnp.int32, (tr, ne), 1).astype(F32)
    vals, sels, idxs = [], [], []
    for _ in range(TOP_K):
        m = jnp.max(l, axis=-1, keepdims=True)
        idx = jnp.min(jnp.where(l == m, lane, float(ne)), axis=-1, keepdims=True)
        sel = lane == idx
        vals.append(m)
        idxs.append(idx)
        sels.append(sel)
        l = jnp.where(sel, -jnp.inf, l)
    es = [jnp.exp(v - vals[0]) for v in vals]
    tot = es[0] + es[1] + es[2] + es[3]
    chosen = (sels[0] | sels[1] | sels[2] | sels[3]).astype(F32)
    prefix = jnp.dot(tri_ref[...], chosen.astype(BF16), preferred_element_type=F32) + carry_ref[0:1, :]
    out_lane = lax.broadcasted_iota(jnp.int32, (tr, LANE), 1)
    rt = jnp.zeros((tr, LANE), F32)
    for k in range(TOP_K):
        rank = jnp.sum(jnp.where(sels[k], prefix, 0.0), axis=-1, keepdims=True)
        rt = jnp.where(out_lane == k, idxs[k], rt)
        rt = jnp.where(out_lane == TOP_K + k, es[k] / tot, rt)
        rt = jnp.where(out_lane == 2 * TOP_K + k, rank, rt)
    rt_ref[...] = rt
    new = carry_ref[0:1, :] + jnp.sum(chosen, axis=0, keepdims=True)
    carry_ref[...] = jnp.broadcast_to(new, carry_ref.shape)
    cnt_ref[...] = carry_ref[...]


def _route(logits):
    t, ne = logits.shape
    tr = 512
    tri = jnp.asarray(np.tril(np.ones((tr, tr), np.float32), -1), BF16)
    return pl.pallas_call(
        _route_body,
        grid=(t // tr,),
        in_specs=[pl.BlockSpec((tr, ne), lambda i: (i, 0)),
                  pl.BlockSpec((tr, tr), lambda i: (0, 0))],
        out_specs=[pl.BlockSpec((tr, LANE), lambda i: (i, 0)),
                   pl.BlockSpec((SUBLANE, ne), lambda i: (0, 0))],
        out_shape=[jax.ShapeDtypeStruct((t, LANE), F32), jax.ShapeDtypeStruct((SUBLANE, ne), F32)],
        scratch_shapes=[pltpu.VMEM((SUBLANE, ne), F32)],
        compiler_params=_cparams(("arbitrary",), 32),
        name="route",
    )(logits, tri)


def _moe_up_body(be_ref, nu_ref, idx0_ref, idxn_ref, h2s_ref, w_ref, bias_ref, o_ref, xbuf, sem):
    del be_ref
    b = pl.program_id(0)
    nu = nu_ref[0]
    f = o_ref.shape[1]
    slab = SLAB_ROWS

    def issue(idx_ref, slot):
        def body(r, c):
            tok = idx_ref[0, 0, r]
            pltpu.make_async_copy(h2s_ref.at[pl.ds(pl.multiple_of(tok * slab, slab), slab), :],
                                  xbuf.at[slot, pl.ds(pl.multiple_of(r * slab, slab), slab), :],
                                  sem.at[slot]).start()
            return c
        lax.fori_loop(0, BM, body, 0)

    @pl.when(b == 0)
    def _():
        issue(idx0_ref, 0)

    @pl.when(b + 1 < nu)
    def _():
        issue(idxn_ref, (b + 1) % 2)

    @pl.when(b < nu)
    def _():
        slot = b % 2
        pltpu.make_async_copy(h2s_ref.at[pl.ds(0, BM * slab), :], xbuf.at[slot], sem.at[slot]).wait()
        x = jnp.concatenate([xbuf[slot, pl.ds(s, BM, stride=slab), :].astype(BF16) for s in range(slab)],
                            axis=1)
        half = f // 2
        for c0 in (0, half):
            gate = (jnp.dot(x, w_ref[0, :, c0:c0 + half], preferred_element_type=F32)
                    + bias_ref[0, :, c0:c0 + half])
            up = (jnp.dot(x, w_ref[0, :, f + c0:f + c0 + half], preferred_element_type=F32)
                  + bias_ref[0, :, f + c0:f + c0 + half])
            gate = jnp.minimum(gate, SWIGLU_LIMIT)
            up = jnp.clip(up, -SWIGLU_LIMIT, SWIGLU_LIMIT)
            o_ref[:, c0:c0 + half] = (gate * jax.nn.sigmoid(SWIGLU_ALPHA * gate) * (up + 1.0)).astype(BF16)

    @pl.when(b >= nu)
    def _():
        o_ref[...] = jnp.zeros_like(o_ref)


def _moe_up(h2s, src_tok, wgu_bf, bgu, blk_e, n_used):
    ne, d, f2 = wgu_bf.shape
    f = f2 // 2
    p = src_tok.shape[0]
    nb = p // BM
    idx3 = src_tok.reshape(nb, 1, BM)
    grid_spec = pltpu.PrefetchScalarGridSpec(
        num_scalar_prefetch=2,
        grid=(nb,),
        in_specs=[pl.BlockSpec((1, 1, BM), lambda b, be, nu: (0, 0, 0), memory_space=pltpu.SMEM),
                  pl.BlockSpec((1, 1, BM), lambda b, be, nu: (jnp.minimum(b + 1, nb - 1), 0, 0),
                               memory_space=pltpu.SMEM),
                  pl.BlockSpec(memory_space=pl.ANY),
                  pl.BlockSpec((1, d, f2), lambda b, be, nu: (be[b], 0, 0)),
                  pl.BlockSpec((1, 1, f2), lambda b, be, nu: (be[b], 0, 0))],
        out_specs=pl.BlockSpec((BM, f), lambda b, be, nu: (b, 0)),
        scratch_shapes=[pltpu.VMEM((2, BM * SLAB_ROWS, LANE), F32), pltpu.SemaphoreType.DMA((2,))],
    )
    return pl.pallas_call(
        _moe_up_body,
        grid_spec=grid_spec,
        out_shape=jax.ShapeDtypeStruct((p, f), BF16),
        compiler_params=_cparams(("arbitrary",), 56),
        name="moe_up",
    )(blk_e, n_used, idx3, idx3, h2s, wgu_bf, bgu.reshape(ne, 1, f2))


def _moe_down_body(be_ref, nu_ref, nv_ref, dst_ref, a_ref, w_ref, bias_ref, ysc_ref, ybuf, sem):
    del be_ref
    b = pl.program_id(0)
    nb = pl.num_programs(0)
    nu = nu_ref[0]
    slab = SLAB_ROWS

    def row_copy(slot, r, d):
        return pltpu.make_async_copy(ybuf.at[slot, pl.ds(pl.multiple_of(r * slab, slab), slab), :],
                                     ysc_ref.at[pl.ds(pl.multiple_of(d * slab, slab), slab), :],
                                     sem.at[slot])

    def drain(step):
        def body(r, c):
            row_copy(step % 2, 0, 0).wait()
            return c
        lax.fori_loop(0, nv_ref[step], body, 0)

    @pl.when((b >= 2) & (b < nu))
    def _():
        drain(b - 2)

    @pl.when(b < nu)
    def _():
        slot = b % 2
        y = jnp.dot(a_ref[...], w_ref[0], preferred_element_type=F32) + bias_ref[0]
        for s in range(slab):
            ybuf[slot, pl.ds(s, BM, stride=slab), :] = y[:, s * LANE:(s + 1) * LANE]

        def body(r, c):
            row_copy(slot, r, dst_ref[0, 0, r]).start()
            return c
        lax.fori_loop(0, nv_ref[b], body, 0)

    @pl.when(b == nb - 1)
    def _():
        @pl.when(nu >= 2)
        def _():
            drain(nu - 2)
        drain(nu - 1)


def _moe_down(act, dst_slot, nvalid, n_out_rows, wd_bf, bd, blk_e, n_used):
    p, f = act.shape
    ne, _, d = wd_bf.shape
    nb = p // BM
    grid_spec = pltpu.PrefetchScalarGridSpec(
        num_scalar_prefetch=3,
        grid=(nb,),
        in_specs=[pl.BlockSpec((1, 1, BM), lambda b, be, nu, nv: (b, 0, 0), memory_space=pltpu.SMEM),
                  pl.BlockSpec((BM, f), lambda b, be, nu, nv: (b, 0)),
                  pl.BlockSpec((1, f, d), lambda b, be, nu, nv: (be[b], 0, 0)),
                  pl.BlockSpec((1, 1, d), lambda b, be, nu, nv: (be[b], 0, 0))],
        out_specs=pl.BlockSpec(memory_space=pl.ANY),
        scratch_shapes=[pltpu.VMEM((2, BM * SLAB_ROWS, LANE), F32), pltpu.SemaphoreType.DMA((2,))],
    )
    return pl.pallas_call(
        _moe_down_body,
        grid_spec=grid_spec,
        out_shape=jax.ShapeDtypeStruct((n_out_rows * SLAB_ROWS, LANE), F32),
        compiler_params=pltpu.CompilerParams(dimension_semantics=("arbitrary",),
                                             vmem_limit_bytes=48 * 1024 * 1024, has_side_effects=True),
        name="moe_down",
    )(blk_e, n_used, nvalid, dst_slot.reshape(nb, 1, BM), act, wd_bf, bd.reshape(ne, 1, d))


def _final_body(ysc_ref, rt_ref, x1_ref, ada_ref, g_ref, o_ref):
    tm = x1_ref.shape[0]
    rt = rt_ref[...]
    stride = TOP_K * SLAB_ROWS
    pieces = []
    for s in range(SLAB_ROWS):
        acc = rt[:, TOP_K:TOP_K + 1] * ysc_ref[pl.ds(s, tm, stride=stride), :]
        for k in range(1, TOP_K):
            acc = acc + rt[:, TOP_K + k:TOP_K + k + 1] * ysc_ref[pl.ds(k * SLAB_ROWS + s, tm, stride=stride), :]
        pieces.append(acc)
    y = jnp.concatenate(pieces, axis=1)
    gt_f = ada_ref[0, 5:6, :]
    o_ref[...] = x1_ref[...] + gt_f * (_rms(y) * g_ref[...])


def _final(ysc, rt, x1, ada3, g_post, seq):
    t, d = x1.shape
    tm = 256
    per_b = seq // tm
    return pl.pallas_call(
        _final_body,
        grid=(t // tm,),
        in_specs=[pl.BlockSpec((tm * TOP_K * SLAB_ROWS, LANE), lambda i: (i, 0)),
                  pl.BlockSpec((tm, LANE), lambda i: (i, 0)),
                  pl.BlockSpec((tm, d), lambda i: (i, 0)),
                  pl.BlockSpec((1, 6, d), lambda i: (i // per_b, 0, 0)),
                  pl.BlockSpec((1, d), lambda i: (0, 0))],
        out_specs=pl.BlockSpec((tm, d), lambda i: (i, 0)),
        out_shape=jax.ShapeDtypeStruct((t, d), F32),
        compiler_params=_cparams(("arbitrary",), 48),
        name="final",
    )(ysc, rt, x1, ada3, g_post.reshape(1, d))


def _mixer_ffn_layer(x2, ada3, bsz, seq, g_mix_pre, g_mix_post, g_ffn_pre, g_ffn_post, w_in, lb, g_rec_out,
                     rel_bias, w_branch_rec, w_branch_att, w_o, w_router, b_router, w_gate_up, b_gate_up,
                     w_down, b_down):
    t, d = x2.shape
    d_rec = w_branch_rec.shape[0]
    w_att = w_branch_att.shape[0]
    d_att = 3 * w_att
    widths = dict(q_r=d_rec, i_r=d_rec, zf_f=d_rec, zf_b=d_rec, z_o=d_rec, q_a=d_att, k_a=d_att, v_a=d_att,
                  zg_rec=d, zg_att=d)
    ref_order = ("q_r", "i_r", "zf_f", "zf_b", "z_o", "q_a", "k_a", "v_a", "zg_rec", "zg_att")
    my_order = ("zg_rec", "zg_att", "q_r", "i_r", "zf_f", "zf_b", "z_o", "q_a", "k_a", "v_a")
    ref_off, acc = {}, 0
    for name in ref_order:
        ref_off[name] = acc
        acc += widths[name]
    col, acc = {}, 0
    for name in my_order:
        col[name] = acc // LANE
        acc += widths[name]
    w_in_bf = jnp.concatenate([w_in[:, ref_off[nm]:ref_off[nm] + widths[nm]].astype(BF16) for nm in my_order],
                              axis=1)

    proj = _inproj(x2, g_mix_pre, ada3, w_in_bf, seq)

    oi, qtf, qtb, utf, utb, df, db = _hgrn_a(proj, lb, col, t, d_rec)
    rec_o = _hgrn_c(oi, qtf, qtb, utf, utb, df, db, proj, g_rec_out, col, bsz, seq, d_rec)

    nums, stats = [], []
    for g, (window, dil) in enumerate(DIL_GROUPS):
        hs = slice(g * ATT_HEADS_PER_GROUP, (g + 1) * ATT_HEADS_PER_GROUP)
        bias = _band_bias(rel_bias[:, hs], window, dil)
        num, st = _attn_group(proj, bias, col, g, dil, bsz, seq)
        nums.append(num)
        stats.append(st)

    x1, h2s, logits = _merge(rec_o, nums, stats, proj, x2, ada3, g_mix_post, g_ffn_pre,
                            w_branch_rec.astype(BF16), w_branch_att.astype(BF16), w_o.astype(BF16),
                            w_router, b_router, col, seq)

    rt, cnt = _route(logits)
    ne = logits.shape[1]
    counts = cnt[0].astype(jnp.int32)
    top_idx = rt[:, 0:TOP_K].astype(jnp.int32)
    rank = rt[:, 2 * TOP_K:3 * TOP_K].astype(jnp.int32)
    padded = (counts + BM - 1) // BM * BM
    pends = jnp.cumsum(padded)
    pstarts = pends - padded
    dest = (pstarts[top_idx] + rank).reshape(-1)
    p_rows = t * TOP_K + ne * BM
    nb = p_rows // BM
    blk_e = jnp.minimum(jnp.searchsorted(pends, jnp.arange(nb, dtype=jnp.int32) * BM, side="right"),
                        ne - 1).astype(jnp.int32)
    n_used = (pends[-1:] // BM).astype(jnp.int32)

    n_assign = t * TOP_K
    slot_assign = jnp.full((p_rows,), -1, jnp.int32).at[dest].set(jnp.arange(n_assign, dtype=jnp.int32))
    src_tok = jnp.maximum(slot_assign, 0) // TOP_K
    valid_end = pstarts + counts
    blk_start = jnp.arange(nb, dtype=jnp.int32) * BM
    nvalid = jnp.clip(valid_end[blk_e] - blk_start, 0, BM).astype(jnp.int32)
    nvalid = jnp.where(jnp.arange(nb) < n_used[0], nvalid, 0)

    act = _moe_up(h2s, src_tok, w_gate_up.astype(BF16), b_gate_up, blk_e, n_used)
    ysc = _moe_down(act, slot_assign, nvalid, n_assign, w_down.astype(BF16), b_down, blk_e, n_used)
    return _final(ysc, rt, x1, ada3, g_ffn_post, seq)


def kernel(x, c, w_ada, b_ada, g_mix_pre, g_mix_post, g_ffn_pre, g_ffn_post, w_in, g_rec_out, w_branch_rec,
           w_branch_att, w_o, w_router, b_router, w_gate_up, b_gate_up, w_down, b_down, rec_lb_table, rel_bias):
    bsz, seq, d = x.shape
    depth = w_in.shape[0]
    lb_all = jnp.cumsum(jax.nn.softmax(rec_lb_table.astype(F32), axis=1), axis=1)
    x2 = x.reshape(bsz * seq, d)
    for layer in range(depth):
        ada3 = _ada(c, w_ada[layer], b_ada[layer]).reshape(bsz, 6, d)
        x2 = _mixer_ffn_layer(x2, ada3, bsz, seq, g_mix_pre[layer], g_mix_post[layer], g_ffn_pre[layer],
                              g_ffn_post[layer], w_in[layer], lb_all[:, layer], g_rec_out[layer], rel_bias,
                              w_branch_rec[layer], w_branch_att[layer], w_o[layer], w_router[layer],
                              b_router[layer], w_gate_up[layer], b_gate_up[layer], w_down[layer],
                              b_down[layer])
    return x2.reshape(bsz, seq, d)
```

```python
import functools
import math

import numpy as np
import jax
import jax.numpy as jnp
from jax import lax
from jax.experimental import pallas as pl
from jax.experimental.pallas import tpu as pltpu

F32 = jnp.float32
BF16 = jnp.bfloat16

LANE = 128
SUBLANE = 8
SLAB_ROWS = 16

REC_HEAD_DIM = 128
REC_CHUNK = 64
ATT_HEAD_DIM = 128
ATT_HEADS_PER_GROUP = 4
ATT_BLOCK = 64
DIL_GROUPS = ((128, 1), (512, 4), (2048, 16))
NUM_BUCKETS = 32
MAX_DISTANCE = 1024
N_EXPERTS = 32
TOP_K = 4
SWIGLU_LIMIT = 7.0
SWIGLU_ALPHA = 1.702
RMS_EPS = 1e-6
NEG_INF = -1e30

N_LEVELS = 6
HGRN_UNROLL = 4
DMA_UNROLL = 8
ATT_UNROLL = 8
BM = 256

_NT = (((1,), (1,)), ((), ()))
_TN = (((0,), (0,)), ((), ()))


def _cparams(sem, vmem_mb):
    return pltpu.CompilerParams(dimension_semantics=sem, vmem_limit_bytes=vmem_mb * 1024 * 1024)


def _rms(x):
    return x * lax.rsqrt(jnp.mean(x * x, axis=-1, keepdims=True) + RMS_EPS)


def _ada_body(c_ref, w_ref, b_ref, o_ref):
    c = c_ref[...]
    cond = (c * jax.nn.sigmoid(c)).astype(BF16)
    o_ref[...] = jnp.dot(cond, w_ref[...].astype(BF16), preferred_element_type=F32) + b_ref[...]


def _ada(c, w, b):
    bsz, d = c.shape
    n = w.shape[1]
    tn = 1024
    cp = jnp.zeros((SUBLANE, d), F32).at[:bsz].set(c)
    out = pl.pallas_call(
        _ada_body,
        grid=(n // tn,),
        in_specs=[pl.BlockSpec((SUBLANE, d), lambda j: (0, 0)),
                  pl.BlockSpec((d, tn), lambda j: (0, j)),
                  pl.BlockSpec((1, tn), lambda j: (0, j))],
        out_specs=pl.BlockSpec((SUBLANE, tn), lambda j: (0, j)),
        out_shape=jax.ShapeDtypeStruct((SUBLANE, n), F32),
        compiler_params=_cparams(("arbitrary",), 40),
        name="ada",
    )(cp, w, b.reshape(1, n))
    return out[:bsz]


def _inproj_body(x_ref, g_ref, ada_ref, w_ref, o_ref, h_ref):
    @pl.when(pl.program_id(1) == 0)
    def _():
        y = _rms(x_ref[...]) * g_ref[...]
        sh = ada_ref[0, 0:1, :]
        sc = ada_ref[0, 1:2, :]
        h_ref[...] = (y * (1.0 + sc) + sh).astype(BF16)

    o_ref[...] = jnp.dot(h_ref[...], w_ref[...], preferred_element_type=F32)


def _inproj(x2, g, ada3, w_bf, seq):
    t, d = x2.shape
    n = w_bf.shape[1]
    tm, tn = 1024, 768
    per_b = seq // tm
    return pl.pallas_call(
        _inproj_body,
        grid=(t // tm, n // tn),
        in_specs=[pl.BlockSpec((tm, d), lambda i, j: (i, 0)),
                  pl.BlockSpec((1, d), lambda i, j: (0, 0)),
                  pl.BlockSpec((1, 6, d), lambda i, j: (i // per_b, 0, 0)),
                  pl.BlockSpec((d, tn), lambda i, j: (0, j))],
        out_specs=pl.BlockSpec((tm, tn), lambda i, j: (i, j)),
        out_shape=jax.ShapeDtypeStruct((t, n), F32),
        scratch_shapes=[pltpu.VMEM((tm, d), BF16)],
        compiler_params=_cparams(("arbitrary", "arbitrary"), 48),
        name="inproj",
    )(x2, g.reshape(1, d), ada3, w_bf)


def _hgrn_consts():
    c = REC_CHUNK
    r = np.arange(c)[:, None]
    m = np.arange(c)[None, :]
    wf = np.zeros((8 * c, c), np.float32)
    wb = np.zeros((8 * c, c), np.float32)
    mf = np.zeros((N_LEVELS + 1, c, c), np.float32)
    for lvl in range(N_LEVELS):
        s = 32 >> lvl
        m0 = (r // (2 * s)) * (2 * s) + s
        up = r >= m0
        wf[lvl * c:(lvl + 1) * c] = np.where(up, (m >= m0) & (m <= r), (m > r) & (m <= m0 - 1))
        wb[lvl * c:(lvl + 1) * c] = np.where(up, (m >= m0) & (m <= r - 1), (m >= r) & (m <= m0 - 1))
        i = np.arange(c)[:, None]
        j = np.arange(c)[None, :]
        mf[lvl] = (i // (2 * s) == j // (2 * s)) & (i % (2 * s) >= s) & (j % (2 * s) < s)
    mf[N_LEVELS] = np.eye(c)
    wf[6 * c:7 * c] = m <= r
    wf[7 * c:8 * c] = m > r
    wb[6 * c:7 * c] = m >= r
    wb[7 * c:8 * c] = m < r
    mb = np.transpose(mf, (0, 2, 1)).copy()
    wf3 = np.concatenate([wf, wf, wf], axis=1)
    wb3 = np.concatenate([wb, wb, wb], axis=1)
    return (jnp.asarray(wf3, BF16), jnp.asarray(wb3, BF16), jnp.asarray(mf, F32), jnp.asarray(mb, F32))


def _split3(g):
    hi = g.astype(BF16)
    r1 = g - hi.astype(F32)
    mid = r1.astype(BF16)
    lo = (r1 - mid.astype(F32)).astype(BF16)
    return jnp.concatenate([hi, mid, lo], axis=0)


def _hgrn_a_body(q_ref, i_ref, zf_ref, zb_ref, lb_ref, wf_ref, wb_ref, mf_ref, mb_ref,
                 oi_ref, qtf_ref, qtb_ref, utf_ref, utb_ref, df_ref, db_ref, *, cpb):
    c = REC_CHUNK
    dirs = ((zf_ref, wf_ref, mf_ref, qtf_ref, utf_ref, df_ref, 0, c - 1),
            (zb_ref, wb_ref, mb_ref, qtb_ref, utb_ref, db_ref, 1, 0))

    def chunk_group(cg, carry):
        cis = [cg * HGRN_UNROLL + u for u in range(HGRN_UNROLL)]
        rows = [pl.ds(pl.multiple_of(ci * c, c), c) for ci in cis]
        zqs = [q_ref[rw, :] for rw in rows]
        qs = [zq * jax.nn.sigmoid(zq) for zq in zqs]
        vbs = [i_ref[rw, :].astype(BF16) for rw in rows]
        units = [(u, d) for u in range(HGRN_UNROLL) for d in range(2)]
        ks, es = {}, {}
        for u, d in units:
            z_ref, w_ref = dirs[d][0], dirs[d][1]
            lb = lb_ref[d:d + 1, :]
            f = lb + (1.0 - lb) * jax.nn.sigmoid(z_ref[rows[u], :])
            ks[u, d] = 1.0 - f
            es[u, d] = jnp.exp(jnp.dot(w_ref[...], _split3(jnp.log(f)), preferred_element_type=F32))
        acc = [jnp.zeros((c, c), F32) for _ in range(HGRN_UNROLL)]
        for lvl in range(N_LEVELS + 1):
            for u, d in units:
                m_ref = dirs[d][2]
                if lvl < N_LEVELS:
                    el = es[u, d][lvl * c:(lvl + 1) * c]
                    qa, ka = (qs[u] * el).astype(BF16), (ks[u, d] * el).astype(BF16)
                else:
                    qa, ka = qs[u].astype(BF16), ks[u, d].astype(BF16)
                p = lax.dot_general(qa, ka, _NT, preferred_element_type=F32)
                acc[u] = acc[u] + p * m_ref[lvl]
        for u, d in units:
            _, _, _, qt_ref, ut_ref, d_ref, _, drow = dirs[d]
            e = es[u, d]
            qt_ref[rows[u], :] = (qs[u] * e[6 * c:7 * c]).astype(BF16)
            kt = (ks[u, d] * e[7 * c:8 * c]).astype(BF16)
            ut_ref[cis[u]] = lax.dot_general(vbs[u], kt, _TN, preferred_element_type=F32)
            d_ref[pl.ds(cis[u], 1), :] = e[6 * c + drow:6 * c + drow + 1]
        for u in range(HGRN_UNROLL):
            oi_ref[rows[u], :] = jnp.dot(acc[u].astype(BF16), vbs[u], preferred_element_type=F32)
        return carry

    lax.fori_loop(0, cpb // HGRN_UNROLL, chunk_group, 0)


def _hgrn_a(proj, lb, col, t, d_rec):
    heads = d_rec // REC_HEAD_DIM
    tq = 512
    cpb = tq // REC_CHUNK
    nchunks = t // REC_CHUNK
    wf, wb, mf, mb = _hgrn_consts()
    hd = REC_HEAD_DIM

    def colspec(off):
        return pl.BlockSpec((tq, hd), lambda i, h: (i, off + h))

    full2 = lambda i, h: (0, 0)
    full3 = lambda i, h: (0, 0, 0)
    row_spec = pl.BlockSpec((tq, hd), lambda i, h: (i, h))
    u_spec = pl.BlockSpec((cpb, hd, hd), lambda i, h: (i, 0, h))
    d_spec = pl.BlockSpec((cpb, hd), lambda i, h: (i, h))
    return pl.pallas_call(
        functools.partial(_hgrn_a_body, cpb=cpb),
        grid=(t // tq, heads),
        in_specs=[colspec(col["q_r"]), colspec(col["i_r"]), colspec(col["zf_f"]), colspec(col["zf_b"]),
                  pl.BlockSpec((2, hd), lambda i, h: (0, h)),
                  pl.BlockSpec(wf.shape, full2), pl.BlockSpec(wb.shape, full2),
                  pl.BlockSpec(mf.shape, full3), pl.BlockSpec(mb.shape, full3)],
        out_specs=[row_spec, row_spec, row_spec, u_spec, u_spec, d_spec, d_spec],
        out_shape=[jax.ShapeDtypeStruct((t, d_rec), F32),
                   jax.ShapeDtypeStruct((t, d_rec), BF16),
                   jax.ShapeDtypeStruct((t, d_rec), BF16),
                   jax.ShapeDtypeStruct((nchunks, hd, d_rec), F32),
                   jax.ShapeDtypeStruct((nchunks, hd, d_rec), F32),
                   jax.ShapeDtypeStruct((nchunks, d_rec), F32),
                   jax.ShapeDtypeStruct((nchunks, d_rec), F32)],
        compiler_params=_cparams(("arbitrary", "arbitrary"), 32),
        name="hgrn_a",
    )(proj, proj, proj, proj, lb, wf, wb, mf, mb)


def _hgrn_c_body(oi_ref, qtf_ref, qtb_ref, utf_ref, utb_ref, df_ref, db_ref, z_ref, g_ref,
                 out_ref, acc_ref, *, nchunks):
    c = REC_CHUNK
    hd = REC_HEAD_DIM

    def fwd(n, st):
        rows = pl.ds(pl.multiple_of(n * c, c), c)
        o = lax.dot_general(qtf_ref[rows, :], st.astype(BF16), _NT, preferred_element_type=F32)
        acc_ref[rows, :] = oi_ref[rows, :] + o
        return df_ref[pl.ds(n, 1), :] * st + utf_ref[n]

    lax.fori_loop(0, nchunks, fwd, jnp.zeros((hd, hd), F32), unroll=4)

    def bwd(i, st):
        n = nchunks - 1 - i
        rows = pl.ds(pl.multiple_of(n * c, c), c)
        o = lax.dot_general(qtb_ref[rows, :], st.astype(BF16), _NT, preferred_element_type=F32)
        acc_ref[rows, :] = acc_ref[rows, :] + o
        return db_ref[pl.ds(n, 1), :] * st + utb_ref[n]

    lax.fori_loop(0, nchunks, bwd, jnp.zeros((hd, hd), F32), unroll=4)

    o = _rms(acc_ref[...])
    out_ref[...] = (o * g_ref[...] * jax.nn.sigmoid(z_ref[...])).astype(BF16)


def _hgrn_c(oi, qtf, qtb, utf, utb, df, db, proj, g_out, col, bsz, seq, d_rec):
    heads = d_rec // REC_HEAD_DIM
    hd = REC_HEAD_DIM
    nchunks = seq // REC_CHUNK
    row_spec = pl.BlockSpec((seq, hd), lambda b, h: (b, h))
    u_spec = pl.BlockSpec((nchunks, hd, hd), lambda b, h: (b, 0, h))
    d_spec = pl.BlockSpec((nchunks, hd), lambda b, h: (b, h))
    zo = col["z_o"]
    return pl.pallas_call(
        functools.partial(_hgrn_c_body, nchunks=nchunks),
        grid=(bsz, heads),
        in_specs=[row_spec, row_spec, row_spec, u_spec, u_spec, d_spec, d_spec,
                  pl.BlockSpec((seq, hd), lambda b, h: (b, zo + h)),
                  pl.BlockSpec((1, hd), lambda b, h: (0, h))],
        out_specs=row_spec,
        out_shape=jax.ShapeDtypeStruct((bsz * seq, d_rec), BF16),
        scratch_shapes=[pltpu.VMEM((seq, hd), F32)],
        compiler_params=_cparams(("arbitrary", "arbitrary"), 48),
        name="hgrn_c",
    )(oi, qtf, qtb, utf, utb, df, db, proj, g_out.reshape(1, d_rec))


def _t5_bucket(rel):
    half_buckets = NUM_BUCKETS // 2
    ret = np.where(rel > 0, half_buckets, 0)
    n = np.abs(rel)
    max_exact = half_buckets // 2
    nf = np.maximum(n, 1).astype(np.float32)
    large = max_exact + (np.log(nf / np.float32(max_exact)) / np.float32(math.log(MAX_DISTANCE / max_exact))
                         * np.float32(half_buckets - max_exact)).astype(np.int32)
    large = np.minimum(large, half_buckets - 1)
    return ret + np.where(n < max_exact, n, large)


def _band_bias(rel_bias_g, window, dil):
    half = window // (2 * dil)
    q_off = np.arange(ATT_BLOCK)[:, None]
    rel = np.arange(3 * ATT_BLOCK)[None, :] - ATT_BLOCK - q_off
    onehot = (_t5_bucket(rel * dil)[..., None] == np.arange(NUM_BUCKETS)).astype(np.float32)
    bias = jnp.einsum("qkb,bh->hqk", jnp.asarray(onehot), rel_bias_g.astype(F32),
                      precision=lax.Precision.HIGHEST)
    return jnp.where(jnp.asarray(np.abs(rel) <= half)[None], bias, NEG_INF)


def _attn_body(q_ref, kp_ref, k_ref, kn_ref, vp_ref, v_ref, vn_ref, bias_ref,
               num_ref, st_ref, kc_ref, vc_ref, *, dil, tq, sub_len):
    blk = ATT_BLOCK
    nqb = tq // blk
    n = pl.program_id(1)
    scale = ATT_HEAD_DIM ** -0.5

    def sds(start, size):
        if dil == 1:
            return pl.ds(start, size)
        return pl.ds(start, size, stride=dil)

    cu = kc_ref.shape[0]
    qu = ATT_UNROLL // cu

    def deinterleave(r, j):
        kc_ref[j, 0:blk, :] = kp_ref[sds(r, blk), :].astype(BF16)
        kc_ref[j, blk:blk + tq, :] = k_ref[sds(r, tq), :].astype(BF16)
        kc_ref[j, blk + tq:2 * blk + tq, :] = kn_ref[sds(r, blk), :].astype(BF16)
        vc_ref[j, 0:blk, :] = vp_ref[sds(r, blk), :].astype(BF16)
        vc_ref[j, blk:blk + tq, :] = v_ref[sds(r, tq), :].astype(BF16)
        vc_ref[j, blk + tq:2 * blk + tq, :] = vn_ref[sds(r, blk), :].astype(BF16)

    def units(r0, qb0):
        us = [(j, u) for j in range(cu) for u in range(qu)]
        q0s = [pl.multiple_of((qb0 + u) * blk, blk) for _, u in us]
        rows = [sds(r0 + j + dil * q0, blk) for (j, _), q0 in zip(us, q0s)]
        lane = lax.broadcasted_iota(jnp.int32, (blk, LANE), 1)
        key_iota = lax.broadcasted_iota(jnp.int32, (1, 3 * blk), 1)
        bias = bias_ref[0]
        qs = [q_ref[rw, :].astype(BF16) for rw in rows]
        kws = [kc_ref[j, pl.ds(q0, 3 * blk), :] for (j, _), q0 in zip(us, q0s)]
        vws = [vc_ref[j, pl.ds(q0, 3 * blk), :] for (j, _), q0 in zip(us, q0s)]
        ss = [lax.dot_general(q, kw, _NT, preferred_element_type=F32) * scale for q, kw in zip(qs, kws)]
        valids = []
        for q0 in q0s:
            kpos = n * tq + q0 - blk + key_iota
            valids.append((kpos >= 0) & (kpos < sub_len))
        ss = [jnp.where(valid, s + bias, NEG_INF) for s, valid in zip(ss, valids)]
        ms = [jnp.max(s, axis=-1, keepdims=True) for s in ss]
        ps = [jnp.exp(s - m) for s, m in zip(ss, ms)]
        ls = [jnp.sum(p, axis=-1, keepdims=True) for p in ps]
        nums = [jnp.dot(p.astype(BF16), vw, preferred_element_type=F32) for p, vw in zip(ps, vws)]
        for rw, num, m, l in zip(rows, nums, ms, ls):
            num_ref[rw, :] = num
            st_ref[rw, :] = jnp.where(lane < LANE // 2, m, l)

    def class_group(rg, carry):
        r0 = rg * cu
        for j in range(cu):
            deinterleave(r0 + j, j)

        def qgroup(qg, carry2):
            units(r0, qg * qu)
            return carry2

        lax.fori_loop(0, nqb // qu, qgroup, 0)
        return carry

    lax.fori_loop(0, dil // cu, class_group, 0)


def _attn_group(proj, bias, col, g, dil, bsz, seq):
    tile = 1024
    tq = tile // dil
    halo = ATT_BLOCK * dil
    sub_len = seq // dil
    cu = ATT_UNROLL // min(tq // ATT_BLOCK, ATT_UNROLL)
    nh = ATT_HEADS_PER_GROUP
    hd = ATT_HEAD_DIM
    qc = col["q_a"] + g * nh
    kc = col["k_a"] + g * nh
    vc = col["v_a"] + g * nh
    tiles_b = seq // tile
    halos_b = seq // halo
    hpt = tile // halo

    own = lambda c: pl.BlockSpec((tile, hd), lambda b, n, h: (b * tiles_b + n, c + h))
    prev = lambda c: pl.BlockSpec(
        (halo, hd), lambda b, n, h: (b * halos_b + jnp.maximum(n * hpt - 1, 0), c + h))
    nxt = lambda c: pl.BlockSpec(
        (halo, hd), lambda b, n, h: (b * halos_b + jnp.minimum((n + 1) * hpt, halos_b - 1), c + h))
    t = bsz * seq
    return pl.pallas_call(
        functools.partial(_attn_body, dil=dil, tq=tq, sub_len=sub_len),
        grid=(bsz, tiles_b, nh),
        in_specs=[own(qc), prev(kc), own(kc), nxt(kc), prev(vc), own(vc), nxt(vc),
                  pl.BlockSpec((1,) + bias.shape[1:], lambda b, n, h: (h, 0, 0))],
        out_specs=[pl.BlockSpec((tile, hd), lambda b, n, h: (b * tiles_b + n, h)),
                   pl.BlockSpec((tile, LANE), lambda b, n, h: (b * tiles_b + n, h))],
        out_shape=[jax.ShapeDtypeStruct((t, nh * hd), F32), jax.ShapeDtypeStruct((t, nh * LANE), F32)],
        scratch_shapes=[pltpu.VMEM((cu, tq + 2 * ATT_BLOCK, hd), BF16),
                        pltpu.VMEM((cu, tq + 2 * ATT_BLOCK, hd), BF16)],
        compiler_params=_cparams(("arbitrary", "arbitrary", "arbitrary"), 32),
        name=f"attn_d{dil}",
    )(proj, proj, proj, proj, proj, proj, proj, bias)


def _merge_body(rec_ref, n0_ref, n1_ref, n2_ref, s0_ref, s1_ref, s2_ref, zgr_ref, zga_ref, x_ref,
                ada_ref, gpost_ref, gpre_ref, wbr_ref, wba_ref, wo_ref, wr_ref, br_ref,
                x1_ref, h2_ref, lg_ref):
    nh = ATT_HEADS_PER_GROUP
    hd = ATT_HEAD_DIM
    half = LANE // 2
    lane = lax.broadcasted_iota(jnp.int32, (rec_ref.shape[0], LANE), 1)
    heads = []
    for h in range(nh):
        cols = slice(h * hd, (h + 1) * hd)
        st = [s[:, cols] for s in (s0_ref, s1_ref, s2_ref)]
        top = jnp.maximum(jnp.maximum(st[0], st[1]), st[2])
        ws = [jnp.exp(s - top) for s in st]
        den = (ws[0] * pltpu.roll(st[0], half, 1) + ws[1] * pltpu.roll(st[1], half, 1)
               + ws[2] * pltpu.roll(st[2], half, 1))
        coef = [w / den for w in ws]
        coef = [jnp.where(lane < half, c, pltpu.roll(c, half, 1)) for c in coef]
        num = coef[0] * n0_ref[:, cols] + coef[1] * n1_ref[:, cols] + coef[2] * n2_ref[:, cols]
        heads.append(num.astype(BF16))
    att = jnp.concatenate(heads, axis=1)
    y_rec = jnp.dot(rec_ref[...], wbr_ref[...], preferred_element_type=F32)
    y_att = jnp.dot(att, wba_ref[...], preferred_element_type=F32)
    merged = jax.nn.sigmoid(zgr_ref[...]) * y_rec + jax.nn.sigmoid(zga_ref[...]) * y_att
    y = jnp.dot(merged.astype(BF16), wo_ref[...], preferred_element_type=F32)
    gt_m = ada_ref[0, 2:3, :]
    sh_f = ada_ref[0, 3:4, :]
    sc_f = ada_ref[0, 4:5, :]
    x1 = x_ref[...] + gt_m * (_rms(y) * gpost_ref[...])
    x1_ref[...] = x1
    h2 = _rms(x1) * gpre_ref[...] * (1.0 + sc_f) + sh_f
    tm = h2.shape[0]
    for s in range(SLAB_ROWS):
        h2_ref[pl.ds(s, tm, stride=SLAB_ROWS), :] = h2[:, s * LANE:(s + 1) * LANE]
    lg_ref[...] = jnp.dot(h2, wr_ref[...], preferred_element_type=F32,
                          precision=lax.Precision.HIGHEST) + br_ref[...]


def _merge(rec_o, nums, stats, proj, x2, ada3, g_post, g_pre, wbr, wba, wo, w_router, b_router, col, seq):
    t, d = x2.shape
    tm = 256
    per_b = seq // tm
    d_rec = rec_o.shape[1]
    w_att = nums[0].shape[1]
    ne = w_router.shape[1]
    dl = d // LANE
    row = lambda w: pl.BlockSpec((tm, w), lambda i: (i, 0))
    const = lambda shape: pl.BlockSpec(shape, lambda i: (0,) * len(shape), pipeline_mode=pl.Buffered(1))
    zgr = col["zg_rec"] // dl
    zga = col["zg_att"] // dl
    return pl.pallas_call(
        _merge_body,
        grid=(t // tm,),
        in_specs=[row(d_rec), row(w_att), row(w_att), row(w_att), row(w_att), row(w_att), row(w_att),
                  pl.BlockSpec((tm, d), lambda i: (i, zgr)),
                  pl.BlockSpec((tm, d), lambda i: (i, zga)),
                  row(d),
                  pl.BlockSpec((1, 6, d), lambda i: (i // per_b, 0, 0)),
                  const((1, d)), const((1, d)),
                  const(wbr.shape), const(wba.shape), const(wo.shape), const(w_router.shape),
                  const((1, ne))],
        out_specs=[row(d), pl.BlockSpec((tm * SLAB_ROWS, LANE), lambda i: (i, 0)),
                   pl.BlockSpec((tm, ne), lambda i: (i, 0))],
        out_shape=[jax.ShapeDtypeStruct((t, d), F32), jax.ShapeDtypeStruct((t * SLAB_ROWS, LANE), F32),
                   jax.ShapeDtypeStruct((t, ne), F32)],
        compiler_params=_cparams(("arbitrary",), 56),
        name="merge",
    )(rec_o, nums[0], nums[1], nums[2], stats[0], stats[1], stats[2], proj, proj, x2, ada3,
      g_post.reshape(1, d), g_pre.reshape(1, d), wbr, wba, wo, w_router, b_router.reshape(1, ne))


def _route_body(lg_ref, tri_ref, rt_ref, cnt_ref, carry_ref):
    i = pl.program_id(0)
    tr, ne = lg_ref.shape

    @pl.when(i == 0)
    def _():
        carry_ref[...] = jnp.zeros_like(carry_ref)

    l = lg_ref[...]
    lane = lax.broadcasted_iota(jnp.int32, (tr, ne), 1).astype(F32)
    vals, sels, idxs = [], [], []
    for _ in range(TOP_K):
        m = jnp.max(l, axis=-1, keepdims=True)
        idx = jnp.min(jnp.where(l == m, lane, float(ne)), axis=-1, keepdims=True)
        sel = lane == idx
        vals.append(m)
        idxs.append(idx)
        sels.append(sel)
        l = jnp.where(sel, -jnp.inf, l)
    es = [jnp.exp(v - vals[0]) for v in vals]
    tot = es[0] + es[1] + es[2] + es[3]
    chosen = (sels[0] | sels[1] | sels[2] | sels[3]).astype(F32)
    prefix = jnp.dot(tri_ref[...], chosen.astype(BF16), preferred_element_type=F32) + carry_ref[0:1, :]
    out_lane = lax.broadcasted_iota(jnp.int32, (tr, LANE), 1)
    rt = jnp.zeros((tr, LANE), F32)
    for k in range(TOP_K):
        rank = jnp.sum(jnp.where(sels[k], prefix, 0.0), axis=-1, keepdims=True)
        rt = jnp.where(out_lane == k, idxs[k], rt)
        rt = jnp.where(out_lane == TOP_K + k, es[k] / tot, rt)
        rt = jnp.where(out_lane == 2 * TOP_K + k, rank, rt)
    rt_ref[...] = rt
    new = carry_ref[0:1, :] + jnp.sum(chosen, axis=0, keepdims=True)
    carry_ref[...] = jnp.broadcast_to(new, carry_ref.shape)
    cnt_ref[...] = carry_ref[...]


def _route(logits):
    t, ne = logits.shape
    tr = 512
    tri = jnp.asarray(np.tril(np.ones((tr, tr), np.float32), -1), BF16)
    return pl.pallas_call(
        _route_body,
        grid=(t // tr,),
        in_specs=[pl.BlockSpec((tr, ne), lambda i: (i, 0)),
                  pl.BlockSpec((tr, tr), lambda i: (0, 0))],
        out_specs=[pl.BlockSpec((tr, LANE), lambda i: (i, 0)),
                   pl.BlockSpec((SUBLANE, ne), lambda i: (0, 0))],
        out_shape=[jax.ShapeDtypeStruct((t, LANE), F32), jax.ShapeDtypeStruct((SUBLANE, ne), F32)],
        scratch_shapes=[pltpu.VMEM((SUBLANE, ne), F32)],
        compiler_params=_cparams(("arbitrary",), 32),
        name="route",
    )(logits, tri)


def _moe_up_body(be_ref, nu_ref, idx0_ref, idxn_ref, h2s_ref, w_ref, bias_ref, o_ref, xbuf, sem):
    del be_ref
    b = pl.program_id(0)
    nu = nu_ref[0]
    f = o_ref.shape[1]
    slab = SLAB_ROWS

    def issue(idx_ref, slot):
        def body(r, c):
            tok = idx_ref[0, 0, r]
            pltpu.make_async_copy(h2s_ref.at[pl.ds(pl.multiple_of(tok * slab, slab), slab), :],
                                  xbuf.at[slot, pl.ds(pl.multiple_of(r * slab, slab), slab), :],
                                  sem.at[slot]).start()
            return c
        lax.fori_loop(0, BM, body, 0, unroll=DMA_UNROLL)

    @pl.when(b == 0)
    def _():
        issue(idx0_ref, 0)

    @pl.when(b + 1 < nu)
    def _():
        issue(idxn_ref, (b + 1) % 2)

    @pl.when(b < nu)
    def _():
        slot = b % 2
        pltpu.make_async_copy(h2s_ref.at[pl.ds(0, BM * slab), :], xbuf.at[slot], sem.at[slot]).wait()
        x = jnp.concatenate([xbuf[slot, pl.ds(s, BM, stride=slab), :].astype(BF16) for s in range(slab)],
                            axis=1)
        half = f // 2
        for c0 in (0, half):
            gate = (jnp.dot(x, w_ref[0, :, c0:c0 + half], preferred_element_type=F32)
                    + bias_ref[0, :, c0:c0 + half])
            up = (jnp.dot(x, w_ref[0, :, f + c0:f + c0 + half], preferred_element_type=F32)
                  + bias_ref[0, :, f + c0:f + c0 + half])
            gate = jnp.minimum(gate, SWIGLU_LIMIT)
            up = jnp.clip(up, -SWIGLU_LIMIT, SWIGLU_LIMIT)
            o_ref[:, c0:c0 + half] = (gate * jax.nn.sigmoid(SWIGLU_ALPHA * gate) * (up + 1.0)).astype(BF16)

    @pl.when(b >= nu)
    def _():
        o_ref[...] = jnp.zeros_like(o_ref)


def _moe_up(h2s, src_tok, wgu_bf, bgu, blk_e, n_used):
    ne, d, f2 = wgu_bf.shape
    f = f2 // 2
    p = src_tok.shape[0]
    nb = p // BM
    idx3 = src_tok.reshape(nb, 1, BM)
    grid_spec = pltpu.PrefetchScalarGridSpec(
        num_scalar_prefetch=2,
        grid=(nb,),
        in_specs=[pl.BlockSpec((1, 1, BM), lambda b, be, nu: (0, 0, 0), memory_space=pltpu.SMEM),
                  pl.BlockSpec((1, 1, BM), lambda b, be, nu: (jnp.minimum(b + 1, nb - 1), 0, 0),
                               memory_space=pltpu.SMEM),
                  pl.BlockSpec(memory_space=pl.ANY),
                  pl.BlockSpec((1, d, f2), lambda b, be, nu: (be[b], 0, 0)),
                  pl.BlockSpec((1, 1, f2), lambda b, be, nu: (be[b], 0, 0))],
        out_specs=pl.BlockSpec((BM, f), lambda b, be, nu: (b, 0)),
        scratch_shapes=[pltpu.VMEM((2, BM * SLAB_ROWS, LANE), F32), pltpu.SemaphoreType.DMA((2,))],
    )
    return pl.pallas_call(
        _moe_up_body,
        grid_spec=grid_spec,
        out_shape=jax.ShapeDtypeStruct((p, f), BF16),
        compiler_params=_cparams(("arbitrary",), 56),
        name="moe_up",
    )(blk_e, n_used, idx3, idx3, h2s, wgu_bf, bgu.reshape(ne, 1, f2))


def _moe_down_body(be_ref, nu_ref, nv_ref, dst_ref, a_ref, w_ref, bias_ref, ysc_ref, ybuf, sem):
    del be_ref
    b = pl.program_id(0)
    nb = pl.num_programs(0)
    nu = nu_ref[0]
    slab = SLAB_ROWS

    def row_copy(slot, r, d):
        return pltpu.make_async_copy(ybuf.at[slot, pl.ds(pl.multiple_of(r * slab, slab), slab), :],
                                     ysc_ref.at[pl.ds(pl.multiple_of(d * slab, slab), slab), :],
                                     sem.at[slot])

    def drain(step):
        slot = step % 2
        count = nv_ref[step]

        @pl.when(count == BM)
        def _():
            pltpu.make_async_copy(ybuf.at[slot], ysc_ref.at[pl.ds(0, BM * slab), :], sem.at[slot]).wait()

        @pl.when(count < BM)
        def _():
            def body(r, c):
                row_copy(slot, 0, 0).wait()
                return c
            lax.fori_loop(0, count, body, 0)

    @pl.when((b >= 2) & (b < nu))
    def _():
        drain(b - 2)

    @pl.when(b < nu)
    def _():
        slot = b % 2
        y = jnp.dot(a_ref[...], w_ref[0], preferred_element_type=F32) + bias_ref[0]
        for s in range(slab):
            ybuf[slot, pl.ds(s, BM, stride=slab), :] = y[:, s * LANE:(s + 1) * LANE]

        def body(r, c):
            row_copy(slot, r, dst_ref[0, 0, r]).start()
            return c

        @pl.when(nv_ref[b] == BM)
        def _():
            lax.fori_loop(0, BM, body, 0, unroll=DMA_UNROLL)

        @pl.when(nv_ref[b] < BM)
        def _():
            lax.fori_loop(0, nv_ref[b], body, 0)

    @pl.when(b == nb - 1)
    def _():
        @pl.when(nu >= 2)
        def _():
            drain(nu - 2)
        drain(nu - 1)


def _moe_down(act, dst_slot, nvalid, n_out_rows, wd_bf, bd, blk_e, n_used):
    p, f = act.shape
    ne, _, d = wd_bf.shape
    nb = p // BM
    grid_spec = pltpu.PrefetchScalarGridSpec(
        num_scalar_prefetch=3,
        grid=(nb,),
        in_specs=[pl.BlockSpec((1, 1, BM), lambda b, be, nu, nv: (b, 0, 0), memory_space=pltpu.SMEM),
                  pl.BlockSpec((BM, f), lambda b, be, nu, nv: (b, 0)),
                  pl.BlockSpec((1, f, d), lambda b, be, nu, nv: (be[b], 0, 0)),
                  pl.BlockSpec((1, 1, d), lambda b, be, nu, nv: (be[b], 0, 0))],
        out_specs=pl.BlockSpec(memory_space=pl.ANY),
        scratch_shapes=[pltpu.VMEM((2, BM * SLAB_ROWS, LANE), F32), pltpu.SemaphoreType.DMA((2,))],
    )
    return pl.pallas_call(
        _moe_down_body,
        grid_spec=grid_spec,
        out_shape=jax.ShapeDtypeStruct((n_out_rows * SLAB_ROWS, LANE), F32),
        compiler_params=pltpu.CompilerParams(dimension_semantics=("arbitrary",),
                                             vmem_limit_bytes=48 * 1024 * 1024, has_side_effects=True),
        name="moe_down",
    )(blk_e, n_used, nvalid, dst_slot.reshape(nb, 1, BM), act, wd_bf, bd.reshape(ne, 1, d))


def _final_body(ysc_ref, rt_ref, x1_ref, ada_ref, g_ref, o_ref):
    tm = x1_ref.shape[0]
    rt = rt_ref[...]
    stride = TOP_K * SLAB_ROWS
    pieces = []
    for s in range(SLAB_ROWS):
        acc = rt[:, TOP_K:TOP_K + 1] * ysc_ref[pl.ds(s, tm, stride=stride), :]
        for k in range(1, TOP_K):
            acc = acc + rt[:, TOP_K + k:TOP_K + k + 1] * ysc_ref[pl.ds(k * SLAB_ROWS + s, tm, stride=stride), :]
        pieces.append(acc)
    y = jnp.concatenate(pieces, axis=1)
    gt_f = ada_ref[0, 5:6, :]
    o_ref[...] = x1_ref[...] + gt_f * (_rms(y) * g_ref[...])


def _final(ysc, rt, x1, ada3, g_post, seq):
    t, d = x1.shape
    tm = 256
    per_b = seq // tm
    return pl.pallas_call(
        _final_body,
        grid=(t // tm,),
        in_specs=[pl.BlockSpec((tm * TOP_K * SLAB_ROWS, LANE), lambda i: (i, 0)),
                  pl.BlockSpec((tm, LANE), lambda i: (i, 0)),
                  pl.BlockSpec((tm, d), lambda i: (i, 0)),
                  pl.BlockSpec((1, 6, d), lambda i: (i // per_b, 0, 0)),
                  pl.BlockSpec((1, d), lambda i: (0, 0))],
        out_specs=pl.BlockSpec((tm, d), lambda i: (i, 0)),
        out_shape=jax.ShapeDtypeStruct((t, d), F32),
        compiler_params=_cparams(("arbitrary",), 48),
        name="final",
    )(ysc, rt, x1, ada3, g_post.reshape(1, d))


def _mixer_ffn_layer(x2, ada3, bsz, seq, g_mix_pre, g_mix_post, g_ffn_pre, g_ffn_post, w_in, lb, g_rec_out,
                     rel_bias, w_branch_rec, w_branch_att, w_o, w_router, b_router, w_gate_up, b_gate_up,
                     w_down, b_down):
    t, d = x2.shape
    d_rec = w_branch_rec.shape[0]
    w_att = w_branch_att.shape[0]
    d_att = 3 * w_att
    widths = dict(q_r=d_rec, i_r=d_rec, zf_f=d_rec, zf_b=d_rec, z_o=d_rec, q_a=d_att, k_a=d_att, v_a=d_att,
                  zg_rec=d, zg_att=d)
    ref_order = ("q_r", "i_r", "zf_f", "zf_b", "z_o", "q_a", "k_a", "v_a", "zg_rec", "zg_att")
    my_order = ("zg_rec", "zg_att", "q_r", "i_r", "zf_f", "zf_b", "z_o", "q_a", "k_a", "v_a")
    ref_off, acc = {}, 0
    for name in ref_order:
        ref_off[name] = acc
        acc += widths[name]
    col, acc = {}, 0
    for name in my_order:
        col[name] = acc // LANE
        acc += widths[name]
    w_in_bf = jnp.concatenate([w_in[:, ref_off[nm]:ref_off[nm] + widths[nm]].astype(BF16) for nm in my_order],
                              axis=1)

    proj = _inproj(x2, g_mix_pre, ada3, w_in_bf, seq)

    oi, qtf, qtb, utf, utb, df, db = _hgrn_a(proj, lb, col, t, d_rec)
    rec_o = _hgrn_c(oi, qtf, qtb, utf, utb, df, db, proj, g_rec_out, col, bsz, seq, d_rec)

    nums, stats = [], []
    for g, (window, dil) in enumerate(DIL_GROUPS):
        hs = slice(g * ATT_HEADS_PER_GROUP, (g + 1) * ATT_HEADS_PER_GROUP)
        bias = _band_bias(rel_bias[:, hs], window, dil)
        num, st = _attn_group(proj, bias, col, g, dil, bsz, seq)
        nums.append(num)
        stats.append(st)

    x1, h2s, logits = _merge(rec_o, nums, stats, proj, x2, ada3, g_mix_post, g_ffn_pre,
                            w_branch_rec.astype(BF16), w_branch_att.astype(BF16), w_o.astype(BF16),
                            w_router, b_router, col, seq)

    rt, cnt = _route(logits)
    ne = logits.shape[1]
    counts = cnt[0].astype(jnp.int32)
    top_idx = rt[:, 0:TOP_K].astype(jnp.int32)
    rank = rt[:, 2 * TOP_K:3 * TOP_K].astype(jnp.int32)
    padded = (counts + BM - 1) // BM * BM
    pends = jnp.cumsum(padded)
    pstarts = pends - padded
    experts = jnp.arange(ne, dtype=jnp.int32)
    pstart_sel = jnp.sum(jnp.where(top_idx[..., None] == experts, pstarts, 0), axis=-1)
    dest = (pstart_sel + rank).reshape(-1)
    p_rows = t * TOP_K + ne * BM
    nb = p_rows // BM
    blk_start = jnp.arange(nb, dtype=jnp.int32) * BM
    blk_e = jnp.minimum(jnp.sum((pends[None, :] <= blk_start[:, None]).astype(jnp.int32), axis=1), ne - 1)
    n_used = (pends[-1:] // BM).astype(jnp.int32)

    n_assign = t * TOP_K
    slot_assign = jnp.full((p_rows,), -1, jnp.int32).at[dest].set(jnp.arange(n_assign, dtype=jnp.int32))
    src_tok = jnp.maximum(slot_assign, 0) // TOP_K
    valid_end = pstarts + counts
    blk_end = jnp.sum(jnp.where(blk_e[:, None] == experts, valid_end, 0), axis=-1)
    nvalid = jnp.clip(blk_end - blk_start, 0, BM).astype(jnp.int32)
    nvalid = jnp.where(jnp.arange(nb) < n_used[0], nvalid, 0)

    act = _moe_up(h2s, src_tok, w_gate_up.astype(BF16), b_gate_up, blk_e, n_used)
    ysc = _moe_down(act, slot_assign, nvalid, n_assign, w_down.astype(BF16), b_down, blk_e, n_used)
    return _final(ysc, rt, x1, ada3, g_ffn_post, seq)


def kernel(x, c, w_ada, b_ada, g_mix_pre, g_mix_post, g_ffn_pre, g_ffn_post, w_in, g_rec_out, w_branch_rec,
           w_branch_att, w_o, w_router, b_router, w_gate_up, b_gate_up, w_down, b_down, rec_lb_table, rel_bias):
    bsz, seq, d = x.shape
    depth = w_in.shape[0]
    lb_all = jnp.cumsum(jax.nn.softmax(rec_lb_table.astype(F32), axis=1), axis=1)
    x2 = x.reshape(bsz * seq, d)
    for layer in range(depth):
        ada3 = _ada(c, w_ada[layer], b_ada[layer]).reshape(bsz, 6, d)
        x2 = _mixer_ffn_layer(x2, ada3, bsz, seq, g_mix_pre[layer], g_mix_post[layer], g_ffn_pre[layer],
                              g_ffn_post[layer], w_in[layer], lb_all[:, layer], g_rec_out[layer], rel_bias,
                              w_branch_rec[layer], w_branch_att[layer], w_o[layer], w_router[layer],
                              b_router[layer], w_gate_up[layer], b_gate_up[layer], w_down[layer],
                              b_down[layer])
    return x2.reshape(bsz, seq, d)
```

```python
import functools
import math

import numpy as np
import jax
import jax.numpy as jnp
from jax import lax
from jax.experimental import pallas as pl
from jax.experimental.pallas import tpu as pltpu

F32 = jnp.float32
BF16 = jnp.bfloat16

LANE = 128
SUBLANE = 8
SLAB_ROWS = 16

REC_HEAD_DIM = 128
REC_CHUNK = 64
ATT_HEAD_DIM = 128
ATT_HEADS_PER_GROUP = 4
ATT_BLOCK = 64
DIL_GROUPS = ((128, 1), (512, 4), (2048, 16))
NUM_BUCKETS = 32
MAX_DISTANCE = 1024
N_EXPERTS = 32
TOP_K = 4
SWIGLU_LIMIT = 7.0
SWIGLU_ALPHA = 1.702
RMS_EPS = 1e-6
NEG_INF = -1e30

N_LEVELS = 6
W_CHUNK = 128
CAST_ROWS = 32
HGRN_UNROLL = 4
DMA_UNROLL = 8
ATT_UNROLL = 8
BM = 256

_NT = (((1,), (1,)), ((), ()))
_TN = (((0,), (0,)), ((), ()))


def _cparams(sem, vmem_mb):
    return pltpu.CompilerParams(dimension_semantics=sem, vmem_limit_bytes=vmem_mb * 1024 * 1024)


def _rms(x):
    return x * lax.rsqrt(jnp.mean(x * x, axis=-1, keepdims=True) + RMS_EPS)


def _ada_body(c_ref, w_ref, b_ref, o_ref):
    c = c_ref[...]
    cond = (c * jax.nn.sigmoid(c)).astype(BF16)
    o_ref[...] = jnp.dot(cond, w_ref[...].astype(BF16), preferred_element_type=F32) + b_ref[...]


def _ada(c, w, b):
    bsz, d = c.shape
    n = w.shape[1]
    tn = 1024
    cp = jnp.zeros((SUBLANE, d), F32).at[:bsz].set(c)
    out = pl.pallas_call(
        _ada_body,
        grid=(n // tn,),
        in_specs=[pl.BlockSpec((SUBLANE, d), lambda j: (0, 0)),
                  pl.BlockSpec((d, tn), lambda j: (0, j)),
                  pl.BlockSpec((1, tn), lambda j: (0, j))],
        out_specs=pl.BlockSpec((SUBLANE, tn), lambda j: (0, j)),
        out_shape=jax.ShapeDtypeStruct((SUBLANE, n), F32),
        compiler_params=_cparams(("arbitrary",), 40),
        name="ada",
    )(cp, w, b.reshape(1, n))
    return out[:bsz]


def _inproj_body(x_ref, g_ref, ada_ref, w_ref, o_ref, h_ref):
    @pl.when(pl.program_id(1) == 0)
    def _():
        y = _rms(x_ref[...]) * g_ref[...]
        sh = ada_ref[0, 0:1, :]
        sc = ada_ref[0, 1:2, :]
        h_ref[...] = (y * (1.0 + sc) + sh).astype(BF16)

    o_ref[...] = jnp.dot(h_ref[...], w_ref[...], preferred_element_type=F32)


def _inproj(x2, g, ada3, w_bf, seq):
    t, d = x2.shape
    n = w_bf.shape[1]
    tm, tn = 1024, 768
    per_b = seq // tm
    return pl.pallas_call(
        _inproj_body,
        grid=(t // tm, n // tn),
        in_specs=[pl.BlockSpec((tm, d), lambda i, j: (i, 0)),
                  pl.BlockSpec((1, d), lambda i, j: (0, 0)),
                  pl.BlockSpec((1, 6, d), lambda i, j: (i // per_b, 0, 0)),
                  pl.BlockSpec((d, tn), lambda i, j: (0, j))],
        out_specs=pl.BlockSpec((tm, tn), lambda i, j: (i, j)),
        out_shape=jax.ShapeDtypeStruct((t, n), F32),
        scratch_shapes=[pltpu.VMEM((tm, d), BF16)],
        compiler_params=_cparams(("arbitrary", "arbitrary"), 48),
        name="inproj",
    )(x2, g.reshape(1, d), ada3, w_bf)


def _hgrn_consts():
    c = REC_CHUNK
    r = np.arange(c)[:, None]
    m = np.arange(c)[None, :]
    wf = np.zeros((8 * c, c), np.float32)
    wb = np.zeros((8 * c, c), np.float32)
    mf = np.zeros((N_LEVELS + 1, c, c), np.float32)
    for lvl in range(N_LEVELS):
        s = 32 >> lvl
        m0 = (r // (2 * s)) * (2 * s) + s
        up = r >= m0
        wf[lvl * c:(lvl + 1) * c] = np.where(up, (m >= m0) & (m <= r), (m > r) & (m <= m0 - 1))
        wb[lvl * c:(lvl + 1) * c] = np.where(up, (m >= m0) & (m <= r - 1), (m >= r) & (m <= m0 - 1))
        i = np.arange(c)[:, None]
        j = np.arange(c)[None, :]
        mf[lvl] = (i // (2 * s) == j // (2 * s)) & (i % (2 * s) >= s) & (j % (2 * s) < s)
    mf[N_LEVELS] = np.eye(c)
    wf[6 * c:7 * c] = m <= r
    wf[7 * c:8 * c] = m > r
    wb[6 * c:7 * c] = m >= r
    wb[7 * c:8 * c] = m < r
    mb = np.transpose(mf, (0, 2, 1)).copy()
    wf3 = np.concatenate([wf, wf, wf], axis=1)
    wb3 = np.concatenate([wb, wb, wb], axis=1)
    return (jnp.asarray(wf3, BF16), jnp.asarray(wb3, BF16), jnp.asarray(mf, F32), jnp.asarray(mb, F32))


def _split3(g):
    hi = g.astype(BF16)
    r1 = g - hi.astype(F32)
    mid = r1.astype(BF16)
    lo = (r1 - mid.astype(F32)).astype(BF16)
    return jnp.concatenate([hi, mid, lo], axis=0)


def _hgrn_a_body(q_ref, i_ref, zf_ref, zb_ref, lb_ref, wf_ref, wb_ref, mf_ref, mb_ref,
                 oi_ref, qtf_ref, qtb_ref, utf_ref, utb_ref, df_ref, db_ref, *, cpb):
    c = REC_CHUNK
    dirs = ((zf_ref, wf_ref, mf_ref, qtf_ref, utf_ref, df_ref, 0, c - 1),
            (zb_ref, wb_ref, mb_ref, qtb_ref, utb_ref, db_ref, 1, 0))

    def chunk_group(cg, carry):
        cis = [cg * HGRN_UNROLL + u for u in range(HGRN_UNROLL)]
        rows = [pl.ds(pl.multiple_of(ci * c, c), c) for ci in cis]
        zqs = [q_ref[rw, :] for rw in rows]
        qs = [zq * jax.nn.sigmoid(zq) for zq in zqs]
        vbs = [i_ref[rw, :].astype(BF16) for rw in rows]
        units = [(u, d) for u in range(HGRN_UNROLL) for d in range(2)]
        ks, es = {}, {}
        for u, d in units:
            z_ref, w_ref = dirs[d][0], dirs[d][1]
            lb = lb_ref[d:d + 1, :]
            f = lb + (1.0 - lb) * jax.nn.sigmoid(z_ref[rows[u], :])
            ks[u, d] = 1.0 - f
            es[u, d] = jnp.exp(jnp.dot(w_ref[...], _split3(jnp.log(f)), preferred_element_type=F32))
        acc = [jnp.zeros((c, c), F32) for _ in range(HGRN_UNROLL)]
        for lvl in range(N_LEVELS + 1):
            for u, d in units:
                m_ref = dirs[d][2]
                if lvl < N_LEVELS:
                    el = es[u, d][lvl * c:(lvl + 1) * c]
                    qa, ka = (qs[u] * el).astype(BF16), (ks[u, d] * el).astype(BF16)
                else:
                    qa, ka = qs[u].astype(BF16), ks[u, d].astype(BF16)
                p = lax.dot_general(qa, ka, _NT, preferred_element_type=F32)
                acc[u] = acc[u] + p * m_ref[lvl]
        for u, d in units:
            _, _, _, qt_ref, ut_ref, d_ref, _, drow = dirs[d]
            e = es[u, d]
            qt_ref[rows[u], :] = (qs[u] * e[6 * c:7 * c]).astype(BF16)
            kt = (ks[u, d] * e[7 * c:8 * c]).astype(BF16)
            ut_ref[cis[u]] = lax.dot_general(vbs[u], kt, _TN, preferred_element_type=F32)
            d_ref[pl.ds(cis[u], 1), :] = e[6 * c + drow:6 * c + drow + 1]
        for u in range(HGRN_UNROLL):
            oi_ref[rows[u], :] = jnp.dot(acc[u].astype(BF16), vbs[u], preferred_element_type=F32)
        return carry

    lax.fori_loop(0, cpb // HGRN_UNROLL, chunk_group, 0)


def _hgrn_a(proj, lb, col, t, d_rec):
    heads = d_rec // REC_HEAD_DIM
    tq = 512
    cpb = tq // REC_CHUNK
    nchunks = t // REC_CHUNK
    wf, wb, mf, mb = _hgrn_consts()
    hd = REC_HEAD_DIM

    def colspec(off):
        return pl.BlockSpec((tq, hd), lambda i, h: (i, off + h))

    full2 = lambda i, h: (0, 0)
    full3 = lambda i, h: (0, 0, 0)
    row_spec = pl.BlockSpec((tq, hd), lambda i, h: (i, h))
    u_spec = pl.BlockSpec((cpb, hd, hd), lambda i, h: (i, 0, h))
    d_spec = pl.BlockSpec((cpb, hd), lambda i, h: (i, h))
    return pl.pallas_call(
        functools.partial(_hgrn_a_body, cpb=cpb),
        grid=(t // tq, heads),
        in_specs=[colspec(col["q_r"]), colspec(col["i_r"]), colspec(col["zf_f"]), colspec(col["zf_b"]),
                  pl.BlockSpec((2, hd), lambda i, h: (0, h)),
                  pl.BlockSpec(wf.shape, full2), pl.BlockSpec(wb.shape, full2),
                  pl.BlockSpec(mf.shape, full3), pl.BlockSpec(mb.shape, full3)],
        out_specs=[row_spec, row_spec, row_spec, u_spec, u_spec, d_spec, d_spec],
        out_shape=[jax.ShapeDtypeStruct((t, d_rec), F32),
                   jax.ShapeDtypeStruct((t, d_rec), BF16),
                   jax.ShapeDtypeStruct((t, d_rec), BF16),
                   jax.ShapeDtypeStruct((nchunks, hd, d_rec), F32),
                   jax.ShapeDtypeStruct((nchunks, hd, d_rec), F32),
                   jax.ShapeDtypeStruct((nchunks, d_rec), F32),
                   jax.ShapeDtypeStruct((nchunks, d_rec), F32)],
        compiler_params=_cparams(("arbitrary", "arbitrary"), 32),
        name="hgrn_a",
    )(proj, proj, proj, proj, lb, wf, wb, mf, mb)


def _hgrn_c_body(oi_ref, qtf_ref, qtb_ref, utf_ref, utb_ref, df_ref, db_ref, z_ref, g_ref,
                 out_ref, acc_ref, *, nchunks):
    c = REC_CHUNK
    hd = REC_HEAD_DIM

    def fwd(n, st):
        rows = pl.ds(pl.multiple_of(n * c, c), c)
        o = lax.dot_general(qtf_ref[rows, :], st.astype(BF16), _NT, preferred_element_type=F32)
        acc_ref[rows, :] = oi_ref[rows, :] + o
        return df_ref[pl.ds(n, 1), :] * st + utf_ref[n]

    lax.fori_loop(0, nchunks, fwd, jnp.zeros((hd, hd), F32), unroll=4)

    def bwd(i, st):
        n = nchunks - 1 - i
        rows = pl.ds(pl.multiple_of(n * c, c), c)
        o = lax.dot_general(qtb_ref[rows, :], st.astype(BF16), _NT, preferred_element_type=F32)
        acc_ref[rows, :] = acc_ref[rows, :] + o
        return db_ref[pl.ds(n, 1), :] * st + utb_ref[n]

    lax.fori_loop(0, nchunks, bwd, jnp.zeros((hd, hd), F32), unroll=4)

    o = _rms(acc_ref[...])
    out_ref[...] = (o * g_ref[...] * jax.nn.sigmoid(z_ref[...])).astype(BF16)


def _hgrn_c(oi, qtf, qtb, utf, utb, df, db, proj, g_out, col, bsz, seq, d_rec):
    heads = d_rec // REC_HEAD_DIM
    hd = REC_HEAD_DIM
    nchunks = seq // REC_CHUNK
    row_spec = pl.BlockSpec((seq, hd), lambda b, h: (b, h))
    u_spec = pl.BlockSpec((nchunks, hd, hd), lambda b, h: (b, 0, h))
    d_spec = pl.BlockSpec((nchunks, hd), lambda b, h: (b, h))
    zo = col["z_o"]
    return pl.pallas_call(
        functools.partial(_hgrn_c_body, nchunks=nchunks),
        grid=(bsz, heads),
        in_specs=[row_spec, row_spec, row_spec, u_spec, u_spec, d_spec, d_spec,
                  pl.BlockSpec((seq, hd), lambda b, h: (b, zo + h)),
                  pl.BlockSpec((1, hd), lambda b, h: (0, h))],
        out_specs=row_spec,
        out_shape=jax.ShapeDtypeStruct((bsz * seq, d_rec), BF16),
        scratch_shapes=[pltpu.VMEM((seq, hd), F32)],
        compiler_params=_cparams(("arbitrary", "arbitrary"), 48),
        name="hgrn_c",
    )(oi, qtf, qtb, utf, utb, df, db, proj, g_out.reshape(1, d_rec))


def _t5_bucket(rel):
    half_buckets = NUM_BUCKETS // 2
    ret = np.where(rel > 0, half_buckets, 0)
    n = np.abs(rel)
    max_exact = half_buckets // 2
    nf = np.maximum(n, 1).astype(np.float32)
    large = max_exact + (np.log(nf / np.float32(max_exact)) / np.float32(math.log(MAX_DISTANCE / max_exact))
                         * np.float32(half_buckets - max_exact)).astype(np.int32)
    large = np.minimum(large, half_buckets - 1)
    return ret + np.where(n < max_exact, n, large)


def _band_bias(rel_bias_g, window, dil):
    half = window // (2 * dil)
    q_off = np.arange(ATT_BLOCK)[:, None]
    rel = np.arange(3 * ATT_BLOCK)[None, :] - ATT_BLOCK - q_off
    onehot = (_t5_bucket(rel * dil)[..., None] == np.arange(NUM_BUCKETS)).astype(np.float32)
    bias = jnp.einsum("qkb,bh->hqk", jnp.asarray(onehot), rel_bias_g.astype(F32),
                      precision=lax.Precision.HIGHEST)
    return jnp.where(jnp.asarray(np.abs(rel) <= half)[None], bias, NEG_INF)


def _attn_body(q_ref, kp_ref, k_ref, kn_ref, vp_ref, v_ref, vn_ref, bias_ref,
               num_ref, st_ref, kc_ref, vc_ref, *, dil, tq, sub_len):
    blk = ATT_BLOCK
    nqb = tq // blk
    n = pl.program_id(1)
    scale = ATT_HEAD_DIM ** -0.5

    def sds(start, size):
        if dil == 1:
            return pl.ds(start, size)
        return pl.ds(start, size, stride=dil)

    cu = kc_ref.shape[0]
    qu = ATT_UNROLL // cu

    def deinterleave(r, j):
        kc_ref[j, 0:blk, :] = kp_ref[sds(r, blk), :].astype(BF16)
        kc_ref[j, blk:blk + tq, :] = k_ref[sds(r, tq), :].astype(BF16)
        kc_ref[j, blk + tq:2 * blk + tq, :] = kn_ref[sds(r, blk), :].astype(BF16)
        vc_ref[j, 0:blk, :] = vp_ref[sds(r, blk), :].astype(BF16)
        vc_ref[j, blk:blk + tq, :] = v_ref[sds(r, tq), :].astype(BF16)
        vc_ref[j, blk + tq:2 * blk + tq, :] = vn_ref[sds(r, blk), :].astype(BF16)

    def units(r0, qb0):
        us = [(j, u) for j in range(cu) for u in range(qu)]
        q0s = [pl.multiple_of((qb0 + u) * blk, blk) for _, u in us]
        rows = [sds(r0 + j + dil * q0, blk) for (j, _), q0 in zip(us, q0s)]
        lane = lax.broadcasted_iota(jnp.int32, (blk, LANE), 1)
        key_iota = lax.broadcasted_iota(jnp.int32, (1, 3 * blk), 1)
        bias = bias_ref[0]
        qs = [q_ref[rw, :].astype(BF16) for rw in rows]
        kws = [kc_ref[j, pl.ds(q0, 3 * blk), :] for (j, _), q0 in zip(us, q0s)]
        vws = [vc_ref[j, pl.ds(q0, 3 * blk), :] for (j, _), q0 in zip(us, q0s)]
        ss = [lax.dot_general(q, kw, _NT, preferred_element_type=F32) * scale for q, kw in zip(qs, kws)]
        valids = []
        for q0 in q0s:
            kpos = n * tq + q0 - blk + key_iota
            valids.append((kpos >= 0) & (kpos < sub_len))
        ss = [jnp.where(valid, s + bias, NEG_INF) for s, valid in zip(ss, valids)]
        ms = [jnp.max(s, axis=-1, keepdims=True) for s in ss]
        ps = [jnp.exp(s - m) for s, m in zip(ss, ms)]
        ls = [jnp.sum(p, axis=-1, keepdims=True) for p in ps]
        nums = [jnp.dot(p.astype(BF16), vw, preferred_element_type=F32) for p, vw in zip(ps, vws)]
        for rw, num, m, l in zip(rows, nums, ms, ls):
            num_ref[rw, :] = num
            st_ref[rw, :] = jnp.where(lane < LANE // 2, m, l)

    def class_group(rg, carry):
        r0 = rg * cu
        for j in range(cu):
            deinterleave(r0 + j, j)

        def qgroup(qg, carry2):
            units(r0, qg * qu)
            return carry2

        lax.fori_loop(0, nqb // qu, qgroup, 0)
        return carry

    lax.fori_loop(0, dil // cu, class_group, 0)


def _attn_group(proj, bias, col, g, dil, bsz, seq):
    tile = 1024
    tq = tile // dil
    halo = ATT_BLOCK * dil
    sub_len = seq // dil
    cu = ATT_UNROLL // min(tq // ATT_BLOCK, ATT_UNROLL)
    nh = ATT_HEADS_PER_GROUP
    hd = ATT_HEAD_DIM
    qc = col["q_a"] + g * nh
    kc = col["k_a"] + g * nh
    vc = col["v_a"] + g * nh
    tiles_b = seq // tile
    halos_b = seq // halo
    hpt = tile // halo

    own = lambda c: pl.BlockSpec((tile, hd), lambda b, n, h: (b * tiles_b + n, c + h))
    prev = lambda c: pl.BlockSpec(
        (halo, hd), lambda b, n, h: (b * halos_b + jnp.maximum(n * hpt - 1, 0), c + h))
    nxt = lambda c: pl.BlockSpec(
        (halo, hd), lambda b, n, h: (b * halos_b + jnp.minimum((n + 1) * hpt, halos_b - 1), c + h))
    t = bsz * seq
    return pl.pallas_call(
        functools.partial(_attn_body, dil=dil, tq=tq, sub_len=sub_len),
        grid=(bsz, tiles_b, nh),
        in_specs=[own(qc), prev(kc), own(kc), nxt(kc), prev(vc), own(vc), nxt(vc),
                  pl.BlockSpec((1,) + bias.shape[1:], lambda b, n, h: (h, 0, 0))],
        out_specs=[pl.BlockSpec((tile, hd), lambda b, n, h: (b * tiles_b + n, h)),
                   pl.BlockSpec((tile, LANE), lambda b, n, h: (b * tiles_b + n, h))],
        out_shape=[jax.ShapeDtypeStruct((t, nh * hd), F32), jax.ShapeDtypeStruct((t, nh * LANE), F32)],
        scratch_shapes=[pltpu.VMEM((cu, tq + 2 * ATT_BLOCK, hd), BF16),
                        pltpu.VMEM((cu, tq + 2 * ATT_BLOCK, hd), BF16)],
        compiler_params=_cparams(("arbitrary", "arbitrary", "arbitrary"), 32),
        name=f"attn_d{dil}",
    )(proj, proj, proj, proj, proj, proj, proj, bias)


def _merge_body(rec_ref, n0_ref, n1_ref, n2_ref, s0_ref, s1_ref, s2_ref, zgr_ref, zga_ref, x_ref,
                ada_ref, gpost_ref, gpre_ref, wbr_ref, wba_ref, wo_ref, wr_ref, br_ref,
                x1_ref, h2_ref, lg_ref):
    nh = ATT_HEADS_PER_GROUP
    hd = ATT_HEAD_DIM
    half = LANE // 2
    lane = lax.broadcasted_iota(jnp.int32, (rec_ref.shape[0], LANE), 1)
    heads = []
    for h in range(nh):
        cols = slice(h * hd, (h + 1) * hd)
        st = [s[:, cols] for s in (s0_ref, s1_ref, s2_ref)]
        top = jnp.maximum(jnp.maximum(st[0], st[1]), st[2])
        ws = [jnp.exp(s - top) for s in st]
        den = (ws[0] * pltpu.roll(st[0], half, 1) + ws[1] * pltpu.roll(st[1], half, 1)
               + ws[2] * pltpu.roll(st[2], half, 1))
        coef = [w / den for w in ws]
        coef = [jnp.where(lane < half, c, pltpu.roll(c, half, 1)) for c in coef]
        num = coef[0] * n0_ref[:, cols] + coef[1] * n1_ref[:, cols] + coef[2] * n2_ref[:, cols]
        heads.append(num.astype(BF16))
    att = jnp.concatenate(heads, axis=1)
    y_rec = jnp.dot(rec_ref[...], wbr_ref[...], preferred_element_type=F32)
    y_att = jnp.dot(att, wba_ref[...], preferred_element_type=F32)
    merged = jax.nn.sigmoid(zgr_ref[...]) * y_rec + jax.nn.sigmoid(zga_ref[...]) * y_att
    y = jnp.dot(merged.astype(BF16), wo_ref[...], preferred_element_type=F32)
    gt_m = ada_ref[0, 2:3, :]
    sh_f = ada_ref[0, 3:4, :]
    sc_f = ada_ref[0, 4:5, :]
    x1 = x_ref[...] + gt_m * (_rms(y) * gpost_ref[...])
    x1_ref[...] = x1
    h2 = _rms(x1) * gpre_ref[...] * (1.0 + sc_f) + sh_f
    tm = h2.shape[0]
    for s in range(SLAB_ROWS):
        h2_ref[pl.ds(s, tm, stride=SLAB_ROWS), :] = h2[:, s * LANE:(s + 1) * LANE]
    lg_ref[...] = jnp.dot(h2, wr_ref[...], preferred_element_type=F32,
                          precision=lax.Precision.HIGHEST) + br_ref[...]


def _merge(rec_o, nums, stats, proj, x2, ada3, g_post, g_pre, wbr, wba, wo, w_router, b_router, col, seq):
    t, d = x2.shape
    tm = 256
    per_b = seq // tm
    d_rec = rec_o.shape[1]
    w_att = nums[0].shape[1]
    ne = w_router.shape[1]
    dl = d // LANE
    row = lambda w: pl.BlockSpec((tm, w), lambda i: (i, 0))
    const = lambda shape: pl.BlockSpec(shape, lambda i: (0,) * len(shape), pipeline_mode=pl.Buffered(1))
    zgr = col["zg_rec"] // dl
    zga = col["zg_att"] // dl
    return pl.pallas_call(
        _merge_body,
        grid=(t // tm,),
        in_specs=[row(d_rec), row(w_att), row(w_att), row(w_att), row(w_att), row(w_att), row(w_att),
                  pl.BlockSpec((tm, d), lambda i: (i, zgr)),
                  pl.BlockSpec((tm, d), lambda i: (i, zga)),
                  row(d),
                  pl.BlockSpec((1, 6, d), lambda i: (i // per_b, 0, 0)),
                  const((1, d)), const((1, d)),
                  const(wbr.shape), const(wba.shape), const(wo.shape), const(w_router.shape),
                  const((1, ne))],
        out_specs=[row(d), pl.BlockSpec((tm * SLAB_ROWS, LANE), lambda i: (i, 0)),
                   pl.BlockSpec((tm, ne), lambda i: (i, 0))],
        out_shape=[jax.ShapeDtypeStruct((t, d), F32), jax.ShapeDtypeStruct((t * SLAB_ROWS, LANE), F32),
                   jax.ShapeDtypeStruct((t, ne), F32)],
        compiler_params=_cparams(("arbitrary",), 56),
        name="merge",
    )(rec_o, nums[0], nums[1], nums[2], stats[0], stats[1], stats[2], proj, proj, x2, ada3,
      g_post.reshape(1, d), g_pre.reshape(1, d), wbr, wba, wo, w_router, b_router.reshape(1, ne))


def _route_body(lg_ref, tri_ref, rt_ref, cnt_ref, carry_ref):
    i = pl.program_id(0)
    tr, ne = lg_ref.shape

    @pl.when(i == 0)
    def _():
        carry_ref[...] = jnp.zeros_like(carry_ref)

    l = lg_ref[...]
    lane = lax.broadcasted_iota(jnp.int32, (tr, ne), 1).astype(F32)
    vals, sels, idxs = [], [], []
    for _ in range(TOP_K):
        m = jnp.max(l, axis=-1, keepdims=True)
        idx = jnp.min(jnp.where(l == m, lane, float(ne)), axis=-1, keepdims=True)
        sel = lane == idx
        vals.append(m)
        idxs.append(idx)
        sels.append(sel)
        l = jnp.where(sel, -jnp.inf, l)
    es = [jnp.exp(v - vals[0]) for v in vals]
    tot = es[0] + es[1] + es[2] + es[3]
    chosen = (sels[0] | sels[1] | sels[2] | sels[3]).astype(F32)
    prefix = jnp.dot(tri_ref[...], chosen.astype(BF16), preferred_element_type=F32) + carry_ref[0:1, :]
    out_lane = lax.broadcasted_iota(jnp.int32, (tr, LANE), 1)
    rt = jnp.zeros((tr, LANE), F32)
    for k in range(TOP_K):
        rank = jnp.sum(jnp.where(sels[k], prefix, 0.0), axis=-1, keepdims=True)
        rt = jnp.where(out_lane == k, idxs[k], rt)
        rt = jnp.where(out_lane == TOP_K + k, es[k] / tot, rt)
        rt = jnp.where(out_lane == 2 * TOP_K + k, rank, rt)
    rt_ref[...] = rt
    new = carry_ref[0:1, :] + jnp.sum(chosen, axis=0, keepdims=True)
    carry_ref[...] = jnp.broadcast_to(new, carry_ref.shape)
    cnt_ref[...] = carry_ref[...]


def _route(logits):
    t, ne = logits.shape
    tr = 512
    tri = jnp.asarray(np.tril(np.ones((tr, tr), np.float32), -1), BF16)
    return pl.pallas_call(
        _route_body,
        grid=(t // tr,),
        in_specs=[pl.BlockSpec((tr, ne), lambda i: (i, 0)),
                  pl.BlockSpec((tr, tr), lambda i: (0, 0))],
        out_specs=[pl.BlockSpec((tr, LANE), lambda i: (i, 0)),
                   pl.BlockSpec((SUBLANE, ne), lambda i: (0, 0))],
        out_shape=[jax.ShapeDtypeStruct((t, LANE), F32), jax.ShapeDtypeStruct((SUBLANE, ne), F32)],
        scratch_shapes=[pltpu.VMEM((SUBLANE, ne), F32)],
        compiler_params=_cparams(("arbitrary",), 32),
        name="route",
    )(logits, tri)


def _weight_stream(w_hbm, wbf, stage, wsem, e, slot, c0, c1):
    kc = stage.shape[1]

    def copy(c):
        return pltpu.make_async_copy(w_hbm.at[e, pl.ds(pl.multiple_of(c * kc, kc), kc), :],
                                     stage.at[c % 2], wsem.at[c % 2])

    def prime():
        @pl.when(c0 < c1)
        def _():
            copy(c0).start()

        @pl.when(c0 + 1 < c1)
        def _():
            copy(c0 + 1).start()

    def finish():
        def body(c, carry):
            copy(c).wait()
            buf = c % 2

            def cast(i, carry2):
                r = pl.multiple_of(i * CAST_ROWS, CAST_ROWS)
                wbf[slot, pl.ds(pl.multiple_of(c * kc, kc) + r, CAST_ROWS), :] = (
                    stage[buf, pl.ds(r, CAST_ROWS), :].astype(BF16))
                return carry2
            lax.fori_loop(0, kc // CAST_ROWS, cast, 0)

            @pl.when(c + 2 < c1)
            def _():
                copy(c + 2).start()
            return carry
        lax.fori_loop(c0, c1, body, 0)

    return prime, finish


def _moe_up_body(be_ref, nu_ref, ws_ref, wn_ref, wc0_ref, wc1_ref, idx0_ref, idxn_ref, h2s_ref, w_hbm, bias_ref,
                 o_ref, xbuf, wbf, stage, sem, wsem):
    b = pl.program_id(0)
    nu = nu_ref[0]
    f = o_ref.shape[1]
    slab = SLAB_ROWS
    nch = wbf.shape[1] // stage.shape[1]

    def issue(idx_ref, slot):
        def body(r, c):
            tok = idx_ref[0, 0, r]
            pltpu.make_async_copy(h2s_ref.at[pl.ds(pl.multiple_of(tok * slab, slab), slab), :],
                                  xbuf.at[slot, pl.ds(pl.multiple_of(r * slab, slab), slab), :],
                                  sem.at[slot]).start()
            return c
        lax.fori_loop(0, BM, body, 0, unroll=DMA_UNROLL)

    @pl.when(b == 0)
    def _():
        issue(idx0_ref, 0)
        prime0, finish0 = _weight_stream(w_hbm, wbf, stage, wsem, be_ref[0], ws_ref[0], 0, nch)
        prime0()
        finish0()

    @pl.when(b + 1 < nu)
    def _():
        issue(idxn_ref, (b + 1) % 2)

    @pl.when(b < nu)
    def _():
        wslot = ws_ref[b]
        prime, finish = _weight_stream(w_hbm, wbf, stage, wsem, wn_ref[b], 1 - wslot, wc0_ref[b], wc1_ref[b])
        prime()
        slot = b % 2
        pltpu.make_async_copy(h2s_ref.at[pl.ds(0, BM * slab), :], xbuf.at[slot], sem.at[slot]).wait()
        x = jnp.concatenate([xbuf[slot, pl.ds(s, BM, stride=slab), :].astype(BF16) for s in range(slab)],
                            axis=1)
        half = f // 2
        for c0 in (0, half):
            gate = (jnp.dot(x, wbf[wslot, :, c0:c0 + half], preferred_element_type=F32)
                    + bias_ref[0, :, c0:c0 + half])
            up = (jnp.dot(x, wbf[wslot, :, f + c0:f + c0 + half], preferred_element_type=F32)
                  + bias_ref[0, :, f + c0:f + c0 + half])
            gate = jnp.minimum(gate, SWIGLU_LIMIT)
            up = jnp.clip(up, -SWIGLU_LIMIT, SWIGLU_LIMIT)
            o_ref[:, c0:c0 + half] = (gate * jax.nn.sigmoid(SWIGLU_ALPHA * gate) * (up + 1.0)).astype(BF16)
        finish()

    @pl.when(b >= nu)
    def _():
        o_ref[...] = jnp.zeros_like(o_ref)


def _moe_up(h2s, src_tok, wgu, bgu, sched):
    ne, d, f2 = wgu.shape
    f = f2 // 2
    p = src_tok.shape[0]
    nb = p // BM
    idx3 = src_tok.reshape(nb, 1, BM)
    smem_blk = lambda imap: pl.BlockSpec((1, 1, BM), imap, memory_space=pltpu.SMEM)
    grid_spec = pltpu.PrefetchScalarGridSpec(
        num_scalar_prefetch=6,
        grid=(nb,),
        in_specs=[smem_blk(lambda b, be, *_: (0, 0, 0)),
                  smem_blk(lambda b, be, *_: (jnp.minimum(b + 1, nb - 1), 0, 0)),
                  pl.BlockSpec(memory_space=pl.ANY),
                  pl.BlockSpec(memory_space=pl.ANY),
                  pl.BlockSpec((1, 1, f2), lambda b, be, *_: (be[b], 0, 0))],
        out_specs=pl.BlockSpec((BM, f), lambda b, be, *_: (b, 0)),
        scratch_shapes=[pltpu.VMEM((2, BM * SLAB_ROWS, LANE), F32),
                        pltpu.VMEM((2, d, f2), BF16),
                        pltpu.VMEM((2, W_CHUNK, f2), F32),
                        pltpu.SemaphoreType.DMA((2,)), pltpu.SemaphoreType.DMA((2,))],
    )
    return pl.pallas_call(
        _moe_up_body,
        grid_spec=grid_spec,
        out_shape=jax.ShapeDtypeStruct((p, f), BF16),
        compiler_params=_cparams(("arbitrary",), 58),
        name="moe_up",
    )(sched["blk_e"], sched["n_used"], sched["wslot"], sched["wnext"], sched["wc0"], sched["wc1"],
      idx3, idx3, h2s, wgu, bgu.reshape(ne, 1, f2))


def _moe_down_body(be_ref, nu_ref, ws_ref, wn_ref, wc0_ref, wc1_ref, nv_ref, dst_ref, a_ref, w_hbm, bias_ref,
                   ysc_ref, ybuf, wbf, stage, sem, wsem):
    b = pl.program_id(0)
    nb = pl.num_programs(0)
    nu = nu_ref[0]
    slab = SLAB_ROWS
    nch = wbf.shape[1] // stage.shape[1]

    @pl.when(b == 0)
    def _():
        prime0, finish0 = _weight_stream(w_hbm, wbf, stage, wsem, be_ref[0], ws_ref[0], 0, nch)
        prime0()
        finish0()

    def row_copy(slot, r, d):
        return pltpu.make_async_copy(ybuf.at[slot, pl.ds(pl.multiple_of(r * slab, slab), slab), :],
                                     ysc_ref.at[pl.ds(pl.multiple_of(d * slab, slab), slab), :],
                                     sem.at[slot])

    def drain(step):
        slot = step % 2
        count = nv_ref[step]

        @pl.when(count == BM)
        def _():
            pltpu.make_async_copy(ybuf.at[slot], ysc_ref.at[pl.ds(0, BM * slab), :], sem.at[slot]).wait()

        @pl.when(count < BM)
        def _():
            def body(r, c):
                row_copy(slot, 0, 0).wait()
                return c
            lax.fori_loop(0, count, body, 0)

    @pl.when((b >= 2) & (b < nu))
    def _():
        drain(b - 2)

    @pl.when(b < nu)
    def _():
        slot = b % 2
        wslot = ws_ref[b]
        prime, finish = _weight_stream(w_hbm, wbf, stage, wsem, wn_ref[b], 1 - wslot, wc0_ref[b], wc1_ref[b])
        prime()
        y = jnp.dot(a_ref[...], wbf[wslot], preferred_element_type=F32) + bias_ref[0]
        for s in range(slab):
            ybuf[slot, pl.ds(s, BM, stride=slab), :] = y[:, s * LANE:(s + 1) * LANE]
        finish()

        def body(r, c):
            row_copy(slot, r, dst_ref[0, 0, r]).start()
            return c

        @pl.when(nv_ref[b] == BM)
        def _():
            lax.fori_loop(0, BM, body, 0, unroll=DMA_UNROLL)

        @pl.when(nv_ref[b] < BM)
        def _():
            lax.fori_loop(0, nv_ref[b], body, 0)

    @pl.when(b == nb - 1)
    def _():
        @pl.when(nu >= 2)
        def _():
            drain(nu - 2)
        drain(nu - 1)


def _moe_down(act, dst_slot, n_out_rows, wd, bd, sched):
    p, f = act.shape
    ne, _, d = wd.shape
    nb = p // BM
    grid_spec = pltpu.PrefetchScalarGridSpec(
        num_scalar_prefetch=7,
        grid=(nb,),
        in_specs=[pl.BlockSpec((1, 1, BM), lambda b, be, *_: (b, 0, 0), memory_space=pltpu.SMEM),
                  pl.BlockSpec((BM, f), lambda b, be, *_: (b, 0)),
                  pl.BlockSpec(memory_space=pl.ANY),
                  pl.BlockSpec((1, 1, d), lambda b, be, *_: (be[b], 0, 0))],
        out_specs=pl.BlockSpec(memory_space=pl.ANY),
        scratch_shapes=[pltpu.VMEM((2, BM * SLAB_ROWS, LANE), F32),
                        pltpu.VMEM((2, f, d), BF16),
                        pltpu.VMEM((2, W_CHUNK, d), F32),
                        pltpu.SemaphoreType.DMA((2,)), pltpu.SemaphoreType.DMA((2,))],
    )
    return pl.pallas_call(
        _moe_down_body,
        grid_spec=grid_spec,
        out_shape=jax.ShapeDtypeStruct((n_out_rows * SLAB_ROWS, LANE), F32),
        compiler_params=pltpu.CompilerParams(dimension_semantics=("arbitrary",),
                                             vmem_limit_bytes=48 * 1024 * 1024, has_side_effects=True),
        name="moe_down",
    )(sched["blk_e"], sched["n_used"], sched["wslot"], sched["wnext"], sched["wc0"], sched["wc1"],
      sched["nvalid"], dst_slot.reshape(nb, 1, BM), act, wd, bd.reshape(ne, 1, d))


def _final_body(ysc_ref, rt_ref, x1_ref, ada_ref, g_ref, o_ref):
    tm = x1_ref.shape[0]
    rt = rt_ref[...]
    stride = TOP_K * SLAB_ROWS
    pieces = []
    for s in range(SLAB_ROWS):
        acc = rt[:, TOP_K:TOP_K + 1] * ysc_ref[pl.ds(s, tm, stride=stride), :]
        for k in range(1, TOP_K):
            acc = acc + rt[:, TOP_K + k:TOP_K + k + 1] * ysc_ref[pl.ds(k * SLAB_ROWS + s, tm, stride=stride), :]
        pieces.append(acc)
    y = jnp.concatenate(pieces, axis=1)
    gt_f = ada_ref[0, 5:6, :]
    o_ref[...] = x1_ref[...] + gt_f * (_rms(y) * g_ref[...])


def _final(ysc, rt, x1, ada3, g_post, seq):
    t, d = x1.shape
    tm = 256
    per_b = seq // tm
    return pl.pallas_call(
        _final_body,
        grid=(t // tm,),
        in_specs=[pl.BlockSpec((tm * TOP_K * SLAB_ROWS, LANE), lambda i: (i, 0)),
                  pl.BlockSpec((tm, LANE), lambda i: (i, 0)),
                  pl.BlockSpec((tm, d), lambda i: (i, 0)),
                  pl.BlockSpec((1, 6, d), lambda i: (i // per_b, 0, 0)),
                  pl.BlockSpec((1, d), lambda i: (0, 0))],
        out_specs=pl.BlockSpec((tm, d), lambda i: (i, 0)),
        out_shape=jax.ShapeDtypeStruct((t, d), F32),
        compiler_params=_cparams(("arbitrary",), 48),
        name="final",
    )(ysc, rt, x1, ada3, g_post.reshape(1, d))


def _mixer_ffn_layer(x2, ada3, bsz, seq, g_mix_pre, g_mix_post, g_ffn_pre, g_ffn_post, w_in, lb, g_rec_out,
                     rel_bias, w_branch_rec, w_branch_att, w_o, w_router, b_router, w_gate_up, b_gate_up,
                     w_down, b_down):
    t, d = x2.shape
    d_rec = w_branch_rec.shape[0]
    w_att = w_branch_att.shape[0]
    d_att = 3 * w_att
    widths = dict(q_r=d_rec, i_r=d_rec, zf_f=d_rec, zf_b=d_rec, z_o=d_rec, q_a=d_att, k_a=d_att, v_a=d_att,
                  zg_rec=d, zg_att=d)
    ref_order = ("q_r", "i_r", "zf_f", "zf_b", "z_o", "q_a", "k_a", "v_a", "zg_rec", "zg_att")
    my_order = ("zg_rec", "zg_att", "q_r", "i_r", "zf_f", "zf_b", "z_o", "q_a", "k_a", "v_a")
    ref_off, acc = {}, 0
    for name in ref_order:
        ref_off[name] = acc
        acc += widths[name]
    col, acc = {}, 0
    for name in my_order:
        col[name] = acc // LANE
        acc += widths[name]
    w_in_bf = jnp.concatenate([w_in[:, ref_off[nm]:ref_off[nm] + widths[nm]].astype(BF16) for nm in my_order],
                              axis=1)

    proj = _inproj(x2, g_mix_pre, ada3, w_in_bf, seq)

    oi, qtf, qtb, utf, utb, df, db = _hgrn_a(proj, lb, col, t, d_rec)
    rec_o = _hgrn_c(oi, qtf, qtb, utf, utb, df, db, proj, g_rec_out, col, bsz, seq, d_rec)

    nums, stats = [], []
    for g, (window, dil) in enumerate(DIL_GROUPS):
        hs = slice(g * ATT_HEADS_PER_GROUP, (g + 1) * ATT_HEADS_PER_GROUP)
        bias = _band_bias(rel_bias[:, hs], window, dil)
        num, st = _attn_group(proj, bias, col, g, dil, bsz, seq)
        nums.append(num)
        stats.append(st)

    x1, h2s, logits = _merge(rec_o, nums, stats, proj, x2, ada3, g_mix_post, g_ffn_pre,
                            w_branch_rec.astype(BF16), w_branch_att.astype(BF16), w_o.astype(BF16),
                            w_router, b_router, col, seq)

    rt, cnt = _route(logits)
    ne = logits.shape[1]
    counts = cnt[0].astype(jnp.int32)
    top_idx = rt[:, 0:TOP_K].astype(jnp.int32)
    rank = rt[:, 2 * TOP_K:3 * TOP_K].astype(jnp.int32)
    padded = (counts + BM - 1) // BM * BM
    pends = jnp.cumsum(padded)
    pstarts = pends - padded
    experts = jnp.arange(ne, dtype=jnp.int32)
    pstart_sel = jnp.sum(jnp.where(top_idx[..., None] == experts, pstarts, 0), axis=-1)
    dest = (pstart_sel + rank).reshape(-1)
    p_rows = t * TOP_K + ne * BM
    nb = p_rows // BM
    blk_start = jnp.arange(nb, dtype=jnp.int32) * BM
    blk_e = jnp.minimum(jnp.sum((pends[None, :] <= blk_start[:, None]).astype(jnp.int32), axis=1), ne - 1)
    n_used = (pends[-1:] // BM).astype(jnp.int32)

    n_assign = t * TOP_K
    slot_assign = jnp.full((p_rows,), -1, jnp.int32).at[dest].set(jnp.arange(n_assign, dtype=jnp.int32))
    src_tok = jnp.maximum(slot_assign, 0) // TOP_K
    valid_end = pstarts + counts
    blk_end = jnp.sum(jnp.where(blk_e[:, None] == experts, valid_end, 0), axis=-1)
    nvalid = jnp.clip(blk_end - blk_start, 0, BM).astype(jnp.int32)
    nvalid = jnp.where(jnp.arange(nb) < n_used[0], nvalid, 0)

    lookup = lambda table: jnp.sum(jnp.where(blk_e[:, None] == experts, table, 0), axis=-1)
    nch = d // W_CHUNK
    nonempty = padded > 0
    order = jnp.cumsum(nonempty.astype(jnp.int32)) - 1
    later = lax.cummin(jnp.where(nonempty, experts, ne)[::-1])[::-1]
    next_e = jnp.concatenate([later[1:], jnp.full((1,), ne, jnp.int32)])
    blk_next = lookup(next_e)
    has_next = (blk_next < ne) & (jnp.arange(nb) < n_used[0])
    k_in_run = jnp.arange(nb, dtype=jnp.int32) - lookup(pstarts // BM)
    n_in_run = jnp.maximum(lookup(padded // BM), 1)
    sched = dict(
        blk_e=blk_e, n_used=n_used, nvalid=nvalid,
        wslot=lookup(order) % 2,
        wnext=jnp.where(has_next, blk_next, blk_e),
        wc0=jnp.where(has_next, k_in_run * nch // n_in_run, 0),
        wc1=jnp.where(has_next, (k_in_run + 1) * nch // n_in_run, 0))
    sched = {k: v.astype(jnp.int32) for k, v in sched.items()}

    act = _moe_up(h2s, src_tok, w_gate_up, b_gate_up, sched)
    ysc = _moe_down(act, slot_assign, n_assign, w_down, b_down, sched)
    return _final(ysc, rt, x1, ada3, g_ffn_post, seq)


def kernel(x, c, w_ada, b_ada, g_mix_pre, g_mix_post, g_ffn_pre, g_ffn_post, w_in, g_rec_out, w_branch_rec,
           w_branch_att, w_o, w_router, b_router, w_gate_up, b_gate_up, w_down, b_down, rec_lb_table, rel_bias):
    bsz, seq, d = x.shape
    depth = w_in.shape[0]
    lb_all = jnp.cumsum(jax.nn.softmax(rec_lb_table.astype(F32), axis=1), axis=1)
    x2 = x.reshape(bsz * seq, d)
    for layer in range(depth):
        ada3 = _ada(c, w_ada[layer], b_ada[layer]).reshape(bsz, 6, d)
        x2 = _mixer_ffn_layer(x2, ada3, bsz, seq, g_mix_pre[layer], g_mix_post[layer], g_ffn_pre[layer],
                              g_ffn_post[layer], w_in[layer], lb_all[:, layer], g_rec_out[layer], rel_bias,
                              w_branch_rec[layer], w_branch_att[layer], w_o[layer], w_router[layer],
                              b_router[layer], w_gate_up[layer], b_gate_up[layer], w_down[layer],
                              b_down[layer])
    return x2.reshape(bsz, seq, d)
```

```python
import functools
import math

import numpy as np
import jax
import jax.numpy as jnp
from jax import lax
from jax.experimental import pallas as pl
from jax.experimental.pallas import tpu as pltpu

F32 = jnp.float32
BF16 = jnp.bfloat16

LANE = 128
SUBLANE = 8
SLAB_ROWS = 16

REC_HEAD_DIM = 128
REC_CHUNK = 64
ATT_HEAD_DIM = 128
ATT_HEADS_PER_GROUP = 4
ATT_BLOCK = 64
DIL_GROUPS = ((128, 1), (512, 4), (2048, 16))
NUM_BUCKETS = 32
MAX_DISTANCE = 1024
N_EXPERTS = 32
TOP_K = 4
SWIGLU_LIMIT = 7.0
SWIGLU_ALPHA = 1.702
RMS_EPS = 1e-6
NEG_INF = -1e30

N_LEVELS = 6
W_CHUNK_BYTES = 4 * 1024 * 1024
CAST_ROWS = 32
HGRN_UNROLL = 4
DMA_UNROLL = 8
ATT_UNROLL = 8
BM = 256

_NT = (((1,), (1,)), ((), ()))
_TN = (((0,), (0,)), ((), ()))


def _cparams(sem, vmem_mb):
    return pltpu.CompilerParams(dimension_semantics=sem, vmem_limit_bytes=vmem_mb * 1024 * 1024)


def _rms(x):
    return x * lax.rsqrt(jnp.mean(x * x, axis=-1, keepdims=True) + RMS_EPS)


def _ada_body(c_ref, w_ref, b_ref, o_ref):
    c = c_ref[...]
    cond = (c * jax.nn.sigmoid(c)).astype(BF16)
    o_ref[...] = jnp.dot(cond, w_ref[...].astype(BF16), preferred_element_type=F32) + b_ref[...]


def _ada(c, w, b):
    bsz, d = c.shape
    n = w.shape[1]
    tn = 1024
    cp = jnp.zeros((SUBLANE, d), F32).at[:bsz].set(c)
    out = pl.pallas_call(
        _ada_body,
        grid=(n // tn,),
        in_specs=[pl.BlockSpec((SUBLANE, d), lambda j: (0, 0)),
                  pl.BlockSpec((d, tn), lambda j: (0, j)),
                  pl.BlockSpec((1, tn), lambda j: (0, j))],
        out_specs=pl.BlockSpec((SUBLANE, tn), lambda j: (0, j)),
        out_shape=jax.ShapeDtypeStruct((SUBLANE, n), F32),
        compiler_params=_cparams(("arbitrary",), 40),
        name="ada",
    )(cp, w, b.reshape(1, n))
    return out[:bsz]


def _inproj_body(x_ref, g_ref, ada_ref, w_ref, o_ref, h_ref):
    @pl.when(pl.program_id(1) == 0)
    def _():
        y = _rms(x_ref[...]) * g_ref[...]
        sh = ada_ref[0, 0:1, :]
        sc = ada_ref[0, 1:2, :]
        h_ref[...] = (y * (1.0 + sc) + sh).astype(BF16)

    o_ref[...] = jnp.dot(h_ref[...], w_ref[...], preferred_element_type=F32)


def _inproj(x2, g, ada3, w_bf, seq):
    t, d = x2.shape
    n = w_bf.shape[1]
    tm, tn = 1024, 768
    per_b = seq // tm
    return pl.pallas_call(
        _inproj_body,
        grid=(t // tm, n // tn),
        in_specs=[pl.BlockSpec((tm, d), lambda i, j: (i, 0)),
                  pl.BlockSpec((1, d), lambda i, j: (0, 0)),
                  pl.BlockSpec((1, 6, d), lambda i, j: (i // per_b, 0, 0)),
                  pl.BlockSpec((d, tn), lambda i, j: (0, j))],
        out_specs=pl.BlockSpec((tm, tn), lambda i, j: (i, j)),
        out_shape=jax.ShapeDtypeStruct((t, n), F32),
        scratch_shapes=[pltpu.VMEM((tm, d), BF16)],
        compiler_params=_cparams(("arbitrary", "arbitrary"), 48),
        name="inproj",
    )(x2, g.reshape(1, d), ada3, w_bf)


def _hgrn_consts():
    c = REC_CHUNK
    r = np.arange(c)[:, None]
    m = np.arange(c)[None, :]
    wf = np.zeros((8 * c, c), np.float32)
    wb = np.zeros((8 * c, c), np.float32)
    mf = np.zeros((N_LEVELS + 1, c, c), np.float32)
    for lvl in range(N_LEVELS):
        s = 32 >> lvl
        m0 = (r // (2 * s)) * (2 * s) + s
        up = r >= m0
        wf[lvl * c:(lvl + 1) * c] = np.where(up, (m >= m0) & (m <= r), (m > r) & (m <= m0 - 1))
        wb[lvl * c:(lvl + 1) * c] = np.where(up, (m >= m0) & (m <= r - 1), (m >= r) & (m <= m0 - 1))
        i = np.arange(c)[:, None]
        j = np.arange(c)[None, :]
        mf[lvl] = (i // (2 * s) == j // (2 * s)) & (i % (2 * s) >= s) & (j % (2 * s) < s)
    mf[N_LEVELS] = np.eye(c)
    wf[6 * c:7 * c] = m <= r
    wf[7 * c:8 * c] = m > r
    wb[6 * c:7 * c] = m >= r
    wb[7 * c:8 * c] = m < r
    mb = np.transpose(mf, (0, 2, 1)).copy()
    wf3 = np.concatenate([wf, wf, wf], axis=1)
    wb3 = np.concatenate([wb, wb, wb], axis=1)
    return (jnp.asarray(wf3, BF16), jnp.asarray(wb3, BF16), jnp.asarray(mf, F32), jnp.asarray(mb, F32))


def _split3(g):
    hi = g.astype(BF16)
    r1 = g - hi.astype(F32)
    mid = r1.astype(BF16)
    lo = (r1 - mid.astype(F32)).astype(BF16)
    return jnp.concatenate([hi, mid, lo], axis=0)


def _hgrn_a_body(q_ref, i_ref, zf_ref, zb_ref, lb_ref, wf_ref, wb_ref, mf_ref, mb_ref,
                 oi_ref, qtf_ref, qtb_ref, utf_ref, utb_ref, df_ref, db_ref, *, cpb):
    c = REC_CHUNK
    dirs = ((zf_ref, wf_ref, mf_ref, qtf_ref, utf_ref, df_ref, 0, c - 1),
            (zb_ref, wb_ref, mb_ref, qtb_ref, utb_ref, db_ref, 1, 0))

    def chunk_group(cg, carry):
        cis = [cg * HGRN_UNROLL + u for u in range(HGRN_UNROLL)]
        rows = [pl.ds(pl.multiple_of(ci * c, c), c) for ci in cis]
        zqs = [q_ref[rw, :] for rw in rows]
        qs = [zq * jax.nn.sigmoid(zq) for zq in zqs]
        vbs = [i_ref[rw, :].astype(BF16) for rw in rows]
        units = [(u, d) for u in range(HGRN_UNROLL) for d in range(2)]
        ks, es = {}, {}
        for u, d in units:
            z_ref, w_ref = dirs[d][0], dirs[d][1]
            lb = lb_ref[d:d + 1, :]
            f = lb + (1.0 - lb) * jax.nn.sigmoid(z_ref[rows[u], :])
            ks[u, d] = 1.0 - f
            es[u, d] = jnp.exp(jnp.dot(w_ref[...], _split3(jnp.log(f)), preferred_element_type=F32))
        acc = [jnp.zeros((c, c), F32) for _ in range(HGRN_UNROLL)]
        for lvl in range(N_LEVELS + 1):
            for u, d in units:
                m_ref = dirs[d][2]
                if lvl < N_LEVELS:
                    el = es[u, d][lvl * c:(lvl + 1) * c]
                    qa, ka = (qs[u] * el).astype(BF16), (ks[u, d] * el).astype(BF16)
                else:
                    qa, ka = qs[u].astype(BF16), ks[u, d].astype(BF16)
                p = lax.dot_general(qa, ka, _NT, preferred_element_type=F32)
                acc[u] = acc[u] + p * m_ref[lvl]
        for u, d in units:
            _, _, _, qt_ref, ut_ref, d_ref, _, drow = dirs[d]
            e = es[u, d]
            qt_ref[rows[u], :] = (qs[u] * e[6 * c:7 * c]).astype(BF16)
            kt = (ks[u, d] * e[7 * c:8 * c]).astype(BF16)
            ut_ref[cis[u]] = lax.dot_general(vbs[u], kt, _TN, preferred_element_type=F32)
            d_ref[pl.ds(cis[u], 1), :] = e[6 * c + drow:6 * c + drow + 1]
        for u in range(HGRN_UNROLL):
            oi_ref[rows[u], :] = jnp.dot(acc[u].astype(BF16), vbs[u], preferred_element_type=F32)
        return carry

    lax.fori_loop(0, cpb // HGRN_UNROLL, chunk_group, 0)


def _hgrn_a(proj, lb, col, t, d_rec):
    heads = d_rec // REC_HEAD_DIM
    tq = 512
    cpb = tq // REC_CHUNK
    nchunks = t // REC_CHUNK
    wf, wb, mf, mb = _hgrn_consts()
    hd = REC_HEAD_DIM

    def colspec(off):
        return pl.BlockSpec((tq, hd), lambda i, h: (i, off + h))

    full2 = lambda i, h: (0, 0)
    full3 = lambda i, h: (0, 0, 0)
    row_spec = pl.BlockSpec((tq, hd), lambda i, h: (i, h))
    u_spec = pl.BlockSpec((cpb, hd, hd), lambda i, h: (i, 0, h))
    d_spec = pl.BlockSpec((cpb, hd), lambda i, h: (i, h))
    return pl.pallas_call(
        functools.partial(_hgrn_a_body, cpb=cpb),
        grid=(t // tq, heads),
        in_specs=[colspec(col["q_r"]), colspec(col["i_r"]), colspec(col["zf_f"]), colspec(col["zf_b"]),
                  pl.BlockSpec((2, hd), lambda i, h: (0, h)),
                  pl.BlockSpec(wf.shape, full2), pl.BlockSpec(wb.shape, full2),
                  pl.BlockSpec(mf.shape, full3), pl.BlockSpec(mb.shape, full3)],
        out_specs=[row_spec, row_spec, row_spec, u_spec, u_spec, d_spec, d_spec],
        out_shape=[jax.ShapeDtypeStruct((t, d_rec), F32),
                   jax.ShapeDtypeStruct((t, d_rec), BF16),
                   jax.ShapeDtypeStruct((t, d_rec), BF16),
                   jax.ShapeDtypeStruct((nchunks, hd, d_rec), F32),
                   jax.ShapeDtypeStruct((nchunks, hd, d_rec), F32),
                   jax.ShapeDtypeStruct((nchunks, d_rec), F32),
                   jax.ShapeDtypeStruct((nchunks, d_rec), F32)],
        compiler_params=_cparams(("arbitrary", "arbitrary"), 32),
        name="hgrn_a",
    )(proj, proj, proj, proj, lb, wf, wb, mf, mb)


def _hgrn_c_body(oi_ref, qtf_ref, qtb_ref, utf_ref, utb_ref, df_ref, db_ref, z_ref, g_ref,
                 out_ref, acc_ref, *, nchunks):
    c = REC_CHUNK
    hd = REC_HEAD_DIM

    def fwd(n, st):
        rows = pl.ds(pl.multiple_of(n * c, c), c)
        o = lax.dot_general(qtf_ref[rows, :], st.astype(BF16), _NT, preferred_element_type=F32)
        acc_ref[rows, :] = oi_ref[rows, :] + o
        return df_ref[pl.ds(n, 1), :] * st + utf_ref[n]

    lax.fori_loop(0, nchunks, fwd, jnp.zeros((hd, hd), F32), unroll=4)

    def bwd(i, st):
        n = nchunks - 1 - i
        rows = pl.ds(pl.multiple_of(n * c, c), c)
        o = lax.dot_general(qtb_ref[rows, :], st.astype(BF16), _NT, preferred_element_type=F32)
        acc_ref[rows, :] = acc_ref[rows, :] + o
        return db_ref[pl.ds(n, 1), :] * st + utb_ref[n]

    lax.fori_loop(0, nchunks, bwd, jnp.zeros((hd, hd), F32), unroll=4)

    o = _rms(acc_ref[...])
    out_ref[...] = (o * g_ref[...] * jax.nn.sigmoid(z_ref[...])).astype(BF16)


def _hgrn_c(oi, qtf, qtb, utf, utb, df, db, proj, g_out, col, bsz, seq, d_rec):
    heads = d_rec // REC_HEAD_DIM
    hd = REC_HEAD_DIM
    nchunks = seq // REC_CHUNK
    row_spec = pl.BlockSpec((seq, hd), lambda b, h: (b, h))
    u_spec = pl.BlockSpec((nchunks, hd, hd), lambda b, h: (b, 0, h))
    d_spec = pl.BlockSpec((nchunks, hd), lambda b, h: (b, h))
    zo = col["z_o"]
    return pl.pallas_call(
        functools.partial(_hgrn_c_body, nchunks=nchunks),
        grid=(bsz, heads),
        in_specs=[row_spec, row_spec, row_spec, u_spec, u_spec, d_spec, d_spec,
                  pl.BlockSpec((seq, hd), lambda b, h: (b, zo + h)),
                  pl.BlockSpec((1, hd), lambda b, h: (0, h))],
        out_specs=row_spec,
        out_shape=jax.ShapeDtypeStruct((bsz * seq, d_rec), BF16),
        scratch_shapes=[pltpu.VMEM((seq, hd), F32)],
        compiler_params=_cparams(("arbitrary", "arbitrary"), 48),
        name="hgrn_c",
    )(oi, qtf, qtb, utf, utb, df, db, proj, g_out.reshape(1, d_rec))


def _t5_bucket(rel):
    half_buckets = NUM_BUCKETS // 2
    ret = np.where(rel > 0, half_buckets, 0)
    n = np.abs(rel)
    max_exact = half_buckets // 2
    nf = np.maximum(n, 1).astype(np.float32)
    large = max_exact + (np.log(nf / np.float32(max_exact)) / np.float32(math.log(MAX_DISTANCE / max_exact))
                         * np.float32(half_buckets - max_exact)).astype(np.int32)
    large = np.minimum(large, half_buckets - 1)
    return ret + np.where(n < max_exact, n, large)


def _band_bias(rel_bias_g, window, dil):
    half = window // (2 * dil)
    q_off = np.arange(ATT_BLOCK)[:, None]
    rel = np.arange(3 * ATT_BLOCK)[None, :] - ATT_BLOCK - q_off
    onehot = (_t5_bucket(rel * dil)[..., None] == np.arange(NUM_BUCKETS)).astype(np.float32)
    bias = jnp.einsum("qkb,bh->hqk", jnp.asarray(onehot), rel_bias_g.astype(F32),
                      precision=lax.Precision.HIGHEST)
    return jnp.where(jnp.asarray(np.abs(rel) <= half)[None], bias, NEG_INF)


def _attn_body(q_ref, kp_ref, k_ref, kn_ref, vp_ref, v_ref, vn_ref, bias_ref,
               num_ref, st_ref, kc_ref, vc_ref, *, dil, tq, sub_len):
    blk = ATT_BLOCK
    nqb = tq // blk
    n = pl.program_id(1)
    scale = ATT_HEAD_DIM ** -0.5

    def sds(start, size):
        if dil == 1:
            return pl.ds(start, size)
        return pl.ds(start, size, stride=dil)

    cu = kc_ref.shape[0]
    qu = ATT_UNROLL // cu

    def deinterleave(r, j):
        kc_ref[j, 0:blk, :] = kp_ref[sds(r, blk), :].astype(BF16)
        kc_ref[j, blk:blk + tq, :] = k_ref[sds(r, tq), :].astype(BF16)
        kc_ref[j, blk + tq:2 * blk + tq, :] = kn_ref[sds(r, blk), :].astype(BF16)
        vc_ref[j, 0:blk, :] = vp_ref[sds(r, blk), :].astype(BF16)
        vc_ref[j, blk:blk + tq, :] = v_ref[sds(r, tq), :].astype(BF16)
        vc_ref[j, blk + tq:2 * blk + tq, :] = vn_ref[sds(r, blk), :].astype(BF16)

    def units(r0, qb0):
        us = [(j, u) for j in range(cu) for u in range(qu)]
        q0s = [pl.multiple_of((qb0 + u) * blk, blk) for _, u in us]
        rows = [sds(r0 + j + dil * q0, blk) for (j, _), q0 in zip(us, q0s)]
        lane = lax.broadcasted_iota(jnp.int32, (blk, LANE), 1)
        key_iota = lax.broadcasted_iota(jnp.int32, (1, 3 * blk), 1)
        bias = bias_ref[0]
        qs = [q_ref[rw, :].astype(BF16) for rw in rows]
        kws = [kc_ref[j, pl.ds(q0, 3 * blk), :] for (j, _), q0 in zip(us, q0s)]
        vws = [vc_ref[j, pl.ds(q0, 3 * blk), :] for (j, _), q0 in zip(us, q0s)]
        ss = [lax.dot_general(q, kw, _NT, preferred_element_type=F32) * scale for q, kw in zip(qs, kws)]
        valids = []
        for q0 in q0s:
            kpos = n * tq + q0 - blk + key_iota
            valids.append((kpos >= 0) & (kpos < sub_len))
        ss = [jnp.where(valid, s + bias, NEG_INF) for s, valid in zip(ss, valids)]
        ms = [jnp.max(s, axis=-1, keepdims=True) for s in ss]
        ps = [jnp.exp(s - m) for s, m in zip(ss, ms)]
        ls = [jnp.sum(p, axis=-1, keepdims=True) for p in ps]
        nums = [jnp.dot(p.astype(BF16), vw, preferred_element_type=F32) for p, vw in zip(ps, vws)]
        for rw, num, m, l in zip(rows, nums, ms, ls):
            num_ref[rw, :] = num
            st_ref[rw, :] = jnp.where(lane < LANE // 2, m, l)

    def class_group(rg, carry):
        r0 = rg * cu
        for j in range(cu):
            deinterleave(r0 + j, j)

        def qgroup(qg, carry2):
            units(r0, qg * qu)
            return carry2

        lax.fori_loop(0, nqb // qu, qgroup, 0)
        return carry

    lax.fori_loop(0, dil // cu, class_group, 0)


def _attn_group(proj, bias, col, g, dil, bsz, seq):
    tile = 1024
    tq = tile // dil
    halo = ATT_BLOCK * dil
    sub_len = seq // dil
    cu = ATT_UNROLL // min(tq // ATT_BLOCK, ATT_UNROLL)
    nh = ATT_HEADS_PER_GROUP
    hd = ATT_HEAD_DIM
    qc = col["q_a"] + g * nh
    kc = col["k_a"] + g * nh
    vc = col["v_a"] + g * nh
    tiles_b = seq // tile
    halos_b = seq // halo
    hpt = tile // halo

    own = lambda c: pl.BlockSpec((tile, hd), lambda b, n, h: (b * tiles_b + n, c + h))
    prev = lambda c: pl.BlockSpec(
        (halo, hd), lambda b, n, h: (b * halos_b + jnp.maximum(n * hpt - 1, 0), c + h))
    nxt = lambda c: pl.BlockSpec(
        (halo, hd), lambda b, n, h: (b * halos_b + jnp.minimum((n + 1) * hpt, halos_b - 1), c + h))
    t = bsz * seq
    return pl.pallas_call(
        functools.partial(_attn_body, dil=dil, tq=tq, sub_len=sub_len),
        grid=(bsz, tiles_b, nh),
        in_specs=[own(qc), prev(kc), own(kc), nxt(kc), prev(vc), own(vc), nxt(vc),
                  pl.BlockSpec((1,) + bias.shape[1:], lambda b, n, h: (h, 0, 0))],
        out_specs=[pl.BlockSpec((tile, hd), lambda b, n, h: (b * tiles_b + n, h)),
                   pl.BlockSpec((tile, LANE), lambda b, n, h: (b * tiles_b + n, h))],
        out_shape=[jax.ShapeDtypeStruct((t, nh * hd), F32), jax.ShapeDtypeStruct((t, nh * LANE), F32)],
        scratch_shapes=[pltpu.VMEM((cu, tq + 2 * ATT_BLOCK, hd), BF16),
                        pltpu.VMEM((cu, tq + 2 * ATT_BLOCK, hd), BF16)],
        compiler_params=_cparams(("arbitrary", "arbitrary", "arbitrary"), 32),
        name=f"attn_d{dil}",
    )(proj, proj, proj, proj, proj, proj, proj, bias)


def _merge_body(rec_ref, n0_ref, n1_ref, n2_ref, s0_ref, s1_ref, s2_ref, zgr_ref, zga_ref, x_ref,
                ada_ref, gpost_ref, gpre_ref, wbr_ref, wba_ref, wo_ref, wr_ref, br_ref,
                x1_ref, h2_ref, lg_ref):
    nh = ATT_HEADS_PER_GROUP
    hd = ATT_HEAD_DIM
    half = LANE // 2
    lane = lax.broadcasted_iota(jnp.int32, (rec_ref.shape[0], LANE), 1)
    heads = []
    for h in range(nh):
        cols = slice(h * hd, (h + 1) * hd)
        st = [s[:, cols] for s in (s0_ref, s1_ref, s2_ref)]
        top = jnp.maximum(jnp.maximum(st[0], st[1]), st[2])
        ws = [jnp.exp(s - top) for s in st]
        den = (ws[0] * pltpu.roll(st[0], half, 1) + ws[1] * pltpu.roll(st[1], half, 1)
               + ws[2] * pltpu.roll(st[2], half, 1))
        coef = [w / den for w in ws]
        coef = [jnp.where(lane < half, c, pltpu.roll(c, half, 1)) for c in coef]
        num = coef[0] * n0_ref[:, cols] + coef[1] * n1_ref[:, cols] + coef[2] * n2_ref[:, cols]
        heads.append(num.astype(BF16))
    att = jnp.concatenate(heads, axis=1)
    y_rec = jnp.dot(rec_ref[...], wbr_ref[...], preferred_element_type=F32)
    y_att = jnp.dot(att, wba_ref[...], preferred_element_type=F32)
    merged = jax.nn.sigmoid(zgr_ref[...]) * y_rec + jax.nn.sigmoid(zga_ref[...]) * y_att
    y = jnp.dot(merged.astype(BF16), wo_ref[...], preferred_element_type=F32)
    gt_m = ada_ref[0, 2:3, :]
    sh_f = ada_ref[0, 3:4, :]
    sc_f = ada_ref[0, 4:5, :]
    x1 = x_ref[...] + gt_m * (_rms(y) * gpost_ref[...])
    x1_ref[...] = x1
    h2 = _rms(x1) * gpre_ref[...] * (1.0 + sc_f) + sh_f
    tm = h2.shape[0]
    for s in range(SLAB_ROWS):
        h2_ref[pl.ds(s, tm, stride=SLAB_ROWS), :] = h2[:, s * LANE:(s + 1) * LANE]
    lg_ref[...] = jnp.dot(h2, wr_ref[...], preferred_element_type=F32,
                          precision=lax.Precision.HIGHEST) + br_ref[...]


def _merge(rec_o, nums, stats, proj, x2, ada3, g_post, g_pre, wbr, wba, wo, w_router, b_router, col, seq):
    t, d = x2.shape
    tm = 256
    per_b = seq // tm
    d_rec = rec_o.shape[1]
    w_att = nums[0].shape[1]
    ne = w_router.shape[1]
    dl = d // LANE
    row = lambda w: pl.BlockSpec((tm, w), lambda i: (i, 0))
    const = lambda shape: pl.BlockSpec(shape, lambda i: (0,) * len(shape), pipeline_mode=pl.Buffered(1))
    zgr = col["zg_rec"] // dl
    zga = col["zg_att"] // dl
    return pl.pallas_call(
        _merge_body,
        grid=(t // tm,),
        in_specs=[row(d_rec), row(w_att), row(w_att), row(w_att), row(w_att), row(w_att), row(w_att),
                  pl.BlockSpec((tm, d), lambda i: (i, zgr)),
                  pl.BlockSpec((tm, d), lambda i: (i, zga)),
                  row(d),
                  pl.BlockSpec((1, 6, d), lambda i: (i // per_b, 0, 0)),
                  const((1, d)), const((1, d)),
                  const(wbr.shape), const(wba.shape), const(wo.shape), const(w_router.shape),
                  const((1, ne))],
        out_specs=[row(d), pl.BlockSpec((tm * SLAB_ROWS, LANE), lambda i: (i, 0)),
                   pl.BlockSpec((tm, ne), lambda i: (i, 0))],
        out_shape=[jax.ShapeDtypeStruct((t, d), F32), jax.ShapeDtypeStruct((t * SLAB_ROWS, LANE), F32),
                   jax.ShapeDtypeStruct((t, ne), F32)],
        compiler_params=_cparams(("arbitrary",), 56),
        name="merge",
    )(rec_o, nums[0], nums[1], nums[2], stats[0], stats[1], stats[2], proj, proj, x2, ada3,
      g_post.reshape(1, d), g_pre.reshape(1, d), wbr, wba, wo, w_router, b_router.reshape(1, ne))


def _route_body(lg_ref, tri_ref, rt_ref, cnt_ref, carry_ref):
    i = pl.program_id(0)
    tr, ne = lg_ref.shape

    @pl.when(i == 0)
    def _():
        carry_ref[...] = jnp.zeros_like(carry_ref)

    l = lg_ref[...]
    lane = lax.broadcasted_iota(jnp.int32, (tr, ne), 1).astype(F32)
    vals, sels, idxs = [], [], []
    for _ in range(TOP_K):
        m = jnp.max(l, axis=-1, keepdims=True)
        idx = jnp.min(jnp.where(l == m, lane, float(ne)), axis=-1, keepdims=True)
        sel = lane == idx
        vals.append(m)
        idxs.append(idx)
        sels.append(sel)
        l = jnp.where(sel, -jnp.inf, l)
    es = [jnp.exp(v - vals[0]) for v in vals]
    tot = es[0] + es[1] + es[2] + es[3]
    chosen = (sels[0] | sels[1] | sels[2] | sels[3]).astype(F32)
    prefix = jnp.dot(tri_ref[...], chosen.astype(BF16), preferred_element_type=F32) + carry_ref[0:1, :]
    out_lane = lax.broadcasted_iota(jnp.int32, (tr, LANE), 1)
    rt = jnp.zeros((tr, LANE), F32)
    for k in range(TOP_K):
        rank = jnp.sum(jnp.where(sels[k], prefix, 0.0), axis=-1, keepdims=True)
        rt = jnp.where(out_lane == k, idxs[k], rt)
        rt = jnp.where(out_lane == TOP_K + k, es[k] / tot, rt)
        rt = jnp.where(out_lane == 2 * TOP_K + k, rank, rt)
    rt_ref[...] = rt
    new = carry_ref[0:1, :] + jnp.sum(chosen, axis=0, keepdims=True)
    carry_ref[...] = jnp.broadcast_to(new, carry_ref.shape)
    cnt_ref[...] = carry_ref[...]


def _route(logits):
    t, ne = logits.shape
    tr = 512
    tri = jnp.asarray(np.tril(np.ones((tr, tr), np.float32), -1), BF16)
    return pl.pallas_call(
        _route_body,
        grid=(t // tr,),
        in_specs=[pl.BlockSpec((tr, ne), lambda i: (i, 0)),
                  pl.BlockSpec((tr, tr), lambda i: (0, 0))],
        out_specs=[pl.BlockSpec((tr, LANE), lambda i: (i, 0)),
                   pl.BlockSpec((SUBLANE, ne), lambda i: (0, 0))],
        out_shape=[jax.ShapeDtypeStruct((t, LANE), F32), jax.ShapeDtypeStruct((SUBLANE, ne), F32)],
        scratch_shapes=[pltpu.VMEM((SUBLANE, ne), F32)],
        compiler_params=_cparams(("arbitrary",), 32),
        name="route",
    )(logits, tri)


def _chunk_rows(ncols):
    return W_CHUNK_BYTES // (4 * ncols)


def _weight_stream(w_hbm, wbf, stage, wsem, e, slot, c0, c1):
    kc = stage.shape[1]

    def copy(c):
        return pltpu.make_async_copy(w_hbm.at[e, pl.ds(pl.multiple_of(c * kc, kc), kc), :],
                                     stage.at[c % 2], wsem.at[c % 2])

    def prime():
        @pl.when(c0 < c1)
        def _():
            copy(c0).start()

        @pl.when(c0 + 1 < c1)
        def _():
            copy(c0 + 1).start()

    def finish():
        def body(c, carry):
            copy(c).wait()
            buf = c % 2

            def cast(i, carry2):
                r = pl.multiple_of(i * CAST_ROWS, CAST_ROWS)
                wbf[slot, pl.ds(pl.multiple_of(c * kc, kc) + r, CAST_ROWS), :] = (
                    stage[buf, pl.ds(r, CAST_ROWS), :].astype(BF16))
                return carry2
            lax.fori_loop(0, kc // CAST_ROWS, cast, 0)

            @pl.when(c + 2 < c1)
            def _():
                copy(c + 2).start()
            return carry
        lax.fori_loop(c0, c1, body, 0)

    return prime, finish


def _moe_up_body(be_ref, nu_ref, ws_ref, wn_ref, wc0_ref, wc1_ref, idx0_ref, idxn_ref, h2s_ref, w_hbm, bias_ref,
                 o_ref, xbuf, wbf, stage, sem, wsem):
    b = pl.program_id(0)
    nu = nu_ref[0]
    f = o_ref.shape[1]
    slab = SLAB_ROWS
    nch = wbf.shape[1] // stage.shape[1]

    def issue(idx_ref, slot):
        def body(r, c):
            tok = idx_ref[0, 0, r]
            pltpu.make_async_copy(h2s_ref.at[pl.ds(pl.multiple_of(tok * slab, slab), slab), :],
                                  xbuf.at[slot, pl.ds(pl.multiple_of(r * slab, slab), slab), :],
                                  sem.at[slot]).start()
            return c
        lax.fori_loop(0, BM, body, 0, unroll=DMA_UNROLL)

    @pl.when(b == 0)
    def _():
        issue(idx0_ref, 0)
        prime0, finish0 = _weight_stream(w_hbm, wbf, stage, wsem, be_ref[0], ws_ref[0], 0, nch)
        prime0()
        finish0()

    @pl.when(b + 1 < nu)
    def _():
        issue(idxn_ref, (b + 1) % 2)

    @pl.when(b < nu)
    def _():
        wslot = ws_ref[b]
        prime, finish = _weight_stream(w_hbm, wbf, stage, wsem, wn_ref[b], 1 - wslot, wc0_ref[b], wc1_ref[b])
        prime()
        slot = b % 2
        pltpu.make_async_copy(h2s_ref.at[pl.ds(0, BM * slab), :], xbuf.at[slot], sem.at[slot]).wait()
        x = jnp.concatenate([xbuf[slot, pl.ds(s, BM, stride=slab), :].astype(BF16) for s in range(slab)],
                            axis=1)
        half = f // 2
        for c0 in (0, half):
            gate = (jnp.dot(x, wbf[wslot, :, c0:c0 + half], preferred_element_type=F32)
                    + bias_ref[0, :, c0:c0 + half])
            up = (jnp.dot(x, wbf[wslot, :, f + c0:f + c0 + half], preferred_element_type=F32)
                  + bias_ref[0, :, f + c0:f + c0 + half])
            gate = jnp.minimum(gate, SWIGLU_LIMIT)
            up = jnp.clip(up, -SWIGLU_LIMIT, SWIGLU_LIMIT)
            o_ref[:, c0:c0 + half] = (gate * jax.nn.sigmoid(SWIGLU_ALPHA * gate) * (up + 1.0)).astype(BF16)
        finish()

    @pl.when(b >= nu)
    def _():
        o_ref[...] = jnp.zeros_like(o_ref)


def _moe_up(h2s, src_tok, wgu, bgu, sched):
    ne, d, f2 = wgu.shape
    f = f2 // 2
    p = src_tok.shape[0]
    nb = p // BM
    idx3 = src_tok.reshape(nb, 1, BM)
    smem_blk = lambda imap: pl.BlockSpec((1, 1, BM), imap, memory_space=pltpu.SMEM)
    grid_spec = pltpu.PrefetchScalarGridSpec(
        num_scalar_prefetch=6,
        grid=(nb,),
        in_specs=[smem_blk(lambda b, be, *_: (0, 0, 0)),
                  smem_blk(lambda b, be, *_: (jnp.minimum(b + 1, nb - 1), 0, 0)),
                  pl.BlockSpec(memory_space=pl.ANY),
                  pl.BlockSpec(memory_space=pl.ANY),
                  pl.BlockSpec((1, 1, f2), lambda b, be, *_: (be[b], 0, 0))],
        out_specs=pl.BlockSpec((BM, f), lambda b, be, *_: (b, 0)),
        scratch_shapes=[pltpu.VMEM((2, BM * SLAB_ROWS, LANE), F32),
                        pltpu.VMEM((2, d, f2), BF16),
                        pltpu.VMEM((2, _chunk_rows(f2), f2), F32),
                        pltpu.SemaphoreType.DMA((2,)), pltpu.SemaphoreType.DMA((2,))],
    )
    return pl.pallas_call(
        _moe_up_body,
        grid_spec=grid_spec,
        out_shape=jax.ShapeDtypeStruct((p, f), BF16),
        compiler_params=_cparams(("arbitrary",), 58),
        name="moe_up",
    )(sched["blk_e"], sched["n_used"], sched["wslot"], sched["wnext"], sched["wc0"], sched["wc1"],
      idx3, idx3, h2s, wgu, bgu.reshape(ne, 1, f2))


def _moe_down_body(be_ref, nu_ref, ws_ref, wn_ref, wc0_ref, wc1_ref, nv_ref, dst_ref, a_ref, w_hbm, bias_ref,
                   ysc_ref, ybuf, wbf, stage, sem, wsem):
    b = pl.program_id(0)
    nb = pl.num_programs(0)
    nu = nu_ref[0]
    slab = SLAB_ROWS
    nch = wbf.shape[1] // stage.shape[1]

    @pl.when(b == 0)
    def _():
        prime0, finish0 = _weight_stream(w_hbm, wbf, stage, wsem, be_ref[0], ws_ref[0], 0, nch)
        prime0()
        finish0()

    def row_copy(slot, r, d):
        return pltpu.make_async_copy(ybuf.at[slot, pl.ds(pl.multiple_of(r * slab, slab), slab), :],
                                     ysc_ref.at[pl.ds(pl.multiple_of(d * slab, slab), slab), :],
                                     sem.at[slot])

    def drain(step):
        slot = step % 2
        count = nv_ref[step]

        @pl.when(count == BM)
        def _():
            pltpu.make_async_copy(ybuf.at[slot], ysc_ref.at[pl.ds(0, BM * slab), :], sem.at[slot]).wait()

        @pl.when(count < BM)
        def _():
            def body(r, c):
                row_copy(slot, 0, 0).wait()
                return c
            lax.fori_loop(0, count, body, 0)

    @pl.when((b >= 2) & (b < nu))
    def _():
        drain(b - 2)

    @pl.when(b < nu)
    def _():
        slot = b % 2
        wslot = ws_ref[b]
        prime, finish = _weight_stream(w_hbm, wbf, stage, wsem, wn_ref[b], 1 - wslot, wc0_ref[b], wc1_ref[b])
        prime()
        y = jnp.dot(a_ref[...], wbf[wslot], preferred_element_type=F32) + bias_ref[0]
        for s in range(slab):
            ybuf[slot, pl.ds(s, BM, stride=slab), :] = y[:, s * LANE:(s + 1) * LANE]
        finish()

        def body(r, c):
            row_copy(slot, r, dst_ref[0, 0, r]).start()
            return c

        @pl.when(nv_ref[b] == BM)
        def _():
            lax.fori_loop(0, BM, body, 0, unroll=DMA_UNROLL)

        @pl.when(nv_ref[b] < BM)
        def _():
            lax.fori_loop(0, nv_ref[b], body, 0)

    @pl.when(b == nb - 1)
    def _():
        @pl.when(nu >= 2)
        def _():
            drain(nu - 2)
        drain(nu - 1)


def _moe_down(act, dst_slot, n_out_rows, wd, bd, sched):
    p, f = act.shape
    ne, _, d = wd.shape
    nb = p // BM
    grid_spec = pltpu.PrefetchScalarGridSpec(
        num_scalar_prefetch=7,
        grid=(nb,),
        in_specs=[pl.BlockSpec((1, 1, BM), lambda b, be, *_: (b, 0, 0), memory_space=pltpu.SMEM),
                  pl.BlockSpec((BM, f), lambda b, be, *_: (b, 0)),
                  pl.BlockSpec(memory_space=pl.ANY),
                  pl.BlockSpec((1, 1, d), lambda b, be, *_: (be[b], 0, 0))],
        out_specs=pl.BlockSpec(memory_space=pl.ANY),
        scratch_shapes=[pltpu.VMEM((2, BM * SLAB_ROWS, LANE), F32),
                        pltpu.VMEM((2, f, d), BF16),
                        pltpu.VMEM((2, _chunk_rows(d), d), F32),
                        pltpu.SemaphoreType.DMA((2,)), pltpu.SemaphoreType.DMA((2,))],
    )
    return pl.pallas_call(
        _moe_down_body,
        grid_spec=grid_spec,
        out_shape=jax.ShapeDtypeStruct((n_out_rows * SLAB_ROWS, LANE), F32),
        compiler_params=pltpu.CompilerParams(dimension_semantics=("arbitrary",),
                                             vmem_limit_bytes=48 * 1024 * 1024, has_side_effects=True),
        name="moe_down",
    )(sched["blk_e"], sched["n_used"], sched["wslot"], sched["wnext"], sched["wc0"], sched["wc1"],
      sched["nvalid"], dst_slot.reshape(nb, 1, BM), act, wd, bd.reshape(ne, 1, d))


def _final_body(ysc_ref, rt_ref, x1_ref, ada_ref, g_ref, o_ref):
    tm = x1_ref.shape[0]
    rt = rt_ref[...]
    stride = TOP_K * SLAB_ROWS
    pieces = []
    for s in range(SLAB_ROWS):
        acc = rt[:, TOP_K:TOP_K + 1] * ysc_ref[pl.ds(s, tm, stride=stride), :]
        for k in range(1, TOP_K):
            acc = acc + rt[:, TOP_K + k:TOP_K + k + 1] * ysc_ref[pl.ds(k * SLAB_ROWS + s, tm, stride=stride), :]
        pieces.append(acc)
    y = jnp.concatenate(pieces, axis=1)
    gt_f = ada_ref[0, 5:6, :]
    o_ref[...] = x1_ref[...] + gt_f * (_rms(y) * g_ref[...])


def _final(ysc, rt, x1, ada3, g_post, seq):
    t, d = x1.shape
    tm = 256
    per_b = seq // tm
    return pl.pallas_call(
        _final_body,
        grid=(t // tm,),
        in_specs=[pl.BlockSpec((tm * TOP_K * SLAB_ROWS, LANE), lambda i: (i, 0)),
                  pl.BlockSpec((tm, LANE), lambda i: (i, 0)),
                  pl.BlockSpec((tm, d), lambda i: (i, 0)),
                  pl.BlockSpec((1, 6, d), lambda i: (i // per_b, 0, 0)),
                  pl.BlockSpec((1, d), lambda i: (0, 0))],
        out_specs=pl.BlockSpec((tm, d), lambda i: (i, 0)),
        out_shape=jax.ShapeDtypeStruct((t, d), F32),
        compiler_params=_cparams(("arbitrary",), 48),
        name="final",
    )(ysc, rt, x1, ada3, g_post.reshape(1, d))


def _mixer_ffn_layer(x2, ada3, bsz, seq, g_mix_pre, g_mix_post, g_ffn_pre, g_ffn_post, w_in, lb, g_rec_out,
                     rel_bias, w_branch_rec, w_branch_att, w_o, w_router, b_router, w_gate_up, b_gate_up,
                     w_down, b_down):
    t, d = x2.shape
    d_rec = w_branch_rec.shape[0]
    w_att = w_branch_att.shape[0]
    d_att = 3 * w_att
    widths = dict(q_r=d_rec, i_r=d_rec, zf_f=d_rec, zf_b=d_rec, z_o=d_rec, q_a=d_att, k_a=d_att, v_a=d_att,
                  zg_rec=d, zg_att=d)
    ref_order = ("q_r", "i_r", "zf_f", "zf_b", "z_o", "q_a", "k_a", "v_a", "zg_rec", "zg_att")
    my_order = ("zg_rec", "zg_att", "q_r", "i_r", "zf_f", "zf_b", "z_o", "q_a", "k_a", "v_a")
    ref_off, acc = {}, 0
    for name in ref_order:
        ref_off[name] = acc
        acc += widths[name]
    col, acc = {}, 0
    for name in my_order:
        col[name] = acc // LANE
        acc += widths[name]
    w_in_bf = jnp.concatenate([w_in[:, ref_off[nm]:ref_off[nm] + widths[nm]].astype(BF16) for nm in my_order],
                              axis=1)

    proj = _inproj(x2, g_mix_pre, ada3, w_in_bf, seq)

    oi, qtf, qtb, utf, utb, df, db = _hgrn_a(proj, lb, col, t, d_rec)
    rec_o = _hgrn_c(oi, qtf, qtb, utf, utb, df, db, proj, g_rec_out, col, bsz, seq, d_rec)

    nums, stats = [], []
    for g, (window, dil) in enumerate(DIL_GROUPS):
        hs = slice(g * ATT_HEADS_PER_GROUP, (g + 1) * ATT_HEADS_PER_GROUP)
        bias = _band_bias(rel_bias[:, hs], window, dil)
        num, st = _attn_group(proj, bias, col, g, dil, bsz, seq)
        nums.append(num)
        stats.append(st)

    x1, h2s, logits = _merge(rec_o, nums, stats, proj, x2, ada3, g_mix_post, g_ffn_pre,
                            w_branch_rec.astype(BF16), w_branch_att.astype(BF16), w_o.astype(BF16),
                            w_router, b_router, col, seq)

    rt, cnt = _route(logits)
    ne = logits.shape[1]
    counts = cnt[0].astype(jnp.int32)
    top_idx = rt[:, 0:TOP_K].astype(jnp.int32)
    rank = rt[:, 2 * TOP_K:3 * TOP_K].astype(jnp.int32)
    padded = (counts + BM - 1) // BM * BM
    pends = jnp.cumsum(padded)
    pstarts = pends - padded
    experts = jnp.arange(ne, dtype=jnp.int32)
    pstart_sel = jnp.sum(jnp.where(top_idx[..., None] == experts, pstarts, 0), axis=-1)
    dest = (pstart_sel + rank).reshape(-1)
    p_rows = t * TOP_K + ne * BM
    nb = p_rows // BM
    blk_start = jnp.arange(nb, dtype=jnp.int32) * BM
    blk_e = jnp.minimum(jnp.sum((pends[None, :] <= blk_start[:, None]).astype(jnp.int32), axis=1), ne - 1)
    n_used = (pends[-1:] // BM).astype(jnp.int32)

    n_assign = t * TOP_K
    slot_assign = jnp.full((p_rows,), -1, jnp.int32).at[dest].set(jnp.arange(n_assign, dtype=jnp.int32))
    src_tok = jnp.maximum(slot_assign, 0) // TOP_K
    valid_end = pstarts + counts
    blk_end = jnp.sum(jnp.where(blk_e[:, None] == experts, valid_end, 0), axis=-1)
    nvalid = jnp.clip(blk_end - blk_start, 0, BM).astype(jnp.int32)
    nvalid = jnp.where(jnp.arange(nb) < n_used[0], nvalid, 0)

    lookup = lambda table: jnp.sum(jnp.where(blk_e[:, None] == experts, table, 0), axis=-1)
    nonempty = padded > 0
    order = jnp.cumsum(nonempty.astype(jnp.int32)) - 1
    later = lax.cummin(jnp.where(nonempty, experts, ne)[::-1])[::-1]
    next_e = jnp.concatenate([later[1:], jnp.full((1,), ne, jnp.int32)])
    blk_next = lookup(next_e)
    has_next = (blk_next < ne) & (jnp.arange(nb) < n_used[0])
    k_in_run = jnp.arange(nb, dtype=jnp.int32) - lookup(pstarts // BM)
    n_in_run = jnp.maximum(lookup(padded // BM), 1)
    common = dict(blk_e=blk_e, n_used=n_used, nvalid=nvalid, wslot=lookup(order) % 2,
                  wnext=jnp.where(has_next, blk_next, blk_e))

    def schedule(w):
        nch = w.shape[1] // _chunk_rows(w.shape[2])
        s = dict(common, wc0=jnp.where(has_next, k_in_run * nch // n_in_run, 0),
                 wc1=jnp.where(has_next, (k_in_run + 1) * nch // n_in_run, 0))
        return {k: v.astype(jnp.int32) for k, v in s.items()}

    act = _moe_up(h2s, src_tok, w_gate_up, b_gate_up, schedule(w_gate_up))
    ysc = _moe_down(act, slot_assign, n_assign, w_down, b_down, schedule(w_down))
    return _final(ysc, rt, x1, ada3, g_ffn_post, seq)


def kernel(x, c, w_ada, b_ada, g_mix_pre, g_mix_post, g_ffn_pre, g_ffn_post, w_in, g_rec_out, w_branch_rec,
           w_branch_att, w_o, w_router, b_router, w_gate_up, b_gate_up, w_down, b_down, rec_lb_table, rel_bias):
    bsz, seq, d = x.shape
    depth = w_in.shape[0]
    lb_all = jnp.cumsum(jax.nn.softmax(rec_lb_table.astype(F32), axis=1), axis=1)
    x2 = x.reshape(bsz * seq, d)
    for layer in range(depth):
        ada3 = _ada(c, w_ada[layer], b_ada[layer]).reshape(bsz, 6, d)
        x2 = _mixer_ffn_layer(x2, ada3, bsz, seq, g_mix_pre[layer], g_mix_post[layer], g_ffn_pre[layer],
                              g_ffn_post[layer], w_in[layer], lb_all[:, layer], g_rec_out[layer], rel_bias,
                              w_branch_rec[layer], w_branch_att[layer], w_o[layer], w_router[layer],
                              b_router[layer], w_gate_up[layer], b_gate_up[layer], w_down[layer],
                              b_down[layer])
    return x2.reshape(bsz, seq, d)
```

```python
import functools
import math

import numpy as np
import jax
import jax.numpy as jnp
from jax import lax
from jax.experimental import pallas as pl
from jax.experimental.pallas import tpu as pltpu

F32 = jnp.float32
BF16 = jnp.bfloat16

LANE = 128
SUBLANE = 8
SLAB_ROWS = 16

REC_HEAD_DIM = 128
REC_CHUNK = 64
ATT_HEAD_DIM = 128
ATT_HEADS_PER_GROUP = 4
ATT_BLOCK = 64
DIL_GROUPS = ((128, 1), (512, 4), (2048, 16))
NUM_BUCKETS = 32
MAX_DISTANCE = 1024
N_EXPERTS = 32
TOP_K = 4
SWIGLU_LIMIT = 7.0
SWIGLU_ALPHA = 1.702
RMS_EPS = 1e-6
NEG_INF = -1e30

N_LEVELS = 6
W_CHUNK_BYTES = 4 * 1024 * 1024
CAST_ROWS = 32
HGRN_UNROLL = 4
DMA_UNROLL = 8
ATT_UNROLL = 8
BM = 256

_NT = (((1,), (1,)), ((), ()))
_TN = (((0,), (0,)), ((), ()))


def _cparams(sem, vmem_mb):
    return pltpu.CompilerParams(dimension_semantics=sem, vmem_limit_bytes=vmem_mb * 1024 * 1024)


def _rms(x):
    return x * lax.rsqrt(jnp.mean(x * x, axis=-1, keepdims=True) + RMS_EPS)


def _ada_body(c_ref, w_ref, b_ref, o_ref):
    c = c_ref[...]
    cond = (c * jax.nn.sigmoid(c)).astype(BF16)
    o_ref[...] = jnp.dot(cond, w_ref[...].astype(BF16), preferred_element_type=F32) + b_ref[...]


def _ada(c, w, b):
    bsz, d = c.shape
    n = w.shape[1]
    tn = 1024
    cp = jnp.zeros((SUBLANE, d), F32).at[:bsz].set(c)
    out = pl.pallas_call(
        _ada_body,
        grid=(n // tn,),
        in_specs=[pl.BlockSpec((SUBLANE, d), lambda j: (0, 0)),
                  pl.BlockSpec((d, tn), lambda j: (0, j)),
                  pl.BlockSpec((1, tn), lambda j: (0, j))],
        out_specs=pl.BlockSpec((SUBLANE, tn), lambda j: (0, j)),
        out_shape=jax.ShapeDtypeStruct((SUBLANE, n), F32),
        compiler_params=_cparams(("arbitrary",), 40),
        name="ada",
    )(cp, w, b.reshape(1, n))
    return out[:bsz]


def _inproj_body(x_ref, g_ref, ada_ref, w_ref, o_ref, h_ref):
    @pl.when(pl.program_id(1) == 0)
    def _():
        y = _rms(x_ref[...]) * g_ref[...]
        sh = ada_ref[0, 0:1, :]
        sc = ada_ref[0, 1:2, :]
        h_ref[...] = (y * (1.0 + sc) + sh).astype(BF16)

    o_ref[...] = jnp.dot(h_ref[...], w_ref[...], preferred_element_type=F32)


def _inproj(x2, g, ada3, w_bf, seq):
    t, d = x2.shape
    n = w_bf.shape[1]
    tm, tn = 1024, 768
    per_b = seq // tm
    return pl.pallas_call(
        _inproj_body,
        grid=(t // tm, n // tn),
        in_specs=[pl.BlockSpec((tm, d), lambda i, j: (i, 0)),
                  pl.BlockSpec((1, d), lambda i, j: (0, 0)),
                  pl.BlockSpec((1, 6, d), lambda i, j: (i // per_b, 0, 0)),
                  pl.BlockSpec((d, tn), lambda i, j: (0, j))],
        out_specs=pl.BlockSpec((tm, tn), lambda i, j: (i, j)),
        out_shape=jax.ShapeDtypeStruct((t, n), F32),
        scratch_shapes=[pltpu.VMEM((tm, d), BF16)],
        compiler_params=_cparams(("arbitrary", "arbitrary"), 48),
        name="inproj",
    )(x2, g.reshape(1, d), ada3, w_bf)


def _hgrn_consts():
    c = REC_CHUNK
    r = np.arange(c)[:, None]
    m = np.arange(c)[None, :]
    wf = np.zeros((8 * c, c), np.float32)
    wb = np.zeros((8 * c, c), np.float32)
    mf = np.zeros((N_LEVELS + 1, c, c), np.float32)
    for lvl in range(N_LEVELS):
        s = 32 >> lvl
        m0 = (r // (2 * s)) * (2 * s) + s
        up = r >= m0
        wf[lvl * c:(lvl + 1) * c] = np.where(up, (m >= m0) & (m <= r), (m > r) & (m <= m0 - 1))
        wb[lvl * c:(lvl + 1) * c] = np.where(up, (m >= m0) & (m <= r - 1), (m >= r) & (m <= m0 - 1))
        i = np.arange(c)[:, None]
        j = np.arange(c)[None, :]
        mf[lvl] = (i // (2 * s) == j // (2 * s)) & (i % (2 * s) >= s) & (j % (2 * s) < s)
    mf[N_LEVELS] = np.eye(c)
    wf[6 * c:7 * c] = m <= r
    wf[7 * c:8 * c] = m > r
    wb[6 * c:7 * c] = m >= r
    wb[7 * c:8 * c] = m < r
    mb = np.transpose(mf, (0, 2, 1)).copy()
    wf3 = np.concatenate([wf, wf, wf], axis=1)
    wb3 = np.concatenate([wb, wb, wb], axis=1)
    return (jnp.asarray(wf3, BF16), jnp.asarray(wb3, BF16), jnp.asarray(mf, F32), jnp.asarray(mb, F32))


def _split3(g):
    hi = g.astype(BF16)
    r1 = g - hi.astype(F32)
    mid = r1.astype(BF16)
    lo = (r1 - mid.astype(F32)).astype(BF16)
    return jnp.concatenate([hi, mid, lo], axis=0)


def _hgrn_a_body(q_ref, i_ref, zf_ref, zb_ref, lb_ref, wf_ref, wb_ref, mf_ref, mb_ref,
                 oi_ref, qtf_ref, qtb_ref, utf_ref, utb_ref, df_ref, db_ref, *, cpb):
    c = REC_CHUNK
    dirs = ((zf_ref, wf_ref, mf_ref, qtf_ref, utf_ref, df_ref, 0, c - 1),
            (zb_ref, wb_ref, mb_ref, qtb_ref, utb_ref, db_ref, 1, 0))

    def chunk_group(cg, carry):
        cis = [cg * HGRN_UNROLL + u for u in range(HGRN_UNROLL)]
        rows = [pl.ds(pl.multiple_of(ci * c, c), c) for ci in cis]
        zqs = [q_ref[rw, :] for rw in rows]
        qs = [zq * jax.nn.sigmoid(zq) for zq in zqs]
        vbs = [i_ref[rw, :].astype(BF16) for rw in rows]
        units = [(u, d) for u in range(HGRN_UNROLL) for d in range(2)]
        ks, es = {}, {}
        for u, d in units:
            z_ref, w_ref = dirs[d][0], dirs[d][1]
            lb = lb_ref[d:d + 1, :]
            f = lb + (1.0 - lb) * jax.nn.sigmoid(z_ref[rows[u], :])
            ks[u, d] = 1.0 - f
            es[u, d] = jnp.exp(jnp.dot(w_ref[...], _split3(jnp.log(f)), preferred_element_type=F32))
        acc = [jnp.zeros((c, c), F32) for _ in range(HGRN_UNROLL)]
        for lvl in range(N_LEVELS + 1):
            for u, d in units:
                m_ref = dirs[d][2]
                if lvl < N_LEVELS:
                    el = es[u, d][lvl * c:(lvl + 1) * c]
                    qa, ka = (qs[u] * el).astype(BF16), (ks[u, d] * el).astype(BF16)
                else:
                    qa, ka = qs[u].astype(BF16), ks[u, d].astype(BF16)
                p = lax.dot_general(qa, ka, _NT, preferred_element_type=F32)
                acc[u] = acc[u] + p * m_ref[lvl]
        for u, d in units:
            _, _, _, qt_ref, ut_ref, d_ref, _, drow = dirs[d]
            e = es[u, d]
            qt_ref[rows[u], :] = (qs[u] * e[6 * c:7 * c]).astype(BF16)
            kt = (ks[u, d] * e[7 * c:8 * c]).astype(BF16)
            ut_ref[cis[u]] = lax.dot_general(vbs[u], kt, _TN, preferred_element_type=F32)
            d_ref[pl.ds(cis[u], 1), :] = e[6 * c + drow:6 * c + drow + 1]
        for u in range(HGRN_UNROLL):
            oi_ref[rows[u], :] = jnp.dot(acc[u].astype(BF16), vbs[u], preferred_element_type=F32)
        return carry

    lax.fori_loop(0, cpb // HGRN_UNROLL, chunk_group, 0)


def _hgrn_a(proj, lb, col, t, d_rec):
    heads = d_rec // REC_HEAD_DIM
    tq = 512
    cpb = tq // REC_CHUNK
    nchunks = t // REC_CHUNK
    wf, wb, mf, mb = _hgrn_consts()
    hd = REC_HEAD_DIM

    def colspec(off):
        return pl.BlockSpec((tq, hd), lambda i, h: (i, off + h))

    full2 = lambda i, h: (0, 0)
    full3 = lambda i, h: (0, 0, 0)
    row_spec = pl.BlockSpec((tq, hd), lambda i, h: (i, h))
    u_spec = pl.BlockSpec((cpb, hd, hd), lambda i, h: (i, 0, h))
    d_spec = pl.BlockSpec((cpb, hd), lambda i, h: (i, h))
    return pl.pallas_call(
        functools.partial(_hgrn_a_body, cpb=cpb),
        grid=(t // tq, heads),
        in_specs=[colspec(col["q_r"]), colspec(col["i_r"]), colspec(col["zf_f"]), colspec(col["zf_b"]),
                  pl.BlockSpec((2, hd), lambda i, h: (0, h)),
                  pl.BlockSpec(wf.shape, full2), pl.BlockSpec(wb.shape, full2),
                  pl.BlockSpec(mf.shape, full3), pl.BlockSpec(mb.shape, full3)],
        out_specs=[row_spec, row_spec, row_spec, u_spec, u_spec, d_spec, d_spec],
        out_shape=[jax.ShapeDtypeStruct((t, d_rec), F32),
                   jax.ShapeDtypeStruct((t, d_rec), BF16),
                   jax.ShapeDtypeStruct((t, d_rec), BF16),
                   jax.ShapeDtypeStruct((nchunks, hd, d_rec), F32),
                   jax.ShapeDtypeStruct((nchunks, hd, d_rec), F32),
                   jax.ShapeDtypeStruct((nchunks, d_rec), F32),
                   jax.ShapeDtypeStruct((nchunks, d_rec), F32)],
        compiler_params=_cparams(("arbitrary", "arbitrary"), 32),
        name="hgrn_a",
    )(proj, proj, proj, proj, lb, wf, wb, mf, mb)


def _hgrn_c_body(oi_ref, qtf_ref, qtb_ref, utf_ref, utb_ref, df_ref, db_ref, z_ref, g_ref,
                 out_ref, acc_ref, *, nchunks):
    c = REC_CHUNK
    hd = REC_HEAD_DIM

    def fwd(n, st):
        rows = pl.ds(pl.multiple_of(n * c, c), c)
        o = lax.dot_general(qtf_ref[rows, :], st.astype(BF16), _NT, preferred_element_type=F32)
        acc_ref[rows, :] = oi_ref[rows, :] + o
        return df_ref[pl.ds(n, 1), :] * st + utf_ref[n]

    lax.fori_loop(0, nchunks, fwd, jnp.zeros((hd, hd), F32), unroll=4)

    def bwd(i, st):
        n = nchunks - 1 - i
        rows = pl.ds(pl.multiple_of(n * c, c), c)
        o = lax.dot_general(qtb_ref[rows, :], st.astype(BF16), _NT, preferred_element_type=F32)
        acc_ref[rows, :] = acc_ref[rows, :] + o
        return db_ref[pl.ds(n, 1), :] * st + utb_ref[n]

    lax.fori_loop(0, nchunks, bwd, jnp.zeros((hd, hd), F32), unroll=4)

    o = _rms(acc_ref[...])
    out_ref[...] = (o * g_ref[...] * jax.nn.sigmoid(z_ref[...])).astype(BF16)


def _hgrn_c(oi, qtf, qtb, utf, utb, df, db, proj, g_out, col, bsz, seq, d_rec):
    heads = d_rec // REC_HEAD_DIM
    hd = REC_HEAD_DIM
    nchunks = seq // REC_CHUNK
    row_spec = pl.BlockSpec((seq, hd), lambda b, h: (b, h))
    u_spec = pl.BlockSpec((nchunks, hd, hd), lambda b, h: (b, 0, h))
    d_spec = pl.BlockSpec((nchunks, hd), lambda b, h: (b, h))
    zo = col["z_o"]
    return pl.pallas_call(
        functools.partial(_hgrn_c_body, nchunks=nchunks),
        grid=(bsz, heads),
        in_specs=[row_spec, row_spec, row_spec, u_spec, u_spec, d_spec, d_spec,
                  pl.BlockSpec((seq, hd), lambda b, h: (b, zo + h)),
                  pl.BlockSpec((1, hd), lambda b, h: (0, h))],
        out_specs=row_spec,
        out_shape=jax.ShapeDtypeStruct((bsz * seq, d_rec), BF16),
        scratch_shapes=[pltpu.VMEM((seq, hd), F32)],
        compiler_params=_cparams(("arbitrary", "arbitrary"), 48),
        name="hgrn_c",
    )(oi, qtf, qtb, utf, utb, df, db, proj, g_out.reshape(1, d_rec))


def _t5_bucket(rel):
    half_buckets = NUM_BUCKETS // 2
    ret = np.where(rel > 0, half_buckets, 0)
    n = np.abs(rel)
    max_exact = half_buckets // 2
    nf = np.maximum(n, 1).astype(np.float32)
    large = max_exact + (np.log(nf / np.float32(max_exact)) / np.float32(math.log(MAX_DISTANCE / max_exact))
                         * np.float32(half_buckets - max_exact)).astype(np.int32)
    large = np.minimum(large, half_buckets - 1)
    return ret + np.where(n < max_exact, n, large)


def _band_bias(rel_bias_g, window, dil):
    half = window // (2 * dil)
    q_off = np.arange(ATT_BLOCK)[:, None]
    rel = np.arange(3 * ATT_BLOCK)[None, :] - ATT_BLOCK - q_off
    onehot = (_t5_bucket(rel * dil)[..., None] == np.arange(NUM_BUCKETS)).astype(np.float32)
    bias = jnp.einsum("qkb,bh->hqk", jnp.asarray(onehot), rel_bias_g.astype(F32),
                      precision=lax.Precision.HIGHEST)
    return jnp.where(jnp.asarray(np.abs(rel) <= half)[None], bias, NEG_INF)


def _attn_body(q_ref, kp_ref, k_ref, kn_ref, vp_ref, v_ref, vn_ref, bias_ref,
               num_ref, st_ref, kc_ref, vc_ref, *, dil, tq, sub_len):
    blk = ATT_BLOCK
    nqb = tq // blk
    n = pl.program_id(1)
    scale = ATT_HEAD_DIM ** -0.5

    def sds(start, size):
        if dil == 1:
            return pl.ds(start, size)
        return pl.ds(start, size, stride=dil)

    cu = kc_ref.shape[0]
    qu = ATT_UNROLL // cu

    def deinterleave(r, j):
        kc_ref[j, 0:blk, :] = kp_ref[sds(r, blk), :].astype(BF16)
        kc_ref[j, blk:blk + tq, :] = k_ref[sds(r, tq), :].astype(BF16)
        kc_ref[j, blk + tq:2 * blk + tq, :] = kn_ref[sds(r, blk), :].astype(BF16)
        vc_ref[j, 0:blk, :] = vp_ref[sds(r, blk), :].astype(BF16)
        vc_ref[j, blk:blk + tq, :] = v_ref[sds(r, tq), :].astype(BF16)
        vc_ref[j, blk + tq:2 * blk + tq, :] = vn_ref[sds(r, blk), :].astype(BF16)

    def units(r0, qb0):
        us = [(j, u) for j in range(cu) for u in range(qu)]
        q0s = [pl.multiple_of((qb0 + u) * blk, blk) for _, u in us]
        rows = [sds(r0 + j + dil * q0, blk) for (j, _), q0 in zip(us, q0s)]
        lane = lax.broadcasted_iota(jnp.int32, (blk, LANE), 1)
        key_iota = lax.broadcasted_iota(jnp.int32, (1, 3 * blk), 1)
        bias = bias_ref[0]
        qs = [q_ref[rw, :].astype(BF16) for rw in rows]
        kws = [kc_ref[j, pl.ds(q0, 3 * blk), :] for (j, _), q0 in zip(us, q0s)]
        vws = [vc_ref[j, pl.ds(q0, 3 * blk), :] for (j, _), q0 in zip(us, q0s)]
        ss = [lax.dot_general(q, kw, _NT, preferred_element_type=F32) * scale for q, kw in zip(qs, kws)]
        valids = []
        for q0 in q0s:
            kpos = n * tq + q0 - blk + key_iota
            valids.append((kpos >= 0) & (kpos < sub_len))
        ss = [jnp.where(valid, s + bias, NEG_INF) for s, valid in zip(ss, valids)]
        ms = [jnp.max(s, axis=-1, keepdims=True) for s in ss]
        ps = [jnp.exp(s - m) for s, m in zip(ss, ms)]
        ls = [jnp.sum(p, axis=-1, keepdims=True) for p in ps]
        nums = [jnp.dot(p.astype(BF16), vw, preferred_element_type=F32) for p, vw in zip(ps, vws)]
        for rw, num, m, l in zip(rows, nums, ms, ls):
            num_ref[rw, :] = num
            st_ref[rw, :] = jnp.where(lane < LANE // 2, m, l)

    def class_group(rg, carry):
        r0 = rg * cu
        for j in range(cu):
            deinterleave(r0 + j, j)

        def qgroup(qg, carry2):
            units(r0, qg * qu)
            return carry2

        lax.fori_loop(0, nqb // qu, qgroup, 0)
        return carry

    lax.fori_loop(0, dil // cu, class_group, 0)


def _attn_group(proj, bias, col, g, dil, bsz, seq):
    tile = 1024
    tq = tile // dil
    halo = ATT_BLOCK * dil
    sub_len = seq // dil
    cu = ATT_UNROLL // min(tq // ATT_BLOCK, ATT_UNROLL)
    nh = ATT_HEADS_PER_GROUP
    hd = ATT_HEAD_DIM
    qc = col["q_a"] + g * nh
    kc = col["k_a"] + g * nh
    vc = col["v_a"] + g * nh
    tiles_b = seq // tile
    halos_b = seq // halo
    hpt = tile // halo

    own = lambda c: pl.BlockSpec((tile, hd), lambda b, n, h: (b * tiles_b + n, c + h))
    prev = lambda c: pl.BlockSpec(
        (halo, hd), lambda b, n, h: (b * halos_b + jnp.maximum(n * hpt - 1, 0), c + h))
    nxt = lambda c: pl.BlockSpec(
        (halo, hd), lambda b, n, h: (b * halos_b + jnp.minimum((n + 1) * hpt, halos_b - 1), c + h))
    t = bsz * seq
    return pl.pallas_call(
        functools.partial(_attn_body, dil=dil, tq=tq, sub_len=sub_len),
        grid=(bsz, tiles_b, nh),
        in_specs=[own(qc), prev(kc), own(kc), nxt(kc), prev(vc), own(vc), nxt(vc),
                  pl.BlockSpec((1,) + bias.shape[1:], lambda b, n, h: (h, 0, 0))],
        out_specs=[pl.BlockSpec((tile, hd), lambda b, n, h: (b * tiles_b + n, h)),
                   pl.BlockSpec((tile, LANE), lambda b, n, h: (b * tiles_b + n, h))],
        out_shape=[jax.ShapeDtypeStruct((t, nh * hd), F32), jax.ShapeDtypeStruct((t, nh * LANE), F32)],
        scratch_shapes=[pltpu.VMEM((cu, tq + 2 * ATT_BLOCK, hd), BF16),
                        pltpu.VMEM((cu, tq + 2 * ATT_BLOCK, hd), BF16)],
        compiler_params=_cparams(("arbitrary", "arbitrary", "arbitrary"), 32),
        name=f"attn_d{dil}",
    )(proj, proj, proj, proj, proj, proj, proj, bias)


def _merge_body(rec_ref, n0_ref, n1_ref, n2_ref, s0_ref, s1_ref, s2_ref, zgr_ref, zga_ref, x_ref,
                ada_ref, gpost_ref, gpre_ref, wbr_ref, wba_ref, wo_ref, wr_ref, br_ref,
                x1_ref, h2_ref, lg_ref):
    nh = ATT_HEADS_PER_GROUP
    hd = ATT_HEAD_DIM
    half = LANE // 2
    lane = lax.broadcasted_iota(jnp.int32, (rec_ref.shape[0], LANE), 1)
    heads = []
    for h in range(nh):
        cols = slice(h * hd, (h + 1) * hd)
        st = [s[:, cols] for s in (s0_ref, s1_ref, s2_ref)]
        top = jnp.maximum(jnp.maximum(st[0], st[1]), st[2])
        ws = [jnp.exp(s - top) for s in st]
        den = (ws[0] * pltpu.roll(st[0], half, 1) + ws[1] * pltpu.roll(st[1], half, 1)
               + ws[2] * pltpu.roll(st[2], half, 1))
        coef = [w / den for w in ws]
        coef = [jnp.where(lane < half, c, pltpu.roll(c, half, 1)) for c in coef]
        num = coef[0] * n0_ref[:, cols] + coef[1] * n1_ref[:, cols] + coef[2] * n2_ref[:, cols]
        heads.append(num.astype(BF16))
    att = jnp.concatenate(heads, axis=1)
    y_rec = jnp.dot(rec_ref[...], wbr_ref[...], preferred_element_type=F32)
    y_att = jnp.dot(att, wba_ref[...], preferred_element_type=F32)
    merged = jax.nn.sigmoid(zgr_ref[...]) * y_rec + jax.nn.sigmoid(zga_ref[...]) * y_att
    y = jnp.dot(merged.astype(BF16), wo_ref[...], preferred_element_type=F32)
    gt_m = ada_ref[0, 2:3, :]
    sh_f = ada_ref[0, 3:4, :]
    sc_f = ada_ref[0, 4:5, :]
    x1 = x_ref[...] + gt_m * (_rms(y) * gpost_ref[...])
    x1_ref[...] = x1
    h2 = _rms(x1) * gpre_ref[...] * (1.0 + sc_f) + sh_f
    tm = h2.shape[0]
    for s in range(SLAB_ROWS):
        h2_ref[pl.ds(s, tm, stride=SLAB_ROWS), :] = h2[:, s * LANE:(s + 1) * LANE]
    ne = lg_ref.shape[1]
    h_hi = h2.astype(BF16)
    h_lo = (h2 - h_hi.astype(F32)).astype(BF16)
    both = jnp.dot(h_hi, wr_ref[...], preferred_element_type=F32)
    cross = jnp.dot(h_lo, wr_ref[:, 0:ne], preferred_element_type=F32)
    lg_ref[...] = both[:, 0:ne] + both[:, ne:2 * ne] + cross + br_ref[...]


def _merge(rec_o, nums, stats, proj, x2, ada3, g_post, g_pre, wbr, wba, wo, w_router, b_router, col, seq):
    t, d = x2.shape
    tm = 256
    per_b = seq // tm
    d_rec = rec_o.shape[1]
    w_att = nums[0].shape[1]
    ne = w_router.shape[1]
    wr_hi = w_router.astype(BF16)
    wr_lo = (w_router - wr_hi.astype(F32)).astype(BF16)
    w_router = jnp.concatenate([wr_hi, wr_lo], axis=1)
    dl = d // LANE
    row = lambda w: pl.BlockSpec((tm, w), lambda i: (i, 0))
    const = lambda shape: pl.BlockSpec(shape, lambda i: (0,) * len(shape), pipeline_mode=pl.Buffered(1))
    zgr = col["zg_rec"] // dl
    zga = col["zg_att"] // dl
    return pl.pallas_call(
        _merge_body,
        grid=(t // tm,),
        in_specs=[row(d_rec), row(w_att), row(w_att), row(w_att), row(w_att), row(w_att), row(w_att),
                  pl.BlockSpec((tm, d), lambda i: (i, zgr)),
                  pl.BlockSpec((tm, d), lambda i: (i, zga)),
                  row(d),
                  pl.BlockSpec((1, 6, d), lambda i: (i // per_b, 0, 0)),
                  const((1, d)), const((1, d)),
                  const(wbr.shape), const(wba.shape), const(wo.shape), const(w_router.shape),
                  const((1, ne))],
        out_specs=[row(d), pl.BlockSpec((tm * SLAB_ROWS, LANE), lambda i: (i, 0)),
                   pl.BlockSpec((tm, ne), lambda i: (i, 0))],
        out_shape=[jax.ShapeDtypeStruct((t, d), F32), jax.ShapeDtypeStruct((t * SLAB_ROWS, LANE), F32),
                   jax.ShapeDtypeStruct((t, ne), F32)],
        compiler_params=_cparams(("arbitrary",), 56),
        name="merge",
    )(rec_o, nums[0], nums[1], nums[2], stats[0], stats[1], stats[2], proj, proj, x2, ada3,
      g_post.reshape(1, d), g_pre.reshape(1, d), wbr, wba, wo, w_router, b_router.reshape(1, ne))


def _route_body(lg_ref, tri_ref, rt_ref, cnt_ref, carry_ref):
    i = pl.program_id(0)
    tr, ne = lg_ref.shape

    @pl.when(i == 0)
    def _():
        carry_ref[...] = jnp.zeros_like(carry_ref)

    l = lg_ref[...]
    lane = lax.broadcasted_iota(jnp.int32, (tr, ne), 1).astype(F32)
    vals, sels, idxs = [], [], []
    for _ in range(TOP_K):
        m = jnp.max(l, axis=-1, keepdims=True)
        idx = jnp.min(jnp.where(l == m, lane, float(ne)), axis=-1, keepdims=True)
        sel = lane == idx
        vals.append(m)
        idxs.append(idx)
        sels.append(sel)
        l = jnp.where(sel, -jnp.inf, l)
    es = [jnp.exp(v - vals[0]) for v in vals]
    tot = es[0] + es[1] + es[2] + es[3]
    chosen = (sels[0] | sels[1] | sels[2] | sels[3]).astype(F32)
    prefix = jnp.dot(tri_ref[...], chosen.astype(BF16), preferred_element_type=F32) + carry_ref[0:1, :]
    out_lane = lax.broadcasted_iota(jnp.int32, (tr, LANE), 1)
    rt = jnp.zeros((tr, LANE), F32)
    for k in range(TOP_K):
        rank = jnp.sum(jnp.where(sels[k], prefix, 0.0), axis=-1, keepdims=True)
        rt = jnp.where(out_lane == k, idxs[k], rt)
        rt = jnp.where(out_lane == TOP_K + k, es[k] / tot, rt)
        rt = jnp.where(out_lane == 2 * TOP_K + k, rank, rt)
    rt_ref[...] = rt
    new = carry_ref[0:1, :] + jnp.sum(chosen, axis=0, keepdims=True)
    carry_ref[...] = jnp.broadcast_to(new, carry_ref.shape)
    cnt_ref[...] = carry_ref[...]


def _route(logits):
    t, ne = logits.shape
    tr = 512
    tri = jnp.asarray(np.tril(np.ones((tr, tr), np.float32), -1), BF16)
    return pl.pallas_call(
        _route_body,
        grid=(t // tr,),
        in_specs=[pl.BlockSpec((tr, ne), lambda i: (i, 0)),
                  pl.BlockSpec((tr, tr), lambda i: (0, 0))],
        out_specs=[pl.BlockSpec((tr, LANE), lambda i: (i, 0)),
                   pl.BlockSpec((SUBLANE, ne), lambda i: (0, 0))],
        out_shape=[jax.ShapeDtypeStruct((t, LANE), F32), jax.ShapeDtypeStruct((SUBLANE, ne), F32)],
        scratch_shapes=[pltpu.VMEM((SUBLANE, ne), F32)],
        compiler_params=_cparams(("arbitrary",), 32),
        name="route",
    )(logits, tri)


def _chunk_rows(ncols):
    return W_CHUNK_BYTES // (4 * ncols)


def _weight_stream(w_hbm, wbf, stage, wsem, e, slot, c0, c1):
    kc = stage.shape[1]

    def copy(c):
        return pltpu.make_async_copy(w_hbm.at[e, pl.ds(pl.multiple_of(c * kc, kc), kc), :],
                                     stage.at[c % 2], wsem.at[c % 2])

    def prime():
        def body(c, carry):
            copy(c).start()
            return carry
        lax.fori_loop(c0, jnp.minimum(c0 + 2, c1), body, 0)

    def finish():
        def body(c, carry):
            copy(c).wait()
            buf = c % 2

            def cast(i, carry2):
                r = pl.multiple_of(i * CAST_ROWS, CAST_ROWS)
                wbf[slot, pl.ds(pl.multiple_of(c * kc, kc) + r, CAST_ROWS), :] = (
                    stage[buf, pl.ds(r, CAST_ROWS), :].astype(BF16))
                return carry2
            lax.fori_loop(0, kc // CAST_ROWS, cast, 0)

            @pl.when(c + 2 < c1)
            def _():
                copy(c + 2).start()
            return carry
        lax.fori_loop(c0, c1, body, 0)

    return prime, finish


def _moe_up_body(be_ref, nu_ref, ws_ref, wn_ref, wc0_ref, wc1_ref, idx0_ref, idxn_ref, h2s_ref, w_hbm, bias_ref,
                 o_ref, xbuf, wbf, stage, sem, wsem):
    b = pl.program_id(0)
    nu = nu_ref[0]
    f = o_ref.shape[1]
    slab = SLAB_ROWS
    nch = wbf.shape[1] // stage.shape[1]

    def issue(idx_ref, slot):
        def body(r, c):
            tok = idx_ref[0, 0, r]
            pltpu.make_async_copy(h2s_ref.at[pl.ds(pl.multiple_of(tok * slab, slab), slab), :],
                                  xbuf.at[slot, pl.ds(pl.multiple_of(r * slab, slab), slab), :],
                                  sem.at[slot]).start()
            return c
        lax.fori_loop(0, BM, body, 0, unroll=DMA_UNROLL)

    @pl.when(b == 0)
    def _():
        issue(idx0_ref, 0)
        prime0, finish0 = _weight_stream(w_hbm, wbf, stage, wsem, be_ref[0], ws_ref[0], 0, nch)
        prime0()
        finish0()

    @pl.when(b + 1 < nu)
    def _():
        issue(idxn_ref, (b + 1) % 2)

    @pl.when(b < nu)
    def _():
        wslot = ws_ref[b]
        prime, finish = _weight_stream(w_hbm, wbf, stage, wsem, wn_ref[b], 1 - wslot, wc0_ref[b], wc1_ref[b])
        prime()
        slot = b % 2
        pltpu.make_async_copy(h2s_ref.at[pl.ds(0, BM * slab), :], xbuf.at[slot], sem.at[slot]).wait()
        x = jnp.concatenate([xbuf[slot, pl.ds(s, BM, stride=slab), :].astype(BF16) for s in range(slab)],
                            axis=1)
        half = f // 2
        for c0 in (0, half):
            gate = (jnp.dot(x, wbf[wslot, :, c0:c0 + half], preferred_element_type=F32)
                    + bias_ref[0, :, c0:c0 + half])
            up = (jnp.dot(x, wbf[wslot, :, f + c0:f + c0 + half], preferred_element_type=F32)
                  + bias_ref[0, :, f + c0:f + c0 + half])
            gate = jnp.minimum(gate, SWIGLU_LIMIT)
            up = jnp.clip(up, -SWIGLU_LIMIT, SWIGLU_LIMIT)
            o_ref[:, c0:c0 + half] = (gate * jax.nn.sigmoid(SWIGLU_ALPHA * gate) * (up + 1.0)).astype(BF16)
        finish()

    @pl.when(b >= nu)
    def _():
        o_ref[...] = jnp.zeros_like(o_ref)


def _moe_up(h2s, src_tok, wgu, bgu, sched):
    ne, d, f2 = wgu.shape
    f = f2 // 2
    p = src_tok.shape[0]
    nb = p // BM
    idx3 = src_tok.reshape(nb, 1, BM)
    smem_blk = lambda imap: pl.BlockSpec((1, 1, BM), imap, memory_space=pltpu.SMEM)
    grid_spec = pltpu.PrefetchScalarGridSpec(
        num_scalar_prefetch=6,
        grid=(nb,),
        in_specs=[smem_blk(lambda b, be, *_: (0, 0, 0)),
                  smem_blk(lambda b, be, *_: (jnp.minimum(b + 1, nb - 1), 0, 0)),
                  pl.BlockSpec(memory_space=pl.ANY),
                  pl.BlockSpec(memory_space=pl.ANY),
                  pl.BlockSpec((1, 1, f2), lambda b, be, *_: (be[b], 0, 0))],
        out_specs=pl.BlockSpec((BM, f), lambda b, be, *_: (b, 0)),
        scratch_shapes=[pltpu.VMEM((2, BM * SLAB_ROWS, LANE), F32),
                        pltpu.VMEM((2, d, f2), BF16),
                        pltpu.VMEM((2, _chunk_rows(f2), f2), F32),
                        pltpu.SemaphoreType.DMA((2,)), pltpu.SemaphoreType.DMA((2,))],
    )
    return pl.pallas_call(
        _moe_up_body,
        grid_spec=grid_spec,
        out_shape=jax.ShapeDtypeStruct((p, f), BF16),
        compiler_params=_cparams(("arbitrary",), 58),
        name="moe_up",
    )(sched["blk_e"], sched["n_used"], sched["wslot"], sched["wnext"], sched["wc0"], sched["wc1"],
      idx3, idx3, h2s, wgu, bgu.reshape(ne, 1, f2))


def _moe_down_body(be_ref, nu_ref, ws_ref, wn_ref, wc0_ref, wc1_ref, nv_ref, dst_ref, a_ref, w_hbm, bias_ref,
                   ysc_ref, ybuf, wbf, stage, sem, wsem):
    b = pl.program_id(0)
    nb = pl.num_programs(0)
    nu = nu_ref[0]
    slab = SLAB_ROWS
    nch = wbf.shape[1] // stage.shape[1]

    @pl.when(b == 0)
    def _():
        prime0, finish0 = _weight_stream(w_hbm, wbf, stage, wsem, be_ref[0], ws_ref[0], 0, nch)
        prime0()
        finish0()

    def row_copy(slot, r, d):
        return pltpu.make_async_copy(ybuf.at[slot, pl.ds(pl.multiple_of(r * slab, slab), slab), :],
                                     ysc_ref.at[pl.ds(pl.multiple_of(d * slab, slab), slab), :],
                                     sem.at[slot])

    def drain(step):
        slot = step % 2
        count = nv_ref[step]

        @pl.when(count == BM)
        def _():
            pltpu.make_async_copy(ybuf.at[slot], ysc_ref.at[pl.ds(0, BM * slab), :], sem.at[slot]).wait()

        @pl.when(count < BM)
        def _():
            def body(r, c):
                row_copy(slot, 0, 0).wait()
                return c
            lax.fori_loop(0, count, body, 0)

    @pl.when((b >= 2) & (b < nu))
    def _():
        drain(b - 2)

    @pl.when(b < nu)
    def _():
        slot = b % 2
        wslot = ws_ref[b]
        prime, finish = _weight_stream(w_hbm, wbf, stage, wsem, wn_ref[b], 1 - wslot, wc0_ref[b], wc1_ref[b])
        prime()
        y = jnp.dot(a_ref[...], wbf[wslot], preferred_element_type=F32) + bias_ref[0]
        for s in range(slab):
            ybuf[slot, pl.ds(s, BM, stride=slab), :] = y[:, s * LANE:(s + 1) * LANE]
        finish()

        def body(r, c):
            row_copy(slot, r, dst_ref[0, 0, r]).start()
            return c

        @pl.when(nv_ref[b] == BM)
        def _():
            lax.fori_loop(0, BM, body, 0, unroll=DMA_UNROLL)

        @pl.when(nv_ref[b] < BM)
        def _():
            lax.fori_loop(0, nv_ref[b], body, 0)

    @pl.when(b == nb - 1)
    def _():
        @pl.when(nu >= 2)
        def _():
            drain(nu - 2)
        drain(nu - 1)


def _moe_down(act, dst_slot, n_out_rows, wd, bd, sched):
    p, f = act.shape
    ne, _, d = wd.shape
    nb = p // BM
    grid_spec = pltpu.PrefetchScalarGridSpec(
        num_scalar_prefetch=7,
        grid=(nb,),
        in_specs=[pl.BlockSpec((1, 1, BM), lambda b, be, *_: (b, 0, 0), memory_space=pltpu.SMEM),
                  pl.BlockSpec((BM, f), lambda b, be, *_: (b, 0)),
                  pl.BlockSpec(memory_space=pl.ANY),
                  pl.BlockSpec((1, 1, d), lambda b, be, *_: (be[b], 0, 0))],
        out_specs=pl.BlockSpec(memory_space=pl.ANY),
        scratch_shapes=[pltpu.VMEM((2, BM * SLAB_ROWS, LANE), F32),
                        pltpu.VMEM((2, f, d), BF16),
                        pltpu.VMEM((2, _chunk_rows(d), d), F32),
                        pltpu.SemaphoreType.DMA((2,)), pltpu.SemaphoreType.DMA((2,))],
    )
    return pl.pallas_call(
        _moe_down_body,
        grid_spec=grid_spec,
        out_shape=jax.ShapeDtypeStruct((n_out_rows * SLAB_ROWS, LANE), F32),
        compiler_params=pltpu.CompilerParams(dimension_semantics=("arbitrary",),
                                             vmem_limit_bytes=48 * 1024 * 1024, has_side_effects=True),
        name="moe_down",
    )(sched["blk_e"], sched["n_used"], sched["wslot"], sched["wnext"], sched["wc0"], sched["wc1"],
      sched["nvalid"], dst_slot.reshape(nb, 1, BM), act, wd, bd.reshape(ne, 1, d))


def _final_body(ysc_ref, rt_ref, x1_ref, ada_ref, g_ref, o_ref):
    tm = x1_ref.shape[0]
    rt = rt_ref[...]
    stride = TOP_K * SLAB_ROWS
    pieces = []
    for s in range(SLAB_ROWS):
        acc = rt[:, TOP_K:TOP_K + 1] * ysc_ref[pl.ds(s, tm, stride=stride), :]
        for k in range(1, TOP_K):
            acc = acc + rt[:, TOP_K + k:TOP_K + k + 1] * ysc_ref[pl.ds(k * SLAB_ROWS + s, tm, stride=stride), :]
        pieces.append(acc)
    y = jnp.concatenate(pieces, axis=1)
    gt_f = ada_ref[0, 5:6, :]
    o_ref[...] = x1_ref[...] + gt_f * (_rms(y) * g_ref[...])


def _final(ysc, rt, x1, ada3, g_post, seq):
    t, d = x1.shape
    tm = 256
    per_b = seq // tm
    return pl.pallas_call(
        _final_body,
        grid=(t // tm,),
        in_specs=[pl.BlockSpec((tm * TOP_K * SLAB_ROWS, LANE), lambda i: (i, 0)),
                  pl.BlockSpec((tm, LANE), lambda i: (i, 0)),
                  pl.BlockSpec((tm, d), lambda i: (i, 0)),
                  pl.BlockSpec((1, 6, d), lambda i: (i // per_b, 0, 0)),
                  pl.BlockSpec((1, d), lambda i: (0, 0))],
        out_specs=pl.BlockSpec((tm, d), lambda i: (i, 0)),
        out_shape=jax.ShapeDtypeStruct((t, d), F32),
        compiler_params=_cparams(("arbitrary",), 48),
        name="final",
    )(ysc, rt, x1, ada3, g_post.reshape(1, d))


def _mixer_ffn_layer(x2, ada3, bsz, seq, g_mix_pre, g_mix_post, g_ffn_pre, g_ffn_post, w_in, lb, g_rec_out,
                     rel_bias, w_branch_rec, w_branch_att, w_o, w_router, b_router, w_gate_up, b_gate_up,
                     w_down, b_down):
    t, d = x2.shape
    d_rec = w_branch_rec.shape[0]
    w_att = w_branch_att.shape[0]
    d_att = 3 * w_att
    widths = dict(q_r=d_rec, i_r=d_rec, zf_f=d_rec, zf_b=d_rec, z_o=d_rec, q_a=d_att, k_a=d_att, v_a=d_att,
                  zg_rec=d, zg_att=d)
    ref_order = ("q_r", "i_r", "zf_f", "zf_b", "z_o", "q_a", "k_a", "v_a", "zg_rec", "zg_att")
    my_order = ("zg_rec", "zg_att", "q_r", "i_r", "zf_f", "zf_b", "z_o", "q_a", "k_a", "v_a")
    ref_off, acc = {}, 0
    for name in ref_order:
        ref_off[name] = acc
        acc += widths[name]
    col, acc = {}, 0
    for name in my_order:
        col[name] = acc // LANE
        acc += widths[name]
    w_in_bf = jnp.concatenate([w_in[:, ref_off[nm]:ref_off[nm] + widths[nm]].astype(BF16) for nm in my_order],
                              axis=1)

    proj = _inproj(x2, g_mix_pre, ada3, w_in_bf, seq)

    oi, qtf, qtb, utf, utb, df, db = _hgrn_a(proj, lb, col, t, d_rec)
    rec_o = _hgrn_c(oi, qtf, qtb, utf, utb, df, db, proj, g_rec_out, col, bsz, seq, d_rec)

    nums, stats = [], []
    for g, (window, dil) in enumerate(DIL_GROUPS):
        hs = slice(g * ATT_HEADS_PER_GROUP, (g + 1) * ATT_HEADS_PER_GROUP)
        bias = _band_bias(rel_bias[:, hs], window, dil)
        num, st = _attn_group(proj, bias, col, g, dil, bsz, seq)
        nums.append(num)
        stats.append(st)

    x1, h2s, logits = _merge(rec_o, nums, stats, proj, x2, ada3, g_mix_post, g_ffn_pre,
                            w_branch_rec.astype(BF16), w_branch_att.astype(BF16), w_o.astype(BF16),
                            w_router, b_router, col, seq)

    rt, cnt = _route(logits)
    ne = logits.shape[1]
    counts = cnt[0].astype(jnp.int32)
    top_idx = rt[:, 0:TOP_K].astype(jnp.int32)
    rank = rt[:, 2 * TOP_K:3 * TOP_K].astype(jnp.int32)
    padded = (counts + BM - 1) // BM * BM
    pends = jnp.cumsum(padded)
    pstarts = pends - padded
    experts = jnp.arange(ne, dtype=jnp.int32)
    pstart_sel = jnp.sum(jnp.where(top_idx[..., None] == experts, pstarts, 0), axis=-1)
    dest = (pstart_sel + rank).reshape(-1)
    p_rows = t * TOP_K + ne * BM
    nb = p_rows // BM
    blk_start = jnp.arange(nb, dtype=jnp.int32) * BM
    blk_e = jnp.minimum(jnp.sum((pends[None, :] <= blk_start[:, None]).astype(jnp.int32), axis=1), ne - 1)
    n_used = (pends[-1:] // BM).astype(jnp.int32)

    n_assign = t * TOP_K
    slot_assign = jnp.full((p_rows,), -1, jnp.int32).at[dest].set(jnp.arange(n_assign, dtype=jnp.int32))
    src_tok = jnp.maximum(slot_assign, 0) // TOP_K
    valid_end = pstarts + counts
    blk_end = jnp.sum(jnp.where(blk_e[:, None] == experts, valid_end, 0), axis=-1)
    nvalid = jnp.clip(blk_end - blk_start, 0, BM).astype(jnp.int32)
    nvalid = jnp.where(jnp.arange(nb) < n_used[0], nvalid, 0)

    lookup = lambda table: jnp.sum(jnp.where(blk_e[:, None] == experts, table, 0), axis=-1)
    nonempty = padded > 0
    order = jnp.cumsum(nonempty.astype(jnp.int32)) - 1
    later = lax.cummin(jnp.where(nonempty, experts, ne)[::-1])[::-1]
    next_e = jnp.concatenate([later[1:], jnp.full((1,), ne, jnp.int32)])
    blk_next = lookup(next_e)
    has_next = (blk_next < ne) & (jnp.arange(nb) < n_used[0])
    k_in_run = jnp.arange(nb, dtype=jnp.int32) - lookup(pstarts // BM)
    n_in_run = jnp.maximum(lookup(padded // BM), 1)
    common = dict(blk_e=blk_e, n_used=n_used, nvalid=nvalid, wslot=lookup(order) % 2,
                  wnext=jnp.where(has_next, blk_next, blk_e))

    def schedule(w):
        nch = w.shape[1] // _chunk_rows(w.shape[2])
        s = dict(common, wc0=jnp.where(has_next, k_in_run * nch // n_in_run, 0),
                 wc1=jnp.where(has_next, (k_in_run + 1) * nch // n_in_run, 0))
        return {k: v.astype(jnp.int32) for k, v in s.items()}

    act = _moe_up(h2s, src_tok, w_gate_up, b_gate_up, schedule(w_gate_up))
    ysc = _moe_down(act, slot_assign, n_assign, w_down, b_down, schedule(w_down))
    return _final(ysc, rt, x1, ada3, g_ffn_post, seq)


def kernel(x, c, w_ada, b_ada, g_mix_pre, g_mix_post, g_ffn_pre, g_ffn_post, w_in, g_rec_out, w_branch_rec,
           w_branch_att, w_o, w_router, b_router, w_gate_up, b_gate_up, w_down, b_down, rec_lb_table, rel_bias):
    bsz, seq, d = x.shape
    depth = w_in.shape[0]
    lb_all = jnp.cumsum(jax.nn.softmax(rec_lb_table.astype(F32), axis=1), axis=1)
    x2 = x.reshape(bsz * seq, d)
    for layer in range(depth):
        ada3 = _ada(c, w_ada[layer], b_ada[layer]).reshape(bsz, 6, d)
        x2 = _mixer_ffn_layer(x2, ada3, bsz, seq, g_mix_pre[layer], g_mix_post[layer], g_ffn_pre[layer],
                              g_ffn_post[layer], w_in[layer], lb_all[:, layer], g_rec_out[layer], rel_bias,
                              w_branch_rec[layer], w_branch_att[layer], w_o[layer], w_router[layer],
                              b_router[layer], w_gate_up[layer], b_gate_up[layer], w_down[layer],
                              b_down[layer])
    return x2.reshape(bsz, seq, d)
```

```python
import functools
import math

import numpy as np
import jax
import jax.numpy as jnp
from jax import lax
from jax.experimental import pallas as pl
from jax.experimental.pallas import tpu as pltpu

F32 = jnp.float32
BF16 = jnp.bfloat16

LANE = 128
SUBLANE = 8
SLAB_ROWS = 16

REC_HEAD_DIM = 128
REC_CHUNK = 64
ATT_HEAD_DIM = 128
ATT_HEADS_PER_GROUP = 4
ATT_BLOCK = 64
DIL_GROUPS = ((128, 1), (512, 4), (2048, 16))
NUM_BUCKETS = 32
MAX_DISTANCE = 1024
N_EXPERTS = 32
TOP_K = 4
SWIGLU_LIMIT = 7.0
SWIGLU_ALPHA = 1.702
RMS_EPS = 1e-6
NEG_INF = -1e30

N_LEVELS = 6
W_CHUNK_BYTES = 4 * 1024 * 1024
CAST_ROWS = 32
HGRN_UNROLL = 4
DMA_UNROLL = 8
ATT_UNROLL = 8
BM = 256

_NT = (((1,), (1,)), ((), ()))
_TN = (((0,), (0,)), ((), ()))


def _cparams(sem, vmem_mb):
    return pltpu.CompilerParams(dimension_semantics=sem, vmem_limit_bytes=vmem_mb * 1024 * 1024)


def _rms(x):
    return x * lax.rsqrt(jnp.mean(x * x, axis=-1, keepdims=True) + RMS_EPS)


def _ada_body(c_ref, w_ref, b_ref, o_ref):
    c = c_ref[...]
    cond = (c * jax.nn.sigmoid(c)).astype(BF16)
    o_ref[...] = jnp.dot(cond, w_ref[...].astype(BF16), preferred_element_type=F32) + b_ref[...]


def _ada(c, w, b):
    bsz, d = c.shape
    n = w.shape[1]
    tn = 1024
    cp = jnp.zeros((SUBLANE, d), F32).at[:bsz].set(c)
    out = pl.pallas_call(
        _ada_body,
        grid=(n // tn,),
        in_specs=[pl.BlockSpec((SUBLANE, d), lambda j: (0, 0)),
                  pl.BlockSpec((d, tn), lambda j: (0, j)),
                  pl.BlockSpec((1, tn), lambda j: (0, j))],
        out_specs=pl.BlockSpec((SUBLANE, tn), lambda j: (0, j)),
        out_shape=jax.ShapeDtypeStruct((SUBLANE, n), F32),
        compiler_params=_cparams(("arbitrary",), 40),
        name="ada",
    )(cp, w, b.reshape(1, n))
    return out[:bsz]


def _inproj_body(x_ref, g_ref, ada_ref, w_ref, o_ref, h_ref):
    @pl.when(pl.program_id(1) == 0)
    def _():
        y = _rms(x_ref[...]) * g_ref[...]
        sh = ada_ref[0, 0:1, :]
        sc = ada_ref[0, 1:2, :]
        h_ref[...] = (y * (1.0 + sc) + sh).astype(BF16)

    o_ref[...] = jnp.dot(h_ref[...], w_ref[...], preferred_element_type=F32)


def _inproj(x2, g, ada3, w_bf, seq):
    t, d = x2.shape
    n = w_bf.shape[1]
    tm, tn = 1024, 768
    per_b = seq // tm
    return pl.pallas_call(
        _inproj_body,
        grid=(t // tm, n // tn),
        in_specs=[pl.BlockSpec((tm, d), lambda i, j: (i, 0)),
                  pl.BlockSpec((1, d), lambda i, j: (0, 0)),
                  pl.BlockSpec((1, 6, d), lambda i, j: (i // per_b, 0, 0)),
                  pl.BlockSpec((d, tn), lambda i, j: (0, j))],
        out_specs=pl.BlockSpec((tm, tn), lambda i, j: (i, j)),
        out_shape=jax.ShapeDtypeStruct((t, n), F32),
        scratch_shapes=[pltpu.VMEM((tm, d), BF16)],
        compiler_params=_cparams(("arbitrary", "arbitrary"), 48),
        name="inproj",
    )(x2, g.reshape(1, d), ada3, w_bf)


def _hgrn_consts():
    c = REC_CHUNK
    r = np.arange(c)[:, None]
    m = np.arange(c)[None, :]
    wf = np.zeros((8 * c, c), np.float32)
    wb = np.zeros((8 * c, c), np.float32)
    mf = np.zeros((N_LEVELS + 1, c, c), np.float32)
    for lvl in range(N_LEVELS):
        s = 32 >> lvl
        m0 = (r // (2 * s)) * (2 * s) + s
        up = r >= m0
        wf[lvl * c:(lvl + 1) * c] = np.where(up, (m >= m0) & (m <= r), (m > r) & (m <= m0 - 1))
        wb[lvl * c:(lvl + 1) * c] = np.where(up, (m >= m0) & (m <= r - 1), (m >= r) & (m <= m0 - 1))
        i = np.arange(c)[:, None]
        j = np.arange(c)[None, :]
        mf[lvl] = (i // (2 * s) == j // (2 * s)) & (i % (2 * s) >= s) & (j % (2 * s) < s)
    mf[N_LEVELS] = np.eye(c)
    wf[6 * c:7 * c] = m <= r
    wf[7 * c:8 * c] = m > r
    wb[6 * c:7 * c] = m >= r
    wb[7 * c:8 * c] = m < r
    mb = np.transpose(mf, (0, 2, 1)).copy()
    wf3 = np.concatenate([wf, wf, wf], axis=1)
    wb3 = np.concatenate([wb, wb, wb], axis=1)
    return (jnp.asarray(wf3, BF16), jnp.asarray(wb3, BF16), jnp.asarray(mf, F32), jnp.asarray(mb, F32))


def _split3(g):
    hi = g.astype(BF16)
    r1 = g - hi.astype(F32)
    mid = r1.astype(BF16)
    lo = (r1 - mid.astype(F32)).astype(BF16)
    return jnp.concatenate([hi, mid, lo], axis=0)


def _hgrn_a_body(q_ref, i_ref, zf_ref, zb_ref, lb_ref, wf_ref, wb_ref, mf_ref, mb_ref,
                 oi_ref, qtf_ref, qtb_ref, utf_ref, utb_ref, df_ref, db_ref, *, cpb):
    c = REC_CHUNK
    dirs = ((zf_ref, wf_ref, mf_ref, qtf_ref, utf_ref, df_ref, 0, c - 1),
            (zb_ref, wb_ref, mb_ref, qtb_ref, utb_ref, db_ref, 1, 0))

    def chunk_group(cg, carry):
        cis = [cg * HGRN_UNROLL + u for u in range(HGRN_UNROLL)]
        rows = [pl.ds(pl.multiple_of(ci * c, c), c) for ci in cis]
        zqs = [q_ref[rw, :] for rw in rows]
        qs = [zq * jax.nn.sigmoid(zq) for zq in zqs]
        vbs = [i_ref[rw, :].astype(BF16) for rw in rows]
        units = [(u, d) for u in range(HGRN_UNROLL) for d in range(2)]
        ks, es = {}, {}
        for u, d in units:
            z_ref, w_ref = dirs[d][0], dirs[d][1]
            lb = lb_ref[d:d + 1, :]
            f = lb + (1.0 - lb) * jax.nn.sigmoid(z_ref[rows[u], :])
            ks[u, d] = 1.0 - f
            es[u, d] = jnp.exp(jnp.dot(w_ref[...], _split3(jnp.log(f)), preferred_element_type=F32))
        acc = [jnp.zeros((c, c), F32) for _ in range(HGRN_UNROLL)]
        for lvl in range(N_LEVELS + 1):
            for u, d in units:
                m_ref = dirs[d][2]
                if lvl < N_LEVELS:
                    el = es[u, d][lvl * c:(lvl + 1) * c]
                    qa, ka = (qs[u] * el).astype(BF16), (ks[u, d] * el).astype(BF16)
                else:
                    qa, ka = qs[u].astype(BF16), ks[u, d].astype(BF16)
                p = lax.dot_general(qa, ka, _NT, preferred_element_type=F32)
                acc[u] = acc[u] + p * m_ref[lvl]
        for u, d in units:
            _, _, _, qt_ref, ut_ref, d_ref, _, drow = dirs[d]
            e = es[u, d]
            qt_ref[rows[u], :] = (qs[u] * e[6 * c:7 * c]).astype(BF16)
            kt = (ks[u, d] * e[7 * c:8 * c]).astype(BF16)
            ut_ref[cis[u]] = lax.dot_general(vbs[u], kt, _TN, preferred_element_type=F32)
            d_ref[pl.ds(cis[u], 1), :] = e[6 * c + drow:6 * c + drow + 1]
        for u in range(HGRN_UNROLL):
            oi_ref[rows[u], :] = jnp.dot(acc[u].astype(BF16), vbs[u], preferred_element_type=F32)
        return carry

    lax.fori_loop(0, cpb // HGRN_UNROLL, chunk_group, 0)


def _hgrn_a(proj, lb, col, t, d_rec):
    heads = d_rec // REC_HEAD_DIM
    tq = 512
    cpb = tq // REC_CHUNK
    nchunks = t // REC_CHUNK
    wf, wb, mf, mb = _hgrn_consts()
    hd = REC_HEAD_DIM

    def colspec(off):
        return pl.BlockSpec((tq, hd), lambda i, h: (i, off + h))

    full2 = lambda i, h: (0, 0)
    full3 = lambda i, h: (0, 0, 0)
    row_spec = pl.BlockSpec((tq, hd), lambda i, h: (i, h))
    u_spec = pl.BlockSpec((cpb, hd, hd), lambda i, h: (i, 0, h))
    d_spec = pl.BlockSpec((cpb, hd), lambda i, h: (i, h))
    return pl.pallas_call(
        functools.partial(_hgrn_a_body, cpb=cpb),
        grid=(t // tq, heads),
        in_specs=[colspec(col["q_r"]), colspec(col["i_r"]), colspec(col["zf_f"]), colspec(col["zf_b"]),
                  pl.BlockSpec((2, hd), lambda i, h: (0, h)),
                  pl.BlockSpec(wf.shape, full2), pl.BlockSpec(wb.shape, full2),
                  pl.BlockSpec(mf.shape, full3), pl.BlockSpec(mb.shape, full3)],
        out_specs=[row_spec, row_spec, row_spec, u_spec, u_spec, d_spec, d_spec],
        out_shape=[jax.ShapeDtypeStruct((t, d_rec), F32),
                   jax.ShapeDtypeStruct((t, d_rec), BF16),
                   jax.ShapeDtypeStruct((t, d_rec), BF16),
                   jax.ShapeDtypeStruct((nchunks, hd, d_rec), F32),
                   jax.ShapeDtypeStruct((nchunks, hd, d_rec), F32),
                   jax.ShapeDtypeStruct((nchunks, d_rec), F32),
                   jax.ShapeDtypeStruct((nchunks, d_rec), F32)],
        compiler_params=_cparams(("arbitrary", "arbitrary"), 32),
        name="hgrn_a",
    )(proj, proj, proj, proj, lb, wf, wb, mf, mb)


def _hgrn_c_body(oi_ref, qtf_ref, qtb_ref, utf_ref, utb_ref, df_ref, db_ref, z_ref, g_ref,
                 out_ref, acc_ref, accb_ref, *, nchunks):
    c = REC_CHUNK
    hd = REC_HEAD_DIM

    unroll = 4

    def step(i, carry):
        st_f, st_b = carry
        pending = []
        for u in range(unroll):
            nf = i * unroll + u
            nb = nchunks - 1 - nf
            rows_f = pl.ds(pl.multiple_of(nf * c, c), c)
            rows_b = pl.ds(pl.multiple_of(nb * c, c), c)
            of = lax.dot_general(qtf_ref[rows_f, :], st_f.astype(BF16), _NT, preferred_element_type=F32)
            ob = lax.dot_general(qtb_ref[rows_b, :], st_b.astype(BF16), _NT, preferred_element_type=F32)
            pending.append((rows_f, rows_b, oi_ref[rows_f, :] + of, ob))
            st_f = df_ref[pl.ds(nf, 1), :] * st_f + utf_ref[nf]
            st_b = db_ref[pl.ds(nb, 1), :] * st_b + utb_ref[nb]
        for rows_f, rows_b, vf, vb in pending:
            acc_ref[rows_f, :] = vf
            accb_ref[rows_b, :] = vb
        return st_f, st_b

    zero = jnp.zeros((hd, hd), F32)
    lax.fori_loop(0, nchunks // unroll, step, (zero, zero))

    o = _rms(acc_ref[...] + accb_ref[...])
    out_ref[...] = (o * g_ref[...] * jax.nn.sigmoid(z_ref[...])).astype(BF16)


def _hgrn_c(oi, qtf, qtb, utf, utb, df, db, proj, g_out, col, bsz, seq, d_rec):
    heads = d_rec // REC_HEAD_DIM
    hd = REC_HEAD_DIM
    nchunks = seq // REC_CHUNK
    row_spec = pl.BlockSpec((seq, hd), lambda b, h: (b, h))
    u_spec = pl.BlockSpec((nchunks, hd, hd), lambda b, h: (b, 0, h))
    d_spec = pl.BlockSpec((nchunks, hd), lambda b, h: (b, h))
    zo = col["z_o"]
    return pl.pallas_call(
        functools.partial(_hgrn_c_body, nchunks=nchunks),
        grid=(bsz, heads),
        in_specs=[row_spec, row_spec, row_spec, u_spec, u_spec, d_spec, d_spec,
                  pl.BlockSpec((seq, hd), lambda b, h: (b, zo + h)),
                  pl.BlockSpec((1, hd), lambda b, h: (0, h))],
        out_specs=row_spec,
        out_shape=jax.ShapeDtypeStruct((bsz * seq, d_rec), BF16),
        scratch_shapes=[pltpu.VMEM((seq, hd), F32), pltpu.VMEM((seq, hd), F32)],
        compiler_params=_cparams(("arbitrary", "arbitrary"), 48),
        name="hgrn_c",
    )(oi, qtf, qtb, utf, utb, df, db, proj, g_out.reshape(1, d_rec))


def _t5_bucket(rel):
    half_buckets = NUM_BUCKETS // 2
    ret = np.where(rel > 0, half_buckets, 0)
    n = np.abs(rel)
    max_exact = half_buckets // 2
    nf = np.maximum(n, 1).astype(np.float32)
    large = max_exact + (np.log(nf / np.float32(max_exact)) / np.float32(math.log(MAX_DISTANCE / max_exact))
                         * np.float32(half_buckets - max_exact)).astype(np.int32)
    large = np.minimum(large, half_buckets - 1)
    return ret + np.where(n < max_exact, n, large)


def _band_bias(rel_bias_g, window, dil):
    half = window // (2 * dil)
    q_off = np.arange(ATT_BLOCK)[:, None]
    rel = np.arange(3 * ATT_BLOCK)[None, :] - ATT_BLOCK - q_off
    onehot = (_t5_bucket(rel * dil)[..., None] == np.arange(NUM_BUCKETS)).astype(np.float32)
    bias = jnp.einsum("qkb,bh->hqk", jnp.asarray(onehot), rel_bias_g.astype(F32),
                      precision=lax.Precision.HIGHEST)
    return jnp.where(jnp.asarray(np.abs(rel) <= half)[None], bias, NEG_INF)


def _attn_body(q_ref, kp_ref, k_ref, kn_ref, vp_ref, v_ref, vn_ref, bias_ref,
               num_ref, st_ref, kc_ref, vc_ref, *, dil, tq, sub_len):
    blk = ATT_BLOCK
    nqb = tq // blk
    n = pl.program_id(1)
    scale = ATT_HEAD_DIM ** -0.5

    def sds(start, size):
        if dil == 1:
            return pl.ds(start, size)
        return pl.ds(start, size, stride=dil)

    cu = kc_ref.shape[0]
    qu = ATT_UNROLL // cu

    def deinterleave(r, j):
        kc_ref[j, 0:blk, :] = kp_ref[sds(r, blk), :].astype(BF16)
        kc_ref[j, blk:blk + tq, :] = k_ref[sds(r, tq), :].astype(BF16)
        kc_ref[j, blk + tq:2 * blk + tq, :] = kn_ref[sds(r, blk), :].astype(BF16)
        vc_ref[j, 0:blk, :] = vp_ref[sds(r, blk), :].astype(BF16)
        vc_ref[j, blk:blk + tq, :] = v_ref[sds(r, tq), :].astype(BF16)
        vc_ref[j, blk + tq:2 * blk + tq, :] = vn_ref[sds(r, blk), :].astype(BF16)

    def units(r0, qb0):
        us = [(j, u) for j in range(cu) for u in range(qu)]
        q0s = [pl.multiple_of((qb0 + u) * blk, blk) for _, u in us]
        rows = [sds(r0 + j + dil * q0, blk) for (j, _), q0 in zip(us, q0s)]
        lane = lax.broadcasted_iota(jnp.int32, (blk, LANE), 1)
        key_iota = lax.broadcasted_iota(jnp.int32, (1, 3 * blk), 1)
        bias = bias_ref[0]
        qs = [q_ref[rw, :].astype(BF16) for rw in rows]
        kws = [kc_ref[j, pl.ds(q0, 3 * blk), :] for (j, _), q0 in zip(us, q0s)]
        vws = [vc_ref[j, pl.ds(q0, 3 * blk), :] for (j, _), q0 in zip(us, q0s)]
        ss = [lax.dot_general(q, kw, _NT, preferred_element_type=F32) * scale for q, kw in zip(qs, kws)]
        valids = []
        for q0 in q0s:
            kpos = n * tq + q0 - blk + key_iota
            valids.append((kpos >= 0) & (kpos < sub_len))
        ss = [jnp.where(valid, s + bias, NEG_INF) for s, valid in zip(ss, valids)]
        ms = [jnp.max(s, axis=-1, keepdims=True) for s in ss]
        ps = [jnp.exp(s - m) for s, m in zip(ss, ms)]
        ls = [jnp.sum(p, axis=-1, keepdims=True) for p in ps]
        nums = [jnp.dot(p.astype(BF16), vw, preferred_element_type=F32) for p, vw in zip(ps, vws)]
        for rw, num, m, l in zip(rows, nums, ms, ls):
            num_ref[rw, :] = num
            st_ref[rw, :] = jnp.where(lane < LANE // 2, m, l)

    def class_group(rg, carry):
        r0 = rg * cu
        for j in range(cu):
            deinterleave(r0 + j, j)

        def qgroup(qg, carry2):
            units(r0, qg * qu)
            return carry2

        lax.fori_loop(0, nqb // qu, qgroup, 0)
        return carry

    lax.fori_loop(0, dil // cu, class_group, 0)


def _attn_group(proj, bias, col, g, dil, bsz, seq):
    tile = 1024
    tq = tile // dil
    halo = ATT_BLOCK * dil
    sub_len = seq // dil
    cu = ATT_UNROLL // min(tq // ATT_BLOCK, ATT_UNROLL)
    nh = ATT_HEADS_PER_GROUP
    hd = ATT_HEAD_DIM
    qc = col["q_a"] + g * nh
    kc = col["k_a"] + g * nh
    vc = col["v_a"] + g * nh
    tiles_b = seq // tile
    halos_b = seq // halo
    hpt = tile // halo

    own = lambda c: pl.BlockSpec((tile, hd), lambda b, n, h: (b * tiles_b + n, c + h))
    prev = lambda c: pl.BlockSpec(
        (halo, hd), lambda b, n, h: (b * halos_b + jnp.maximum(n * hpt - 1, 0), c + h))
    nxt = lambda c: pl.BlockSpec(
        (halo, hd), lambda b, n, h: (b * halos_b + jnp.minimum((n + 1) * hpt, halos_b - 1), c + h))
    t = bsz * seq
    return pl.pallas_call(
        functools.partial(_attn_body, dil=dil, tq=tq, sub_len=sub_len),
        grid=(bsz, tiles_b, nh),
        in_specs=[own(qc), prev(kc), own(kc), nxt(kc), prev(vc), own(vc), nxt(vc),
                  pl.BlockSpec((1,) + bias.shape[1:], lambda b, n, h: (h, 0, 0))],
        out_specs=[pl.BlockSpec((tile, hd), lambda b, n, h: (b * tiles_b + n, h)),
                   pl.BlockSpec((tile, LANE), lambda b, n, h: (b * tiles_b + n, h))],
        out_shape=[jax.ShapeDtypeStruct((t, nh * hd), F32), jax.ShapeDtypeStruct((t, nh * LANE), F32)],
        scratch_shapes=[pltpu.VMEM((cu, tq + 2 * ATT_BLOCK, hd), BF16),
                        pltpu.VMEM((cu, tq + 2 * ATT_BLOCK, hd), BF16)],
        compiler_params=_cparams(("arbitrary", "arbitrary", "arbitrary"), 32),
        name=f"attn_d{dil}",
    )(proj, proj, proj, proj, proj, proj, proj, bias)


def _merge_body(rec_ref, n0_ref, n1_ref, n2_ref, s0_ref, s1_ref, s2_ref, zgr_ref, zga_ref, x_ref,
                ada_ref, gpost_ref, gpre_ref, wbr_ref, wba_ref, wo_ref, wr_ref, br_ref,
                x1_ref, h2_ref, lg_ref):
    nh = ATT_HEADS_PER_GROUP
    hd = ATT_HEAD_DIM
    half = LANE // 2
    lane = lax.broadcasted_iota(jnp.int32, (rec_ref.shape[0], LANE), 1)
    heads = []
    for h in range(nh):
        cols = slice(h * hd, (h + 1) * hd)
        st = [s[:, cols] for s in (s0_ref, s1_ref, s2_ref)]
        top = jnp.maximum(jnp.maximum(st[0], st[1]), st[2])
        ws = [jnp.exp(s - top) for s in st]
        den = (ws[0] * pltpu.roll(st[0], half, 1) + ws[1] * pltpu.roll(st[1], half, 1)
               + ws[2] * pltpu.roll(st[2], half, 1))
        coef = [w / den for w in ws]
        coef = [jnp.where(lane < half, c, pltpu.roll(c, half, 1)) for c in coef]
        num = coef[0] * n0_ref[:, cols] + coef[1] * n1_ref[:, cols] + coef[2] * n2_ref[:, cols]
        heads.append(num.astype(BF16))
    att = jnp.concatenate(heads, axis=1)
    y_rec = jnp.dot(rec_ref[...], wbr_ref[...], preferred_element_type=F32)
    y_att = jnp.dot(att, wba_ref[...], preferred_element_type=F32)
    merged = jax.nn.sigmoid(zgr_ref[...]) * y_rec + jax.nn.sigmoid(zga_ref[...]) * y_att
    y = jnp.dot(merged.astype(BF16), wo_ref[...], preferred_element_type=F32)
    gt_m = ada_ref[0, 2:3, :]
    sh_f = ada_ref[0, 3:4, :]
    sc_f = ada_ref[0, 4:5, :]
    x1 = x_ref[...] + gt_m * (_rms(y) * gpost_ref[...])
    x1_ref[...] = x1
    h2 = _rms(x1) * gpre_ref[...] * (1.0 + sc_f) + sh_f
    tm = h2.shape[0]
    for s in range(SLAB_ROWS):
        h2_ref[pl.ds(s, tm, stride=SLAB_ROWS), :] = h2[:, s * LANE:(s + 1) * LANE]
    ne = lg_ref.shape[1]
    h_hi = h2.astype(BF16)
    h_lo = (h2 - h_hi.astype(F32)).astype(BF16)
    both = jnp.dot(h_hi, wr_ref[...], preferred_element_type=F32)
    cross = jnp.dot(h_lo, wr_ref[:, 0:ne], preferred_element_type=F32)
    lg_ref[...] = both[:, 0:ne] + both[:, ne:2 * ne] + cross + br_ref[...]


def _merge(rec_o, nums, stats, proj, x2, ada3, g_post, g_pre, wbr, wba, wo, w_router, b_router, col, seq):
    t, d = x2.shape
    tm = 256
    per_b = seq // tm
    d_rec = rec_o.shape[1]
    w_att = nums[0].shape[1]
    ne = w_router.shape[1]
    wr_hi = w_router.astype(BF16)
    wr_lo = (w_router - wr_hi.astype(F32)).astype(BF16)
    w_router = jnp.concatenate([wr_hi, wr_lo], axis=1)
    dl = d // LANE
    row = lambda w: pl.BlockSpec((tm, w), lambda i: (i, 0))
    const = lambda shape: pl.BlockSpec(shape, lambda i: (0,) * len(shape), pipeline_mode=pl.Buffered(1))
    zgr = col["zg_rec"] // dl
    zga = col["zg_att"] // dl
    return pl.pallas_call(
        _merge_body,
        grid=(t // tm,),
        in_specs=[row(d_rec), row(w_att), row(w_att), row(w_att), row(w_att), row(w_att), row(w_att),
                  pl.BlockSpec((tm, d), lambda i: (i, zgr)),
                  pl.BlockSpec((tm, d), lambda i: (i, zga)),
                  row(d),
                  pl.BlockSpec((1, 6, d), lambda i: (i // per_b, 0, 0)),
                  const((1, d)), const((1, d)),
                  const(wbr.shape), const(wba.shape), const(wo.shape), const(w_router.shape),
                  const((1, ne))],
        out_specs=[row(d), pl.BlockSpec((tm * SLAB_ROWS, LANE), lambda i: (i, 0)),
                   pl.BlockSpec((tm, ne), lambda i: (i, 0))],
        out_shape=[jax.ShapeDtypeStruct((t, d), F32), jax.ShapeDtypeStruct((t * SLAB_ROWS, LANE), F32),
                   jax.ShapeDtypeStruct((t, ne), F32)],
        compiler_params=_cparams(("arbitrary",), 56),
        name="merge",
    )(rec_o, nums[0], nums[1], nums[2], stats[0], stats[1], stats[2], proj, proj, x2, ada3,
      g_post.reshape(1, d), g_pre.reshape(1, d), wbr, wba, wo, w_router, b_router.reshape(1, ne))


def _route_body(lg_ref, tri_ref, rt_ref, cnt_ref, carry_ref):
    i = pl.program_id(0)
    tr, ne = lg_ref.shape

    @pl.when(i == 0)
    def _():
        carry_ref[...] = jnp.zeros_like(carry_ref)

    l = lg_ref[...]
    lane = lax.broadcasted_iota(jnp.int32, (tr, ne), 1).astype(F32)
    vals, sels, idxs = [], [], []
    for _ in range(TOP_K):
        m = jnp.max(l, axis=-1, keepdims=True)
        idx = jnp.min(jnp.where(l == m, lane, float(ne)), axis=-1, keepdims=True)
        sel = lane == idx
        vals.append(m)
        idxs.append(idx)
        sels.append(sel)
        l = jnp.where(sel, -jnp.inf, l)
    es = [jnp.exp(v - vals[0]) for v in vals]
    tot = es[0] + es[1] + es[2] + es[3]
    chosen = (sels[0] | sels[1] | sels[2] | sels[3]).astype(F32)
    prefix = jnp.dot(tri_ref[...], chosen.astype(BF16), preferred_element_type=F32) + carry_ref[0:1, :]
    out_lane = lax.broadcasted_iota(jnp.int32, (tr, LANE), 1)
    rt = jnp.zeros((tr, LANE), F32)
    for k in range(TOP_K):
        rank = jnp.sum(jnp.where(sels[k], prefix, 0.0), axis=-1, keepdims=True)
        rt = jnp.where(out_lane == k, idxs[k], rt)
        rt = jnp.where(out_lane == TOP_K + k, es[k] / tot, rt)
        rt = jnp.where(out_lane == 2 * TOP_K + k, rank, rt)
    rt_ref[...] = rt
    new = carry_ref[0:1, :] + jnp.sum(chosen, axis=0, keepdims=True)
    carry_ref[...] = jnp.broadcast_to(new, carry_ref.shape)
    cnt_ref[...] = carry_ref[...]


def _route(logits):
    t, ne = logits.shape
    tr = 512
    tri = jnp.asarray(np.tril(np.ones((tr, tr), np.float32), -1), BF16)
    return pl.pallas_call(
        _route_body,
        grid=(t // tr,),
        in_specs=[pl.BlockSpec((tr, ne), lambda i: (i, 0)),
                  pl.BlockSpec((tr, tr), lambda i: (0, 0))],
        out_specs=[pl.BlockSpec((tr, LANE), lambda i: (i, 0)),
                   pl.BlockSpec((SUBLANE, ne), lambda i: (0, 0))],
        out_shape=[jax.ShapeDtypeStruct((t, LANE), F32), jax.ShapeDtypeStruct((SUBLANE, ne), F32)],
        scratch_shapes=[pltpu.VMEM((SUBLANE, ne), F32)],
        compiler_params=_cparams(("arbitrary",), 32),
        name="route",
    )(logits, tri)


def _chunk_rows(ncols):
    return W_CHUNK_BYTES // (4 * ncols)


def _weight_stream(w_hbm, wbf, stage, wsem, e, slot, c0, c1):
    kc = stage.shape[1]

    def copy(c):
        return pltpu.make_async_copy(w_hbm.at[e, pl.ds(pl.multiple_of(c * kc, kc), kc), :],
                                     stage.at[c % 2], wsem.at[c % 2])

    def prime():
        def body(c, carry):
            copy(c).start()
            return carry
        lax.fori_loop(c0, jnp.minimum(c0 + 2, c1), body, 0)

    def finish():
        def body(c, carry):
            copy(c).wait()
            buf = c % 2

            def cast(i, carry2):
                r = pl.multiple_of(i * CAST_ROWS, CAST_ROWS)
                wbf[slot, pl.ds(pl.multiple_of(c * kc, kc) + r, CAST_ROWS), :] = (
                    stage[buf, pl.ds(r, CAST_ROWS), :].astype(BF16))
                return carry2
            lax.fori_loop(0, kc // CAST_ROWS, cast, 0)

            @pl.when(c + 2 < c1)
            def _():
                copy(c + 2).start()
            return carry
        lax.fori_loop(c0, c1, body, 0)

    return prime, finish


def _moe_up_body(be_ref, nu_ref, ws_ref, wn_ref, wc0_ref, wc1_ref, idx0_ref, idxn_ref, h2s_ref, w_hbm, bias_ref,
                 o_ref, xbuf, wbf, stage, sem, wsem):
    b = pl.program_id(0)
    nu = nu_ref[0]
    f = o_ref.shape[1]
    slab = SLAB_ROWS
    nch = wbf.shape[1] // stage.shape[1]

    def row_start(idx_ref, slot, r):
        tok = jnp.maximum(idx_ref[0, 0, r], 0) // TOP_K
        pltpu.make_async_copy(h2s_ref.at[pl.ds(pl.multiple_of(tok * slab, slab), slab), :],
                              xbuf.at[slot, pl.ds(pl.multiple_of(r * slab, slab), slab), :],
                              sem.at[slot]).start()

    def wait_rows(slot):
        pltpu.make_async_copy(h2s_ref.at[pl.ds(0, BM * slab), :], xbuf.at[slot], sem.at[slot]).wait()

    def issue(idx_ref, slot):
        def body(r, c):
            row_start(idx_ref, slot, r)
            return c
        lax.fori_loop(0, BM, body, 0, unroll=DMA_UNROLL)

    @pl.when(b == 0)
    def _():
        issue(idx0_ref, 0)
        prime0, finish0 = _weight_stream(w_hbm, wbf, stage, wsem, be_ref[0], ws_ref[0], 0, nch)
        prime0()
        finish0()

    @pl.when(b + 1 < nu)
    def _():
        issue(idxn_ref, (b + 1) % 2)

    @pl.when(b < nu)
    def _():
        wslot = ws_ref[b]
        prime, finish = _weight_stream(w_hbm, wbf, stage, wsem, wn_ref[b], 1 - wslot, wc0_ref[b], wc1_ref[b])
        prime()
        slot = b % 2
        wait_rows(slot)
        x = jnp.concatenate([xbuf[slot, pl.ds(s, BM, stride=slab), :].astype(BF16) for s in range(slab)],
                            axis=1)
        half = f // 2
        for c0 in (0, half):
            gate = (jnp.dot(x, wbf[wslot, :, c0:c0 + half], preferred_element_type=F32)
                    + bias_ref[0, :, c0:c0 + half])
            up = (jnp.dot(x, wbf[wslot, :, f + c0:f + c0 + half], preferred_element_type=F32)
                  + bias_ref[0, :, f + c0:f + c0 + half])
            gate = jnp.minimum(gate, SWIGLU_LIMIT)
            up = jnp.clip(up, -SWIGLU_LIMIT, SWIGLU_LIMIT)
            o_ref[:, c0:c0 + half] = (gate * jax.nn.sigmoid(SWIGLU_ALPHA * gate) * (up + 1.0)).astype(BF16)
        finish()

    @pl.when(b >= nu)
    def _():
        o_ref[...] = jnp.zeros_like(o_ref)


def _moe_up(h2s, src_tok, wgu, bgu, sched):
    ne, d, f2 = wgu.shape
    f = f2 // 2
    p = src_tok.shape[0]
    nb = p // BM
    idx3 = src_tok.reshape(nb, 1, BM)
    smem_blk = lambda imap: pl.BlockSpec((1, 1, BM), imap, memory_space=pltpu.SMEM)
    grid_spec = pltpu.PrefetchScalarGridSpec(
        num_scalar_prefetch=6,
        grid=(nb,),
        in_specs=[smem_blk(lambda b, be, *_: (0, 0, 0)),
                  smem_blk(lambda b, be, *_: (jnp.minimum(b + 1, nb - 1), 0, 0)),
                  pl.BlockSpec(memory_space=pl.ANY),
                  pl.BlockSpec(memory_space=pl.ANY),
                  pl.BlockSpec((1, 1, f2), lambda b, be, *_: (be[b], 0, 0))],
        out_specs=pl.BlockSpec((BM, f), lambda b, be, *_: (b, 0)),
        scratch_shapes=[pltpu.VMEM((2, BM * SLAB_ROWS, LANE), F32),
                        pltpu.VMEM((2, d, f2), BF16),
                        pltpu.VMEM((2, _chunk_rows(f2), f2), F32),
                        pltpu.SemaphoreType.DMA((2,)), pltpu.SemaphoreType.DMA((2,))],
    )
    return pl.pallas_call(
        _moe_up_body,
        grid_spec=grid_spec,
        out_shape=jax.ShapeDtypeStruct((p, f), BF16),
        compiler_params=_cparams(("arbitrary",), 58),
        name="moe_up",
    )(sched["blk_e"], sched["n_used"], sched["wslot"], sched["wnext"], sched["wc0"], sched["wc1"],
      idx3, idx3, h2s, wgu, bgu.reshape(ne, 1, f2))


def _moe_down_body(be_ref, nu_ref, ws_ref, wn_ref, wc0_ref, wc1_ref, nv_ref, dst_ref, a_ref, w_hbm, bias_ref,
                   ysc_ref, ybuf, wbf, stage, sem, wsem):
    b = pl.program_id(0)
    nb = pl.num_programs(0)
    nu = nu_ref[0]
    slab = SLAB_ROWS
    nch = wbf.shape[1] // stage.shape[1]

    @pl.when(b == 0)
    def _():
        prime0, finish0 = _weight_stream(w_hbm, wbf, stage, wsem, be_ref[0], ws_ref[0], 0, nch)
        prime0()
        finish0()

    def row_copy(slot, r, d):
        return pltpu.make_async_copy(ybuf.at[slot, pl.ds(pl.multiple_of(r * slab, slab), slab), :],
                                     ysc_ref.at[pl.ds(pl.multiple_of(d * slab, slab), slab), :],
                                     sem.at[slot])

    def drain(step):
        slot = step % 2
        count = nv_ref[step]

        @pl.when(count == BM)
        def _():
            pltpu.make_async_copy(ybuf.at[slot], ysc_ref.at[pl.ds(0, BM * slab), :], sem.at[slot]).wait()

        @pl.when(count < BM)
        def _():
            def body(r, c):
                row_copy(slot, 0, 0).wait()
                return c
            lax.fori_loop(0, count, body, 0)

    @pl.when((b >= 2) & (b < nu))
    def _():
        drain(b - 2)

    @pl.when(b < nu)
    def _():
        slot = b % 2
        wslot = ws_ref[b]
        prime, finish = _weight_stream(w_hbm, wbf, stage, wsem, wn_ref[b], 1 - wslot, wc0_ref[b], wc1_ref[b])
        prime()
        y = jnp.dot(a_ref[...], wbf[wslot], preferred_element_type=F32) + bias_ref[0]
        for s in range(slab):
            ybuf[slot, pl.ds(s, BM, stride=slab), :] = y[:, s * LANE:(s + 1) * LANE]
        finish()

        def body(r, c):
            row_copy(slot, r, dst_ref[0, 0, r]).start()
            return c

        @pl.when(nv_ref[b] == BM)
        def _():
            lax.fori_loop(0, BM, body, 0, unroll=DMA_UNROLL)

        @pl.when(nv_ref[b] < BM)
        def _():
            lax.fori_loop(0, nv_ref[b], body, 0)

    @pl.when(b == nb - 1)
    def _():
        @pl.when(nu >= 2)
        def _():
            drain(nu - 2)
        drain(nu - 1)


def _moe_down(act, dst_slot, n_out_rows, wd, bd, sched):
    p, f = act.shape
    ne, _, d = wd.shape
    nb = p // BM
    grid_spec = pltpu.PrefetchScalarGridSpec(
        num_scalar_prefetch=7,
        grid=(nb,),
        in_specs=[pl.BlockSpec((1, 1, BM), lambda b, be, *_: (b, 0, 0), memory_space=pltpu.SMEM),
                  pl.BlockSpec((BM, f), lambda b, be, *_: (b, 0)),
                  pl.BlockSpec(memory_space=pl.ANY),
                  pl.BlockSpec((1, 1, d), lambda b, be, *_: (be[b], 0, 0))],
        out_specs=pl.BlockSpec(memory_space=pl.ANY),
        scratch_shapes=[pltpu.VMEM((2, BM * SLAB_ROWS, LANE), F32),
                        pltpu.VMEM((2, f, d), BF16),
                        pltpu.VMEM((2, _chunk_rows(d), d), F32),
                        pltpu.SemaphoreType.DMA((2,)), pltpu.SemaphoreType.DMA((2,))],
    )
    return pl.pallas_call(
        _moe_down_body,
        grid_spec=grid_spec,
        out_shape=jax.ShapeDtypeStruct((n_out_rows * SLAB_ROWS, LANE), F32),
        compiler_params=pltpu.CompilerParams(dimension_semantics=("arbitrary",),
                                             vmem_limit_bytes=48 * 1024 * 1024, has_side_effects=True),
        name="moe_down",
    )(sched["blk_e"], sched["n_used"], sched["wslot"], sched["wnext"], sched["wc0"], sched["wc1"],
      sched["nvalid"], dst_slot.reshape(nb, 1, BM), act, wd, bd.reshape(ne, 1, d))


def _final_body(ysc_ref, rt_ref, x1_ref, ada_ref, g_ref, o_ref):
    tm = x1_ref.shape[0]
    rt = rt_ref[...]
    stride = TOP_K * SLAB_ROWS
    pieces = []
    for s in range(SLAB_ROWS):
        acc = rt[:, TOP_K:TOP_K + 1] * ysc_ref[pl.ds(s, tm, stride=stride), :]
        for k in range(1, TOP_K):
            acc = acc + rt[:, TOP_K + k:TOP_K + k + 1] * ysc_ref[pl.ds(k * SLAB_ROWS + s, tm, stride=stride), :]
        pieces.append(acc)
    y = jnp.concatenate(pieces, axis=1)
    gt_f = ada_ref[0, 5:6, :]
    o_ref[...] = x1_ref[...] + gt_f * (_rms(y) * g_ref[...])


def _final(ysc, rt, x1, ada3, g_post, seq):
    t, d = x1.shape
    tm = 256
    per_b = seq // tm
    return pl.pallas_call(
        _final_body,
        grid=(t // tm,),
        in_specs=[pl.BlockSpec((tm * TOP_K * SLAB_ROWS, LANE), lambda i: (i, 0)),
                  pl.BlockSpec((tm, LANE), lambda i: (i, 0)),
                  pl.BlockSpec((tm, d), lambda i: (i, 0)),
                  pl.BlockSpec((1, 6, d), lambda i: (i // per_b, 0, 0)),
                  pl.BlockSpec((1, d), lambda i: (0, 0))],
        out_specs=pl.BlockSpec((tm, d), lambda i: (i, 0)),
        out_shape=jax.ShapeDtypeStruct((t, d), F32),
        compiler_params=_cparams(("arbitrary",), 48),
        name="final",
    )(ysc, rt, x1, ada3, g_post.reshape(1, d))


def _mixer_ffn_layer(x2, ada3, bsz, seq, g_mix_pre, g_mix_post, g_ffn_pre, g_ffn_post, w_in, lb, g_rec_out,
                     rel_bias, w_branch_rec, w_branch_att, w_o, w_router, b_router, w_gate_up, b_gate_up,
                     w_down, b_down):
    t, d = x2.shape
    d_rec = w_branch_rec.shape[0]
    w_att = w_branch_att.shape[0]
    d_att = 3 * w_att
    widths = dict(q_r=d_rec, i_r=d_rec, zf_f=d_rec, zf_b=d_rec, z_o=d_rec, q_a=d_att, k_a=d_att, v_a=d_att,
                  zg_rec=d, zg_att=d)
    ref_order = ("q_r", "i_r", "zf_f", "zf_b", "z_o", "q_a", "k_a", "v_a", "zg_rec", "zg_att")
    my_order = ("zg_rec", "zg_att", "q_r", "i_r", "zf_f", "zf_b", "z_o", "q_a", "k_a", "v_a")
    ref_off, acc = {}, 0
    for name in ref_order:
        ref_off[name] = acc
        acc += widths[name]
    col, acc = {}, 0
    for name in my_order:
        col[name] = acc // LANE
        acc += widths[name]
    w_in_bf = jnp.concatenate([w_in[:, ref_off[nm]:ref_off[nm] + widths[nm]].astype(BF16) for nm in my_order],
                              axis=1)

    proj = _inproj(x2, g_mix_pre, ada3, w_in_bf, seq)

    oi, qtf, qtb, utf, utb, df, db = _hgrn_a(proj, lb, col, t, d_rec)
    rec_o = _hgrn_c(oi, qtf, qtb, utf, utb, df, db, proj, g_rec_out, col, bsz, seq, d_rec)

    nums, stats = [], []
    for g, (window, dil) in enumerate(DIL_GROUPS):
        hs = slice(g * ATT_HEADS_PER_GROUP, (g + 1) * ATT_HEADS_PER_GROUP)
        bias = _band_bias(rel_bias[:, hs], window, dil)
        num, st = _attn_group(proj, bias, col, g, dil, bsz, seq)
        nums.append(num)
        stats.append(st)

    x1, h2s, logits = _merge(rec_o, nums, stats, proj, x2, ada3, g_mix_post, g_ffn_pre,
                            w_branch_rec.astype(BF16), w_branch_att.astype(BF16), w_o.astype(BF16),
                            w_router, b_router, col, seq)

    rt, cnt = _route(logits)
    ne = logits.shape[1]
    counts = cnt[0].astype(jnp.int32)
    top_idx = rt[:, 0:TOP_K].astype(jnp.int32)
    rank = rt[:, 2 * TOP_K:3 * TOP_K].astype(jnp.int32)
    padded = (counts + BM - 1) // BM * BM
    pends = jnp.cumsum(padded)
    pstarts = pends - padded
    experts = jnp.arange(ne, dtype=jnp.int32)
    pstart_sel = jnp.sum(jnp.where(top_idx[..., None] == experts, pstarts, 0), axis=-1)
    dest = (pstart_sel + rank).reshape(-1)
    p_rows = t * TOP_K + ne * BM
    nb = p_rows // BM
    blk_start = jnp.arange(nb, dtype=jnp.int32) * BM
    blk_e = jnp.minimum(jnp.sum((pends[None, :] <= blk_start[:, None]).astype(jnp.int32), axis=1), ne - 1)
    n_used = (pends[-1:] // BM).astype(jnp.int32)

    n_assign = t * TOP_K
    slot_assign = jnp.full((p_rows,), -1, jnp.int32).at[dest].set(jnp.arange(n_assign, dtype=jnp.int32))
    valid_end = pstarts + counts
    blk_end = jnp.sum(jnp.where(blk_e[:, None] == experts, valid_end, 0), axis=-1)
    nvalid = jnp.clip(blk_end - blk_start, 0, BM).astype(jnp.int32)
    nvalid = jnp.where(jnp.arange(nb) < n_used[0], nvalid, 0)

    lookup = lambda table: jnp.sum(jnp.where(blk_e[:, None] == experts, table, 0), axis=-1)
    nonempty = padded > 0
    order = jnp.cumsum(nonempty.astype(jnp.int32)) - 1
    later = lax.cummin(jnp.where(nonempty, experts, ne)[::-1])[::-1]
    next_e = jnp.concatenate([later[1:], jnp.full((1,), ne, jnp.int32)])
    blk_next = lookup(next_e)
    has_next = (blk_next < ne) & (jnp.arange(nb) < n_used[0])
    k_in_run = jnp.arange(nb, dtype=jnp.int32) - lookup(pstarts // BM)
    n_in_run = jnp.maximum(lookup(padded // BM), 1)
    common = dict(blk_e=blk_e, n_used=n_used, nvalid=nvalid, wslot=lookup(order) % 2,
                  wnext=jnp.where(has_next, blk_next, blk_e))

    def schedule(w):
        nch = w.shape[1] // _chunk_rows(w.shape[2])
        s = dict(common, wc0=jnp.where(has_next, k_in_run * nch // n_in_run, 0),
                 wc1=jnp.where(has_next, (k_in_run + 1) * nch // n_in_run, 0))
        return {k: v.astype(jnp.int32) for k, v in s.items()}

    act = _moe_up(h2s, slot_assign, w_gate_up, b_gate_up, schedule(w_gate_up))
    ysc = _moe_down(act, slot_assign, n_assign, w_down, b_down, schedule(w_down))
    return _final(ysc, rt, x1, ada3, g_ffn_post, seq)


def kernel(x, c, w_ada, b_ada, g_mix_pre, g_mix_post, g_ffn_pre, g_ffn_post, w_in, g_rec_out, w_branch_rec,
           w_branch_att, w_o, w_router, b_router, w_gate_up, b_gate_up, w_down, b_down, rec_lb_table, rel_bias):
    bsz, seq, d = x.shape
    depth = w_in.shape[0]
    lb_all = jnp.cumsum(jax.nn.softmax(rec_lb_table.astype(F32), axis=1), axis=1)
    x2 = x.reshape(bsz * seq, d)
    for layer in range(depth):
        ada3 = _ada(c, w_ada[layer], b_ada[layer]).reshape(bsz, 6, d)
        x2 = _mixer_ffn_layer(x2, ada3, bsz, seq, g_mix_pre[layer], g_mix_post[layer], g_ffn_pre[layer],
                              g_ffn_post[layer], w_in[layer], lb_all[:, layer], g_rec_out[layer], rel_bias,
                              w_branch_rec[layer], w_branch_att[layer], w_o[layer], w_router[layer],
                              b_router[layer], w_gate_up[layer], b_gate_up[layer], w_down[layer],
                              b_down[layer])
    return x2.reshape(bsz, seq, d)
```

```python
import functools
import math

import numpy as np
import jax
import jax.numpy as jnp
from jax import lax
from jax.experimental import pallas as pl
from jax.experimental.pallas import tpu as pltpu

F32 = jnp.float32
BF16 = jnp.bfloat16

LANE = 128
SUBLANE = 8
SLAB_ROWS = 16

REC_HEAD_DIM = 128
REC_CHUNK = 64
ATT_HEAD_DIM = 128
ATT_HEADS_PER_GROUP = 4
ATT_BLOCK = 64
DIL_GROUPS = ((128, 1), (512, 4), (2048, 16))
NUM_BUCKETS = 32
MAX_DISTANCE = 1024
N_EXPERTS = 32
TOP_K = 4
SWIGLU_LIMIT = 7.0
SWIGLU_ALPHA = 1.702
RMS_EPS = 1e-6
NEG_INF = -1e30

N_LEVELS = 6
W_CHUNK_BYTES = 4 * 1024 * 1024
CAST_ROWS = 32
HGRN_UNROLL = 4
DMA_UNROLL = 8
ATT_UNROLL = 8
BM = 256

_NT = (((1,), (1,)), ((), ()))
_TN = (((0,), (0,)), ((), ()))


def _cparams(sem, vmem_mb):
    return pltpu.CompilerParams(dimension_semantics=sem, vmem_limit_bytes=vmem_mb * 1024 * 1024)


def _rms(x):
    return x * lax.rsqrt(jnp.mean(x * x, axis=-1, keepdims=True) + RMS_EPS)


def _ada_body(c_ref, w_ref, b_ref, o_ref):
    c = c_ref[...]
    cond = (c * jax.nn.sigmoid(c)).astype(BF16)
    o_ref[...] = jnp.dot(cond, w_ref[...].astype(BF16), preferred_element_type=F32) + b_ref[...]


def _ada(c, w, b):
    bsz, d = c.shape
    n = w.shape[1]
    tn = 1024
    cp = jnp.zeros((SUBLANE, d), F32).at[:bsz].set(c)
    out = pl.pallas_call(
        _ada_body,
        grid=(n // tn,),
        in_specs=[pl.BlockSpec((SUBLANE, d), lambda j: (0, 0)),
                  pl.BlockSpec((d, tn), lambda j: (0, j)),
                  pl.BlockSpec((1, tn), lambda j: (0, j))],
        out_specs=pl.BlockSpec((SUBLANE, tn), lambda j: (0, j)),
        out_shape=jax.ShapeDtypeStruct((SUBLANE, n), F32),
        compiler_params=_cparams(("arbitrary",), 40),
        name="ada",
    )(cp, w, b.reshape(1, n))
    return out[:bsz]


def _inproj_body(x_ref, g_ref, ada_ref, w_ref, o_ref, h_ref):
    @pl.when(pl.program_id(1) == 0)
    def _():
        y = _rms(x_ref[...]) * g_ref[...]
        sh = ada_ref[0, 0:1, :]
        sc = ada_ref[0, 1:2, :]
        h_ref[...] = (y * (1.0 + sc) + sh).astype(BF16)

    o_ref[...] = jnp.dot(h_ref[...], w_ref[...].astype(BF16), preferred_element_type=F32)


def _inproj(x2, g, ada3, w_in, seq, rot):
    t, d = x2.shape
    n = w_in.shape[1]
    tm, tn = 1024, 512
    per_b = seq // tm
    nj = n // tn
    assert rot % tn == 0 and n % tn == 0
    return pl.pallas_call(
        _inproj_body,
        grid=(t // tm, nj),
        in_specs=[pl.BlockSpec((tm, d), lambda i, j: (i, 0)),
                  pl.BlockSpec((1, d), lambda i, j: (0, 0)),
                  pl.BlockSpec((1, 6, d), lambda i, j: (i // per_b, 0, 0)),
                  pl.BlockSpec((d, tn), lambda i, j: (0, (j + rot // tn) % nj))],
        out_specs=pl.BlockSpec((tm, tn), lambda i, j: (i, j)),
        out_shape=jax.ShapeDtypeStruct((t, n), F32),
        scratch_shapes=[pltpu.VMEM((tm, d), BF16)],
        compiler_params=_cparams(("arbitrary", "arbitrary"), 48),
        name="inproj",
    )(x2, g.reshape(1, d), ada3, w_in)


def _hgrn_consts():
    c = REC_CHUNK
    r = np.arange(c)[:, None]
    m = np.arange(c)[None, :]
    wf = np.zeros((8 * c, c), np.float32)
    wb = np.zeros((8 * c, c), np.float32)
    mf = np.zeros((N_LEVELS + 1, c, c), np.float32)
    for lvl in range(N_LEVELS):
        s = 32 >> lvl
        m0 = (r // (2 * s)) * (2 * s) + s
        up = r >= m0
        wf[lvl * c:(lvl + 1) * c] = np.where(up, (m >= m0) & (m <= r), (m > r) & (m <= m0 - 1))
        wb[lvl * c:(lvl + 1) * c] = np.where(up, (m >= m0) & (m <= r - 1), (m >= r) & (m <= m0 - 1))
        i = np.arange(c)[:, None]
        j = np.arange(c)[None, :]
        mf[lvl] = (i // (2 * s) == j // (2 * s)) & (i % (2 * s) >= s) & (j % (2 * s) < s)
    mf[N_LEVELS] = np.eye(c)
    wf[6 * c:7 * c] = m <= r
    wf[7 * c:8 * c] = m > r
    wb[6 * c:7 * c] = m >= r
    wb[7 * c:8 * c] = m < r
    mb = np.transpose(mf, (0, 2, 1)).copy()
    wf3 = np.concatenate([wf, wf, wf], axis=1)
    wb3 = np.concatenate([wb, wb, wb], axis=1)
    return (jnp.asarray(wf3, BF16), jnp.asarray(wb3, BF16), jnp.asarray(mf, F32), jnp.asarray(mb, F32))


def _split3(g):
    hi = g.astype(BF16)
    r1 = g - hi.astype(F32)
    mid = r1.astype(BF16)
    lo = (r1 - mid.astype(F32)).astype(BF16)
    return jnp.concatenate([hi, mid, lo], axis=0)


def _hgrn_a_body(q_ref, i_ref, zf_ref, zb_ref, lb_ref, wf_ref, wb_ref, mf_ref, mb_ref,
                 oi_ref, qtf_ref, qtb_ref, utf_ref, utb_ref, df_ref, db_ref, *, cpb):
    c = REC_CHUNK
    dirs = ((zf_ref, wf_ref, mf_ref, qtf_ref, utf_ref, df_ref, 0, c - 1),
            (zb_ref, wb_ref, mb_ref, qtb_ref, utb_ref, db_ref, 1, 0))

    def chunk_group(cg, carry):
        cis = [cg * HGRN_UNROLL + u for u in range(HGRN_UNROLL)]
        rows = [pl.ds(pl.multiple_of(ci * c, c), c) for ci in cis]
        zqs = [q_ref[rw, :] for rw in rows]
        qs = [zq * jax.nn.sigmoid(zq) for zq in zqs]
        vbs = [i_ref[rw, :].astype(BF16) for rw in rows]
        units = [(u, d) for u in range(HGRN_UNROLL) for d in range(2)]
        ks, es = {}, {}
        for u, d in units:
            z_ref, w_ref = dirs[d][0], dirs[d][1]
            lb = lb_ref[d:d + 1, :]
            f = lb + (1.0 - lb) * jax.nn.sigmoid(z_ref[rows[u], :])
            ks[u, d] = 1.0 - f
            es[u, d] = jnp.exp(jnp.dot(w_ref[...], _split3(jnp.log(f)), preferred_element_type=F32))
        acc = [jnp.zeros((c, c), F32) for _ in range(HGRN_UNROLL)]
        for lvl in range(N_LEVELS + 1):
            for u, d in units:
                m_ref = dirs[d][2]
                if lvl < N_LEVELS:
                    el = es[u, d][lvl * c:(lvl + 1) * c]
                    qa, ka = (qs[u] * el).astype(BF16), (ks[u, d] * el).astype(BF16)
                else:
                    qa, ka = qs[u].astype(BF16), ks[u, d].astype(BF16)
                p = lax.dot_general(qa, ka, _NT, preferred_element_type=F32)
                acc[u] = acc[u] + p * m_ref[lvl]
        for u, d in units:
            _, _, _, qt_ref, ut_ref, d_ref, _, drow = dirs[d]
            e = es[u, d]
            qt_ref[rows[u], :] = (qs[u] * e[6 * c:7 * c]).astype(BF16)
            kt = (ks[u, d] * e[7 * c:8 * c]).astype(BF16)
            ut_ref[cis[u]] = lax.dot_general(vbs[u], kt, _TN, preferred_element_type=F32)
            d_ref[pl.ds(cis[u], 1), :] = e[6 * c + drow:6 * c + drow + 1]
        for u in range(HGRN_UNROLL):
            oi_ref[rows[u], :] = jnp.dot(acc[u].astype(BF16), vbs[u], preferred_element_type=F32)
        return carry

    lax.fori_loop(0, cpb // HGRN_UNROLL, chunk_group, 0)


def _hgrn_a(proj, lb, col, t, d_rec):
    heads = d_rec // REC_HEAD_DIM
    tq = 1024
    cpb = tq // REC_CHUNK
    nchunks = t // REC_CHUNK
    wf, wb, mf, mb = _hgrn_consts()
    hd = REC_HEAD_DIM

    def colspec(off):
        return pl.BlockSpec((tq, hd), lambda i, h: (i, off + h))

    full2 = lambda i, h: (0, 0)
    full3 = lambda i, h: (0, 0, 0)
    row_spec = pl.BlockSpec((tq, hd), lambda i, h: (i, h))
    u_spec = pl.BlockSpec((cpb, hd, hd), lambda i, h: (i, 0, h))
    d_spec = pl.BlockSpec((cpb, hd), lambda i, h: (i, h))
    return pl.pallas_call(
        functools.partial(_hgrn_a_body, cpb=cpb),
        grid=(t // tq, heads),
        in_specs=[colspec(col["q_r"]), colspec(col["i_r"]), colspec(col["zf_f"]), colspec(col["zf_b"]),
                  pl.BlockSpec((2, hd), lambda i, h: (0, h)),
                  pl.BlockSpec(wf.shape, full2), pl.BlockSpec(wb.shape, full2),
                  pl.BlockSpec(mf.shape, full3), pl.BlockSpec(mb.shape, full3)],
        out_specs=[row_spec, row_spec, row_spec, u_spec, u_spec, d_spec, d_spec],
        out_shape=[jax.ShapeDtypeStruct((t, d_rec), F32),
                   jax.ShapeDtypeStruct((t, d_rec), BF16),
                   jax.ShapeDtypeStruct((t, d_rec), BF16),
                   jax.ShapeDtypeStruct((nchunks, hd, d_rec), F32),
                   jax.ShapeDtypeStruct((nchunks, hd, d_rec), F32),
                   jax.ShapeDtypeStruct((nchunks, d_rec), F32),
                   jax.ShapeDtypeStruct((nchunks, d_rec), F32)],
        compiler_params=_cparams(("arbitrary", "arbitrary"), 32),
        name="hgrn_a",
    )(proj, proj, proj, proj, lb, wf, wb, mf, mb)


def _hgrn_c_body(oi_ref, qtf_ref, qtb_ref, utf_ref, utb_ref, df_ref, db_ref, z_ref, g_ref,
                 out_ref, acc_ref, accb_ref, *, nchunks):
    c = REC_CHUNK
    hd = REC_HEAD_DIM

    unroll = 4

    def step(i, carry):
        st_f, st_b = carry
        pending = []
        for u in range(unroll):
            nf = i * unroll + u
            nb = nchunks - 1 - nf
            rows_f = pl.ds(pl.multiple_of(nf * c, c), c)
            rows_b = pl.ds(pl.multiple_of(nb * c, c), c)
            of = lax.dot_general(qtf_ref[rows_f, :], st_f.astype(BF16), _NT, preferred_element_type=F32)
            ob = lax.dot_general(qtb_ref[rows_b, :], st_b.astype(BF16), _NT, preferred_element_type=F32)
            pending.append((rows_f, rows_b, oi_ref[rows_f, :] + of, ob))
            st_f = df_ref[pl.ds(nf, 1), :] * st_f + utf_ref[nf]
            st_b = db_ref[pl.ds(nb, 1), :] * st_b + utb_ref[nb]
        for rows_f, rows_b, vf, vb in pending:
            acc_ref[rows_f, :] = vf
            accb_ref[rows_b, :] = vb
        return st_f, st_b

    zero = jnp.zeros((hd, hd), F32)
    lax.fori_loop(0, nchunks // unroll, step, (zero, zero))

    o = _rms(acc_ref[...] + accb_ref[...])
    out_ref[...] = (o * g_ref[...] * jax.nn.sigmoid(z_ref[...])).astype(BF16)


def _hgrn_c(oi, qtf, qtb, utf, utb, df, db, proj, g_out, col, bsz, seq, d_rec):
    heads = d_rec // REC_HEAD_DIM
    hd = REC_HEAD_DIM
    nchunks = seq // REC_CHUNK
    row_spec = pl.BlockSpec((seq, hd), lambda b, h: (b, h))
    u_spec = pl.BlockSpec((nchunks, hd, hd), lambda b, h: (b, 0, h))
    d_spec = pl.BlockSpec((nchunks, hd), lambda b, h: (b, h))
    zo = col["z_o"]
    return pl.pallas_call(
        functools.partial(_hgrn_c_body, nchunks=nchunks),
        grid=(bsz, heads),
        in_specs=[row_spec, row_spec, row_spec, u_spec, u_spec, d_spec, d_spec,
                  pl.BlockSpec((seq, hd), lambda b, h: (b, zo + h)),
                  pl.BlockSpec((1, hd), lambda b, h: (0, h))],
        out_specs=row_spec,
        out_shape=jax.ShapeDtypeStruct((bsz * seq, d_rec), BF16),
        scratch_shapes=[pltpu.VMEM((seq, hd), F32), pltpu.VMEM((seq, hd), F32)],
        compiler_params=_cparams(("arbitrary", "arbitrary"), 48),
        name="hgrn_c",
    )(oi, qtf, qtb, utf, utb, df, db, proj, g_out.reshape(1, d_rec))


def _t5_bucket(rel):
    half_buckets = NUM_BUCKETS // 2
    ret = np.where(rel > 0, half_buckets, 0)
    n = np.abs(rel)
    max_exact = half_buckets // 2
    nf = np.maximum(n, 1).astype(np.float32)
    large = max_exact + (np.log(nf / np.float32(max_exact)) / np.float32(math.log(MAX_DISTANCE / max_exact))
                         * np.float32(half_buckets - max_exact)).astype(np.int32)
    large = np.minimum(large, half_buckets - 1)
    return ret + np.where(n < max_exact, n, large)


def _band_bias(rel_bias_g, window, dil):
    half = window // (2 * dil)
    q_off = np.arange(ATT_BLOCK)[:, None]
    rel = np.arange(3 * ATT_BLOCK)[None, :] - ATT_BLOCK - q_off
    onehot = (_t5_bucket(rel * dil)[..., None] == np.arange(NUM_BUCKETS)).astype(np.float32)
    bias = jnp.einsum("qkb,bh->hqk", jnp.asarray(onehot), rel_bias_g.astype(F32),
                      precision=lax.Precision.HIGHEST)
    return jnp.where(jnp.asarray(np.abs(rel) <= half)[None], bias, NEG_INF)


def _attn_body(q_ref, kp_ref, k_ref, kn_ref, vp_ref, v_ref, vn_ref, bias_ref,
               num_ref, st_ref, kc_ref, vc_ref, *, dil, tq, sub_len):
    blk = ATT_BLOCK
    nqb = tq // blk
    n = pl.program_id(1)
    scale = ATT_HEAD_DIM ** -0.5

    def sds(start, size):
        if dil == 1:
            return pl.ds(start, size)
        return pl.ds(start, size, stride=dil)

    cu = kc_ref.shape[0]
    qu = ATT_UNROLL // cu

    def deinterleave(r, j):
        kc_ref[j, 0:blk, :] = kp_ref[sds(r, blk), :].astype(BF16)
        kc_ref[j, blk:blk + tq, :] = k_ref[sds(r, tq), :].astype(BF16)
        kc_ref[j, blk + tq:2 * blk + tq, :] = kn_ref[sds(r, blk), :].astype(BF16)
        vc_ref[j, 0:blk, :] = vp_ref[sds(r, blk), :].astype(BF16)
        vc_ref[j, blk:blk + tq, :] = v_ref[sds(r, tq), :].astype(BF16)
        vc_ref[j, blk + tq:2 * blk + tq, :] = vn_ref[sds(r, blk), :].astype(BF16)

    def units(r0, qb0):
        us = [(j, u) for j in range(cu) for u in range(qu)]
        q0s = [pl.multiple_of((qb0 + u) * blk, blk) for _, u in us]
        rows = [sds(r0 + j + dil * q0, blk) for (j, _), q0 in zip(us, q0s)]
        lane = lax.broadcasted_iota(jnp.int32, (blk, LANE), 1)
        key_iota = lax.broadcasted_iota(jnp.int32, (1, 3 * blk), 1)
        bias = bias_ref[0]
        qs = [q_ref[rw, :].astype(BF16) for rw in rows]
        kws = [kc_ref[j, pl.ds(q0, 3 * blk), :] for (j, _), q0 in zip(us, q0s)]
        vws = [vc_ref[j, pl.ds(q0, 3 * blk), :] for (j, _), q0 in zip(us, q0s)]
        ss = [lax.dot_general(q, kw, _NT, preferred_element_type=F32) * scale for q, kw in zip(qs, kws)]
        valids = []
        for q0 in q0s:
            kpos = n * tq + q0 - blk + key_iota
            valids.append((kpos >= 0) & (kpos < sub_len))
        ss = [jnp.where(valid, s + bias, NEG_INF) for s, valid in zip(ss, valids)]
        ms = [jnp.max(s, axis=-1, keepdims=True) for s in ss]
        ps = [jnp.exp(s - m) for s, m in zip(ss, ms)]
        ls = [jnp.sum(p, axis=-1, keepdims=True) for p in ps]
        nums = [jnp.dot(p.astype(BF16), vw, preferred_element_type=F32) for p, vw in zip(ps, vws)]
        for rw, num, m, l in zip(rows, nums, ms, ls):
            num_ref[rw, :] = num
            st_ref[rw, :] = jnp.where(lane < LANE // 2, m, l)

    def class_group(rg, carry):
        r0 = rg * cu
        for j in range(cu):
            deinterleave(r0 + j, j)

        def qgroup(qg, carry2):
            units(r0, qg * qu)
            return carry2

        lax.fori_loop(0, nqb // qu, qgroup, 0)
        return carry

    lax.fori_loop(0, dil // cu, class_group, 0)


def _attn_group(proj, bias, col, g, dil, bsz, seq):
    tile = 1024
    tq = tile // dil
    halo = ATT_BLOCK * dil
    sub_len = seq // dil
    cu = ATT_UNROLL // min(tq // ATT_BLOCK, ATT_UNROLL)
    nh = ATT_HEADS_PER_GROUP
    hd = ATT_HEAD_DIM
    qc = col["q_a"] + g * nh
    kc = col["k_a"] + g * nh
    vc = col["v_a"] + g * nh
    tiles_b = seq // tile
    halos_b = seq // halo
    hpt = tile // halo

    own = lambda c: pl.BlockSpec((tile, hd), lambda b, n, h: (b * tiles_b + n, c + h))
    prev = lambda c: pl.BlockSpec(
        (halo, hd), lambda b, n, h: (b * halos_b + jnp.maximum(n * hpt - 1, 0), c + h))
    nxt = lambda c: pl.BlockSpec(
        (halo, hd), lambda b, n, h: (b * halos_b + jnp.minimum((n + 1) * hpt, halos_b - 1), c + h))
    t = bsz * seq
    return pl.pallas_call(
        functools.partial(_attn_body, dil=dil, tq=tq, sub_len=sub_len),
        grid=(bsz, tiles_b, nh),
        in_specs=[own(qc), prev(kc), own(kc), nxt(kc), prev(vc), own(vc), nxt(vc),
                  pl.BlockSpec((1,) + bias.shape[1:], lambda b, n, h: (h, 0, 0))],
        out_specs=[pl.BlockSpec((tile, hd), lambda b, n, h: (b * tiles_b + n, h)),
                   pl.BlockSpec((tile, LANE), lambda b, n, h: (b * tiles_b + n, h))],
        out_shape=[jax.ShapeDtypeStruct((t, nh * hd), F32), jax.ShapeDtypeStruct((t, nh * LANE), F32)],
        scratch_shapes=[pltpu.VMEM((cu, tq + 2 * ATT_BLOCK, hd), BF16),
                        pltpu.VMEM((cu, tq + 2 * ATT_BLOCK, hd), BF16)],
        compiler_params=_cparams(("arbitrary", "arbitrary", "arbitrary"), 32),
        name=f"attn_d{dil}",
    )(proj, proj, proj, proj, proj, proj, proj, bias)


def _merge_body(rec_ref, n0_ref, n1_ref, n2_ref, s0_ref, s1_ref, s2_ref, zgr_ref, zga_ref, x_ref,
                ada_ref, gpost_ref, gpre_ref, wbr_ref, wba_ref, wo_ref, wr_ref, br_ref,
                x1_ref, h2_ref, lg_ref):
    nh = ATT_HEADS_PER_GROUP
    hd = ATT_HEAD_DIM
    half = LANE // 2
    lane = lax.broadcasted_iota(jnp.int32, (rec_ref.shape[0], LANE), 1)
    heads = []
    for h in range(nh):
        cols = slice(h * hd, (h + 1) * hd)
        st = [s[:, cols] for s in (s0_ref, s1_ref, s2_ref)]
        top = jnp.maximum(jnp.maximum(st[0], st[1]), st[2])
        ws = [jnp.exp(s - top) for s in st]
        den = (ws[0] * pltpu.roll(st[0], half, 1) + ws[1] * pltpu.roll(st[1], half, 1)
               + ws[2] * pltpu.roll(st[2], half, 1))
        coef = [w / den for w in ws]
        coef = [jnp.where(lane < half, c, pltpu.roll(c, half, 1)) for c in coef]
        num = coef[0] * n0_ref[:, cols] + coef[1] * n1_ref[:, cols] + coef[2] * n2_ref[:, cols]
        heads.append(num.astype(BF16))
    att = jnp.concatenate(heads, axis=1)
    y_rec = jnp.dot(rec_ref[...], wbr_ref[...], preferred_element_type=F32)
    y_att = jnp.dot(att, wba_ref[...], preferred_element_type=F32)
    merged = jax.nn.sigmoid(zgr_ref[...]) * y_rec + jax.nn.sigmoid(zga_ref[...]) * y_att
    y = jnp.dot(merged.astype(BF16), wo_ref[...], preferred_element_type=F32)
    gt_m = ada_ref[0, 2:3, :]
    sh_f = ada_ref[0, 3:4, :]
    sc_f = ada_ref[0, 4:5, :]
    x1 = x_ref[...] + gt_m * (_rms(y) * gpost_ref[...])
    x1_ref[...] = x1
    h2 = _rms(x1) * gpre_ref[...] * (1.0 + sc_f) + sh_f
    tm = h2.shape[0]
    for s in range(SLAB_ROWS):
        h2_ref[pl.ds(s, tm, stride=SLAB_ROWS), :] = h2[:, s * LANE:(s + 1) * LANE]
    ne = lg_ref.shape[1]
    h_hi = h2.astype(BF16)
    h_lo = (h2 - h_hi.astype(F32)).astype(BF16)
    both = jnp.dot(h_hi, wr_ref[...], preferred_element_type=F32)
    cross = jnp.dot(h_lo, wr_ref[:, 0:ne], preferred_element_type=F32)
    lg_ref[...] = both[:, 0:ne] + both[:, ne:2 * ne] + cross + br_ref[...]


def _merge(rec_o, nums, stats, proj, x2, ada3, g_post, g_pre, wbr, wba, wo, w_router, b_router, col, seq):
    t, d = x2.shape
    tm = 256
    per_b = seq // tm
    d_rec = rec_o.shape[1]
    w_att = nums[0].shape[1]
    ne = w_router.shape[1]
    wr_hi = w_router.astype(BF16)
    wr_lo = (w_router - wr_hi.astype(F32)).astype(BF16)
    w_router = jnp.concatenate([wr_hi, wr_lo], axis=1)
    dl = d // LANE
    row = lambda w: pl.BlockSpec((tm, w), lambda i: (i, 0))
    const = lambda shape: pl.BlockSpec(shape, lambda i: (0,) * len(shape), pipeline_mode=pl.Buffered(1))
    zgr = col["zg_rec"] // dl
    zga = col["zg_att"] // dl
    return pl.pallas_call(
        _merge_body,
        grid=(t // tm,),
        in_specs=[row(d_rec), row(w_att), row(w_att), row(w_att), row(w_att), row(w_att), row(w_att),
                  pl.BlockSpec((tm, d), lambda i: (i, zgr)),
                  pl.BlockSpec((tm, d), lambda i: (i, zga)),
                  row(d),
                  pl.BlockSpec((1, 6, d), lambda i: (i // per_b, 0, 0)),
                  const((1, d)), const((1, d)),
                  const(wbr.shape), const(wba.shape), const(wo.shape), const(w_router.shape),
                  const((1, ne))],
        out_specs=[row(d), pl.BlockSpec((tm * SLAB_ROWS, LANE), lambda i: (i, 0)),
                   pl.BlockSpec((tm, ne), lambda i: (i, 0))],
        out_shape=[jax.ShapeDtypeStruct((t, d), F32), jax.ShapeDtypeStruct((t * SLAB_ROWS, LANE), F32),
                   jax.ShapeDtypeStruct((t, ne), F32)],
        compiler_params=_cparams(("arbitrary",), 56),
        name="merge",
    )(rec_o, nums[0], nums[1], nums[2], stats[0], stats[1], stats[2], proj, proj, x2, ada3,
      g_post.reshape(1, d), g_pre.reshape(1, d), wbr, wba, wo, w_router, b_router.reshape(1, ne))


def _route_body(lg_ref, tri_ref, rt_ref, cnt_ref, carry_ref):
    i = pl.program_id(0)
    tr, ne = lg_ref.shape

    @pl.when(i == 0)
    def _():
        carry_ref[...] = jnp.zeros_like(carry_ref)

    l = lg_ref[...]
    lane = lax.broadcasted_iota(jnp.int32, (tr, ne), 1).astype(F32)
    vals, sels, idxs = [], [], []
    for _ in range(TOP_K):
        m = jnp.max(l, axis=-1, keepdims=True)
        idx = jnp.min(jnp.where(l == m, lane, float(ne)), axis=-1, keepdims=True)
        sel = lane == idx
        vals.append(m)
        idxs.append(idx)
        sels.append(sel)
        l = jnp.where(sel, -jnp.inf, l)
    es = [jnp.exp(v - vals[0]) for v in vals]
    tot = es[0] + es[1] + es[2] + es[3]
    chosen = (sels[0] | sels[1] | sels[2] | sels[3]).astype(F32)
    prefix = jnp.dot(tri_ref[...], chosen.astype(BF16), preferred_element_type=F32) + carry_ref[0:1, :]
    out_lane = lax.broadcasted_iota(jnp.int32, (tr, LANE), 1)
    rt = jnp.zeros((tr, LANE), F32)
    for k in range(TOP_K):
        rank = jnp.sum(jnp.where(sels[k], prefix, 0.0), axis=-1, keepdims=True)
        rt = jnp.where(out_lane == k, idxs[k], rt)
        rt = jnp.where(out_lane == TOP_K + k, es[k] / tot, rt)
        rt = jnp.where(out_lane == 2 * TOP_K + k, rank, rt)
    rt_ref[...] = rt
    new = carry_ref[0:1, :] + jnp.sum(chosen, axis=0, keepdims=True)
    carry_ref[...] = jnp.broadcast_to(new, carry_ref.shape)
    cnt_ref[...] = carry_ref[...]


def _route(logits):
    t, ne = logits.shape
    tr = 512
    tri = jnp.asarray(np.tril(np.ones((tr, tr), np.float32), -1), BF16)
    return pl.pallas_call(
        _route_body,
        grid=(t // tr,),
        in_specs=[pl.BlockSpec((tr, ne), lambda i: (i, 0)),
                  pl.BlockSpec((tr, tr), lambda i: (0, 0))],
        out_specs=[pl.BlockSpec((tr, LANE), lambda i: (i, 0)),
                   pl.BlockSpec((SUBLANE, ne), lambda i: (0, 0))],
        out_shape=[jax.ShapeDtypeStruct((t, LANE), F32), jax.ShapeDtypeStruct((SUBLANE, ne), F32)],
        scratch_shapes=[pltpu.VMEM((SUBLANE, ne), F32)],
        compiler_params=_cparams(("arbitrary",), 32),
        name="route",
    )(logits, tri)


def _chunk_rows(ncols):
    return W_CHUNK_BYTES // (4 * ncols)


def _weight_stream(w_hbm, wbf, stage, wsem, e, slot, c0, c1):
    kc = stage.shape[1]

    def copy(c):
        return pltpu.make_async_copy(w_hbm.at[e, pl.ds(pl.multiple_of(c * kc, kc), kc), :],
                                     stage.at[c % 2], wsem.at[c % 2])

    def prime():
        def body(c, carry):
            copy(c).start()
            return carry
        lax.fori_loop(c0, jnp.minimum(c0 + 2, c1), body, 0)

    def finish():
        def body(c, carry):
            copy(c).wait()
            buf = c % 2

            def cast(i, carry2):
                r = pl.multiple_of(i * CAST_ROWS, CAST_ROWS)
                wbf[slot, pl.ds(pl.multiple_of(c * kc, kc) + r, CAST_ROWS), :] = (
                    stage[buf, pl.ds(r, CAST_ROWS), :].astype(BF16))
                return carry2
            lax.fori_loop(0, kc // CAST_ROWS, cast, 0)

            @pl.when(c + 2 < c1)
            def _():
                copy(c + 2).start()
            return carry
        lax.fori_loop(c0, c1, body, 0)

    return prime, finish


def _moe_up_body(be_ref, nu_ref, ws_ref, wn_ref, wc0_ref, wc1_ref, idx0_ref, idxn_ref, h2s_ref, w_hbm, bias_ref,
                 o_ref, xbuf, wbf, stage, sem, wsem):
    b = pl.program_id(0)
    nu = nu_ref[0]
    f = o_ref.shape[1]
    slab = SLAB_ROWS
    nch = wbf.shape[1] // stage.shape[1]

    def row_start(idx_ref, slot, r):
        tok = idx_ref[0, 0, r]
        pltpu.make_async_copy(h2s_ref.at[pl.ds(pl.multiple_of(tok * slab, slab), slab), :],
                              xbuf.at[slot, pl.ds(pl.multiple_of(r * slab, slab), slab), :],
                              sem.at[slot]).start()

    def wait_rows(slot):
        pltpu.make_async_copy(h2s_ref.at[pl.ds(0, BM * slab), :], xbuf.at[slot], sem.at[slot]).wait()

    def issue(idx_ref, slot):
        def body(r, c):
            row_start(idx_ref, slot, r)
            return c
        lax.fori_loop(0, BM, body, 0, unroll=DMA_UNROLL)

    @pl.when(b == 0)
    def _():
        issue(idx0_ref, 0)
        prime0, finish0 = _weight_stream(w_hbm, wbf, stage, wsem, be_ref[0], ws_ref[0], 0, nch)
        prime0()
        finish0()

    @pl.when(b + 1 < nu)
    def _():
        issue(idxn_ref, (b + 1) % 2)

    @pl.when(b < nu)
    def _():
        wslot = ws_ref[b]
        prime, finish = _weight_stream(w_hbm, wbf, stage, wsem, wn_ref[b], 1 - wslot, wc0_ref[b], wc1_ref[b])
        prime()
        slot = b % 2
        wait_rows(slot)
        x = jnp.concatenate([xbuf[slot, pl.ds(s, BM, stride=slab), :].astype(BF16) for s in range(slab)],
                            axis=1)
        half = f // 2
        for c0 in (0, half):
            gate = (jnp.dot(x, wbf[wslot, :, c0:c0 + half], preferred_element_type=F32)
                    + bias_ref[0, :, c0:c0 + half])
            up = (jnp.dot(x, wbf[wslot, :, f + c0:f + c0 + half], preferred_element_type=F32)
                  + bias_ref[0, :, f + c0:f + c0 + half])
            gate = jnp.minimum(gate, SWIGLU_LIMIT)
            up = jnp.clip(up, -SWIGLU_LIMIT, SWIGLU_LIMIT)
            o_ref[:, c0:c0 + half] = (gate * jax.nn.sigmoid(SWIGLU_ALPHA * gate) * (up + 1.0)).astype(BF16)
        finish()

    @pl.when(b >= nu)
    def _():
        o_ref[...] = jnp.zeros_like(o_ref)


def _moe_up(h2s, src_tok, wgu, bgu, sched):
    ne, d, f2 = wgu.shape
    f = f2 // 2
    p = src_tok.shape[0]
    nb = p // BM
    idx3 = src_tok.reshape(nb, 1, BM)
    smem_blk = lambda imap: pl.BlockSpec((1, 1, BM), imap, memory_space=pltpu.SMEM)
    grid_spec = pltpu.PrefetchScalarGridSpec(
        num_scalar_prefetch=6,
        grid=(nb,),
        in_specs=[smem_blk(lambda b, be, *_: (0, 0, 0)),
                  smem_blk(lambda b, be, *_: (jnp.minimum(b + 1, nb - 1), 0, 0)),
                  pl.BlockSpec(memory_space=pl.ANY),
                  pl.BlockSpec(memory_space=pl.ANY),
                  pl.BlockSpec((1, 1, f2), lambda b, be, *_: (be[b], 0, 0))],
        out_specs=pl.BlockSpec((BM, f), lambda b, be, *_: (b, 0)),
        scratch_shapes=[pltpu.VMEM((2, BM * SLAB_ROWS, LANE), F32),
                        pltpu.VMEM((2, d, f2), BF16),
                        pltpu.VMEM((2, _chunk_rows(f2), f2), F32),
                        pltpu.SemaphoreType.DMA((2,)), pltpu.SemaphoreType.DMA((2,))],
    )
    return pl.pallas_call(
        _moe_up_body,
        grid_spec=grid_spec,
        out_shape=jax.ShapeDtypeStruct((p, f), BF16),
        compiler_params=_cparams(("arbitrary",), 58),
        name="moe_up",
    )(sched["blk_e"], sched["n_used"], sched["wslot"], sched["wnext"], sched["wc0"], sched["wc1"],
      idx3, idx3, h2s, wgu, bgu.reshape(ne, 1, f2))


def _moe_down_body(be_ref, nu_ref, ws_ref, wn_ref, wc0_ref, wc1_ref, nv_ref, dst_ref, a_ref, w_hbm, bias_ref,
                   ysc_ref, ybuf, wbf, stage, sem, wsem):
    b = pl.program_id(0)
    nb = pl.num_programs(0)
    nu = nu_ref[0]
    slab = SLAB_ROWS
    nch = wbf.shape[1] // stage.shape[1]

    @pl.when(b == 0)
    def _():
        prime0, finish0 = _weight_stream(w_hbm, wbf, stage, wsem, be_ref[0], ws_ref[0], 0, nch)
        prime0()
        finish0()

    def row_copy(slot, r, d):
        return pltpu.make_async_copy(ybuf.at[slot, pl.ds(pl.multiple_of(r * slab, slab), slab), :],
                                     ysc_ref.at[pl.ds(pl.multiple_of(d * slab, slab), slab), :],
                                     sem.at[slot])

    def drain(step):
        slot = step % 2
        count = nv_ref[step]

        @pl.when(count == BM)
        def _():
            pltpu.make_async_copy(ybuf.at[slot], ysc_ref.at[pl.ds(0, BM * slab), :], sem.at[slot]).wait()

        @pl.when(count < BM)
        def _():
            def body(r, c):
                row_copy(slot, 0, 0).wait()
                return c
            lax.fori_loop(0, count, body, 0)

    @pl.when((b >= 2) & (b < nu))
    def _():
        drain(b - 2)

    @pl.when(b < nu)
    def _():
        slot = b % 2
        wslot = ws_ref[b]
        prime, finish = _weight_stream(w_hbm, wbf, stage, wsem, wn_ref[b], 1 - wslot, wc0_ref[b], wc1_ref[b])
        prime()
        y = jnp.dot(a_ref[...], wbf[wslot], preferred_element_type=F32) + bias_ref[0]
        for s in range(slab):
            ybuf[slot, pl.ds(s, BM, stride=slab), :] = y[:, s * LANE:(s + 1) * LANE]
        finish()

        def body(r, c):
            row_copy(slot, r, dst_ref[0, 0, r]).start()
            return c

        @pl.when(nv_ref[b] == BM)
        def _():
            lax.fori_loop(0, BM, body, 0, unroll=DMA_UNROLL)

        @pl.when(nv_ref[b] < BM)
        def _():
            lax.fori_loop(0, nv_ref[b], body, 0)

    @pl.when(b == nb - 1)
    def _():
        @pl.when(nu >= 2)
        def _():
            drain(nu - 2)
        drain(nu - 1)


def _moe_down(act, dst_slot, n_out_rows, wd, bd, sched):
    p, f = act.shape
    ne, _, d = wd.shape
    nb = p // BM
    grid_spec = pltpu.PrefetchScalarGridSpec(
        num_scalar_prefetch=7,
        grid=(nb,),
        in_specs=[pl.BlockSpec((1, 1, BM), lambda b, be, *_: (b, 0, 0), memory_space=pltpu.SMEM),
                  pl.BlockSpec((BM, f), lambda b, be, *_: (b, 0)),
                  pl.BlockSpec(memory_space=pl.ANY),
                  pl.BlockSpec((1, 1, d), lambda b, be, *_: (be[b], 0, 0))],
        out_specs=pl.BlockSpec(memory_space=pl.ANY),
        scratch_shapes=[pltpu.VMEM((2, BM * SLAB_ROWS, LANE), F32),
                        pltpu.VMEM((2, f, d), BF16),
                        pltpu.VMEM((2, _chunk_rows(d), d), F32),
                        pltpu.SemaphoreType.DMA((2,)), pltpu.SemaphoreType.DMA((2,))],
    )
    return pl.pallas_call(
        _moe_down_body,
        grid_spec=grid_spec,
        out_shape=jax.ShapeDtypeStruct((n_out_rows * SLAB_ROWS, LANE), F32),
        compiler_params=pltpu.CompilerParams(dimension_semantics=("arbitrary",),
                                             vmem_limit_bytes=48 * 1024 * 1024, has_side_effects=True),
        name="moe_down",
    )(sched["blk_e"], sched["n_used"], sched["wslot"], sched["wnext"], sched["wc0"], sched["wc1"],
      sched["nvalid"], dst_slot.reshape(nb, 1, BM), act, wd, bd.reshape(ne, 1, d))


def _final_body(ysc_ref, rt_ref, x1_ref, ada_ref, g_ref, o_ref):
    tm = x1_ref.shape[0]
    rt = rt_ref[...]
    stride = TOP_K * SLAB_ROWS
    pieces = []
    for s in range(SLAB_ROWS):
        acc = rt[:, TOP_K:TOP_K + 1] * ysc_ref[pl.ds(s, tm, stride=stride), :]
        for k in range(1, TOP_K):
            acc = acc + rt[:, TOP_K + k:TOP_K + k + 1] * ysc_ref[pl.ds(k * SLAB_ROWS + s, tm, stride=stride), :]
        pieces.append(acc)
    y = jnp.concatenate(pieces, axis=1)
    gt_f = ada_ref[0, 5:6, :]
    o_ref[...] = x1_ref[...] + gt_f * (_rms(y) * g_ref[...])


def _final(ysc, rt, x1, ada3, g_post, seq):
    t, d = x1.shape
    tm = 256
    per_b = seq // tm
    return pl.pallas_call(
        _final_body,
        grid=(t // tm,),
        in_specs=[pl.BlockSpec((tm * TOP_K * SLAB_ROWS, LANE), lambda i: (i, 0)),
                  pl.BlockSpec((tm, LANE), lambda i: (i, 0)),
                  pl.BlockSpec((tm, d), lambda i: (i, 0)),
                  pl.BlockSpec((1, 6, d), lambda i: (i // per_b, 0, 0)),
                  pl.BlockSpec((1, d), lambda i: (0, 0))],
        out_specs=pl.BlockSpec((tm, d), lambda i: (i, 0)),
        out_shape=jax.ShapeDtypeStruct((t, d), F32),
        compiler_params=_cparams(("arbitrary",), 48),
        name="final",
    )(ysc, rt, x1, ada3, g_post.reshape(1, d))


def _mixer_ffn_layer(x2, ada3, bsz, seq, g_mix_pre, g_mix_post, g_ffn_pre, g_ffn_post, w_in, lb, g_rec_out,
                     rel_bias, w_branch_rec, w_branch_att, w_o, w_router, b_router, w_gate_up, b_gate_up,
                     w_down, b_down):
    t, d = x2.shape
    d_rec = w_branch_rec.shape[0]
    w_att = w_branch_att.shape[0]
    d_att = 3 * w_att
    widths = dict(q_r=d_rec, i_r=d_rec, zf_f=d_rec, zf_b=d_rec, z_o=d_rec, q_a=d_att, k_a=d_att, v_a=d_att,
                  zg_rec=d, zg_att=d)
    my_order = ("zg_rec", "zg_att", "q_r", "i_r", "zf_f", "zf_b", "z_o", "q_a", "k_a", "v_a")
    col, acc = {}, 0
    for name in my_order:
        col[name] = acc // LANE
        acc += widths[name]
    rot = acc - 2 * d

    proj = _inproj(x2, g_mix_pre, ada3, w_in, seq, rot)

    oi, qtf, qtb, utf, utb, df, db = _hgrn_a(proj, lb, col, t, d_rec)
    rec_o = _hgrn_c(oi, qtf, qtb, utf, utb, df, db, proj, g_rec_out, col, bsz, seq, d_rec)

    nums, stats = [], []
    for g, (window, dil) in enumerate(DIL_GROUPS):
        hs = slice(g * ATT_HEADS_PER_GROUP, (g + 1) * ATT_HEADS_PER_GROUP)
        bias = _band_bias(rel_bias[:, hs], window, dil)
        num, st = _attn_group(proj, bias, col, g, dil, bsz, seq)
        nums.append(num)
        stats.append(st)

    x1, h2s, logits = _merge(rec_o, nums, stats, proj, x2, ada3, g_mix_post, g_ffn_pre,
                            w_branch_rec.astype(BF16), w_branch_att.astype(BF16), w_o.astype(BF16),
                            w_router, b_router, col, seq)

    rt, cnt = _route(logits)
    ne = logits.shape[1]
    counts = cnt[0].astype(jnp.int32)
    top_idx = rt[:, 0:TOP_K].astype(jnp.int32)
    rank = rt[:, 2 * TOP_K:3 * TOP_K].astype(jnp.int32)
    padded = (counts + BM - 1) // BM * BM
    pends = jnp.cumsum(padded)
    pstarts = pends - padded
    experts = jnp.arange(ne, dtype=jnp.int32)
    pstart_sel = jnp.sum(jnp.where(top_idx[..., None] == experts, pstarts, 0), axis=-1)
    dest = (pstart_sel + rank).reshape(-1)
    p_rows = t * TOP_K + ne * BM
    nb = p_rows // BM
    blk_start = jnp.arange(nb, dtype=jnp.int32) * BM
    blk_e = jnp.minimum(jnp.sum((pends[None, :] <= blk_start[:, None]).astype(jnp.int32), axis=1), ne - 1)
    n_used = (pends[-1:] // BM).astype(jnp.int32)

    n_assign = t * TOP_K
    slot_assign = jnp.full((p_rows,), -1, jnp.int32).at[dest].set(jnp.arange(n_assign, dtype=jnp.int32))
    valid_end = pstarts + counts
    blk_end = jnp.sum(jnp.where(blk_e[:, None] == experts, valid_end, 0), axis=-1)
    nvalid = jnp.clip(blk_end - blk_start, 0, BM).astype(jnp.int32)
    nvalid = jnp.where(jnp.arange(nb) < n_used[0], nvalid, 0)

    lookup = lambda table: jnp.sum(jnp.where(blk_e[:, None] == experts, table, 0), axis=-1)
    nonempty = padded > 0
    order = jnp.cumsum(nonempty.astype(jnp.int32)) - 1
    later = lax.cummin(jnp.where(nonempty, experts, ne)[::-1])[::-1]
    next_e = jnp.concatenate([later[1:], jnp.full((1,), ne, jnp.int32)])
    blk_next = lookup(next_e)
    has_next = (blk_next < ne) & (jnp.arange(nb) < n_used[0])
    k_in_run = jnp.arange(nb, dtype=jnp.int32) - lookup(pstarts // BM)
    n_in_run = jnp.maximum(lookup(padded // BM), 1)
    common = dict(blk_e=blk_e, n_used=n_used, nvalid=nvalid, wslot=lookup(order) % 2,
                  wnext=jnp.where(has_next, blk_next, blk_e))

    def schedule(w):
        nch = w.shape[1] // _chunk_rows(w.shape[2])
        s = dict(common, wc0=jnp.where(has_next, k_in_run * nch // n_in_run, 0),
                 wc1=jnp.where(has_next, (k_in_run + 1) * nch // n_in_run, 0))
        return {k: v.astype(jnp.int32) for k, v in s.items()}

    src_tok = jnp.maximum(slot_assign, 0) // TOP_K
    act = _moe_up(h2s, src_tok, w_gate_up, b_gate_up, schedule(w_gate_up))
    ysc = _moe_down(act, slot_assign, n_assign, w_down, b_down, schedule(w_down))
    return _final(ysc, rt, x1, ada3, g_ffn_post, seq)


def kernel(x, c, w_ada, b_ada, g_mix_pre, g_mix_post, g_ffn_pre, g_ffn_post, w_in, g_rec_out, w_branch_rec,
           w_branch_att, w_o, w_router, b_router, w_gate_up, b_gate_up, w_down, b_down, rec_lb_table, rel_bias):
    bsz, seq, d = x.shape
    depth = w_in.shape[0]
    lb_all = jnp.cumsum(jax.nn.softmax(rec_lb_table.astype(F32), axis=1), axis=1)
    x2 = x.reshape(bsz * seq, d)
    for layer in range(depth):
        ada3 = _ada(c, w_ada[layer], b_ada[layer]).reshape(bsz, 6, d)
        x2 = _mixer_ffn_layer(x2, ada3, bsz, seq, g_mix_pre[layer], g_mix_post[layer], g_ffn_pre[layer],
                              g_ffn_post[layer], w_in[layer], lb_all[:, layer], g_rec_out[layer], rel_bias,
                              w_branch_rec[layer], w_branch_att[layer], w_o[layer], w_router[layer],
                              b_router[layer], w_gate_up[layer], b_gate_up[layer], w_down[layer],
                              b_down[layer])
    return x2.reshape(bsz, seq, d)
```

```python
import functools
import math

import numpy as np
import jax
import jax.numpy as jnp
from jax import lax
from jax.experimental import pallas as pl
from jax.experimental.pallas import tpu as pltpu

F32 = jnp.float32
BF16 = jnp.bfloat16

LANE = 128
SUBLANE = 8
SLAB_ROWS = 16

REC_HEAD_DIM = 128
REC_CHUNK = 64
ATT_HEAD_DIM = 128
ATT_HEADS_PER_GROUP = 4
ATT_BLOCK = 64
DIL_GROUPS = ((128, 1), (512, 4), (2048, 16))
NUM_BUCKETS = 32
MAX_DISTANCE = 1024
N_EXPERTS = 32
TOP_K = 4
SWIGLU_LIMIT = 7.0
SWIGLU_ALPHA = 1.702
RMS_EPS = 1e-6
NEG_INF = -1e30

N_LEVELS = 6
W_CHUNK_BYTES = 4 * 1024 * 1024
CAST_ROWS = 32
HGRN_UNROLL = 4
DMA_UNROLL = 8
ATT_UNROLL = 8
BM = 256

_NT = (((1,), (1,)), ((), ()))
_TN = (((0,), (0,)), ((), ()))


def _cparams(sem, vmem_mb):
    return pltpu.CompilerParams(dimension_semantics=sem, vmem_limit_bytes=vmem_mb * 1024 * 1024)


def _rms(x):
    return x * lax.rsqrt(jnp.mean(x * x, axis=-1, keepdims=True) + RMS_EPS)


def _ada_body(c_ref, w_ref, b_ref, o_ref):
    c = c_ref[...]
    cond = (c * jax.nn.sigmoid(c)).astype(BF16)
    o_ref[...] = jnp.dot(cond, w_ref[...].astype(BF16), preferred_element_type=F32) + b_ref[...]


def _ada(c, w, b):
    bsz, d = c.shape
    n = w.shape[1]
    tn = 1024
    cp = jnp.zeros((SUBLANE, d), F32).at[:bsz].set(c)
    out = pl.pallas_call(
        _ada_body,
        grid=(n // tn,),
        in_specs=[pl.BlockSpec((SUBLANE, d), lambda j: (0, 0)),
                  pl.BlockSpec((d, tn), lambda j: (0, j)),
                  pl.BlockSpec((1, tn), lambda j: (0, j))],
        out_specs=pl.BlockSpec((SUBLANE, tn), lambda j: (0, j)),
        out_shape=jax.ShapeDtypeStruct((SUBLANE, n), F32),
        compiler_params=_cparams(("arbitrary",), 40),
        name="ada",
    )(cp, w, b.reshape(1, n))
    return out[:bsz]


def _inproj_body(x_ref, g_ref, ada_ref, w0_ref, wn_ref, o_ref, h_ref, wa_ref, wb_ref):
    i = pl.program_id(0)
    j = pl.program_id(1)

    @pl.when(j == 0)
    def _():
        y = _rms(x_ref[...]) * g_ref[...]
        sh = ada_ref[0, 0:1, :]
        sc = ada_ref[0, 1:2, :]
        h_ref[...] = (y * (1.0 + sc) + sh).astype(BF16)

    @pl.when((i == 0) & (j == 0))
    def _():
        wa_ref[...] = w0_ref[...].astype(BF16)

    step = i * pl.num_programs(1) + j

    @pl.when(step % 2 == 0)
    def _():
        o_ref[...] = jnp.dot(h_ref[...], wa_ref[...], preferred_element_type=F32)
        wb_ref[...] = wn_ref[...].astype(BF16)

    @pl.when(step % 2 == 1)
    def _():
        o_ref[...] = jnp.dot(h_ref[...], wb_ref[...], preferred_element_type=F32)
        wa_ref[...] = wn_ref[...].astype(BF16)


def _inproj(x2, g, ada3, w_in, seq, rot):
    t, d = x2.shape
    n = w_in.shape[1]
    tm, tn = 1024, 512
    per_b = seq // tm
    nj = n // tn
    assert rot % tn == 0 and n % tn == 0
    return pl.pallas_call(
        _inproj_body,
        grid=(t // tm, nj),
        in_specs=[pl.BlockSpec((tm, d), lambda i, j: (i, 0)),
                  pl.BlockSpec((1, d), lambda i, j: (0, 0)),
                  pl.BlockSpec((1, 6, d), lambda i, j: (i // per_b, 0, 0)),
                  pl.BlockSpec((d, tn), lambda i, j: (0, (rot // tn) % nj), pipeline_mode=pl.Buffered(1)),
                  pl.BlockSpec((d, tn), lambda i, j: (0, (j + 1 + rot // tn) % nj))],
        out_specs=pl.BlockSpec((tm, tn), lambda i, j: (i, j)),
        out_shape=jax.ShapeDtypeStruct((t, n), F32),
        scratch_shapes=[pltpu.VMEM((tm, d), BF16), pltpu.VMEM((d, tn), BF16), pltpu.VMEM((d, tn), BF16)],
        compiler_params=_cparams(("arbitrary", "arbitrary"), 52),
        name="inproj",
    )(x2, g.reshape(1, d), ada3, w_in, w_in)


def _hgrn_consts():
    c = REC_CHUNK
    r = np.arange(c)[:, None]
    m = np.arange(c)[None, :]
    wf = np.zeros((8 * c, c), np.float32)
    wb = np.zeros((8 * c, c), np.float32)
    mf = np.zeros((N_LEVELS + 1, c, c), np.float32)
    for lvl in range(N_LEVELS):
        s = 32 >> lvl
        m0 = (r // (2 * s)) * (2 * s) + s
        up = r >= m0
        wf[lvl * c:(lvl + 1) * c] = np.where(up, (m >= m0) & (m <= r), (m > r) & (m <= m0 - 1))
        wb[lvl * c:(lvl + 1) * c] = np.where(up, (m >= m0) & (m <= r - 1), (m >= r) & (m <= m0 - 1))
        i = np.arange(c)[:, None]
        j = np.arange(c)[None, :]
        mf[lvl] = (i // (2 * s) == j // (2 * s)) & (i % (2 * s) >= s) & (j % (2 * s) < s)
    mf[N_LEVELS] = np.eye(c)
    wf[6 * c:7 * c] = m <= r
    wf[7 * c:8 * c] = m > r
    wb[6 * c:7 * c] = m >= r
    wb[7 * c:8 * c] = m < r
    mb = np.transpose(mf, (0, 2, 1)).copy()
    wf3 = np.concatenate([wf, wf, wf], axis=1)
    wb3 = np.concatenate([wb, wb, wb], axis=1)
    return (jnp.asarray(wf3, BF16), jnp.asarray(wb3, BF16), jnp.asarray(mf, F32), jnp.asarray(mb, F32))


def _split3(g):
    hi = g.astype(BF16)
    r1 = g - hi.astype(F32)
    mid = r1.astype(BF16)
    lo = (r1 - mid.astype(F32)).astype(BF16)
    return jnp.concatenate([hi, mid, lo], axis=0)


def _hgrn_a_body(q_ref, i_ref, zf_ref, zb_ref, lb_ref, wf_ref, wb_ref, mf_ref, mb_ref,
                 oi_ref, qtf_ref, qtb_ref, utf_ref, utb_ref, df_ref, db_ref, *, cpb):
    c = REC_CHUNK
    dirs = ((zf_ref, wf_ref, mf_ref, qtf_ref, utf_ref, df_ref, 0, c - 1),
            (zb_ref, wb_ref, mb_ref, qtb_ref, utb_ref, db_ref, 1, 0))

    def chunk_group(cg, carry):
        cis = [cg * HGRN_UNROLL + u for u in range(HGRN_UNROLL)]
        rows = [pl.ds(pl.multiple_of(ci * c, c), c) for ci in cis]
        zqs = [q_ref[rw, :] for rw in rows]
        qs = [zq * jax.nn.sigmoid(zq) for zq in zqs]
        vbs = [i_ref[rw, :].astype(BF16) for rw in rows]
        units = [(u, d) for u in range(HGRN_UNROLL) for d in range(2)]
        ks, es = {}, {}
        for u, d in units:
            z_ref, w_ref = dirs[d][0], dirs[d][1]
            lb = lb_ref[d:d + 1, :]
            f = lb + (1.0 - lb) * jax.nn.sigmoid(z_ref[rows[u], :])
            ks[u, d] = 1.0 - f
            es[u, d] = jnp.exp(jnp.dot(w_ref[...], _split3(jnp.log(f)), preferred_element_type=F32))
        acc = [jnp.zeros((c, c), F32) for _ in range(HGRN_UNROLL)]
        for lvl in range(N_LEVELS + 1):
            for u, d in units:
                m_ref = dirs[d][2]
                if lvl < N_LEVELS:
                    el = es[u, d][lvl * c:(lvl + 1) * c]
                    qa, ka = (qs[u] * el).astype(BF16), (ks[u, d] * el).astype(BF16)
                else:
                    qa, ka = qs[u].astype(BF16), ks[u, d].astype(BF16)
                p = lax.dot_general(qa, ka, _NT, preferred_element_type=F32)
                acc[u] = acc[u] + p * m_ref[lvl]
        for u, d in units:
            _, _, _, qt_ref, ut_ref, d_ref, _, drow = dirs[d]
            e = es[u, d]
            qt_ref[rows[u], :] = (qs[u] * e[6 * c:7 * c]).astype(BF16)
            kt = (ks[u, d] * e[7 * c:8 * c]).astype(BF16)
            ut_ref[cis[u]] = lax.dot_general(vbs[u], kt, _TN, preferred_element_type=F32)
            d_ref[pl.ds(cis[u], 1), :] = e[6 * c + drow:6 * c + drow + 1]
        for u in range(HGRN_UNROLL):
            oi_ref[rows[u], :] = jnp.dot(acc[u].astype(BF16), vbs[u], preferred_element_type=F32)
        return carry

    lax.fori_loop(0, cpb // HGRN_UNROLL, chunk_group, 0)


def _hgrn_a(proj, lb, col, t, d_rec):
    heads = d_rec // REC_HEAD_DIM
    tq = 1024
    cpb = tq // REC_CHUNK
    nchunks = t // REC_CHUNK
    wf, wb, mf, mb = _hgrn_consts()
    hd = REC_HEAD_DIM

    def colspec(off):
        return pl.BlockSpec((tq, hd), lambda i, h: (i, off + h))

    full2 = lambda i, h: (0, 0)
    full3 = lambda i, h: (0, 0, 0)
    row_spec = pl.BlockSpec((tq, hd), lambda i, h: (i, h))
    u_spec = pl.BlockSpec((cpb, hd, hd), lambda i, h: (i, 0, h))
    d_spec = pl.BlockSpec((cpb, hd), lambda i, h: (i, h))
    return pl.pallas_call(
        functools.partial(_hgrn_a_body, cpb=cpb),
        grid=(t // tq, heads),
        in_specs=[colspec(col["q_r"]), colspec(col["i_r"]), colspec(col["zf_f"]), colspec(col["zf_b"]),
                  pl.BlockSpec((2, hd), lambda i, h: (0, h)),
                  pl.BlockSpec(wf.shape, full2), pl.BlockSpec(wb.shape, full2),
                  pl.BlockSpec(mf.shape, full3), pl.BlockSpec(mb.shape, full3)],
        out_specs=[row_spec, row_spec, row_spec, u_spec, u_spec, d_spec, d_spec],
        out_shape=[jax.ShapeDtypeStruct((t, d_rec), F32),
                   jax.ShapeDtypeStruct((t, d_rec), BF16),
                   jax.ShapeDtypeStruct((t, d_rec), BF16),
                   jax.ShapeDtypeStruct((nchunks, hd, d_rec), F32),
                   jax.ShapeDtypeStruct((nchunks, hd, d_rec), F32),
                   jax.ShapeDtypeStruct((nchunks, d_rec), F32),
                   jax.ShapeDtypeStruct((nchunks, d_rec), F32)],
        compiler_params=_cparams(("arbitrary", "arbitrary"), 32),
        name="hgrn_a",
    )(proj, proj, proj, proj, lb, wf, wb, mf, mb)


def _hgrn_c_body(oi_ref, qtf_ref, qtb_ref, utf_ref, utb_ref, df_ref, db_ref, z_ref, g_ref,
                 out_ref, acc_ref, accb_ref, *, nchunks):
    c = REC_CHUNK
    hd = REC_HEAD_DIM

    unroll = 4

    def step(i, carry):
        st_f, st_b = carry
        pending = []
        for u in range(unroll):
            nf = i * unroll + u
            nb = nchunks - 1 - nf
            rows_f = pl.ds(pl.multiple_of(nf * c, c), c)
            rows_b = pl.ds(pl.multiple_of(nb * c, c), c)
            of = lax.dot_general(qtf_ref[rows_f, :], st_f.astype(BF16), _NT, preferred_element_type=F32)
            ob = lax.dot_general(qtb_ref[rows_b, :], st_b.astype(BF16), _NT, preferred_element_type=F32)
            pending.append((rows_f, rows_b, oi_ref[rows_f, :] + of, ob))
            st_f = df_ref[pl.ds(nf, 1), :] * st_f + utf_ref[nf]
            st_b = db_ref[pl.ds(nb, 1), :] * st_b + utb_ref[nb]
        for rows_f, rows_b, vf, vb in pending:
            acc_ref[rows_f, :] = vf
            accb_ref[rows_b, :] = vb
        return st_f, st_b

    zero = jnp.zeros((hd, hd), F32)
    lax.fori_loop(0, nchunks // unroll, step, (zero, zero))

    o = _rms(acc_ref[...] + accb_ref[...])
    out_ref[...] = (o * g_ref[...] * jax.nn.sigmoid(z_ref[...])).astype(BF16)


def _hgrn_c(oi, qtf, qtb, utf, utb, df, db, proj, g_out, col, bsz, seq, d_rec):
    heads = d_rec // REC_HEAD_DIM
    hd = REC_HEAD_DIM
    nchunks = seq // REC_CHUNK
    row_spec = pl.BlockSpec((seq, hd), lambda b, h: (b, h))
    u_spec = pl.BlockSpec((nchunks, hd, hd), lambda b, h: (b, 0, h))
    d_spec = pl.BlockSpec((nchunks, hd), lambda b, h: (b, h))
    zo = col["z_o"]
    return pl.pallas_call(
        functools.partial(_hgrn_c_body, nchunks=nchunks),
        grid=(bsz, heads),
        in_specs=[row_spec, row_spec, row_spec, u_spec, u_spec, d_spec, d_spec,
                  pl.BlockSpec((seq, hd), lambda b, h: (b, zo + h)),
                  pl.BlockSpec((1, hd), lambda b, h: (0, h))],
        out_specs=row_spec,
        out_shape=jax.ShapeDtypeStruct((bsz * seq, d_rec), BF16),
        scratch_shapes=[pltpu.VMEM((seq, hd), F32), pltpu.VMEM((seq, hd), F32)],
        compiler_params=_cparams(("arbitrary", "arbitrary"), 48),
        name="hgrn_c",
    )(oi, qtf, qtb, utf, utb, df, db, proj, g_out.reshape(1, d_rec))


def _t5_bucket(rel):
    half_buckets = NUM_BUCKETS // 2
    ret = np.where(rel > 0, half_buckets, 0)
    n = np.abs(rel)
    max_exact = half_buckets // 2
    nf = np.maximum(n, 1).astype(np.float32)
    large = max_exact + (np.log(nf / np.float32(max_exact)) / np.float32(math.log(MAX_DISTANCE / max_exact))
                         * np.float32(half_buckets - max_exact)).astype(np.int32)
    large = np.minimum(large, half_buckets - 1)
    return ret + np.where(n < max_exact, n, large)


def _band_bias(rel_bias_g, window, dil):
    half = window // (2 * dil)
    q_off = np.arange(ATT_BLOCK)[:, None]
    rel = np.arange(3 * ATT_BLOCK)[None, :] - ATT_BLOCK - q_off
    onehot = (_t5_bucket(rel * dil)[..., None] == np.arange(NUM_BUCKETS)).astype(np.float32)
    bias = jnp.einsum("qkb,bh->hqk", jnp.asarray(onehot), rel_bias_g.astype(F32),
                      precision=lax.Precision.HIGHEST)
    return jnp.where(jnp.asarray(np.abs(rel) <= half)[None], bias, NEG_INF)


def _attn_body(q_ref, kp_ref, k_ref, kn_ref, vp_ref, v_ref, vn_ref, bias_ref,
               num_ref, st_ref, kc_ref, vc_ref, *, dil, tq, sub_len):
    blk = ATT_BLOCK
    nqb = tq // blk
    n = pl.program_id(1)
    scale = ATT_HEAD_DIM ** -0.5

    def sds(start, size):
        if dil == 1:
            return pl.ds(start, size)
        return pl.ds(start, size, stride=dil)

    cu = kc_ref.shape[0]
    qu = ATT_UNROLL // cu

    def deinterleave(r, j):
        kc_ref[j, 0:blk, :] = kp_ref[sds(r, blk), :].astype(BF16)
        kc_ref[j, blk:blk + tq, :] = k_ref[sds(r, tq), :].astype(BF16)
        kc_ref[j, blk + tq:2 * blk + tq, :] = kn_ref[sds(r, blk), :].astype(BF16)
        vc_ref[j, 0:blk, :] = vp_ref[sds(r, blk), :].astype(BF16)
        vc_ref[j, blk:blk + tq, :] = v_ref[sds(r, tq), :].astype(BF16)
        vc_ref[j, blk + tq:2 * blk + tq, :] = vn_ref[sds(r, blk), :].astype(BF16)

    def units(r0, qb0):
        us = [(j, u) for j in range(cu) for u in range(qu)]
        q0s = [pl.multiple_of((qb0 + u) * blk, blk) for _, u in us]
        rows = [sds(r0 + j + dil * q0, blk) for (j, _), q0 in zip(us, q0s)]
        lane = lax.broadcasted_iota(jnp.int32, (blk, LANE), 1)
        key_iota = lax.broadcasted_iota(jnp.int32, (1, 3 * blk), 1)
        bias = bias_ref[0]
        qs = [q_ref[rw, :].astype(BF16) for rw in rows]
        kws = [kc_ref[j, pl.ds(q0, 3 * blk), :] for (j, _), q0 in zip(us, q0s)]
        vws = [vc_ref[j, pl.ds(q0, 3 * blk), :] for (j, _), q0 in zip(us, q0s)]
        ss = [lax.dot_general(q, kw, _NT, preferred_element_type=F32) * scale for q, kw in zip(qs, kws)]
        valids = []
        for q0 in q0s:
            kpos = n * tq + q0 - blk + key_iota
            valids.append((kpos >= 0) & (kpos < sub_len))
        ss = [jnp.where(valid, s + bias, NEG_INF) for s, valid in zip(ss, valids)]
        ms = [jnp.max(s, axis=-1, keepdims=True) for s in ss]
        ps = [jnp.exp(s - m) for s, m in zip(ss, ms)]
        ls = [jnp.sum(p, axis=-1, keepdims=True) for p in ps]
        nums = [jnp.dot(p.astype(BF16), vw, preferred_element_type=F32) for p, vw in zip(ps, vws)]
        for rw, num, m, l in zip(rows, nums, ms, ls):
            num_ref[rw, :] = num
            st_ref[rw, :] = jnp.where(lane < LANE // 2, m, l)

    def class_group(rg, carry):
        r0 = rg * cu
        for j in range(cu):
            deinterleave(r0 + j, j)

        def qgroup(qg, carry2):
            units(r0, qg * qu)
            return carry2

        lax.fori_loop(0, nqb // qu, qgroup, 0)
        return carry

    lax.fori_loop(0, dil // cu, class_group, 0)


def _attn_group(proj, bias, col, g, dil, bsz, seq):
    tile = 1024
    tq = tile // dil
    halo = ATT_BLOCK * dil
    sub_len = seq // dil
    cu = ATT_UNROLL // min(tq // ATT_BLOCK, ATT_UNROLL)
    nh = ATT_HEADS_PER_GROUP
    hd = ATT_HEAD_DIM
    qc = col["q_a"] + g * nh
    kc = col["k_a"] + g * nh
    vc = col["v_a"] + g * nh
    tiles_b = seq // tile
    halos_b = seq // halo
    hpt = tile // halo

    own = lambda c: pl.BlockSpec((tile, hd), lambda b, n, h: (b * tiles_b + n, c + h))
    prev = lambda c: pl.BlockSpec(
        (halo, hd), lambda b, n, h: (b * halos_b + jnp.maximum(n * hpt - 1, 0), c + h))
    nxt = lambda c: pl.BlockSpec(
        (halo, hd), lambda b, n, h: (b * halos_b + jnp.minimum((n + 1) * hpt, halos_b - 1), c + h))
    t = bsz * seq
    return pl.pallas_call(
        functools.partial(_attn_body, dil=dil, tq=tq, sub_len=sub_len),
        grid=(bsz, tiles_b, nh),
        in_specs=[own(qc), prev(kc), own(kc), nxt(kc), prev(vc), own(vc), nxt(vc),
                  pl.BlockSpec((1,) + bias.shape[1:], lambda b, n, h: (h, 0, 0))],
        out_specs=[pl.BlockSpec((tile, hd), lambda b, n, h: (b * tiles_b + n, h)),
                   pl.BlockSpec((tile, LANE), lambda b, n, h: (b * tiles_b + n, h))],
        out_shape=[jax.ShapeDtypeStruct((t, nh * hd), F32), jax.ShapeDtypeStruct((t, nh * LANE), F32)],
        scratch_shapes=[pltpu.VMEM((cu, tq + 2 * ATT_BLOCK, hd), BF16),
                        pltpu.VMEM((cu, tq + 2 * ATT_BLOCK, hd), BF16)],
        compiler_params=_cparams(("arbitrary", "arbitrary", "arbitrary"), 32),
        name=f"attn_d{dil}",
    )(proj, proj, proj, proj, proj, proj, proj, bias)


def _merge_body(rec_ref, n0_ref, n1_ref, n2_ref, s0_ref, s1_ref, s2_ref, zgr_ref, zga_ref, x_ref,
                ada_ref, gpost_ref, gpre_ref, wbr_ref, wba_ref, wo_ref, wr_ref, br_ref,
                x1_ref, h2_ref, lg_ref):
    nh = ATT_HEADS_PER_GROUP
    hd = ATT_HEAD_DIM
    half = LANE // 2
    lane = lax.broadcasted_iota(jnp.int32, (rec_ref.shape[0], LANE), 1)
    heads = []
    for h in range(nh):
        cols = slice(h * hd, (h + 1) * hd)
        st = [s[:, cols] for s in (s0_ref, s1_ref, s2_ref)]
        top = jnp.maximum(jnp.maximum(st[0], st[1]), st[2])
        ws = [jnp.exp(s - top) for s in st]
        den = (ws[0] * pltpu.roll(st[0], half, 1) + ws[1] * pltpu.roll(st[1], half, 1)
               + ws[2] * pltpu.roll(st[2], half, 1))
        coef = [w / den for w in ws]
        coef = [jnp.where(lane < half, c, pltpu.roll(c, half, 1)) for c in coef]
        num = coef[0] * n0_ref[:, cols] + coef[1] * n1_ref[:, cols] + coef[2] * n2_ref[:, cols]
        heads.append(num.astype(BF16))
    att = jnp.concatenate(heads, axis=1)
    y_rec = jnp.dot(rec_ref[...], wbr_ref[...], preferred_element_type=F32)
    y_att = jnp.dot(att, wba_ref[...], preferred_element_type=F32)
    merged = jax.nn.sigmoid(zgr_ref[...]) * y_rec + jax.nn.sigmoid(zga_ref[...]) * y_att
    y = jnp.dot(merged.astype(BF16), wo_ref[...], preferred_element_type=F32)
    gt_m = ada_ref[0, 2:3, :]
    sh_f = ada_ref[0, 3:4, :]
    sc_f = ada_ref[0, 4:5, :]
    x1 = x_ref[...] + gt_m * (_rms(y) * gpost_ref[...])
    x1_ref[...] = x1
    h2 = _rms(x1) * gpre_ref[...] * (1.0 + sc_f) + sh_f
    tm = h2.shape[0]
    for s in range(SLAB_ROWS):
        h2_ref[pl.ds(s, tm, stride=SLAB_ROWS), :] = h2[:, s * LANE:(s + 1) * LANE]
    ne = lg_ref.shape[1]
    h_hi = h2.astype(BF16)
    h_lo = (h2 - h_hi.astype(F32)).astype(BF16)
    both = jnp.dot(h_hi, wr_ref[...], preferred_element_type=F32)
    cross = jnp.dot(h_lo, wr_ref[:, 0:ne], preferred_element_type=F32)
    lg_ref[...] = both[:, 0:ne] + both[:, ne:2 * ne] + cross + br_ref[...]


def _merge(rec_o, nums, stats, proj, x2, ada3, g_post, g_pre, wbr, wba, wo, w_router, b_router, col, seq):
    t, d = x2.shape
    tm = 256
    per_b = seq // tm
    d_rec = rec_o.shape[1]
    w_att = nums[0].shape[1]
    ne = w_router.shape[1]
    wr_hi = w_router.astype(BF16)
    wr_lo = (w_router - wr_hi.astype(F32)).astype(BF16)
    w_router = jnp.concatenate([wr_hi, wr_lo], axis=1)
    dl = d // LANE
    row = lambda w: pl.BlockSpec((tm, w), lambda i: (i, 0))
    const = lambda shape: pl.BlockSpec(shape, lambda i: (0,) * len(shape), pipeline_mode=pl.Buffered(1))
    zgr = col["zg_rec"] // dl
    zga = col["zg_att"] // dl
    return pl.pallas_call(
        _merge_body,
        grid=(t // tm,),
        in_specs=[row(d_rec), row(w_att), row(w_att), row(w_att), row(w_att), row(w_att), row(w_att),
                  pl.BlockSpec((tm, d), lambda i: (i, zgr)),
                  pl.BlockSpec((tm, d), lambda i: (i, zga)),
                  row(d),
                  pl.BlockSpec((1, 6, d), lambda i: (i // per_b, 0, 0)),
                  const((1, d)), const((1, d)),
                  const(wbr.shape), const(wba.shape), const(wo.shape), const(w_router.shape),
                  const((1, ne))],
        out_specs=[row(d), pl.BlockSpec((tm * SLAB_ROWS, LANE), lambda i: (i, 0)),
                   pl.BlockSpec((tm, ne), lambda i: (i, 0))],
        out_shape=[jax.ShapeDtypeStruct((t, d), F32), jax.ShapeDtypeStruct((t * SLAB_ROWS, LANE), F32),
                   jax.ShapeDtypeStruct((t, ne), F32)],
        compiler_params=_cparams(("arbitrary",), 56),
        name="merge",
    )(rec_o, nums[0], nums[1], nums[2], stats[0], stats[1], stats[2], proj, proj, x2, ada3,
      g_post.reshape(1, d), g_pre.reshape(1, d), wbr, wba, wo, w_router, b_router.reshape(1, ne))


def _route_body(lg_ref, tri_ref, rt_ref, cnt_ref, carry_ref):
    i = pl.program_id(0)
    tr, ne = lg_ref.shape

    @pl.when(i == 0)
    def _():
        carry_ref[...] = jnp.zeros_like(carry_ref)

    l = lg_ref[...]
    lane = lax.broadcasted_iota(jnp.int32, (tr, ne), 1).astype(F32)
    vals, sels, idxs = [], [], []
    for _ in range(TOP_K):
        m = jnp.max(l, axis=-1, keepdims=True)
        idx = jnp.min(jnp.where(l == m, lane, float(ne)), axis=-1, keepdims=True)
        sel = lane == idx
        vals.append(m)
        idxs.append(idx)
        sels.append(sel)
        l = jnp.where(sel, -jnp.inf, l)
    es = [jnp.exp(v - vals[0]) for v in vals]
    tot = es[0] + es[1] + es[2] + es[3]
    chosen = (sels[0] | sels[1] | sels[2] | sels[3]).astype(F32)
    prefix = jnp.dot(tri_ref[...], chosen.astype(BF16), preferred_element_type=F32) + carry_ref[0:1, :]
    out_lane = lax.broadcasted_iota(jnp.int32, (tr, LANE), 1)
    rt = jnp.zeros((tr, LANE), F32)
    for k in range(TOP_K):
        rank = jnp.sum(jnp.where(sels[k], prefix, 0.0), axis=-1, keepdims=True)
        rt = jnp.where(out_lane == k, idxs[k], rt)
        rt = jnp.where(out_lane == TOP_K + k, es[k] / tot, rt)
        rt = jnp.where(out_lane == 2 * TOP_K + k, rank, rt)
    rt_ref[...] = rt
    new = carry_ref[0:1, :] + jnp.sum(chosen, axis=0, keepdims=True)
    carry_ref[...] = jnp.broadcast_to(new, carry_ref.shape)
    cnt_ref[...] = carry_ref[...]


def _route(logits):
    t, ne = logits.shape
    tr = 512
    tri = jnp.asarray(np.tril(np.ones((tr, tr), np.float32), -1), BF16)
    return pl.pallas_call(
        _route_body,
        grid=(t // tr,),
        in_specs=[pl.BlockSpec((tr, ne), lambda i: (i, 0)),
                  pl.BlockSpec((tr, tr), lambda i: (0, 0))],
        out_specs=[pl.BlockSpec((tr, LANE), lambda i: (i, 0)),
                   pl.BlockSpec((SUBLANE, ne), lambda i: (0, 0))],
        out_shape=[jax.ShapeDtypeStruct((t, LANE), F32), jax.ShapeDtypeStruct((SUBLANE, ne), F32)],
        scratch_shapes=[pltpu.VMEM((SUBLANE, ne), F32)],
        compiler_params=_cparams(("arbitrary",), 32),
        name="route",
    )(logits, tri)


def _chunk_rows(ncols):
    return W_CHUNK_BYTES // (4 * ncols)


def _weight_stream(w_hbm, wbf, stage, wsem, e, slot, c0, c1):
    kc = stage.shape[1]

    def copy(c):
        return pltpu.make_async_copy(w_hbm.at[e, pl.ds(pl.multiple_of(c * kc, kc), kc), :],
                                     stage.at[c % 2], wsem.at[c % 2])

    def prime():
        def body(c, carry):
            copy(c).start()
            return carry
        lax.fori_loop(c0, jnp.minimum(c0 + 2, c1), body, 0)

    def finish():
        def body(c, carry):
            copy(c).wait()
            buf = c % 2

            def cast(i, carry2):
                r = pl.multiple_of(i * CAST_ROWS, CAST_ROWS)
                wbf[slot, pl.ds(pl.multiple_of(c * kc, kc) + r, CAST_ROWS), :] = (
                    stage[buf, pl.ds(r, CAST_ROWS), :].astype(BF16))
                return carry2
            lax.fori_loop(0, kc // CAST_ROWS, cast, 0)

            @pl.when(c + 2 < c1)
            def _():
                copy(c + 2).start()
            return carry
        lax.fori_loop(c0, c1, body, 0)

    return prime, finish


def _moe_up_body(be_ref, nu_ref, ws_ref, wn_ref, wc0_ref, wc1_ref, idx0_ref, idxn_ref, h2s_ref, w_hbm, bias_ref,
                 o_ref, xbuf, wbf, stage, sem, wsem):
    b = pl.program_id(0)
    nu = nu_ref[0]
    f = o_ref.shape[1]
    slab = SLAB_ROWS
    nch = wbf.shape[1] // stage.shape[1]

    def row_start(idx_ref, slot, r):
        tok = idx_ref[0, 0, r]
        pltpu.make_async_copy(h2s_ref.at[pl.ds(pl.multiple_of(tok * slab, slab), slab), :],
                              xbuf.at[slot, pl.ds(pl.multiple_of(r * slab, slab), slab), :],
                              sem.at[slot]).start()

    def wait_rows(slot):
        pltpu.make_async_copy(h2s_ref.at[pl.ds(0, BM * slab), :], xbuf.at[slot], sem.at[slot]).wait()

    def issue(idx_ref, slot):
        def body(r, c):
            row_start(idx_ref, slot, r)
            return c
        lax.fori_loop(0, BM, body, 0, unroll=DMA_UNROLL)

    @pl.when(b == 0)
    def _():
        issue(idx0_ref, 0)
        prime0, finish0 = _weight_stream(w_hbm, wbf, stage, wsem, be_ref[0], ws_ref[0], 0, nch)
        prime0()
        finish0()

    @pl.when(b + 1 < nu)
    def _():
        issue(idxn_ref, (b + 1) % 2)

    @pl.when(b < nu)
    def _():
        wslot = ws_ref[b]
        prime, finish = _weight_stream(w_hbm, wbf, stage, wsem, wn_ref[b], 1 - wslot, wc0_ref[b], wc1_ref[b])
        prime()
        slot = b % 2
        wait_rows(slot)
        x = jnp.concatenate([xbuf[slot, pl.ds(s, BM, stride=slab), :].astype(BF16) for s in range(slab)],
                            axis=1)
        half = f // 2
        for c0 in (0, half):
            gate = (jnp.dot(x, wbf[wslot, :, c0:c0 + half], preferred_element_type=F32)
                    + bias_ref[0, :, c0:c0 + half])
            up = (jnp.dot(x, wbf[wslot, :, f + c0:f + c0 + half], preferred_element_type=F32)
                  + bias_ref[0, :, f + c0:f + c0 + half])
            gate = jnp.minimum(gate, SWIGLU_LIMIT)
            up = jnp.clip(up, -SWIGLU_LIMIT, SWIGLU_LIMIT)
            o_ref[:, c0:c0 + half] = (gate * jax.nn.sigmoid(SWIGLU_ALPHA * gate) * (up + 1.0)).astype(BF16)
        finish()

    @pl.when(b >= nu)
    def _():
        o_ref[...] = jnp.zeros_like(o_ref)


def _moe_up(h2s, src_tok, wgu, bgu, sched):
    ne, d, f2 = wgu.shape
    f = f2 // 2
    p = src_tok.shape[0]
    nb = p // BM
    idx3 = src_tok.reshape(nb, 1, BM)
    smem_blk = lambda imap: pl.BlockSpec((1, 1, BM), imap, memory_space=pltpu.SMEM)
    grid_spec = pltpu.PrefetchScalarGridSpec(
        num_scalar_prefetch=6,
        grid=(nb,),
        in_specs=[smem_blk(lambda b, be, *_: (0, 0, 0)),
                  smem_blk(lambda b, be, *_: (jnp.minimum(b + 1, nb - 1), 0, 0)),
                  pl.BlockSpec(memory_space=pl.ANY),
                  pl.BlockSpec(memory_space=pl.ANY),
                  pl.BlockSpec((1, 1, f2), lambda b, be, *_: (be[b], 0, 0))],
        out_specs=pl.BlockSpec((BM, f), lambda b, be, *_: (b, 0)),
        scratch_shapes=[pltpu.VMEM((2, BM * SLAB_ROWS, LANE), F32),
                        pltpu.VMEM((2, d, f2), BF16),
                        pltpu.VMEM((2, _chunk_rows(f2), f2), F32),
                        pltpu.SemaphoreType.DMA((2,)), pltpu.SemaphoreType.DMA((2,))],
    )
    return pl.pallas_call(
        _moe_up_body,
        grid_spec=grid_spec,
        out_shape=jax.ShapeDtypeStruct((p, f), BF16),
        compiler_params=_cparams(("arbitrary",), 58),
        name="moe_up",
    )(sched["blk_e"], sched["n_used"], sched["wslot"], sched["wnext"], sched["wc0"], sched["wc1"],
      idx3, idx3, h2s, wgu, bgu.reshape(ne, 1, f2))


def _moe_down_body(be_ref, nu_ref, ws_ref, wn_ref, wc0_ref, wc1_ref, nv_ref, dst_ref, a_ref, w_hbm, bias_ref,
                   ysc_ref, ybuf, wbf, stage, sem, wsem):
    b = pl.program_id(0)
    nb = pl.num_programs(0)
    nu = nu_ref[0]
    slab = SLAB_ROWS
    nch = wbf.shape[1] // stage.shape[1]

    @pl.when(b == 0)
    def _():
        prime0, finish0 = _weight_stream(w_hbm, wbf, stage, wsem, be_ref[0], ws_ref[0], 0, nch)
        prime0()
        finish0()

    def row_copy(slot, r, d):
        return pltpu.make_async_copy(ybuf.at[slot, pl.ds(pl.multiple_of(r * slab, slab), slab), :],
                                     ysc_ref.at[pl.ds(pl.multiple_of(d * slab, slab), slab), :],
                                     sem.at[slot])

    def drain(step):
        slot = step % 2
        count = nv_ref[step]

        @pl.when(count == BM)
        def _():
            pltpu.make_async_copy(ybuf.at[slot], ysc_ref.at[pl.ds(0, BM * slab), :], sem.at[slot]).wait()

        @pl.when(count < BM)
        def _():
            def body(r, c):
                row_copy(slot, 0, 0).wait()
                return c
            lax.fori_loop(0, count, body, 0)

    @pl.when((b >= 2) & (b < nu))
    def _():
        drain(b - 2)

    @pl.when(b < nu)
    def _():
        slot = b % 2
        wslot = ws_ref[b]
        prime, finish = _weight_stream(w_hbm, wbf, stage, wsem, wn_ref[b], 1 - wslot, wc0_ref[b], wc1_ref[b])
        prime()
        y = jnp.dot(a_ref[...], wbf[wslot], preferred_element_type=F32) + bias_ref[0]
        for s in range(slab):
            ybuf[slot, pl.ds(s, BM, stride=slab), :] = y[:, s * LANE:(s + 1) * LANE]
        finish()

        def body(r, c):
            row_copy(slot, r, dst_ref[0, 0, r]).start()
            return c

        @pl.when(nv_ref[b] == BM)
        def _():
            lax.fori_loop(0, BM, body, 0, unroll=DMA_UNROLL)

        @pl.when(nv_ref[b] < BM)
        def _():
            lax.fori_loop(0, nv_ref[b], body, 0)

    @pl.when(b == nb - 1)
    def _():
        @pl.when(nu >= 2)
        def _():
            drain(nu - 2)
        drain(nu - 1)


def _moe_down(act, dst_slot, n_out_rows, wd, bd, sched):
    p, f = act.shape
    ne, _, d = wd.shape
    nb = p // BM
    grid_spec = pltpu.PrefetchScalarGridSpec(
        num_scalar_prefetch=7,
        grid=(nb,),
        in_specs=[pl.BlockSpec((1, 1, BM), lambda b, be, *_: (b, 0, 0), memory_space=pltpu.SMEM),
                  pl.BlockSpec((BM, f), lambda b, be, *_: (b, 0)),
                  pl.BlockSpec(memory_space=pl.ANY),
                  pl.BlockSpec((1, 1, d), lambda b, be, *_: (be[b], 0, 0))],
        out_specs=pl.BlockSpec(memory_space=pl.ANY),
        scratch_shapes=[pltpu.VMEM((2, BM * SLAB_ROWS, LANE), F32),
                        pltpu.VMEM((2, f, d), BF16),
                        pltpu.VMEM((2, _chunk_rows(d), d), F32),
                        pltpu.SemaphoreType.DMA((2,)), pltpu.SemaphoreType.DMA((2,))],
    )
    return pl.pallas_call(
        _moe_down_body,
        grid_spec=grid_spec,
        out_shape=jax.ShapeDtypeStruct((n_out_rows * SLAB_ROWS, LANE), F32),
        compiler_params=pltpu.CompilerParams(dimension_semantics=("arbitrary",),
                                             vmem_limit_bytes=48 * 1024 * 1024, has_side_effects=True),
        name="moe_down",
    )(sched["blk_e"], sched["n_used"], sched["wslot"], sched["wnext"], sched["wc0"], sched["wc1"],
      sched["nvalid"], dst_slot.reshape(nb, 1, BM), act, wd, bd.reshape(ne, 1, d))


def _final_body(ysc_ref, rt_ref, x1_ref, ada_ref, g_ref, o_ref):
    tm = x1_ref.shape[0]
    rt = rt_ref[...]
    stride = TOP_K * SLAB_ROWS
    pieces = []
    for s in range(SLAB_ROWS):
        acc = rt[:, TOP_K:TOP_K + 1] * ysc_ref[pl.ds(s, tm, stride=stride), :]
        for k in range(1, TOP_K):
            acc = acc + rt[:, TOP_K + k:TOP_K + k + 1] * ysc_ref[pl.ds(k * SLAB_ROWS + s, tm, stride=stride), :]
        pieces.append(acc)
    y = jnp.concatenate(pieces, axis=1)
    gt_f = ada_ref[0, 5:6, :]
    o_ref[...] = x1_ref[...] + gt_f * (_rms(y) * g_ref[...])


def _final(ysc, rt, x1, ada3, g_post, seq):
    t, d = x1.shape
    tm = 256
    per_b = seq // tm
    return pl.pallas_call(
        _final_body,
        grid=(t // tm,),
        in_specs=[pl.BlockSpec((tm * TOP_K * SLAB_ROWS, LANE), lambda i: (i, 0)),
                  pl.BlockSpec((tm, LANE), lambda i: (i, 0)),
                  pl.BlockSpec((tm, d), lambda i: (i, 0)),
                  pl.BlockSpec((1, 6, d), lambda i: (i // per_b, 0, 0)),
                  pl.BlockSpec((1, d), lambda i: (0, 0))],
        out_specs=pl.BlockSpec((tm, d), lambda i: (i, 0)),
        out_shape=jax.ShapeDtypeStruct((t, d), F32),
        compiler_params=_cparams(("arbitrary",), 48),
        name="final",
    )(ysc, rt, x1, ada3, g_post.reshape(1, d))


def _mixer_ffn_layer(x2, ada3, bsz, seq, g_mix_pre, g_mix_post, g_ffn_pre, g_ffn_post, w_in, lb, g_rec_out,
                     rel_bias, w_branch_rec, w_branch_att, w_o, w_router, b_router, w_gate_up, b_gate_up,
                     w_down, b_down):
    t, d = x2.shape
    d_rec = w_branch_rec.shape[0]
    w_att = w_branch_att.shape[0]
    d_att = 3 * w_att
    widths = dict(q_r=d_rec, i_r=d_rec, zf_f=d_rec, zf_b=d_rec, z_o=d_rec, q_a=d_att, k_a=d_att, v_a=d_att,
                  zg_rec=d, zg_att=d)
    my_order = ("zg_rec", "zg_att", "q_r", "i_r", "zf_f", "zf_b", "z_o", "q_a", "k_a", "v_a")
    col, acc = {}, 0
    for name in my_order:
        col[name] = acc // LANE
        acc += widths[name]
    rot = acc - 2 * d

    proj = _inproj(x2, g_mix_pre, ada3, w_in, seq, rot)

    oi, qtf, qtb, utf, utb, df, db = _hgrn_a(proj, lb, col, t, d_rec)
    rec_o = _hgrn_c(oi, qtf, qtb, utf, utb, df, db, proj, g_rec_out, col, bsz, seq, d_rec)

    nums, stats = [], []
    for g, (window, dil) in enumerate(DIL_GROUPS):
        hs = slice(g * ATT_HEADS_PER_GROUP, (g + 1) * ATT_HEADS_PER_GROUP)
        bias = _band_bias(rel_bias[:, hs], window, dil)
        num, st = _attn_group(proj, bias, col, g, dil, bsz, seq)
        nums.append(num)
        stats.append(st)

    x1, h2s, logits = _merge(rec_o, nums, stats, proj, x2, ada3, g_mix_post, g_ffn_pre,
                            w_branch_rec.astype(BF16), w_branch_att.astype(BF16), w_o.astype(BF16),
                            w_router, b_router, col, seq)

    rt, cnt = _route(logits)
    ne = logits.shape[1]
    counts = cnt[0].astype(jnp.int32)
    top_idx = rt[:, 0:TOP_K].astype(jnp.int32)
    rank = rt[:, 2 * TOP_K:3 * TOP_K].astype(jnp.int32)
    padded = (counts + BM - 1) // BM * BM
    pends = jnp.cumsum(padded)
    pstarts = pends - padded
    experts = jnp.arange(ne, dtype=jnp.int32)
    pstart_sel = jnp.sum(jnp.where(top_idx[..., None] == experts, pstarts, 0), axis=-1)
    dest = (pstart_sel + rank).reshape(-1)
    p_rows = t * TOP_K + ne * BM
    nb = p_rows // BM
    blk_start = jnp.arange(nb, dtype=jnp.int32) * BM
    blk_e = jnp.minimum(jnp.sum((pends[None, :] <= blk_start[:, None]).astype(jnp.int32), axis=1), ne - 1)
    n_used = (pends[-1:] // BM).astype(jnp.int32)

    n_assign = t * TOP_K
    slot_assign = jnp.full((p_rows,), -1, jnp.int32).at[dest].set(jnp.arange(n_assign, dtype=jnp.int32))
    valid_end = pstarts + counts
    blk_end = jnp.sum(jnp.where(blk_e[:, None] == experts, valid_end, 0), axis=-1)
    nvalid = jnp.clip(blk_end - blk_start, 0, BM).astype(jnp.int32)
    nvalid = jnp.where(jnp.arange(nb) < n_used[0], nvalid, 0)

    lookup = lambda table: jnp.sum(jnp.where(blk_e[:, None] == experts, table, 0), axis=-1)
    nonempty = padded > 0
    order = jnp.cumsum(nonempty.astype(jnp.int32)) - 1
    later = lax.cummin(jnp.where(nonempty, experts, ne)[::-1])[::-1]
    next_e = jnp.concatenate([later[1:], jnp.full((1,), ne, jnp.int32)])
    blk_next = lookup(next_e)
    has_next = (blk_next < ne) & (jnp.arange(nb) < n_used[0])
    k_in_run = jnp.arange(nb, dtype=jnp.int32) - lookup(pstarts // BM)
    n_in_run = jnp.maximum(lookup(padded // BM), 1)
    common = dict(blk_e=blk_e, n_used=n_used, nvalid=nvalid, wslot=lookup(order) % 2,
                  wnext=jnp.where(has_next, blk_next, blk_e))

    def schedule(w):
        nch = w.shape[1] // _chunk_rows(w.shape[2])
        s = dict(common, wc0=jnp.where(has_next, k_in_run * nch // n_in_run, 0),
                 wc1=jnp.where(has_next, (k_in_run + 1) * nch // n_in_run, 0))
        return {k: v.astype(jnp.int32) for k, v in s.items()}

    src_tok = jnp.maximum(slot_assign, 0) // TOP_K
    act = _moe_up(h2s, src_tok, w_gate_up, b_gate_up, schedule(w_gate_up))
    ysc = _moe_down(act, slot_assign, n_assign, w_down, b_down, schedule(w_down))
    return _final(ysc, rt, x1, ada3, g_ffn_post, seq)


def kernel(x, c, w_ada, b_ada, g_mix_pre, g_mix_post, g_ffn_pre, g_ffn_post, w_in, g_rec_out, w_branch_rec,
           w_branch_att, w_o, w_router, b_router, w_gate_up, b_gate_up, w_down, b_down, rec_lb_table, rel_bias):
    bsz, seq, d = x.shape
    depth = w_in.shape[0]
    lb_all = jnp.cumsum(jax.nn.softmax(rec_lb_table.astype(F32), axis=1), axis=1)
    x2 = x.reshape(bsz * seq, d)
    for layer in range(depth):
        ada3 = _ada(c, w_ada[layer], b_ada[layer]).reshape(bsz, 6, d)
        x2 = _mixer_ffn_layer(x2, ada3, bsz, seq, g_mix_pre[layer], g_mix_post[layer], g_ffn_pre[layer],
                              g_ffn_post[layer], w_in[layer], lb_all[:, layer], g_rec_out[layer], rel_bias,
                              w_branch_rec[layer], w_branch_att[layer], w_o[layer], w_router[layer],
                              b_router[layer], w_gate_up[layer], b_gate_up[layer], w_down[layer],
                              b_down[layer])
    return x2.reshape(bsz, seq, d)
```

```python
import functools
import math

import numpy as np
import jax
import jax.numpy as jnp
from jax import lax
from jax.experimental import pallas as pl
from jax.experimental.pallas import tpu as pltpu

F32 = jnp.float32
BF16 = jnp.bfloat16

LANE = 128
SUBLANE = 8
SLAB_ROWS = 16

REC_HEAD_DIM = 128
REC_CHUNK = 64
ATT_HEAD_DIM = 128
ATT_HEADS_PER_GROUP = 4
ATT_BLOCK = 64
DIL_GROUPS = ((128, 1), (512, 4), (2048, 16))
NUM_BUCKETS = 32
MAX_DISTANCE = 1024
N_EXPERTS = 32
TOP_K = 4
SWIGLU_LIMIT = 7.0
SWIGLU_ALPHA = 1.702
RMS_EPS = 1e-6
NEG_INF = -1e30

N_LEVELS = 6
W_CHUNK_BYTES = 4 * 1024 * 1024
CAST_ROWS = 32
HGRN_UNROLL = 4
DMA_UNROLL = 8
ATT_UNROLL = 8
BM = 256

_NT = (((1,), (1,)), ((), ()))
_TN = (((0,), (0,)), ((), ()))


def _cparams(sem, vmem_mb):
    return pltpu.CompilerParams(dimension_semantics=sem, vmem_limit_bytes=vmem_mb * 1024 * 1024)


def _rms(x):
    return x * lax.rsqrt(jnp.mean(x * x, axis=-1, keepdims=True) + RMS_EPS)


def _ada_body(c_ref, w_ref, b_ref, o_ref):
    c = c_ref[...]
    cond = (c * jax.nn.sigmoid(c)).astype(BF16)
    o_ref[...] = jnp.dot(cond, w_ref[...].astype(BF16), preferred_element_type=F32) + b_ref[...]


def _ada(c, w, b):
    bsz, d = c.shape
    n = w.shape[1]
    tn = 1024
    cp = jnp.zeros((SUBLANE, d), F32).at[:bsz].set(c)
    out = pl.pallas_call(
        _ada_body,
        grid=(n // tn,),
        in_specs=[pl.BlockSpec((SUBLANE, d), lambda j: (0, 0)),
                  pl.BlockSpec((d, tn), lambda j: (0, j)),
                  pl.BlockSpec((1, tn), lambda j: (0, j))],
        out_specs=pl.BlockSpec((SUBLANE, tn), lambda j: (0, j)),
        out_shape=jax.ShapeDtypeStruct((SUBLANE, n), F32),
        compiler_params=_cparams(("arbitrary",), 40),
        name="ada",
    )(cp, w, b.reshape(1, n))
    return out[:bsz]


def _inproj_body(x_ref, g_ref, ada_ref, w_ref, o_ref, h_ref):
    @pl.when(pl.program_id(1) == 0)
    def _():
        half = x_ref.shape[0] // 2
        sh = ada_ref[0, 0:1, :]
        sc = ada_ref[0, 1:2, :]
        for r0 in (0, half):
            y = _rms(x_ref[r0:r0 + half, :]) * g_ref[...]
            h_ref[r0:r0 + half, :] = (y * (1.0 + sc) + sh).astype(BF16)

    o_ref[...] = jnp.dot(h_ref[...], w_ref[...].astype(BF16), preferred_element_type=F32)


def _inproj(x2, g, ada3, w_in, seq, rot):
    t, d = x2.shape
    n = w_in.shape[1]
    tm, tn = 2048, 512
    per_b = seq // tm
    nj = n // tn
    assert rot % tn == 0 and n % tn == 0 and seq % tm == 0
    return pl.pallas_call(
        _inproj_body,
        grid=(t // tm, nj),
        in_specs=[pl.BlockSpec((tm, d), lambda i, j: (i, 0), pipeline_mode=pl.Buffered(1)),
                  pl.BlockSpec((1, d), lambda i, j: (0, 0)),
                  pl.BlockSpec((1, 6, d), lambda i, j: (i // per_b, 0, 0)),
                  pl.BlockSpec((d, tn), lambda i, j: (0, (j + rot // tn) % nj))],
        out_specs=pl.BlockSpec((tm, tn), lambda i, j: (i, j)),
        out_shape=jax.ShapeDtypeStruct((t, n), F32),
        scratch_shapes=[pltpu.VMEM((tm, d), BF16)],
        compiler_params=_cparams(("arbitrary", "arbitrary"), 56),
        name="inproj",
    )(x2, g.reshape(1, d), ada3, w_in)


def _hgrn_consts():
    c = REC_CHUNK
    r = np.arange(c)[:, None]
    m = np.arange(c)[None, :]
    wf = np.zeros((8 * c, c), np.float32)
    wb = np.zeros((8 * c, c), np.float32)
    mf = np.zeros((N_LEVELS + 1, c, c), np.float32)
    for lvl in range(N_LEVELS):
        s = 32 >> lvl
        m0 = (r // (2 * s)) * (2 * s) + s
        up = r >= m0
        wf[lvl * c:(lvl + 1) * c] = np.where(up, (m >= m0) & (m <= r), (m > r) & (m <= m0 - 1))
        wb[lvl * c:(lvl + 1) * c] = np.where(up, (m >= m0) & (m <= r - 1), (m >= r) & (m <= m0 - 1))
        i = np.arange(c)[:, None]
        j = np.arange(c)[None, :]
        mf[lvl] = (i // (2 * s) == j // (2 * s)) & (i % (2 * s) >= s) & (j % (2 * s) < s)
    mf[N_LEVELS] = np.eye(c)
    wf[6 * c:7 * c] = m <= r
    wf[7 * c:8 * c] = m > r
    wb[6 * c:7 * c] = m >= r
    wb[7 * c:8 * c] = m < r
    mb = np.transpose(mf, (0, 2, 1)).copy()
    wf3 = np.concatenate([wf, wf, wf], axis=1)
    wb3 = np.concatenate([wb, wb, wb], axis=1)
    return (jnp.asarray(wf3, BF16), jnp.asarray(wb3, BF16), jnp.asarray(mf, F32), jnp.asarray(mb, F32))


def _split3(g):
    hi = g.astype(BF16)
    r1 = g - hi.astype(F32)
    mid = r1.astype(BF16)
    lo = (r1 - mid.astype(F32)).astype(BF16)
    return jnp.concatenate([hi, mid, lo], axis=0)


def _hgrn_a_body(q_ref, i_ref, zf_ref, zb_ref, lb_ref, wf_ref, wb_ref, mf_ref, mb_ref,
                 oi_ref, qtf_ref, qtb_ref, utf_ref, utb_ref, df_ref, db_ref, *, cpb):
    c = REC_CHUNK
    dirs = ((zf_ref, wf_ref, mf_ref, qtf_ref, utf_ref, df_ref, 0, c - 1),
            (zb_ref, wb_ref, mb_ref, qtb_ref, utb_ref, db_ref, 1, 0))

    def chunk_group(cg, carry):
        cis = [cg * HGRN_UNROLL + u for u in range(HGRN_UNROLL)]
        rows = [pl.ds(pl.multiple_of(ci * c, c), c) for ci in cis]
        zqs = [q_ref[rw, :] for rw in rows]
        qs = [zq * jax.nn.sigmoid(zq) for zq in zqs]
        vbs = [i_ref[rw, :].astype(BF16) for rw in rows]
        units = [(u, d) for u in range(HGRN_UNROLL) for d in range(2)]
        ks, es = {}, {}
        for u, d in units:
            z_ref, w_ref = dirs[d][0], dirs[d][1]
            lb = lb_ref[d:d + 1, :]
            f = lb + (1.0 - lb) * jax.nn.sigmoid(z_ref[rows[u], :])
            ks[u, d] = 1.0 - f
            es[u, d] = jnp.exp(jnp.dot(w_ref[...], _split3(jnp.log(f)), preferred_element_type=F32))
        acc = [jnp.zeros((c, c), F32) for _ in range(HGRN_UNROLL)]
        for lvl in range(N_LEVELS + 1):
            for u, d in units:
                m_ref = dirs[d][2]
                if lvl < N_LEVELS:
                    el = es[u, d][lvl * c:(lvl + 1) * c]
                    qa, ka = (qs[u] * el).astype(BF16), (ks[u, d] * el).astype(BF16)
                else:
                    qa, ka = qs[u].astype(BF16), ks[u, d].astype(BF16)
                p = lax.dot_general(qa, ka, _NT, preferred_element_type=F32)
                acc[u] = acc[u] + p * m_ref[lvl]
        for u, d in units:
            _, _, _, qt_ref, ut_ref, d_ref, _, drow = dirs[d]
            e = es[u, d]
            qt_ref[rows[u], :] = (qs[u] * e[6 * c:7 * c]).astype(BF16)
            kt = (ks[u, d] * e[7 * c:8 * c]).astype(BF16)
            ut_ref[cis[u]] = lax.dot_general(vbs[u], kt, _TN, preferred_element_type=F32)
            d_ref[pl.ds(cis[u], 1), :] = e[6 * c + drow:6 * c + drow + 1]
        for u in range(HGRN_UNROLL):
            oi_ref[rows[u], :] = jnp.dot(acc[u].astype(BF16), vbs[u], preferred_element_type=F32)
        return carry

    lax.fori_loop(0, cpb // HGRN_UNROLL, chunk_group, 0)


def _hgrn_a(proj, lb, col, t, d_rec):
    heads = d_rec // REC_HEAD_DIM
    tq = 1024
    cpb = tq // REC_CHUNK
    nchunks = t // REC_CHUNK
    wf, wb, mf, mb = _hgrn_consts()
    hd = REC_HEAD_DIM

    def colspec(off):
        return pl.BlockSpec((tq, hd), lambda i, h: (i, off + h))

    full2 = lambda i, h: (0, 0)
    full3 = lambda i, h: (0, 0, 0)
    row_spec = pl.BlockSpec((tq, hd), lambda i, h: (i, h))
    u_spec = pl.BlockSpec((cpb, hd, hd), lambda i, h: (i, 0, h))
    d_spec = pl.BlockSpec((cpb, hd), lambda i, h: (i, h))
    return pl.pallas_call(
        functools.partial(_hgrn_a_body, cpb=cpb),
        grid=(t // tq, heads),
        in_specs=[colspec(col["q_r"]), colspec(col["i_r"]), colspec(col["zf_f"]), colspec(col["zf_b"]),
                  pl.BlockSpec((2, hd), lambda i, h: (0, h)),
                  pl.BlockSpec(wf.shape, full2), pl.BlockSpec(wb.shape, full2),
                  pl.BlockSpec(mf.shape, full3), pl.BlockSpec(mb.shape, full3)],
        out_specs=[row_spec, row_spec, row_spec, u_spec, u_spec, d_spec, d_spec],
        out_shape=[jax.ShapeDtypeStruct((t, d_rec), F32),
                   jax.ShapeDtypeStruct((t, d_rec), BF16),
                   jax.ShapeDtypeStruct((t, d_rec), BF16),
                   jax.ShapeDtypeStruct((nchunks, hd, d_rec), F32),
                   jax.ShapeDtypeStruct((nchunks, hd, d_rec), F32),
                   jax.ShapeDtypeStruct((nchunks, d_rec), F32),
                   jax.ShapeDtypeStruct((nchunks, d_rec), F32)],
        compiler_params=_cparams(("arbitrary", "arbitrary"), 32),
        name="hgrn_a",
    )(proj, proj, proj, proj, lb, wf, wb, mf, mb)


def _hgrn_c_body(oi_ref, qtf_ref, qtb_ref, utf_ref, utb_ref, df_ref, db_ref, z_ref, g_ref,
                 out_ref, acc_ref, accb_ref, *, nchunks):
    c = REC_CHUNK
    hd = REC_HEAD_DIM

    unroll = 4

    def step(i, carry):
        st_f, st_b = carry
        pending = []
        for u in range(unroll):
            nf = i * unroll + u
            nb = nchunks - 1 - nf
            rows_f = pl.ds(pl.multiple_of(nf * c, c), c)
            rows_b = pl.ds(pl.multiple_of(nb * c, c), c)
            of = lax.dot_general(qtf_ref[rows_f, :], st_f.astype(BF16), _NT, preferred_element_type=F32)
            ob = lax.dot_general(qtb_ref[rows_b, :], st_b.astype(BF16), _NT, preferred_element_type=F32)
            pending.append((rows_f, rows_b, oi_ref[rows_f, :] + of, ob))
            st_f = df_ref[pl.ds(nf, 1), :] * st_f + utf_ref[nf]
            st_b = db_ref[pl.ds(nb, 1), :] * st_b + utb_ref[nb]
        for rows_f, rows_b, vf, vb in pending:
            acc_ref[rows_f, :] = vf
            accb_ref[rows_b, :] = vb
        return st_f, st_b

    zero = jnp.zeros((hd, hd), F32)
    lax.fori_loop(0, nchunks // unroll, step, (zero, zero))

    o = _rms(acc_ref[...] + accb_ref[...])
    out_ref[...] = (o * g_ref[...] * jax.nn.sigmoid(z_ref[...])).astype(BF16)


def _hgrn_c(oi, qtf, qtb, utf, utb, df, db, proj, g_out, col, bsz, seq, d_rec):
    heads = d_rec // REC_HEAD_DIM
    hd = REC_HEAD_DIM
    nchunks = seq // REC_CHUNK
    row_spec = pl.BlockSpec((seq, hd), lambda b, h: (b, h))
    u_spec = pl.BlockSpec((nchunks, hd, hd), lambda b, h: (b, 0, h))
    d_spec = pl.BlockSpec((nchunks, hd), lambda b, h: (b, h))
    zo = col["z_o"]
    return pl.pallas_call(
        functools.partial(_hgrn_c_body, nchunks=nchunks),
        grid=(bsz, heads),
        in_specs=[row_spec, row_spec, row_spec, u_spec, u_spec, d_spec, d_spec,
                  pl.BlockSpec((seq, hd), lambda b, h: (b, zo + h)),
                  pl.BlockSpec((1, hd), lambda b, h: (0, h))],
        out_specs=row_spec,
        out_shape=jax.ShapeDtypeStruct((bsz * seq, d_rec), BF16),
        scratch_shapes=[pltpu.VMEM((seq, hd), F32), pltpu.VMEM((seq, hd), F32)],
        compiler_params=_cparams(("arbitrary", "arbitrary"), 48),
        name="hgrn_c",
    )(oi, qtf, qtb, utf, utb, df, db, proj, g_out.reshape(1, d_rec))


def _t5_bucket(rel):
    half_buckets = NUM_BUCKETS // 2
    ret = np.where(rel > 0, half_buckets, 0)
    n = np.abs(rel)
    max_exact = half_buckets // 2
    nf = np.maximum(n, 1).astype(np.float32)
    large = max_exact + (np.log(nf / np.float32(max_exact)) / np.float32(math.log(MAX_DISTANCE / max_exact))
                         * np.float32(half_buckets - max_exact)).astype(np.int32)
    large = np.minimum(large, half_buckets - 1)
    return ret + np.where(n < max_exact, n, large)


def _band_bias(rel_bias_g, window, dil):
    half = window // (2 * dil)
    q_off = np.arange(ATT_BLOCK)[:, None]
    rel = np.arange(3 * ATT_BLOCK)[None, :] - ATT_BLOCK - q_off
    onehot = (_t5_bucket(rel * dil)[..., None] == np.arange(NUM_BUCKETS)).astype(np.float32)
    bias = jnp.einsum("qkb,bh->hqk", jnp.asarray(onehot), rel_bias_g.astype(F32),
                      precision=lax.Precision.HIGHEST)
    return jnp.where(jnp.asarray(np.abs(rel) <= half)[None], bias, NEG_INF)


def _attn_body(q_ref, kp_ref, k_ref, kn_ref, vp_ref, v_ref, vn_ref, bias_ref,
               num_ref, st_ref, kc_ref, vc_ref, *, dil, tq, sub_len):
    blk = ATT_BLOCK
    nqb = tq // blk
    n = pl.program_id(1)
    scale = ATT_HEAD_DIM ** -0.5

    def sds(start, size):
        if dil == 1:
            return pl.ds(start, size)
        return pl.ds(start, size, stride=dil)

    cu = kc_ref.shape[0]
    qu = ATT_UNROLL // cu

    def deinterleave(r, j):
        kc_ref[j, 0:blk, :] = kp_ref[sds(r, blk), :].astype(BF16)
        kc_ref[j, blk:blk + tq, :] = k_ref[sds(r, tq), :].astype(BF16)
        kc_ref[j, blk + tq:2 * blk + tq, :] = kn_ref[sds(r, blk), :].astype(BF16)
        vc_ref[j, 0:blk, :] = vp_ref[sds(r, blk), :].astype(BF16)
        vc_ref[j, blk:blk + tq, :] = v_ref[sds(r, tq), :].astype(BF16)
        vc_ref[j, blk + tq:2 * blk + tq, :] = vn_ref[sds(r, blk), :].astype(BF16)

    def units(r0, qb0):
        us = [(j, u) for j in range(cu) for u in range(qu)]
        q0s = [pl.multiple_of((qb0 + u) * blk, blk) for _, u in us]
        rows = [sds(r0 + j + dil * q0, blk) for (j, _), q0 in zip(us, q0s)]
        lane = lax.broadcasted_iota(jnp.int32, (blk, LANE), 1)
        key_iota = lax.broadcasted_iota(jnp.int32, (1, 3 * blk), 1)
        bias = bias_ref[0]
        qs = [q_ref[rw, :].astype(BF16) for rw in rows]
        kws = [kc_ref[j, pl.ds(q0, 3 * blk), :] for (j, _), q0 in zip(us, q0s)]
        vws = [vc_ref[j, pl.ds(q0, 3 * blk), :] for (j, _), q0 in zip(us, q0s)]
        ss = [lax.dot_general(q, kw, _NT, preferred_element_type=F32) * scale for q, kw in zip(qs, kws)]
        valids = []
        for q0 in q0s:
            kpos = n * tq + q0 - blk + key_iota
            valids.append((kpos >= 0) & (kpos < sub_len))
        ss = [jnp.where(valid, s + bias, NEG_INF) for s, valid in zip(ss, valids)]
        ms = [jnp.max(s, axis=-1, keepdims=True) for s in ss]
        ps = [jnp.exp(s - m) for s, m in zip(ss, ms)]
        ls = [jnp.sum(p, axis=-1, keepdims=True) for p in ps]
        nums = [jnp.dot(p.astype(BF16), vw, preferred_element_type=F32) for p, vw in zip(ps, vws)]
        for rw, num, m, l in zip(rows, nums, ms, ls):
            num_ref[rw, :] = num
            st_ref[rw, :] = jnp.where(lane < LANE // 2, m, l)

    def class_group(rg, carry):
        r0 = rg * cu
        for j in range(cu):
            deinterleave(r0 + j, j)

        def qgroup(qg, carry2):
            units(r0, qg * qu)
            return carry2

        lax.fori_loop(0, nqb // qu, qgroup, 0)
        return carry

    lax.fori_loop(0, dil // cu, class_group, 0)


def _attn_group(proj, bias, col, g, dil, bsz, seq):
    tile = 1024
    tq = tile // dil
    halo = ATT_BLOCK * dil
    sub_len = seq // dil
    cu = ATT_UNROLL // min(tq // ATT_BLOCK, ATT_UNROLL)
    nh = ATT_HEADS_PER_GROUP
    hd = ATT_HEAD_DIM
    qc = col["q_a"] + g * nh
    kc = col["k_a"] + g * nh
    vc = col["v_a"] + g * nh
    tiles_b = seq // tile
    halos_b = seq // halo
    hpt = tile // halo

    own = lambda c: pl.BlockSpec((tile, hd), lambda b, n, h: (b * tiles_b + n, c + h))
    prev = lambda c: pl.BlockSpec(
        (halo, hd), lambda b, n, h: (b * halos_b + jnp.maximum(n * hpt - 1, 0), c + h))
    nxt = lambda c: pl.BlockSpec(
        (halo, hd), lambda b, n, h: (b * halos_b + jnp.minimum((n + 1) * hpt, halos_b - 1), c + h))
    t = bsz * seq
    return pl.pallas_call(
        functools.partial(_attn_body, dil=dil, tq=tq, sub_len=sub_len),
        grid=(bsz, tiles_b, nh),
        in_specs=[own(qc), prev(kc), own(kc), nxt(kc), prev(vc), own(vc), nxt(vc),
                  pl.BlockSpec((1,) + bias.shape[1:], lambda b, n, h: (h, 0, 0))],
        out_specs=[pl.BlockSpec((tile, hd), lambda b, n, h: (b * tiles_b + n, h)),
                   pl.BlockSpec((tile, LANE), lambda b, n, h: (b * tiles_b + n, h))],
        out_shape=[jax.ShapeDtypeStruct((t, nh * hd), F32), jax.ShapeDtypeStruct((t, nh * LANE), F32)],
        scratch_shapes=[pltpu.VMEM((cu, tq + 2 * ATT_BLOCK, hd), BF16),
                        pltpu.VMEM((cu, tq + 2 * ATT_BLOCK, hd), BF16)],
        compiler_params=_cparams(("arbitrary", "arbitrary", "arbitrary"), 32),
        name=f"attn_d{dil}",
    )(proj, proj, proj, proj, proj, proj, proj, bias)


def _merge_body(rec_ref, n0_ref, n1_ref, n2_ref, s0_ref, s1_ref, s2_ref, zgr_ref, zga_ref, x_ref,
                ada_ref, gpost_ref, gpre_ref, wbr_ref, wba_ref, wo_ref, wr_ref, br_ref,
                x1_ref, h2_ref, lg_ref):
    nh = ATT_HEADS_PER_GROUP
    hd = ATT_HEAD_DIM
    half = LANE // 2
    lane = lax.broadcasted_iota(jnp.int32, (rec_ref.shape[0], LANE), 1)
    heads = []
    for h in range(nh):
        cols = slice(h * hd, (h + 1) * hd)
        st = [s[:, cols] for s in (s0_ref, s1_ref, s2_ref)]
        top = jnp.maximum(jnp.maximum(st[0], st[1]), st[2])
        ws = [jnp.exp(s - top) for s in st]
        den = (ws[0] * pltpu.roll(st[0], half, 1) + ws[1] * pltpu.roll(st[1], half, 1)
               + ws[2] * pltpu.roll(st[2], half, 1))
        coef = [w / den for w in ws]
        coef = [jnp.where(lane < half, c, pltpu.roll(c, half, 1)) for c in coef]
        num = coef[0] * n0_ref[:, cols] + coef[1] * n1_ref[:, cols] + coef[2] * n2_ref[:, cols]
        heads.append(num.astype(BF16))
    att = jnp.concatenate(heads, axis=1)
    y_rec = jnp.dot(rec_ref[...], wbr_ref[...], preferred_element_type=F32)
    y_att = jnp.dot(att, wba_ref[...], preferred_element_type=F32)
    merged = jax.nn.sigmoid(zgr_ref[...]) * y_rec + jax.nn.sigmoid(zga_ref[...]) * y_att
    y = jnp.dot(merged.astype(BF16), wo_ref[...], preferred_element_type=F32)
    gt_m = ada_ref[0, 2:3, :]
    sh_f = ada_ref[0, 3:4, :]
    sc_f = ada_ref[0, 4:5, :]
    x1 = x_ref[...] + gt_m * (_rms(y) * gpost_ref[...])
    x1_ref[...] = x1
    h2 = _rms(x1) * gpre_ref[...] * (1.0 + sc_f) + sh_f
    tm = h2.shape[0]
    for s in range(SLAB_ROWS):
        h2_ref[pl.ds(s, tm, stride=SLAB_ROWS), :] = h2[:, s * LANE:(s + 1) * LANE]
    ne = lg_ref.shape[1]
    h_hi = h2.astype(BF16)
    h_lo = (h2 - h_hi.astype(F32)).astype(BF16)
    both = jnp.dot(h_hi, wr_ref[...], preferred_element_type=F32)
    cross = jnp.dot(h_lo, wr_ref[:, 0:ne], preferred_element_type=F32)
    lg_ref[...] = both[:, 0:ne] + both[:, ne:2 * ne] + cross + br_ref[...]


def _merge(rec_o, nums, stats, proj, x2, ada3, g_post, g_pre, wbr, wba, wo, w_router, b_router, col, seq):
    t, d = x2.shape
    tm = 256
    per_b = seq // tm
    d_rec = rec_o.shape[1]
    w_att = nums[0].shape[1]
    ne = w_router.shape[1]
    wr_hi = w_router.astype(BF16)
    wr_lo = (w_router - wr_hi.astype(F32)).astype(BF16)
    w_router = jnp.concatenate([wr_hi, wr_lo], axis=1)
    dl = d // LANE
    row = lambda w: pl.BlockSpec((tm, w), lambda i: (i, 0))
    const = lambda shape: pl.BlockSpec(shape, lambda i: (0,) * len(shape), pipeline_mode=pl.Buffered(1))
    zgr = col["zg_rec"] // dl
    zga = col["zg_att"] // dl
    return pl.pallas_call(
        _merge_body,
        grid=(t // tm,),
        in_specs=[row(d_rec), row(w_att), row(w_att), row(w_att), row(w_att), row(w_att), row(w_att),
                  pl.BlockSpec((tm, d), lambda i: (i, zgr)),
                  pl.BlockSpec((tm, d), lambda i: (i, zga)),
                  row(d),
                  pl.BlockSpec((1, 6, d), lambda i: (i // per_b, 0, 0)),
                  const((1, d)), const((1, d)),
                  const(wbr.shape), const(wba.shape), const(wo.shape), const(w_router.shape),
                  const((1, ne))],
        out_specs=[row(d), pl.BlockSpec((tm * SLAB_ROWS, LANE), lambda i: (i, 0)),
                   pl.BlockSpec((tm, ne), lambda i: (i, 0))],
        out_shape=[jax.ShapeDtypeStruct((t, d), F32), jax.ShapeDtypeStruct((t * SLAB_ROWS, LANE), F32),
                   jax.ShapeDtypeStruct((t, ne), F32)],
        compiler_params=_cparams(("arbitrary",), 56),
        name="merge",
    )(rec_o, nums[0], nums[1], nums[2], stats[0], stats[1], stats[2], proj, proj, x2, ada3,
      g_post.reshape(1, d), g_pre.reshape(1, d), wbr, wba, wo, w_router, b_router.reshape(1, ne))


def _route_body(lg_ref, tri_ref, rt_ref, cnt_ref, carry_ref):
    i = pl.program_id(0)
    tr, ne = lg_ref.shape

    @pl.when(i == 0)
    def _():
        carry_ref[...] = jnp.zeros_like(carry_ref)

    l = lg_ref[...]
    lane = lax.broadcasted_iota(jnp.int32, (tr, ne), 1).astype(F32)
    vals, sels, idxs = [], [], []
    for _ in range(TOP_K):
        m = jnp.max(l, axis=-1, keepdims=True)
        idx = jnp.min(jnp.where(l == m, lane, float(ne)), axis=-1, keepdims=True)
        sel = lane == idx
        vals.append(m)
        idxs.append(idx)
        sels.append(sel)
        l = jnp.where(sel, -jnp.inf, l)
    es = [jnp.exp(v - vals[0]) for v in vals]
    tot = es[0] + es[1] + es[2] + es[3]
    chosen = (sels[0] | sels[1] | sels[2] | sels[3]).astype(F32)
    prefix = jnp.dot(tri_ref[...], chosen.astype(BF16), preferred_element_type=F32) + carry_ref[0:1, :]
    out_lane = lax.broadcasted_iota(jnp.int32, (tr, LANE), 1)
    rt = jnp.zeros((tr, LANE), F32)
    for k in range(TOP_K):
        rank = jnp.sum(jnp.where(sels[k], prefix, 0.0), axis=-1, keepdims=True)
        rt = jnp.where(out_lane == k, idxs[k], rt)
        rt = jnp.where(out_lane == TOP_K + k, es[k] / tot, rt)
        rt = jnp.where(out_lane == 2 * TOP_K + k, rank, rt)
    rt_ref[...] = rt
    new = carry_ref[0:1, :] + jnp.sum(chosen, axis=0, keepdims=True)
    carry_ref[...] = jnp.broadcast_to(new, carry_ref.shape)
    cnt_ref[...] = carry_ref[...]


def _route(logits):
    t, ne = logits.shape
    tr = 512
    tri = jnp.asarray(np.tril(np.ones((tr, tr), np.float32), -1), BF16)
    return pl.pallas_call(
        _route_body,
        grid=(t // tr,),
        in_specs=[pl.BlockSpec((tr, ne), lambda i: (i, 0)),
                  pl.BlockSpec((tr, tr), lambda i: (0, 0))],
        out_specs=[pl.BlockSpec((tr, LANE), lambda i: (i, 0)),
                   pl.BlockSpec((SUBLANE, ne), lambda i: (0, 0))],
        out_shape=[jax.ShapeDtypeStruct((t, LANE), F32), jax.ShapeDtypeStruct((SUBLANE, ne), F32)],
        scratch_shapes=[pltpu.VMEM((SUBLANE, ne), F32)],
        compiler_params=_cparams(("arbitrary",), 32),
        name="route",
    )(logits, tri)


def _chunk_rows(ncols):
    return W_CHUNK_BYTES // (4 * ncols)


def _weight_stream(w_hbm, wbf, stage, wsem, e, slot, c0, c1):
    kc = stage.shape[1]

    def copy(c):
        return pltpu.make_async_copy(w_hbm.at[e, pl.ds(pl.multiple_of(c * kc, kc), kc), :],
                                     stage.at[c % 2], wsem.at[c % 2])

    def prime():
        def body(c, carry):
            copy(c).start()
            return carry
        lax.fori_loop(c0, jnp.minimum(c0 + 2, c1), body, 0)

    def finish():
        def body(c, carry):
            copy(c).wait()
            buf = c % 2

            def cast(i, carry2):
                r = pl.multiple_of(i * CAST_ROWS, CAST_ROWS)
                wbf[slot, pl.ds(pl.multiple_of(c * kc, kc) + r, CAST_ROWS), :] = (
                    stage[buf, pl.ds(r, CAST_ROWS), :].astype(BF16))
                return carry2
            lax.fori_loop(0, kc // CAST_ROWS, cast, 0)

            @pl.when(c + 2 < c1)
            def _():
                copy(c + 2).start()
            return carry
        lax.fori_loop(c0, c1, body, 0)

    return prime, finish


def _moe_up_body(be_ref, nu_ref, ws_ref, wn_ref, wc0_ref, wc1_ref, nv_ref, idx0_ref, idxn_ref, h2s_ref, w_hbm,
                 bias_ref, o_ref, xbuf, wbf, stage, sem, wsem):
    b = pl.program_id(0)
    nu = nu_ref[0]
    f = o_ref.shape[1]
    slab = SLAB_ROWS
    nch = wbf.shape[1] // stage.shape[1]

    def row_start(idx_ref, slot, r):
        tok = idx_ref[0, 0, r]
        pltpu.make_async_copy(h2s_ref.at[pl.ds(pl.multiple_of(tok * slab, slab), slab), :],
                              xbuf.at[slot, pl.ds(pl.multiple_of(r * slab, slab), slab), :],
                              sem.at[slot]).start()

    def wait_rows(slot):
        pltpu.make_async_copy(h2s_ref.at[pl.ds(0, BM * slab), :], xbuf.at[slot], sem.at[slot]).wait()

    def issue(idx_ref, slot):
        def body(r, c):
            row_start(idx_ref, slot, r)
            return c
        lax.fori_loop(0, BM, body, 0, unroll=DMA_UNROLL)

    @pl.when(b == 0)
    def _():
        issue(idx0_ref, 0)
        prime0, finish0 = _weight_stream(w_hbm, wbf, stage, wsem, be_ref[0], ws_ref[0], 0, nch)
        prime0()
        finish0()

    @pl.when(b + 1 < nu)
    def _():
        issue(idxn_ref, (b + 1) % 2)

    @pl.when(b < nu)
    def _():
        wslot = ws_ref[b]
        prime, finish = _weight_stream(w_hbm, wbf, stage, wsem, wn_ref[b], 1 - wslot, wc0_ref[b], wc1_ref[b])
        prime()
        slot = b % 2
        wait_rows(slot)

        def compute(m):
            x = jnp.concatenate([xbuf[slot, pl.ds(s, m, stride=slab), :].astype(BF16) for s in range(slab)],
                                axis=1)
            half = f // 2
            for c0 in (0, half):
                gate = (jnp.dot(x, wbf[wslot, :, c0:c0 + half], preferred_element_type=F32)
                        + bias_ref[0, :, c0:c0 + half])
                up = (jnp.dot(x, wbf[wslot, :, f + c0:f + c0 + half], preferred_element_type=F32)
                      + bias_ref[0, :, f + c0:f + c0 + half])
                gate = jnp.minimum(gate, SWIGLU_LIMIT)
                up = jnp.clip(up, -SWIGLU_LIMIT, SWIGLU_LIMIT)
                o_ref[0:m, c0:c0 + half] = (gate * jax.nn.sigmoid(SWIGLU_ALPHA * gate) * (up + 1.0)).astype(BF16)
            if m < BM:
                o_ref[m:BM, :] = jnp.zeros((BM - m, f), BF16)

        @pl.when(nv_ref[b] > BM // 2)
        def _():
            compute(BM)

        @pl.when(nv_ref[b] <= BM // 2)
        def _():
            compute(BM // 2)

        finish()

    @pl.when(b >= nu)
    def _():
        o_ref[...] = jnp.zeros_like(o_ref)


def _moe_up(h2s, src_tok, wgu, bgu, sched):
    ne, d, f2 = wgu.shape
    f = f2 // 2
    p = src_tok.shape[0]
    nb = p // BM
    idx3 = src_tok.reshape(nb, 1, BM)
    smem_blk = lambda imap: pl.BlockSpec((1, 1, BM), imap, memory_space=pltpu.SMEM)
    grid_spec = pltpu.PrefetchScalarGridSpec(
        num_scalar_prefetch=7,
        grid=(nb,),
        in_specs=[smem_blk(lambda b, be, *_: (0, 0, 0)),
                  smem_blk(lambda b, be, *_: (jnp.minimum(b + 1, nb - 1), 0, 0)),
                  pl.BlockSpec(memory_space=pl.ANY),
                  pl.BlockSpec(memory_space=pl.ANY),
                  pl.BlockSpec((1, 1, f2), lambda b, be, *_: (be[b], 0, 0))],
        out_specs=pl.BlockSpec((BM, f), lambda b, be, *_: (b, 0)),
        scratch_shapes=[pltpu.VMEM((2, BM * SLAB_ROWS, LANE), F32),
                        pltpu.VMEM((2, d, f2), BF16),
                        pltpu.VMEM((2, _chunk_rows(f2), f2), F32),
                        pltpu.SemaphoreType.DMA((2,)), pltpu.SemaphoreType.DMA((2,))],
    )
    return pl.pallas_call(
        _moe_up_body,
        grid_spec=grid_spec,
        out_shape=jax.ShapeDtypeStruct((p, f), BF16),
        compiler_params=_cparams(("arbitrary",), 58),
        name="moe_up",
    )(sched["blk_e"], sched["n_used"], sched["wslot"], sched["wnext"], sched["wc0"], sched["wc1"],
      sched["nvalid"], idx3, idx3, h2s, wgu, bgu.reshape(ne, 1, f2))


def _moe_down_body(be_ref, nu_ref, ws_ref, wn_ref, wc0_ref, wc1_ref, nv_ref, dst_ref, a_ref, w_hbm, bias_ref,
                   ysc_ref, ybuf, wbf, stage, sem, wsem):
    b = pl.program_id(0)
    nb = pl.num_programs(0)
    nu = nu_ref[0]
    slab = SLAB_ROWS
    nch = wbf.shape[1] // stage.shape[1]

    @pl.when(b == 0)
    def _():
        prime0, finish0 = _weight_stream(w_hbm, wbf, stage, wsem, be_ref[0], ws_ref[0], 0, nch)
        prime0()
        finish0()

    def row_copy(slot, r, d):
        return pltpu.make_async_copy(ybuf.at[slot, pl.ds(pl.multiple_of(r * slab, slab), slab), :],
                                     ysc_ref.at[pl.ds(pl.multiple_of(d * slab, slab), slab), :],
                                     sem.at[slot])

    def drain(step):
        slot = step % 2
        count = nv_ref[step]

        @pl.when(count == BM)
        def _():
            pltpu.make_async_copy(ybuf.at[slot], ysc_ref.at[pl.ds(0, BM * slab), :], sem.at[slot]).wait()

        @pl.when(count < BM)
        def _():
            def body(r, c):
                row_copy(slot, 0, 0).wait()
                return c
            lax.fori_loop(0, count, body, 0)

    @pl.when((b >= 2) & (b < nu))
    def _():
        drain(b - 2)

    @pl.when(b < nu)
    def _():
        slot = b % 2
        wslot = ws_ref[b]
        prime, finish = _weight_stream(w_hbm, wbf, stage, wsem, wn_ref[b], 1 - wslot, wc0_ref[b], wc1_ref[b])
        prime()
        def compute(m):
            y = jnp.dot(a_ref[0:m, :], wbf[wslot], preferred_element_type=F32) + bias_ref[0]
            for s in range(slab):
                ybuf[slot, pl.ds(s, m, stride=slab), :] = y[:, s * LANE:(s + 1) * LANE]

        @pl.when(nv_ref[b] > BM // 2)
        def _():
            compute(BM)

        @pl.when(nv_ref[b] <= BM // 2)
        def _():
            compute(BM // 2)

        finish()

        def body(r, c):
            row_copy(slot, r, dst_ref[0, 0, r]).start()
            return c

        @pl.when(nv_ref[b] == BM)
        def _():
            lax.fori_loop(0, BM, body, 0, unroll=DMA_UNROLL)

        @pl.when(nv_ref[b] < BM)
        def _():
            lax.fori_loop(0, nv_ref[b], body, 0)

    @pl.when(b == nb - 1)
    def _():
        @pl.when(nu >= 2)
        def _():
            drain(nu - 2)
        drain(nu - 1)


def _moe_down(act, dst_slot, n_out_rows, wd, bd, sched):
    p, f = act.shape
    ne, _, d = wd.shape
    nb = p // BM
    grid_spec = pltpu.PrefetchScalarGridSpec(
        num_scalar_prefetch=7,
        grid=(nb,),
        in_specs=[pl.BlockSpec((1, 1, BM), lambda b, be, *_: (b, 0, 0), memory_space=pltpu.SMEM),
                  pl.BlockSpec((BM, f), lambda b, be, *_: (b, 0)),
                  pl.BlockSpec(memory_space=pl.ANY),
                  pl.BlockSpec((1, 1, d), lambda b, be, *_: (be[b], 0, 0))],
        out_specs=pl.BlockSpec(memory_space=pl.ANY),
        scratch_shapes=[pltpu.VMEM((2, BM * SLAB_ROWS, LANE), F32),
                        pltpu.VMEM((2, f, d), BF16),
                        pltpu.VMEM((2, _chunk_rows(d), d), F32),
                        pltpu.SemaphoreType.DMA((2,)), pltpu.SemaphoreType.DMA((2,))],
    )
    return pl.pallas_call(
        _moe_down_body,
        grid_spec=grid_spec,
        out_shape=jax.ShapeDtypeStruct((n_out_rows * SLAB_ROWS, LANE), F32),
        compiler_params=pltpu.CompilerParams(dimension_semantics=("arbitrary",),
                                             vmem_limit_bytes=48 * 1024 * 1024, has_side_effects=True),
        name="moe_down",
    )(sched["blk_e"], sched["n_used"], sched["wslot"], sched["wnext"], sched["wc0"], sched["wc1"],
      sched["nvalid"], dst_slot.reshape(nb, 1, BM), act, wd, bd.reshape(ne, 1, d))


def _final_body(ysc_ref, rt_ref, x1_ref, ada_ref, g_ref, o_ref):
    tm = x1_ref.shape[0]
    rt = rt_ref[...]
    stride = TOP_K * SLAB_ROWS
    pieces = []
    for s in range(SLAB_ROWS):
        acc = rt[:, TOP_K:TOP_K + 1] * ysc_ref[pl.ds(s, tm, stride=stride), :]
        for k in range(1, TOP_K):
            acc = acc + rt[:, TOP_K + k:TOP_K + k + 1] * ysc_ref[pl.ds(k * SLAB_ROWS + s, tm, stride=stride), :]
        pieces.append(acc)
    y = jnp.concatenate(pieces, axis=1)
    gt_f = ada_ref[0, 5:6, :]
    o_ref[...] = x1_ref[...] + gt_f * (_rms(y) * g_ref[...])


def _final(ysc, rt, x1, ada3, g_post, seq):
    t, d = x1.shape
    tm = 256
    per_b = seq // tm
    return pl.pallas_call(
        _final_body,
        grid=(t // tm,),
        in_specs=[pl.BlockSpec((tm * TOP_K * SLAB_ROWS, LANE), lambda i: (i, 0)),
                  pl.BlockSpec((tm, LANE), lambda i: (i, 0)),
                  pl.BlockSpec((tm, d), lambda i: (i, 0)),
                  pl.BlockSpec((1, 6, d), lambda i: (i // per_b, 0, 0)),
                  pl.BlockSpec((1, d), lambda i: (0, 0))],
        out_specs=pl.BlockSpec((tm, d), lambda i: (i, 0)),
        out_shape=jax.ShapeDtypeStruct((t, d), F32),
        compiler_params=_cparams(("arbitrary",), 48),
        name="final",
    )(ysc, rt, x1, ada3, g_post.reshape(1, d))


def _mixer_ffn_layer(x2, ada3, bsz, seq, g_mix_pre, g_mix_post, g_ffn_pre, g_ffn_post, w_in, lb, g_rec_out,
                     rel_bias, w_branch_rec, w_branch_att, w_o, w_router, b_router, w_gate_up, b_gate_up,
                     w_down, b_down):
    t, d = x2.shape
    d_rec = w_branch_rec.shape[0]
    w_att = w_branch_att.shape[0]
    d_att = 3 * w_att
    widths = dict(q_r=d_rec, i_r=d_rec, zf_f=d_rec, zf_b=d_rec, z_o=d_rec, q_a=d_att, k_a=d_att, v_a=d_att,
                  zg_rec=d, zg_att=d)
    my_order = ("zg_rec", "zg_att", "q_r", "i_r", "zf_f", "zf_b", "z_o", "q_a", "k_a", "v_a")
    col, acc = {}, 0
    for name in my_order:
        col[name] = acc // LANE
        acc += widths[name]
    rot = acc - 2 * d

    proj = _inproj(x2, g_mix_pre, ada3, w_in, seq, rot)

    oi, qtf, qtb, utf, utb, df, db = _hgrn_a(proj, lb, col, t, d_rec)
    rec_o = _hgrn_c(oi, qtf, qtb, utf, utb, df, db, proj, g_rec_out, col, bsz, seq, d_rec)

    nums, stats = [], []
    for g, (window, dil) in enumerate(DIL_GROUPS):
        hs = slice(g * ATT_HEADS_PER_GROUP, (g + 1) * ATT_HEADS_PER_GROUP)
        bias = _band_bias(rel_bias[:, hs], window, dil)
        num, st = _attn_group(proj, bias, col, g, dil, bsz, seq)
        nums.append(num)
        stats.append(st)

    x1, h2s, logits = _merge(rec_o, nums, stats, proj, x2, ada3, g_mix_post, g_ffn_pre,
                            w_branch_rec.astype(BF16), w_branch_att.astype(BF16), w_o.astype(BF16),
                            w_router, b_router, col, seq)

    rt, cnt = _route(logits)
    ne = logits.shape[1]
    counts = cnt[0].astype(jnp.int32)
    top_idx = rt[:, 0:TOP_K].astype(jnp.int32)
    rank = rt[:, 2 * TOP_K:3 * TOP_K].astype(jnp.int32)
    padded = (counts + BM - 1) // BM * BM
    pends = jnp.cumsum(padded)
    pstarts = pends - padded
    experts = jnp.arange(ne, dtype=jnp.int32)
    pstart_sel = jnp.sum(jnp.where(top_idx[..., None] == experts, pstarts, 0), axis=-1)
    dest = (pstart_sel + rank).reshape(-1)
    p_rows = t * TOP_K + ne * BM
    nb = p_rows // BM
    blk_start = jnp.arange(nb, dtype=jnp.int32) * BM
    blk_e = jnp.minimum(jnp.sum((pends[None, :] <= blk_start[:, None]).astype(jnp.int32), axis=1), ne - 1)
    n_used = (pends[-1:] // BM).astype(jnp.int32)

    n_assign = t * TOP_K
    slot_assign = jnp.full((p_rows,), -1, jnp.int32).at[dest].set(jnp.arange(n_assign, dtype=jnp.int32))
    valid_end = pstarts + counts
    blk_end = jnp.sum(jnp.where(blk_e[:, None] == experts, valid_end, 0), axis=-1)
    nvalid = jnp.clip(blk_end - blk_start, 0, BM).astype(jnp.int32)
    nvalid = jnp.where(jnp.arange(nb) < n_used[0], nvalid, 0)

    lookup = lambda table: jnp.sum(jnp.where(blk_e[:, None] == experts, table, 0), axis=-1)
    nonempty = padded > 0
    order = jnp.cumsum(nonempty.astype(jnp.int32)) - 1
    later = lax.cummin(jnp.where(nonempty, experts, ne)[::-1])[::-1]
    next_e = jnp.concatenate([later[1:], jnp.full((1,), ne, jnp.int32)])
    blk_next = lookup(next_e)
    has_next = (blk_next < ne) & (jnp.arange(nb) < n_used[0])
    k_in_run = jnp.arange(nb, dtype=jnp.int32) - lookup(pstarts // BM)
    n_in_run = jnp.maximum(lookup(padded // BM), 1)
    common = dict(blk_e=blk_e, n_used=n_used, nvalid=nvalid, wslot=lookup(order) % 2,
                  wnext=jnp.where(has_next, blk_next, blk_e))

    def schedule(w):
        nch = w.shape[1] // _chunk_rows(w.shape[2])
        s = dict(common, wc0=jnp.where(has_next, k_in_run * nch // n_in_run, 0),
                 wc1=jnp.where(has_next, (k_in_run + 1) * nch // n_in_run, 0))
        return {k: v.astype(jnp.int32) for k, v in s.items()}

    src_tok = jnp.maximum(slot_assign, 0) // TOP_K
    act = _moe_up(h2s, src_tok, w_gate_up, b_gate_up, schedule(w_gate_up))
    ysc = _moe_down(act, slot_assign, n_assign, w_down, b_down, schedule(w_down))
    return _final(ysc, rt, x1, ada3, g_ffn_post, seq)


def kernel(x, c, w_ada, b_ada, g_mix_pre, g_mix_post, g_ffn_pre, g_ffn_post, w_in, g_rec_out, w_branch_rec,
           w_branch_att, w_o, w_router, b_router, w_gate_up, b_gate_up, w_down, b_down, rec_lb_table, rel_bias):
    bsz, seq, d = x.shape
    depth = w_in.shape[0]
    lb_all = jnp.cumsum(jax.nn.softmax(rec_lb_table.astype(F32), axis=1), axis=1)
    x2 = x.reshape(bsz * seq, d)
    for layer in range(depth):
        ada3 = _ada(c, w_ada[layer], b_ada[layer]).reshape(bsz, 6, d)
        x2 = _mixer_ffn_layer(x2, ada3, bsz, seq, g_mix_pre[layer], g_mix_post[layer], g_ffn_pre[layer],
                              g_ffn_post[layer], w_in[layer], lb_all[:, layer], g_rec_out[layer], rel_bias,
                              w_branch_rec[layer], w_branch_att[layer], w_o[layer], w_router[layer],
                              b_router[layer], w_gate_up[layer], b_gate_up[layer], w_down[layer],
                              b_down[layer])
    return x2.reshape(bsz, seq, d)
```

```python
import functools
import math

import numpy as np
import jax
import jax.numpy as jnp
from jax import lax
from jax.experimental import pallas as pl
from jax.experimental.pallas import tpu as pltpu

F32 = jnp.float32
BF16 = jnp.bfloat16

LANE = 128
SUBLANE = 8
SLAB_ROWS = 16

REC_HEAD_DIM = 128
REC_CHUNK = 64
ATT_HEAD_DIM = 128
ATT_HEADS_PER_GROUP = 4
ATT_BLOCK = 64
DIL_GROUPS = ((128, 1), (512, 4), (2048, 16))
NUM_BUCKETS = 32
MAX_DISTANCE = 1024
N_EXPERTS = 32
TOP_K = 4
SWIGLU_LIMIT = 7.0
SWIGLU_ALPHA = 1.702
RMS_EPS = 1e-6
NEG_INF = -1e30

N_LEVELS = 6
W_CHUNK_BYTES = 4 * 1024 * 1024
CAST_ROWS = 32
HGRN_UNROLL = 4
DMA_UNROLL = 8
ATT_UNROLL = 8
BM = 256

_NT = (((1,), (1,)), ((), ()))
_TN = (((0,), (0,)), ((), ()))


def _cparams(sem, vmem_mb):
    return pltpu.CompilerParams(dimension_semantics=sem, vmem_limit_bytes=vmem_mb * 1024 * 1024)


def _rms(x):
    return x * lax.rsqrt(jnp.mean(x * x, axis=-1, keepdims=True) + RMS_EPS)


def _ada_body(c_ref, w_ref, b_ref, o_ref):
    c = c_ref[...]
    cond = (c * jax.nn.sigmoid(c)).astype(BF16)
    o_ref[...] = jnp.dot(cond, w_ref[...].astype(BF16), preferred_element_type=F32) + b_ref[...]


def _ada(c, w, b):
    bsz, d = c.shape
    n = w.shape[1]
    tn = 1024
    cp = jnp.zeros((SUBLANE, d), F32).at[:bsz].set(c)
    out = pl.pallas_call(
        _ada_body,
        grid=(n // tn,),
        in_specs=[pl.BlockSpec((SUBLANE, d), lambda j: (0, 0)),
                  pl.BlockSpec((d, tn), lambda j: (0, j)),
                  pl.BlockSpec((1, tn), lambda j: (0, j))],
        out_specs=pl.BlockSpec((SUBLANE, tn), lambda j: (0, j)),
        out_shape=jax.ShapeDtypeStruct((SUBLANE, n), F32),
        compiler_params=_cparams(("arbitrary",), 40),
        name="ada",
    )(cp, w, b.reshape(1, n))
    return out[:bsz]


def _inproj_body(x_ref, g_ref, ada_ref, w_ref, o_ref, h_ref):
    @pl.when(pl.program_id(1) == 0)
    def _():
        half = x_ref.shape[0] // 2
        sh = ada_ref[0, 0:1, :]
        sc = ada_ref[0, 1:2, :]
        for r0 in (0, half):
            y = _rms(x_ref[r0:r0 + half, :]) * g_ref[...]
            h_ref[r0:r0 + half, :] = (y * (1.0 + sc) + sh).astype(BF16)

    o_ref[...] = jnp.dot(h_ref[...], w_ref[...].astype(BF16), preferred_element_type=F32)


def _inproj(x2, g, ada3, w_in, seq, rot):
    t, d = x2.shape
    n = w_in.shape[1]
    tm, tn = 2048, 512
    per_b = seq // tm
    nj = n // tn
    assert rot % tn == 0 and n % tn == 0 and seq % tm == 0
    return pl.pallas_call(
        _inproj_body,
        grid=(t // tm, nj),
        in_specs=[pl.BlockSpec((tm, d), lambda i, j: (i, 0), pipeline_mode=pl.Buffered(1)),
                  pl.BlockSpec((1, d), lambda i, j: (0, 0)),
                  pl.BlockSpec((1, 6, d), lambda i, j: (i // per_b, 0, 0)),
                  pl.BlockSpec((d, tn), lambda i, j: (0, (j + rot // tn) % nj))],
        out_specs=pl.BlockSpec((tm, tn), lambda i, j: (i, j)),
        out_shape=jax.ShapeDtypeStruct((t, n), F32),
        scratch_shapes=[pltpu.VMEM((tm, d), BF16)],
        compiler_params=_cparams(("arbitrary", "arbitrary"), 56),
        name="inproj",
    )(x2, g.reshape(1, d), ada3, w_in)


def _hgrn_consts():
    c = REC_CHUNK
    r = np.arange(c)[:, None]
    m = np.arange(c)[None, :]
    wf = np.zeros((8 * c, c), np.float32)
    wb = np.zeros((8 * c, c), np.float32)
    mf = np.zeros((N_LEVELS + 1, c, c), np.float32)
    for lvl in range(N_LEVELS):
        s = 32 >> lvl
        m0 = (r // (2 * s)) * (2 * s) + s
        up = r >= m0
        wf[lvl * c:(lvl + 1) * c] = np.where(up, (m >= m0) & (m <= r), (m > r) & (m <= m0 - 1))
        wb[lvl * c:(lvl + 1) * c] = np.where(up, (m >= m0) & (m <= r - 1), (m >= r) & (m <= m0 - 1))
        i = np.arange(c)[:, None]
        j = np.arange(c)[None, :]
        mf[lvl] = (i // (2 * s) == j // (2 * s)) & (i % (2 * s) >= s) & (j % (2 * s) < s)
    mf[N_LEVELS] = np.eye(c)
    wf[6 * c:7 * c] = m <= r
    wf[7 * c:8 * c] = m > r
    wb[6 * c:7 * c] = m >= r
    wb[7 * c:8 * c] = m < r
    mfb = mf + np.transpose(mf, (0, 2, 1))
    mfb[N_LEVELS] = np.eye(c)
    up = np.zeros((N_LEVELS, c, LANE), np.float32)
    for lvl in range(N_LEVELS):
        s = 32 >> lvl
        up[lvl] = ((np.arange(c) % (2 * s)) >= s)[:, None]
    wf3 = np.concatenate([wf, wf, wf], axis=1)
    wb3 = np.concatenate([wb, wb, wb], axis=1)
    return (jnp.asarray(wf3, BF16), jnp.asarray(wb3, BF16), jnp.asarray(mfb, F32), jnp.asarray(up, F32),
            jnp.asarray(1.0 - up, F32))


def _split3(g):
    hi = g.astype(BF16)
    r1 = g - hi.astype(F32)
    mid = r1.astype(BF16)
    lo = (r1 - mid.astype(F32)).astype(BF16)
    return jnp.concatenate([hi, mid, lo], axis=0)


def _hgrn_a_body(q_ref, i_ref, zf_ref, zb_ref, lb_ref, wf_ref, wb_ref, mf_ref, up_ref, lo_ref,
                 oi_ref, qtf_ref, qtb_ref, utf_ref, utb_ref, df_ref, db_ref, *, cpb):
    c = REC_CHUNK
    dirs = ((zf_ref, wf_ref, None, qtf_ref, utf_ref, df_ref, 0, c - 1),
            (zb_ref, wb_ref, None, qtb_ref, utb_ref, db_ref, 1, 0))

    def chunk_group(cg, carry):
        cis = [cg * HGRN_UNROLL + u for u in range(HGRN_UNROLL)]
        rows = [pl.ds(pl.multiple_of(ci * c, c), c) for ci in cis]
        zqs = [q_ref[rw, :] for rw in rows]
        qs = [zq * jax.nn.sigmoid(zq) for zq in zqs]
        vbs = [i_ref[rw, :].astype(BF16) for rw in rows]
        units = [(u, d) for u in range(HGRN_UNROLL) for d in range(2)]
        ks, es = {}, {}
        for u, d in units:
            z_ref, w_ref = dirs[d][0], dirs[d][1]
            lb = lb_ref[d:d + 1, :]
            f = lb + (1.0 - lb) * jax.nn.sigmoid(z_ref[rows[u], :])
            ks[u, d] = 1.0 - f
            es[u, d] = jnp.exp(jnp.dot(w_ref[...], _split3(jnp.log(f)), preferred_element_type=F32))
        acc = [jnp.zeros((c, c), F32) for _ in range(HGRN_UNROLL)]
        for lvl in range(N_LEVELS + 1):
            for u in range(HGRN_UNROLL):
                if lvl < N_LEVELS:
                    ef = es[u, 0][lvl * c:(lvl + 1) * c]
                    eb = es[u, 1][lvl * c:(lvl + 1) * c]
                    up, lo = up_ref[lvl], lo_ref[lvl]
                    qa = jnp.concatenate([(qs[u] * (ef * up)).astype(BF16), (qs[u] * (eb * lo)).astype(BF16)],
                                         axis=1)
                    ka = jnp.concatenate([(ks[u, 0] * (ef * lo)).astype(BF16),
                                          (ks[u, 1] * (eb * up)).astype(BF16)], axis=1)
                else:
                    qa, ka = qs[u].astype(BF16), (ks[u, 0] + ks[u, 1]).astype(BF16)
                p = lax.dot_general(qa, ka, _NT, preferred_element_type=F32)
                acc[u] = acc[u] + p * mf_ref[lvl]
        for u, d in units:
            _, _, _, qt_ref, ut_ref, d_ref, _, drow = dirs[d]
            e = es[u, d]
            qt_ref[rows[u], :] = (qs[u] * e[6 * c:7 * c]).astype(BF16)
            kt = (ks[u, d] * e[7 * c:8 * c]).astype(BF16)
            ut_ref[cis[u]] = lax.dot_general(vbs[u], kt, _TN, preferred_element_type=F32)
            d_ref[pl.ds(cis[u], 1), :] = e[6 * c + drow:6 * c + drow + 1]
        for u in range(HGRN_UNROLL):
            oi_ref[rows[u], :] = jnp.dot(acc[u].astype(BF16), vbs[u], preferred_element_type=F32)
        return carry

    lax.fori_loop(0, cpb // HGRN_UNROLL, chunk_group, 0)


def _hgrn_a(proj, lb, col, t, d_rec):
    heads = d_rec // REC_HEAD_DIM
    tq = 1024
    cpb = tq // REC_CHUNK
    nchunks = t // REC_CHUNK
    wf, wb, mf, up, lo = _hgrn_consts()
    hd = REC_HEAD_DIM

    def colspec(off):
        return pl.BlockSpec((tq, hd), lambda i, h: (i, off + h))

    full2 = lambda i, h: (0, 0)
    full3 = lambda i, h: (0, 0, 0)
    row_spec = pl.BlockSpec((tq, hd), lambda i, h: (i, h))
    u_spec = pl.BlockSpec((cpb, hd, hd), lambda i, h: (i, 0, h))
    d_spec = pl.BlockSpec((cpb, hd), lambda i, h: (i, h))
    return pl.pallas_call(
        functools.partial(_hgrn_a_body, cpb=cpb),
        grid=(t // tq, heads),
        in_specs=[colspec(col["q_r"]), colspec(col["i_r"]), colspec(col["zf_f"]), colspec(col["zf_b"]),
                  pl.BlockSpec((2, hd), lambda i, h: (0, h)),
                  pl.BlockSpec(wf.shape, full2), pl.BlockSpec(wb.shape, full2),
                  pl.BlockSpec(mf.shape, full3), pl.BlockSpec(up.shape, full3), pl.BlockSpec(lo.shape, full3)],
        out_specs=[row_spec, row_spec, row_spec, u_spec, u_spec, d_spec, d_spec],
        out_shape=[jax.ShapeDtypeStruct((t, d_rec), F32),
                   jax.ShapeDtypeStruct((t, d_rec), BF16),
                   jax.ShapeDtypeStruct((t, d_rec), BF16),
                   jax.ShapeDtypeStruct((nchunks, hd, d_rec), F32),
                   jax.ShapeDtypeStruct((nchunks, hd, d_rec), F32),
                   jax.ShapeDtypeStruct((nchunks, d_rec), F32),
                   jax.ShapeDtypeStruct((nchunks, d_rec), F32)],
        compiler_params=_cparams(("arbitrary", "arbitrary"), 32),
        name="hgrn_a",
    )(proj, proj, proj, proj, lb, wf, wb, mf, up, lo)


def _hgrn_c_body(oi_ref, qtf_ref, qtb_ref, utf_ref, utb_ref, df_ref, db_ref, z_ref, g_ref,
                 out_ref, acc_ref, accb_ref, *, nchunks):
    c = REC_CHUNK
    hd = REC_HEAD_DIM

    unroll = 4

    def step(i, carry):
        st_f, st_b = carry
        pending = []
        for u in range(unroll):
            nf = i * unroll + u
            nb = nchunks - 1 - nf
            rows_f = pl.ds(pl.multiple_of(nf * c, c), c)
            rows_b = pl.ds(pl.multiple_of(nb * c, c), c)
            of = lax.dot_general(qtf_ref[rows_f, :], st_f.astype(BF16), _NT, preferred_element_type=F32)
            ob = lax.dot_general(qtb_ref[rows_b, :], st_b.astype(BF16), _NT, preferred_element_type=F32)
            pending.append((rows_f, rows_b, oi_ref[rows_f, :] + of, ob))
            st_f = df_ref[pl.ds(nf, 1), :] * st_f + utf_ref[nf]
            st_b = db_ref[pl.ds(nb, 1), :] * st_b + utb_ref[nb]
        for rows_f, rows_b, vf, vb in pending:
            acc_ref[rows_f, :] = vf
            accb_ref[rows_b, :] = vb
        return st_f, st_b

    zero = jnp.zeros((hd, hd), F32)
    lax.fori_loop(0, nchunks // unroll, step, (zero, zero))

    o = _rms(acc_ref[...] + accb_ref[...])
    out_ref[...] = (o * g_ref[...] * jax.nn.sigmoid(z_ref[...])).astype(BF16)


def _hgrn_c(oi, qtf, qtb, utf, utb, df, db, proj, g_out, col, bsz, seq, d_rec):
    heads = d_rec // REC_HEAD_DIM
    hd = REC_HEAD_DIM
    nchunks = seq // REC_CHUNK
    row_spec = pl.BlockSpec((seq, hd), lambda b, h: (b, h))
    u_spec = pl.BlockSpec((nchunks, hd, hd), lambda b, h: (b, 0, h))
    d_spec = pl.BlockSpec((nchunks, hd), lambda b, h: (b, h))
    zo = col["z_o"]
    return pl.pallas_call(
        functools.partial(_hgrn_c_body, nchunks=nchunks),
        grid=(bsz, heads),
        in_specs=[row_spec, row_spec, row_spec, u_spec, u_spec, d_spec, d_spec,
                  pl.BlockSpec((seq, hd), lambda b, h: (b, zo + h)),
                  pl.BlockSpec((1, hd), lambda b, h: (0, h))],
        out_specs=row_spec,
        out_shape=jax.ShapeDtypeStruct((bsz * seq, d_rec), BF16),
        scratch_shapes=[pltpu.VMEM((seq, hd), F32), pltpu.VMEM((seq, hd), F32)],
        compiler_params=_cparams(("arbitrary", "arbitrary"), 48),
        name="hgrn_c",
    )(oi, qtf, qtb, utf, utb, df, db, proj, g_out.reshape(1, d_rec))


def _t5_bucket(rel):
    half_buckets = NUM_BUCKETS // 2
    ret = np.where(rel > 0, half_buckets, 0)
    n = np.abs(rel)
    max_exact = half_buckets // 2
    nf = np.maximum(n, 1).astype(np.float32)
    large = max_exact + (np.log(nf / np.float32(max_exact)) / np.float32(math.log(MAX_DISTANCE / max_exact))
                         * np.float32(half_buckets - max_exact)).astype(np.int32)
    large = np.minimum(large, half_buckets - 1)
    return ret + np.where(n < max_exact, n, large)


def _band_bias(rel_bias_g, window, dil):
    half = window // (2 * dil)
    q_off = np.arange(ATT_BLOCK)[:, None]
    rel = np.arange(3 * ATT_BLOCK)[None, :] - ATT_BLOCK - q_off
    onehot = (_t5_bucket(rel * dil)[..., None] == np.arange(NUM_BUCKETS)).astype(np.float32)
    bias = jnp.einsum("qkb,bh->hqk", jnp.asarray(onehot), rel_bias_g.astype(F32),
                      precision=lax.Precision.HIGHEST)
    return jnp.where(jnp.asarray(np.abs(rel) <= half)[None], bias, NEG_INF)


def _attn_body(q_ref, kp_ref, k_ref, kn_ref, vp_ref, v_ref, vn_ref, bias_ref,
               num_ref, st_ref, kc_ref, vc_ref, *, dil, tq, sub_len):
    blk = ATT_BLOCK
    nqb = tq // blk
    n = pl.program_id(1)
    scale = ATT_HEAD_DIM ** -0.5

    def sds(start, size):
        if dil == 1:
            return pl.ds(start, size)
        return pl.ds(start, size, stride=dil)

    cu = kc_ref.shape[0]
    qu = ATT_UNROLL // cu

    def deinterleave(r, j):
        kc_ref[j, 0:blk, :] = kp_ref[sds(r, blk), :].astype(BF16)
        kc_ref[j, blk:blk + tq, :] = k_ref[sds(r, tq), :].astype(BF16)
        kc_ref[j, blk + tq:2 * blk + tq, :] = kn_ref[sds(r, blk), :].astype(BF16)
        vc_ref[j, 0:blk, :] = vp_ref[sds(r, blk), :].astype(BF16)
        vc_ref[j, blk:blk + tq, :] = v_ref[sds(r, tq), :].astype(BF16)
        vc_ref[j, blk + tq:2 * blk + tq, :] = vn_ref[sds(r, blk), :].astype(BF16)

    def units(r0, qb0):
        us = [(j, u) for j in range(cu) for u in range(qu)]
        q0s = [pl.multiple_of((qb0 + u) * blk, blk) for _, u in us]
        rows = [sds(r0 + j + dil * q0, blk) for (j, _), q0 in zip(us, q0s)]
        lane = lax.broadcasted_iota(jnp.int32, (blk, LANE), 1)
        key_iota = lax.broadcasted_iota(jnp.int32, (1, 3 * blk), 1)
        bias = bias_ref[0]
        qs = [q_ref[rw, :].astype(BF16) for rw in rows]
        kws = [kc_ref[j, pl.ds(q0, 3 * blk), :] for (j, _), q0 in zip(us, q0s)]
        vws = [vc_ref[j, pl.ds(q0, 3 * blk), :] for (j, _), q0 in zip(us, q0s)]
        ss = [lax.dot_general(q, kw, _NT, preferred_element_type=F32) * scale for q, kw in zip(qs, kws)]
        valids = []
        for q0 in q0s:
            kpos = n * tq + q0 - blk + key_iota
            valids.append((kpos >= 0) & (kpos < sub_len))
        ss = [jnp.where(valid, s + bias, NEG_INF) for s, valid in zip(ss, valids)]
        ms = [jnp.max(s, axis=-1, keepdims=True) for s in ss]
        ps = [jnp.exp(s - m) for s, m in zip(ss, ms)]
        ls = [jnp.sum(p, axis=-1, keepdims=True) for p in ps]
        nums = [jnp.dot(p.astype(BF16), vw, preferred_element_type=F32) for p, vw in zip(ps, vws)]
        for rw, num, m, l in zip(rows, nums, ms, ls):
            num_ref[rw, :] = num
            st_ref[rw, :] = jnp.where(lane < LANE // 2, m, l)

    def class_group(rg, carry):
        r0 = rg * cu
        for j in range(cu):
            deinterleave(r0 + j, j)

        def qgroup(qg, carry2):
            units(r0, qg * qu)
            return carry2

        lax.fori_loop(0, nqb // qu, qgroup, 0)
        return carry

    lax.fori_loop(0, dil // cu, class_group, 0)


def _attn_group(proj, bias, col, g, dil, bsz, seq):
    tile = 1024
    tq = tile // dil
    halo = ATT_BLOCK * dil
    sub_len = seq // dil
    cu = ATT_UNROLL // min(tq // ATT_BLOCK, ATT_UNROLL)
    nh = ATT_HEADS_PER_GROUP
    hd = ATT_HEAD_DIM
    qc = col["q_a"] + g * nh
    kc = col["k_a"] + g * nh
    vc = col["v_a"] + g * nh
    tiles_b = seq // tile
    halos_b = seq // halo
    hpt = tile // halo

    own = lambda c: pl.BlockSpec((tile, hd), lambda b, n, h: (b * tiles_b + n, c + h))
    prev = lambda c: pl.BlockSpec(
        (halo, hd), lambda b, n, h: (b * halos_b + jnp.maximum(n * hpt - 1, 0), c + h))
    nxt = lambda c: pl.BlockSpec(
        (halo, hd), lambda b, n, h: (b * halos_b + jnp.minimum((n + 1) * hpt, halos_b - 1), c + h))
    t = bsz * seq
    return pl.pallas_call(
        functools.partial(_attn_body, dil=dil, tq=tq, sub_len=sub_len),
        grid=(bsz, tiles_b, nh),
        in_specs=[own(qc), prev(kc), own(kc), nxt(kc), prev(vc), own(vc), nxt(vc),
                  pl.BlockSpec((1,) + bias.shape[1:], lambda b, n, h: (h, 0, 0))],
        out_specs=[pl.BlockSpec((tile, hd), lambda b, n, h: (b * tiles_b + n, h)),
                   pl.BlockSpec((tile, LANE), lambda b, n, h: (b * tiles_b + n, h))],
        out_shape=[jax.ShapeDtypeStruct((t, nh * hd), F32), jax.ShapeDtypeStruct((t, nh * LANE), F32)],
        scratch_shapes=[pltpu.VMEM((cu, tq + 2 * ATT_BLOCK, hd), BF16),
                        pltpu.VMEM((cu, tq + 2 * ATT_BLOCK, hd), BF16)],
        compiler_params=_cparams(("arbitrary", "arbitrary", "arbitrary"), 32),
        name=f"attn_d{dil}",
    )(proj, proj, proj, proj, proj, proj, proj, bias)


def _merge_body(rec_ref, n0_ref, n1_ref, n2_ref, s0_ref, s1_ref, s2_ref, zgr_ref, zga_ref, x_ref,
                ada_ref, gpost_ref, gpre_ref, wbr_ref, wba_ref, wo_ref, wr_ref, br_ref,
                x1_ref, h2_ref, lg_ref):
    nh = ATT_HEADS_PER_GROUP
    hd = ATT_HEAD_DIM
    half = LANE // 2
    lane = lax.broadcasted_iota(jnp.int32, (rec_ref.shape[0], LANE), 1)
    heads = []
    for h in range(nh):
        cols = slice(h * hd, (h + 1) * hd)
        st = [s[:, cols] for s in (s0_ref, s1_ref, s2_ref)]
        top = jnp.maximum(jnp.maximum(st[0], st[1]), st[2])
        ws = [jnp.exp(s - top) for s in st]
        den = (ws[0] * pltpu.roll(st[0], half, 1) + ws[1] * pltpu.roll(st[1], half, 1)
               + ws[2] * pltpu.roll(st[2], half, 1))
        coef = [w / den for w in ws]
        coef = [jnp.where(lane < half, c, pltpu.roll(c, half, 1)) for c in coef]
        num = coef[0] * n0_ref[:, cols] + coef[1] * n1_ref[:, cols] + coef[2] * n2_ref[:, cols]
        heads.append(num.astype(BF16))
    att = jnp.concatenate(heads, axis=1)
    y_rec = jnp.dot(rec_ref[...], wbr_ref[...], preferred_element_type=F32)
    y_att = jnp.dot(att, wba_ref[...], preferred_element_type=F32)
    merged = jax.nn.sigmoid(zgr_ref[...]) * y_rec + jax.nn.sigmoid(zga_ref[...]) * y_att
    y = jnp.dot(merged.astype(BF16), wo_ref[...], preferred_element_type=F32)
    gt_m = ada_ref[0, 2:3, :]
    sh_f = ada_ref[0, 3:4, :]
    sc_f = ada_ref[0, 4:5, :]
    x1 = x_ref[...] + gt_m * (_rms(y) * gpost_ref[...])
    x1_ref[...] = x1
    h2 = _rms(x1) * gpre_ref[...] * (1.0 + sc_f) + sh_f
    tm = h2.shape[0]
    for s in range(SLAB_ROWS):
        h2_ref[pl.ds(s, tm, stride=SLAB_ROWS), :] = h2[:, s * LANE:(s + 1) * LANE]
    ne = lg_ref.shape[1]
    h_hi = h2.astype(BF16)
    h_lo = (h2 - h_hi.astype(F32)).astype(BF16)
    both = jnp.dot(h_hi, wr_ref[...], preferred_element_type=F32)
    cross = jnp.dot(h_lo, wr_ref[:, 0:ne], preferred_element_type=F32)
    lg_ref[...] = both[:, 0:ne] + both[:, ne:2 * ne] + cross + br_ref[...]


def _merge(rec_o, nums, stats, proj, x2, ada3, g_post, g_pre, wbr, wba, wo, w_router, b_router, col, seq):
    t, d = x2.shape
    tm = 256
    per_b = seq // tm
    d_rec = rec_o.shape[1]
    w_att = nums[0].shape[1]
    ne = w_router.shape[1]
    wr_hi = w_router.astype(BF16)
    wr_lo = (w_router - wr_hi.astype(F32)).astype(BF16)
    w_router = jnp.concatenate([wr_hi, wr_lo], axis=1)
    dl = d // LANE
    row = lambda w: pl.BlockSpec((tm, w), lambda i: (i, 0))
    const = lambda shape: pl.BlockSpec(shape, lambda i: (0,) * len(shape), pipeline_mode=pl.Buffered(1))
    zgr = col["zg_rec"] // dl
    zga = col["zg_att"] // dl
    return pl.pallas_call(
        _merge_body,
        grid=(t // tm,),
        in_specs=[row(d_rec), row(w_att), row(w_att), row(w_att), row(w_att), row(w_att), row(w_att),
                  pl.BlockSpec((tm, d), lambda i: (i, zgr)),
                  pl.BlockSpec((tm, d), lambda i: (i, zga)),
                  row(d),
                  pl.BlockSpec((1, 6, d), lambda i: (i // per_b, 0, 0)),
                  const((1, d)), const((1, d)),
                  const(wbr.shape), const(wba.shape), const(wo.shape), const(w_router.shape),
                  const((1, ne))],
        out_specs=[row(d), pl.BlockSpec((tm * SLAB_ROWS, LANE), lambda i: (i, 0)),
                   pl.BlockSpec((tm, ne), lambda i: (i, 0))],
        out_shape=[jax.ShapeDtypeStruct((t, d), F32), jax.ShapeDtypeStruct((t * SLAB_ROWS, LANE), F32),
                   jax.ShapeDtypeStruct((t, ne), F32)],
        compiler_params=_cparams(("arbitrary",), 56),
        name="merge",
    )(rec_o, nums[0], nums[1], nums[2], stats[0], stats[1], stats[2], proj, proj, x2, ada3,
      g_post.reshape(1, d), g_pre.reshape(1, d), wbr, wba, wo, w_router, b_router.reshape(1, ne))


def _route_body(lg_ref, tri_ref, rt_ref, cnt_ref, carry_ref):
    i = pl.program_id(0)
    tr, ne = lg_ref.shape

    @pl.when(i == 0)
    def _():
        carry_ref[...] = jnp.zeros_like(carry_ref)

    l = lg_ref[...]
    lane = lax.broadcasted_iota(jnp.int32, (tr, ne), 1).astype(F32)
    vals, sels, idxs = [], [], []
    for _ in range(TOP_K):
        m = jnp.max(l, axis=-1, keepdims=True)
        idx = jnp.min(jnp.where(l == m, lane, float(ne)), axis=-1, keepdims=True)
        sel = lane == idx
        vals.append(m)
        idxs.append(idx)
        sels.append(sel)
        l = jnp.where(sel, -jnp.inf, l)
    es = [jnp.exp(v - vals[0]) for v in vals]
    tot = es[0] + es[1] + es[2] + es[3]
    chosen = (sels[0] | sels[1] | sels[2] | sels[3]).astype(F32)
    prefix = jnp.dot(tri_ref[...], chosen.astype(BF16), preferred_element_type=F32) + carry_ref[0:1, :]
    out_lane = lax.broadcasted_iota(jnp.int32, (tr, LANE), 1)
    rt = jnp.zeros((tr, LANE), F32)
    for k in range(TOP_K):
        rank = jnp.sum(jnp.where(sels[k], prefix, 0.0), axis=-1, keepdims=True)
        rt = jnp.where(out_lane == k, idxs[k], rt)
        rt = jnp.where(out_lane == TOP_K + k, es[k] / tot, rt)
        rt = jnp.where(out_lane == 2 * TOP_K + k, rank, rt)
    rt_ref[...] = rt
    new = carry_ref[0:1, :] + jnp.sum(chosen, axis=0, keepdims=True)
    carry_ref[...] = jnp.broadcast_to(new, carry_ref.shape)
    cnt_ref[...] = carry_ref[...]


def _route(logits):
    t, ne = logits.shape
    tr = 512
    tri = jnp.asarray(np.tril(np.ones((tr, tr), np.float32), -1), BF16)
    return pl.pallas_call(
        _route_body,
        grid=(t // tr,),
        in_specs=[pl.BlockSpec((tr, ne), lambda i: (i, 0)),
                  pl.BlockSpec((tr, tr), lambda i: (0, 0))],
        out_specs=[pl.BlockSpec((tr, LANE), lambda i: (i, 0)),
                   pl.BlockSpec((SUBLANE, ne), lambda i: (0, 0))],
        out_shape=[jax.ShapeDtypeStruct((t, LANE), F32), jax.ShapeDtypeStruct((SUBLANE, ne), F32)],
        scratch_shapes=[pltpu.VMEM((SUBLANE, ne), F32)],
        compiler_params=_cparams(("arbitrary",), 32),
        name="route",
    )(logits, tri)


def _chunk_rows(ncols):
    return W_CHUNK_BYTES // (4 * ncols)


def _weight_stream(w_hbm, wbf, stage, wsem, e, slot, c0, c1):
    kc = stage.shape[1]

    def copy(c):
        return pltpu.make_async_copy(w_hbm.at[e, pl.ds(pl.multiple_of(c * kc, kc), kc), :],
                                     stage.at[c % 2], wsem.at[c % 2])

    def prime():
        def body(c, carry):
            copy(c).start()
            return carry
        lax.fori_loop(c0, jnp.minimum(c0 + 2, c1), body, 0)

    def finish():
        def body(c, carry):
            copy(c).wait()
            buf = c % 2

            def cast(i, carry2):
                r = pl.multiple_of(i * CAST_ROWS, CAST_ROWS)
                wbf[slot, pl.ds(pl.multiple_of(c * kc, kc) + r, CAST_ROWS), :] = (
                    stage[buf, pl.ds(r, CAST_ROWS), :].astype(BF16))
                return carry2
            lax.fori_loop(0, kc // CAST_ROWS, cast, 0)

            @pl.when(c + 2 < c1)
            def _():
                copy(c + 2).start()
            return carry
        lax.fori_loop(c0, c1, body, 0)

    return prime, finish


def _moe_up_body(be_ref, nu_ref, ws_ref, wn_ref, wc0_ref, wc1_ref, nv_ref, idx0_ref, idxn_ref, h2s_ref, w_hbm,
                 bias_ref, o_ref, xbuf, wbf, stage, sem, wsem):
    b = pl.program_id(0)
    nu = nu_ref[0]
    f = o_ref.shape[1]
    slab = SLAB_ROWS
    nch = wbf.shape[1] // stage.shape[1]

    def row_start(idx_ref, slot, r):
        tok = idx_ref[0, 0, r]
        pltpu.make_async_copy(h2s_ref.at[pl.ds(pl.multiple_of(tok * slab, slab), slab), :],
                              xbuf.at[slot, pl.ds(pl.multiple_of(r * slab, slab), slab), :],
                              sem.at[slot]).start()

    def wait_rows(slot):
        pltpu.make_async_copy(h2s_ref.at[pl.ds(0, BM * slab), :], xbuf.at[slot], sem.at[slot]).wait()

    def issue(idx_ref, slot):
        def body(r, c):
            row_start(idx_ref, slot, r)
            return c
        lax.fori_loop(0, BM, body, 0, unroll=DMA_UNROLL)

    @pl.when(b == 0)
    def _():
        issue(idx0_ref, 0)
        prime0, finish0 = _weight_stream(w_hbm, wbf, stage, wsem, be_ref[0], ws_ref[0], 0, nch)
        prime0()
        finish0()

    @pl.when(b + 1 < nu)
    def _():
        issue(idxn_ref, (b + 1) % 2)

    @pl.when(b < nu)
    def _():
        wslot = ws_ref[b]
        prime, finish = _weight_stream(w_hbm, wbf, stage, wsem, wn_ref[b], 1 - wslot, wc0_ref[b], wc1_ref[b])
        prime()
        slot = b % 2
        wait_rows(slot)

        def compute(m):
            x = jnp.concatenate([xbuf[slot, pl.ds(s, m, stride=slab), :].astype(BF16) for s in range(slab)],
                                axis=1)
            half = f // 2
            for c0 in (0, half):
                gate = (jnp.dot(x, wbf[wslot, :, c0:c0 + half], preferred_element_type=F32)
                        + bias_ref[0, :, c0:c0 + half])
                up = (jnp.dot(x, wbf[wslot, :, f + c0:f + c0 + half], preferred_element_type=F32)
                      + bias_ref[0, :, f + c0:f + c0 + half])
                gate = jnp.minimum(gate, SWIGLU_LIMIT)
                up = jnp.clip(up, -SWIGLU_LIMIT, SWIGLU_LIMIT)
                o_ref[0:m, c0:c0 + half] = (gate * jax.nn.sigmoid(SWIGLU_ALPHA * gate) * (up + 1.0)).astype(BF16)
            if m < BM:
                o_ref[m:BM, :] = jnp.zeros((BM - m, f), BF16)

        @pl.when(nv_ref[b] > BM // 2)
        def _():
            compute(BM)

        @pl.when(nv_ref[b] <= BM // 2)
        def _():
            compute(BM // 2)

        finish()

    @pl.when(b >= nu)
    def _():
        o_ref[...] = jnp.zeros_like(o_ref)


def _moe_up(h2s, src_tok, wgu, bgu, sched):
    ne, d, f2 = wgu.shape
    f = f2 // 2
    p = src_tok.shape[0]
    nb = p // BM
    idx3 = src_tok.reshape(nb, 1, BM)
    smem_blk = lambda imap: pl.BlockSpec((1, 1, BM), imap, memory_space=pltpu.SMEM)
    grid_spec = pltpu.PrefetchScalarGridSpec(
        num_scalar_prefetch=7,
        grid=(nb,),
        in_specs=[smem_blk(lambda b, be, *_: (0, 0, 0)),
                  smem_blk(lambda b, be, *_: (jnp.minimum(b + 1, nb - 1), 0, 0)),
                  pl.BlockSpec(memory_space=pl.ANY),
                  pl.BlockSpec(memory_space=pl.ANY),
                  pl.BlockSpec((1, 1, f2), lambda b, be, *_: (be[b], 0, 0))],
        out_specs=pl.BlockSpec((BM, f), lambda b, be, *_: (b, 0)),
        scratch_shapes=[pltpu.VMEM((2, BM * SLAB_ROWS, LANE), F32),
                        pltpu.VMEM((2, d, f2), BF16),
                        pltpu.VMEM((2, _chunk_rows(f2), f2), F32),
                        pltpu.SemaphoreType.DMA((2,)), pltpu.SemaphoreType.DMA((2,))],
    )
    return pl.pallas_call(
        _moe_up_body,
        grid_spec=grid_spec,
        out_shape=jax.ShapeDtypeStruct((p, f), BF16),
        compiler_params=_cparams(("arbitrary",), 58),
        name="moe_up",
    )(sched["blk_e"], sched["n_used"], sched["wslot"], sched["wnext"], sched["wc0"], sched["wc1"],
      sched["nvalid"], idx3, idx3, h2s, wgu, bgu.reshape(ne, 1, f2))


def _moe_down_body(be_ref, nu_ref, ws_ref, wn_ref, wc0_ref, wc1_ref, nv_ref, dst_ref, a_ref, w_hbm, bias_ref,
                   ysc_ref, ybuf, wbf, stage, sem, wsem):
    b = pl.program_id(0)
    nb = pl.num_programs(0)
    nu = nu_ref[0]
    slab = SLAB_ROWS
    nch = wbf.shape[1] // stage.shape[1]

    @pl.when(b == 0)
    def _():
        prime0, finish0 = _weight_stream(w_hbm, wbf, stage, wsem, be_ref[0], ws_ref[0], 0, nch)
        prime0()
        finish0()

    def row_copy(slot, r, d):
        return pltpu.make_async_copy(ybuf.at[slot, pl.ds(pl.multiple_of(r * slab, slab), slab), :],
                                     ysc_ref.at[pl.ds(pl.multiple_of(d * slab, slab), slab), :],
                                     sem.at[slot])

    def drain(step):
        slot = step % 2
        count = nv_ref[step]

        @pl.when(count == BM)
        def _():
            pltpu.make_async_copy(ybuf.at[slot], ysc_ref.at[pl.ds(0, BM * slab), :], sem.at[slot]).wait()

        @pl.when(count < BM)
        def _():
            def body(r, c):
                row_copy(slot, 0, 0).wait()
                return c
            lax.fori_loop(0, count, body, 0)

    @pl.when((b >= 2) & (b < nu))
    def _():
        drain(b - 2)

    @pl.when(b < nu)
    def _():
        slot = b % 2
        wslot = ws_ref[b]
        prime, finish = _weight_stream(w_hbm, wbf, stage, wsem, wn_ref[b], 1 - wslot, wc0_ref[b], wc1_ref[b])
        prime()
        def compute(m):
            y = jnp.dot(a_ref[0:m, :], wbf[wslot], preferred_element_type=F32) + bias_ref[0]
            for s in range(slab):
                ybuf[slot, pl.ds(s, m, stride=slab), :] = y[:, s * LANE:(s + 1) * LANE]

        @pl.when(nv_ref[b] > BM // 2)
        def _():
            compute(BM)

        @pl.when(nv_ref[b] <= BM // 2)
        def _():
            compute(BM // 2)

        finish()

        def body(r, c):
            row_copy(slot, r, dst_ref[0, 0, r]).start()
            return c

        @pl.when(nv_ref[b] == BM)
        def _():
            lax.fori_loop(0, BM, body, 0, unroll=DMA_UNROLL)

        @pl.when(nv_ref[b] < BM)
        def _():
            lax.fori_loop(0, nv_ref[b], body, 0)

    @pl.when(b == nb - 1)
    def _():
        @pl.when(nu >= 2)
        def _():
            drain(nu - 2)
        drain(nu - 1)


def _moe_down(act, dst_slot, n_out_rows, wd, bd, sched):
    p, f = act.shape
    ne, _, d = wd.shape
    nb = p // BM
    grid_spec = pltpu.PrefetchScalarGridSpec(
        num_scalar_prefetch=7,
        grid=(nb,),
        in_specs=[pl.BlockSpec((1, 1, BM), lambda b, be, *_: (b, 0, 0), memory_space=pltpu.SMEM),
                  pl.BlockSpec((BM, f), lambda b, be, *_: (b, 0)),
                  pl.BlockSpec(memory_space=pl.ANY),
                  pl.BlockSpec((1, 1, d), lambda b, be, *_: (be[b], 0, 0))],
        out_specs=pl.BlockSpec(memory_space=pl.ANY),
        scratch_shapes=[pltpu.VMEM((2, BM * SLAB_ROWS, LANE), F32),
                        pltpu.VMEM((2, f, d), BF16),
                        pltpu.VMEM((2, _chunk_rows(d), d), F32),
                        pltpu.SemaphoreType.DMA((2,)), pltpu.SemaphoreType.DMA((2,))],
    )
    return pl.pallas_call(
        _moe_down_body,
        grid_spec=grid_spec,
        out_shape=jax.ShapeDtypeStruct((n_out_rows * SLAB_ROWS, LANE), F32),
        compiler_params=pltpu.CompilerParams(dimension_semantics=("arbitrary",),
                                             vmem_limit_bytes=48 * 1024 * 1024, has_side_effects=True),
        name="moe_down",
    )(sched["blk_e"], sched["n_used"], sched["wslot"], sched["wnext"], sched["wc0"], sched["wc1"],
      sched["nvalid"], dst_slot.reshape(nb, 1, BM), act, wd, bd.reshape(ne, 1, d))


def _final_body(ysc_ref, rt_ref, x1_ref, ada_ref, g_ref, o_ref):
    tm = x1_ref.shape[0]
    rt = rt_ref[...]
    stride = TOP_K * SLAB_ROWS
    pieces = []
    for s in range(SLAB_ROWS):
        acc = rt[:, TOP_K:TOP_K + 1] * ysc_ref[pl.ds(s, tm, stride=stride), :]
        for k in range(1, TOP_K):
            acc = acc + rt[:, TOP_K + k:TOP_K + k + 1] * ysc_ref[pl.ds(k * SLAB_ROWS + s, tm, stride=stride), :]
        pieces.append(acc)
    y = jnp.concatenate(pieces, axis=1)
    gt_f = ada_ref[0, 5:6, :]
    o_ref[...] = x1_ref[...] + gt_f * (_rms(y) * g_ref[...])


def _final(ysc, rt, x1, ada3, g_post, seq):
    t, d = x1.shape
    tm = 256
    per_b = seq // tm
    return pl.pallas_call(
        _final_body,
        grid=(t // tm,),
        in_specs=[pl.BlockSpec((tm * TOP_K * SLAB_ROWS, LANE), lambda i: (i, 0)),
                  pl.BlockSpec((tm, LANE), lambda i: (i, 0)),
                  pl.BlockSpec((tm, d), lambda i: (i, 0)),
                  pl.BlockSpec((1, 6, d), lambda i: (i // per_b, 0, 0)),
                  pl.BlockSpec((1, d), lambda i: (0, 0))],
        out_specs=pl.BlockSpec((tm, d), lambda i: (i, 0)),
        out_shape=jax.ShapeDtypeStruct((t, d), F32),
        compiler_params=_cparams(("arbitrary",), 48),
        name="final",
    )(ysc, rt, x1, ada3, g_post.reshape(1, d))


def _mixer_ffn_layer(x2, ada3, bsz, seq, g_mix_pre, g_mix_post, g_ffn_pre, g_ffn_post, w_in, lb, g_rec_out,
                     rel_bias, w_branch_rec, w_branch_att, w_o, w_router, b_router, w_gate_up, b_gate_up,
                     w_down, b_down):
    t, d = x2.shape
    d_rec = w_branch_rec.shape[0]
    w_att = w_branch_att.shape[0]
    d_att = 3 * w_att
    widths = dict(q_r=d_rec, i_r=d_rec, zf_f=d_rec, zf_b=d_rec, z_o=d_rec, q_a=d_att, k_a=d_att, v_a=d_att,
                  zg_rec=d, zg_att=d)
    my_order = ("zg_rec", "zg_att", "q_r", "i_r", "zf_f", "zf_b", "z_o", "q_a", "k_a", "v_a")
    col, acc = {}, 0
    for name in my_order:
        col[name] = acc // LANE
        acc += widths[name]
    rot = acc - 2 * d

    proj = _inproj(x2, g_mix_pre, ada3, w_in, seq, rot)

    oi, qtf, qtb, utf, utb, df, db = _hgrn_a(proj, lb, col, t, d_rec)
    rec_o = _hgrn_c(oi, qtf, qtb, utf, utb, df, db, proj, g_rec_out, col, bsz, seq, d_rec)

    nums, stats = [], []
    for g, (window, dil) in enumerate(DIL_GROUPS):
        hs = slice(g * ATT_HEADS_PER_GROUP, (g + 1) * ATT_HEADS_PER_GROUP)
        bias = _band_bias(rel_bias[:, hs], window, dil)
        num, st = _attn_group(proj, bias, col, g, dil, bsz, seq)
        nums.append(num)
        stats.append(st)

    x1, h2s, logits = _merge(rec_o, nums, stats, proj, x2, ada3, g_mix_post, g_ffn_pre,
                            w_branch_rec.astype(BF16), w_branch_att.astype(BF16), w_o.astype(BF16),
                            w_router, b_router, col, seq)

    rt, cnt = _route(logits)
    ne = logits.shape[1]
    counts = cnt[0].astype(jnp.int32)
    top_idx = rt[:, 0:TOP_K].astype(jnp.int32)
    rank = rt[:, 2 * TOP_K:3 * TOP_K].astype(jnp.int32)
    padded = (counts + BM - 1) // BM * BM
    pends = jnp.cumsum(padded)
    pstarts = pends - padded
    experts = jnp.arange(ne, dtype=jnp.int32)
    pstart_sel = jnp.sum(jnp.where(top_idx[..., None] == experts, pstarts, 0), axis=-1)
    dest = (pstart_sel + rank).reshape(-1)
    p_rows = t * TOP_K + ne * BM
    nb = p_rows // BM
    blk_start = jnp.arange(nb, dtype=jnp.int32) * BM
    blk_e = jnp.minimum(jnp.sum((pends[None, :] <= blk_start[:, None]).astype(jnp.int32), axis=1), ne - 1)
    n_used = (pends[-1:] // BM).astype(jnp.int32)

    n_assign = t * TOP_K
    slot_assign = jnp.full((p_rows,), -1, jnp.int32).at[dest].set(jnp.arange(n_assign, dtype=jnp.int32))
    valid_end = pstarts + counts
    blk_end = jnp.sum(jnp.where(blk_e[:, None] == experts, valid_end, 0), axis=-1)
    nvalid = jnp.clip(blk_end - blk_start, 0, BM).astype(jnp.int32)
    nvalid = jnp.where(jnp.arange(nb) < n_used[0], nvalid, 0)

    lookup = lambda table: jnp.sum(jnp.where(blk_e[:, None] == experts, table, 0), axis=-1)
    nonempty = padded > 0
    order = jnp.cumsum(nonempty.astype(jnp.int32)) - 1
    later = lax.cummin(jnp.where(nonempty, experts, ne)[::-1])[::-1]
    next_e = jnp.concatenate([later[1:], jnp.full((1,), ne, jnp.int32)])
    blk_next = lookup(next_e)
    has_next = (blk_next < ne) & (jnp.arange(nb) < n_used[0])
    k_in_run = jnp.arange(nb, dtype=jnp.int32) - lookup(pstarts // BM)
    n_in_run = jnp.maximum(lookup(padded // BM), 1)
    common = dict(blk_e=blk_e, n_used=n_used, nvalid=nvalid, wslot=lookup(order) % 2,
                  wnext=jnp.where(has_next, blk_next, blk_e))

    def schedule(w):
        nch = w.shape[1] // _chunk_rows(w.shape[2])
        s = dict(common, wc0=jnp.where(has_next, k_in_run * nch // n_in_run, 0),
                 wc1=jnp.where(has_next, (k_in_run + 1) * nch // n_in_run, 0))
        return {k: v.astype(jnp.int32) for k, v in s.items()}

    src_tok = jnp.maximum(slot_assign, 0) // TOP_K
    act = _moe_up(h2s, src_tok, w_gate_up, b_gate_up, schedule(w_gate_up))
    ysc = _moe_down(act, slot_assign, n_assign, w_down, b_down, schedule(w_down))
    return _final(ysc, rt, x1, ada3, g_ffn_post, seq)


def kernel(x, c, w_ada, b_ada, g_mix_pre, g_mix_post, g_ffn_pre, g_ffn_post, w_in, g_rec_out, w_branch_rec,
           w_branch_att, w_o, w_router, b_router, w_gate_up, b_gate_up, w_down, b_down, rec_lb_table, rel_bias):
    bsz, seq, d = x.shape
    depth = w_in.shape[0]
    lb_all = jnp.cumsum(jax.nn.softmax(rec_lb_table.astype(F32), axis=1), axis=1)
    x2 = x.reshape(bsz * seq, d)
    for layer in range(depth):
        ada3 = _ada(c, w_ada[layer], b_ada[layer]).reshape(bsz, 6, d)
        x2 = _mixer_ffn_layer(x2, ada3, bsz, seq, g_mix_pre[layer], g_mix_post[layer], g_ffn_pre[layer],
                              g_ffn_post[layer], w_in[layer], lb_all[:, layer], g_rec_out[layer], rel_bias,
                              w_branch_rec[layer], w_branch_att[layer], w_o[layer], w_router[layer],
                              b_router[layer], w_gate_up[layer], b_gate_up[layer], w_down[layer],
                              b_down[layer])
    return x2.reshape(bsz, seq, d)
```

```python
import functools
import math

import numpy as np
import jax
import jax.numpy as jnp
from jax import lax
from jax.experimental import pallas as pl
from jax.experimental.pallas import tpu as pltpu

F32 = jnp.float32
BF16 = jnp.bfloat16

LANE = 128
SUBLANE = 8
SLAB_ROWS = 16

REC_HEAD_DIM = 128
REC_CHUNK = 64
ATT_HEAD_DIM = 128
ATT_HEADS_PER_GROUP = 4
ATT_BLOCK = 64
DIL_GROUPS = ((128, 1), (512, 4), (2048, 16))
NUM_BUCKETS = 32
MAX_DISTANCE = 1024
N_EXPERTS = 32
TOP_K = 4
SWIGLU_LIMIT = 7.0
SWIGLU_ALPHA = 1.702
RMS_EPS = 1e-6
NEG_INF = -1e30

N_LEVELS = 6
W_CHUNK_BYTES = 4 * 1024 * 1024
CAST_ROWS = 32
HGRN_UNROLL = 4
DMA_UNROLL = 8
ATT_UNROLL = 8
BM = 256

_NT = (((1,), (1,)), ((), ()))
_TN = (((0,), (0,)), ((), ()))


def _cparams(sem, vmem_mb):
    return pltpu.CompilerParams(dimension_semantics=sem, vmem_limit_bytes=vmem_mb * 1024 * 1024)


def _rms(x):
    return x * lax.rsqrt(jnp.mean(x * x, axis=-1, keepdims=True) + RMS_EPS)


def _ada_body(c_ref, w_ref, b_ref, o_ref):
    c = c_ref[...]
    cond = (c * jax.nn.sigmoid(c)).astype(BF16)
    o_ref[...] = jnp.dot(cond, w_ref[...].astype(BF16), preferred_element_type=F32) + b_ref[...]


def _ada(c, w, b):
    bsz, d = c.shape
    n = w.shape[1]
    tn = 1024
    cp = jnp.zeros((SUBLANE, d), F32).at[:bsz].set(c)
    out = pl.pallas_call(
        _ada_body,
        grid=(n // tn,),
        in_specs=[pl.BlockSpec((SUBLANE, d), lambda j: (0, 0)),
                  pl.BlockSpec((d, tn), lambda j: (0, j)),
                  pl.BlockSpec((1, tn), lambda j: (0, j))],
        out_specs=pl.BlockSpec((SUBLANE, tn), lambda j: (0, j)),
        out_shape=jax.ShapeDtypeStruct((SUBLANE, n), F32),
        compiler_params=_cparams(("arbitrary",), 40),
        name="ada",
    )(cp, w, b.reshape(1, n))
    return out[:bsz]


def _inproj_body(x_ref, g_ref, ada_ref, w_ref, o_ref, h_ref):
    @pl.when(pl.program_id(1) == 0)
    def _():
        half = x_ref.shape[0] // 2
        sh = ada_ref[0, 0:1, :]
        sc = ada_ref[0, 1:2, :]
        for r0 in (0, half):
            y = _rms(x_ref[r0:r0 + half, :]) * g_ref[...]
            h_ref[r0:r0 + half, :] = (y * (1.0 + sc) + sh).astype(BF16)

    o_ref[...] = jnp.dot(h_ref[...], w_ref[...].astype(BF16), preferred_element_type=F32)


def _inproj(x2, g, ada3, w_in, seq, rot):
    t, d = x2.shape
    n = w_in.shape[1]
    tm, tn = 2048, 512
    per_b = seq // tm
    nj = n // tn
    assert rot % tn == 0 and n % tn == 0 and seq % tm == 0
    return pl.pallas_call(
        _inproj_body,
        grid=(t // tm, nj),
        in_specs=[pl.BlockSpec((tm, d), lambda i, j: (i, 0), pipeline_mode=pl.Buffered(1)),
                  pl.BlockSpec((1, d), lambda i, j: (0, 0)),
                  pl.BlockSpec((1, 6, d), lambda i, j: (i // per_b, 0, 0)),
                  pl.BlockSpec((d, tn), lambda i, j: (0, (j + rot // tn) % nj))],
        out_specs=pl.BlockSpec((tm, tn), lambda i, j: (i, j)),
        out_shape=jax.ShapeDtypeStruct((t, n), F32),
        scratch_shapes=[pltpu.VMEM((tm, d), BF16)],
        compiler_params=_cparams(("arbitrary", "arbitrary"), 56),
        name="inproj",
    )(x2, g.reshape(1, d), ada3, w_in)


def _hgrn_consts():
    c = REC_CHUNK
    r = np.arange(c)[:, None]
    m = np.arange(c)[None, :]
    wf = np.zeros((8 * c, c), np.float32)
    wb = np.zeros((8 * c, c), np.float32)
    mf = np.zeros((N_LEVELS + 1, c, c), np.float32)
    for lvl in range(N_LEVELS):
        s = 32 >> lvl
        m0 = (r // (2 * s)) * (2 * s) + s
        up = r >= m0
        wf[lvl * c:(lvl + 1) * c] = np.where(up, (m >= m0) & (m <= r), (m > r) & (m <= m0 - 1))
        wb[lvl * c:(lvl + 1) * c] = np.where(up, (m >= m0) & (m <= r - 1), (m >= r) & (m <= m0 - 1))
        i = np.arange(c)[:, None]
        j = np.arange(c)[None, :]
        mf[lvl] = (i // (2 * s) == j // (2 * s)) & (i % (2 * s) >= s) & (j % (2 * s) < s)
    mf[N_LEVELS] = np.eye(c)
    wf[6 * c:7 * c] = m <= r
    wf[7 * c:8 * c] = m > r
    wb[6 * c:7 * c] = m >= r
    wb[7 * c:8 * c] = m < r
    mfb = mf + np.transpose(mf, (0, 2, 1))
    mfb[N_LEVELS] = np.eye(c)
    up = np.zeros((N_LEVELS, c, LANE), np.float32)
    for lvl in range(N_LEVELS):
        s = 32 >> lvl
        up[lvl] = ((np.arange(c) % (2 * s)) >= s)[:, None]
    wf3 = np.concatenate([wf, wf, wf], axis=1)
    wb3 = np.concatenate([wb, wb, wb], axis=1)
    return (jnp.asarray(wf3, BF16), jnp.asarray(wb3, BF16), jnp.asarray(mfb, F32), jnp.asarray(up, F32),
            jnp.asarray(1.0 - up, F32))


def _split3(g):
    hi = g.astype(BF16)
    r1 = g - hi.astype(F32)
    mid = r1.astype(BF16)
    lo = (r1 - mid.astype(F32)).astype(BF16)
    return jnp.concatenate([hi, mid, lo], axis=0)


def _hgrn_a_body(q_ref, i_ref, zf_ref, zb_ref, lb_ref, wf_ref, wb_ref, mf_ref, up_ref, lo_ref,
                 oi_ref, qtf_ref, qtb_ref, utf_ref, utb_ref, df_ref, db_ref, *, cpb):
    c = REC_CHUNK
    dirs = ((zf_ref, wf_ref, None, qtf_ref, utf_ref, df_ref, 0, c - 1),
            (zb_ref, wb_ref, None, qtb_ref, utb_ref, db_ref, 1, 0))

    def chunk_group(cg, carry):
        cis = [cg * HGRN_UNROLL + u for u in range(HGRN_UNROLL)]
        rows = [pl.ds(pl.multiple_of(ci * c, c), c) for ci in cis]
        zqs = [q_ref[rw, :] for rw in rows]
        qs = [zq * jax.nn.sigmoid(zq) for zq in zqs]
        vbs = [i_ref[rw, :].astype(BF16) for rw in rows]
        units = [(u, d) for u in range(HGRN_UNROLL) for d in range(2)]
        ks, es = {}, {}
        for u, d in units:
            z_ref, w_ref = dirs[d][0], dirs[d][1]
            lb = lb_ref[d:d + 1, :]
            f = lb + (1.0 - lb) * jax.nn.sigmoid(z_ref[rows[u], :])
            ks[u, d] = 1.0 - f
            es[u, d] = jnp.exp(jnp.dot(w_ref[...], _split3(jnp.log(f)), preferred_element_type=F32))
        acc = [jnp.zeros((c, c), F32) for _ in range(HGRN_UNROLL)]
        for lvl in range(N_LEVELS + 1):
            for u in range(HGRN_UNROLL):
                if lvl < N_LEVELS:
                    ef = es[u, 0][lvl * c:(lvl + 1) * c]
                    eb = es[u, 1][lvl * c:(lvl + 1) * c]
                    up, lo = up_ref[lvl], lo_ref[lvl]
                    qa = jnp.concatenate([(qs[u] * (ef * up)).astype(BF16), (qs[u] * (eb * lo)).astype(BF16)],
                                         axis=1)
                    ka = jnp.concatenate([(ks[u, 0] * (ef * lo)).astype(BF16),
                                          (ks[u, 1] * (eb * up)).astype(BF16)], axis=1)
                else:
                    qa, ka = qs[u].astype(BF16), (ks[u, 0] + ks[u, 1]).astype(BF16)
                p = lax.dot_general(qa, ka, _NT, preferred_element_type=F32)
                acc[u] = acc[u] + p * mf_ref[lvl]
        for u, d in units:
            _, _, _, qt_ref, ut_ref, d_ref, _, drow = dirs[d]
            e = es[u, d]
            qt_ref[rows[u], :] = (qs[u] * e[6 * c:7 * c]).astype(BF16)
            kt = (ks[u, d] * e[7 * c:8 * c]).astype(BF16)
            ut_ref[cis[u]] = lax.dot_general(vbs[u], kt, _TN, preferred_element_type=F32)
            d_ref[pl.ds(cis[u], 1), :] = e[6 * c + drow:6 * c + drow + 1]
        for u in range(HGRN_UNROLL):
            oi_ref[rows[u], :] = jnp.dot(acc[u].astype(BF16), vbs[u], preferred_element_type=F32)
        return carry

    lax.fori_loop(0, cpb // HGRN_UNROLL, chunk_group, 0)


def _hgrn_a(proj, lb, col, t, d_rec):
    heads = d_rec // REC_HEAD_DIM
    tq = 1024
    cpb = tq // REC_CHUNK
    nchunks = t // REC_CHUNK
    wf, wb, mf, up, lo = _hgrn_consts()
    hd = REC_HEAD_DIM

    def colspec(off):
        return pl.BlockSpec((tq, hd), lambda i, h: (i, off + h))

    full2 = lambda i, h: (0, 0)
    full3 = lambda i, h: (0, 0, 0)
    row_spec = pl.BlockSpec((tq, hd), lambda i, h: (i, h))
    u_spec = pl.BlockSpec((cpb, hd, hd), lambda i, h: (i, 0, h))
    d_spec = pl.BlockSpec((cpb, hd), lambda i, h: (i, h))
    return pl.pallas_call(
        functools.partial(_hgrn_a_body, cpb=cpb),
        grid=(t // tq, heads),
        in_specs=[colspec(col["q_r"]), colspec(col["i_r"]), colspec(col["zf_f"]), colspec(col["zf_b"]),
                  pl.BlockSpec((2, hd), lambda i, h: (0, h)),
                  pl.BlockSpec(wf.shape, full2), pl.BlockSpec(wb.shape, full2),
                  pl.BlockSpec(mf.shape, full3), pl.BlockSpec(up.shape, full3), pl.BlockSpec(lo.shape, full3)],
        out_specs=[row_spec, row_spec, row_spec, u_spec, u_spec, d_spec, d_spec],
        out_shape=[jax.ShapeDtypeStruct((t, d_rec), F32),
                   jax.ShapeDtypeStruct((t, d_rec), BF16),
                   jax.ShapeDtypeStruct((t, d_rec), BF16),
                   jax.ShapeDtypeStruct((nchunks, hd, d_rec), F32),
                   jax.ShapeDtypeStruct((nchunks, hd, d_rec), F32),
                   jax.ShapeDtypeStruct((nchunks, d_rec), F32),
                   jax.ShapeDtypeStruct((nchunks, d_rec), F32)],
        compiler_params=_cparams(("arbitrary", "arbitrary"), 32),
        name="hgrn_a",
    )(proj, proj, proj, proj, lb, wf, wb, mf, up, lo)


def _hgrn_c_body(oi_ref, qtf_ref, qtb_ref, utf_ref, utb_ref, df_ref, db_ref, z_ref, g_ref,
                 out_ref, acc_ref, accb_ref, *, nchunks):
    c = REC_CHUNK
    hd = REC_HEAD_DIM

    unroll = 4

    def step(i, carry):
        st_f, st_b = carry
        pending = []
        for u in range(unroll):
            nf = i * unroll + u
            nb = nchunks - 1 - nf
            rows_f = pl.ds(pl.multiple_of(nf * c, c), c)
            rows_b = pl.ds(pl.multiple_of(nb * c, c), c)
            of = lax.dot_general(qtf_ref[rows_f, :], st_f.astype(BF16), _NT, preferred_element_type=F32)
            ob = lax.dot_general(qtb_ref[rows_b, :], st_b.astype(BF16), _NT, preferred_element_type=F32)
            pending.append((rows_f, rows_b, oi_ref[rows_f, :] + of, ob))
            st_f = df_ref[pl.ds(nf, 1), :] * st_f + utf_ref[nf]
            st_b = db_ref[pl.ds(nb, 1), :] * st_b + utb_ref[nb]
        for rows_f, rows_b, vf, vb in pending:
            acc_ref[rows_f, :] = vf
            accb_ref[rows_b, :] = vb
        return st_f, st_b

    zero = jnp.zeros((hd, hd), F32)
    lax.fori_loop(0, nchunks // unroll, step, (zero, zero))

    o = _rms(acc_ref[...] + accb_ref[...])
    out_ref[...] = (o * g_ref[...] * jax.nn.sigmoid(z_ref[...])).astype(BF16)


def _hgrn_c(oi, qtf, qtb, utf, utb, df, db, proj, g_out, col, bsz, seq, d_rec):
    heads = d_rec // REC_HEAD_DIM
    hd = REC_HEAD_DIM
    nchunks = seq // REC_CHUNK
    row_spec = pl.BlockSpec((seq, hd), lambda b, h: (b, h))
    u_spec = pl.BlockSpec((nchunks, hd, hd), lambda b, h: (b, 0, h))
    d_spec = pl.BlockSpec((nchunks, hd), lambda b, h: (b, h))
    zo = col["z_o"]
    return pl.pallas_call(
        functools.partial(_hgrn_c_body, nchunks=nchunks),
        grid=(bsz, heads),
        in_specs=[row_spec, row_spec, row_spec, u_spec, u_spec, d_spec, d_spec,
                  pl.BlockSpec((seq, hd), lambda b, h: (b, zo + h)),
                  pl.BlockSpec((1, hd), lambda b, h: (0, h))],
        out_specs=row_spec,
        out_shape=jax.ShapeDtypeStruct((bsz * seq, d_rec), BF16),
        scratch_shapes=[pltpu.VMEM((seq, hd), F32), pltpu.VMEM((seq, hd), F32)],
        compiler_params=_cparams(("arbitrary", "arbitrary"), 48),
        name="hgrn_c",
    )(oi, qtf, qtb, utf, utb, df, db, proj, g_out.reshape(1, d_rec))


def _t5_bucket(rel):
    half_buckets = NUM_BUCKETS // 2
    ret = np.where(rel > 0, half_buckets, 0)
    n = np.abs(rel)
    max_exact = half_buckets // 2
    nf = np.maximum(n, 1).astype(np.float32)
    large = max_exact + (np.log(nf / np.float32(max_exact)) / np.float32(math.log(MAX_DISTANCE / max_exact))
                         * np.float32(half_buckets - max_exact)).astype(np.int32)
    large = np.minimum(large, half_buckets - 1)
    return ret + np.where(n < max_exact, n, large)


def _band_bias(rel_bias_g, window, dil):
    half = window // (2 * dil)
    q_off = np.arange(ATT_BLOCK)[:, None]
    rel = np.arange(3 * ATT_BLOCK)[None, :] - ATT_BLOCK - q_off
    onehot = (_t5_bucket(rel * dil)[..., None] == np.arange(NUM_BUCKETS)).astype(np.float32)
    bias = jnp.einsum("qkb,bh->hqk", jnp.asarray(onehot), rel_bias_g.astype(F32),
                      precision=lax.Precision.HIGHEST)
    return jnp.where(jnp.asarray(np.abs(rel) <= half)[None], bias, NEG_INF)


def _attn_body(q_ref, kp_ref, k_ref, kn_ref, vp_ref, v_ref, vn_ref, bias_ref,
               num_ref, st_ref, kc_ref, vc_ref, *, dil, tq, sub_len):
    blk = ATT_BLOCK
    nqb = tq // blk
    n = pl.program_id(1)
    scale = ATT_HEAD_DIM ** -0.5

    def sds(start, size):
        if dil == 1:
            return pl.ds(start, size)
        return pl.ds(start, size, stride=dil)

    cu = kc_ref.shape[0]
    qu = ATT_UNROLL // cu

    def deinterleave(r, j):
        kc_ref[j, 0:blk, :] = kp_ref[sds(r, blk), :].astype(BF16)
        kc_ref[j, blk:blk + tq, :] = k_ref[sds(r, tq), :].astype(BF16)
        kc_ref[j, blk + tq:2 * blk + tq, :] = kn_ref[sds(r, blk), :].astype(BF16)
        vc_ref[j, 0:blk, :] = vp_ref[sds(r, blk), :].astype(BF16)
        vc_ref[j, blk:blk + tq, :] = v_ref[sds(r, tq), :].astype(BF16)
        vc_ref[j, blk + tq:2 * blk + tq, :] = vn_ref[sds(r, blk), :].astype(BF16)

    def units(r0, qb0):
        us = [(j, u) for j in range(cu) for u in range(qu)]
        q0s = [pl.multiple_of((qb0 + u) * blk, blk) for _, u in us]
        rows = [sds(r0 + j + dil * q0, blk) for (j, _), q0 in zip(us, q0s)]
        lane = lax.broadcasted_iota(jnp.int32, (blk, LANE), 1)
        key_iota = lax.broadcasted_iota(jnp.int32, (1, 3 * blk), 1)
        bias = bias_ref[0]
        qs = [q_ref[rw, :].astype(BF16) for rw in rows]
        kws = [kc_ref[j, pl.ds(q0, 3 * blk), :] for (j, _), q0 in zip(us, q0s)]
        vws = [vc_ref[j, pl.ds(q0, 3 * blk), :] for (j, _), q0 in zip(us, q0s)]
        ss = [lax.dot_general(q, kw, _NT, preferred_element_type=F32) * scale for q, kw in zip(qs, kws)]
        valids = []
        for q0 in q0s:
            kpos = n * tq + q0 - blk + key_iota
            valids.append((kpos >= 0) & (kpos < sub_len))
        ss = [jnp.where(valid, s + bias, NEG_INF) for s, valid in zip(ss, valids)]
        ms = [jnp.max(s, axis=-1, keepdims=True) for s in ss]
        ps = [jnp.exp(s - m) for s, m in zip(ss, ms)]
        ls = [jnp.sum(p, axis=-1, keepdims=True) for p in ps]
        nums = [jnp.dot(p.astype(BF16), vw, preferred_element_type=F32) for p, vw in zip(ps, vws)]
        for rw, num, m, l in zip(rows, nums, ms, ls):
            num_ref[rw, :] = num
            st_ref[rw, :] = jnp.where(lane < LANE // 2, m, l)

    def class_group(rg, carry):
        r0 = rg * cu
        for j in range(cu):
            deinterleave(r0 + j, j)

        def qgroup(qg, carry2):
            units(r0, qg * qu)
            return carry2

        lax.fori_loop(0, nqb // qu, qgroup, 0)
        return carry

    lax.fori_loop(0, dil // cu, class_group, 0)


def _attn_group(proj, bias, col, g, dil, bsz, seq):
    tile = 1024
    tq = tile // dil
    halo = ATT_BLOCK * dil
    sub_len = seq // dil
    cu = ATT_UNROLL // min(tq // ATT_BLOCK, ATT_UNROLL)
    nh = ATT_HEADS_PER_GROUP
    hd = ATT_HEAD_DIM
    qc = col["q_a"] + g * nh
    kc = col["k_a"] + g * nh
    vc = col["v_a"] + g * nh
    tiles_b = seq // tile
    halos_b = seq // halo
    hpt = tile // halo

    own = lambda c: pl.BlockSpec((tile, hd), lambda b, n, h: (b * tiles_b + n, c + h))
    prev = lambda c: pl.BlockSpec(
        (halo, hd), lambda b, n, h: (b * halos_b + jnp.maximum(n * hpt - 1, 0), c + h))
    nxt = lambda c: pl.BlockSpec(
        (halo, hd), lambda b, n, h: (b * halos_b + jnp.minimum((n + 1) * hpt, halos_b - 1), c + h))
    t = bsz * seq
    return pl.pallas_call(
        functools.partial(_attn_body, dil=dil, tq=tq, sub_len=sub_len),
        grid=(bsz, tiles_b, nh),
        in_specs=[own(qc), prev(kc), own(kc), nxt(kc), prev(vc), own(vc), nxt(vc),
                  pl.BlockSpec((1,) + bias.shape[1:], lambda b, n, h: (h, 0, 0))],
        out_specs=[pl.BlockSpec((tile, hd), lambda b, n, h: (b * tiles_b + n, h)),
                   pl.BlockSpec((tile, LANE), lambda b, n, h: (b * tiles_b + n, h))],
        out_shape=[jax.ShapeDtypeStruct((t, nh * hd), F32), jax.ShapeDtypeStruct((t, nh * LANE), F32)],
        scratch_shapes=[pltpu.VMEM((cu, tq + 2 * ATT_BLOCK, hd), BF16),
                        pltpu.VMEM((cu, tq + 2 * ATT_BLOCK, hd), BF16)],
        compiler_params=_cparams(("arbitrary", "arbitrary", "arbitrary"), 32),
        name=f"attn_d{dil}",
    )(proj, proj, proj, proj, proj, proj, proj, bias)


def _merge_body(rec_ref, n0_ref, n1_ref, n2_ref, s0_ref, s1_ref, s2_ref, zgr_ref, zga_ref, x_ref,
                ada_ref, gpost_ref, gpre_ref, wbr_ref, wba_ref, wo_ref, wr_ref, br_ref,
                x1_ref, h2_ref, lg_ref):
    nh = ATT_HEADS_PER_GROUP
    hd = ATT_HEAD_DIM
    half = LANE // 2
    lane = lax.broadcasted_iota(jnp.int32, (rec_ref.shape[0], LANE), 1)
    heads = []
    for h in range(nh):
        cols = slice(h * hd, (h + 1) * hd)
        st = [s[:, cols] for s in (s0_ref, s1_ref, s2_ref)]
        top = jnp.maximum(jnp.maximum(st[0], st[1]), st[2])
        ws = [jnp.exp(s - top) for s in st]
        den = (ws[0] * pltpu.roll(st[0], half, 1) + ws[1] * pltpu.roll(st[1], half, 1)
               + ws[2] * pltpu.roll(st[2], half, 1))
        coef = [w / den for w in ws]
        coef = [jnp.where(lane < half, c, pltpu.roll(c, half, 1)) for c in coef]
        num = coef[0] * n0_ref[:, cols] + coef[1] * n1_ref[:, cols] + coef[2] * n2_ref[:, cols]
        heads.append(num.astype(BF16))
    att = jnp.concatenate(heads, axis=1)
    y_rec = jnp.dot(rec_ref[...], wbr_ref[...], preferred_element_type=F32)
    y_att = jnp.dot(att, wba_ref[...], preferred_element_type=F32)
    merged = jax.nn.sigmoid(zgr_ref[...]) * y_rec + jax.nn.sigmoid(zga_ref[...]) * y_att
    y = jnp.dot(merged.astype(BF16), wo_ref[...], preferred_element_type=F32)
    gt_m = ada_ref[0, 2:3, :]
    sh_f = ada_ref[0, 3:4, :]
    sc_f = ada_ref[0, 4:5, :]
    x1 = x_ref[...] + gt_m * (_rms(y) * gpost_ref[...])
    x1_ref[...] = x1
    h2 = _rms(x1) * gpre_ref[...] * (1.0 + sc_f) + sh_f
    tm = h2.shape[0]
    for s in range(SLAB_ROWS):
        h2_ref[pl.ds(s, tm, stride=SLAB_ROWS), :] = h2[:, s * LANE:(s + 1) * LANE]
    ne = lg_ref.shape[1]
    h_hi = h2.astype(BF16)
    h_lo = (h2 - h_hi.astype(F32)).astype(BF16)
    both = jnp.dot(h_hi, wr_ref[...], preferred_element_type=F32)
    cross = jnp.dot(h_lo, wr_ref[:, 0:ne], preferred_element_type=F32)
    lg_ref[...] = both[:, 0:ne] + both[:, ne:2 * ne] + cross + br_ref[...]


def _merge(rec_o, nums, stats, proj, x2, ada3, g_post, g_pre, wbr, wba, wo, w_router, b_router, col, seq):
    t, d = x2.shape
    tm = 256
    per_b = seq // tm
    d_rec = rec_o.shape[1]
    w_att = nums[0].shape[1]
    ne = w_router.shape[1]
    wr_hi = w_router.astype(BF16)
    wr_lo = (w_router - wr_hi.astype(F32)).astype(BF16)
    w_router = jnp.concatenate([wr_hi, wr_lo], axis=1)
    dl = d // LANE
    row = lambda w: pl.BlockSpec((tm, w), lambda i: (i, 0))
    const = lambda shape: pl.BlockSpec(shape, lambda i: (0,) * len(shape), pipeline_mode=pl.Buffered(1))
    zgr = col["zg_rec"] // dl
    zga = col["zg_att"] // dl
    return pl.pallas_call(
        _merge_body,
        grid=(t // tm,),
        in_specs=[row(d_rec), row(w_att), row(w_att), row(w_att), row(w_att), row(w_att), row(w_att),
                  pl.BlockSpec((tm, d), lambda i: (i, zgr)),
                  pl.BlockSpec((tm, d), lambda i: (i, zga)),
                  row(d),
                  pl.BlockSpec((1, 6, d), lambda i: (i // per_b, 0, 0)),
                  const((1, d)), const((1, d)),
                  const(wbr.shape), const(wba.shape), const(wo.shape), const(w_router.shape),
                  const((1, ne))],
        out_specs=[row(d), pl.BlockSpec((tm * SLAB_ROWS, LANE), lambda i: (i, 0)),
                   pl.BlockSpec((tm, ne), lambda i: (i, 0))],
        out_shape=[jax.ShapeDtypeStruct((t, d), F32), jax.ShapeDtypeStruct((t * SLAB_ROWS, LANE), F32),
                   jax.ShapeDtypeStruct((t, ne), F32)],
        compiler_params=_cparams(("arbitrary",), 56),
        name="merge",
    )(rec_o, nums[0], nums[1], nums[2], stats[0], stats[1], stats[2], proj, proj, x2, ada3,
      g_post.reshape(1, d), g_pre.reshape(1, d), wbr, wba, wo, w_router, b_router.reshape(1, ne))


def _route_body(lg_ref, tri_ref, rt_ref, cnt_ref, carry_ref):
    i = pl.program_id(0)
    tr, ne = lg_ref.shape

    @pl.when(i == 0)
    def _():
        carry_ref[...] = jnp.zeros_like(carry_ref)

    l = lg_ref[...]
    lane = lax.broadcasted_iota(jnp.int32, (tr, ne), 1).astype(F32)
    vals, sels, idxs = [], [], []
    for _ in range(TOP_K):
        m = jnp.max(l, axis=-1, keepdims=True)
        idx = jnp.min(jnp.where(l == m, lane, float(ne)), axis=-1, keepdims=True)
        sel = lane == idx
        vals.append(m)
        idxs.append(idx)
        sels.append(sel)
        l = jnp.where(sel, -jnp.inf, l)
    es = [jnp.exp(v - vals[0]) for v in vals]
    tot = es[0] + es[1] + es[2] + es[3]
    chosen = (sels[0] | sels[1] | sels[2] | sels[3]).astype(F32)
    prefix = jnp.dot(tri_ref[...], chosen.astype(BF16), preferred_element_type=F32) + carry_ref[0:1, :]
    out_lane = lax.broadcasted_iota(jnp.int32, (tr, LANE), 1)
    rt = jnp.zeros((tr, LANE), F32)
    for k in range(TOP_K):
        rank = jnp.sum(jnp.where(sels[k], prefix, 0.0), axis=-1, keepdims=True)
        rt = jnp.where(out_lane == k, idxs[k], rt)
        rt = jnp.where(out_lane == TOP_K + k, es[k] / tot, rt)
        rt = jnp.where(out_lane == 2 * TOP_K + k, rank, rt)
    rt_ref[...] = rt
    new = carry_ref[0:1, :] + jnp.sum(chosen, axis=0, keepdims=True)
    carry_ref[...] = jnp.broadcast_to(new, carry_ref.shape)
    cnt_ref[...] = carry_ref[...]


def _route(logits):
    t, ne = logits.shape
    tr = 512
    tri = jnp.asarray(np.tril(np.ones((tr, tr), np.float32), -1), BF16)
    return pl.pallas_call(
        _route_body,
        grid=(t // tr,),
        in_specs=[pl.BlockSpec((tr, ne), lambda i: (i, 0)),
                  pl.BlockSpec((tr, tr), lambda i: (0, 0))],
        out_specs=[pl.BlockSpec((tr, LANE), lambda i: (i, 0)),
                   pl.BlockSpec((SUBLANE, ne), lambda i: (0, 0))],
        out_shape=[jax.ShapeDtypeStruct((t, LANE), F32), jax.ShapeDtypeStruct((SUBLANE, ne), F32)],
        scratch_shapes=[pltpu.VMEM((SUBLANE, ne), F32)],
        compiler_params=_cparams(("arbitrary",), 32),
        name="route",
    )(logits, tri)


def _chunk_rows(ncols):
    return W_CHUNK_BYTES // (4 * ncols)


def _weight_stream(w_hbm, wbf, stage, wsem, e, slot, c0, c1, priority=0):
    kc = stage.shape[1]

    def copy(c):
        return pltpu.make_async_copy(w_hbm.at[e, pl.ds(pl.multiple_of(c * kc, kc), kc), :],
                                     stage.at[c % 2], wsem.at[c % 2])

    def prime():
        def body(c, carry):
            copy(c).start(priority=priority)
            return carry
        lax.fori_loop(c0, jnp.minimum(c0 + 2, c1), body, 0)

    def finish():
        def body(c, carry):
            copy(c).wait()
            buf = c % 2

            def cast(i, carry2):
                r = pl.multiple_of(i * CAST_ROWS, CAST_ROWS)
                wbf[slot, pl.ds(pl.multiple_of(c * kc, kc) + r, CAST_ROWS), :] = (
                    stage[buf, pl.ds(r, CAST_ROWS), :].astype(BF16))
                return carry2
            lax.fori_loop(0, kc // CAST_ROWS, cast, 0)

            @pl.when(c + 2 < c1)
            def _():
                copy(c + 2).start(priority=priority)
            return carry
        lax.fori_loop(c0, c1, body, 0)

    return prime, finish


def _moe_up_body(be_ref, nu_ref, ws_ref, wn_ref, wc0_ref, wc1_ref, nv_ref, idx0_ref, idxn_ref, h2s_ref, w_hbm,
                 bias_ref, o_ref, xbuf, wbf, stage, sem, wsem):
    b = pl.program_id(0)
    nu = nu_ref[0]
    f = o_ref.shape[1]
    slab = SLAB_ROWS
    nch = wbf.shape[1] // stage.shape[1]

    def row_start(idx_ref, slot, r):
        tok = idx_ref[0, 0, r]
        pltpu.make_async_copy(h2s_ref.at[pl.ds(pl.multiple_of(tok * slab, slab), slab), :],
                              xbuf.at[slot, pl.ds(pl.multiple_of(r * slab, slab), slab), :],
                              sem.at[slot]).start()

    def wait_rows(slot):
        pltpu.make_async_copy(h2s_ref.at[pl.ds(0, BM * slab), :], xbuf.at[slot], sem.at[slot]).wait()

    def issue(idx_ref, slot):
        def body(r, c):
            row_start(idx_ref, slot, r)
            return c
        lax.fori_loop(0, BM, body, 0, unroll=DMA_UNROLL)

    @pl.when(b == 0)
    def _():
        issue(idx0_ref, 0)
        prime0, finish0 = _weight_stream(w_hbm, wbf, stage, wsem, be_ref[0], ws_ref[0], 0, nch)
        prime0()
        finish0()

    @pl.when(b + 1 < nu)
    def _():
        issue(idxn_ref, (b + 1) % 2)

    @pl.when(b < nu)
    def _():
        wslot = ws_ref[b]
        prime, finish = _weight_stream(w_hbm, wbf, stage, wsem, wn_ref[b], 1 - wslot, wc0_ref[b], wc1_ref[b],
                                       priority=1)
        prime()
        slot = b % 2
        wait_rows(slot)

        def compute(m):
            x = jnp.concatenate([xbuf[slot, pl.ds(s, m, stride=slab), :].astype(BF16) for s in range(slab)],
                                axis=1)
            half = f // 2
            for c0 in (0, half):
                gate = (jnp.dot(x, wbf[wslot, :, c0:c0 + half], preferred_element_type=F32)
                        + bias_ref[0, :, c0:c0 + half])
                up = (jnp.dot(x, wbf[wslot, :, f + c0:f + c0 + half], preferred_element_type=F32)
                      + bias_ref[0, :, f + c0:f + c0 + half])
                gate = jnp.minimum(gate, SWIGLU_LIMIT)
                up = jnp.clip(up, -SWIGLU_LIMIT, SWIGLU_LIMIT)
                o_ref[0:m, c0:c0 + half] = (gate * jax.nn.sigmoid(SWIGLU_ALPHA * gate) * (up + 1.0)).astype(BF16)
            if m < BM:
                o_ref[m:BM, :] = jnp.zeros((BM - m, f), BF16)

        @pl.when(nv_ref[b] > BM // 2)
        def _():
            compute(BM)

        @pl.when(nv_ref[b] <= BM // 2)
        def _():
            compute(BM // 2)

        finish()

    @pl.when(b >= nu)
    def _():
        o_ref[...] = jnp.zeros_like(o_ref)


def _moe_up(h2s, src_tok, wgu, bgu, sched):
    ne, d, f2 = wgu.shape
    f = f2 // 2
    p = src_tok.shape[0]
    nb = p // BM
    idx3 = src_tok.reshape(nb, 1, BM)
    smem_blk = lambda imap: pl.BlockSpec((1, 1, BM), imap, memory_space=pltpu.SMEM)
    grid_spec = pltpu.PrefetchScalarGridSpec(
        num_scalar_prefetch=7,
        grid=(nb,),
        in_specs=[smem_blk(lambda b, be, *_: (0, 0, 0)),
                  smem_blk(lambda b, be, *_: (jnp.minimum(b + 1, nb - 1), 0, 0)),
                  pl.BlockSpec(memory_space=pl.ANY),
                  pl.BlockSpec(memory_space=pl.ANY),
                  pl.BlockSpec((1, 1, f2), lambda b, be, *_: (be[b], 0, 0))],
        out_specs=pl.BlockSpec((BM, f), lambda b, be, *_: (b, 0)),
        scratch_shapes=[pltpu.VMEM((2, BM * SLAB_ROWS, LANE), F32),
                        pltpu.VMEM((2, d, f2), BF16),
                        pltpu.VMEM((2, _chunk_rows(f2), f2), F32),
                        pltpu.SemaphoreType.DMA((2,)), pltpu.SemaphoreType.DMA((2,))],
    )
    return pl.pallas_call(
        _moe_up_body,
        grid_spec=grid_spec,
        out_shape=jax.ShapeDtypeStruct((p, f), BF16),
        compiler_params=_cparams(("arbitrary",), 58),
        name="moe_up",
    )(sched["blk_e"], sched["n_used"], sched["wslot"], sched["wnext"], sched["wc0"], sched["wc1"],
      sched["nvalid"], idx3, idx3, h2s, wgu, bgu.reshape(ne, 1, f2))


def _moe_down_body(be_ref, nu_ref, ws_ref, wn_ref, wc0_ref, wc1_ref, nv_ref, dst_ref, a_ref, w_hbm, bias_ref,
                   ysc_ref, ybuf, wbf, stage, sem, wsem):
    b = pl.program_id(0)
    nb = pl.num_programs(0)
    nu = nu_ref[0]
    slab = SLAB_ROWS
    nch = wbf.shape[1] // stage.shape[1]

    @pl.when(b == 0)
    def _():
        prime0, finish0 = _weight_stream(w_hbm, wbf, stage, wsem, be_ref[0], ws_ref[0], 0, nch)
        prime0()
        finish0()

    def row_copy(slot, r, d):
        return pltpu.make_async_copy(ybuf.at[slot, pl.ds(pl.multiple_of(r * slab, slab), slab), :],
                                     ysc_ref.at[pl.ds(pl.multiple_of(d * slab, slab), slab), :],
                                     sem.at[slot])

    def drain(step):
        slot = step % 2
        count = nv_ref[step]

        @pl.when(count == BM)
        def _():
            pltpu.make_async_copy(ybuf.at[slot], ysc_ref.at[pl.ds(0, BM * slab), :], sem.at[slot]).wait()

        @pl.when(count < BM)
        def _():
            def body(r, c):
                row_copy(slot, 0, 0).wait()
                return c
            lax.fori_loop(0, count, body, 0)

    @pl.when((b >= 2) & (b < nu))
    def _():
        drain(b - 2)

    @pl.when(b < nu)
    def _():
        slot = b % 2
        wslot = ws_ref[b]
        prime, finish = _weight_stream(w_hbm, wbf, stage, wsem, wn_ref[b], 1 - wslot, wc0_ref[b], wc1_ref[b])
        prime()
        def compute(m):
            y = jnp.dot(a_ref[0:m, :], wbf[wslot], preferred_element_type=F32) + bias_ref[0]
            for s in range(slab):
                ybuf[slot, pl.ds(s, m, stride=slab), :] = y[:, s * LANE:(s + 1) * LANE]

        @pl.when(nv_ref[b] > BM // 2)
        def _():
            compute(BM)

        @pl.when(nv_ref[b] <= BM // 2)
        def _():
            compute(BM // 2)

        finish()

        def body(r, c):
            row_copy(slot, r, dst_ref[0, 0, r]).start()
            return c

        def pair(i, c):
            row_copy(slot, 2 * i, dst_ref[0, 0, 2 * i]).start(priority=0)
            row_copy(slot, 2 * i + 1, dst_ref[0, 0, 2 * i + 1]).start(priority=1)
            return c

        @pl.when(nv_ref[b] == BM)
        def _():
            lax.fori_loop(0, BM // 2, pair, 0, unroll=DMA_UNROLL // 2)

        @pl.when(nv_ref[b] < BM)
        def _():
            lax.fori_loop(0, nv_ref[b], body, 0)

    @pl.when(b == nb - 1)
    def _():
        @pl.when(nu >= 2)
        def _():
            drain(nu - 2)
        drain(nu - 1)


def _moe_down(act, dst_slot, n_out_rows, wd, bd, sched):
    p, f = act.shape
    ne, _, d = wd.shape
    nb = p // BM
    grid_spec = pltpu.PrefetchScalarGridSpec(
        num_scalar_prefetch=7,
        grid=(nb,),
        in_specs=[pl.BlockSpec((1, 1, BM), lambda b, be, *_: (b, 0, 0), memory_space=pltpu.SMEM),
                  pl.BlockSpec((BM, f), lambda b, be, *_: (b, 0)),
                  pl.BlockSpec(memory_space=pl.ANY),
                  pl.BlockSpec((1, 1, d), lambda b, be, *_: (be[b], 0, 0))],
        out_specs=pl.BlockSpec(memory_space=pl.ANY),
        scratch_shapes=[pltpu.VMEM((2, BM * SLAB_ROWS, LANE), F32),
                        pltpu.VMEM((2, f, d), BF16),
                        pltpu.VMEM((2, _chunk_rows(d), d), F32),
                        pltpu.SemaphoreType.DMA((2,)), pltpu.SemaphoreType.DMA((2,))],
    )
    return pl.pallas_call(
        _moe_down_body,
        grid_spec=grid_spec,
        out_shape=jax.ShapeDtypeStruct((n_out_rows * SLAB_ROWS, LANE), F32),
        compiler_params=pltpu.CompilerParams(dimension_semantics=("arbitrary",),
                                             vmem_limit_bytes=48 * 1024 * 1024, has_side_effects=True),
        name="moe_down",
    )(sched["blk_e"], sched["n_used"], sched["wslot"], sched["wnext"], sched["wc0"], sched["wc1"],
      sched["nvalid"], dst_slot.reshape(nb, 1, BM), act, wd, bd.reshape(ne, 1, d))


def _final_body(ysc_ref, rt_ref, x1_ref, ada_ref, g_ref, o_ref):
    tm = x1_ref.shape[0]
    rt = rt_ref[...]
    stride = TOP_K * SLAB_ROWS
    pieces = []
    for s in range(SLAB_ROWS):
        acc = rt[:, TOP_K:TOP_K + 1] * ysc_ref[pl.ds(s, tm, stride=stride), :]
        for k in range(1, TOP_K):
            acc = acc + rt[:, TOP_K + k:TOP_K + k + 1] * ysc_ref[pl.ds(k * SLAB_ROWS + s, tm, stride=stride), :]
        pieces.append(acc)
    y = jnp.concatenate(pieces, axis=1)
    gt_f = ada_ref[0, 5:6, :]
    o_ref[...] = x1_ref[...] + gt_f * (_rms(y) * g_ref[...])


def _final(ysc, rt, x1, ada3, g_post, seq):
    t, d = x1.shape
    tm = 256
    per_b = seq // tm
    return pl.pallas_call(
        _final_body,
        grid=(t // tm,),
        in_specs=[pl.BlockSpec((tm * TOP_K * SLAB_ROWS, LANE), lambda i: (i, 0)),
                  pl.BlockSpec((tm, LANE), lambda i: (i, 0)),
                  pl.BlockSpec((tm, d), lambda i: (i, 0)),
                  pl.BlockSpec((1, 6, d), lambda i: (i // per_b, 0, 0)),
                  pl.BlockSpec((1, d), lambda i: (0, 0))],
        out_specs=pl.BlockSpec((tm, d), lambda i: (i, 0)),
        out_shape=jax.ShapeDtypeStruct((t, d), F32),
        compiler_params=_cparams(("arbitrary",), 48),
        name="final",
    )(ysc, rt, x1, ada3, g_post.reshape(1, d))


def _mixer_ffn_layer(x2, ada3, bsz, seq, g_mix_pre, g_mix_post, g_ffn_pre, g_ffn_post, w_in, lb, g_rec_out,
                     rel_bias, w_branch_rec, w_branch_att, w_o, w_router, b_router, w_gate_up, b_gate_up,
                     w_down, b_down):
    t, d = x2.shape
    d_rec = w_branch_rec.shape[0]
    w_att = w_branch_att.shape[0]
    d_att = 3 * w_att
    widths = dict(q_r=d_rec, i_r=d_rec, zf_f=d_rec, zf_b=d_rec, z_o=d_rec, q_a=d_att, k_a=d_att, v_a=d_att,
                  zg_rec=d, zg_att=d)
    my_order = ("zg_rec", "zg_att", "q_r", "i_r", "zf_f", "zf_b", "z_o", "q_a", "k_a", "v_a")
    col, acc = {}, 0
    for name in my_order:
        col[name] = acc // LANE
        acc += widths[name]
    rot = acc - 2 * d

    proj = _inproj(x2, g_mix_pre, ada3, w_in, seq, rot)

    oi, qtf, qtb, utf, utb, df, db = _hgrn_a(proj, lb, col, t, d_rec)
    rec_o = _hgrn_c(oi, qtf, qtb, utf, utb, df, db, proj, g_rec_out, col, bsz, seq, d_rec)

    nums, stats = [], []
    for g, (window, dil) in enumerate(DIL_GROUPS):
        hs = slice(g * ATT_HEADS_PER_GROUP, (g + 1) * ATT_HEADS_PER_GROUP)
        bias = _band_bias(rel_bias[:, hs], window, dil)
        num, st = _attn_group(proj, bias, col, g, dil, bsz, seq)
        nums.append(num)
        stats.append(st)

    x1, h2s, logits = _merge(rec_o, nums, stats, proj, x2, ada3, g_mix_post, g_ffn_pre,
                            w_branch_rec.astype(BF16), w_branch_att.astype(BF16), w_o.astype(BF16),
                            w_router, b_router, col, seq)

    rt, cnt = _route(logits)
    ne = logits.shape[1]
    counts = cnt[0].astype(jnp.int32)
    top_idx = rt[:, 0:TOP_K].astype(jnp.int32)
    rank = rt[:, 2 * TOP_K:3 * TOP_K].astype(jnp.int32)
    padded = (counts + BM - 1) // BM * BM
    pends = jnp.cumsum(padded)
    pstarts = pends - padded
    experts = jnp.arange(ne, dtype=jnp.int32)
    pstart_sel = jnp.sum(jnp.where(top_idx[..., None] == experts, pstarts, 0), axis=-1)
    dest = (pstart_sel + rank).reshape(-1)
    p_rows = t * TOP_K + ne * BM
    nb = p_rows // BM
    blk_start = jnp.arange(nb, dtype=jnp.int32) * BM
    blk_e = jnp.minimum(jnp.sum((pends[None, :] <= blk_start[:, None]).astype(jnp.int32), axis=1), ne - 1)
    n_used = (pends[-1:] // BM).astype(jnp.int32)

    n_assign = t * TOP_K
    slot_assign = jnp.full((p_rows,), -1, jnp.int32).at[dest].set(jnp.arange(n_assign, dtype=jnp.int32))
    valid_end = pstarts + counts
    blk_end = jnp.sum(jnp.where(blk_e[:, None] == experts, valid_end, 0), axis=-1)
    nvalid = jnp.clip(blk_end - blk_start, 0, BM).astype(jnp.int32)
    nvalid = jnp.where(jnp.arange(nb) < n_used[0], nvalid, 0)

    lookup = lambda table: jnp.sum(jnp.where(blk_e[:, None] == experts, table, 0), axis=-1)
    nonempty = padded > 0
    order = jnp.cumsum(nonempty.astype(jnp.int32)) - 1
    later = lax.cummin(jnp.where(nonempty, experts, ne)[::-1])[::-1]
    next_e = jnp.concatenate([later[1:], jnp.full((1,), ne, jnp.int32)])
    blk_next = lookup(next_e)
    has_next = (blk_next < ne) & (jnp.arange(nb) < n_used[0])
    k_in_run = jnp.arange(nb, dtype=jnp.int32) - lookup(pstarts // BM)
    n_in_run = jnp.maximum(lookup(padded // BM), 1)
    common = dict(blk_e=blk_e, n_used=n_used, nvalid=nvalid, wslot=lookup(order) % 2,
                  wnext=jnp.where(has_next, blk_next, blk_e))

    def schedule(w):
        nch = w.shape[1] // _chunk_rows(w.shape[2])
        s = dict(common, wc0=jnp.where(has_next, k_in_run * nch // n_in_run, 0),
                 wc1=jnp.where(has_next, (k_in_run + 1) * nch // n_in_run, 0))
        return {k: v.astype(jnp.int32) for k, v in s.items()}

    src_tok = jnp.maximum(slot_assign, 0) // TOP_K
    act = _moe_up(h2s, src_tok, w_gate_up, b_gate_up, schedule(w_gate_up))
    ysc = _moe_down(act, slot_assign, n_assign, w_down, b_down, schedule(w_down))
    return _final(ysc, rt, x1, ada3, g_ffn_post, seq)


def kernel(x, c, w_ada, b_ada, g_mix_pre, g_mix_post, g_ffn_pre, g_ffn_post, w_in, g_rec_out, w_branch_rec,
           w_branch_att, w_o, w_router, b_router, w_gate_up, b_gate_up, w_down, b_down, rec_lb_table, rel_bias):
    bsz, seq, d = x.shape
    depth = w_in.shape[0]
    lb_all = jnp.cumsum(jax.nn.softmax(rec_lb_table.astype(F32), axis=1), axis=1)
    x2 = x.reshape(bsz * seq, d)
    for layer in range(depth):
        ada3 = _ada(c, w_ada[layer], b_ada[layer]).reshape(bsz, 6, d)
        x2 = _mixer_ffn_layer(x2, ada3, bsz, seq, g_mix_pre[layer], g_mix_post[layer], g_ffn_pre[layer],
                              g_ffn_post[layer], w_in[layer], lb_all[:, layer], g_rec_out[layer], rel_bias,
                              w_branch_rec[layer], w_branch_att[layer], w_o[layer], w_router[layer],
                              b_router[layer], w_gate_up[layer], b_gate_up[layer], w_down[layer],
                              b_down[layer])
    return x2.reshape(bsz, seq, d)
```

```python
import functools
import math

import numpy as np
import jax
import jax.numpy as jnp
from jax import lax
from jax.experimental import pallas as pl
from jax.experimental.pallas import tpu as pltpu

F32 = jnp.float32
BF16 = jnp.bfloat16

LANE = 128
SUBLANE = 8
SLAB_ROWS = 16

REC_HEAD_DIM = 128
REC_CHUNK = 64
ATT_HEAD_DIM = 128
ATT_HEADS_PER_GROUP = 4
ATT_BLOCK = 64
DIL_GROUPS = ((128, 1), (512, 4), (2048, 16))
NUM_BUCKETS = 32
MAX_DISTANCE = 1024
N_EXPERTS = 32
TOP_K = 4
SWIGLU_LIMIT = 7.0
SWIGLU_ALPHA = 1.702
RMS_EPS = 1e-6
NEG_INF = -1e30

N_LEVELS = 6
W_CHUNK_BYTES = 4 * 1024 * 1024
CAST_ROWS = 32
HGRN_UNROLL = 4
DMA_UNROLL = 8
ATT_UNROLL = 8
BM = 256

_NT = (((1,), (1,)), ((), ()))
_TN = (((0,), (0,)), ((), ()))


def _cparams(sem, vmem_mb):
    return pltpu.CompilerParams(dimension_semantics=sem, vmem_limit_bytes=vmem_mb * 1024 * 1024)


def _rms(x):
    return x * lax.rsqrt(jnp.mean(x * x, axis=-1, keepdims=True) + RMS_EPS)


def _ada_body(c_ref, w_ref, b_ref, o_ref):
    c = c_ref[...]
    cond = (c * jax.nn.sigmoid(c)).astype(BF16)
    o_ref[...] = jnp.dot(cond, w_ref[...].astype(BF16), preferred_element_type=F32) + b_ref[...]


def _ada(c, w, b):
    bsz, d = c.shape
    n = w.shape[1]
    tn = 1024
    cp = jnp.zeros((SUBLANE, d), F32).at[:bsz].set(c)
    out = pl.pallas_call(
        _ada_body,
        grid=(n // tn,),
        in_specs=[pl.BlockSpec((SUBLANE, d), lambda j: (0, 0)),
                  pl.BlockSpec((d, tn), lambda j: (0, j)),
                  pl.BlockSpec((1, tn), lambda j: (0, j))],
        out_specs=pl.BlockSpec((SUBLANE, tn), lambda j: (0, j)),
        out_shape=jax.ShapeDtypeStruct((SUBLANE, n), F32),
        compiler_params=_cparams(("arbitrary",), 40),
        name="ada",
    )(cp, w, b.reshape(1, n))
    return out[:bsz]


def _inproj_body(x_ref, g_ref, ada_ref, w_ref, o_ref, h_ref):
    @pl.when(pl.program_id(1) == 0)
    def _():
        half = x_ref.shape[0] // 2
        sh = ada_ref[0, 0:1, :]
        sc = ada_ref[0, 1:2, :]
        for r0 in (0, half):
            y = _rms(x_ref[r0:r0 + half, :]) * g_ref[...]
            h_ref[r0:r0 + half, :] = (y * (1.0 + sc) + sh).astype(BF16)

    o_ref[...] = jnp.dot(h_ref[...], w_ref[...].astype(BF16), preferred_element_type=F32)


def _inproj(x2, g, ada3, w_in, seq, rot):
    t, d = x2.shape
    n = w_in.shape[1]
    tm, tn = 2048, 512
    per_b = seq // tm
    nj = n // tn
    assert rot % tn == 0 and n % tn == 0 and seq % tm == 0
    return pl.pallas_call(
        _inproj_body,
        grid=(t // tm, nj),
        in_specs=[pl.BlockSpec((tm, d), lambda i, j: (i, 0), pipeline_mode=pl.Buffered(1)),
                  pl.BlockSpec((1, d), lambda i, j: (0, 0)),
                  pl.BlockSpec((1, 6, d), lambda i, j: (i // per_b, 0, 0)),
                  pl.BlockSpec((d, tn), lambda i, j: (0, (j + rot // tn) % nj))],
        out_specs=pl.BlockSpec((tm, tn), lambda i, j: (i, j)),
        out_shape=jax.ShapeDtypeStruct((t, n), F32),
        scratch_shapes=[pltpu.VMEM((tm, d), BF16)],
        compiler_params=_cparams(("arbitrary", "arbitrary"), 56),
        name="inproj",
    )(x2, g.reshape(1, d), ada3, w_in)


def _hgrn_consts():
    c = REC_CHUNK
    r = np.arange(c)[:, None]
    m = np.arange(c)[None, :]
    wf = np.zeros((8 * c, c), np.float32)
    wb = np.zeros((8 * c, c), np.float32)
    mf = np.zeros((N_LEVELS + 1, c, c), np.float32)
    for lvl in range(N_LEVELS):
        s = 32 >> lvl
        m0 = (r // (2 * s)) * (2 * s) + s
        up = r >= m0
        wf[lvl * c:(lvl + 1) * c] = np.where(up, (m >= m0) & (m <= r), (m > r) & (m <= m0 - 1))
        wb[lvl * c:(lvl + 1) * c] = np.where(up, (m >= m0) & (m <= r - 1), (m >= r) & (m <= m0 - 1))
        i = np.arange(c)[:, None]
        j = np.arange(c)[None, :]
        mf[lvl] = (i // (2 * s) == j // (2 * s)) & (i % (2 * s) >= s) & (j % (2 * s) < s)
    mf[N_LEVELS] = np.eye(c)
    wf[6 * c:7 * c] = m <= r
    wf[7 * c:8 * c] = m > r
    wb[6 * c:7 * c] = m >= r
    wb[7 * c:8 * c] = m < r
    mfb = mf + np.transpose(mf, (0, 2, 1))
    mfb[N_LEVELS] = np.eye(c)
    up = np.zeros((N_LEVELS, c, LANE), np.float32)
    for lvl in range(N_LEVELS):
        s = 32 >> lvl
        up[lvl] = ((np.arange(c) % (2 * s)) >= s)[:, None]
    wf3 = np.concatenate([wf, wf, wf], axis=1)
    wb3 = np.concatenate([wb, wb, wb], axis=1)
    return (jnp.asarray(wf3, BF16), jnp.asarray(wb3, BF16), jnp.asarray(mfb, F32), jnp.asarray(up, F32),
            jnp.asarray(1.0 - up, F32))


def _split3(g):
    hi = g.astype(BF16)
    r1 = g - hi.astype(F32)
    mid = r1.astype(BF16)
    lo = (r1 - mid.astype(F32)).astype(BF16)
    return jnp.concatenate([hi, mid, lo], axis=0)


def _hgrn_a_body(q_ref, i_ref, zf_ref, zb_ref, lb_ref, wf_ref, wb_ref, mf_ref, up_ref, lo_ref,
                 oi_ref, qtf_ref, qtb_ref, utf_ref, utb_ref, df_ref, db_ref, *, cpb):
    c = REC_CHUNK
    dirs = ((zf_ref, wf_ref, None, qtf_ref, utf_ref, df_ref, 0, c - 1),
            (zb_ref, wb_ref, None, qtb_ref, utb_ref, db_ref, 1, 0))

    def chunk_group(cg, carry):
        cis = [cg * HGRN_UNROLL + u for u in range(HGRN_UNROLL)]
        rows = [pl.ds(pl.multiple_of(ci * c, c), c) for ci in cis]
        zqs = [q_ref[rw, :] for rw in rows]
        qs = [zq * jax.nn.sigmoid(zq) for zq in zqs]
        vbs = [i_ref[rw, :].astype(BF16) for rw in rows]
        units = [(u, d) for u in range(HGRN_UNROLL) for d in range(2)]
        ks, es = {}, {}
        for u, d in units:
            z_ref, w_ref = dirs[d][0], dirs[d][1]
            lb = lb_ref[d:d + 1, :]
            f = lb + (1.0 - lb) * jax.nn.sigmoid(z_ref[rows[u], :])
            ks[u, d] = 1.0 - f
            es[u, d] = jnp.exp(jnp.dot(w_ref[...], _split3(jnp.log(f)), preferred_element_type=F32))
        acc = [jnp.zeros((c, c), F32) for _ in range(HGRN_UNROLL)]
        for lvl in range(N_LEVELS + 1):
            for u in range(HGRN_UNROLL):
                if lvl < N_LEVELS:
                    ef = es[u, 0][lvl * c:(lvl + 1) * c]
                    eb = es[u, 1][lvl * c:(lvl + 1) * c]
                    up, lo = up_ref[lvl], lo_ref[lvl]
                    qa = jnp.concatenate([(qs[u] * (ef * up)).astype(BF16), (qs[u] * (eb * lo)).astype(BF16)],
                                         axis=1)
                    ka = jnp.concatenate([(ks[u, 0] * (ef * lo)).astype(BF16),
                                          (ks[u, 1] * (eb * up)).astype(BF16)], axis=1)
                else:
                    qa, ka = qs[u].astype(BF16), (ks[u, 0] + ks[u, 1]).astype(BF16)
                p = lax.dot_general(qa, ka, _NT, preferred_element_type=F32)
                acc[u] = acc[u] + p * mf_ref[lvl]
        for u, d in units:
            _, _, _, qt_ref, ut_ref, d_ref, _, drow = dirs[d]
            e = es[u, d]
            qt_ref[rows[u], :] = (qs[u] * e[6 * c:7 * c]).astype(BF16)
            kt = (ks[u, d] * e[7 * c:8 * c]).astype(BF16)
            ut_ref[cis[u]] = lax.dot_general(vbs[u], kt, _TN, preferred_element_type=F32)
            d_ref[pl.ds(cis[u], 1), :] = e[6 * c + drow:6 * c + drow + 1]
        for u in range(HGRN_UNROLL):
            oi_ref[rows[u], :] = jnp.dot(acc[u].astype(BF16), vbs[u], preferred_element_type=F32)
        return carry

    lax.fori_loop(0, cpb // HGRN_UNROLL, chunk_group, 0)


def _hgrn_a(proj, lb, col, t, d_rec):
    heads = d_rec // REC_HEAD_DIM
    tq = 1024
    cpb = tq // REC_CHUNK
    nchunks = t // REC_CHUNK
    wf, wb, mf, up, lo = _hgrn_consts()
    hd = REC_HEAD_DIM

    def colspec(off):
        return pl.BlockSpec((tq, hd), lambda i, h: (i, off + h))

    full2 = lambda i, h: (0, 0)
    full3 = lambda i, h: (0, 0, 0)
    row_spec = pl.BlockSpec((tq, hd), lambda i, h: (i, h))
    u_spec = pl.BlockSpec((cpb, hd, hd), lambda i, h: (i, 0, h))
    d_spec = pl.BlockSpec((cpb, hd), lambda i, h: (i, h))
    return pl.pallas_call(
        functools.partial(_hgrn_a_body, cpb=cpb),
        grid=(t // tq, heads),
        in_specs=[colspec(col["q_r"]), colspec(col["i_r"]), colspec(col["zf_f"]), colspec(col["zf_b"]),
                  pl.BlockSpec((2, hd), lambda i, h: (0, h)),
                  pl.BlockSpec(wf.shape, full2), pl.BlockSpec(wb.shape, full2),
                  pl.BlockSpec(mf.shape, full3), pl.BlockSpec(up.shape, full3), pl.BlockSpec(lo.shape, full3)],
        out_specs=[row_spec, row_spec, row_spec, u_spec, u_spec, d_spec, d_spec],
        out_shape=[jax.ShapeDtypeStruct((t, d_rec), F32),
                   jax.ShapeDtypeStruct((t, d_rec), BF16),
                   jax.ShapeDtypeStruct((t, d_rec), BF16),
                   jax.ShapeDtypeStruct((nchunks, hd, d_rec), F32),
                   jax.ShapeDtypeStruct((nchunks, hd, d_rec), F32),
                   jax.ShapeDtypeStruct((nchunks, d_rec), F32),
                   jax.ShapeDtypeStruct((nchunks, d_rec), F32)],
        compiler_params=_cparams(("arbitrary", "arbitrary"), 32),
        name="hgrn_a",
    )(proj, proj, proj, proj, lb, wf, wb, mf, up, lo)


def _hgrn_c_body(oi_ref, qtf_ref, qtb_ref, utf_ref, utb_ref, df_ref, db_ref, z_ref, g_ref,
                 out_ref, acc_ref, accb_ref, *, nchunks):
    c = REC_CHUNK
    hd = REC_HEAD_DIM

    unroll = 4

    def step(i, carry):
        st_f, st_b = carry
        pending = []
        for u in range(unroll):
            nf = i * unroll + u
            nb = nchunks - 1 - nf
            rows_f = pl.ds(pl.multiple_of(nf * c, c), c)
            rows_b = pl.ds(pl.multiple_of(nb * c, c), c)
            of = lax.dot_general(qtf_ref[rows_f, :], st_f.astype(BF16), _NT, preferred_element_type=F32)
            ob = lax.dot_general(qtb_ref[rows_b, :], st_b.astype(BF16), _NT, preferred_element_type=F32)
            pending.append((rows_f, rows_b, oi_ref[rows_f, :] + of, ob))
            st_f = df_ref[pl.ds(nf, 1), :] * st_f + utf_ref[nf]
            st_b = db_ref[pl.ds(nb, 1), :] * st_b + utb_ref[nb]
        for rows_f, rows_b, vf, vb in pending:
            acc_ref[rows_f, :] = vf
            accb_ref[rows_b, :] = vb
        return st_f, st_b

    zero = jnp.zeros((hd, hd), F32)
    lax.fori_loop(0, nchunks // unroll, step, (zero, zero))

    o = _rms(acc_ref[...] + accb_ref[...])
    out_ref[...] = (o * g_ref[...] * jax.nn.sigmoid(z_ref[...])).astype(BF16)


def _hgrn_c(oi, qtf, qtb, utf, utb, df, db, proj, g_out, col, bsz, seq, d_rec):
    heads = d_rec // REC_HEAD_DIM
    hd = REC_HEAD_DIM
    nchunks = seq // REC_CHUNK
    row_spec = pl.BlockSpec((seq, hd), lambda b, h: (b, h))
    u_spec = pl.BlockSpec((nchunks, hd, hd), lambda b, h: (b, 0, h))
    d_spec = pl.BlockSpec((nchunks, hd), lambda b, h: (b, h))
    zo = col["z_o"]
    return pl.pallas_call(
        functools.partial(_hgrn_c_body, nchunks=nchunks),
        grid=(bsz, heads),
        in_specs=[row_spec, row_spec, row_spec, u_spec, u_spec, d_spec, d_spec,
                  pl.BlockSpec((seq, hd), lambda b, h: (b, zo + h)),
                  pl.BlockSpec((1, hd), lambda b, h: (0, h))],
        out_specs=row_spec,
        out_shape=jax.ShapeDtypeStruct((bsz * seq, d_rec), BF16),
        scratch_shapes=[pltpu.VMEM((seq, hd), F32), pltpu.VMEM((seq, hd), F32)],
        compiler_params=_cparams(("arbitrary", "arbitrary"), 48),
        name="hgrn_c",
    )(oi, qtf, qtb, utf, utb, df, db, proj, g_out.reshape(1, d_rec))


def _t5_bucket(rel):
    half_buckets = NUM_BUCKETS // 2
    ret = np.where(rel > 0, half_buckets, 0)
    n = np.abs(rel)
    max_exact = half_buckets // 2
    nf = np.maximum(n, 1).astype(np.float32)
    large = max_exact + (np.log(nf / np.float32(max_exact)) / np.float32(math.log(MAX_DISTANCE / max_exact))
                         * np.float32(half_buckets - max_exact)).astype(np.int32)
    large = np.minimum(large, half_buckets - 1)
    return ret + np.where(n < max_exact, n, large)


def _band_bias(rel_bias_g, window, dil):
    half = window // (2 * dil)
    q_off = np.arange(ATT_BLOCK)[:, None]
    rel = np.arange(3 * ATT_BLOCK)[None, :] - ATT_BLOCK - q_off
    onehot = (_t5_bucket(rel * dil)[..., None] == np.arange(NUM_BUCKETS)).astype(np.float32)
    bias = jnp.einsum("qkb,bh->hqk", jnp.asarray(onehot), rel_bias_g.astype(F32),
                      precision=lax.Precision.HIGHEST)
    return jnp.where(jnp.asarray(np.abs(rel) <= half)[None], bias, NEG_INF)


def _attn_body(q_ref, kp_ref, k_ref, kn_ref, vp_ref, v_ref, vn_ref, bias_ref,
               num_ref, st_ref, kc_ref, vc_ref, *, dil, tq, sub_len):
    blk = ATT_BLOCK
    nqb = tq // blk
    n = pl.program_id(1)
    scale = ATT_HEAD_DIM ** -0.5

    def sds(start, size):
        if dil == 1:
            return pl.ds(start, size)
        return pl.ds(start, size, stride=dil)

    cu = kc_ref.shape[0]
    qu = ATT_UNROLL // cu

    def deinterleave(r, j):
        kc_ref[j, 0:blk, :] = kp_ref[sds(r, blk), :].astype(BF16)
        kc_ref[j, blk:blk + tq, :] = k_ref[sds(r, tq), :].astype(BF16)
        kc_ref[j, blk + tq:2 * blk + tq, :] = kn_ref[sds(r, blk), :].astype(BF16)
        vc_ref[j, 0:blk, :] = vp_ref[sds(r, blk), :].astype(BF16)
        vc_ref[j, blk:blk + tq, :] = v_ref[sds(r, tq), :].astype(BF16)
        vc_ref[j, blk + tq:2 * blk + tq, :] = vn_ref[sds(r, blk), :].astype(BF16)

    def units(r0, qb0):
        us = [(j, u) for j in range(cu) for u in range(qu)]
        q0s = [pl.multiple_of((qb0 + u) * blk, blk) for _, u in us]
        rows = [sds(r0 + j + dil * q0, blk) for (j, _), q0 in zip(us, q0s)]
        lane = lax.broadcasted_iota(jnp.int32, (blk, LANE), 1)
        key_iota = lax.broadcasted_iota(jnp.int32, (1, 3 * blk), 1)
        bias = bias_ref[0]
        qs = [q_ref[rw, :].astype(BF16) for rw in rows]
        kws = [kc_ref[j, pl.ds(q0, 3 * blk), :] for (j, _), q0 in zip(us, q0s)]
        vws = [vc_ref[j, pl.ds(q0, 3 * blk), :] for (j, _), q0 in zip(us, q0s)]
        ss = [lax.dot_general(q, kw, _NT, preferred_element_type=F32) * scale for q, kw in zip(qs, kws)]
        valids = []
        for q0 in q0s:
            kpos = n * tq + q0 - blk + key_iota
            valids.append((kpos >= 0) & (kpos < sub_len))
        ss = [jnp.where(valid, s + bias, NEG_INF) for s, valid in zip(ss, valids)]
        ms = [jnp.max(s, axis=-1, keepdims=True) for s in ss]
        ps = [jnp.exp(s - m) for s, m in zip(ss, ms)]
        ls = [jnp.sum(p, axis=-1, keepdims=True) for p in ps]
        nums = [jnp.dot(p.astype(BF16), vw, preferred_element_type=F32) for p, vw in zip(ps, vws)]
        for rw, num, m, l in zip(rows, nums, ms, ls):
            num_ref[rw, :] = num
            st_ref[rw, :] = jnp.where(lane < LANE // 2, m, l)

    def class_group(rg, carry):
        r0 = rg * cu
        for j in range(cu):
            deinterleave(r0 + j, j)

        def qgroup(qg, carry2):
            units(r0, qg * qu)
            return carry2

        lax.fori_loop(0, nqb // qu, qgroup, 0)
        return carry

    lax.fori_loop(0, dil // cu, class_group, 0)


def _attn_group(proj, bias, col, g, dil, bsz, seq):
    tile = 1024
    tq = tile // dil
    halo = ATT_BLOCK * dil
    sub_len = seq // dil
    cu = ATT_UNROLL // min(tq // ATT_BLOCK, ATT_UNROLL)
    nh = ATT_HEADS_PER_GROUP
    hd = ATT_HEAD_DIM
    qc = col["q_a"] + g * nh
    kc = col["k_a"] + g * nh
    vc = col["v_a"] + g * nh
    tiles_b = seq // tile
    halos_b = seq // halo
    hpt = tile // halo

    own = lambda c: pl.BlockSpec((tile, hd), lambda b, n, h: (b * tiles_b + n, c + h))
    prev = lambda c: pl.BlockSpec(
        (halo, hd), lambda b, n, h: (b * halos_b + jnp.maximum(n * hpt - 1, 0), c + h))
    nxt = lambda c: pl.BlockSpec(
        (halo, hd), lambda b, n, h: (b * halos_b + jnp.minimum((n + 1) * hpt, halos_b - 1), c + h))
    t = bsz * seq
    return pl.pallas_call(
        functools.partial(_attn_body, dil=dil, tq=tq, sub_len=sub_len),
        grid=(bsz, tiles_b, nh),
        in_specs=[own(qc), prev(kc), own(kc), nxt(kc), prev(vc), own(vc), nxt(vc),
                  pl.BlockSpec((1,) + bias.shape[1:], lambda b, n, h: (h, 0, 0))],
        out_specs=[pl.BlockSpec((tile, hd), lambda b, n, h: (b * tiles_b + n, h)),
                   pl.BlockSpec((tile, LANE), lambda b, n, h: (b * tiles_b + n, h))],
        out_shape=[jax.ShapeDtypeStruct((t, nh * hd), F32), jax.ShapeDtypeStruct((t, nh * LANE), F32)],
        scratch_shapes=[pltpu.VMEM((cu, tq + 2 * ATT_BLOCK, hd), BF16),
                        pltpu.VMEM((cu, tq + 2 * ATT_BLOCK, hd), BF16)],
        compiler_params=_cparams(("arbitrary", "arbitrary", "arbitrary"), 32),
        name=f"attn_d{dil}",
    )(proj, proj, proj, proj, proj, proj, proj, bias)


def _merge_body(rec_ref, n0_ref, n1_ref, n2_ref, s0_ref, s1_ref, s2_ref, zgr_ref, zga_ref, x_ref,
                ada_ref, gpost_ref, gpre_ref, wbr_ref, wba_ref, wo_ref, wr_ref, br_ref,
                x1_ref, h2_ref, lg_ref):
    nh = ATT_HEADS_PER_GROUP
    hd = ATT_HEAD_DIM
    half = LANE // 2
    lane = lax.broadcasted_iota(jnp.int32, (rec_ref.shape[0], LANE), 1)
    heads = []
    for h in range(nh):
        cols = slice(h * hd, (h + 1) * hd)
        st = [s[:, cols] for s in (s0_ref, s1_ref, s2_ref)]
        top = jnp.maximum(jnp.maximum(st[0], st[1]), st[2])
        ws = [jnp.exp(s - top) for s in st]
        den = (ws[0] * pltpu.roll(st[0], half, 1) + ws[1] * pltpu.roll(st[1], half, 1)
               + ws[2] * pltpu.roll(st[2], half, 1))
        coef = [w / den for w in ws]
        coef = [jnp.where(lane < half, c, pltpu.roll(c, half, 1)) for c in coef]
        num = coef[0] * n0_ref[:, cols] + coef[1] * n1_ref[:, cols] + coef[2] * n2_ref[:, cols]
        heads.append(num.astype(BF16))
    att = jnp.concatenate(heads, axis=1)
    y_rec = jnp.dot(rec_ref[...], wbr_ref[...], preferred_element_type=F32)
    y_att = jnp.dot(att, wba_ref[...], preferred_element_type=F32)
    merged = jax.nn.sigmoid(zgr_ref[...]) * y_rec + jax.nn.sigmoid(zga_ref[...]) * y_att
    y = jnp.dot(merged.astype(BF16), wo_ref[...], preferred_element_type=F32)
    gt_m = ada_ref[0, 2:3, :]
    sh_f = ada_ref[0, 3:4, :]
    sc_f = ada_ref[0, 4:5, :]
    x1 = x_ref[...] + gt_m * (_rms(y) * gpost_ref[...])
    x1_ref[...] = x1
    h2 = _rms(x1) * gpre_ref[...] * (1.0 + sc_f) + sh_f
    tm = h2.shape[0]
    for s in range(SLAB_ROWS):
        h2_ref[pl.ds(s, tm, stride=SLAB_ROWS), :] = h2[:, s * LANE:(s + 1) * LANE]
    ne = lg_ref.shape[1]
    h_hi = h2.astype(BF16)
    h_lo = (h2 - h_hi.astype(F32)).astype(BF16)
    both = jnp.dot(h_hi, wr_ref[...], preferred_element_type=F32)
    cross = jnp.dot(h_lo, wr_ref[:, 0:ne], preferred_element_type=F32)
    lg_ref[...] = both[:, 0:ne] + both[:, ne:2 * ne] + cross + br_ref[...]


def _merge(rec_o, nums, stats, proj, x2, ada3, g_post, g_pre, wbr, wba, wo, w_router, b_router, col, seq):
    t, d = x2.shape
    tm = 256
    per_b = seq // tm
    d_rec = rec_o.shape[1]
    w_att = nums[0].shape[1]
    ne = w_router.shape[1]
    wr_hi = w_router.astype(BF16)
    wr_lo = (w_router - wr_hi.astype(F32)).astype(BF16)
    w_router = jnp.concatenate([wr_hi, wr_lo], axis=1)
    dl = d // LANE
    row = lambda w: pl.BlockSpec((tm, w), lambda i: (i, 0))
    const = lambda shape: pl.BlockSpec(shape, lambda i: (0,) * len(shape), pipeline_mode=pl.Buffered(1))
    zgr = col["zg_rec"] // dl
    zga = col["zg_att"] // dl
    return pl.pallas_call(
        _merge_body,
        grid=(t // tm,),
        in_specs=[row(d_rec), row(w_att), row(w_att), row(w_att), row(w_att), row(w_att), row(w_att),
                  pl.BlockSpec((tm, d), lambda i: (i, zgr)),
                  pl.BlockSpec((tm, d), lambda i: (i, zga)),
                  row(d),
                  pl.BlockSpec((1, 6, d), lambda i: (i // per_b, 0, 0)),
                  const((1, d)), const((1, d)),
                  const(wbr.shape), const(wba.shape), const(wo.shape), const(w_router.shape),
                  const((1, ne))],
        out_specs=[row(d), pl.BlockSpec((tm * SLAB_ROWS, LANE), lambda i: (i, 0)),
                   pl.BlockSpec((tm, ne), lambda i: (i, 0))],
        out_shape=[jax.ShapeDtypeStruct((t, d), F32), jax.ShapeDtypeStruct((t * SLAB_ROWS, LANE), F32),
                   jax.ShapeDtypeStruct((t, ne), F32)],
        compiler_params=_cparams(("arbitrary",), 56),
        name="merge",
    )(rec_o, nums[0], nums[1], nums[2], stats[0], stats[1], stats[2], proj, proj, x2, ada3,
      g_post.reshape(1, d), g_pre.reshape(1, d), wbr, wba, wo, w_router, b_router.reshape(1, ne))


def _route_body(lg_ref, tri_ref, rt_ref, cnt_ref, carry_ref):
    i = pl.program_id(0)
    tr, ne = lg_ref.shape

    @pl.when(i == 0)
    def _():
        carry_ref[...] = jnp.zeros_like(carry_ref)

    l = lg_ref[...]
    lane = lax.broadcasted_iota(jnp.int32, (tr, ne), 1).astype(F32)
    vals, sels, idxs = [], [], []
    for _ in range(TOP_K):
        m = jnp.max(l, axis=-1, keepdims=True)
        idx = jnp.min(jnp.where(l == m, lane, float(ne)), axis=-1, keepdims=True)
        sel = lane == idx
        vals.append(m)
        idxs.append(idx)
        sels.append(sel)
        l = jnp.where(sel, -jnp.inf, l)
    es = [jnp.exp(v - vals[0]) for v in vals]
    tot = es[0] + es[1] + es[2] + es[3]
    chosen = (sels[0] | sels[1] | sels[2] | sels[3]).astype(F32)
    prefix = jnp.dot(tri_ref[...], chosen.astype(BF16), preferred_element_type=F32) + carry_ref[0:1, :]
    out_lane = lax.broadcasted_iota(jnp.int32, (tr, LANE), 1)
    rt = jnp.zeros((tr, LANE), F32)
    for k in range(TOP_K):
        rank = jnp.sum(jnp.where(sels[k], prefix, 0.0), axis=-1, keepdims=True)
        rt = jnp.where(out_lane == k, idxs[k], rt)
        rt = jnp.where(out_lane == TOP_K + k, es[k] / tot, rt)
        rt = jnp.where(out_lane == 2 * TOP_K + k, rank, rt)
    rt_ref[...] = rt
    new = carry_ref[0:1, :] + jnp.sum(chosen, axis=0, keepdims=True)
    carry_ref[...] = jnp.broadcast_to(new, carry_ref.shape)
    cnt_ref[...] = carry_ref[...]


def _route(logits):
    t, ne = logits.shape
    tr = 512
    tri = jnp.asarray(np.tril(np.ones((tr, tr), np.float32), -1), BF16)
    return pl.pallas_call(
        _route_body,
        grid=(t // tr,),
        in_specs=[pl.BlockSpec((tr, ne), lambda i: (i, 0)),
                  pl.BlockSpec((tr, tr), lambda i: (0, 0))],
        out_specs=[pl.BlockSpec((tr, LANE), lambda i: (i, 0)),
                   pl.BlockSpec((SUBLANE, ne), lambda i: (0, 0))],
        out_shape=[jax.ShapeDtypeStruct((t, LANE), F32), jax.ShapeDtypeStruct((SUBLANE, ne), F32)],
        scratch_shapes=[pltpu.VMEM((SUBLANE, ne), F32)],
        compiler_params=_cparams(("arbitrary",), 32),
        name="route",
    )(logits, tri)


def _chunk_rows(ncols):
    return W_CHUNK_BYTES // (4 * ncols)


def _weight_stream(w_hbm, wbf, stage, wsem, e, slot, c0, c1, priority=0):
    kc = stage.shape[1]

    def copy(c):
        return pltpu.make_async_copy(w_hbm.at[e, pl.ds(pl.multiple_of(c * kc, kc), kc), :],
                                     stage.at[c % 2], wsem.at[c % 2])

    def prime():
        def body(c, carry):
            copy(c).start(priority=priority)
            return carry
        lax.fori_loop(c0, jnp.minimum(c0 + 2, c1), body, 0)

    def finish():
        def body(c, carry):
            copy(c).wait()
            buf = c % 2

            def cast(i, carry2):
                r = pl.multiple_of(i * CAST_ROWS, CAST_ROWS)
                wbf[slot, pl.ds(pl.multiple_of(c * kc, kc) + r, CAST_ROWS), :] = (
                    stage[buf, pl.ds(r, CAST_ROWS), :].astype(BF16))
                return carry2
            lax.fori_loop(0, kc // CAST_ROWS, cast, 0)

            @pl.when(c + 2 < c1)
            def _():
                copy(c + 2).start(priority=priority)
            return carry
        lax.fori_loop(c0, c1, body, 0)

    return prime, finish


def _moe_up_body(be_ref, nu_ref, ws_ref, wn_ref, wc0_ref, wc1_ref, nv_ref, idx0_ref, idxn_ref, h2s_ref, w_hbm,
                 bias_ref, o_ref, xbuf, wbf, stage, sem, wsem):
    b = pl.program_id(0)
    nu = nu_ref[0]
    f = o_ref.shape[1]
    slab = SLAB_ROWS
    nch = wbf.shape[1] // stage.shape[1]

    def row_start(idx_ref, slot, r):
        tok = idx_ref[0, 0, r]
        pltpu.make_async_copy(h2s_ref.at[pl.ds(pl.multiple_of(tok * slab, slab), slab), :],
                              xbuf.at[slot, pl.ds(pl.multiple_of(r * slab, slab), slab), :],
                              sem.at[slot]).start()

    def wait_rows(slot):
        pltpu.make_async_copy(h2s_ref.at[pl.ds(0, BM * slab), :], xbuf.at[slot], sem.at[slot]).wait()

    def issue(idx_ref, slot):
        def body(r, c):
            row_start(idx_ref, slot, r)
            return c
        lax.fori_loop(0, BM, body, 0, unroll=DMA_UNROLL)

    @pl.when(b == 0)
    def _():
        issue(idx0_ref, 0)
        prime0, finish0 = _weight_stream(w_hbm, wbf, stage, wsem, be_ref[0], ws_ref[0], 0, nch)
        prime0()
        finish0()

    @pl.when(b + 1 < nu)
    def _():
        issue(idxn_ref, (b + 1) % 2)

    @pl.when(b < nu)
    def _():
        wslot = ws_ref[b]
        prime, finish = _weight_stream(w_hbm, wbf, stage, wsem, wn_ref[b], 1 - wslot, wc0_ref[b], wc1_ref[b],
                                       priority=1)
        prime()
        slot = b % 2
        wait_rows(slot)

        def compute(m):
            x = jnp.concatenate([xbuf[slot, pl.ds(s, m, stride=slab), :].astype(BF16) for s in range(slab)],
                                axis=1)
            half = f // 2
            for c0 in (0, half):
                gate = (jnp.dot(x, wbf[wslot, :, c0:c0 + half], preferred_element_type=F32)
                        + bias_ref[0, :, c0:c0 + half])
                up = (jnp.dot(x, wbf[wslot, :, f + c0:f + c0 + half], preferred_element_type=F32)
                      + bias_ref[0, :, f + c0:f + c0 + half])
                gate = jnp.minimum(gate, SWIGLU_LIMIT)
                up = jnp.clip(up, -SWIGLU_LIMIT, SWIGLU_LIMIT)
                o_ref[0:m, c0:c0 + half] = (gate * jax.nn.sigmoid(SWIGLU_ALPHA * gate) * (up + 1.0)).astype(BF16)
            if m < BM:
                o_ref[m:BM, :] = jnp.zeros((BM - m, f), BF16)

        @pl.when(nv_ref[b] > BM // 2)
        def _():
            compute(BM)

        @pl.when(nv_ref[b] <= BM // 2)
        def _():
            compute(BM // 2)

        finish()

    @pl.when(b >= nu)
    def _():
        o_ref[...] = jnp.zeros_like(o_ref)


def _moe_up(h2s, src_tok, wgu, bgu, sched):
    ne, d, f2 = wgu.shape
    f = f2 // 2
    p = src_tok.shape[0]
    nb = p // BM
    idx3 = src_tok.reshape(nb, 1, BM)
    smem_blk = lambda imap: pl.BlockSpec((1, 1, BM), imap, memory_space=pltpu.SMEM)
    grid_spec = pltpu.PrefetchScalarGridSpec(
        num_scalar_prefetch=7,
        grid=(nb,),
        in_specs=[smem_blk(lambda b, be, *_: (0, 0, 0)),
                  smem_blk(lambda b, be, *_: (jnp.minimum(b + 1, nb - 1), 0, 0)),
                  pl.BlockSpec(memory_space=pl.ANY),
                  pl.BlockSpec(memory_space=pl.ANY),
                  pl.BlockSpec((1, 1, f2), lambda b, be, *_: (be[b], 0, 0))],
        out_specs=pl.BlockSpec((BM, f), lambda b, be, *_: (b, 0)),
        scratch_shapes=[pltpu.VMEM((2, BM * SLAB_ROWS, LANE), F32),
                        pltpu.VMEM((2, d, f2), BF16),
                        pltpu.VMEM((2, _chunk_rows(f2), f2), F32),
                        pltpu.SemaphoreType.DMA((2,)), pltpu.SemaphoreType.DMA((2,))],
    )
    return pl.pallas_call(
        _moe_up_body,
        grid_spec=grid_spec,
        out_shape=jax.ShapeDtypeStruct((p, f), BF16),
        compiler_params=_cparams(("arbitrary",), 58),
        name="moe_up",
    )(sched["blk_e"], sched["n_used"], sched["wslot"], sched["wnext"], sched["wc0"], sched["wc1"],
      sched["nvalid"], idx3, idx3, h2s, wgu, bgu.reshape(ne, 1, f2))


def _moe_down_body(be_ref, nu_ref, ws_ref, wn_ref, wc0_ref, wc1_ref, nv_ref, dst_ref, a_ref, w_hbm, bias_ref,
                   ysc_ref, ybuf, wbf, stage, sem, wsem):
    b = pl.program_id(0)
    nb = pl.num_programs(0)
    nu = nu_ref[0]
    slab = SLAB_ROWS
    nch = wbf.shape[1] // stage.shape[1]

    @pl.when(b == 0)
    def _():
        prime0, finish0 = _weight_stream(w_hbm, wbf, stage, wsem, be_ref[0], ws_ref[0], 0, nch)
        prime0()
        finish0()

    def row_copy(slot, r, d):
        return pltpu.make_async_copy(ybuf.at[slot, pl.ds(pl.multiple_of(r * slab, slab), slab), :],
                                     ysc_ref.at[pl.ds(pl.multiple_of(d * slab, slab), slab), :],
                                     sem.at[slot])

    def drain(step):
        slot = step % 2
        count = nv_ref[step]

        @pl.when(count == BM)
        def _():
            pltpu.make_async_copy(ybuf.at[slot], ysc_ref.at[pl.ds(0, BM * slab), :], sem.at[slot]).wait()

        @pl.when(count < BM)
        def _():
            def body(r, c):
                row_copy(slot, 0, 0).wait()
                return c
            lax.fori_loop(0, count, body, 0)

    @pl.when((b >= 2) & (b < nu))
    def _():
        drain(b - 2)

    @pl.when(b < nu)
    def _():
        slot = b % 2
        wslot = ws_ref[b]
        prime, finish = _weight_stream(w_hbm, wbf, stage, wsem, wn_ref[b], 1 - wslot, wc0_ref[b], wc1_ref[b])
        prime()
        def compute(m):
            y = jnp.dot(a_ref[0:m, :], wbf[wslot], preferred_element_type=F32) + bias_ref[0]
            for s in range(slab):
                ybuf[slot, pl.ds(s, m, stride=slab), :] = y[:, s * LANE:(s + 1) * LANE]

        @pl.when(nv_ref[b] > BM // 2)
        def _():
            compute(BM)

        @pl.when(nv_ref[b] <= BM // 2)
        def _():
            compute(BM // 2)

        finish()

        def body(r, c):
            row_copy(slot, r, dst_ref[0, 0, r]).start()
            return c

        def pair(i, c):
            row_copy(slot, 2 * i, dst_ref[0, 0, 2 * i]).start(priority=0)
            row_copy(slot, 2 * i + 1, dst_ref[0, 0, 2 * i + 1]).start(priority=1)
            return c

        @pl.when(nv_ref[b] == BM)
        def _():
            lax.fori_loop(0, BM // 2, pair, 0, unroll=DMA_UNROLL // 2)

        @pl.when(nv_ref[b] < BM)
        def _():
            lax.fori_loop(0, nv_ref[b], body, 0)

    @pl.when(b == nb - 1)
    def _():
        @pl.when(nu >= 2)
        def _():
            drain(nu - 2)
        drain(nu - 1)


def _moe_down(act, dst_slot, n_out_rows, wd, bd, sched):
    p, f = act.shape
    ne, _, d = wd.shape
    nb = p // BM
    grid_spec = pltpu.PrefetchScalarGridSpec(
        num_scalar_prefetch=7,
        grid=(nb,),
        in_specs=[pl.BlockSpec((1, 1, BM), lambda b, be, *_: (b, 0, 0), memory_space=pltpu.SMEM),
                  pl.BlockSpec((BM, f), lambda b, be, *_: (b, 0)),
                  pl.BlockSpec(memory_space=pl.ANY),
                  pl.BlockSpec((1, 1, d), lambda b, be, *_: (be[b], 0, 0))],
        out_specs=pl.BlockSpec(memory_space=pl.ANY),
        scratch_shapes=[pltpu.VMEM((2, BM * SLAB_ROWS, LANE), F32),
                        pltpu.VMEM((2, f, d), BF16),
                        pltpu.VMEM((2, _chunk_rows(d), d), F32),
                        pltpu.SemaphoreType.DMA((2,)), pltpu.SemaphoreType.DMA((2,))],
    )
    return pl.pallas_call(
        _moe_down_body,
        grid_spec=grid_spec,
        out_shape=jax.ShapeDtypeStruct((n_out_rows * SLAB_ROWS, LANE), F32),
        compiler_params=pltpu.CompilerParams(dimension_semantics=("arbitrary",),
                                             vmem_limit_bytes=48 * 1024 * 1024, has_side_effects=True),
        name="moe_down",
    )(sched["blk_e"], sched["n_used"], sched["wslot"], sched["wnext"], sched["wc0"], sched["wc1"],
      sched["nvalid"], dst_slot.reshape(nb, 1, BM), act, wd, bd.reshape(ne, 1, d))


def _final_body(y0_ref, y1_ref, y2_ref, y3_ref, rt_ref, x1_ref, ada_ref, g_ref, o_ref):
    tm = x1_ref.shape[0]
    rt = rt_ref[...]
    y_refs = (y0_ref, y1_ref, y2_ref, y3_ref)
    pieces = []
    for s in range(SLAB_ROWS):
        acc = rt[:, TOP_K:TOP_K + 1] * y_refs[0][pl.ds(s, tm, stride=SLAB_ROWS), :]
        for k in range(1, TOP_K):
            acc = acc + rt[:, TOP_K + k:TOP_K + k + 1] * y_refs[k][pl.ds(s, tm, stride=SLAB_ROWS), :]
        pieces.append(acc)
    y = jnp.concatenate(pieces, axis=1)
    gt_f = ada_ref[0, 5:6, :]
    o_ref[...] = x1_ref[...] + gt_f * (_rms(y) * g_ref[...])


def _final(ysc, rt, x1, ada3, g_post, seq):
    t, d = x1.shape
    tm = 256
    per_b = seq // tm
    nt = t // tm
    assert TOP_K == 4
    slot_spec = lambda k: pl.BlockSpec((tm * SLAB_ROWS, LANE), lambda i: (k * nt + i, 0))
    return pl.pallas_call(
        _final_body,
        grid=(nt,),
        in_specs=[slot_spec(0), slot_spec(1), slot_spec(2), slot_spec(3),
                  pl.BlockSpec((tm, LANE), lambda i: (i, 0)),
                  pl.BlockSpec((tm, d), lambda i: (i, 0)),
                  pl.BlockSpec((1, 6, d), lambda i: (i // per_b, 0, 0)),
                  pl.BlockSpec((1, d), lambda i: (0, 0))],
        out_specs=pl.BlockSpec((tm, d), lambda i: (i, 0)),
        out_shape=jax.ShapeDtypeStruct((t, d), F32),
        compiler_params=_cparams(("arbitrary",), 48),
        name="final",
    )(ysc, ysc, ysc, ysc, rt, x1, ada3, g_post.reshape(1, d))


def _mixer_ffn_layer(x2, ada3, bsz, seq, g_mix_pre, g_mix_post, g_ffn_pre, g_ffn_post, w_in, lb, g_rec_out,
                     rel_bias, w_branch_rec, w_branch_att, w_o, w_router, b_router, w_gate_up, b_gate_up,
                     w_down, b_down):
    t, d = x2.shape
    d_rec = w_branch_rec.shape[0]
    w_att = w_branch_att.shape[0]
    d_att = 3 * w_att
    widths = dict(q_r=d_rec, i_r=d_rec, zf_f=d_rec, zf_b=d_rec, z_o=d_rec, q_a=d_att, k_a=d_att, v_a=d_att,
                  zg_rec=d, zg_att=d)
    my_order = ("zg_rec", "zg_att", "q_r", "i_r", "zf_f", "zf_b", "z_o", "q_a", "k_a", "v_a")
    col, acc = {}, 0
    for name in my_order:
        col[name] = acc // LANE
        acc += widths[name]
    rot = acc - 2 * d

    proj = _inproj(x2, g_mix_pre, ada3, w_in, seq, rot)

    oi, qtf, qtb, utf, utb, df, db = _hgrn_a(proj, lb, col, t, d_rec)
    rec_o = _hgrn_c(oi, qtf, qtb, utf, utb, df, db, proj, g_rec_out, col, bsz, seq, d_rec)

    nums, stats = [], []
    for g, (window, dil) in enumerate(DIL_GROUPS):
        hs = slice(g * ATT_HEADS_PER_GROUP, (g + 1) * ATT_HEADS_PER_GROUP)
        bias = _band_bias(rel_bias[:, hs], window, dil)
        num, st = _attn_group(proj, bias, col, g, dil, bsz, seq)
        nums.append(num)
        stats.append(st)

    x1, h2s, logits = _merge(rec_o, nums, stats, proj, x2, ada3, g_mix_post, g_ffn_pre,
                            w_branch_rec.astype(BF16), w_branch_att.astype(BF16), w_o.astype(BF16),
                            w_router, b_router, col, seq)

    rt, cnt = _route(logits)
    ne = logits.shape[1]
    counts = cnt[0].astype(jnp.int32)
    top_idx = rt[:, 0:TOP_K].astype(jnp.int32)
    rank = rt[:, 2 * TOP_K:3 * TOP_K].astype(jnp.int32)
    padded = (counts + BM - 1) // BM * BM
    pends = jnp.cumsum(padded)
    pstarts = pends - padded
    experts = jnp.arange(ne, dtype=jnp.int32)
    pstart_sel = jnp.sum(jnp.where(top_idx[..., None] == experts, pstarts, 0), axis=-1)
    dest = (pstart_sel + rank).T.reshape(-1)
    p_rows = t * TOP_K + ne * BM
    nb = p_rows // BM
    blk_start = jnp.arange(nb, dtype=jnp.int32) * BM
    blk_e = jnp.minimum(jnp.sum((pends[None, :] <= blk_start[:, None]).astype(jnp.int32), axis=1), ne - 1)
    n_used = (pends[-1:] // BM).astype(jnp.int32)

    n_assign = t * TOP_K
    slot_assign = jnp.full((p_rows,), -1, jnp.int32).at[dest].set(jnp.arange(n_assign, dtype=jnp.int32))
    valid_end = pstarts + counts
    blk_end = jnp.sum(jnp.where(blk_e[:, None] == experts, valid_end, 0), axis=-1)
    nvalid = jnp.clip(blk_end - blk_start, 0, BM).astype(jnp.int32)
    nvalid = jnp.where(jnp.arange(nb) < n_used[0], nvalid, 0)

    lookup = lambda table: jnp.sum(jnp.where(blk_e[:, None] == experts, table, 0), axis=-1)
    nonempty = padded > 0
    order = jnp.cumsum(nonempty.astype(jnp.int32)) - 1
    later = lax.cummin(jnp.where(nonempty, experts, ne)[::-1])[::-1]
    next_e = jnp.concatenate([later[1:], jnp.full((1,), ne, jnp.int32)])
    blk_next = lookup(next_e)
    has_next = (blk_next < ne) & (jnp.arange(nb) < n_used[0])
    k_in_run = jnp.arange(nb, dtype=jnp.int32) - lookup(pstarts // BM)
    n_in_run = jnp.maximum(lookup(padded // BM), 1)
    common = dict(blk_e=blk_e, n_used=n_used, nvalid=nvalid, wslot=lookup(order) % 2,
                  wnext=jnp.where(has_next, blk_next, blk_e))

    def schedule(w):
        nch = w.shape[1] // _chunk_rows(w.shape[2])
        s = dict(common, wc0=jnp.where(has_next, k_in_run * nch // n_in_run, 0),
                 wc1=jnp.where(has_next, (k_in_run + 1) * nch // n_in_run, 0))
        return {k: v.astype(jnp.int32) for k, v in s.items()}

    src_tok = jnp.maximum(slot_assign, 0) % t
    act = _moe_up(h2s, src_tok, w_gate_up, b_gate_up, schedule(w_gate_up))
    ysc = _moe_down(act, slot_assign, n_assign, w_down, b_down, schedule(w_down))
    return _final(ysc, rt, x1, ada3, g_ffn_post, seq)


def kernel(x, c, w_ada, b_ada, g_mix_pre, g_mix_post, g_ffn_pre, g_ffn_post, w_in, g_rec_out, w_branch_rec,
           w_branch_att, w_o, w_router, b_router, w_gate_up, b_gate_up, w_down, b_down, rec_lb_table, rel_bias):
    bsz, seq, d = x.shape
    depth = w_in.shape[0]
    lb_all = jnp.cumsum(jax.nn.softmax(rec_lb_table.astype(F32), axis=1), axis=1)
    x2 = x.reshape(bsz * seq, d)
    for layer in range(depth):
        ada3 = _ada(c, w_ada[layer], b_ada[layer]).reshape(bsz, 6, d)
        x2 = _mixer_ffn_layer(x2, ada3, bsz, seq, g_mix_pre[layer], g_mix_post[layer], g_ffn_pre[layer],
                              g_ffn_post[layer], w_in[layer], lb_all[:, layer], g_rec_out[layer], rel_bias,
                              w_branch_rec[layer], w_branch_att[layer], w_o[layer], w_router[layer],
                              b_router[layer], w_gate_up[layer], b_gate_up[layer], w_down[layer],
                              b_down[layer])
    return x2.reshape(bsz, seq, d)
```

```python
import functools
import math

import numpy as np
import jax
import jax.numpy as jnp
from jax import lax
from jax.experimental import pallas as pl
from jax.experimental.pallas import tpu as pltpu

F32 = jnp.float32
BF16 = jnp.bfloat16

LANE = 128
SUBLANE = 8
SLAB_ROWS = 16
SLAB_PITCH = 24

REC_HEAD_DIM = 128
REC_CHUNK = 64
ATT_HEAD_DIM = 128
ATT_HEADS_PER_GROUP = 4
ATT_BLOCK = 64
DIL_GROUPS = ((128, 1), (512, 4), (2048, 16))
NUM_BUCKETS = 32
MAX_DISTANCE = 1024
N_EXPERTS = 32
TOP_K = 4
SWIGLU_LIMIT = 7.0
SWIGLU_ALPHA = 1.702
RMS_EPS = 1e-6
NEG_INF = -1e30

N_LEVELS = 6
W_CHUNK_BYTES = 4 * 1024 * 1024
CAST_ROWS = 32
HGRN_UNROLL = 4
DMA_UNROLL = 8
ATT_UNROLL = 8
BM = 256

_NT = (((1,), (1,)), ((), ()))
_TN = (((0,), (0,)), ((), ()))


def _cparams(sem, vmem_mb):
    return pltpu.CompilerParams(dimension_semantics=sem, vmem_limit_bytes=vmem_mb * 1024 * 1024)


def _rms(x):
    return x * lax.rsqrt(jnp.mean(x * x, axis=-1, keepdims=True) + RMS_EPS)


def _ada_body(c_ref, w_ref, b_ref, o_ref):
    c = c_ref[...]
    cond = (c * jax.nn.sigmoid(c)).astype(BF16)
    o_ref[...] = jnp.dot(cond, w_ref[...].astype(BF16), preferred_element_type=F32) + b_ref[...]


def _ada(c, w, b):
    bsz, d = c.shape
    n = w.shape[1]
    tn = 1024
    cp = jnp.zeros((SUBLANE, d), F32).at[:bsz].set(c)
    out = pl.pallas_call(
        _ada_body,
        grid=(n // tn,),
        in_specs=[pl.BlockSpec((SUBLANE, d), lambda j: (0, 0)),
                  pl.BlockSpec((d, tn), lambda j: (0, j)),
                  pl.BlockSpec((1, tn), lambda j: (0, j))],
        out_specs=pl.BlockSpec((SUBLANE, tn), lambda j: (0, j)),
        out_shape=jax.ShapeDtypeStruct((SUBLANE, n), F32),
        compiler_params=_cparams(("arbitrary",), 40),
        name="ada",
    )(cp, w, b.reshape(1, n))
    return out[:bsz]


def _inproj_body(x_ref, g_ref, ada_ref, w_ref, o_ref, h_ref):
    @pl.when(pl.program_id(1) == 0)
    def _():
        half = x_ref.shape[0] // 2
        sh = ada_ref[0, 0:1, :]
        sc = ada_ref[0, 1:2, :]
        for r0 in (0, half):
            y = _rms(x_ref[r0:r0 + half, :]) * g_ref[...]
            h_ref[r0:r0 + half, :] = (y * (1.0 + sc) + sh).astype(BF16)

    o_ref[...] = jnp.dot(h_ref[...], w_ref[...].astype(BF16), preferred_element_type=F32)


def _inproj(x2, g, ada3, w_in, seq, rot):
    t, d = x2.shape
    n = w_in.shape[1]
    tm, tn = 2048, 512
    per_b = seq // tm
    nj = n // tn
    assert rot % tn == 0 and n % tn == 0 and seq % tm == 0
    return pl.pallas_call(
        _inproj_body,
        grid=(t // tm, nj),
        in_specs=[pl.BlockSpec((tm, d), lambda i, j: (i, 0), pipeline_mode=pl.Buffered(1)),
                  pl.BlockSpec((1, d), lambda i, j: (0, 0)),
                  pl.BlockSpec((1, 6, d), lambda i, j: (i // per_b, 0, 0)),
                  pl.BlockSpec((d, tn), lambda i, j: (0, (j + rot // tn) % nj))],
        out_specs=pl.BlockSpec((tm, tn), lambda i, j: (i, j)),
        out_shape=jax.ShapeDtypeStruct((t, n), F32),
        scratch_shapes=[pltpu.VMEM((tm, d), BF16)],
        compiler_params=_cparams(("arbitrary", "arbitrary"), 56),
        name="inproj",
    )(x2, g.reshape(1, d), ada3, w_in)


def _hgrn_consts():
    c = REC_CHUNK
    r = np.arange(c)[:, None]
    m = np.arange(c)[None, :]
    nw = N_LEVELS - 1
    wf = np.zeros(((nw + 2) * c, c), np.float32)
    wb = np.zeros(((nw + 2) * c, c), np.float32)
    mf = np.zeros((N_LEVELS + 1, c, c), np.float32)
    for lvl in range(N_LEVELS):
        s = 32 >> lvl
        m0 = (r // (2 * s)) * (2 * s) + s
        up = r >= m0
        if lvl < nw:
            wf[lvl * c:(lvl + 1) * c] = np.where(up, (m >= m0) & (m <= r), (m > r) & (m <= m0 - 1))
            wb[lvl * c:(lvl + 1) * c] = np.where(up, (m >= m0) & (m <= r - 1), (m >= r) & (m <= m0 - 1))
        i = np.arange(c)[:, None]
        j = np.arange(c)[None, :]
        mf[lvl] = (i // (2 * s) == j // (2 * s)) & (i % (2 * s) >= s) & (j % (2 * s) < s)
    mf[N_LEVELS] = np.eye(c)
    wf[nw * c:(nw + 1) * c] = m <= r
    wf[(nw + 1) * c:(nw + 2) * c] = m > r
    wb[nw * c:(nw + 1) * c] = m >= r
    wb[(nw + 1) * c:(nw + 2) * c] = m < r
    mfb = mf + np.transpose(mf, (0, 2, 1))
    mfb[N_LEVELS] = np.eye(c)
    up = np.zeros((N_LEVELS, c, LANE), np.float32)
    for lvl in range(N_LEVELS):
        s = 32 >> lvl
        up[lvl] = ((np.arange(c) % (2 * s)) >= s)[:, None]
    wf3 = np.concatenate([wf, wf, wf], axis=1)
    wb3 = np.concatenate([wb, wb, wb], axis=1)
    return (jnp.asarray(wf3, BF16), jnp.asarray(wb3, BF16), jnp.asarray(mfb, F32), jnp.asarray(up, F32),
            jnp.asarray(1.0 - up, F32))


def _split3(g):
    hi = g.astype(BF16)
    r1 = g - hi.astype(F32)
    mid = r1.astype(BF16)
    lo = (r1 - mid.astype(F32)).astype(BF16)
    return jnp.concatenate([hi, mid, lo], axis=0)


def _hgrn_a_body(q_ref, i_ref, zf_ref, zb_ref, lb_ref, wf_ref, wb_ref, mf_ref, up_ref, lo_ref,
                 oi_ref, qtf_ref, qtb_ref, utf_ref, utb_ref, df_ref, db_ref, *, cpb):
    c = REC_CHUNK
    dirs = ((zf_ref, wf_ref, None, qtf_ref, utf_ref, df_ref, 0, c - 1),
            (zb_ref, wb_ref, None, qtb_ref, utb_ref, db_ref, 1, 0))

    def chunk_group(cg, carry):
        cis = [cg * HGRN_UNROLL + u for u in range(HGRN_UNROLL)]
        rows = [pl.ds(pl.multiple_of(ci * c, c), c) for ci in cis]
        zqs = [q_ref[rw, :] for rw in rows]
        qs = [zq * jax.nn.sigmoid(zq) for zq in zqs]
        vbs = [i_ref[rw, :].astype(BF16) for rw in rows]
        units = [(u, d) for u in range(HGRN_UNROLL) for d in range(2)]
        nw = N_LEVELS - 1
        ks, es, fs = {}, {}, {}
        for u, d in units:
            z_ref, w_ref = dirs[d][0], dirs[d][1]
            lb = lb_ref[d:d + 1, :]
            f = lb + (1.0 - lb) * jax.nn.sigmoid(z_ref[rows[u], :])
            fs[u, d] = f
            ks[u, d] = 1.0 - f
            es[u, d] = jnp.exp(jnp.dot(w_ref[...], _split3(jnp.log(f)), preferred_element_type=F32))
        acc = [jnp.zeros((c, c), F32) for _ in range(HGRN_UNROLL)]
        for lvl in range(N_LEVELS + 1):
            for u in range(HGRN_UNROLL):
                if lvl < N_LEVELS:
                    up, lo = up_ref[lvl], lo_ref[lvl]
                    if lvl < nw:
                        ef = es[u, 0][lvl * c:(lvl + 1) * c]
                        eb = es[u, 1][lvl * c:(lvl + 1) * c]
                    else:
                        ef = fs[u, 0] * up + lo
                        eb = fs[u, 1] * lo + up
                    qa = jnp.concatenate([(qs[u] * (ef * up)).astype(BF16), (qs[u] * (eb * lo)).astype(BF16)],
                                         axis=1)
                    ka = jnp.concatenate([(ks[u, 0] * (ef * lo)).astype(BF16),
                                          (ks[u, 1] * (eb * up)).astype(BF16)], axis=1)
                else:
                    qa, ka = qs[u].astype(BF16), (ks[u, 0] + ks[u, 1]).astype(BF16)
                p = lax.dot_general(qa, ka, _NT, preferred_element_type=F32)
                acc[u] = acc[u] + p * mf_ref[lvl]
        for u, d in units:
            _, _, _, qt_ref, ut_ref, d_ref, _, drow = dirs[d]
            e = es[u, d]
            qt_ref[rows[u], :] = (qs[u] * e[nw * c:(nw + 1) * c]).astype(BF16)
            kt = (ks[u, d] * e[(nw + 1) * c:(nw + 2) * c]).astype(BF16)
            ut_ref[cis[u]] = lax.dot_general(vbs[u], kt, _TN, preferred_element_type=F32)
            d_ref[pl.ds(cis[u], 1), :] = e[nw * c + drow:nw * c + drow + 1]
        for u in range(HGRN_UNROLL):
            oi_ref[rows[u], :] = jnp.dot(acc[u].astype(BF16), vbs[u], preferred_element_type=F32)
        return carry

    lax.fori_loop(0, cpb // HGRN_UNROLL, chunk_group, 0)


def _hgrn_a(proj, lb, col, t, d_rec):
    heads = d_rec // REC_HEAD_DIM
    tq = 1024
    cpb = tq // REC_CHUNK
    nchunks = t // REC_CHUNK
    wf, wb, mf, up, lo = _hgrn_consts()
    hd = REC_HEAD_DIM

    def colspec(off):
        return pl.BlockSpec((tq, hd), lambda i, h: (i, off + h))

    full2 = lambda i, h: (0, 0)
    full3 = lambda i, h: (0, 0, 0)
    row_spec = pl.BlockSpec((tq, hd), lambda i, h: (i, h))
    u_spec = pl.BlockSpec((cpb, hd, hd), lambda i, h: (i, 0, h))
    d_spec = pl.BlockSpec((cpb, hd), lambda i, h: (i, h))
    return pl.pallas_call(
        functools.partial(_hgrn_a_body, cpb=cpb),
        grid=(t // tq, heads),
        in_specs=[colspec(col["q_r"]), colspec(col["i_r"]), colspec(col["zf_f"]), colspec(col["zf_b"]),
                  pl.BlockSpec((2, hd), lambda i, h: (0, h)),
                  pl.BlockSpec(wf.shape, full2), pl.BlockSpec(wb.shape, full2),
                  pl.BlockSpec(mf.shape, full3), pl.BlockSpec(up.shape, full3), pl.BlockSpec(lo.shape, full3)],
        out_specs=[row_spec, row_spec, row_spec, u_spec, u_spec, d_spec, d_spec],
        out_shape=[jax.ShapeDtypeStruct((t, d_rec), F32),
                   jax.ShapeDtypeStruct((t, d_rec), BF16),
                   jax.ShapeDtypeStruct((t, d_rec), BF16),
                   jax.ShapeDtypeStruct((nchunks, hd, d_rec), F32),
                   jax.ShapeDtypeStruct((nchunks, hd, d_rec), F32),
                   jax.ShapeDtypeStruct((nchunks, d_rec), F32),
                   jax.ShapeDtypeStruct((nchunks, d_rec), F32)],
        compiler_params=_cparams(("arbitrary", "arbitrary"), 32),
        name="hgrn_a",
    )(proj, proj, proj, proj, lb, wf, wb, mf, up, lo)


def _hgrn_c_body(oi_ref, qtf_ref, qtb_ref, utf_ref, utb_ref, df_ref, db_ref, z_ref, g_ref,
                 out_ref, acc_ref, accb_ref, *, nchunks):
    c = REC_CHUNK
    hd = REC_HEAD_DIM

    unroll = 4

    def step(i, carry):
        st_f, st_b = carry
        pending = []
        for u in range(unroll):
            nf = i * unroll + u
            nb = nchunks - 1 - nf
            rows_f = pl.ds(pl.multiple_of(nf * c, c), c)
            rows_b = pl.ds(pl.multiple_of(nb * c, c), c)
            of = lax.dot_general(qtf_ref[rows_f, :], st_f.astype(BF16), _NT, preferred_element_type=F32)
            ob = lax.dot_general(qtb_ref[rows_b, :], st_b.astype(BF16), _NT, preferred_element_type=F32)
            pending.append((rows_f, rows_b, oi_ref[rows_f, :] + of, ob))
            st_f = df_ref[pl.ds(nf, 1), :] * st_f + utf_ref[nf]
            st_b = db_ref[pl.ds(nb, 1), :] * st_b + utb_ref[nb]
        for rows_f, rows_b, vf, vb in pending:
            acc_ref[rows_f, :] = vf
            accb_ref[rows_b, :] = vb
        return st_f, st_b

    zero = jnp.zeros((hd, hd), F32)
    lax.fori_loop(0, nchunks // unroll, step, (zero, zero))

    o = _rms(acc_ref[...] + accb_ref[...])
    out_ref[...] = (o * g_ref[...] * jax.nn.sigmoid(z_ref[...])).astype(BF16)


def _hgrn_c(oi, qtf, qtb, utf, utb, df, db, proj, g_out, col, bsz, seq, d_rec):
    heads = d_rec // REC_HEAD_DIM
    hd = REC_HEAD_DIM
    nchunks = seq // REC_CHUNK
    row_spec = pl.BlockSpec((seq, hd), lambda b, h: (b, h))
    u_spec = pl.BlockSpec((nchunks, hd, hd), lambda b, h: (b, 0, h))
    d_spec = pl.BlockSpec((nchunks, hd), lambda b, h: (b, h))
    zo = col["z_o"]
    return pl.pallas_call(
        functools.partial(_hgrn_c_body, nchunks=nchunks),
        grid=(bsz, heads),
        in_specs=[row_spec, row_spec, row_spec, u_spec, u_spec, d_spec, d_spec,
                  pl.BlockSpec((seq, hd), lambda b, h: (b, zo + h)),
                  pl.BlockSpec((1, hd), lambda b, h: (0, h))],
        out_specs=row_spec,
        out_shape=jax.ShapeDtypeStruct((bsz * seq, d_rec), BF16),
        scratch_shapes=[pltpu.VMEM((seq, hd), F32), pltpu.VMEM((seq, hd), F32)],
        compiler_params=_cparams(("arbitrary", "arbitrary"), 48),
        name="hgrn_c",
    )(oi, qtf, qtb, utf, utb, df, db, proj, g_out.reshape(1, d_rec))


def _t5_bucket(rel):
    half_buckets = NUM_BUCKETS // 2
    ret = np.where(rel > 0, half_buckets, 0)
    n = np.abs(rel)
    max_exact = half_buckets // 2
    nf = np.maximum(n, 1).astype(np.float32)
    large = max_exact + (np.log(nf / np.float32(max_exact)) / np.float32(math.log(MAX_DISTANCE / max_exact))
                         * np.float32(half_buckets - max_exact)).astype(np.int32)
    large = np.minimum(large, half_buckets - 1)
    return ret + np.where(n < max_exact, n, large)


def _band_bias(rel_bias_g, window, dil):
    half = window // (2 * dil)
    q_off = np.arange(ATT_BLOCK)[:, None]
    rel = np.arange(3 * ATT_BLOCK)[None, :] - ATT_BLOCK - q_off
    onehot = (_t5_bucket(rel * dil)[..., None] == np.arange(NUM_BUCKETS)).astype(np.float32)
    bias = jnp.einsum("qkb,bh->hqk", jnp.asarray(onehot), rel_bias_g.astype(F32),
                      precision=lax.Precision.HIGHEST)
    return jnp.where(jnp.asarray(np.abs(rel) <= half)[None], bias, NEG_INF)


def _attn_body(q_ref, kp_ref, k_ref, kn_ref, vp_ref, v_ref, vn_ref, bias_ref,
               num_ref, st_ref, kc_ref, vc_ref, *, dil, tq, sub_len):
    blk = ATT_BLOCK
    nqb = tq // blk
    n = pl.program_id(1)
    scale = ATT_HEAD_DIM ** -0.5

    def sds(start, size):
        if dil == 1:
            return pl.ds(start, size)
        return pl.ds(start, size, stride=dil)

    cu = kc_ref.shape[0]
    qu = ATT_UNROLL // cu

    def deinterleave(r, j):
        kc_ref[j, 0:blk, :] = kp_ref[sds(r, blk), :].astype(BF16)
        kc_ref[j, blk:blk + tq, :] = k_ref[sds(r, tq), :].astype(BF16)
        kc_ref[j, blk + tq:2 * blk + tq, :] = kn_ref[sds(r, blk), :].astype(BF16)
        vc_ref[j, 0:blk, :] = vp_ref[sds(r, blk), :].astype(BF16)
        vc_ref[j, blk:blk + tq, :] = v_ref[sds(r, tq), :].astype(BF16)
        vc_ref[j, blk + tq:2 * blk + tq, :] = vn_ref[sds(r, blk), :].astype(BF16)

    def units(r0, qb0):
        us = [(j, u) for j in range(cu) for u in range(qu)]
        q0s = [pl.multiple_of((qb0 + u) * blk, blk) for _, u in us]
        rows = [sds(r0 + j + dil * q0, blk) for (j, _), q0 in zip(us, q0s)]
        lane = lax.broadcasted_iota(jnp.int32, (blk, LANE), 1)
        key_iota = lax.broadcasted_iota(jnp.int32, (1, 3 * blk), 1)
        bias = bias_ref[0]
        qs = [q_ref[rw, :].astype(BF16) for rw in rows]
        kws = [kc_ref[j, pl.ds(q0, 3 * blk), :] for (j, _), q0 in zip(us, q0s)]
        vws = [vc_ref[j, pl.ds(q0, 3 * blk), :] for (j, _), q0 in zip(us, q0s)]
        ss = [lax.dot_general(q, kw, _NT, preferred_element_type=F32) * scale for q, kw in zip(qs, kws)]
        valids = []
        for q0 in q0s:
            kpos = n * tq + q0 - blk + key_iota
            valids.append((kpos >= 0) & (kpos < sub_len))
        ss = [jnp.where(valid, s + bias, NEG_INF) for s, valid in zip(ss, valids)]
        ms = [jnp.max(s, axis=-1, keepdims=True) for s in ss]
        ps = [jnp.exp(s - m) for s, m in zip(ss, ms)]
        ls = [jnp.sum(p, axis=-1, keepdims=True) for p in ps]
        nums = [jnp.dot(p.astype(BF16), vw, preferred_element_type=F32) for p, vw in zip(ps, vws)]
        for rw, num, m, l in zip(rows, nums, ms, ls):
            num_ref[rw, :] = num
            st_ref[rw, :] = jnp.where(lane < LANE // 2, m, l)

    def class_group(rg, carry):
        r0 = rg * cu
        for j in range(cu):
            deinterleave(r0 + j, j)

        def qgroup(qg, carry2):
            units(r0, qg * qu)
            return carry2

        lax.fori_loop(0, nqb // qu, qgroup, 0)
        return carry

    lax.fori_loop(0, dil // cu, class_group, 0)


def _attn_group(proj, bias, col, g, dil, bsz, seq):
    tile = 1024
    tq = tile // dil
    halo = ATT_BLOCK * dil
    sub_len = seq // dil
    cu = ATT_UNROLL // min(tq // ATT_BLOCK, ATT_UNROLL)
    nh = ATT_HEADS_PER_GROUP
    hd = ATT_HEAD_DIM
    qc = col["q_a"] + g * nh
    kc = col["k_a"] + g * nh
    vc = col["v_a"] + g * nh
    tiles_b = seq // tile
    halos_b = seq // halo
    hpt = tile // halo

    own = lambda c: pl.BlockSpec((tile, hd), lambda b, n, h: (b * tiles_b + n, c + h))
    prev = lambda c: pl.BlockSpec(
        (halo, hd), lambda b, n, h: (b * halos_b + jnp.maximum(n * hpt - 1, 0), c + h))
    nxt = lambda c: pl.BlockSpec(
        (halo, hd), lambda b, n, h: (b * halos_b + jnp.minimum((n + 1) * hpt, halos_b - 1), c + h))
    t = bsz * seq
    return pl.pallas_call(
        functools.partial(_attn_body, dil=dil, tq=tq, sub_len=sub_len),
        grid=(bsz, tiles_b, nh),
        in_specs=[own(qc), prev(kc), own(kc), nxt(kc), prev(vc), own(vc), nxt(vc),
                  pl.BlockSpec((1,) + bias.shape[1:], lambda b, n, h: (h, 0, 0))],
        out_specs=[pl.BlockSpec((tile, hd), lambda b, n, h: (b * tiles_b + n, h)),
                   pl.BlockSpec((tile, LANE), lambda b, n, h: (b * tiles_b + n, h))],
        out_shape=[jax.ShapeDtypeStruct((t, nh * hd), F32), jax.ShapeDtypeStruct((t, nh * LANE), F32)],
        scratch_shapes=[pltpu.VMEM((cu, tq + 2 * ATT_BLOCK, hd), BF16),
                        pltpu.VMEM((cu, tq + 2 * ATT_BLOCK, hd), BF16)],
        compiler_params=_cparams(("arbitrary", "arbitrary", "arbitrary"), 32),
        name=f"attn_d{dil}",
    )(proj, proj, proj, proj, proj, proj, proj, bias)


def _merge_body(rec_ref, n0_ref, n1_ref, n2_ref, s0_ref, s1_ref, s2_ref, zgr_ref, zga_ref, x_ref,
                ada_ref, gpost_ref, gpre_ref, wbr_ref, wba_ref, wo_ref, wr_ref, br_ref,
                x1_ref, h2_ref, lg_ref):
    nh = ATT_HEADS_PER_GROUP
    hd = ATT_HEAD_DIM
    half = LANE // 2
    lane = lax.broadcasted_iota(jnp.int32, (rec_ref.shape[0], LANE), 1)
    heads = []
    for h in range(nh):
        cols = slice(h * hd, (h + 1) * hd)
        st = [s[:, cols] for s in (s0_ref, s1_ref, s2_ref)]
        top = jnp.maximum(jnp.maximum(st[0], st[1]), st[2])
        ws = [jnp.exp(s - top) for s in st]
        den = (ws[0] * pltpu.roll(st[0], half, 1) + ws[1] * pltpu.roll(st[1], half, 1)
               + ws[2] * pltpu.roll(st[2], half, 1))
        coef = [w / den for w in ws]
        coef = [jnp.where(lane < half, c, pltpu.roll(c, half, 1)) for c in coef]
        num = coef[0] * n0_ref[:, cols] + coef[1] * n1_ref[:, cols] + coef[2] * n2_ref[:, cols]
        heads.append(num.astype(BF16))
    att = jnp.concatenate(heads, axis=1)
    y_rec = jnp.dot(rec_ref[...], wbr_ref[...], preferred_element_type=F32)
    y_att = jnp.dot(att, wba_ref[...], preferred_element_type=F32)
    merged = jax.nn.sigmoid(zgr_ref[...]) * y_rec + jax.nn.sigmoid(zga_ref[...]) * y_att
    y = jnp.dot(merged.astype(BF16), wo_ref[...], preferred_element_type=F32)
    gt_m = ada_ref[0, 2:3, :]
    sh_f = ada_ref[0, 3:4, :]
    sc_f = ada_ref[0, 4:5, :]
    x1 = x_ref[...] + gt_m * (_rms(y) * gpost_ref[...])
    x1_ref[...] = x1
    h2 = _rms(x1) * gpre_ref[...] * (1.0 + sc_f) + sh_f
    tm = h2.shape[0]
    for s in range(SLAB_ROWS):
        h2_ref[pl.ds(s, tm, stride=SLAB_ROWS), :] = h2[:, s * LANE:(s + 1) * LANE]
    ne = lg_ref.shape[1]
    h_hi = h2.astype(BF16)
    h_lo = (h2 - h_hi.astype(F32)).astype(BF16)
    both = jnp.dot(h_hi, wr_ref[...], preferred_element_type=F32)
    cross = jnp.dot(h_lo, wr_ref[:, 0:ne], preferred_element_type=F32)
    lg_ref[...] = both[:, 0:ne] + both[:, ne:2 * ne] + cross + br_ref[...]


def _merge(rec_o, nums, stats, proj, x2, ada3, g_post, g_pre, wbr, wba, wo, w_router, b_router, col, seq):
    t, d = x2.shape
    tm = 256
    per_b = seq // tm
    d_rec = rec_o.shape[1]
    w_att = nums[0].shape[1]
    ne = w_router.shape[1]
    wr_hi = w_router.astype(BF16)
    wr_lo = (w_router - wr_hi.astype(F32)).astype(BF16)
    w_router = jnp.concatenate([wr_hi, wr_lo], axis=1)
    dl = d // LANE
    row = lambda w: pl.BlockSpec((tm, w), lambda i: (i, 0))
    const = lambda shape: pl.BlockSpec(shape, lambda i: (0,) * len(shape), pipeline_mode=pl.Buffered(1))
    zgr = col["zg_rec"] // dl
    zga = col["zg_att"] // dl
    return pl.pallas_call(
        _merge_body,
        grid=(t // tm,),
        in_specs=[row(d_rec), row(w_att), row(w_att), row(w_att), row(w_att), row(w_att), row(w_att),
                  pl.BlockSpec((tm, d), lambda i: (i, zgr)),
                  pl.BlockSpec((tm, d), lambda i: (i, zga)),
                  row(d),
                  pl.BlockSpec((1, 6, d), lambda i: (i // per_b, 0, 0)),
                  const((1, d)), const((1, d)),
                  const(wbr.shape), const(wba.shape), const(wo.shape), const(w_router.shape),
                  const((1, ne))],
        out_specs=[row(d), pl.BlockSpec((tm * SLAB_ROWS, LANE), lambda i: (i, 0)),
                   pl.BlockSpec((tm, ne), lambda i: (i, 0))],
        out_shape=[jax.ShapeDtypeStruct((t, d), F32), jax.ShapeDtypeStruct((t * SLAB_ROWS, LANE), F32),
                   jax.ShapeDtypeStruct((t, ne), F32)],
        compiler_params=_cparams(("arbitrary",), 56),
        name="merge",
    )(rec_o, nums[0], nums[1], nums[2], stats[0], stats[1], stats[2], proj, proj, x2, ada3,
      g_post.reshape(1, d), g_pre.reshape(1, d), wbr, wba, wo, w_router, b_router.reshape(1, ne))


def _route_body(lg_ref, tri_ref, rt_ref, cnt_ref, carry_ref):
    i = pl.program_id(0)
    tr, ne = lg_ref.shape

    @pl.when(i == 0)
    def _():
        carry_ref[...] = jnp.zeros_like(carry_ref)

    l = lg_ref[...]
    lane = lax.broadcasted_iota(jnp.int32, (tr, ne), 1).astype(F32)
    vals, sels, idxs = [], [], []
    for _ in range(TOP_K):
        m = jnp.max(l, axis=-1, keepdims=True)
        idx = jnp.min(jnp.where(l == m, lane, float(ne)), axis=-1, keepdims=True)
        sel = lane == idx
        vals.append(m)
        idxs.append(idx)
        sels.append(sel)
        l = jnp.where(sel, -jnp.inf, l)
    es = [jnp.exp(v - vals[0]) for v in vals]
    tot = es[0] + es[1] + es[2] + es[3]
    chosen = (sels[0] | sels[1] | sels[2] | sels[3]).astype(F32)
    prefix = jnp.dot(tri_ref[...], chosen.astype(BF16), preferred_element_type=F32) + carry_ref[0:1, :]
    out_lane = lax.broadcasted_iota(jnp.int32, (tr, LANE), 1)
    rt = jnp.zeros((tr, LANE), F32)
    for k in range(TOP_K):
        rank = jnp.sum(jnp.where(sels[k], prefix, 0.0), axis=-1, keepdims=True)
        rt = jnp.where(out_lane == k, idxs[k], rt)
        rt = jnp.where(out_lane == TOP_K + k, es[k] / tot, rt)
        rt = jnp.where(out_lane == 2 * TOP_K + k, rank, rt)
    rt_ref[...] = rt
    new = carry_ref[0:1, :] + jnp.sum(chosen, axis=0, keepdims=True)
    carry_ref[...] = jnp.broadcast_to(new, carry_ref.shape)
    cnt_ref[...] = carry_ref[...]


def _route(logits):
    t, ne = logits.shape
    tr = 512
    tri = jnp.asarray(np.tril(np.ones((tr, tr), np.float32), -1), BF16)
    return pl.pallas_call(
        _route_body,
        grid=(t // tr,),
        in_specs=[pl.BlockSpec((tr, ne), lambda i: (i, 0)),
                  pl.BlockSpec((tr, tr), lambda i: (0, 0))],
        out_specs=[pl.BlockSpec((tr, LANE), lambda i: (i, 0)),
                   pl.BlockSpec((SUBLANE, ne), lambda i: (0, 0))],
        out_shape=[jax.ShapeDtypeStruct((t, LANE), F32), jax.ShapeDtypeStruct((SUBLANE, ne), F32)],
        scratch_shapes=[pltpu.VMEM((SUBLANE, ne), F32)],
        compiler_params=_cparams(("arbitrary",), 32),
        name="route",
    )(logits, tri)


def _chunk_rows(ncols):
    return W_CHUNK_BYTES // (4 * ncols)


def _weight_stream(w_hbm, wbf, stage, wsem, e, slot, c0, c1, priority=0):
    kc = stage.shape[1]

    def copy(c):
        return pltpu.make_async_copy(w_hbm.at[e, pl.ds(pl.multiple_of(c * kc, kc), kc), :],
                                     stage.at[c % 2], wsem.at[c % 2])

    def prime():
        def body(c, carry):
            copy(c).start(priority=priority)
            return carry
        lax.fori_loop(c0, jnp.minimum(c0 + 2, c1), body, 0)

    def finish():
        def body(c, carry):
            copy(c).wait()
            buf = c % 2

            def cast(i, carry2):
                r = pl.multiple_of(i * CAST_ROWS, CAST_ROWS)
                wbf[slot, pl.ds(pl.multiple_of(c * kc, kc) + r, CAST_ROWS), :] = (
                    stage[buf, pl.ds(r, CAST_ROWS), :].astype(BF16))
                return carry2
            lax.fori_loop(0, kc // CAST_ROWS, cast, 0)

            @pl.when(c + 2 < c1)
            def _():
                copy(c + 2).start(priority=priority)
            return carry
        lax.fori_loop(c0, c1, body, 0)

    return prime, finish


def _moe_up_body(be_ref, nu_ref, ws_ref, wn_ref, wc0_ref, wc1_ref, nv_ref, idx0_ref, idxn_ref, h2s_ref, w_hbm,
                 bias_ref, o_ref, xbuf, wbf, stage, sem, wsem):
    b = pl.program_id(0)
    nu = nu_ref[0]
    f = o_ref.shape[1]
    slab = SLAB_ROWS
    nch = wbf.shape[1] // stage.shape[1]

    def row_start(idx_ref, slot, r):
        tok = idx_ref[0, 0, r]
        pltpu.make_async_copy(h2s_ref.at[pl.ds(pl.multiple_of(tok * slab, slab), slab), :],
                              xbuf.at[slot, pl.ds(pl.multiple_of(r * SLAB_PITCH, SUBLANE), slab), :],
                              sem.at[slot]).start()

    def wait_rows(slot):
        pltpu.make_async_copy(h2s_ref.at[pl.ds(0, BM * slab), :], xbuf.at[slot, pl.ds(0, BM * slab), :],
                              sem.at[slot]).wait()

    def issue(idx_ref, slot):
        def body(r, c):
            row_start(idx_ref, slot, r)
            return c
        lax.fori_loop(0, BM, body, 0, unroll=DMA_UNROLL)

    @pl.when(b == 0)
    def _():
        issue(idx0_ref, 0)
        prime0, finish0 = _weight_stream(w_hbm, wbf, stage, wsem, be_ref[0], ws_ref[0], 0, nch)
        prime0()
        finish0()

    @pl.when(b + 1 < nu)
    def _():
        issue(idxn_ref, (b + 1) % 2)

    @pl.when(b < nu)
    def _():
        wslot = ws_ref[b]
        prime, finish = _weight_stream(w_hbm, wbf, stage, wsem, wn_ref[b], 1 - wslot, wc0_ref[b], wc1_ref[b],
                                       priority=1)
        prime()
        slot = b % 2
        wait_rows(slot)

        def compute(m):
            x = jnp.concatenate([xbuf[slot, pl.ds(s, m, stride=SLAB_PITCH), :].astype(BF16) for s in range(slab)],
                                axis=1)
            half = f // 2
            for c0 in (0, half):
                gate = (jnp.dot(x, wbf[wslot, :, c0:c0 + half], preferred_element_type=F32)
                        + bias_ref[0, :, c0:c0 + half])
                up = (jnp.dot(x, wbf[wslot, :, f + c0:f + c0 + half], preferred_element_type=F32)
                      + bias_ref[0, :, f + c0:f + c0 + half])
                gate = jnp.minimum(gate, SWIGLU_LIMIT)
                up = jnp.clip(up, -SWIGLU_LIMIT, SWIGLU_LIMIT)
                o_ref[0:m, c0:c0 + half] = (gate * jax.nn.sigmoid(SWIGLU_ALPHA * gate) * (up + 1.0)).astype(BF16)
            if m < BM:
                o_ref[m:BM, :] = jnp.zeros((BM - m, f), BF16)

        @pl.when(nv_ref[b] > BM // 2)
        def _():
            compute(BM)

        @pl.when(nv_ref[b] <= BM // 2)
        def _():
            compute(BM // 2)

        finish()

    @pl.when(b >= nu)
    def _():
        o_ref[...] = jnp.zeros_like(o_ref)


def _moe_up(h2s, src_tok, wgu, bgu, sched):
    ne, d, f2 = wgu.shape
    f = f2 // 2
    p = src_tok.shape[0]
    nb = p // BM
    idx3 = src_tok.reshape(nb, 1, BM)
    smem_blk = lambda imap: pl.BlockSpec((1, 1, BM), imap, memory_space=pltpu.SMEM)
    grid_spec = pltpu.PrefetchScalarGridSpec(
        num_scalar_prefetch=7,
        grid=(nb,),
        in_specs=[smem_blk(lambda b, be, *_: (0, 0, 0)),
                  smem_blk(lambda b, be, *_: (jnp.minimum(b + 1, nb - 1), 0, 0)),
                  pl.BlockSpec(memory_space=pl.ANY),
                  pl.BlockSpec(memory_space=pl.ANY),
                  pl.BlockSpec((1, 1, f2), lambda b, be, *_: (be[b], 0, 0))],
        out_specs=pl.BlockSpec((BM, f), lambda b, be, *_: (b, 0)),
        scratch_shapes=[pltpu.VMEM((2, BM * SLAB_PITCH, LANE), F32),
                        pltpu.VMEM((2, d, f2), BF16),
                        pltpu.VMEM((2, _chunk_rows(f2), f2), F32),
                        pltpu.SemaphoreType.DMA((2,)), pltpu.SemaphoreType.DMA((2,))],
    )
    return pl.pallas_call(
        _moe_up_body,
        grid_spec=grid_spec,
        out_shape=jax.ShapeDtypeStruct((p, f), BF16),
        compiler_params=_cparams(("arbitrary",), 58),
        name="moe_up",
    )(sched["blk_e"], sched["n_used"], sched["wslot"], sched["wnext"], sched["wc0"], sched["wc1"],
      sched["nvalid"], idx3, idx3, h2s, wgu, bgu.reshape(ne, 1, f2))


def _moe_down_body(be_ref, nu_ref, ws_ref, wn_ref, wc0_ref, wc1_ref, nv_ref, dst_ref, a_ref, w_hbm, bias_ref,
                   ysc_ref, ybuf, wbf, stage, sem, wsem):
    b = pl.program_id(0)
    nb = pl.num_programs(0)
    nu = nu_ref[0]
    slab = SLAB_ROWS
    nch = wbf.shape[1] // stage.shape[1]

    @pl.when(b == 0)
    def _():
        prime0, finish0 = _weight_stream(w_hbm, wbf, stage, wsem, be_ref[0], ws_ref[0], 0, nch)
        prime0()
        finish0()

    def row_copy(slot, r, d):
        return pltpu.make_async_copy(ybuf.at[slot, pl.ds(pl.multiple_of(r * SLAB_PITCH, SUBLANE), slab), :],
                                     ysc_ref.at[pl.ds(pl.multiple_of(d * slab, slab), slab), :],
                                     sem.at[slot])

    def drain(step):
        slot = step % 2
        count = nv_ref[step]

        @pl.when(count == BM)
        def _():
            pltpu.make_async_copy(ybuf.at[slot, pl.ds(0, BM * slab), :], ysc_ref.at[pl.ds(0, BM * slab), :],
                                  sem.at[slot]).wait()

        @pl.when(count < BM)
        def _():
            def body(r, c):
                row_copy(slot, 0, 0).wait()
                return c
            lax.fori_loop(0, count, body, 0)

    @pl.when((b >= 2) & (b < nu))
    def _():
        drain(b - 2)

    @pl.when(b < nu)
    def _():
        slot = b % 2
        wslot = ws_ref[b]
        prime, finish = _weight_stream(w_hbm, wbf, stage, wsem, wn_ref[b], 1 - wslot, wc0_ref[b], wc1_ref[b])
        prime()
        def compute(m):
            y = jnp.dot(a_ref[0:m, :], wbf[wslot], preferred_element_type=F32) + bias_ref[0]
            for s in range(slab):
                ybuf[slot, pl.ds(s, m, stride=SLAB_PITCH), :] = y[:, s * LANE:(s + 1) * LANE]

        @pl.when(nv_ref[b] > BM // 2)
        def _():
            compute(BM)

        @pl.when(nv_ref[b] <= BM // 2)
        def _():
            compute(BM // 2)

        finish()

        def body(r, c):
            row_copy(slot, r, dst_ref[0, 0, r]).start()
            return c

        def pair(i, c):
            row_copy(slot, 2 * i, dst_ref[0, 0, 2 * i]).start(priority=0)
            row_copy(slot, 2 * i + 1, dst_ref[0, 0, 2 * i + 1]).start(priority=1)
            return c

        @pl.when(nv_ref[b] == BM)
        def _():
            lax.fori_loop(0, BM // 2, pair, 0, unroll=DMA_UNROLL // 2)

        @pl.when(nv_ref[b] < BM)
        def _():
            lax.fori_loop(0, nv_ref[b], body, 0)

    @pl.when(b == nb - 1)
    def _():
        @pl.when(nu >= 2)
        def _():
            drain(nu - 2)
        drain(nu - 1)


def _moe_down(act, dst_slot, n_out_rows, wd, bd, sched):
    p, f = act.shape
    ne, _, d = wd.shape
    nb = p // BM
    grid_spec = pltpu.PrefetchScalarGridSpec(
        num_scalar_prefetch=7,
        grid=(nb,),
        in_specs=[pl.BlockSpec((1, 1, BM), lambda b, be, *_: (b, 0, 0), memory_space=pltpu.SMEM),
                  pl.BlockSpec((BM, f), lambda b, be, *_: (b, 0)),
                  pl.BlockSpec(memory_space=pl.ANY),
                  pl.BlockSpec((1, 1, d), lambda b, be, *_: (be[b], 0, 0))],
        out_specs=pl.BlockSpec(memory_space=pl.ANY),
        scratch_shapes=[pltpu.VMEM((2, BM * SLAB_PITCH, LANE), F32),
                        pltpu.VMEM((2, f, d), BF16),
                        pltpu.VMEM((2, _chunk_rows(d), d), F32),
                        pltpu.SemaphoreType.DMA((2,)), pltpu.SemaphoreType.DMA((2,))],
    )
    return pl.pallas_call(
        _moe_down_body,
        grid_spec=grid_spec,
        out_shape=jax.ShapeDtypeStruct((n_out_rows * SLAB_ROWS, LANE), F32),
        compiler_params=pltpu.CompilerParams(dimension_semantics=("arbitrary",),
                                             vmem_limit_bytes=48 * 1024 * 1024, has_side_effects=True),
        name="moe_down",
    )(sched["blk_e"], sched["n_used"], sched["wslot"], sched["wnext"], sched["wc0"], sched["wc1"],
      sched["nvalid"], dst_slot.reshape(nb, 1, BM), act, wd, bd.reshape(ne, 1, d))


def _final_body(y0_ref, y1_ref, y2_ref, y3_ref, rt_ref, x1_ref, ada_ref, g_ref, o_ref):
    tm = x1_ref.shape[0]
    rt = rt_ref[...]
    y_refs = (y0_ref, y1_ref, y2_ref, y3_ref)
    pieces = []
    for s in range(SLAB_ROWS):
        acc = rt[:, TOP_K:TOP_K + 1] * y_refs[0][pl.ds(s, tm, stride=SLAB_ROWS), :]
        for k in range(1, TOP_K):
            acc = acc + rt[:, TOP_K + k:TOP_K + k + 1] * y_refs[k][pl.ds(s, tm, stride=SLAB_ROWS), :]
        pieces.append(acc)
    y = jnp.concatenate(pieces, axis=1)
    gt_f = ada_ref[0, 5:6, :]
    o_ref[...] = x1_ref[...] + gt_f * (_rms(y) * g_ref[...])


def _final(ysc, rt, x1, ada3, g_post, seq):
    t, d = x1.shape
    tm = 256
    per_b = seq // tm
    nt = t // tm
    assert TOP_K == 4
    slot_spec = lambda k: pl.BlockSpec((tm * SLAB_ROWS, LANE), lambda i: (k * nt + i, 0))
    return pl.pallas_call(
        _final_body,
        grid=(nt,),
        in_specs=[slot_spec(0), slot_spec(1), slot_spec(2), slot_spec(3),
                  pl.BlockSpec((tm, LANE), lambda i: (i, 0)),
                  pl.BlockSpec((tm, d), lambda i: (i, 0)),
                  pl.BlockSpec((1, 6, d), lambda i: (i // per_b, 0, 0)),
                  pl.BlockSpec((1, d), lambda i: (0, 0))],
        out_specs=pl.BlockSpec((tm, d), lambda i: (i, 0)),
        out_shape=jax.ShapeDtypeStruct((t, d), F32),
        compiler_params=_cparams(("arbitrary",), 48),
        name="final",
    )(ysc, ysc, ysc, ysc, rt, x1, ada3, g_post.reshape(1, d))


def _mixer_ffn_layer(x2, ada3, bsz, seq, g_mix_pre, g_mix_post, g_ffn_pre, g_ffn_post, w_in, lb, g_rec_out,
                     rel_bias, w_branch_rec, w_branch_att, w_o, w_router, b_router, w_gate_up, b_gate_up,
                     w_down, b_down):
    t, d = x2.shape
    d_rec = w_branch_rec.shape[0]
    w_att = w_branch_att.shape[0]
    d_att = 3 * w_att
    widths = dict(q_r=d_rec, i_r=d_rec, zf_f=d_rec, zf_b=d_rec, z_o=d_rec, q_a=d_att, k_a=d_att, v_a=d_att,
                  zg_rec=d, zg_att=d)
    my_order = ("zg_rec", "zg_att", "q_r", "i_r", "zf_f", "zf_b", "z_o", "q_a", "k_a", "v_a")
    col, acc = {}, 0
    for name in my_order:
        col[name] = acc // LANE
        acc += widths[name]
    rot = acc - 2 * d

    proj = _inproj(x2, g_mix_pre, ada3, w_in, seq, rot)

    oi, qtf, qtb, utf, utb, df, db = _hgrn_a(proj, lb, col, t, d_rec)
    rec_o = _hgrn_c(oi, qtf, qtb, utf, utb, df, db, proj, g_rec_out, col, bsz, seq, d_rec)

    nums, stats = [], []
    for g, (window, dil) in enumerate(DIL_GROUPS):
        hs = slice(g * ATT_HEADS_PER_GROUP, (g + 1) * ATT_HEADS_PER_GROUP)
        bias = _band_bias(rel_bias[:, hs], window, dil)
        num, st = _attn_group(proj, bias, col, g, dil, bsz, seq)
        nums.append(num)
        stats.append(st)

    x1, h2s, logits = _merge(rec_o, nums, stats, proj, x2, ada3, g_mix_post, g_ffn_pre,
                            w_branch_rec.astype(BF16), w_branch_att.astype(BF16), w_o.astype(BF16),
                            w_router, b_router, col, seq)

    rt, cnt = _route(logits)
    ne = logits.shape[1]
    counts = cnt[0].astype(jnp.int32)
    top_idx = rt[:, 0:TOP_K].astype(jnp.int32)
    rank = rt[:, 2 * TOP_K:3 * TOP_K].astype(jnp.int32)
    padded = (counts + BM - 1) // BM * BM
    pends = jnp.cumsum(padded)
    pstarts = pends - padded
    experts = jnp.arange(ne, dtype=jnp.int32)
    pstart_sel = jnp.sum(jnp.where(top_idx[..., None] == experts, pstarts, 0), axis=-1)
    dest = (pstart_sel + rank).T.reshape(-1)
    p_rows = t * TOP_K + ne * BM
    nb = p_rows // BM
    blk_start = jnp.arange(nb, dtype=jnp.int32) * BM
    blk_e = jnp.minimum(jnp.sum((pends[None, :] <= blk_start[:, None]).astype(jnp.int32), axis=1), ne - 1)
    n_used = (pends[-1:] // BM).astype(jnp.int32)

    n_assign = t * TOP_K
    slot_assign = jnp.full((p_rows,), -1, jnp.int32).at[dest].set(jnp.arange(n_assign, dtype=jnp.int32))
    valid_end = pstarts + counts
    blk_end = jnp.sum(jnp.where(blk_e[:, None] == experts, valid_end, 0), axis=-1)
    nvalid = jnp.clip(blk_end - blk_start, 0, BM).astype(jnp.int32)
    nvalid = jnp.where(jnp.arange(nb) < n_used[0], nvalid, 0)

    lookup = lambda table: jnp.sum(jnp.where(blk_e[:, None] == experts, table, 0), axis=-1)
    nonempty = padded > 0
    order = jnp.cumsum(nonempty.astype(jnp.int32)) - 1
    later = lax.cummin(jnp.where(nonempty, experts, ne)[::-1])[::-1]
    next_e = jnp.concatenate([later[1:], jnp.full((1,), ne, jnp.int32)])
    blk_next = lookup(next_e)
    has_next = (blk_next < ne) & (jnp.arange(nb) < n_used[0])
    k_in_run = jnp.arange(nb, dtype=jnp.int32) - lookup(pstarts // BM)
    n_in_run = jnp.maximum(lookup(padded // BM), 1)
    common = dict(blk_e=blk_e, n_used=n_used, nvalid=nvalid, wslot=lookup(order) % 2,
                  wnext=jnp.where(has_next, blk_next, blk_e))

    def schedule(w):
        nch = w.shape[1] // _chunk_rows(w.shape[2])
        s = dict(common, wc0=jnp.where(has_next, k_in_run * nch // n_in_run, 0),
                 wc1=jnp.where(has_next, (k_in_run + 1) * nch // n_in_run, 0))
        return {k: v.astype(jnp.int32) for k, v in s.items()}

    src_tok = jnp.maximum(slot_assign, 0) % t
    act = _moe_up(h2s, src_tok, w_gate_up, b_gate_up, schedule(w_gate_up))
    ysc = _moe_down(act, slot_assign, n_assign, w_down, b_down, schedule(w_down))
    return _final(ysc, rt, x1, ada3, g_ffn_post, seq)


def kernel(x, c, w_ada, b_ada, g_mix_pre, g_mix_post, g_ffn_pre, g_ffn_post, w_in, g_rec_out, w_branch_rec,
           w_branch_att, w_o, w_router, b_router, w_gate_up, b_gate_up, w_down, b_down, rec_lb_table, rel_bias):
    bsz, seq, d = x.shape
    depth = w_in.shape[0]
    lb_all = jnp.cumsum(jax.nn.softmax(rec_lb_table.astype(F32), axis=1), axis=1)
    x2 = x.reshape(bsz * seq, d)
    for layer in range(depth):
        ada3 = _ada(c, w_ada[layer], b_ada[layer]).reshape(bsz, 6, d)
        x2 = _mixer_ffn_layer(x2, ada3, bsz, seq, g_mix_pre[layer], g_mix_post[layer], g_ffn_pre[layer],
                              g_ffn_post[layer], w_in[layer], lb_all[:, layer], g_rec_out[layer], rel_bias,
                              w_branch_rec[layer], w_branch_att[layer], w_o[layer], w_router[layer],
                              b_router[layer], w_gate_up[layer], b_gate_up[layer], w_down[layer],
                              b_down[layer])
    return x2.reshape(bsz, seq, d)
```

```python
import functools
import math

import numpy as np
import jax
import jax.numpy as jnp
from jax import lax
from jax.experimental import pallas as pl
from jax.experimental.pallas import tpu as pltpu

F32 = jnp.float32
BF16 = jnp.bfloat16

LANE = 128
SUBLANE = 8
SLAB_ROWS = 16
SLAB_PITCH = 24

REC_HEAD_DIM = 128
REC_CHUNK = 64
ATT_HEAD_DIM = 128
ATT_HEADS_PER_GROUP = 4
ATT_BLOCK = 64
DIL_GROUPS = ((128, 1), (512, 4), (2048, 16))
NUM_BUCKETS = 32
MAX_DISTANCE = 1024
N_EXPERTS = 32
TOP_K = 4
SWIGLU_LIMIT = 7.0
SWIGLU_ALPHA = 1.702
RMS_EPS = 1e-6
NEG_INF = -1e30

N_LEVELS = 6
W_CHUNK_BYTES = 4 * 1024 * 1024
CAST_ROWS = 64
ROW_PATHS = 4
HGRN_UNROLL = 4
DMA_UNROLL = 8
ATT_UNROLL = 8
BM = 256

_NT = (((1,), (1,)), ((), ()))
_TN = (((0,), (0,)), ((), ()))


def _cparams(sem, vmem_mb):
    return pltpu.CompilerParams(dimension_semantics=sem, vmem_limit_bytes=vmem_mb * 1024 * 1024)


def _rms(x):
    return x * lax.rsqrt(jnp.mean(x * x, axis=-1, keepdims=True) + RMS_EPS)


def _ada_body(c_ref, w_ref, b_ref, o_ref):
    c = c_ref[...]
    cond = (c * jax.nn.sigmoid(c)).astype(BF16)
    o_ref[...] = jnp.dot(cond, w_ref[...].astype(BF16), preferred_element_type=F32) + b_ref[...]


def _ada(c, w, b):
    bsz, d = c.shape
    n = w.shape[1]
    tn = 1024
    cp = jnp.zeros((SUBLANE, d), F32).at[:bsz].set(c)
    out = pl.pallas_call(
        _ada_body,
        grid=(n // tn,),
        in_specs=[pl.BlockSpec((SUBLANE, d), lambda j: (0, 0)),
                  pl.BlockSpec((d, tn), lambda j: (0, j)),
                  pl.BlockSpec((1, tn), lambda j: (0, j))],
        out_specs=pl.BlockSpec((SUBLANE, tn), lambda j: (0, j)),
        out_shape=jax.ShapeDtypeStruct((SUBLANE, n), F32),
        compiler_params=_cparams(("arbitrary",), 40),
        name="ada",
    )(cp, w, b.reshape(1, n))
    return out[:bsz]


def _inproj_body(x_ref, g_ref, ada_ref, w_ref, o_ref, h_ref):
    @pl.when(pl.program_id(1) == 0)
    def _():
        half = x_ref.shape[0] // 2
        sh = ada_ref[0, 0:1, :]
        sc = ada_ref[0, 1:2, :]
        for r0 in (0, half):
            y = _rms(x_ref[r0:r0 + half, :]) * g_ref[...]
            h_ref[r0:r0 + half, :] = (y * (1.0 + sc) + sh).astype(BF16)

    o_ref[...] = jnp.dot(h_ref[...], w_ref[...].astype(BF16), preferred_element_type=F32)


def _inproj(x2, g, ada3, w_in, seq, rot):
    t, d = x2.shape
    n = w_in.shape[1]
    tm, tn = 2048, 512
    per_b = seq // tm
    nj = n // tn
    assert rot % tn == 0 and n % tn == 0 and seq % tm == 0
    return pl.pallas_call(
        _inproj_body,
        grid=(t // tm, nj),
        in_specs=[pl.BlockSpec((tm, d), lambda i, j: (i, 0), pipeline_mode=pl.Buffered(1)),
                  pl.BlockSpec((1, d), lambda i, j: (0, 0)),
                  pl.BlockSpec((1, 6, d), lambda i, j: (i // per_b, 0, 0)),
                  pl.BlockSpec((d, tn), lambda i, j: (0, (j + rot // tn) % nj))],
        out_specs=pl.BlockSpec((tm, tn), lambda i, j: (i, j)),
        out_shape=jax.ShapeDtypeStruct((t, n), F32),
        scratch_shapes=[pltpu.VMEM((tm, d), BF16)],
        compiler_params=_cparams(("arbitrary", "arbitrary"), 56),
        name="inproj",
    )(x2, g.reshape(1, d), ada3, w_in)


def _hgrn_consts():
    c = REC_CHUNK
    r = np.arange(c)[:, None]
    m = np.arange(c)[None, :]
    nw = N_LEVELS - 1
    wf = np.zeros(((nw + 2) * c, c), np.float32)
    wb = np.zeros(((nw + 2) * c, c), np.float32)
    mf = np.zeros((N_LEVELS + 1, c, c), np.float32)
    for lvl in range(N_LEVELS):
        s = 32 >> lvl
        m0 = (r // (2 * s)) * (2 * s) + s
        up = r >= m0
        if lvl < nw:
            wf[lvl * c:(lvl + 1) * c] = np.where(up, (m >= m0) & (m <= r), (m > r) & (m <= m0 - 1))
            wb[lvl * c:(lvl + 1) * c] = np.where(up, (m >= m0) & (m <= r - 1), (m >= r) & (m <= m0 - 1))
        i = np.arange(c)[:, None]
        j = np.arange(c)[None, :]
        mf[lvl] = (i // (2 * s) == j // (2 * s)) & (i % (2 * s) >= s) & (j % (2 * s) < s)
    mf[N_LEVELS] = np.eye(c)
    wf[nw * c:(nw + 1) * c] = m <= r
    wf[(nw + 1) * c:(nw + 2) * c] = m > r
    wb[nw * c:(nw + 1) * c] = m >= r
    wb[(nw + 1) * c:(nw + 2) * c] = m < r
    mfb = mf + np.transpose(mf, (0, 2, 1))
    mfb[N_LEVELS] = np.eye(c)
    up = np.zeros((N_LEVELS, c, LANE), np.float32)
    for lvl in range(N_LEVELS):
        s = 32 >> lvl
        up[lvl] = ((np.arange(c) % (2 * s)) >= s)[:, None]
    wf3 = np.concatenate([wf, wf, wf], axis=1)
    wb3 = np.concatenate([wb, wb, wb], axis=1)
    return (jnp.asarray(wf3, BF16), jnp.asarray(wb3, BF16), jnp.asarray(mfb, F32), jnp.asarray(up, F32),
            jnp.asarray(1.0 - up, F32))


def _split3(g):
    hi = g.astype(BF16)
    r1 = g - hi.astype(F32)
    mid = r1.astype(BF16)
    lo = (r1 - mid.astype(F32)).astype(BF16)
    return jnp.concatenate([hi, mid, lo], axis=0)


def _hgrn_a_body(q_ref, i_ref, zf_ref, zb_ref, lb_ref, wf_ref, wb_ref, mf_ref, up_ref, lo_ref,
                 oi_ref, qtf_ref, qtb_ref, utf_ref, utb_ref, df_ref, db_ref, *, cpb):
    c = REC_CHUNK
    dirs = ((zf_ref, wf_ref, None, qtf_ref, utf_ref, df_ref, 0, c - 1),
            (zb_ref, wb_ref, None, qtb_ref, utb_ref, db_ref, 1, 0))

    def chunk_group(cg, carry):
        cis = [cg * HGRN_UNROLL + u for u in range(HGRN_UNROLL)]
        rows = [pl.ds(pl.multiple_of(ci * c, c), c) for ci in cis]
        zqs = [q_ref[rw, :] for rw in rows]
        qs = [zq * jax.nn.sigmoid(zq) for zq in zqs]
        vbs = [i_ref[rw, :].astype(BF16) for rw in rows]
        units = [(u, d) for u in range(HGRN_UNROLL) for d in range(2)]
        nw = N_LEVELS - 1
        ks, es, fs = {}, {}, {}
        for u, d in units:
            z_ref, w_ref = dirs[d][0], dirs[d][1]
            lb = lb_ref[d:d + 1, :]
            f = lb + (1.0 - lb) * jax.nn.sigmoid(z_ref[rows[u], :])
            fs[u, d] = f
            ks[u, d] = 1.0 - f
            es[u, d] = jnp.exp(jnp.dot(w_ref[...], _split3(jnp.log(f)), preferred_element_type=F32))
        acc = [jnp.zeros((c, c), F32) for _ in range(HGRN_UNROLL)]
        for lvl in range(N_LEVELS + 1):
            for u in range(HGRN_UNROLL):
                if lvl < N_LEVELS:
                    up, lo = up_ref[lvl], lo_ref[lvl]
                    if lvl < nw:
                        ef = es[u, 0][lvl * c:(lvl + 1) * c]
                        eb = es[u, 1][lvl * c:(lvl + 1) * c]
                    else:
                        ef = fs[u, 0] * up + lo
                        eb = fs[u, 1] * lo + up
                    qa = jnp.concatenate([(qs[u] * (ef * up)).astype(BF16), (qs[u] * (eb * lo)).astype(BF16)],
                                         axis=1)
                    ka = jnp.concatenate([(ks[u, 0] * (ef * lo)).astype(BF16),
                                          (ks[u, 1] * (eb * up)).astype(BF16)], axis=1)
                else:
                    qa, ka = qs[u].astype(BF16), (ks[u, 0] + ks[u, 1]).astype(BF16)
                p = lax.dot_general(qa, ka, _NT, preferred_element_type=F32)
                acc[u] = acc[u] + p * mf_ref[lvl]
        for u, d in units:
            _, _, _, qt_ref, ut_ref, d_ref, _, drow = dirs[d]
            e = es[u, d]
            qt_ref[rows[u], :] = (qs[u] * e[nw * c:(nw + 1) * c]).astype(BF16)
            kt = (ks[u, d] * e[(nw + 1) * c:(nw + 2) * c]).astype(BF16)
            ut_ref[cis[u]] = lax.dot_general(vbs[u], kt, _TN, preferred_element_type=F32)
            d_ref[pl.ds(cis[u], 1), :] = e[nw * c + drow:nw * c + drow + 1]
        for u in range(HGRN_UNROLL):
            oi_ref[rows[u], :] = jnp.dot(acc[u].astype(BF16), vbs[u], preferred_element_type=F32)
        return carry

    lax.fori_loop(0, cpb // HGRN_UNROLL, chunk_group, 0)


def _hgrn_a(proj, lb, col, t, d_rec):
    heads = d_rec // REC_HEAD_DIM
    tq = 1024
    cpb = tq // REC_CHUNK
    nchunks = t // REC_CHUNK
    wf, wb, mf, up, lo = _hgrn_consts()
    hd = REC_HEAD_DIM

    def colspec(off):
        return pl.BlockSpec((tq, hd), lambda i, h: (i, off + h))

    full2 = lambda i, h: (0, 0)
    full3 = lambda i, h: (0, 0, 0)
    row_spec = pl.BlockSpec((tq, hd), lambda i, h: (i, h))
    u_spec = pl.BlockSpec((cpb, hd, hd), lambda i, h: (i, 0, h))
    d_spec = pl.BlockSpec((cpb, hd), lambda i, h: (i, h))
    return pl.pallas_call(
        functools.partial(_hgrn_a_body, cpb=cpb),
        grid=(t // tq, heads),
        in_specs=[colspec(col["q_r"]), colspec(col["i_r"]), colspec(col["zf_f"]), colspec(col["zf_b"]),
                  pl.BlockSpec((2, hd), lambda i, h: (0, h)),
                  pl.BlockSpec(wf.shape, full2), pl.BlockSpec(wb.shape, full2),
                  pl.BlockSpec(mf.shape, full3), pl.BlockSpec(up.shape, full3), pl.BlockSpec(lo.shape, full3)],
        out_specs=[row_spec, row_spec, row_spec, u_spec, u_spec, d_spec, d_spec],
        out_shape=[jax.ShapeDtypeStruct((t, d_rec), F32),
                   jax.ShapeDtypeStruct((t, d_rec), BF16),
                   jax.ShapeDtypeStruct((t, d_rec), BF16),
                   jax.ShapeDtypeStruct((nchunks, hd, d_rec), F32),
                   jax.ShapeDtypeStruct((nchunks, hd, d_rec), F32),
                   jax.ShapeDtypeStruct((nchunks, d_rec), F32),
                   jax.ShapeDtypeStruct((nchunks, d_rec), F32)],
        compiler_params=_cparams(("arbitrary", "arbitrary"), 32),
        name="hgrn_a",
    )(proj, proj, proj, proj, lb, wf, wb, mf, up, lo)


def _hgrn_c_body(oi_ref, qtf_ref, qtb_ref, utf_ref, utb_ref, df_ref, db_ref, z_ref, g_ref,
                 out_ref, acc_ref, accb_ref, *, nchunks):
    c = REC_CHUNK
    hd = REC_HEAD_DIM

    unroll = 4

    def step(i, carry):
        st_f, st_b = carry
        pending = []
        for u in range(unroll):
            nf = i * unroll + u
            nb = nchunks - 1 - nf
            rows_f = pl.ds(pl.multiple_of(nf * c, c), c)
            rows_b = pl.ds(pl.multiple_of(nb * c, c), c)
            of = lax.dot_general(qtf_ref[rows_f, :], st_f.astype(BF16), _NT, preferred_element_type=F32)
            ob = lax.dot_general(qtb_ref[rows_b, :], st_b.astype(BF16), _NT, preferred_element_type=F32)
            pending.append((rows_f, rows_b, oi_ref[rows_f, :] + of, ob))
            st_f = df_ref[pl.ds(nf, 1), :] * st_f + utf_ref[nf]
            st_b = db_ref[pl.ds(nb, 1), :] * st_b + utb_ref[nb]
        for rows_f, rows_b, vf, vb in pending:
            acc_ref[rows_f, :] = vf
            accb_ref[rows_b, :] = vb
        return st_f, st_b

    zero = jnp.zeros((hd, hd), F32)
    lax.fori_loop(0, nchunks // unroll, step, (zero, zero))

    o = _rms(acc_ref[...] + accb_ref[...])
    out_ref[...] = (o * g_ref[...] * jax.nn.sigmoid(z_ref[...])).astype(BF16)


def _hgrn_c(oi, qtf, qtb, utf, utb, df, db, proj, g_out, col, bsz, seq, d_rec):
    heads = d_rec // REC_HEAD_DIM
    hd = REC_HEAD_DIM
    nchunks = seq // REC_CHUNK
    row_spec = pl.BlockSpec((seq, hd), lambda b, h: (b, h))
    u_spec = pl.BlockSpec((nchunks, hd, hd), lambda b, h: (b, 0, h))
    d_spec = pl.BlockSpec((nchunks, hd), lambda b, h: (b, h))
    zo = col["z_o"]
    return pl.pallas_call(
        functools.partial(_hgrn_c_body, nchunks=nchunks),
        grid=(bsz, heads),
        in_specs=[row_spec, row_spec, row_spec, u_spec, u_spec, d_spec, d_spec,
                  pl.BlockSpec((seq, hd), lambda b, h: (b, zo + h)),
                  pl.BlockSpec((1, hd), lambda b, h: (0, h))],
        out_specs=row_spec,
        out_shape=jax.ShapeDtypeStruct((bsz * seq, d_rec), BF16),
        scratch_shapes=[pltpu.VMEM((seq, hd), F32), pltpu.VMEM((seq, hd), F32)],
        compiler_params=_cparams(("arbitrary", "arbitrary"), 48),
        name="hgrn_c",
    )(oi, qtf, qtb, utf, utb, df, db, proj, g_out.reshape(1, d_rec))


def _t5_bucket(rel):
    half_buckets = NUM_BUCKETS // 2
    ret = np.where(rel > 0, half_buckets, 0)
    n = np.abs(rel)
    max_exact = half_buckets // 2
    nf = np.maximum(n, 1).astype(np.float32)
    large = max_exact + (np.log(nf / np.float32(max_exact)) / np.float32(math.log(MAX_DISTANCE / max_exact))
                         * np.float32(half_buckets - max_exact)).astype(np.int32)
    large = np.minimum(large, half_buckets - 1)
    return ret + np.where(n < max_exact, n, large)


def _band_bias(rel_bias_g, window, dil):
    half = window // (2 * dil)
    q_off = np.arange(ATT_BLOCK)[:, None]
    rel = np.arange(3 * ATT_BLOCK)[None, :] - ATT_BLOCK - q_off
    onehot = (_t5_bucket(rel * dil)[..., None] == np.arange(NUM_BUCKETS)).astype(np.float32)
    bias = jnp.einsum("qkb,bh->hqk", jnp.asarray(onehot), rel_bias_g.astype(F32),
                      precision=lax.Precision.HIGHEST)
    return jnp.where(jnp.asarray(np.abs(rel) <= half)[None], bias, NEG_INF)


def _attn_body(q_ref, kp_ref, k_ref, kn_ref, vp_ref, v_ref, vn_ref, bias_ref,
               num_ref, st_ref, kc_ref, vc_ref, *, dil, tq, sub_len):
    blk = ATT_BLOCK
    nqb = tq // blk
    n = pl.program_id(1)
    scale = ATT_HEAD_DIM ** -0.5

    def sds(start, size):
        if dil == 1:
            return pl.ds(start, size)
        return pl.ds(start, size, stride=dil)

    cu = kc_ref.shape[0]
    qu = ATT_UNROLL // cu

    def deinterleave(r, j):
        kc_ref[j, 0:blk, :] = kp_ref[sds(r, blk), :].astype(BF16)
        kc_ref[j, blk:blk + tq, :] = k_ref[sds(r, tq), :].astype(BF16)
        kc_ref[j, blk + tq:2 * blk + tq, :] = kn_ref[sds(r, blk), :].astype(BF16)
        vc_ref[j, 0:blk, :] = vp_ref[sds(r, blk), :].astype(BF16)
        vc_ref[j, blk:blk + tq, :] = v_ref[sds(r, tq), :].astype(BF16)
        vc_ref[j, blk + tq:2 * blk + tq, :] = vn_ref[sds(r, blk), :].astype(BF16)

    def units(r0, qb0):
        us = [(j, u) for j in range(cu) for u in range(qu)]
        q0s = [pl.multiple_of((qb0 + u) * blk, blk) for _, u in us]
        rows = [sds(r0 + j + dil * q0, blk) for (j, _), q0 in zip(us, q0s)]
        lane = lax.broadcasted_iota(jnp.int32, (blk, LANE), 1)
        key_iota = lax.broadcasted_iota(jnp.int32, (1, 3 * blk), 1)
        bias = bias_ref[0]
        qs = [q_ref[rw, :].astype(BF16) for rw in rows]
        kws = [kc_ref[j, pl.ds(q0, 3 * blk), :] for (j, _), q0 in zip(us, q0s)]
        vws = [vc_ref[j, pl.ds(q0, 3 * blk), :] for (j, _), q0 in zip(us, q0s)]
        ss = [lax.dot_general(q, kw, _NT, preferred_element_type=F32) * scale for q, kw in zip(qs, kws)]
        valids = []
        for q0 in q0s:
            kpos = n * tq + q0 - blk + key_iota
            valids.append((kpos >= 0) & (kpos < sub_len))
        ss = [jnp.where(valid, s + bias, NEG_INF) for s, valid in zip(ss, valids)]
        ms = [jnp.max(s, axis=-1, keepdims=True) for s in ss]
        ps = [jnp.exp(s - m) for s, m in zip(ss, ms)]
        ls = [jnp.sum(p, axis=-1, keepdims=True) for p in ps]
        nums = [jnp.dot(p.astype(BF16), vw, preferred_element_type=F32) for p, vw in zip(ps, vws)]
        for rw, num, m, l in zip(rows, nums, ms, ls):
            num_ref[rw, :] = num
            st_ref[rw, :] = jnp.where(lane < LANE // 2, m, l)

    def class_group(rg, carry):
        r0 = rg * cu
        for j in range(cu):
            deinterleave(r0 + j, j)

        def qgroup(qg, carry2):
            units(r0, qg * qu)
            return carry2

        lax.fori_loop(0, nqb // qu, qgroup, 0)
        return carry

    lax.fori_loop(0, dil // cu, class_group, 0)


def _attn_group(proj, bias, col, g, dil, bsz, seq):
    tile = 1024
    tq = tile // dil
    halo = ATT_BLOCK * dil
    sub_len = seq // dil
    cu = ATT_UNROLL // min(tq // ATT_BLOCK, ATT_UNROLL)
    nh = ATT_HEADS_PER_GROUP
    hd = ATT_HEAD_DIM
    qc = col["q_a"] + g * nh
    kc = col["k_a"] + g * nh
    vc = col["v_a"] + g * nh
    tiles_b = seq // tile
    halos_b = seq // halo
    hpt = tile // halo

    own = lambda c: pl.BlockSpec((tile, hd), lambda b, n, h: (b * tiles_b + n, c + h))
    prev = lambda c: pl.BlockSpec(
        (halo, hd), lambda b, n, h: (b * halos_b + jnp.maximum(n * hpt - 1, 0), c + h))
    nxt = lambda c: pl.BlockSpec(
        (halo, hd), lambda b, n, h: (b * halos_b + jnp.minimum((n + 1) * hpt, halos_b - 1), c + h))
    t = bsz * seq
    return pl.pallas_call(
        functools.partial(_attn_body, dil=dil, tq=tq, sub_len=sub_len),
        grid=(bsz, tiles_b, nh),
        in_specs=[own(qc), prev(kc), own(kc), nxt(kc), prev(vc), own(vc), nxt(vc),
                  pl.BlockSpec((1,) + bias.shape[1:], lambda b, n, h: (h, 0, 0))],
        out_specs=[pl.BlockSpec((tile, hd), lambda b, n, h: (b * tiles_b + n, h)),
                   pl.BlockSpec((tile, LANE), lambda b, n, h: (b * tiles_b + n, h))],
        out_shape=[jax.ShapeDtypeStruct((t, nh * hd), F32), jax.ShapeDtypeStruct((t, nh * LANE), F32)],
        scratch_shapes=[pltpu.VMEM((cu, tq + 2 * ATT_BLOCK, hd), BF16),
                        pltpu.VMEM((cu, tq + 2 * ATT_BLOCK, hd), BF16)],
        compiler_params=_cparams(("arbitrary", "arbitrary", "arbitrary"), 32),
        name=f"attn_d{dil}",
    )(proj, proj, proj, proj, proj, proj, proj, bias)


def _merge_body(rec_ref, n0_ref, n1_ref, n2_ref, s0_ref, s1_ref, s2_ref, zgr_ref, zga_ref, x_ref,
                ada_ref, gpost_ref, gpre_ref, wbr_ref, wba_ref, wo_ref, wr_ref, br_ref,
                x1_ref, h2_ref, lg_ref):
    nh = ATT_HEADS_PER_GROUP
    hd = ATT_HEAD_DIM
    half = LANE // 2
    lane = lax.broadcasted_iota(jnp.int32, (rec_ref.shape[0], LANE), 1)
    heads = []
    for h in range(nh):
        cols = slice(h * hd, (h + 1) * hd)
        st = [s[:, cols] for s in (s0_ref, s1_ref, s2_ref)]
        top = jnp.maximum(jnp.maximum(st[0], st[1]), st[2])
        ws = [jnp.exp(s - top) for s in st]
        den = (ws[0] * pltpu.roll(st[0], half, 1) + ws[1] * pltpu.roll(st[1], half, 1)
               + ws[2] * pltpu.roll(st[2], half, 1))
        coef = [w / den for w in ws]
        coef = [jnp.where(lane < half, c, pltpu.roll(c, half, 1)) for c in coef]
        num = coef[0] * n0_ref[:, cols] + coef[1] * n1_ref[:, cols] + coef[2] * n2_ref[:, cols]
        heads.append(num.astype(BF16))
    att = jnp.concatenate(heads, axis=1)
    y_rec = jnp.dot(rec_ref[...], wbr_ref[...], preferred_element_type=F32)
    y_att = jnp.dot(att, wba_ref[...], preferred_element_type=F32)
    merged = jax.nn.sigmoid(zgr_ref[...]) * y_rec + jax.nn.sigmoid(zga_ref[...]) * y_att
    y = jnp.dot(merged.astype(BF16), wo_ref[...], preferred_element_type=F32)
    gt_m = ada_ref[0, 2:3, :]
    sh_f = ada_ref[0, 3:4, :]
    sc_f = ada_ref[0, 4:5, :]
    x1 = x_ref[...] + gt_m * (_rms(y) * gpost_ref[...])
    x1_ref[...] = x1
    h2 = _rms(x1) * gpre_ref[...] * (1.0 + sc_f) + sh_f
    tm = h2.shape[0]
    for s in range(SLAB_ROWS):
        h2_ref[pl.ds(s, tm, stride=SLAB_ROWS), :] = h2[:, s * LANE:(s + 1) * LANE]
    ne = lg_ref.shape[1]
    h_hi = h2.astype(BF16)
    h_lo = (h2 - h_hi.astype(F32)).astype(BF16)
    both = jnp.dot(h_hi, wr_ref[...], preferred_element_type=F32)
    cross = jnp.dot(h_lo, wr_ref[:, 0:ne], preferred_element_type=F32)
    lg_ref[...] = both[:, 0:ne] + both[:, ne:2 * ne] + cross + br_ref[...]


def _merge(rec_o, nums, stats, proj, x2, ada3, g_post, g_pre, wbr, wba, wo, w_router, b_router, col, seq):
    t, d = x2.shape
    tm = 256
    per_b = seq // tm
    d_rec = rec_o.shape[1]
    w_att = nums[0].shape[1]
    ne = w_router.shape[1]
    wr_hi = w_router.astype(BF16)
    wr_lo = (w_router - wr_hi.astype(F32)).astype(BF16)
    w_router = jnp.concatenate([wr_hi, wr_lo], axis=1)
    dl = d // LANE
    row = lambda w: pl.BlockSpec((tm, w), lambda i: (i, 0))
    const = lambda shape: pl.BlockSpec(shape, lambda i: (0,) * len(shape), pipeline_mode=pl.Buffered(1))
    zgr = col["zg_rec"] // dl
    zga = col["zg_att"] // dl
    return pl.pallas_call(
        _merge_body,
        grid=(t // tm,),
        in_specs=[row(d_rec), row(w_att), row(w_att), row(w_att), row(w_att), row(w_att), row(w_att),
                  pl.BlockSpec((tm, d), lambda i: (i, zgr)),
                  pl.BlockSpec((tm, d), lambda i: (i, zga)),
                  row(d),
                  pl.BlockSpec((1, 6, d), lambda i: (i // per_b, 0, 0)),
                  const((1, d)), const((1, d)),
                  const(wbr.shape), const(wba.shape), const(wo.shape), const(w_router.shape),
                  const((1, ne))],
        out_specs=[row(d), pl.BlockSpec((tm * SLAB_ROWS, LANE), lambda i: (i, 0)),
                   pl.BlockSpec((tm, ne), lambda i: (i, 0))],
        out_shape=[jax.ShapeDtypeStruct((t, d), F32), jax.ShapeDtypeStruct((t * SLAB_ROWS, LANE), F32),
                   jax.ShapeDtypeStruct((t, ne), F32)],
        compiler_params=_cparams(("arbitrary",), 56),
        name="merge",
    )(rec_o, nums[0], nums[1], nums[2], stats[0], stats[1], stats[2], proj, proj, x2, ada3,
      g_post.reshape(1, d), g_pre.reshape(1, d), wbr, wba, wo, w_router, b_router.reshape(1, ne))


def _route_body(lg_ref, tri_ref, rt_ref, cnt_ref, carry_ref):
    i = pl.program_id(0)
    tr, ne = lg_ref.shape

    @pl.when(i == 0)
    def _():
        carry_ref[...] = jnp.zeros_like(carry_ref)

    l = lg_ref[...]
    lane = lax.broadcasted_iota(jnp.int32, (tr, ne), 1).astype(F32)
    vals, sels, idxs = [], [], []
    for _ in range(TOP_K):
        m = jnp.max(l, axis=-1, keepdims=True)
        idx = jnp.min(jnp.where(l == m, lane, float(ne)), axis=-1, keepdims=True)
        sel = lane == idx
        vals.append(m)
        idxs.append(idx)
        sels.append(sel)
        l = jnp.where(sel, -jnp.inf, l)
    es = [jnp.exp(v - vals[0]) for v in vals]
    tot = es[0] + es[1] + es[2] + es[3]
    chosen = (sels[0] | sels[1] | sels[2] | sels[3]).astype(F32)
    prefix = jnp.dot(tri_ref[...], chosen.astype(BF16), preferred_element_type=F32) + carry_ref[0:1, :]
    out_lane = lax.broadcasted_iota(jnp.int32, (tr, LANE), 1)
    rt = jnp.zeros((tr, LANE), F32)
    for k in range(TOP_K):
        rank = jnp.sum(jnp.where(sels[k], prefix, 0.0), axis=-1, keepdims=True)
        rt = jnp.where(out_lane == k, idxs[k], rt)
        rt = jnp.where(out_lane == TOP_K + k, es[k] / tot, rt)
        rt = jnp.where(out_lane == 2 * TOP_K + k, rank, rt)
    rt_ref[...] = rt
    new = carry_ref[0:1, :] + jnp.sum(chosen, axis=0, keepdims=True)
    carry_ref[...] = jnp.broadcast_to(new, carry_ref.shape)
    cnt_ref[...] = carry_ref[...]


def _route(logits):
    t, ne = logits.shape
    tr = 512
    tri = jnp.asarray(np.tril(np.ones((tr, tr), np.float32), -1), BF16)
    return pl.pallas_call(
        _route_body,
        grid=(t // tr,),
        in_specs=[pl.BlockSpec((tr, ne), lambda i: (i, 0)),
                  pl.BlockSpec((tr, tr), lambda i: (0, 0))],
        out_specs=[pl.BlockSpec((tr, LANE), lambda i: (i, 0)),
                   pl.BlockSpec((SUBLANE, ne), lambda i: (0, 0))],
        out_shape=[jax.ShapeDtypeStruct((t, LANE), F32), jax.ShapeDtypeStruct((SUBLANE, ne), F32)],
        scratch_shapes=[pltpu.VMEM((SUBLANE, ne), F32)],
        compiler_params=_cparams(("arbitrary",), 32),
        name="route",
    )(logits, tri)


def _for_row_count(nvalid, compute):
    q = BM // ROW_PATHS
    for i in range(1, ROW_PATHS + 1):
        lo, hi = (i - 1) * q, i * q
        cond = (nvalid > lo) & (nvalid <= hi) if i > 1 else (nvalid <= hi)

        @pl.when(cond)
        def _(m=hi):
            compute(m)


def _chunk_rows(ncols):
    return W_CHUNK_BYTES // (4 * ncols)


def _weight_stream(w_hbm, wbf, stage, wsem, e, slot, c0, c1, priority=0):
    kc = stage.shape[1]

    def copy(c):
        return pltpu.make_async_copy(w_hbm.at[e, pl.ds(pl.multiple_of(c * kc, kc), kc), :],
                                     stage.at[c % 2], wsem.at[c % 2])

    def prime():
        def body(c, carry):
            copy(c).start(priority=priority)
            return carry
        lax.fori_loop(c0, jnp.minimum(c0 + 2, c1), body, 0)

    def finish():
        def body(c, carry):
            copy(c).wait()
            buf = c % 2

            def cast(i, carry2):
                r = pl.multiple_of(i * CAST_ROWS, CAST_ROWS)
                wbf[slot, pl.ds(pl.multiple_of(c * kc, kc) + r, CAST_ROWS), :] = (
                    stage[buf, pl.ds(r, CAST_ROWS), :].astype(BF16))
                return carry2
            lax.fori_loop(0, kc // CAST_ROWS, cast, 0)

            @pl.when(c + 2 < c1)
            def _():
                copy(c + 2).start(priority=priority)
            return carry
        lax.fori_loop(c0, c1, body, 0)

    return prime, finish


def _moe_up_body(be_ref, nu_ref, ws_ref, wn_ref, wc0_ref, wc1_ref, nv_ref, idx0_ref, idxn_ref, h2s_ref, w_hbm,
                 bias_ref, o_ref, xbuf, wbf, stage, sem, wsem):
    b = pl.program_id(0)
    nu = nu_ref[0]
    f = o_ref.shape[1]
    slab = SLAB_ROWS
    nch = wbf.shape[1] // stage.shape[1]

    def row_start(idx_ref, slot, r):
        tok = idx_ref[0, 0, r]
        pltpu.make_async_copy(h2s_ref.at[pl.ds(pl.multiple_of(tok * slab, slab), slab), :],
                              xbuf.at[slot, pl.ds(pl.multiple_of(r * SLAB_PITCH, SUBLANE), slab), :],
                              sem.at[slot]).start()

    def wait_rows(slot):
        pltpu.make_async_copy(h2s_ref.at[pl.ds(0, BM * slab), :], xbuf.at[slot, pl.ds(0, BM * slab), :],
                              sem.at[slot]).wait()

    def issue(idx_ref, slot):
        def body(r, c):
            row_start(idx_ref, slot, r)
            return c
        lax.fori_loop(0, BM, body, 0, unroll=DMA_UNROLL)

    @pl.when(b == 0)
    def _():
        issue(idx0_ref, 0)
        prime0, finish0 = _weight_stream(w_hbm, wbf, stage, wsem, be_ref[0], ws_ref[0], 0, nch)
        prime0()
        finish0()

    @pl.when(b + 1 < nu)
    def _():
        issue(idxn_ref, (b + 1) % 2)

    @pl.when(b < nu)
    def _():
        wslot = ws_ref[b]
        prime, finish = _weight_stream(w_hbm, wbf, stage, wsem, wn_ref[b], 1 - wslot, wc0_ref[b], wc1_ref[b],
                                       priority=1)
        prime()
        slot = b % 2
        wait_rows(slot)

        def compute(m):
            x = jnp.concatenate([xbuf[slot, pl.ds(s, m, stride=SLAB_PITCH), :].astype(BF16) for s in range(slab)],
                                axis=1)
            half = f // 2
            for c0 in (0, half):
                gate = (jnp.dot(x, wbf[wslot, :, c0:c0 + half], preferred_element_type=F32)
                        + bias_ref[0, :, c0:c0 + half])
                up = (jnp.dot(x, wbf[wslot, :, f + c0:f + c0 + half], preferred_element_type=F32)
                      + bias_ref[0, :, f + c0:f + c0 + half])
                gate = jnp.minimum(gate, SWIGLU_LIMIT)
                up = jnp.clip(up, -SWIGLU_LIMIT, SWIGLU_LIMIT)
                o_ref[0:m, c0:c0 + half] = (gate * jax.nn.sigmoid(SWIGLU_ALPHA * gate) * (up + 1.0)).astype(BF16)
            if m < BM:
                o_ref[m:BM, :] = jnp.zeros((BM - m, f), BF16)

        _for_row_count(nv_ref[b], compute)
        finish()

    @pl.when(b >= nu)
    def _():
        o_ref[...] = jnp.zeros_like(o_ref)


def _moe_up(h2s, src_tok, wgu, bgu, sched):
    ne, d, f2 = wgu.shape
    f = f2 // 2
    p = src_tok.shape[0]
    nb = p // BM
    idx3 = src_tok.reshape(nb, 1, BM)
    smem_blk = lambda imap: pl.BlockSpec((1, 1, BM), imap, memory_space=pltpu.SMEM)
    grid_spec = pltpu.PrefetchScalarGridSpec(
        num_scalar_prefetch=7,
        grid=(nb,),
        in_specs=[smem_blk(lambda b, be, *_: (0, 0, 0)),
                  smem_blk(lambda b, be, *_: (jnp.minimum(b + 1, nb - 1), 0, 0)),
                  pl.BlockSpec(memory_space=pl.ANY),
                  pl.BlockSpec(memory_space=pl.ANY),
                  pl.BlockSpec((1, 1, f2), lambda b, be, *_: (be[b], 0, 0))],
        out_specs=pl.BlockSpec((BM, f), lambda b, be, *_: (b, 0)),
        scratch_shapes=[pltpu.VMEM((2, BM * SLAB_PITCH, LANE), F32),
                        pltpu.VMEM((2, d, f2), BF16),
                        pltpu.VMEM((2, _chunk_rows(f2), f2), F32),
                        pltpu.SemaphoreType.DMA((2,)), pltpu.SemaphoreType.DMA((2,))],
    )
    return pl.pallas_call(
        _moe_up_body,
        grid_spec=grid_spec,
        out_shape=jax.ShapeDtypeStruct((p, f), BF16),
        compiler_params=_cparams(("arbitrary",), 58),
        name="moe_up",
    )(sched["blk_e"], sched["n_used"], sched["wslot"], sched["wnext"], sched["wc0"], sched["wc1"],
      sched["nvalid"], idx3, idx3, h2s, wgu, bgu.reshape(ne, 1, f2))


def _moe_down_body(be_ref, nu_ref, ws_ref, wn_ref, wc0_ref, wc1_ref, nv_ref, dst_ref, a_ref, w_hbm, bias_ref,
                   ysc_ref, ybuf, wbf, stage, sem, wsem):
    b = pl.program_id(0)
    nb = pl.num_programs(0)
    nu = nu_ref[0]
    slab = SLAB_ROWS
    nch = wbf.shape[1] // stage.shape[1]

    @pl.when(b == 0)
    def _():
        prime0, finish0 = _weight_stream(w_hbm, wbf, stage, wsem, be_ref[0], ws_ref[0], 0, nch)
        prime0()
        finish0()

    def row_copy(slot, r, d):
        return pltpu.make_async_copy(ybuf.at[slot, pl.ds(pl.multiple_of(r * SLAB_PITCH, SUBLANE), slab), :],
                                     ysc_ref.at[pl.ds(pl.multiple_of(d * slab, slab), slab), :],
                                     sem.at[slot])

    def drain(step):
        slot = step % 2
        count = nv_ref[step]

        @pl.when(count == BM)
        def _():
            pltpu.make_async_copy(ybuf.at[slot, pl.ds(0, BM * slab), :], ysc_ref.at[pl.ds(0, BM * slab), :],
                                  sem.at[slot]).wait()

        @pl.when(count < BM)
        def _():
            def body(r, c):
                row_copy(slot, 0, 0).wait()
                return c
            lax.fori_loop(0, count, body, 0)

    @pl.when((b >= 2) & (b < nu))
    def _():
        drain(b - 2)

    @pl.when(b < nu)
    def _():
        slot = b % 2
        wslot = ws_ref[b]
        prime, finish = _weight_stream(w_hbm, wbf, stage, wsem, wn_ref[b], 1 - wslot, wc0_ref[b], wc1_ref[b])
        prime()
        def compute(m):
            y = jnp.dot(a_ref[0:m, :], wbf[wslot], preferred_element_type=F32) + bias_ref[0]
            for s in range(slab):
                ybuf[slot, pl.ds(s, m, stride=SLAB_PITCH), :] = y[:, s * LANE:(s + 1) * LANE]

        _for_row_count(nv_ref[b], compute)
        finish()

        def body(r, c):
            row_copy(slot, r, dst_ref[0, 0, r]).start()
            return c

        def pair(i, c):
            row_copy(slot, 2 * i, dst_ref[0, 0, 2 * i]).start(priority=0)
            row_copy(slot, 2 * i + 1, dst_ref[0, 0, 2 * i + 1]).start(priority=1)
            return c

        @pl.when(nv_ref[b] == BM)
        def _():
            lax.fori_loop(0, BM // 2, pair, 0, unroll=DMA_UNROLL // 2)

        @pl.when(nv_ref[b] < BM)
        def _():
            lax.fori_loop(0, nv_ref[b], body, 0)

    @pl.when(b == nb - 1)
    def _():
        @pl.when(nu >= 2)
        def _():
            drain(nu - 2)
        drain(nu - 1)


def _moe_down(act, dst_slot, n_out_rows, wd, bd, sched):
    p, f = act.shape
    ne, _, d = wd.shape
    nb = p // BM
    grid_spec = pltpu.PrefetchScalarGridSpec(
        num_scalar_prefetch=7,
        grid=(nb,),
        in_specs=[pl.BlockSpec((1, 1, BM), lambda b, be, *_: (b, 0, 0), memory_space=pltpu.SMEM),
                  pl.BlockSpec((BM, f), lambda b, be, *_: (b, 0)),
                  pl.BlockSpec(memory_space=pl.ANY),
                  pl.BlockSpec((1, 1, d), lambda b, be, *_: (be[b], 0, 0))],
        out_specs=pl.BlockSpec(memory_space=pl.ANY),
        scratch_shapes=[pltpu.VMEM((2, BM * SLAB_PITCH, LANE), F32),
                        pltpu.VMEM((2, f, d), BF16),
                        pltpu.VMEM((2, _chunk_rows(d), d), F32),
                        pltpu.SemaphoreType.DMA((2,)), pltpu.SemaphoreType.DMA((2,))],
    )
    return pl.pallas_call(
        _moe_down_body,
        grid_spec=grid_spec,
        out_shape=jax.ShapeDtypeStruct((n_out_rows * SLAB_ROWS, LANE), F32),
        compiler_params=pltpu.CompilerParams(dimension_semantics=("arbitrary",),
                                             vmem_limit_bytes=48 * 1024 * 1024, has_side_effects=True),
        name="moe_down",
    )(sched["blk_e"], sched["n_used"], sched["wslot"], sched["wnext"], sched["wc0"], sched["wc1"],
      sched["nvalid"], dst_slot.reshape(nb, 1, BM), act, wd, bd.reshape(ne, 1, d))


def _final_body(y0_ref, y1_ref, y2_ref, y3_ref, rt_ref, x1_ref, ada_ref, g_ref, o_ref):
    tm = x1_ref.shape[0]
    rt = rt_ref[...]
    y_refs = (y0_ref, y1_ref, y2_ref, y3_ref)
    pieces = []
    for s in range(SLAB_ROWS):
        acc = rt[:, TOP_K:TOP_K + 1] * y_refs[0][pl.ds(s, tm, stride=SLAB_ROWS), :]
        for k in range(1, TOP_K):
            acc = acc + rt[:, TOP_K + k:TOP_K + k + 1] * y_refs[k][pl.ds(s, tm, stride=SLAB_ROWS), :]
        pieces.append(acc)
    y = jnp.concatenate(pieces, axis=1)
    gt_f = ada_ref[0, 5:6, :]
    o_ref[...] = x1_ref[...] + gt_f * (_rms(y) * g_ref[...])


def _final(ysc, rt, x1, ada3, g_post, seq):
    t, d = x1.shape
    tm = 256
    per_b = seq // tm
    nt = t // tm
    assert TOP_K == 4
    slot_spec = lambda k: pl.BlockSpec((tm * SLAB_ROWS, LANE), lambda i: (k * nt + i, 0))
    return pl.pallas_call(
        _final_body,
        grid=(nt,),
        in_specs=[slot_spec(0), slot_spec(1), slot_spec(2), slot_spec(3),
                  pl.BlockSpec((tm, LANE), lambda i: (i, 0)),
                  pl.BlockSpec((tm, d), lambda i: (i, 0)),
                  pl.BlockSpec((1, 6, d), lambda i: (i // per_b, 0, 0)),
                  pl.BlockSpec((1, d), lambda i: (0, 0))],
        out_specs=pl.BlockSpec((tm, d), lambda i: (i, 0)),
        out_shape=jax.ShapeDtypeStruct((t, d), F32),
        compiler_params=_cparams(("arbitrary",), 48),
        name="final",
    )(ysc, ysc, ysc, ysc, rt, x1, ada3, g_post.reshape(1, d))


def _mixer_ffn_layer(x2, ada3, bsz, seq, g_mix_pre, g_mix_post, g_ffn_pre, g_ffn_post, w_in, lb, g_rec_out,
                     rel_bias, w_branch_rec, w_branch_att, w_o, w_router, b_router, w_gate_up, b_gate_up,
                     w_down, b_down):
    t, d = x2.shape
    d_rec = w_branch_rec.shape[0]
    w_att = w_branch_att.shape[0]
    d_att = 3 * w_att
    widths = dict(q_r=d_rec, i_r=d_rec, zf_f=d_rec, zf_b=d_rec, z_o=d_rec, q_a=d_att, k_a=d_att, v_a=d_att,
                  zg_rec=d, zg_att=d)
    my_order = ("zg_rec", "zg_att", "q_r", "i_r", "zf_f", "zf_b", "z_o", "q_a", "k_a", "v_a")
    col, acc = {}, 0
    for name in my_order:
        col[name] = acc // LANE
        acc += widths[name]
    rot = acc - 2 * d

    proj = _inproj(x2, g_mix_pre, ada3, w_in, seq, rot)

    oi, qtf, qtb, utf, utb, df, db = _hgrn_a(proj, lb, col, t, d_rec)
    rec_o = _hgrn_c(oi, qtf, qtb, utf, utb, df, db, proj, g_rec_out, col, bsz, seq, d_rec)

    nums, stats = [], []
    for g, (window, dil) in enumerate(DIL_GROUPS):
        hs = slice(g * ATT_HEADS_PER_GROUP, (g + 1) * ATT_HEADS_PER_GROUP)
        bias = _band_bias(rel_bias[:, hs], window, dil)
        num, st = _attn_group(proj, bias, col, g, dil, bsz, seq)
        nums.append(num)
        stats.append(st)

    x1, h2s, logits = _merge(rec_o, nums, stats, proj, x2, ada3, g_mix_post, g_ffn_pre,
                            w_branch_rec.astype(BF16), w_branch_att.astype(BF16), w_o.astype(BF16),
                            w_router, b_router, col, seq)

    rt, cnt = _route(logits)
    ne = logits.shape[1]
    counts = cnt[0].astype(jnp.int32)
    top_idx = rt[:, 0:TOP_K].astype(jnp.int32)
    rank = rt[:, 2 * TOP_K:3 * TOP_K].astype(jnp.int32)
    padded = (counts + BM - 1) // BM * BM
    pends = jnp.cumsum(padded)
    pstarts = pends - padded
    experts = jnp.arange(ne, dtype=jnp.int32)
    pstart_sel = jnp.sum(jnp.where(top_idx[..., None] == experts, pstarts, 0), axis=-1)
    dest = (pstart_sel + rank).T.reshape(-1)
    p_rows = t * TOP_K + ne * BM
    nb = p_rows // BM
    blk_start = jnp.arange(nb, dtype=jnp.int32) * BM
    blk_e = jnp.minimum(jnp.sum((pends[None, :] <= blk_start[:, None]).astype(jnp.int32), axis=1), ne - 1)
    n_used = (pends[-1:] // BM).astype(jnp.int32)

    n_assign = t * TOP_K
    slot_assign = jnp.full((p_rows,), -1, jnp.int32).at[dest].set(jnp.arange(n_assign, dtype=jnp.int32))
    valid_end = pstarts + counts
    blk_end = jnp.sum(jnp.where(blk_e[:, None] == experts, valid_end, 0), axis=-1)
    nvalid = jnp.clip(blk_end - blk_start, 0, BM).astype(jnp.int32)
    nvalid = jnp.where(jnp.arange(nb) < n_used[0], nvalid, 0)

    lookup = lambda table: jnp.sum(jnp.where(blk_e[:, None] == experts, table, 0), axis=-1)
    nonempty = padded > 0
    order = jnp.cumsum(nonempty.astype(jnp.int32)) - 1
    later = lax.cummin(jnp.where(nonempty, experts, ne)[::-1])[::-1]
    next_e = jnp.concatenate([later[1:], jnp.full((1,), ne, jnp.int32)])
    blk_next = lookup(next_e)
    has_next = (blk_next < ne) & (jnp.arange(nb) < n_used[0])
    k_in_run = jnp.arange(nb, dtype=jnp.int32) - lookup(pstarts // BM)
    n_in_run = jnp.maximum(lookup(padded // BM), 1)
    common = dict(blk_e=blk_e, n_used=n_used, nvalid=nvalid, wslot=lookup(order) % 2,
                  wnext=jnp.where(has_next, blk_next, blk_e))

    def schedule(w):
        nch = w.shape[1] // _chunk_rows(w.shape[2])
        s = dict(common, wc0=jnp.where(has_next, k_in_run * nch // n_in_run, 0),
                 wc1=jnp.where(has_next, (k_in_run + 1) * nch // n_in_run, 0))
        return {k: v.astype(jnp.int32) for k, v in s.items()}

    src_tok = jnp.maximum(slot_assign, 0) % t
    act = _moe_up(h2s, src_tok, w_gate_up, b_gate_up, schedule(w_gate_up))
    ysc = _moe_down(act, slot_assign, n_assign, w_down, b_down, schedule(w_down))
    return _final(ysc, rt, x1, ada3, g_ffn_post, seq)


def kernel(x, c, w_ada, b_ada, g_mix_pre, g_mix_post, g_ffn_pre, g_ffn_post, w_in, g_rec_out, w_branch_rec,
           w_branch_att, w_o, w_router, b_router, w_gate_up, b_gate_up, w_down, b_down, rec_lb_table, rel_bias):
    bsz, seq, d = x.shape
    depth = w_in.shape[0]
    lb_all = jnp.cumsum(jax.nn.softmax(rec_lb_table.astype(F32), axis=1), axis=1)
    x2 = x.reshape(bsz * seq, d)
    for layer in range(depth):
        ada3 = _ada(c, w_ada[layer], b_ada[layer]).reshape(bsz, 6, d)
        x2 = _mixer_ffn_layer(x2, ada3, bsz, seq, g_mix_pre[layer], g_mix_post[layer], g_ffn_pre[layer],
                              g_ffn_post[layer], w_in[layer], lb_all[:, layer], g_rec_out[layer], rel_bias,
                              w_branch_rec[layer], w_branch_att[layer], w_o[layer], w_router[layer],
                              b_router[layer], w_gate_up[layer], b_gate_up[layer], w_down[layer],
                              b_down[layer])
    return x2.reshape(bsz, seq, d)
```

```python
import functools
import math

import numpy as np
import jax
import jax.numpy as jnp
from jax import lax
from jax.experimental import pallas as pl
from jax.experimental.pallas import tpu as pltpu

F32 = jnp.float32
BF16 = jnp.bfloat16

LANE = 128
SUBLANE = 8
SLAB_ROWS = 16
SLAB_PITCH = 24

REC_HEAD_DIM = 128
REC_CHUNK = 64
ATT_HEAD_DIM = 128
ATT_HEADS_PER_GROUP = 4
ATT_BLOCK = 64
DIL_GROUPS = ((128, 1), (512, 4), (2048, 16))
NUM_BUCKETS = 32
MAX_DISTANCE = 1024
N_EXPERTS = 32
TOP_K = 4
SWIGLU_LIMIT = 7.0
SWIGLU_ALPHA = 1.702
RMS_EPS = 1e-6
NEG_INF = -1e30

N_LEVELS = 6
W_CHUNK_BYTES = 4 * 1024 * 1024
CAST_ROWS = 64
ROW_PATHS = 4
HGRN_UNROLL = 4
DMA_UNROLL = 8
ATT_UNROLL = 8
BM = 256

_NT = (((1,), (1,)), ((), ()))
_TN = (((0,), (0,)), ((), ()))


def _cparams(sem, vmem_mb):
    return pltpu.CompilerParams(dimension_semantics=sem, vmem_limit_bytes=vmem_mb * 1024 * 1024)


def _rms(x):
    return x * lax.rsqrt(jnp.mean(x * x, axis=-1, keepdims=True) + RMS_EPS)


def _ada_body(c_ref, w_ref, b_ref, o_ref):
    c = c_ref[...]
    cond = (c * jax.nn.sigmoid(c)).astype(BF16)
    o_ref[...] = jnp.dot(cond, w_ref[...].astype(BF16), preferred_element_type=F32) + b_ref[...]


def _ada(c, w, b):
    bsz, d = c.shape
    n = w.shape[1]
    tn = 1024
    cp = jnp.zeros((SUBLANE, d), F32).at[:bsz].set(c)
    out = pl.pallas_call(
        _ada_body,
        grid=(n // tn,),
        in_specs=[pl.BlockSpec((SUBLANE, d), lambda j: (0, 0)),
                  pl.BlockSpec((d, tn), lambda j: (0, j)),
                  pl.BlockSpec((1, tn), lambda j: (0, j))],
        out_specs=pl.BlockSpec((SUBLANE, tn), lambda j: (0, j)),
        out_shape=jax.ShapeDtypeStruct((SUBLANE, n), F32),
        compiler_params=_cparams(("arbitrary",), 40),
        name="ada",
    )(cp, w, b.reshape(1, n))
    return out[:bsz]


def _inproj_body(x_ref, g_ref, ada_ref, w_ref, o_ref, h_ref):
    @pl.when(pl.program_id(1) == 0)
    def _():
        half = x_ref.shape[0] // 2
        sh = ada_ref[0, 0:1, :]
        sc = ada_ref[0, 1:2, :]
        for r0 in (0, half):
            y = _rms(x_ref[r0:r0 + half, :]) * g_ref[...]
            h_ref[r0:r0 + half, :] = (y * (1.0 + sc) + sh).astype(BF16)

    o_ref[...] = jnp.dot(h_ref[...], w_ref[...].astype(BF16), preferred_element_type=F32)


def _inproj(x2, g, ada3, w_in, seq, rot):
    t, d = x2.shape
    n = w_in.shape[1]
    tm, tn = 2048, 512
    per_b = seq // tm
    nj = n // tn
    assert rot % tn == 0 and n % tn == 0 and seq % tm == 0
    return pl.pallas_call(
        _inproj_body,
        grid=(t // tm, nj),
        in_specs=[pl.BlockSpec((tm, d), lambda i, j: (i, 0), pipeline_mode=pl.Buffered(1)),
                  pl.BlockSpec((1, d), lambda i, j: (0, 0)),
                  pl.BlockSpec((1, 6, d), lambda i, j: (i // per_b, 0, 0)),
                  pl.BlockSpec((d, tn), lambda i, j: (0, (j + rot // tn) % nj))],
        out_specs=pl.BlockSpec((tm, tn), lambda i, j: (i, j)),
        out_shape=jax.ShapeDtypeStruct((t, n), F32),
        scratch_shapes=[pltpu.VMEM((tm, d), BF16)],
        compiler_params=_cparams(("arbitrary", "arbitrary"), 56),
        name="inproj",
    )(x2, g.reshape(1, d), ada3, w_in)


def _hgrn_consts():
    c = REC_CHUNK
    r = np.arange(c)[:, None]
    m = np.arange(c)[None, :]
    nw = N_LEVELS - 1
    wf = np.zeros(((nw + 2) * c, c), np.float32)
    wb = np.zeros(((nw + 2) * c, c), np.float32)
    mf = np.zeros((N_LEVELS + 1, c, c), np.float32)
    for lvl in range(N_LEVELS):
        s = 32 >> lvl
        m0 = (r // (2 * s)) * (2 * s) + s
        up = r >= m0
        if lvl < nw:
            wf[lvl * c:(lvl + 1) * c] = np.where(up, (m >= m0) & (m <= r), (m > r) & (m <= m0 - 1))
            wb[lvl * c:(lvl + 1) * c] = np.where(up, (m >= m0) & (m <= r - 1), (m >= r) & (m <= m0 - 1))
        i = np.arange(c)[:, None]
        j = np.arange(c)[None, :]
        mf[lvl] = (i // (2 * s) == j // (2 * s)) & (i % (2 * s) >= s) & (j % (2 * s) < s)
    mf[N_LEVELS] = np.eye(c)
    wf[nw * c:(nw + 1) * c] = m <= r
    wf[(nw + 1) * c:(nw + 2) * c] = m > r
    wb[nw * c:(nw + 1) * c] = m >= r
    wb[(nw + 1) * c:(nw + 2) * c] = m < r
    mfb = mf + np.transpose(mf, (0, 2, 1))
    mfb[N_LEVELS] = np.eye(c)
    up = np.zeros((N_LEVELS, c, LANE), np.float32)
    for lvl in range(N_LEVELS):
        s = 32 >> lvl
        up[lvl] = ((np.arange(c) % (2 * s)) >= s)[:, None]
    wf3 = np.concatenate([wf, wf, wf], axis=1)
    wb3 = np.concatenate([wb, wb, wb], axis=1)
    return (jnp.asarray(wf3, BF16), jnp.asarray(wb3, BF16), jnp.asarray(mfb, F32), jnp.asarray(up, F32),
            jnp.asarray(1.0 - up, F32))


def _split3(g):
    hi = g.astype(BF16)
    r1 = g - hi.astype(F32)
    mid = r1.astype(BF16)
    lo = (r1 - mid.astype(F32)).astype(BF16)
    return jnp.concatenate([hi, mid, lo], axis=0)


def _hgrn_a_body(q_ref, i_ref, zf_ref, zb_ref, lb_ref, wf_ref, wb_ref, mf_ref, up_ref, lo_ref,
                 oi_ref, qtf_ref, qtb_ref, utf_ref, utb_ref, df_ref, db_ref, *, cpb):
    c = REC_CHUNK
    dirs = ((zf_ref, wf_ref, None, qtf_ref, utf_ref, df_ref, 0, c - 1),
            (zb_ref, wb_ref, None, qtb_ref, utb_ref, db_ref, 1, 0))

    def chunk_group(cg, carry):
        cis = [cg * HGRN_UNROLL + u for u in range(HGRN_UNROLL)]
        rows = [pl.ds(pl.multiple_of(ci * c, c), c) for ci in cis]
        zqs = [q_ref[rw, :] for rw in rows]
        qs = [zq * jax.nn.sigmoid(zq) for zq in zqs]
        vbs = [i_ref[rw, :].astype(BF16) for rw in rows]
        units = [(u, d) for u in range(HGRN_UNROLL) for d in range(2)]
        nw = N_LEVELS - 1
        ks, es, fs = {}, {}, {}
        for u, d in units:
            z_ref, w_ref = dirs[d][0], dirs[d][1]
            lb = lb_ref[d:d + 1, :]
            f = lb + (1.0 - lb) * jax.nn.sigmoid(z_ref[rows[u], :])
            fs[u, d] = f
            ks[u, d] = 1.0 - f
            es[u, d] = jnp.exp(jnp.dot(w_ref[...], _split3(jnp.log(f)), preferred_element_type=F32))
        acc = [jnp.zeros((c, c), F32) for _ in range(HGRN_UNROLL)]
        for lvl in range(N_LEVELS + 1):
            for u in range(HGRN_UNROLL):
                if lvl < N_LEVELS:
                    up, lo = up_ref[lvl], lo_ref[lvl]
                    if lvl < nw:
                        ef = es[u, 0][lvl * c:(lvl + 1) * c]
                        eb = es[u, 1][lvl * c:(lvl + 1) * c]
                    else:
                        ef = fs[u, 0] * up + lo
                        eb = fs[u, 1] * lo + up
                    qa = jnp.concatenate([(qs[u] * (ef * up)).astype(BF16), (qs[u] * (eb * lo)).astype(BF16)],
                                         axis=1)
                    ka = jnp.concatenate([(ks[u, 0] * (ef * lo)).astype(BF16),
                                          (ks[u, 1] * (eb * up)).astype(BF16)], axis=1)
                else:
                    qa, ka = qs[u].astype(BF16), (ks[u, 0] + ks[u, 1]).astype(BF16)
                p = lax.dot_general(qa, ka, _NT, preferred_element_type=F32)
                acc[u] = acc[u] + p * mf_ref[lvl]
        for u, d in units:
            _, _, _, qt_ref, ut_ref, d_ref, _, drow = dirs[d]
            e = es[u, d]
            qt_ref[rows[u], :] = (qs[u] * e[nw * c:(nw + 1) * c]).astype(BF16)
            kt = (ks[u, d] * e[(nw + 1) * c:(nw + 2) * c]).astype(BF16)
            ut_ref[cis[u]] = lax.dot_general(vbs[u], kt, _TN, preferred_element_type=F32)
            d_ref[pl.ds(cis[u], 1), :] = e[nw * c + drow:nw * c + drow + 1]
        for u in range(HGRN_UNROLL):
            oi_ref[rows[u], :] = jnp.dot(acc[u].astype(BF16), vbs[u], preferred_element_type=F32)
        return carry

    lax.fori_loop(0, cpb // HGRN_UNROLL, chunk_group, 0)


def _hgrn_a(proj, lb, col, t, d_rec):
    heads = d_rec // REC_HEAD_DIM
    tq = 1024
    cpb = tq // REC_CHUNK
    nchunks = t // REC_CHUNK
    wf, wb, mf, up, lo = _hgrn_consts()
    hd = REC_HEAD_DIM

    def colspec(off):
        return pl.BlockSpec((tq, hd), lambda i, h: (i, off + h))

    full2 = lambda i, h: (0, 0)
    full3 = lambda i, h: (0, 0, 0)
    row_spec = pl.BlockSpec((tq, hd), lambda i, h: (i, h))
    u_spec = pl.BlockSpec((cpb, hd, hd), lambda i, h: (i, 0, h))
    d_spec = pl.BlockSpec((cpb, hd), lambda i, h: (i, h))
    return pl.pallas_call(
        functools.partial(_hgrn_a_body, cpb=cpb),
        grid=(t // tq, heads),
        in_specs=[colspec(col["q_r"]), colspec(col["i_r"]), colspec(col["zf_f"]), colspec(col["zf_b"]),
                  pl.BlockSpec((2, hd), lambda i, h: (0, h)),
                  pl.BlockSpec(wf.shape, full2), pl.BlockSpec(wb.shape, full2),
                  pl.BlockSpec(mf.shape, full3), pl.BlockSpec(up.shape, full3), pl.BlockSpec(lo.shape, full3)],
        out_specs=[row_spec, row_spec, row_spec, u_spec, u_spec, d_spec, d_spec],
        out_shape=[jax.ShapeDtypeStruct((t, d_rec), F32),
                   jax.ShapeDtypeStruct((t, d_rec), BF16),
                   jax.ShapeDtypeStruct((t, d_rec), BF16),
                   jax.ShapeDtypeStruct((nchunks, hd, d_rec), F32),
                   jax.ShapeDtypeStruct((nchunks, hd, d_rec), F32),
                   jax.ShapeDtypeStruct((nchunks, d_rec), F32),
                   jax.ShapeDtypeStruct((nchunks, d_rec), F32)],
        compiler_params=_cparams(("arbitrary", "arbitrary"), 32),
        name="hgrn_a",
    )(proj, proj, proj, proj, lb, wf, wb, mf, up, lo)


def _hgrn_c_body(oi_ref, qtf_ref, qtb_ref, utf_ref, utb_ref, df_ref, db_ref, z_ref, g_ref,
                 out_ref, acc_ref, accb_ref, *, nchunks):
    c = REC_CHUNK
    hd = REC_HEAD_DIM

    unroll = 4

    def step(i, carry):
        st_f, st_b = carry
        pending = []
        for u in range(unroll):
            nf = i * unroll + u
            nb = nchunks - 1 - nf
            rows_f = pl.ds(pl.multiple_of(nf * c, c), c)
            rows_b = pl.ds(pl.multiple_of(nb * c, c), c)
            of = lax.dot_general(qtf_ref[rows_f, :], st_f.astype(BF16), _NT, preferred_element_type=F32)
            ob = lax.dot_general(qtb_ref[rows_b, :], st_b.astype(BF16), _NT, preferred_element_type=F32)
            pending.append((rows_f, rows_b, oi_ref[rows_f, :] + of, ob))
            st_f = df_ref[pl.ds(nf, 1), :] * st_f + utf_ref[nf]
            st_b = db_ref[pl.ds(nb, 1), :] * st_b + utb_ref[nb]
        for rows_f, rows_b, vf, vb in pending:
            acc_ref[rows_f, :] = vf
            accb_ref[rows_b, :] = vb
        return st_f, st_b

    zero = jnp.zeros((hd, hd), F32)
    lax.fori_loop(0, nchunks // unroll, step, (zero, zero))

    o = _rms(acc_ref[...] + accb_ref[...])
    out_ref[...] = (o * g_ref[...] * jax.nn.sigmoid(z_ref[...])).astype(BF16)


def _hgrn_c(oi, qtf, qtb, utf, utb, df, db, proj, g_out, col, bsz, seq, d_rec):
    heads = d_rec // REC_HEAD_DIM
    hd = REC_HEAD_DIM
    nchunks = seq // REC_CHUNK
    row_spec = pl.BlockSpec((seq, hd), lambda b, h: (b, h))
    u_spec = pl.BlockSpec((nchunks, hd, hd), lambda b, h: (b, 0, h))
    d_spec = pl.BlockSpec((nchunks, hd), lambda b, h: (b, h))
    zo = col["z_o"]
    return pl.pallas_call(
        functools.partial(_hgrn_c_body, nchunks=nchunks),
        grid=(bsz, heads),
        in_specs=[row_spec, row_spec, row_spec, u_spec, u_spec, d_spec, d_spec,
                  pl.BlockSpec((seq, hd), lambda b, h: (b, zo + h)),
                  pl.BlockSpec((1, hd), lambda b, h: (0, h))],
        out_specs=row_spec,
        out_shape=jax.ShapeDtypeStruct((bsz * seq, d_rec), BF16),
        scratch_shapes=[pltpu.VMEM((seq, hd), F32), pltpu.VMEM((seq, hd), F32)],
        compiler_params=_cparams(("arbitrary", "arbitrary"), 48),
        name="hgrn_c",
    )(oi, qtf, qtb, utf, utb, df, db, proj, g_out.reshape(1, d_rec))


def _t5_bucket(rel):
    half_buckets = NUM_BUCKETS // 2
    ret = np.where(rel > 0, half_buckets, 0)
    n = np.abs(rel)
    max_exact = half_buckets // 2
    nf = np.maximum(n, 1).astype(np.float32)
    large = max_exact + (np.log(nf / np.float32(max_exact)) / np.float32(math.log(MAX_DISTANCE / max_exact))
                         * np.float32(half_buckets - max_exact)).astype(np.int32)
    large = np.minimum(large, half_buckets - 1)
    return ret + np.where(n < max_exact, n, large)


def _band_bias(rel_bias_g, window, dil):
    half = window // (2 * dil)
    q_off = np.arange(ATT_BLOCK)[:, None]
    rel = np.arange(3 * ATT_BLOCK)[None, :] - ATT_BLOCK - q_off
    onehot = (_t5_bucket(rel * dil)[..., None] == np.arange(NUM_BUCKETS)).astype(np.float32)
    bias = jnp.einsum("qkb,bh->hqk", jnp.asarray(onehot), rel_bias_g.astype(F32),
                      precision=lax.Precision.HIGHEST)
    return jnp.where(jnp.asarray(np.abs(rel) <= half)[None], bias, NEG_INF)


def _attn_body(q_ref, kp_ref, k_ref, kn_ref, vp_ref, v_ref, vn_ref, bias_ref,
               num_ref, st_ref, kc_ref, vc_ref, *, dil, tq, sub_len):
    blk = ATT_BLOCK
    nqb = tq // blk
    n = pl.program_id(1)
    scale = ATT_HEAD_DIM ** -0.5

    def sds(start, size):
        if dil == 1:
            return pl.ds(start, size)
        return pl.ds(start, size, stride=dil)

    cu = kc_ref.shape[0]
    qu = ATT_UNROLL // cu

    def deinterleave(r, j):
        kc_ref[j, 0:blk, :] = kp_ref[sds(r, blk), :].astype(BF16)
        kc_ref[j, blk:blk + tq, :] = k_ref[sds(r, tq), :].astype(BF16)
        kc_ref[j, blk + tq:2 * blk + tq, :] = kn_ref[sds(r, blk), :].astype(BF16)
        vc_ref[j, 0:blk, :] = vp_ref[sds(r, blk), :].astype(BF16)
        vc_ref[j, blk:blk + tq, :] = v_ref[sds(r, tq), :].astype(BF16)
        vc_ref[j, blk + tq:2 * blk + tq, :] = vn_ref[sds(r, blk), :].astype(BF16)

    def units(r0, qb0):
        us = [(j, u) for j in range(cu) for u in range(qu)]
        q0s = [pl.multiple_of((qb0 + u) * blk, blk) for _, u in us]
        rows = [sds(r0 + j + dil * q0, blk) for (j, _), q0 in zip(us, q0s)]
        lane = lax.broadcasted_iota(jnp.int32, (blk, LANE), 1)
        key_iota = lax.broadcasted_iota(jnp.int32, (1, 3 * blk), 1)
        bias = bias_ref[0]
        qs = [q_ref[rw, :].astype(BF16) for rw in rows]
        kws = [kc_ref[j, pl.ds(q0, 3 * blk), :] for (j, _), q0 in zip(us, q0s)]
        vws = [vc_ref[j, pl.ds(q0, 3 * blk), :] for (j, _), q0 in zip(us, q0s)]
        ss = [lax.dot_general(q, kw, _NT, preferred_element_type=F32) * scale for q, kw in zip(qs, kws)]
        valids = []
        for q0 in q0s:
            kpos = n * tq + q0 - blk + key_iota
            valids.append((kpos >= 0) & (kpos < sub_len))
        ss = [jnp.where(valid, s + bias, NEG_INF) for s, valid in zip(ss, valids)]
        ms = [jnp.max(s, axis=-1, keepdims=True) for s in ss]
        ps = [jnp.exp(s - m) for s, m in zip(ss, ms)]
        ls = [jnp.sum(p, axis=-1, keepdims=True) for p in ps]
        nums = [jnp.dot(p.astype(BF16), vw, preferred_element_type=F32) for p, vw in zip(ps, vws)]
        for rw, num, m, l in zip(rows, nums, ms, ls):
            num_ref[rw, :] = num
            st_ref[rw, :] = jnp.where(lane < LANE // 2, m, l)

    def class_group(rg, carry):
        r0 = rg * cu
        for j in range(cu):
            deinterleave(r0 + j, j)

        def qgroup(qg, carry2):
            units(r0, qg * qu)
            return carry2

        lax.fori_loop(0, nqb // qu, qgroup, 0)
        return carry

    lax.fori_loop(0, dil // cu, class_group, 0)


def _attn_group(proj, bias, col, g, dil, bsz, seq):
    tile = 1024
    tq = tile // dil
    halo = ATT_BLOCK * dil
    sub_len = seq // dil
    cu = ATT_UNROLL // min(tq // ATT_BLOCK, ATT_UNROLL)
    nh = ATT_HEADS_PER_GROUP
    hd = ATT_HEAD_DIM
    qc = col["q_a"] + g * nh
    kc = col["k_a"] + g * nh
    vc = col["v_a"] + g * nh
    tiles_b = seq // tile
    halos_b = seq // halo
    hpt = tile // halo

    own = lambda c: pl.BlockSpec((tile, hd), lambda b, n, h: (b * tiles_b + n, c + h))
    prev = lambda c: pl.BlockSpec(
        (halo, hd), lambda b, n, h: (b * halos_b + jnp.maximum(n * hpt - 1, 0), c + h))
    nxt = lambda c: pl.BlockSpec(
        (halo, hd), lambda b, n, h: (b * halos_b + jnp.minimum((n + 1) * hpt, halos_b - 1), c + h))
    t = bsz * seq
    return pl.pallas_call(
        functools.partial(_attn_body, dil=dil, tq=tq, sub_len=sub_len),
        grid=(bsz, tiles_b, nh),
        in_specs=[own(qc), prev(kc), own(kc), nxt(kc), prev(vc), own(vc), nxt(vc),
                  pl.BlockSpec((1,) + bias.shape[1:], lambda b, n, h: (h, 0, 0))],
        out_specs=[pl.BlockSpec((tile, hd), lambda b, n, h: (b * tiles_b + n, h)),
                   pl.BlockSpec((tile, LANE), lambda b, n, h: (b * tiles_b + n, h))],
        out_shape=[jax.ShapeDtypeStruct((t, nh * hd), F32), jax.ShapeDtypeStruct((t, nh * LANE), F32)],
        scratch_shapes=[pltpu.VMEM((cu, tq + 2 * ATT_BLOCK, hd), BF16),
                        pltpu.VMEM((cu, tq + 2 * ATT_BLOCK, hd), BF16)],
        compiler_params=_cparams(("arbitrary", "arbitrary", "arbitrary"), 32),
        name=f"attn_d{dil}",
    )(proj, proj, proj, proj, proj, proj, proj, bias)


def _merge_body(rec_ref, n0_ref, n1_ref, n2_ref, s0_ref, s1_ref, s2_ref, zgr_ref, zga_ref, x_ref,
                ada_ref, gpost_ref, gpre_ref, wbr_ref, wba_ref, wo_ref, wr_ref, br_ref,
                x1_ref, h2_ref, lg_ref):
    nh = ATT_HEADS_PER_GROUP
    hd = ATT_HEAD_DIM
    half = LANE // 2
    lane = lax.broadcasted_iota(jnp.int32, (rec_ref.shape[0], LANE), 1)
    heads = []
    for h in range(nh):
        cols = slice(h * hd, (h + 1) * hd)
        st = [s[:, cols] for s in (s0_ref, s1_ref, s2_ref)]
        top = jnp.maximum(jnp.maximum(st[0], st[1]), st[2])
        ws = [jnp.exp(s - top) for s in st]
        den = (ws[0] * pltpu.roll(st[0], half, 1) + ws[1] * pltpu.roll(st[1], half, 1)
               + ws[2] * pltpu.roll(st[2], half, 1))
        coef = [w / den for w in ws]
        coef = [jnp.where(lane < half, c, pltpu.roll(c, half, 1)) for c in coef]
        num = coef[0] * n0_ref[:, cols] + coef[1] * n1_ref[:, cols] + coef[2] * n2_ref[:, cols]
        heads.append(num.astype(BF16))
    att = jnp.concatenate(heads, axis=1)
    y_rec = jnp.dot(rec_ref[...], wbr_ref[...], preferred_element_type=F32)
    y_att = jnp.dot(att, wba_ref[...], preferred_element_type=F32)
    merged = jax.nn.sigmoid(zgr_ref[...]) * y_rec + jax.nn.sigmoid(zga_ref[...]) * y_att
    y = jnp.dot(merged.astype(BF16), wo_ref[...], preferred_element_type=F32)
    gt_m = ada_ref[0, 2:3, :]
    sh_f = ada_ref[0, 3:4, :]
    sc_f = ada_ref[0, 4:5, :]
    x1 = x_ref[...] + gt_m * (_rms(y) * gpost_ref[...])
    x1_ref[...] = x1
    h2 = _rms(x1) * gpre_ref[...] * (1.0 + sc_f) + sh_f
    tm = h2.shape[0]
    for s in range(SLAB_ROWS):
        h2_ref[pl.ds(s, tm, stride=SLAB_ROWS), :] = h2[:, s * LANE:(s + 1) * LANE]
    ne = lg_ref.shape[1]
    h_hi = h2.astype(BF16)
    h_lo = (h2 - h_hi.astype(F32)).astype(BF16)
    both = jnp.dot(h_hi, wr_ref[...], preferred_element_type=F32)
    cross = jnp.dot(h_lo, wr_ref[:, 0:ne], preferred_element_type=F32)
    lg_ref[...] = both[:, 0:ne] + both[:, ne:2 * ne] + cross + br_ref[...]


def _merge(rec_o, nums, stats, proj, x2, ada3, g_post, g_pre, wbr, wba, wo, w_router, b_router, col, seq):
    t, d = x2.shape
    tm = 256
    per_b = seq // tm
    d_rec = rec_o.shape[1]
    w_att = nums[0].shape[1]
    ne = w_router.shape[1]
    wr_hi = w_router.astype(BF16)
    wr_lo = (w_router - wr_hi.astype(F32)).astype(BF16)
    w_router = jnp.concatenate([wr_hi, wr_lo], axis=1)
    dl = d // LANE
    row = lambda w: pl.BlockSpec((tm, w), lambda i: (i, 0))
    const = lambda shape: pl.BlockSpec(shape, lambda i: (0,) * len(shape), pipeline_mode=pl.Buffered(1))
    zgr = col["zg_rec"] // dl
    zga = col["zg_att"] // dl
    return pl.pallas_call(
        _merge_body,
        grid=(t // tm,),
        in_specs=[row(d_rec), row(w_att), row(w_att), row(w_att), row(w_att), row(w_att), row(w_att),
                  pl.BlockSpec((tm, d), lambda i: (i, zgr)),
                  pl.BlockSpec((tm, d), lambda i: (i, zga)),
                  row(d),
                  pl.BlockSpec((1, 6, d), lambda i: (i // per_b, 0, 0)),
                  const((1, d)), const((1, d)),
                  const(wbr.shape), const(wba.shape), const(wo.shape), const(w_router.shape),
                  const((1, ne))],
        out_specs=[row(d), pl.BlockSpec((tm * SLAB_ROWS, LANE), lambda i: (i, 0)),
                   pl.BlockSpec((tm, ne), lambda i: (i, 0))],
        out_shape=[jax.ShapeDtypeStruct((t, d), F32), jax.ShapeDtypeStruct((t * SLAB_ROWS, LANE), F32),
                   jax.ShapeDtypeStruct((t, ne), F32)],
        compiler_params=_cparams(("arbitrary",), 56),
        name="merge",
    )(rec_o, nums[0], nums[1], nums[2], stats[0], stats[1], stats[2], proj, proj, x2, ada3,
      g_post.reshape(1, d), g_pre.reshape(1, d), wbr, wba, wo, w_router, b_router.reshape(1, ne))


def _route_body(lg_ref, tri_ref, rt_ref, cnt_ref, carry_ref):
    i = pl.program_id(0)
    tr, ne = lg_ref.shape

    @pl.when(i == 0)
    def _():
        carry_ref[...] = jnp.zeros_like(carry_ref)

    l = lg_ref[...]
    lane = lax.broadcasted_iota(jnp.int32, (tr, ne), 1).astype(F32)
    vals, sels, idxs = [], [], []
    for _ in range(TOP_K):
        m = jnp.max(l, axis=-1, keepdims=True)
        idx = jnp.min(jnp.where(l == m, lane, float(ne)), axis=-1, keepdims=True)
        sel = lane == idx
        vals.append(m)
        idxs.append(idx)
        sels.append(sel)
        l = jnp.where(sel, -jnp.inf, l)
    es = [jnp.exp(v - vals[0]) for v in vals]
    tot = es[0] + es[1] + es[2] + es[3]
    chosen = (sels[0] | sels[1] | sels[2] | sels[3]).astype(F32)
    prefix = jnp.dot(tri_ref[...], chosen.astype(BF16), preferred_element_type=F32) + carry_ref[0:1, :]
    out_lane = lax.broadcasted_iota(jnp.int32, (tr, LANE), 1)
    rt = jnp.zeros((tr, LANE), F32)
    for k in range(TOP_K):
        rank = jnp.sum(jnp.where(sels[k], prefix, 0.0), axis=-1, keepdims=True)
        rt = jnp.where(out_lane == k, idxs[k], rt)
        rt = jnp.where(out_lane == TOP_K + k, es[k] / tot, rt)
        rt = jnp.where(out_lane == 2 * TOP_K + k, rank, rt)
    rt_ref[...] = rt
    new = carry_ref[0:1, :] + jnp.sum(chosen, axis=0, keepdims=True)
    carry_ref[...] = jnp.broadcast_to(new, carry_ref.shape)
    cnt_ref[...] = carry_ref[...]


def _route(logits):
    t, ne = logits.shape
    tr = 512
    tri = jnp.asarray(np.tril(np.ones((tr, tr), np.float32), -1), BF16)
    return pl.pallas_call(
        _route_body,
        grid=(t // tr,),
        in_specs=[pl.BlockSpec((tr, ne), lambda i: (i, 0)),
                  pl.BlockSpec((tr, tr), lambda i: (0, 0))],
        out_specs=[pl.BlockSpec((tr, LANE), lambda i: (i, 0)),
                   pl.BlockSpec((SUBLANE, ne), lambda i: (0, 0))],
        out_shape=[jax.ShapeDtypeStruct((t, LANE), F32), jax.ShapeDtypeStruct((SUBLANE, ne), F32)],
        scratch_shapes=[pltpu.VMEM((SUBLANE, ne), F32)],
        compiler_params=_cparams(("arbitrary",), 32),
        name="route",
    )(logits, tri)


def _invert_body(dest_ref, out_ref, *, chunk):
    base = pl.program_id(0) * chunk

    @pl.when(pl.program_id(0) == 0)
    def _():
        def fill(p, c):
            out_ref[p] = -1
            return c
        lax.fori_loop(0, out_ref.shape[0], fill, 0, unroll=DMA_UNROLL)

    def body(j, c):
        out_ref[dest_ref[0, 0, j]] = base + j
        return c
    lax.fori_loop(0, chunk, body, 0, unroll=DMA_UNROLL)


def _invert_slots(dest, p_rows):
    n = dest.shape[0]
    chunk = 2048
    nch = n // chunk
    return pl.pallas_call(
        functools.partial(_invert_body, chunk=chunk),
        grid=(nch,),
        in_specs=[pl.BlockSpec((1, 1, chunk), lambda i: (i, 0, 0), memory_space=pltpu.SMEM)],
        out_specs=pl.BlockSpec(memory_space=pltpu.SMEM),
        out_shape=jax.ShapeDtypeStruct((p_rows,), jnp.int32),
        compiler_params=pltpu.CompilerParams(dimension_semantics=("arbitrary",)),
        name="invert_slots",
    )(dest.reshape(nch, 1, chunk))


def _for_row_count(nvalid, compute):
    q = BM // ROW_PATHS
    for i in range(1, ROW_PATHS + 1):
        lo, hi = (i - 1) * q, i * q
        cond = (nvalid > lo) & (nvalid <= hi) if i > 1 else (nvalid <= hi)

        @pl.when(cond)
        def _(m=hi):
            compute(m)


def _chunk_rows(ncols):
    return W_CHUNK_BYTES // (4 * ncols)


def _weight_stream(w_hbm, wbf, stage, wsem, e, slot, c0, c1, priority=0):
    kc = stage.shape[1]

    def copy(c):
        return pltpu.make_async_copy(w_hbm.at[e, pl.ds(pl.multiple_of(c * kc, kc), kc), :],
                                     stage.at[c % 2], wsem.at[c % 2])

    def prime():
        def body(c, carry):
            copy(c).start(priority=priority)
            return carry
        lax.fori_loop(c0, jnp.minimum(c0 + 2, c1), body, 0)

    def finish():
        def body(c, carry):
            copy(c).wait()
            buf = c % 2

            def cast(i, carry2):
                r = pl.multiple_of(i * CAST_ROWS, CAST_ROWS)
                wbf[slot, pl.ds(pl.multiple_of(c * kc, kc) + r, CAST_ROWS), :] = (
                    stage[buf, pl.ds(r, CAST_ROWS), :].astype(BF16))
                return carry2
            lax.fori_loop(0, kc // CAST_ROWS, cast, 0)

            @pl.when(c + 2 < c1)
            def _():
                copy(c + 2).start(priority=priority)
            return carry
        lax.fori_loop(c0, c1, body, 0)

    return prime, finish


def _moe_up_body(be_ref, nu_ref, ws_ref, wn_ref, wc0_ref, wc1_ref, nv_ref, idx0_ref, idxn_ref, h2s_ref, w_hbm,
                 bias_ref, o_ref, xbuf, wbf, stage, sem, wsem):
    b = pl.program_id(0)
    nu = nu_ref[0]
    f = o_ref.shape[1]
    slab = SLAB_ROWS
    nch = wbf.shape[1] // stage.shape[1]

    def row_start(idx_ref, slot, r):
        tok = idx_ref[0, 0, r]
        pltpu.make_async_copy(h2s_ref.at[pl.ds(pl.multiple_of(tok * slab, slab), slab), :],
                              xbuf.at[slot, pl.ds(pl.multiple_of(r * SLAB_PITCH, SUBLANE), slab), :],
                              sem.at[slot]).start()

    def wait_rows(slot):
        pltpu.make_async_copy(h2s_ref.at[pl.ds(0, BM * slab), :], xbuf.at[slot, pl.ds(0, BM * slab), :],
                              sem.at[slot]).wait()

    def issue(idx_ref, slot):
        def body(r, c):
            row_start(idx_ref, slot, r)
            return c
        lax.fori_loop(0, BM, body, 0, unroll=DMA_UNROLL)

    @pl.when(b == 0)
    def _():
        issue(idx0_ref, 0)
        prime0, finish0 = _weight_stream(w_hbm, wbf, stage, wsem, be_ref[0], ws_ref[0], 0, nch)
        prime0()
        finish0()

    @pl.when(b + 1 < nu)
    def _():
        issue(idxn_ref, (b + 1) % 2)

    @pl.when(b < nu)
    def _():
        wslot = ws_ref[b]
        prime, finish = _weight_stream(w_hbm, wbf, stage, wsem, wn_ref[b], 1 - wslot, wc0_ref[b], wc1_ref[b],
                                       priority=1)
        prime()
        slot = b % 2
        wait_rows(slot)

        def compute(m):
            x = jnp.concatenate([xbuf[slot, pl.ds(s, m, stride=SLAB_PITCH), :].astype(BF16) for s in range(slab)],
                                axis=1)
            half = f // 2
            for c0 in (0, half):
                gate = (jnp.dot(x, wbf[wslot, :, c0:c0 + half], preferred_element_type=F32)
                        + bias_ref[0, :, c0:c0 + half])
                up = (jnp.dot(x, wbf[wslot, :, f + c0:f + c0 + half], preferred_element_type=F32)
                      + bias_ref[0, :, f + c0:f + c0 + half])
                gate = jnp.minimum(gate, SWIGLU_LIMIT)
                up = jnp.clip(up, -SWIGLU_LIMIT, SWIGLU_LIMIT)
                o_ref[0:m, c0:c0 + half] = (gate * jax.nn.sigmoid(SWIGLU_ALPHA * gate) * (up + 1.0)).astype(BF16)
            if m < BM:
                o_ref[m:BM, :] = jnp.zeros((BM - m, f), BF16)

        _for_row_count(nv_ref[b], compute)
        finish()

    @pl.when(b >= nu)
    def _():
        o_ref[...] = jnp.zeros_like(o_ref)


def _moe_up(h2s, src_tok, wgu, bgu, sched):
    ne, d, f2 = wgu.shape
    f = f2 // 2
    p = src_tok.shape[0]
    nb = p // BM
    idx3 = src_tok.reshape(nb, 1, BM)
    smem_blk = lambda imap: pl.BlockSpec((1, 1, BM), imap, memory_space=pltpu.SMEM)
    grid_spec = pltpu.PrefetchScalarGridSpec(
        num_scalar_prefetch=7,
        grid=(nb,),
        in_specs=[smem_blk(lambda b, be, *_: (0, 0, 0)),
                  smem_blk(lambda b, be, *_: (jnp.minimum(b + 1, nb - 1), 0, 0)),
                  pl.BlockSpec(memory_space=pl.ANY),
                  pl.BlockSpec(memory_space=pl.ANY),
                  pl.BlockSpec((1, 1, f2), lambda b, be, *_: (be[b], 0, 0))],
        out_specs=pl.BlockSpec((BM, f), lambda b, be, *_: (b, 0)),
        scratch_shapes=[pltpu.VMEM((2, BM * SLAB_PITCH, LANE), F32),
                        pltpu.VMEM((2, d, f2), BF16),
                        pltpu.VMEM((2, _chunk_rows(f2), f2), F32),
                        pltpu.SemaphoreType.DMA((2,)), pltpu.SemaphoreType.DMA((2,))],
    )
    return pl.pallas_call(
        _moe_up_body,
        grid_spec=grid_spec,
        out_shape=jax.ShapeDtypeStruct((p, f), BF16),
        compiler_params=_cparams(("arbitrary",), 58),
        name="moe_up",
    )(sched["blk_e"], sched["n_used"], sched["wslot"], sched["wnext"], sched["wc0"], sched["wc1"],
      sched["nvalid"], idx3, idx3, h2s, wgu, bgu.reshape(ne, 1, f2))


def _moe_down_body(be_ref, nu_ref, ws_ref, wn_ref, wc0_ref, wc1_ref, nv_ref, dst_ref, a_ref, w_hbm, bias_ref,
                   ysc_ref, ybuf, wbf, stage, sem, wsem):
    b = pl.program_id(0)
    nb = pl.num_programs(0)
    nu = nu_ref[0]
    slab = SLAB_ROWS
    nch = wbf.shape[1] // stage.shape[1]

    @pl.when(b == 0)
    def _():
        prime0, finish0 = _weight_stream(w_hbm, wbf, stage, wsem, be_ref[0], ws_ref[0], 0, nch)
        prime0()
        finish0()

    def row_copy(slot, r, d):
        return pltpu.make_async_copy(ybuf.at[slot, pl.ds(pl.multiple_of(r * SLAB_PITCH, SUBLANE), slab), :],
                                     ysc_ref.at[pl.ds(pl.multiple_of(d * slab, slab), slab), :],
                                     sem.at[slot])

    def drain(step):
        slot = step % 2
        count = nv_ref[step]

        @pl.when(count == BM)
        def _():
            pltpu.make_async_copy(ybuf.at[slot, pl.ds(0, BM * slab), :], ysc_ref.at[pl.ds(0, BM * slab), :],
                                  sem.at[slot]).wait()

        @pl.when(count < BM)
        def _():
            def body(r, c):
                row_copy(slot, 0, 0).wait()
                return c
            lax.fori_loop(0, count, body, 0)

    @pl.when((b >= 2) & (b < nu))
    def _():
        drain(b - 2)

    @pl.when(b < nu)
    def _():
        slot = b % 2
        wslot = ws_ref[b]
        prime, finish = _weight_stream(w_hbm, wbf, stage, wsem, wn_ref[b], 1 - wslot, wc0_ref[b], wc1_ref[b])
        prime()
        def compute(m):
            y = jnp.dot(a_ref[0:m, :], wbf[wslot], preferred_element_type=F32) + bias_ref[0]
            for s in range(slab):
                ybuf[slot, pl.ds(s, m, stride=SLAB_PITCH), :] = y[:, s * LANE:(s + 1) * LANE]

        _for_row_count(nv_ref[b], compute)
        finish()

        def body(r, c):
            row_copy(slot, r, dst_ref[0, 0, r]).start()
            return c

        def pair(i, c):
            row_copy(slot, 2 * i, dst_ref[0, 0, 2 * i]).start(priority=0)
            row_copy(slot, 2 * i + 1, dst_ref[0, 0, 2 * i + 1]).start(priority=1)
            return c

        @pl.when(nv_ref[b] == BM)
        def _():
            lax.fori_loop(0, BM // 2, pair, 0, unroll=DMA_UNROLL // 2)

        @pl.when(nv_ref[b] < BM)
        def _():
            lax.fori_loop(0, nv_ref[b], body, 0)

    @pl.when(b == nb - 1)
    def _():
        @pl.when(nu >= 2)
        def _():
            drain(nu - 2)
        drain(nu - 1)


def _moe_down(act, dst_slot, n_out_rows, wd, bd, sched):
    p, f = act.shape
    ne, _, d = wd.shape
    nb = p // BM
    grid_spec = pltpu.PrefetchScalarGridSpec(
        num_scalar_prefetch=7,
        grid=(nb,),
        in_specs=[pl.BlockSpec((1, 1, BM), lambda b, be, *_: (b, 0, 0), memory_space=pltpu.SMEM),
                  pl.BlockSpec((BM, f), lambda b, be, *_: (b, 0)),
                  pl.BlockSpec(memory_space=pl.ANY),
                  pl.BlockSpec((1, 1, d), lambda b, be, *_: (be[b], 0, 0))],
        out_specs=pl.BlockSpec(memory_space=pl.ANY),
        scratch_shapes=[pltpu.VMEM((2, BM * SLAB_PITCH, LANE), F32),
                        pltpu.VMEM((2, f, d), BF16),
                        pltpu.VMEM((2, _chunk_rows(d), d), F32),
                        pltpu.SemaphoreType.DMA((2,)), pltpu.SemaphoreType.DMA((2,))],
    )
    return pl.pallas_call(
        _moe_down_body,
        grid_spec=grid_spec,
        out_shape=jax.ShapeDtypeStruct((n_out_rows * SLAB_ROWS, LANE), F32),
        compiler_params=pltpu.CompilerParams(dimension_semantics=("arbitrary",),
                                             vmem_limit_bytes=48 * 1024 * 1024, has_side_effects=True),
        name="moe_down",
    )(sched["blk_e"], sched["n_used"], sched["wslot"], sched["wnext"], sched["wc0"], sched["wc1"],
      sched["nvalid"], dst_slot.reshape(nb, 1, BM), act, wd, bd.reshape(ne, 1, d))


def _final_body(y0_ref, y1_ref, y2_ref, y3_ref, rt_ref, x1_ref, ada_ref, g_ref, o_ref):
    tm = x1_ref.shape[0]
    rt = rt_ref[...]
    y_refs = (y0_ref, y1_ref, y2_ref, y3_ref)
    pieces = []
    for s in range(SLAB_ROWS):
        acc = rt[:, TOP_K:TOP_K + 1] * y_refs[0][pl.ds(s, tm, stride=SLAB_ROWS), :]
        for k in range(1, TOP_K):
            acc = acc + rt[:, TOP_K + k:TOP_K + k + 1] * y_refs[k][pl.ds(s, tm, stride=SLAB_ROWS), :]
        pieces.append(acc)
    y = jnp.concatenate(pieces, axis=1)
    gt_f = ada_ref[0, 5:6, :]
    o_ref[...] = x1_ref[...] + gt_f * (_rms(y) * g_ref[...])


def _final(ysc, rt, x1, ada3, g_post, seq):
    t, d = x1.shape
    tm = 256
    per_b = seq // tm
    nt = t // tm
    assert TOP_K == 4
    slot_spec = lambda k: pl.BlockSpec((tm * SLAB_ROWS, LANE), lambda i: (k * nt + i, 0))
    return pl.pallas_call(
        _final_body,
        grid=(nt,),
        in_specs=[slot_spec(0), slot_spec(1), slot_spec(2), slot_spec(3),
                  pl.BlockSpec((tm, LANE), lambda i: (i, 0)),
                  pl.BlockSpec((tm, d), lambda i: (i, 0)),
                  pl.BlockSpec((1, 6, d), lambda i: (i // per_b, 0, 0)),
                  pl.BlockSpec((1, d), lambda i: (0, 0))],
        out_specs=pl.BlockSpec((tm, d), lambda i: (i, 0)),
        out_shape=jax.ShapeDtypeStruct((t, d), F32),
        compiler_params=_cparams(("arbitrary",), 48),
        name="final",
    )(ysc, ysc, ysc, ysc, rt, x1, ada3, g_post.reshape(1, d))


def _mixer_ffn_layer(x2, ada3, bsz, seq, g_mix_pre, g_mix_post, g_ffn_pre, g_ffn_post, w_in, lb, g_rec_out,
                     rel_bias, w_branch_rec, w_branch_att, w_o, w_router, b_router, w_gate_up, b_gate_up,
                     w_down, b_down):
    t, d = x2.shape
    d_rec = w_branch_rec.shape[0]
    w_att = w_branch_att.shape[0]
    d_att = 3 * w_att
    widths = dict(q_r=d_rec, i_r=d_rec, zf_f=d_rec, zf_b=d_rec, z_o=d_rec, q_a=d_att, k_a=d_att, v_a=d_att,
                  zg_rec=d, zg_att=d)
    my_order = ("zg_rec", "zg_att", "q_r", "i_r", "zf_f", "zf_b", "z_o", "q_a", "k_a", "v_a")
    col, acc = {}, 0
    for name in my_order:
        col[name] = acc // LANE
        acc += widths[name]
    rot = acc - 2 * d

    proj = _inproj(x2, g_mix_pre, ada3, w_in, seq, rot)

    oi, qtf, qtb, utf, utb, df, db = _hgrn_a(proj, lb, col, t, d_rec)
    rec_o = _hgrn_c(oi, qtf, qtb, utf, utb, df, db, proj, g_rec_out, col, bsz, seq, d_rec)

    nums, stats = [], []
    for g, (window, dil) in enumerate(DIL_GROUPS):
        hs = slice(g * ATT_HEADS_PER_GROUP, (g + 1) * ATT_HEADS_PER_GROUP)
        bias = _band_bias(rel_bias[:, hs], window, dil)
        num, st = _attn_group(proj, bias, col, g, dil, bsz, seq)
        nums.append(num)
        stats.append(st)

    x1, h2s, logits = _merge(rec_o, nums, stats, proj, x2, ada3, g_mix_post, g_ffn_pre,
                            w_branch_rec.astype(BF16), w_branch_att.astype(BF16), w_o.astype(BF16),
                            w_router, b_router, col, seq)

    rt, cnt = _route(logits)
    ne = logits.shape[1]
    counts = cnt[0].astype(jnp.int32)
    top_idx = rt[:, 0:TOP_K].astype(jnp.int32)
    rank = rt[:, 2 * TOP_K:3 * TOP_K].astype(jnp.int32)
    padded = (counts + BM - 1) // BM * BM
    pends = jnp.cumsum(padded)
    pstarts = pends - padded
    experts = jnp.arange(ne, dtype=jnp.int32)
    pstart_sel = jnp.sum(jnp.where(top_idx[..., None] == experts, pstarts, 0), axis=-1)
    dest = (pstart_sel + rank).T.reshape(-1)
    p_rows = t * TOP_K + ne * BM
    nb = p_rows // BM
    blk_start = jnp.arange(nb, dtype=jnp.int32) * BM
    blk_e = jnp.minimum(jnp.sum((pends[None, :] <= blk_start[:, None]).astype(jnp.int32), axis=1), ne - 1)
    n_used = (pends[-1:] // BM).astype(jnp.int32)

    n_assign = t * TOP_K
    slot_assign = _invert_slots(dest, p_rows)
    valid_end = pstarts + counts
    blk_end = jnp.sum(jnp.where(blk_e[:, None] == experts, valid_end, 0), axis=-1)
    nvalid = jnp.clip(blk_end - blk_start, 0, BM).astype(jnp.int32)
    nvalid = jnp.where(jnp.arange(nb) < n_used[0], nvalid, 0)

    lookup = lambda table: jnp.sum(jnp.where(blk_e[:, None] == experts, table, 0), axis=-1)
    nonempty = padded > 0
    order = jnp.cumsum(nonempty.astype(jnp.int32)) - 1
    later = lax.cummin(jnp.where(nonempty, experts, ne)[::-1])[::-1]
    next_e = jnp.concatenate([later[1:], jnp.full((1,), ne, jnp.int32)])
    blk_next = lookup(next_e)
    has_next = (blk_next < ne) & (jnp.arange(nb) < n_used[0])
    k_in_run = jnp.arange(nb, dtype=jnp.int32) - lookup(pstarts // BM)
    n_in_run = jnp.maximum(lookup(padded // BM), 1)
    common = dict(blk_e=blk_e, n_used=n_used, nvalid=nvalid, wslot=lookup(order) % 2,
                  wnext=jnp.where(has_next, blk_next, blk_e))

    def schedule(w):
        nch = w.shape[1] // _chunk_rows(w.shape[2])
        s = dict(common, wc0=jnp.where(has_next, k_in_run * nch // n_in_run, 0),
                 wc1=jnp.where(has_next, (k_in_run + 1) * nch // n_in_run, 0))
        return {k: v.astype(jnp.int32) for k, v in s.items()}

    src_tok = jnp.maximum(slot_assign, 0) % t
    act = _moe_up(h2s, src_tok, w_gate_up, b_gate_up, schedule(w_gate_up))
    ysc = _moe_down(act, slot_assign, n_assign, w_down, b_down, schedule(w_down))
    return _final(ysc, rt, x1, ada3, g_ffn_post, seq)


def kernel(x, c, w_ada, b_ada, g_mix_pre, g_mix_post, g_ffn_pre, g_ffn_post, w_in, g_rec_out, w_branch_rec,
           w_branch_att, w_o, w_router, b_router, w_gate_up, b_gate_up, w_down, b_down, rec_lb_table, rel_bias):
    bsz, seq, d = x.shape
    depth = w_in.shape[0]
    lb_all = jnp.cumsum(jax.nn.softmax(rec_lb_table.astype(F32), axis=1), axis=1)
    x2 = x.reshape(bsz * seq, d)
    for layer in range(depth):
        ada3 = _ada(c, w_ada[layer], b_ada[layer]).reshape(bsz, 6, d)
        x2 = _mixer_ffn_layer(x2, ada3, bsz, seq, g_mix_pre[layer], g_mix_post[layer], g_ffn_pre[layer],
                              g_ffn_post[layer], w_in[layer], lb_all[:, layer], g_rec_out[layer], rel_bias,
                              w_branch_rec[layer], w_branch_att[layer], w_o[layer], w_router[layer],
                              b_router[layer], w_gate_up[layer], b_gate_up[layer], w_down[layer],
                              b_down[layer])
    return x2.reshape(bsz, seq, d)
```

```python
import functools
import math

import numpy as np
import jax
import jax.numpy as jnp
from jax import lax
from jax.experimental import pallas as pl
from jax.experimental.pallas import tpu as pltpu

F32 = jnp.float32
BF16 = jnp.bfloat16

LANE = 128
SUBLANE = 8
SLAB_ROWS = 16
SLAB_PITCH = 24

REC_HEAD_DIM = 128
REC_CHUNK = 64
ATT_HEAD_DIM = 128
ATT_HEADS_PER_GROUP = 4
ATT_BLOCK = 64
DIL_GROUPS = ((128, 1), (512, 4), (2048, 16))
NUM_BUCKETS = 32
MAX_DISTANCE = 1024
N_EXPERTS = 32
TOP_K = 4
SWIGLU_LIMIT = 7.0
SWIGLU_ALPHA = 1.702
RMS_EPS = 1e-6
NEG_INF = -1e30

N_LEVELS = 6
W_CHUNK_BYTES = 4 * 1024 * 1024
CAST_ROWS = 64
ROW_PATHS = 4
HGRN_UNROLL = 4
DMA_UNROLL = 8
ATT_UNROLL = 8
BM = 256

_NT = (((1,), (1,)), ((), ()))
_TN = (((0,), (0,)), ((), ()))


def _cparams(sem, vmem_mb):
    return pltpu.CompilerParams(dimension_semantics=sem, vmem_limit_bytes=vmem_mb * 1024 * 1024)


def _rms(x):
    return x * lax.rsqrt(jnp.mean(x * x, axis=-1, keepdims=True) + RMS_EPS)


def _ada_body(c_ref, w_ref, b_ref, o_ref):
    c = c_ref[...]
    cond = (c * jax.nn.sigmoid(c)).astype(BF16)
    o_ref[...] = jnp.dot(cond, w_ref[...].astype(BF16), preferred_element_type=F32) + b_ref[...]


def _ada(c, w, b):
    bsz, d = c.shape
    n = w.shape[1]
    tn = 1024
    cp = jnp.zeros((SUBLANE, d), F32).at[:bsz].set(c)
    out = pl.pallas_call(
        _ada_body,
        grid=(n // tn,),
        in_specs=[pl.BlockSpec((SUBLANE, d), lambda j: (0, 0)),
                  pl.BlockSpec((d, tn), lambda j: (0, j)),
                  pl.BlockSpec((1, tn), lambda j: (0, j))],
        out_specs=pl.BlockSpec((SUBLANE, tn), lambda j: (0, j)),
        out_shape=jax.ShapeDtypeStruct((SUBLANE, n), F32),
        compiler_params=_cparams(("arbitrary",), 40),
        name="ada",
    )(cp, w, b.reshape(1, n))
    return out[:bsz]


def _inproj_body(x_ref, g_ref, ada_ref, w_ref, o_ref, h_ref):
    @pl.when(pl.program_id(1) == 0)
    def _():
        half = x_ref.shape[0] // 2
        sh = ada_ref[0, 0:1, :]
        sc = ada_ref[0, 1:2, :]
        for r0 in (0, half):
            y = _rms(x_ref[r0:r0 + half, :]) * g_ref[...]
            h_ref[r0:r0 + half, :] = (y * (1.0 + sc) + sh).astype(BF16)

    o_ref[...] = jnp.dot(h_ref[...], w_ref[...].astype(BF16), preferred_element_type=F32)


def _inproj(x2, g, ada3, w_in, seq, rot):
    t, d = x2.shape
    n = w_in.shape[1]
    tm, tn = 2048, 512
    per_b = seq // tm
    nj = n // tn
    assert rot % tn == 0 and n % tn == 0 and seq % tm == 0
    return pl.pallas_call(
        _inproj_body,
        grid=(t // tm, nj),
        in_specs=[pl.BlockSpec((tm, d), lambda i, j: (i, 0), pipeline_mode=pl.Buffered(1)),
                  pl.BlockSpec((1, d), lambda i, j: (0, 0)),
                  pl.BlockSpec((1, 6, d), lambda i, j: (i // per_b, 0, 0)),
                  pl.BlockSpec((d, tn), lambda i, j: (0, (j + rot // tn) % nj))],
        out_specs=pl.BlockSpec((tm, tn), lambda i, j: (i, j)),
        out_shape=jax.ShapeDtypeStruct((t, n), F32),
        scratch_shapes=[pltpu.VMEM((tm, d), BF16)],
        compiler_params=_cparams(("arbitrary", "arbitrary"), 56),
        name="inproj",
    )(x2, g.reshape(1, d), ada3, w_in)


def _hgrn_consts():
    c = REC_CHUNK
    r = np.arange(c)[:, None]
    m = np.arange(c)[None, :]
    nw = N_LEVELS - 1
    wf = np.zeros(((nw + 2) * c, c), np.float32)
    wb = np.zeros(((nw + 2) * c, c), np.float32)
    mf = np.zeros((N_LEVELS + 1, c, c), np.float32)
    for lvl in range(N_LEVELS):
        s = 32 >> lvl
        m0 = (r // (2 * s)) * (2 * s) + s
        up = r >= m0
        if lvl < nw:
            wf[lvl * c:(lvl + 1) * c] = np.where(up, (m >= m0) & (m <= r), (m > r) & (m <= m0 - 1))
            wb[lvl * c:(lvl + 1) * c] = np.where(up, (m >= m0) & (m <= r - 1), (m >= r) & (m <= m0 - 1))
        i = np.arange(c)[:, None]
        j = np.arange(c)[None, :]
        mf[lvl] = (i // (2 * s) == j // (2 * s)) & (i % (2 * s) >= s) & (j % (2 * s) < s)
    mf[N_LEVELS] = np.eye(c)
    wf[nw * c:(nw + 1) * c] = m <= r
    wf[(nw + 1) * c:(nw + 2) * c] = m > r
    wb[nw * c:(nw + 1) * c] = m >= r
    wb[(nw + 1) * c:(nw + 2) * c] = m < r
    mfb = mf + np.transpose(mf, (0, 2, 1))
    mfb[N_LEVELS] = np.eye(c)
    up = np.zeros((N_LEVELS, c, LANE), np.float32)
    for lvl in range(N_LEVELS):
        s = 32 >> lvl
        up[lvl] = ((np.arange(c) % (2 * s)) >= s)[:, None]
    wf3 = np.concatenate([wf, wf, wf], axis=1)
    wb3 = np.concatenate([wb, wb, wb], axis=1)
    return (jnp.asarray(wf3, BF16), jnp.asarray(wb3, BF16), jnp.asarray(mfb, F32), jnp.asarray(up, F32),
            jnp.asarray(1.0 - up, F32))


def _split3(g):
    hi = g.astype(BF16)
    r1 = g - hi.astype(F32)
    mid = r1.astype(BF16)
    lo = (r1 - mid.astype(F32)).astype(BF16)
    return jnp.concatenate([hi, mid, lo], axis=0)


def _hgrn_a_body(q_ref, i_ref, zf_ref, zb_ref, lb_ref, wf_ref, wb_ref, mf_ref, up_ref, lo_ref,
                 oi_ref, qtf_ref, qtb_ref, utf_ref, utb_ref, df_ref, db_ref, *, cpb):
    c = REC_CHUNK
    dirs = ((zf_ref, wf_ref, None, qtf_ref, utf_ref, df_ref, 0, c - 1),
            (zb_ref, wb_ref, None, qtb_ref, utb_ref, db_ref, 1, 0))

    def chunk_group(cg, carry):
        cis = [cg * HGRN_UNROLL + u for u in range(HGRN_UNROLL)]
        rows = [pl.ds(pl.multiple_of(ci * c, c), c) for ci in cis]
        zqs = [q_ref[rw, :] for rw in rows]
        qs = [zq * jax.nn.sigmoid(zq) for zq in zqs]
        vbs = [i_ref[rw, :].astype(BF16) for rw in rows]
        units = [(u, d) for u in range(HGRN_UNROLL) for d in range(2)]
        nw = N_LEVELS - 1
        ks, es, fs = {}, {}, {}
        for u, d in units:
            z_ref, w_ref = dirs[d][0], dirs[d][1]
            lb = lb_ref[d:d + 1, :]
            f = lb + (1.0 - lb) * jax.nn.sigmoid(z_ref[rows[u], :])
            fs[u, d] = f
            ks[u, d] = 1.0 - f
            es[u, d] = jnp.exp(jnp.dot(w_ref[...], _split3(jnp.log(f)), preferred_element_type=F32))
        acc = [jnp.zeros((c, c), F32) for _ in range(HGRN_UNROLL)]
        for lvl in range(N_LEVELS + 1):
            for u in range(HGRN_UNROLL):
                if lvl < N_LEVELS:
                    up, lo = up_ref[lvl], lo_ref[lvl]
                    if lvl < nw:
                        ef = es[u, 0][lvl * c:(lvl + 1) * c]
                        eb = es[u, 1][lvl * c:(lvl + 1) * c]
                    else:
                        ef = fs[u, 0] * up + lo
                        eb = fs[u, 1] * lo + up
                    qa = jnp.concatenate([(qs[u] * (ef * up)).astype(BF16), (qs[u] * (eb * lo)).astype(BF16)],
                                         axis=1)
                    ka = jnp.concatenate([(ks[u, 0] * (ef * lo)).astype(BF16),
                                          (ks[u, 1] * (eb * up)).astype(BF16)], axis=1)
                else:
                    qa, ka = qs[u].astype(BF16), (ks[u, 0] + ks[u, 1]).astype(BF16)
                p = lax.dot_general(qa, ka, _NT, preferred_element_type=F32)
                acc[u] = acc[u] + p * mf_ref[lvl]
        for u, d in units:
            _, _, _, qt_ref, ut_ref, d_ref, _, drow = dirs[d]
            e = es[u, d]
            qt_ref[rows[u], :] = (qs[u] * e[nw * c:(nw + 1) * c]).astype(BF16)
            kt = (ks[u, d] * e[(nw + 1) * c:(nw + 2) * c]).astype(BF16)
            ut_ref[cis[u]] = lax.dot_general(vbs[u], kt, _TN, preferred_element_type=F32).astype(BF16)
            d_ref[pl.ds(cis[u], 1), :] = e[nw * c + drow:nw * c + drow + 1]
        for u in range(HGRN_UNROLL):
            oi_ref[rows[u], :] = jnp.dot(acc[u].astype(BF16), vbs[u], preferred_element_type=F32)
        return carry

    lax.fori_loop(0, cpb // HGRN_UNROLL, chunk_group, 0)


def _hgrn_a(proj, lb, col, t, d_rec):
    heads = d_rec // REC_HEAD_DIM
    tq = 1024
    cpb = tq // REC_CHUNK
    nchunks = t // REC_CHUNK
    wf, wb, mf, up, lo = _hgrn_consts()
    hd = REC_HEAD_DIM

    def colspec(off):
        return pl.BlockSpec((tq, hd), lambda i, h: (i, off + h))

    full2 = lambda i, h: (0, 0)
    full3 = lambda i, h: (0, 0, 0)
    row_spec = pl.BlockSpec((tq, hd), lambda i, h: (i, h))
    u_spec = pl.BlockSpec((cpb, hd, hd), lambda i, h: (i, 0, h))
    d_spec = pl.BlockSpec((cpb, hd), lambda i, h: (i, h))
    return pl.pallas_call(
        functools.partial(_hgrn_a_body, cpb=cpb),
        grid=(t // tq, heads),
        in_specs=[colspec(col["q_r"]), colspec(col["i_r"]), colspec(col["zf_f"]), colspec(col["zf_b"]),
                  pl.BlockSpec((2, hd), lambda i, h: (0, h)),
                  pl.BlockSpec(wf.shape, full2), pl.BlockSpec(wb.shape, full2),
                  pl.BlockSpec(mf.shape, full3), pl.BlockSpec(up.shape, full3), pl.BlockSpec(lo.shape, full3)],
        out_specs=[row_spec, row_spec, row_spec, u_spec, u_spec, d_spec, d_spec],
        out_shape=[jax.ShapeDtypeStruct((t, d_rec), F32),
                   jax.ShapeDtypeStruct((t, d_rec), BF16),
                   jax.ShapeDtypeStruct((t, d_rec), BF16),
                   jax.ShapeDtypeStruct((nchunks, hd, d_rec), BF16),
                   jax.ShapeDtypeStruct((nchunks, hd, d_rec), BF16),
                   jax.ShapeDtypeStruct((nchunks, d_rec), F32),
                   jax.ShapeDtypeStruct((nchunks, d_rec), F32)],
        compiler_params=_cparams(("arbitrary", "arbitrary"), 32),
        name="hgrn_a",
    )(proj, proj, proj, proj, lb, wf, wb, mf, up, lo)


def _hgrn_c_body(oi_ref, qtf_ref, qtb_ref, utf_ref, utb_ref, df_ref, db_ref, z_ref, g_ref,
                 out_ref, acc_ref, accb_ref, *, nchunks):
    c = REC_CHUNK
    hd = REC_HEAD_DIM

    unroll = 4

    def step(i, carry):
        st_f, st_b = carry
        pending = []
        for u in range(unroll):
            nf = i * unroll + u
            nb = nchunks - 1 - nf
            rows_f = pl.ds(pl.multiple_of(nf * c, c), c)
            rows_b = pl.ds(pl.multiple_of(nb * c, c), c)
            of = lax.dot_general(qtf_ref[rows_f, :], st_f.astype(BF16), _NT, preferred_element_type=F32)
            ob = lax.dot_general(qtb_ref[rows_b, :], st_b.astype(BF16), _NT, preferred_element_type=F32)
            pending.append((rows_f, rows_b, oi_ref[rows_f, :] + of, ob))
            st_f = df_ref[pl.ds(nf, 1), :] * st_f + utf_ref[nf].astype(F32)
            st_b = db_ref[pl.ds(nb, 1), :] * st_b + utb_ref[nb].astype(F32)
        for rows_f, rows_b, vf, vb in pending:
            acc_ref[rows_f, :] = vf
            accb_ref[rows_b, :] = vb
        return st_f, st_b

    zero = jnp.zeros((hd, hd), F32)
    lax.fori_loop(0, nchunks // unroll, step, (zero, zero))

    o = _rms(acc_ref[...] + accb_ref[...])
    out_ref[...] = (o * g_ref[...] * jax.nn.sigmoid(z_ref[...])).astype(BF16)


def _hgrn_c(oi, qtf, qtb, utf, utb, df, db, proj, g_out, col, bsz, seq, d_rec):
    heads = d_rec // REC_HEAD_DIM
    hd = REC_HEAD_DIM
    nchunks = seq // REC_CHUNK
    row_spec = pl.BlockSpec((seq, hd), lambda b, h: (b, h))
    u_spec = pl.BlockSpec((nchunks, hd, hd), lambda b, h: (b, 0, h))
    d_spec = pl.BlockSpec((nchunks, hd), lambda b, h: (b, h))
    zo = col["z_o"]
    return pl.pallas_call(
        functools.partial(_hgrn_c_body, nchunks=nchunks),
        grid=(bsz, heads),
        in_specs=[row_spec, row_spec, row_spec, u_spec, u_spec, d_spec, d_spec,
                  pl.BlockSpec((seq, hd), lambda b, h: (b, zo + h)),
                  pl.BlockSpec((1, hd), lambda b, h: (0, h))],
        out_specs=row_spec,
        out_shape=jax.ShapeDtypeStruct((bsz * seq, d_rec), BF16),
        scratch_shapes=[pltpu.VMEM((seq, hd), F32), pltpu.VMEM((seq, hd), F32)],
        compiler_params=_cparams(("arbitrary", "arbitrary"), 48),
        name="hgrn_c",
    )(oi, qtf, qtb, utf, utb, df, db, proj, g_out.reshape(1, d_rec))


def _t5_bucket(rel):
    half_buckets = NUM_BUCKETS // 2
    ret = np.where(rel > 0, half_buckets, 0)
    n = np.abs(rel)
    max_exact = half_buckets // 2
    nf = np.maximum(n, 1).astype(np.float32)
    large = max_exact + (np.log(nf / np.float32(max_exact)) / np.float32(math.log(MAX_DISTANCE / max_exact))
                         * np.float32(half_buckets - max_exact)).astype(np.int32)
    large = np.minimum(large, half_buckets - 1)
    return ret + np.where(n < max_exact, n, large)


def _band_bias(rel_bias_g, window, dil):
    half = window // (2 * dil)
    q_off = np.arange(ATT_BLOCK)[:, None]
    rel = np.arange(3 * ATT_BLOCK)[None, :] - ATT_BLOCK - q_off
    onehot = (_t5_bucket(rel * dil)[..., None] == np.arange(NUM_BUCKETS)).astype(np.float32)
    bias = jnp.einsum("qkb,bh->hqk", jnp.asarray(onehot), rel_bias_g.astype(F32),
                      precision=lax.Precision.HIGHEST)
    return jnp.where(jnp.asarray(np.abs(rel) <= half)[None], bias, NEG_INF)


def _attn_body(q_ref, kp_ref, k_ref, kn_ref, vp_ref, v_ref, vn_ref, bias_ref,
               num_ref, st_ref, kc_ref, vc_ref, *, dil, tq, sub_len):
    blk = ATT_BLOCK
    nqb = tq // blk
    n = pl.program_id(1)
    scale = ATT_HEAD_DIM ** -0.5

    def sds(start, size):
        if dil == 1:
            return pl.ds(start, size)
        return pl.ds(start, size, stride=dil)

    cu = kc_ref.shape[0]
    qu = ATT_UNROLL // cu

    def deinterleave(r, j):
        kc_ref[j, 0:blk, :] = kp_ref[sds(r, blk), :].astype(BF16)
        kc_ref[j, blk:blk + tq, :] = k_ref[sds(r, tq), :].astype(BF16)
        kc_ref[j, blk + tq:2 * blk + tq, :] = kn_ref[sds(r, blk), :].astype(BF16)
        vc_ref[j, 0:blk, :] = vp_ref[sds(r, blk), :].astype(BF16)
        vc_ref[j, blk:blk + tq, :] = v_ref[sds(r, tq), :].astype(BF16)
        vc_ref[j, blk + tq:2 * blk + tq, :] = vn_ref[sds(r, blk), :].astype(BF16)

    def units(r0, qb0):
        us = [(j, u) for j in range(cu) for u in range(qu)]
        q0s = [pl.multiple_of((qb0 + u) * blk, blk) for _, u in us]
        rows = [sds(r0 + j + dil * q0, blk) for (j, _), q0 in zip(us, q0s)]
        lane = lax.broadcasted_iota(jnp.int32, (blk, LANE), 1)
        key_iota = lax.broadcasted_iota(jnp.int32, (1, 3 * blk), 1)
        bias = bias_ref[0]
        qs = [q_ref[rw, :].astype(BF16) for rw in rows]
        kws = [kc_ref[j, pl.ds(q0, 3 * blk), :] for (j, _), q0 in zip(us, q0s)]
        vws = [vc_ref[j, pl.ds(q0, 3 * blk), :] for (j, _), q0 in zip(us, q0s)]
        ss = [lax.dot_general(q, kw, _NT, preferred_element_type=F32) * scale for q, kw in zip(qs, kws)]
        valids = []
        for q0 in q0s:
            kpos = n * tq + q0 - blk + key_iota
            valids.append((kpos >= 0) & (kpos < sub_len))
        ss = [jnp.where(valid, s + bias, NEG_INF) for s, valid in zip(ss, valids)]
        ms = [jnp.max(s, axis=-1, keepdims=True) for s in ss]
        ps = [jnp.exp(s - m) for s, m in zip(ss, ms)]
        ls = [jnp.sum(p, axis=-1, keepdims=True) for p in ps]
        nums = [jnp.dot(p.astype(BF16), vw, preferred_element_type=F32) for p, vw in zip(ps, vws)]
        for rw, num, m, l in zip(rows, nums, ms, ls):
            num_ref[rw, :] = num
            st_ref[rw, :] = jnp.where(lane < LANE // 2, m, l)

    def class_group(rg, carry):
        r0 = rg * cu
        for j in range(cu):
            deinterleave(r0 + j, j)

        def qgroup(qg, carry2):
            units(r0, qg * qu)
            return carry2

        lax.fori_loop(0, nqb // qu, qgroup, 0)
        return carry

    lax.fori_loop(0, dil // cu, class_group, 0)


def _attn_group(proj, bias, col, g, dil, bsz, seq):
    tile = 1024
    tq = tile // dil
    halo = ATT_BLOCK * dil
    sub_len = seq // dil
    cu = ATT_UNROLL // min(tq // ATT_BLOCK, ATT_UNROLL)
    nh = ATT_HEADS_PER_GROUP
    hd = ATT_HEAD_DIM
    qc = col["q_a"] + g * nh
    kc = col["k_a"] + g * nh
    vc = col["v_a"] + g * nh
    tiles_b = seq // tile
    halos_b = seq // halo
    hpt = tile // halo

    own = lambda c: pl.BlockSpec((tile, hd), lambda b, n, h: (b * tiles_b + n, c + h))
    prev = lambda c: pl.BlockSpec(
        (halo, hd), lambda b, n, h: (b * halos_b + jnp.maximum(n * hpt - 1, 0), c + h))
    nxt = lambda c: pl.BlockSpec(
        (halo, hd), lambda b, n, h: (b * halos_b + jnp.minimum((n + 1) * hpt, halos_b - 1), c + h))
    t = bsz * seq
    return pl.pallas_call(
        functools.partial(_attn_body, dil=dil, tq=tq, sub_len=sub_len),
        grid=(bsz, tiles_b, nh),
        in_specs=[own(qc), prev(kc), own(kc), nxt(kc), prev(vc), own(vc), nxt(vc),
                  pl.BlockSpec((1,) + bias.shape[1:], lambda b, n, h: (h, 0, 0))],
        out_specs=[pl.BlockSpec((tile, hd), lambda b, n, h: (b * tiles_b + n, h)),
                   pl.BlockSpec((tile, LANE), lambda b, n, h: (b * tiles_b + n, h))],
        out_shape=[jax.ShapeDtypeStruct((t, nh * hd), F32), jax.ShapeDtypeStruct((t, nh * LANE), F32)],
        scratch_shapes=[pltpu.VMEM((cu, tq + 2 * ATT_BLOCK, hd), BF16),
                        pltpu.VMEM((cu, tq + 2 * ATT_BLOCK, hd), BF16)],
        compiler_params=_cparams(("arbitrary", "arbitrary", "arbitrary"), 32),
        name=f"attn_d{dil}",
    )(proj, proj, proj, proj, proj, proj, proj, bias)


def _merge_body(rec_ref, n0_ref, n1_ref, n2_ref, s0_ref, s1_ref, s2_ref, zgr_ref, zga_ref, x_ref,
                ada_ref, gpost_ref, gpre_ref, wbr_ref, wba_ref, wo_ref, wr_ref, br_ref,
                x1_ref, h2_ref, lg_ref):
    nh = ATT_HEADS_PER_GROUP
    hd = ATT_HEAD_DIM
    half = LANE // 2
    lane = lax.broadcasted_iota(jnp.int32, (rec_ref.shape[0], LANE), 1)
    heads = []
    for h in range(nh):
        cols = slice(h * hd, (h + 1) * hd)
        st = [s[:, cols] for s in (s0_ref, s1_ref, s2_ref)]
        top = jnp.maximum(jnp.maximum(st[0], st[1]), st[2])
        ws = [jnp.exp(s - top) for s in st]
        den = (ws[0] * pltpu.roll(st[0], half, 1) + ws[1] * pltpu.roll(st[1], half, 1)
               + ws[2] * pltpu.roll(st[2], half, 1))
        coef = [w / den for w in ws]
        coef = [jnp.where(lane < half, c, pltpu.roll(c, half, 1)) for c in coef]
        num = coef[0] * n0_ref[:, cols] + coef[1] * n1_ref[:, cols] + coef[2] * n2_ref[:, cols]
        heads.append(num.astype(BF16))
    att = jnp.concatenate(heads, axis=1)
    y_rec = jnp.dot(rec_ref[...], wbr_ref[...], preferred_element_type=F32)
    y_att = jnp.dot(att, wba_ref[...], preferred_element_type=F32)
    merged = jax.nn.sigmoid(zgr_ref[...]) * y_rec + jax.nn.sigmoid(zga_ref[...]) * y_att
    y = jnp.dot(merged.astype(BF16), wo_ref[...], preferred_element_type=F32)
    gt_m = ada_ref[0, 2:3, :]
    sh_f = ada_ref[0, 3:4, :]
    sc_f = ada_ref[0, 4:5, :]
    x1 = x_ref[...] + gt_m * (_rms(y) * gpost_ref[...])
    x1_ref[...] = x1
    h2 = _rms(x1) * gpre_ref[...] * (1.0 + sc_f) + sh_f
    tm = h2.shape[0]
    for s in range(SLAB_ROWS):
        h2_ref[pl.ds(s, tm, stride=SLAB_ROWS), :] = h2[:, s * LANE:(s + 1) * LANE]
    ne = lg_ref.shape[1]
    h_hi = h2.astype(BF16)
    h_lo = (h2 - h_hi.astype(F32)).astype(BF16)
    both = jnp.dot(h_hi, wr_ref[...], preferred_element_type=F32)
    cross = jnp.dot(h_lo, wr_ref[:, 0:ne], preferred_element_type=F32)
    lg_ref[...] = both[:, 0:ne] + both[:, ne:2 * ne] + cross + br_ref[...]


def _merge(rec_o, nums, stats, proj, x2, ada3, g_post, g_pre, wbr, wba, wo, w_router, b_router, col, seq):
    t, d = x2.shape
    tm = 256
    per_b = seq // tm
    d_rec = rec_o.shape[1]
    w_att = nums[0].shape[1]
    ne = w_router.shape[1]
    wr_hi = w_router.astype(BF16)
    wr_lo = (w_router - wr_hi.astype(F32)).astype(BF16)
    w_router = jnp.concatenate([wr_hi, wr_lo], axis=1)
    dl = d // LANE
    row = lambda w: pl.BlockSpec((tm, w), lambda i: (i, 0))
    const = lambda shape: pl.BlockSpec(shape, lambda i: (0,) * len(shape), pipeline_mode=pl.Buffered(1))
    zgr = col["zg_rec"] // dl
    zga = col["zg_att"] // dl
    return pl.pallas_call(
        _merge_body,
        grid=(t // tm,),
        in_specs=[row(d_rec), row(w_att), row(w_att), row(w_att), row(w_att), row(w_att), row(w_att),
                  pl.BlockSpec((tm, d), lambda i: (i, zgr)),
                  pl.BlockSpec((tm, d), lambda i: (i, zga)),
                  row(d),
                  pl.BlockSpec((1, 6, d), lambda i: (i // per_b, 0, 0)),
                  const((1, d)), const((1, d)),
                  const(wbr.shape), const(wba.shape), const(wo.shape), const(w_router.shape),
                  const((1, ne))],
        out_specs=[row(d), pl.BlockSpec((tm * SLAB_ROWS, LANE), lambda i: (i, 0)),
                   pl.BlockSpec((tm, ne), lambda i: (i, 0))],
        out_shape=[jax.ShapeDtypeStruct((t, d), F32), jax.ShapeDtypeStruct((t * SLAB_ROWS, LANE), F32),
                   jax.ShapeDtypeStruct((t, ne), F32)],
        compiler_params=_cparams(("arbitrary",), 56),
        name="merge",
    )(rec_o, nums[0], nums[1], nums[2], stats[0], stats[1], stats[2], proj, proj, x2, ada3,
      g_post.reshape(1, d), g_pre.reshape(1, d), wbr, wba, wo, w_router, b_router.reshape(1, ne))


def _route_body(lg_ref, tri_ref, rt_ref, cnt_ref, carry_ref):
    i = pl.program_id(0)
    tr, ne = lg_ref.shape

    @pl.when(i == 0)
    def _():
        carry_ref[...] = jnp.zeros_like(carry_ref)

    l = lg_ref[...]
    lane = lax.broadcasted_iota(jnp.int32, (tr, ne), 1).astype(F32)
    vals, sels, idxs = [], [], []
    for _ in range(TOP_K):
        m = jnp.max(l, axis=-1, keepdims=True)
        idx = jnp.min(jnp.where(l == m, lane, float(ne)), axis=-1, keepdims=True)
        sel = lane == idx
        vals.append(m)
        idxs.append(idx)
        sels.append(sel)
        l = jnp.where(sel, -jnp.inf, l)
    es = [jnp.exp(v - vals[0]) for v in vals]
    tot = es[0] + es[1] + es[2] + es[3]
    chosen = (sels[0] | sels[1] | sels[2] | sels[3]).astype(F32)
    prefix = jnp.dot(tri_ref[...], chosen.astype(BF16), preferred_element_type=F32) + carry_ref[0:1, :]
    out_lane = lax.broadcasted_iota(jnp.int32, (tr, LANE), 1)
    rt = jnp.zeros((tr, LANE), F32)
    for k in range(TOP_K):
        rank = jnp.sum(jnp.where(sels[k], prefix, 0.0), axis=-1, keepdims=True)
        rt = jnp.where(out_lane == k, idxs[k], rt)
        rt = jnp.where(out_lane == TOP_K + k, es[k] / tot, rt)
        rt = jnp.where(out_lane == 2 * TOP_K + k, rank, rt)
    rt_ref[...] = rt
    new = carry_ref[0:1, :] + jnp.sum(chosen, axis=0, keepdims=True)
    carry_ref[...] = jnp.broadcast_to(new, carry_ref.shape)
    cnt_ref[...] = carry_ref[...]


def _route(logits):
    t, ne = logits.shape
    tr = 512
    tri = jnp.asarray(np.tril(np.ones((tr, tr), np.float32), -1), BF16)
    return pl.pallas_call(
        _route_body,
        grid=(t // tr,),
        in_specs=[pl.BlockSpec((tr, ne), lambda i: (i, 0)),
                  pl.BlockSpec((tr, tr), lambda i: (0, 0))],
        out_specs=[pl.BlockSpec((tr, LANE), lambda i: (i, 0)),
                   pl.BlockSpec((SUBLANE, ne), lambda i: (0, 0))],
        out_shape=[jax.ShapeDtypeStruct((t, LANE), F32), jax.ShapeDtypeStruct((SUBLANE, ne), F32)],
        scratch_shapes=[pltpu.VMEM((SUBLANE, ne), F32)],
        compiler_params=_cparams(("arbitrary",), 32),
        name="route",
    )(logits, tri)


def _invert_body(pad_lo_ref, pad_hi_ref, dest_ref, out_ref, *, chunk):
    base = pl.program_id(0) * chunk

    @pl.when(pl.program_id(0) == 0)
    def _():
        def one_range(e, c):
            def fill(p, c2):
                out_ref[p] = -1
                return c2
            lax.fori_loop(pad_lo_ref[e], pad_hi_ref[e], fill, 0)
            return c
        lax.fori_loop(0, pad_lo_ref.shape[0], one_range, 0)

    def body(j, c):
        out_ref[dest_ref[0, 0, j]] = base + j
        return c
    lax.fori_loop(0, chunk, body, 0, unroll=DMA_UNROLL)


def _invert_slots(dest, pad_lo, pad_hi, p_rows):
    n = dest.shape[0]
    chunk = 2048
    nch = n // chunk
    grid_spec = pltpu.PrefetchScalarGridSpec(
        num_scalar_prefetch=2,
        grid=(nch,),
        in_specs=[pl.BlockSpec((1, 1, chunk), lambda i, lo, hi: (i, 0, 0), memory_space=pltpu.SMEM)],
        out_specs=pl.BlockSpec(memory_space=pltpu.SMEM),
    )
    return pl.pallas_call(
        functools.partial(_invert_body, chunk=chunk),
        grid_spec=grid_spec,
        out_shape=jax.ShapeDtypeStruct((p_rows,), jnp.int32),
        compiler_params=pltpu.CompilerParams(dimension_semantics=("arbitrary",)),
        name="invert_slots",
    )(pad_lo, pad_hi, dest.reshape(nch, 1, chunk))


def _for_row_count(nvalid, compute):
    q = BM // ROW_PATHS
    for i in range(1, ROW_PATHS + 1):
        lo, hi = (i - 1) * q, i * q
        cond = (nvalid > lo) & (nvalid <= hi) if i > 1 else (nvalid <= hi)

        @pl.when(cond)
        def _(m=hi):
            compute(m)


def _chunk_rows(ncols):
    return W_CHUNK_BYTES // (4 * ncols)


def _weight_stream(w_hbm, wbf, stage, wsem, e, slot, c0, c1, priority=0):
    kc = stage.shape[1]

    def copy(c):
        return pltpu.make_async_copy(w_hbm.at[e, pl.ds(pl.multiple_of(c * kc, kc), kc), :],
                                     stage.at[c % 2], wsem.at[c % 2])

    def prime():
        def body(c, carry):
            copy(c).start(priority=priority)
            return carry
        lax.fori_loop(c0, jnp.minimum(c0 + 2, c1), body, 0)

    def finish():
        def body(c, carry):
            copy(c).wait()
            buf = c % 2

            def cast(i, carry2):
                r = pl.multiple_of(i * CAST_ROWS, CAST_ROWS)
                wbf[slot, pl.ds(pl.multiple_of(c * kc, kc) + r, CAST_ROWS), :] = (
                    stage[buf, pl.ds(r, CAST_ROWS), :].astype(BF16))
                return carry2
            lax.fori_loop(0, kc // CAST_ROWS, cast, 0)

            @pl.when(c + 2 < c1)
            def _():
                copy(c + 2).start(priority=priority)
            return carry
        lax.fori_loop(c0, c1, body, 0)

    return prime, finish


def _moe_up_body(be_ref, nu_ref, ws_ref, wn_ref, wc0_ref, wc1_ref, nv_ref, idx0_ref, idxn_ref, h2s_ref, w_hbm,
                 bias_ref, o_ref, xbuf, wbf, stage, sem, wsem):
    b = pl.program_id(0)
    nu = nu_ref[0]
    f = o_ref.shape[1]
    slab = SLAB_ROWS
    nch = wbf.shape[1] // stage.shape[1]

    def row_start(idx_ref, slot, r):
        tok = idx_ref[0, 0, r]
        pltpu.make_async_copy(h2s_ref.at[pl.ds(pl.multiple_of(tok * slab, slab), slab), :],
                              xbuf.at[slot, pl.ds(pl.multiple_of(r * SLAB_PITCH, SUBLANE), slab), :],
                              sem.at[slot]).start()

    def wait_rows(slot):
        pltpu.make_async_copy(h2s_ref.at[pl.ds(0, BM * slab), :], xbuf.at[slot, pl.ds(0, BM * slab), :],
                              sem.at[slot]).wait()

    def issue(idx_ref, slot):
        def body(r, c):
            row_start(idx_ref, slot, r)
            return c
        lax.fori_loop(0, BM, body, 0, unroll=DMA_UNROLL)

    @pl.when(b == 0)
    def _():
        issue(idx0_ref, 0)
        prime0, finish0 = _weight_stream(w_hbm, wbf, stage, wsem, be_ref[0], ws_ref[0], 0, nch)
        prime0()
        finish0()

    @pl.when(b + 1 < nu)
    def _():
        issue(idxn_ref, (b + 1) % 2)

    @pl.when(b < nu)
    def _():
        wslot = ws_ref[b]
        prime, finish = _weight_stream(w_hbm, wbf, stage, wsem, wn_ref[b], 1 - wslot, wc0_ref[b], wc1_ref[b],
                                       priority=1)
        prime()
        slot = b % 2
        wait_rows(slot)

        def compute(m):
            x = jnp.concatenate([xbuf[slot, pl.ds(s, m, stride=SLAB_PITCH), :].astype(BF16) for s in range(slab)],
                                axis=1)
            half = f // 2
            for c0 in (0, half):
                gate = (jnp.dot(x, wbf[wslot, :, c0:c0 + half], preferred_element_type=F32)
                        + bias_ref[0, :, c0:c0 + half])
                up = (jnp.dot(x, wbf[wslot, :, f + c0:f + c0 + half], preferred_element_type=F32)
                      + bias_ref[0, :, f + c0:f + c0 + half])
                gate = jnp.minimum(gate, SWIGLU_LIMIT)
                up = jnp.clip(up, -SWIGLU_LIMIT, SWIGLU_LIMIT)
                o_ref[0:m, c0:c0 + half] = (gate * jax.nn.sigmoid(SWIGLU_ALPHA * gate) * (up + 1.0)).astype(BF16)
            if m < BM:
                o_ref[m:BM, :] = jnp.zeros((BM - m, f), BF16)

        _for_row_count(nv_ref[b], compute)
        finish()

    @pl.when(b >= nu)
    def _():
        o_ref[...] = jnp.zeros_like(o_ref)


def _moe_up(h2s, src_tok, wgu, bgu, sched):
    ne, d, f2 = wgu.shape
    f = f2 // 2
    p = src_tok.shape[0]
    nb = p // BM
    idx3 = src_tok.reshape(nb, 1, BM)
    smem_blk = lambda imap: pl.BlockSpec((1, 1, BM), imap, memory_space=pltpu.SMEM)
    grid_spec = pltpu.PrefetchScalarGridSpec(
        num_scalar_prefetch=7,
        grid=(nb,),
        in_specs=[smem_blk(lambda b, be, *_: (0, 0, 0)),
                  smem_blk(lambda b, be, *_: (jnp.minimum(b + 1, nb - 1), 0, 0)),
                  pl.BlockSpec(memory_space=pl.ANY),
                  pl.BlockSpec(memory_space=pl.ANY),
                  pl.BlockSpec((1, 1, f2), lambda b, be, *_: (be[b], 0, 0))],
        out_specs=pl.BlockSpec((BM, f), lambda b, be, *_: (b, 0)),
        scratch_shapes=[pltpu.VMEM((2, BM * SLAB_PITCH, LANE), F32),
                        pltpu.VMEM((2, d, f2), BF16),
                        pltpu.VMEM((2, _chunk_rows(f2), f2), F32),
                        pltpu.SemaphoreType.DMA((2,)), pltpu.SemaphoreType.DMA((2,))],
    )
    return pl.pallas_call(
        _moe_up_body,
        grid_spec=grid_spec,
        out_shape=jax.ShapeDtypeStruct((p, f), BF16),
        compiler_params=_cparams(("arbitrary",), 58),
        name="moe_up",
    )(sched["blk_e"], sched["n_used"], sched["wslot"], sched["wnext"], sched["wc0"], sched["wc1"],
      sched["nvalid"], idx3, idx3, h2s, wgu, bgu.reshape(ne, 1, f2))


def _moe_down_body(be_ref, nu_ref, ws_ref, wn_ref, wc0_ref, wc1_ref, nv_ref, dst_ref, a_ref, w_hbm, bias_ref,
                   ysc_ref, ybuf, wbf, stage, sem, wsem):
    b = pl.program_id(0)
    nb = pl.num_programs(0)
    nu = nu_ref[0]
    slab = SLAB_ROWS
    nch = wbf.shape[1] // stage.shape[1]

    @pl.when(b == 0)
    def _():
        prime0, finish0 = _weight_stream(w_hbm, wbf, stage, wsem, be_ref[0], ws_ref[0], 0, nch)
        prime0()
        finish0()

    def row_copy(slot, r, d):
        return pltpu.make_async_copy(ybuf.at[slot, pl.ds(pl.multiple_of(r * SLAB_PITCH, SUBLANE), slab), :],
                                     ysc_ref.at[pl.ds(pl.multiple_of(d * slab, slab), slab), :],
                                     sem.at[slot])

    def drain(step):
        slot = step % 2
        count = nv_ref[step]

        @pl.when(count == BM)
        def _():
            pltpu.make_async_copy(ybuf.at[slot, pl.ds(0, BM * slab), :], ysc_ref.at[pl.ds(0, BM * slab), :],
                                  sem.at[slot]).wait()

        @pl.when(count < BM)
        def _():
            def body(r, c):
                row_copy(slot, 0, 0).wait()
                return c
            lax.fori_loop(0, count, body, 0)

    @pl.when((b >= 2) & (b < nu))
    def _():
        drain(b - 2)

    @pl.when(b < nu)
    def _():
        slot = b % 2
        wslot = ws_ref[b]
        prime, finish = _weight_stream(w_hbm, wbf, stage, wsem, wn_ref[b], 1 - wslot, wc0_ref[b], wc1_ref[b])
        prime()
        def compute(m):
            y = jnp.dot(a_ref[0:m, :], wbf[wslot], preferred_element_type=F32) + bias_ref[0]
            for s in range(slab):
                ybuf[slot, pl.ds(s, m, stride=SLAB_PITCH), :] = y[:, s * LANE:(s + 1) * LANE]

        _for_row_count(nv_ref[b], compute)
        finish()

        def body(r, c):
            row_copy(slot, r, dst_ref[0, 0, r]).start()
            return c

        def pair(i, c):
            row_copy(slot, 2 * i, dst_ref[0, 0, 2 * i]).start(priority=0)
            row_copy(slot, 2 * i + 1, dst_ref[0, 0, 2 * i + 1]).start(priority=1)
            return c

        @pl.when(nv_ref[b] == BM)
        def _():
            lax.fori_loop(0, BM // 2, pair, 0, unroll=DMA_UNROLL // 2)

        @pl.when(nv_ref[b] < BM)
        def _():
            lax.fori_loop(0, nv_ref[b], body, 0)

    @pl.when(b == nb - 1)
    def _():
        @pl.when(nu >= 2)
        def _():
            drain(nu - 2)
        drain(nu - 1)


def _moe_down(act, dst_slot, n_out_rows, wd, bd, sched):
    p, f = act.shape
    ne, _, d = wd.shape
    nb = p // BM
    grid_spec = pltpu.PrefetchScalarGridSpec(
        num_scalar_prefetch=7,
        grid=(nb,),
        in_specs=[pl.BlockSpec((1, 1, BM), lambda b, be, *_: (b, 0, 0), memory_space=pltpu.SMEM),
                  pl.BlockSpec((BM, f), lambda b, be, *_: (b, 0)),
                  pl.BlockSpec(memory_space=pl.ANY),
                  pl.BlockSpec((1, 1, d), lambda b, be, *_: (be[b], 0, 0))],
        out_specs=pl.BlockSpec(memory_space=pl.ANY),
        scratch_shapes=[pltpu.VMEM((2, BM * SLAB_PITCH, LANE), F32),
                        pltpu.VMEM((2, f, d), BF16),
                        pltpu.VMEM((2, _chunk_rows(d), d), F32),
                        pltpu.SemaphoreType.DMA((2,)), pltpu.SemaphoreType.DMA((2,))],
    )
    return pl.pallas_call(
        _moe_down_body,
        grid_spec=grid_spec,
        out_shape=jax.ShapeDtypeStruct((n_out_rows * SLAB_ROWS, LANE), F32),
        compiler_params=pltpu.CompilerParams(dimension_semantics=("arbitrary",),
                                             vmem_limit_bytes=48 * 1024 * 1024, has_side_effects=True),
        name="moe_down",
    )(sched["blk_e"], sched["n_used"], sched["wslot"], sched["wnext"], sched["wc0"], sched["wc1"],
      sched["nvalid"], dst_slot.reshape(nb, 1, BM), act, wd, bd.reshape(ne, 1, d))


def _final_body(y0_ref, y1_ref, y2_ref, y3_ref, rt_ref, x1_ref, ada_ref, g_ref, o_ref):
    tm = x1_ref.shape[0]
    rt = rt_ref[...]
    y_refs = (y0_ref, y1_ref, y2_ref, y3_ref)
    pieces = []
    for s in range(SLAB_ROWS):
        acc = rt[:, TOP_K:TOP_K + 1] * y_refs[0][pl.ds(s, tm, stride=SLAB_ROWS), :]
        for k in range(1, TOP_K):
            acc = acc + rt[:, TOP_K + k:TOP_K + k + 1] * y_refs[k][pl.ds(s, tm, stride=SLAB_ROWS), :]
        pieces.append(acc)
    y = jnp.concatenate(pieces, axis=1)
    gt_f = ada_ref[0, 5:6, :]
    o_ref[...] = x1_ref[...] + gt_f * (_rms(y) * g_ref[...])


def _final(ysc, rt, x1, ada3, g_post, seq):
    t, d = x1.shape
    tm = 256
    per_b = seq // tm
    nt = t // tm
    assert TOP_K == 4
    slot_spec = lambda k: pl.BlockSpec((tm * SLAB_ROWS, LANE), lambda i: (k * nt + i, 0))
    return pl.pallas_call(
        _final_body,
        grid=(nt,),
        in_specs=[slot_spec(0), slot_spec(1), slot_spec(2), slot_spec(3),
                  pl.BlockSpec((tm, LANE), lambda i: (i, 0)),
                  pl.BlockSpec((tm, d), lambda i: (i, 0)),
                  pl.BlockSpec((1, 6, d), lambda i: (i // per_b, 0, 0)),
                  pl.BlockSpec((1, d), lambda i: (0, 0))],
        out_specs=pl.BlockSpec((tm, d), lambda i: (i, 0)),
        out_shape=jax.ShapeDtypeStruct((t, d), F32),
        compiler_params=_cparams(("arbitrary",), 48),
        name="final",
    )(ysc, ysc, ysc, ysc, rt, x1, ada3, g_post.reshape(1, d))


def _mixer_ffn_layer(x2, ada3, bsz, seq, g_mix_pre, g_mix_post, g_ffn_pre, g_ffn_post, w_in, lb, g_rec_out,
                     rel_bias, w_branch_rec, w_branch_att, w_o, w_router, b_router, w_gate_up, b_gate_up,
                     w_down, b_down):
    t, d = x2.shape
    d_rec = w_branch_rec.shape[0]
    w_att = w_branch_att.shape[0]
    d_att = 3 * w_att
    widths = dict(q_r=d_rec, i_r=d_rec, zf_f=d_rec, zf_b=d_rec, z_o=d_rec, q_a=d_att, k_a=d_att, v_a=d_att,
                  zg_rec=d, zg_att=d)
    my_order = ("zg_rec", "zg_att", "q_r", "i_r", "zf_f", "zf_b", "z_o", "q_a", "k_a", "v_a")
    col, acc = {}, 0
    for name in my_order:
        col[name] = acc // LANE
        acc += widths[name]
    rot = acc - 2 * d

    proj = _inproj(x2, g_mix_pre, ada3, w_in, seq, rot)

    oi, qtf, qtb, utf, utb, df, db = _hgrn_a(proj, lb, col, t, d_rec)
    rec_o = _hgrn_c(oi, qtf, qtb, utf, utb, df, db, proj, g_rec_out, col, bsz, seq, d_rec)

    nums, stats = [], []
    for g, (window, dil) in enumerate(DIL_GROUPS):
        hs = slice(g * ATT_HEADS_PER_GROUP, (g + 1) * ATT_HEADS_PER_GROUP)
        bias = _band_bias(rel_bias[:, hs], window, dil)
        num, st = _attn_group(proj, bias, col, g, dil, bsz, seq)
        nums.append(num)
        stats.append(st)

    x1, h2s, logits = _merge(rec_o, nums, stats, proj, x2, ada3, g_mix_post, g_ffn_pre,
                            w_branch_rec.astype(BF16), w_branch_att.astype(BF16), w_o.astype(BF16),
                            w_router, b_router, col, seq)

    rt, cnt = _route(logits)
    ne = logits.shape[1]
    counts = cnt[0].astype(jnp.int32)
    top_idx = rt[:, 0:TOP_K].astype(jnp.int32)
    rank = rt[:, 2 * TOP_K:3 * TOP_K].astype(jnp.int32)
    padded = (counts + BM - 1) // BM * BM
    pends = jnp.cumsum(padded)
    pstarts = pends - padded
    experts = jnp.arange(ne, dtype=jnp.int32)
    pstart_sel = jnp.sum(jnp.where(top_idx[..., None] == experts, pstarts, 0), axis=-1)
    dest = (pstart_sel + rank).T.reshape(-1)
    p_rows = t * TOP_K + ne * BM
    nb = p_rows // BM
    blk_start = jnp.arange(nb, dtype=jnp.int32) * BM
    blk_e = jnp.minimum(jnp.sum((pends[None, :] <= blk_start[:, None]).astype(jnp.int32), axis=1), ne - 1)
    n_used = (pends[-1:] // BM).astype(jnp.int32)

    n_assign = t * TOP_K
    valid_end = pstarts + counts
    pad_lo = jnp.concatenate([valid_end, pends[-1:]]).astype(jnp.int32)
    pad_hi = jnp.concatenate([pends, jnp.full((1,), p_rows, jnp.int32)]).astype(jnp.int32)
    slot_assign = _invert_slots(dest, pad_lo, pad_hi, p_rows)
    blk_end = jnp.sum(jnp.where(blk_e[:, None] == experts, valid_end, 0), axis=-1)
    nvalid = jnp.clip(blk_end - blk_start, 0, BM).astype(jnp.int32)
    nvalid = jnp.where(jnp.arange(nb) < n_used[0], nvalid, 0)

    lookup = lambda table: jnp.sum(jnp.where(blk_e[:, None] == experts, table, 0), axis=-1)
    nonempty = padded > 0
    order = jnp.cumsum(nonempty.astype(jnp.int32)) - 1
    later = lax.cummin(jnp.where(nonempty, experts, ne)[::-1])[::-1]
    next_e = jnp.concatenate([later[1:], jnp.full((1,), ne, jnp.int32)])
    blk_next = lookup(next_e)
    has_next = (blk_next < ne) & (jnp.arange(nb) < n_used[0])
    k_in_run = jnp.arange(nb, dtype=jnp.int32) - lookup(pstarts // BM)
    n_in_run = jnp.maximum(lookup(padded // BM), 1)
    common = dict(blk_e=blk_e, n_used=n_used, nvalid=nvalid, wslot=lookup(order) % 2,
                  wnext=jnp.where(has_next, blk_next, blk_e))

    def schedule(w):
        nch = w.shape[1] // _chunk_rows(w.shape[2])
        s = dict(common, wc0=jnp.where(has_next, k_in_run * nch // n_in_run, 0),
                 wc1=jnp.where(has_next, (k_in_run + 1) * nch // n_in_run, 0))
        return {k: v.astype(jnp.int32) for k, v in s.items()}

    src_tok = jnp.maximum(slot_assign, 0) % t
    act = _moe_up(h2s, src_tok, w_gate_up, b_gate_up, schedule(w_gate_up))
    ysc = _moe_down(act, slot_assign, n_assign, w_down, b_down, schedule(w_down))
    return _final(ysc, rt, x1, ada3, g_ffn_post, seq)


def kernel(x, c, w_ada, b_ada, g_mix_pre, g_mix_post, g_ffn_pre, g_ffn_post, w_in, g_rec_out, w_branch_rec,
           w_branch_att, w_o, w_router, b_router, w_gate_up, b_gate_up, w_down, b_down, rec_lb_table, rel_bias):
    bsz, seq, d = x.shape
    depth = w_in.shape[0]
    lb_all = jnp.cumsum(jax.nn.softmax(rec_lb_table.astype(F32), axis=1), axis=1)
    x2 = x.reshape(bsz * seq, d)
    for layer in range(depth):
        ada3 = _ada(c, w_ada[layer], b_ada[layer]).reshape(bsz, 6, d)
        x2 = _mixer_ffn_layer(x2, ada3, bsz, seq, g_mix_pre[layer], g_mix_post[layer], g_ffn_pre[layer],
                              g_ffn_post[layer], w_in[layer], lb_all[:, layer], g_rec_out[layer], rel_bias,
                              w_branch_rec[layer], w_branch_att[layer], w_o[layer], w_router[layer],
                              b_router[layer], w_gate_up[layer], b_gate_up[layer], w_down[layer],
                              b_down[layer])
    return x2.reshape(bsz, seq, d)
```

```python
import functools
import math

import numpy as np
import jax
import jax.numpy as jnp
from jax import lax
from jax.experimental import pallas as pl
from jax.experimental.pallas import tpu as pltpu

F32 = jnp.float32
BF16 = jnp.bfloat16

LANE = 128
SUBLANE = 8
SLAB_ROWS = 16
SLAB_PITCH = 24

REC_HEAD_DIM = 128
REC_CHUNK = 64
ATT_HEAD_DIM = 128
ATT_HEADS_PER_GROUP = 4
ATT_BLOCK = 64
DIL_GROUPS = ((128, 1), (512, 4), (2048, 16))
NUM_BUCKETS = 32
MAX_DISTANCE = 1024
N_EXPERTS = 32
TOP_K = 4
SWIGLU_LIMIT = 7.0
SWIGLU_ALPHA = 1.702
RMS_EPS = 1e-6
NEG_INF = -1e30

N_LEVELS = 6
W_CHUNK_BYTES = 4 * 1024 * 1024
CAST_ROWS = 64
ROW_PATHS = 4
HGRN_UNROLL = 4
DMA_UNROLL = 8
ATT_UNROLL = 8
BM = 256

_NT = (((1,), (1,)), ((), ()))
_TN = (((0,), (0,)), ((), ()))


def _cparams(sem, vmem_mb):
    return pltpu.CompilerParams(dimension_semantics=sem, vmem_limit_bytes=vmem_mb * 1024 * 1024)


def _rms(x):
    return x * lax.rsqrt(jnp.mean(x * x, axis=-1, keepdims=True) + RMS_EPS)


def _ada_body(c_ref, w_ref, b_ref, o_ref):
    c = c_ref[...]
    cond = (c * jax.nn.sigmoid(c)).astype(BF16)
    o_ref[...] = jnp.dot(cond, w_ref[...].astype(BF16), preferred_element_type=F32) + b_ref[...]


def _ada(c, w, b):
    bsz, d = c.shape
    n = w.shape[1]
    tn = 1024
    cp = jnp.zeros((SUBLANE, d), F32).at[:bsz].set(c)
    out = pl.pallas_call(
        _ada_body,
        grid=(n // tn,),
        in_specs=[pl.BlockSpec((SUBLANE, d), lambda j: (0, 0)),
                  pl.BlockSpec((d, tn), lambda j: (0, j)),
                  pl.BlockSpec((1, tn), lambda j: (0, j))],
        out_specs=pl.BlockSpec((SUBLANE, tn), lambda j: (0, j)),
        out_shape=jax.ShapeDtypeStruct((SUBLANE, n), F32),
        compiler_params=_cparams(("arbitrary",), 40),
        name="ada",
    )(cp, w, b.reshape(1, n))
    return out[:bsz]


def _inproj_body(x_ref, g_ref, ada_ref, w_ref, o_ref, h_ref):
    @pl.when(pl.program_id(1) == 0)
    def _():
        half = x_ref.shape[0] // 2
        sh = ada_ref[0, 0:1, :]
        sc = ada_ref[0, 1:2, :]
        for r0 in (0, half):
            y = _rms(x_ref[r0:r0 + half, :]) * g_ref[...]
            h_ref[r0:r0 + half, :] = (y * (1.0 + sc) + sh).astype(BF16)

    o_ref[...] = jnp.dot(h_ref[...], w_ref[...].astype(BF16), preferred_element_type=F32)


def _inproj(x2, g, ada3, w_in, seq, rot):
    t, d = x2.shape
    n = w_in.shape[1]
    tm, tn = 2048, 512
    per_b = seq // tm
    nj = n // tn
    assert rot % tn == 0 and n % tn == 0 and seq % tm == 0
    return pl.pallas_call(
        _inproj_body,
        grid=(t // tm, nj),
        in_specs=[pl.BlockSpec((tm, d), lambda i, j: (i, 0), pipeline_mode=pl.Buffered(1)),
                  pl.BlockSpec((1, d), lambda i, j: (0, 0)),
                  pl.BlockSpec((1, 6, d), lambda i, j: (i // per_b, 0, 0)),
                  pl.BlockSpec((d, tn), lambda i, j: (0, (j + rot // tn) % nj))],
        out_specs=pl.BlockSpec((tm, tn), lambda i, j: (i, j)),
        out_shape=jax.ShapeDtypeStruct((t, n), F32),
        scratch_shapes=[pltpu.VMEM((tm, d), BF16)],
        compiler_params=_cparams(("arbitrary", "arbitrary"), 56),
        name="inproj",
    )(x2, g.reshape(1, d), ada3, w_in)


def _hgrn_consts():
    c = REC_CHUNK
    r = np.arange(c)[:, None]
    m = np.arange(c)[None, :]
    nw = N_LEVELS - 1
    wf = np.zeros(((nw + 2) * c, c), np.float32)
    wb = np.zeros(((nw + 2) * c, c), np.float32)
    mf = np.zeros((N_LEVELS + 1, c, c), np.float32)
    for lvl in range(N_LEVELS):
        s = 32 >> lvl
        m0 = (r // (2 * s)) * (2 * s) + s
        up = r >= m0
        if lvl < nw:
            wf[lvl * c:(lvl + 1) * c] = np.where(up, (m >= m0) & (m <= r), (m > r) & (m <= m0 - 1))
            wb[lvl * c:(lvl + 1) * c] = np.where(up, (m >= m0) & (m <= r - 1), (m >= r) & (m <= m0 - 1))
        i = np.arange(c)[:, None]
        j = np.arange(c)[None, :]
        mf[lvl] = (i // (2 * s) == j // (2 * s)) & (i % (2 * s) >= s) & (j % (2 * s) < s)
    mf[N_LEVELS] = np.eye(c)
    wf[nw * c:(nw + 1) * c] = m <= r
    wf[(nw + 1) * c:(nw + 2) * c] = m > r
    wb[nw * c:(nw + 1) * c] = m >= r
    wb[(nw + 1) * c:(nw + 2) * c] = m < r
    mfb = mf + np.transpose(mf, (0, 2, 1))
    mfb[N_LEVELS] = np.eye(c)
    up = np.zeros((N_LEVELS, c, LANE), np.float32)
    for lvl in range(N_LEVELS):
        s = 32 >> lvl
        up[lvl] = ((np.arange(c) % (2 * s)) >= s)[:, None]
    wf3 = np.concatenate([wf, wf, wf], axis=1)
    wb3 = np.concatenate([wb, wb, wb], axis=1)
    return (jnp.asarray(wf3, BF16), jnp.asarray(wb3, BF16), jnp.asarray(mfb, F32), jnp.asarray(up, F32),
            jnp.asarray(1.0 - up, F32))


def _split3(g):
    hi = g.astype(BF16)
    r1 = g - hi.astype(F32)
    mid = r1.astype(BF16)
    lo = (r1 - mid.astype(F32)).astype(BF16)
    return jnp.concatenate([hi, mid, lo], axis=0)


def _hgrn_a_body(q_ref, i_ref, zf_ref, zb_ref, lb_ref, wf_ref, wb_ref, mf_ref, up_ref, lo_ref,
                 oi_ref, qtf_ref, qtb_ref, utf_ref, utb_ref, df_ref, db_ref, *, cpb):
    c = REC_CHUNK
    dirs = ((zf_ref, wf_ref, None, qtf_ref, utf_ref, df_ref, 0, c - 1),
            (zb_ref, wb_ref, None, qtb_ref, utb_ref, db_ref, 1, 0))

    def chunk_group(cg, carry):
        cis = [cg * HGRN_UNROLL + u for u in range(HGRN_UNROLL)]
        rows = [pl.ds(pl.multiple_of(ci * c, c), c) for ci in cis]
        zqs = [q_ref[rw, :] for rw in rows]
        qs = [zq * jax.nn.sigmoid(zq) for zq in zqs]
        vbs = [i_ref[rw, :].astype(BF16) for rw in rows]
        units = [(u, d) for u in range(HGRN_UNROLL) for d in range(2)]
        nw = N_LEVELS - 1
        ks, es, fs = {}, {}, {}
        for u, d in units:
            z_ref, w_ref = dirs[d][0], dirs[d][1]
            lb = lb_ref[d:d + 1, :]
            f = lb + (1.0 - lb) * jax.nn.sigmoid(z_ref[rows[u], :])
            fs[u, d] = f
            ks[u, d] = 1.0 - f
            es[u, d] = jnp.exp(jnp.dot(w_ref[...], _split3(jnp.log(f)), preferred_element_type=F32))
        acc = [jnp.zeros((c, c), F32) for _ in range(HGRN_UNROLL)]
        for lvl in range(N_LEVELS + 1):
            for u in range(HGRN_UNROLL):
                if lvl < N_LEVELS:
                    up, lo = up_ref[lvl], lo_ref[lvl]
                    if lvl < nw:
                        ef = es[u, 0][lvl * c:(lvl + 1) * c]
                        eb = es[u, 1][lvl * c:(lvl + 1) * c]
                    else:
                        ef = fs[u, 0] * up + lo
                        eb = fs[u, 1] * lo + up
                    qa = jnp.concatenate([(qs[u] * (ef * up)).astype(BF16), (qs[u] * (eb * lo)).astype(BF16)],
                                         axis=1)
                    ka = jnp.concatenate([(ks[u, 0] * (ef * lo)).astype(BF16),
                                          (ks[u, 1] * (eb * up)).astype(BF16)], axis=1)
                else:
                    qa, ka = qs[u].astype(BF16), (ks[u, 0] + ks[u, 1]).astype(BF16)
                p = lax.dot_general(qa, ka, _NT, preferred_element_type=F32)
                acc[u] = acc[u] + p * mf_ref[lvl]
        for u, d in units:
            _, _, _, qt_ref, ut_ref, d_ref, _, drow = dirs[d]
            e = es[u, d]
            qt_ref[rows[u], :] = (qs[u] * e[nw * c:(nw + 1) * c]).astype(BF16)
            kt = (ks[u, d] * e[(nw + 1) * c:(nw + 2) * c]).astype(BF16)
            ut_ref[cis[u]] = lax.dot_general(vbs[u], kt, _TN, preferred_element_type=F32)
            d_ref[pl.ds(cis[u], 1), :] = e[nw * c + drow:nw * c + drow + 1]
        for u in range(HGRN_UNROLL):
            oi_ref[rows[u], :] = jnp.dot(acc[u].astype(BF16), vbs[u], preferred_element_type=F32)
        return carry

    lax.fori_loop(0, cpb // HGRN_UNROLL, chunk_group, 0)


def _hgrn_a(proj, lb, col, t, d_rec):
    heads = d_rec // REC_HEAD_DIM
    tq = 1024
    cpb = tq // REC_CHUNK
    nchunks = t // REC_CHUNK
    wf, wb, mf, up, lo = _hgrn_consts()
    hd = REC_HEAD_DIM

    def colspec(off):
        return pl.BlockSpec((tq, hd), lambda i, h: (i, off + h))

    full2 = lambda i, h: (0, 0)
    full3 = lambda i, h: (0, 0, 0)
    row_spec = pl.BlockSpec((tq, hd), lambda i, h: (i, h))
    u_spec = pl.BlockSpec((cpb, hd, hd), lambda i, h: (i, 0, h))
    d_spec = pl.BlockSpec((cpb, hd), lambda i, h: (i, h))
    return pl.pallas_call(
        functools.partial(_hgrn_a_body, cpb=cpb),
        grid=(t // tq, heads),
        in_specs=[colspec(col["q_r"]), colspec(col["i_r"]), colspec(col["zf_f"]), colspec(col["zf_b"]),
                  pl.BlockSpec((2, hd), lambda i, h: (0, h)),
                  pl.BlockSpec(wf.shape, full2), pl.BlockSpec(wb.shape, full2),
                  pl.BlockSpec(mf.shape, full3), pl.BlockSpec(up.shape, full3), pl.BlockSpec(lo.shape, full3)],
        out_specs=[row_spec, row_spec, row_spec, u_spec, u_spec, d_spec, d_spec],
        out_shape=[jax.ShapeDtypeStruct((t, d_rec), F32),
                   jax.ShapeDtypeStruct((t, d_rec), BF16),
                   jax.ShapeDtypeStruct((t, d_rec), BF16),
                   jax.ShapeDtypeStruct((nchunks, hd, d_rec), F32),
                   jax.ShapeDtypeStruct((nchunks, hd, d_rec), F32),
                   jax.ShapeDtypeStruct((nchunks, d_rec), F32),
                   jax.ShapeDtypeStruct((nchunks, d_rec), F32)],
        compiler_params=_cparams(("arbitrary", "arbitrary"), 32),
        name="hgrn_a",
    )(proj, proj, proj, proj, lb, wf, wb, mf, up, lo)


def _hgrn_c_body(oi_ref, qtf_ref, qtb_ref, utf_ref, utb_ref, df_ref, db_ref, z_ref, g_ref,
                 out_ref, acc_ref, accb_ref, *, nchunks):
    c = REC_CHUNK
    hd = REC_HEAD_DIM

    unroll = 4

    def step(i, carry):
        st_f, st_b = carry
        pending = []
        for u in range(unroll):
            nf = i * unroll + u
            nb = nchunks - 1 - nf
            rows_f = pl.ds(pl.multiple_of(nf * c, c), c)
            rows_b = pl.ds(pl.multiple_of(nb * c, c), c)
            of = lax.dot_general(qtf_ref[rows_f, :], st_f.astype(BF16), _NT, preferred_element_type=F32)
            ob = lax.dot_general(qtb_ref[rows_b, :], st_b.astype(BF16), _NT, preferred_element_type=F32)
            pending.append((rows_f, rows_b, oi_ref[rows_f, :] + of, ob))
            st_f = df_ref[pl.ds(nf, 1), :] * st_f + utf_ref[nf]
            st_b = db_ref[pl.ds(nb, 1), :] * st_b + utb_ref[nb]
        for rows_f, rows_b, vf, vb in pending:
            acc_ref[rows_f, :] = vf
            accb_ref[rows_b, :] = vb
        return st_f, st_b

    zero = jnp.zeros((hd, hd), F32)
    lax.fori_loop(0, nchunks // unroll, step, (zero, zero))

    o = _rms(acc_ref[...] + accb_ref[...])
    out_ref[...] = (o * g_ref[...] * jax.nn.sigmoid(z_ref[...])).astype(BF16)


def _hgrn_c(oi, qtf, qtb, utf, utb, df, db, proj, g_out, col, bsz, seq, d_rec):
    heads = d_rec // REC_HEAD_DIM
    hd = REC_HEAD_DIM
    nchunks = seq // REC_CHUNK
    row_spec = pl.BlockSpec((seq, hd), lambda b, h: (b, h))
    u_spec = pl.BlockSpec((nchunks, hd, hd), lambda b, h: (b, 0, h))
    d_spec = pl.BlockSpec((nchunks, hd), lambda b, h: (b, h))
    zo = col["z_o"]
    return pl.pallas_call(
        functools.partial(_hgrn_c_body, nchunks=nchunks),
        grid=(bsz, heads),
        in_specs=[row_spec, row_spec, row_spec, u_spec, u_spec, d_spec, d_spec,
                  pl.BlockSpec((seq, hd), lambda b, h: (b, zo + h)),
                  pl.BlockSpec((1, hd), lambda b, h: (0, h))],
        out_specs=row_spec,
        out_shape=jax.ShapeDtypeStruct((bsz * seq, d_rec), BF16),
        scratch_shapes=[pltpu.VMEM((seq, hd), F32), pltpu.VMEM((seq, hd), F32)],
        compiler_params=_cparams(("arbitrary", "arbitrary"), 48),
        name="hgrn_c",
    )(oi, qtf, qtb, utf, utb, df, db, proj, g_out.reshape(1, d_rec))


def _t5_bucket(rel):
    half_buckets = NUM_BUCKETS // 2
    ret = np.where(rel > 0, half_buckets, 0)
    n = np.abs(rel)
    max_exact = half_buckets // 2
    nf = np.maximum(n, 1).astype(np.float32)
    large = max_exact + (np.log(nf / np.float32(max_exact)) / np.float32(math.log(MAX_DISTANCE / max_exact))
                         * np.float32(half_buckets - max_exact)).astype(np.int32)
    large = np.minimum(large, half_buckets - 1)
    return ret + np.where(n < max_exact, n, large)


def _band_bias(rel_bias_g, window, dil):
    half = window // (2 * dil)
    q_off = np.arange(ATT_BLOCK)[:, None]
    rel = np.arange(3 * ATT_BLOCK)[None, :] - ATT_BLOCK - q_off
    onehot = (_t5_bucket(rel * dil)[..., None] == np.arange(NUM_BUCKETS)).astype(np.float32)
    bias = jnp.einsum("qkb,bh->hqk", jnp.asarray(onehot), rel_bias_g.astype(F32),
                      precision=lax.Precision.HIGHEST)
    return jnp.where(jnp.asarray(np.abs(rel) <= half)[None], bias, NEG_INF)


def _attn_body(q_ref, kp_ref, k_ref, kn_ref, vp_ref, v_ref, vn_ref, bias_ref,
               num_ref, st_ref, kc_ref, vc_ref, *, dil, tq, sub_len):
    blk = ATT_BLOCK
    nqb = tq // blk
    n = pl.program_id(1)
    scale = ATT_HEAD_DIM ** -0.5

    def sds(start, size):
        if dil == 1:
            return pl.ds(start, size)
        return pl.ds(start, size, stride=dil)

    cu = kc_ref.shape[0]
    qu = ATT_UNROLL // cu

    def deinterleave(r, j):
        kc_ref[j, 0:blk, :] = kp_ref[sds(r, blk), :].astype(BF16)
        kc_ref[j, blk:blk + tq, :] = k_ref[sds(r, tq), :].astype(BF16)
        kc_ref[j, blk + tq:2 * blk + tq, :] = kn_ref[sds(r, blk), :].astype(BF16)
        vc_ref[j, 0:blk, :] = vp_ref[sds(r, blk), :].astype(BF16)
        vc_ref[j, blk:blk + tq, :] = v_ref[sds(r, tq), :].astype(BF16)
        vc_ref[j, blk + tq:2 * blk + tq, :] = vn_ref[sds(r, blk), :].astype(BF16)

    def units(r0, qb0):
        us = [(j, u) for j in range(cu) for u in range(qu)]
        q0s = [pl.multiple_of((qb0 + u) * blk, blk) for _, u in us]
        rows = [sds(r0 + j + dil * q0, blk) for (j, _), q0 in zip(us, q0s)]
        lane = lax.broadcasted_iota(jnp.int32, (blk, LANE), 1)
        key_iota = lax.broadcasted_iota(jnp.int32, (1, 3 * blk), 1)
        bias = bias_ref[0]
        qs = [q_ref[rw, :].astype(BF16) for rw in rows]
        kws = [kc_ref[j, pl.ds(q0, 3 * blk), :] for (j, _), q0 in zip(us, q0s)]
        vws = [vc_ref[j, pl.ds(q0, 3 * blk), :] for (j, _), q0 in zip(us, q0s)]
        ss = [lax.dot_general(q, kw, _NT, preferred_element_type=F32) * scale for q, kw in zip(qs, kws)]
        valids = []
        for q0 in q0s:
            kpos = n * tq + q0 - blk + key_iota
            valids.append((kpos >= 0) & (kpos < sub_len))
        ss = [jnp.where(valid, s + bias, NEG_INF) for s, valid in zip(ss, valids)]
        ms = [jnp.max(s, axis=-1, keepdims=True) for s in ss]
        ps = [jnp.exp(s - m) for s, m in zip(ss, ms)]
        ls = [jnp.sum(p, axis=-1, keepdims=True) for p in ps]
        nums = [jnp.dot(p.astype(BF16), vw, preferred_element_type=F32) for p, vw in zip(ps, vws)]
        for rw, num, m, l in zip(rows, nums, ms, ls):
            num_ref[rw, :] = num
            st_ref[rw, :] = jnp.where(lane < LANE // 2, m, l)

    def class_group(rg, carry):
        r0 = rg * cu
        for j in range(cu):
            deinterleave(r0 + j, j)

        def qgroup(qg, carry2):
            units(r0, qg * qu)
            return carry2

        lax.fori_loop(0, nqb // qu, qgroup, 0)
        return carry

    lax.fori_loop(0, dil // cu, class_group, 0)


def _attn_group(proj, bias, col, g, dil, bsz, seq):
    tile = 1024
    tq = tile // dil
    halo = ATT_BLOCK * dil
    sub_len = seq // dil
    cu = ATT_UNROLL // min(tq // ATT_BLOCK, ATT_UNROLL)
    nh = ATT_HEADS_PER_GROUP
    hd = ATT_HEAD_DIM
    qc = col["q_a"] + g * nh
    kc = col["k_a"] + g * nh
    vc = col["v_a"] + g * nh
    tiles_b = seq // tile
    halos_b = seq // halo
    hpt = tile // halo

    own = lambda c: pl.BlockSpec((tile, hd), lambda b, n, h: (b * tiles_b + n, c + h))
    prev = lambda c: pl.BlockSpec(
        (halo, hd), lambda b, n, h: (b * halos_b + jnp.maximum(n * hpt - 1, 0), c + h))
    nxt = lambda c: pl.BlockSpec(
        (halo, hd), lambda b, n, h: (b * halos_b + jnp.minimum((n + 1) * hpt, halos_b - 1), c + h))
    t = bsz * seq
    return pl.pallas_call(
        functools.partial(_attn_body, dil=dil, tq=tq, sub_len=sub_len),
        grid=(bsz, tiles_b, nh),
        in_specs=[own(qc), prev(kc), own(kc), nxt(kc), prev(vc), own(vc), nxt(vc),
                  pl.BlockSpec((1,) + bias.shape[1:], lambda b, n, h: (h, 0, 0))],
        out_specs=[pl.BlockSpec((tile, hd), lambda b, n, h: (b * tiles_b + n, h)),
                   pl.BlockSpec((tile, LANE), lambda b, n, h: (b * tiles_b + n, h))],
        out_shape=[jax.ShapeDtypeStruct((t, nh * hd), F32), jax.ShapeDtypeStruct((t, nh * LANE), F32)],
        scratch_shapes=[pltpu.VMEM((cu, tq + 2 * ATT_BLOCK, hd), BF16),
                        pltpu.VMEM((cu, tq + 2 * ATT_BLOCK, hd), BF16)],
        compiler_params=_cparams(("arbitrary", "arbitrary", "arbitrary"), 32),
        name=f"attn_d{dil}",
    )(proj, proj, proj, proj, proj, proj, proj, bias)


def _merge_body(rec_ref, n0_ref, n1_ref, n2_ref, s0_ref, s1_ref, s2_ref, zgr_ref, zga_ref, x_ref,
                ada_ref, gpost_ref, gpre_ref, wbr_ref, wba_ref, wo_ref, wr_ref, br_ref,
                x1_ref, h2_ref, lg_ref):
    nh = ATT_HEADS_PER_GROUP
    hd = ATT_HEAD_DIM
    half = LANE // 2
    lane = lax.broadcasted_iota(jnp.int32, (rec_ref.shape[0], LANE), 1)
    heads = []
    for h in range(nh):
        cols = slice(h * hd, (h + 1) * hd)
        st = [s[:, cols] for s in (s0_ref, s1_ref, s2_ref)]
        top = jnp.maximum(jnp.maximum(st[0], st[1]), st[2])
        ws = [jnp.exp(s - top) for s in st]
        den = (ws[0] * pltpu.roll(st[0], half, 1) + ws[1] * pltpu.roll(st[1], half, 1)
               + ws[2] * pltpu.roll(st[2], half, 1))
        coef = [w / den for w in ws]
        coef = [jnp.where(lane < half, c, pltpu.roll(c, half, 1)) for c in coef]
        num = coef[0] * n0_ref[:, cols] + coef[1] * n1_ref[:, cols] + coef[2] * n2_ref[:, cols]
        heads.append(num.astype(BF16))
    att = jnp.concatenate(heads, axis=1)
    y_rec = jnp.dot(rec_ref[...], wbr_ref[...], preferred_element_type=F32)
    y_att = jnp.dot(att, wba_ref[...], preferred_element_type=F32)
    merged = jax.nn.sigmoid(zgr_ref[...]) * y_rec + jax.nn.sigmoid(zga_ref[...]) * y_att
    y = jnp.dot(merged.astype(BF16), wo_ref[...], preferred_element_type=F32)
    gt_m = ada_ref[0, 2:3, :]
    sh_f = ada_ref[0, 3:4, :]
    sc_f = ada_ref[0, 4:5, :]
    x1 = x_ref[...] + gt_m * (_rms(y) * gpost_ref[...])
    x1_ref[...] = x1
    h2 = _rms(x1) * gpre_ref[...] * (1.0 + sc_f) + sh_f
    tm = h2.shape[0]
    for s in range(SLAB_ROWS):
        h2_ref[pl.ds(s, tm, stride=SLAB_ROWS), :] = h2[:, s * LANE:(s + 1) * LANE]
    ne = lg_ref.shape[1]
    h_hi = h2.astype(BF16)
    h_lo = (h2 - h_hi.astype(F32)).astype(BF16)
    both = jnp.dot(h_hi, wr_ref[...], preferred_element_type=F32)
    cross = jnp.dot(h_lo, wr_ref[:, 0:ne], preferred_element_type=F32)
    lg_ref[...] = both[:, 0:ne] + both[:, ne:2 * ne] + cross + br_ref[...]


def _merge(rec_o, nums, stats, proj, x2, ada3, g_post, g_pre, wbr, wba, wo, w_router, b_router, col, seq):
    t, d = x2.shape
    tm = 256
    per_b = seq // tm
    d_rec = rec_o.shape[1]
    w_att = nums[0].shape[1]
    ne = w_router.shape[1]
    wr_hi = w_router.astype(BF16)
    wr_lo = (w_router - wr_hi.astype(F32)).astype(BF16)
    w_router = jnp.concatenate([wr_hi, wr_lo], axis=1)
    dl = d // LANE
    row = lambda w: pl.BlockSpec((tm, w), lambda i: (i, 0))
    const = lambda shape: pl.BlockSpec(shape, lambda i: (0,) * len(shape), pipeline_mode=pl.Buffered(1))
    zgr = col["zg_rec"] // dl
    zga = col["zg_att"] // dl
    return pl.pallas_call(
        _merge_body,
        grid=(t // tm,),
        in_specs=[row(d_rec), row(w_att), row(w_att), row(w_att), row(w_att), row(w_att), row(w_att),
                  pl.BlockSpec((tm, d), lambda i: (i, zgr)),
                  pl.BlockSpec((tm, d), lambda i: (i, zga)),
                  row(d),
                  pl.BlockSpec((1, 6, d), lambda i: (i // per_b, 0, 0)),
                  const((1, d)), const((1, d)),
                  const(wbr.shape), const(wba.shape), const(wo.shape), const(w_router.shape),
                  const((1, ne))],
        out_specs=[row(d), pl.BlockSpec((tm * SLAB_ROWS, LANE), lambda i: (i, 0)),
                   pl.BlockSpec((tm, ne), lambda i: (i, 0))],
        out_shape=[jax.ShapeDtypeStruct((t, d), F32), jax.ShapeDtypeStruct((t * SLAB_ROWS, LANE), F32),
                   jax.ShapeDtypeStruct((t, ne), F32)],
        compiler_params=_cparams(("arbitrary",), 56),
        name="merge",
    )(rec_o, nums[0], nums[1], nums[2], stats[0], stats[1], stats[2], proj, proj, x2, ada3,
      g_post.reshape(1, d), g_pre.reshape(1, d), wbr, wba, wo, w_router, b_router.reshape(1, ne))


def _route_body(lg_ref, tri_ref, rt_ref, cnt_ref, carry_ref):
    i = pl.program_id(0)
    tr, ne = lg_ref.shape

    @pl.when(i == 0)
    def _():
        carry_ref[...] = jnp.zeros_like(carry_ref)

    l = lg_ref[...]
    lane = lax.broadcasted_iota(jnp.int32, (tr, ne), 1).astype(F32)
    vals, sels, idxs = [], [], []
    for _ in range(TOP_K):
        m = jnp.max(l, axis=-1, keepdims=True)
        idx = jnp.min(jnp.where(l == m, lane, float(ne)), axis=-1, keepdims=True)
        sel = lane == idx
        vals.append(m)
        idxs.append(idx)
        sels.append(sel)
        l = jnp.where(sel, -jnp.inf, l)
    es = [jnp.exp(v - vals[0]) for v in vals]
    tot = es[0] + es[1] + es[2] + es[3]
    chosen = (sels[0] | sels[1] | sels[2] | sels[3]).astype(F32)
    prefix = jnp.dot(tri_ref[...], chosen.astype(BF16), preferred_element_type=F32) + carry_ref[0:1, :]
    out_lane = lax.broadcasted_iota(jnp.int32, (tr, LANE), 1)
    rt = jnp.zeros((tr, LANE), F32)
    for k in range(TOP_K):
        rank = jnp.sum(jnp.where(sels[k], prefix, 0.0), axis=-1, keepdims=True)
        rt = jnp.where(out_lane == k, idxs[k], rt)
        rt = jnp.where(out_lane == TOP_K + k, es[k] / tot, rt)
        rt = jnp.where(out_lane == 2 * TOP_K + k, rank, rt)
    rt_ref[...] = rt
    new = carry_ref[0:1, :] + jnp.sum(chosen, axis=0, keepdims=True)
    carry_ref[...] = jnp.broadcast_to(new, carry_ref.shape)
    cnt_ref[...] = carry_ref[...]


def _route(logits):
    t, ne = logits.shape
    tr = 512
    tri = jnp.asarray(np.tril(np.ones((tr, tr), np.float32), -1), BF16)
    return pl.pallas_call(
        _route_body,
        grid=(t // tr,),
        in_specs=[pl.BlockSpec((tr, ne), lambda i: (i, 0)),
                  pl.BlockSpec((tr, tr), lambda i: (0, 0))],
        out_specs=[pl.BlockSpec((tr, LANE), lambda i: (i, 0)),
                   pl.BlockSpec((SUBLANE, ne), lambda i: (0, 0))],
        out_shape=[jax.ShapeDtypeStruct((t, LANE), F32), jax.ShapeDtypeStruct((SUBLANE, ne), F32)],
        scratch_shapes=[pltpu.VMEM((SUBLANE, ne), F32)],
        compiler_params=_cparams(("arbitrary",), 32),
        name="route",
    )(logits, tri)


def _invert_body(pad_lo_ref, pad_hi_ref, dest_ref, out_ref, *, chunk):
    base = pl.program_id(0) * chunk

    @pl.when(pl.program_id(0) == 0)
    def _():
        def one_range(e, c):
            def fill(p, c2):
                out_ref[p] = -1
                return c2
            lax.fori_loop(pad_lo_ref[e], pad_hi_ref[e], fill, 0)
            return c
        lax.fori_loop(0, pad_lo_ref.shape[0], one_range, 0)

    def body(j, c):
        out_ref[dest_ref[0, 0, j]] = base + j
        return c
    lax.fori_loop(0, chunk, body, 0, unroll=DMA_UNROLL)


def _invert_slots(dest, pad_lo, pad_hi, p_rows):
    n = dest.shape[0]
    chunk = 8192
    nch = n // chunk
    grid_spec = pltpu.PrefetchScalarGridSpec(
        num_scalar_prefetch=2,
        grid=(nch,),
        in_specs=[pl.BlockSpec((1, 1, chunk), lambda i, lo, hi: (i, 0, 0), memory_space=pltpu.SMEM)],
        out_specs=pl.BlockSpec(memory_space=pltpu.SMEM),
    )
    return pl.pallas_call(
        functools.partial(_invert_body, chunk=chunk),
        grid_spec=grid_spec,
        out_shape=jax.ShapeDtypeStruct((p_rows,), jnp.int32),
        compiler_params=pltpu.CompilerParams(dimension_semantics=("arbitrary",)),
        name="invert_slots",
    )(pad_lo, pad_hi, dest.reshape(nch, 1, chunk))


def _for_row_count(nvalid, compute):
    q = BM // ROW_PATHS
    for i in range(1, ROW_PATHS + 1):
        lo, hi = (i - 1) * q, i * q
        cond = (nvalid > lo) & (nvalid <= hi) if i > 1 else (nvalid <= hi)

        @pl.when(cond)
        def _(m=hi):
            compute(m)


def _chunk_rows(ncols):
    return W_CHUNK_BYTES // (4 * ncols)


def _weight_stream(w_hbm, wbf, stage, wsem, e, slot, c0, c1, priority=0):
    kc = stage.shape[1]

    def copy(c):
        return pltpu.make_async_copy(w_hbm.at[e, pl.ds(pl.multiple_of(c * kc, kc), kc), :],
                                     stage.at[c % 2], wsem.at[c % 2])

    def prime():
        def body(c, carry):
            copy(c).start(priority=priority)
            return carry
        lax.fori_loop(c0, jnp.minimum(c0 + 2, c1), body, 0)

    def finish():
        def body(c, carry):
            copy(c).wait()
            buf = c % 2

            def cast(i, carry2):
                r = pl.multiple_of(i * CAST_ROWS, CAST_ROWS)
                wbf[slot, pl.ds(pl.multiple_of(c * kc, kc) + r, CAST_ROWS), :] = (
                    stage[buf, pl.ds(r, CAST_ROWS), :].astype(BF16))
                return carry2
            lax.fori_loop(0, kc // CAST_ROWS, cast, 0)

            @pl.when(c + 2 < c1)
            def _():
                copy(c + 2).start(priority=priority)
            return carry
        lax.fori_loop(c0, c1, body, 0)

    return prime, finish


def _moe_up_body(be_ref, nu_ref, ws_ref, wn_ref, wc0_ref, wc1_ref, nv_ref, idx0_ref, idxn_ref, h2s_ref, w_hbm,
                 bias_ref, o_ref, xbuf, wbf, stage, sem, wsem):
    b = pl.program_id(0)
    nu = nu_ref[0]
    f = o_ref.shape[1]
    slab = SLAB_ROWS
    nch = wbf.shape[1] // stage.shape[1]

    def row_start(idx_ref, slot, r):
        tok = idx_ref[0, 0, r]
        pltpu.make_async_copy(h2s_ref.at[pl.ds(pl.multiple_of(tok * slab, slab), slab), :],
                              xbuf.at[slot, pl.ds(pl.multiple_of(r * SLAB_PITCH, SUBLANE), slab), :],
                              sem.at[slot]).start()

    def wait_rows(slot):
        pltpu.make_async_copy(h2s_ref.at[pl.ds(0, BM * slab), :], xbuf.at[slot, pl.ds(0, BM * slab), :],
                              sem.at[slot]).wait()

    def issue(idx_ref, slot):
        def body(r, c):
            row_start(idx_ref, slot, r)
            return c
        lax.fori_loop(0, BM, body, 0, unroll=DMA_UNROLL)

    @pl.when(b == 0)
    def _():
        issue(idx0_ref, 0)
        prime0, finish0 = _weight_stream(w_hbm, wbf, stage, wsem, be_ref[0], ws_ref[0], 0, nch)
        prime0()
        finish0()

    @pl.when(b + 1 < nu)
    def _():
        issue(idxn_ref, (b + 1) % 2)

    @pl.when(b < nu)
    def _():
        wslot = ws_ref[b]
        prime, finish = _weight_stream(w_hbm, wbf, stage, wsem, wn_ref[b], 1 - wslot, wc0_ref[b], wc1_ref[b],
                                       priority=1)
        prime()
        slot = b % 2
        wait_rows(slot)

        def compute(m):
            x = jnp.concatenate([xbuf[slot, pl.ds(s, m, stride=SLAB_PITCH), :].astype(BF16) for s in range(slab)],
                                axis=1)
            half = f // 2
            for c0 in (0, half):
                gate = (jnp.dot(x, wbf[wslot, :, c0:c0 + half], preferred_element_type=F32)
                        + bias_ref[0, :, c0:c0 + half])
                up = (jnp.dot(x, wbf[wslot, :, f + c0:f + c0 + half], preferred_element_type=F32)
                      + bias_ref[0, :, f + c0:f + c0 + half])
                gate = jnp.minimum(gate, SWIGLU_LIMIT)
                up = jnp.clip(up, -SWIGLU_LIMIT, SWIGLU_LIMIT)
                o_ref[0:m, c0:c0 + half] = (gate * jax.nn.sigmoid(SWIGLU_ALPHA * gate) * (up + 1.0)).astype(BF16)
            if m < BM:
                o_ref[m:BM, :] = jnp.zeros((BM - m, f), BF16)

        _for_row_count(nv_ref[b], compute)
        finish()

    @pl.when(b >= nu)
    def _():
        o_ref[...] = jnp.zeros_like(o_ref)


def _moe_up(h2s, src_tok, wgu, bgu, sched):
    ne, d, f2 = wgu.shape
    f = f2 // 2
    p = src_tok.shape[0]
    nb = p // BM
    idx3 = src_tok.reshape(nb, 1, BM)
    smem_blk = lambda imap: pl.BlockSpec((1, 1, BM), imap, memory_space=pltpu.SMEM)
    grid_spec = pltpu.PrefetchScalarGridSpec(
        num_scalar_prefetch=7,
        grid=(nb,),
        in_specs=[smem_blk(lambda b, be, *_: (0, 0, 0)),
                  smem_blk(lambda b, be, *_: (jnp.minimum(b + 1, nb - 1), 0, 0)),
                  pl.BlockSpec(memory_space=pl.ANY),
                  pl.BlockSpec(memory_space=pl.ANY),
                  pl.BlockSpec((1, 1, f2), lambda b, be, *_: (be[b], 0, 0))],
        out_specs=pl.BlockSpec((BM, f), lambda b, be, *_: (b, 0)),
        scratch_shapes=[pltpu.VMEM((2, BM * SLAB_PITCH, LANE), F32),
                        pltpu.VMEM((2, d, f2), BF16),
                        pltpu.VMEM((2, _chunk_rows(f2), f2), F32),
                        pltpu.SemaphoreType.DMA((2,)), pltpu.SemaphoreType.DMA((2,))],
    )
    return pl.pallas_call(
        _moe_up_body,
        grid_spec=grid_spec,
        out_shape=jax.ShapeDtypeStruct((p, f), BF16),
        compiler_params=_cparams(("arbitrary",), 58),
        name="moe_up",
    )(sched["blk_e"], sched["n_used"], sched["wslot"], sched["wnext"], sched["wc0"], sched["wc1"],
      sched["nvalid"], idx3, idx3, h2s, wgu, bgu.reshape(ne, 1, f2))


def _moe_down_body(be_ref, nu_ref, ws_ref, wn_ref, wc0_ref, wc1_ref, nv_ref, dst_ref, a_ref, w_hbm, bias_ref,
                   ysc_ref, ybuf, wbf, stage, sem, wsem):
    b = pl.program_id(0)
    nb = pl.num_programs(0)
    nu = nu_ref[0]
    slab = SLAB_ROWS
    nch = wbf.shape[1] // stage.shape[1]

    @pl.when(b == 0)
    def _():
        prime0, finish0 = _weight_stream(w_hbm, wbf, stage, wsem, be_ref[0], ws_ref[0], 0, nch)
        prime0()
        finish0()

    def row_copy(slot, r, d):
        return pltpu.make_async_copy(ybuf.at[slot, pl.ds(pl.multiple_of(r * SLAB_PITCH, SUBLANE), slab), :],
                                     ysc_ref.at[pl.ds(pl.multiple_of(d * slab, slab), slab), :],
                                     sem.at[slot])

    def drain(step):
        slot = step % 2
        count = nv_ref[step]

        @pl.when(count == BM)
        def _():
            pltpu.make_async_copy(ybuf.at[slot, pl.ds(0, BM * slab), :], ysc_ref.at[pl.ds(0, BM * slab), :],
                                  sem.at[slot]).wait()

        @pl.when(count < BM)
        def _():
            def body(r, c):
                row_copy(slot, 0, 0).wait()
                return c
            lax.fori_loop(0, count, body, 0)

    @pl.when((b >= 2) & (b < nu))
    def _():
        drain(b - 2)

    @pl.when(b < nu)
    def _():
        slot = b % 2
        wslot = ws_ref[b]
        prime, finish = _weight_stream(w_hbm, wbf, stage, wsem, wn_ref[b], 1 - wslot, wc0_ref[b], wc1_ref[b])
        prime()
        def compute(m):
            y = jnp.dot(a_ref[0:m, :], wbf[wslot], preferred_element_type=F32) + bias_ref[0]
            for s in range(slab):
                ybuf[slot, pl.ds(s, m, stride=SLAB_PITCH), :] = y[:, s * LANE:(s + 1) * LANE]

        _for_row_count(nv_ref[b], compute)
        finish()

        def body(r, c):
            row_copy(slot, r, dst_ref[0, 0, r]).start()
            return c

        def pair(i, c):
            row_copy(slot, 2 * i, dst_ref[0, 0, 2 * i]).start(priority=0)
            row_copy(slot, 2 * i + 1, dst_ref[0, 0, 2 * i + 1]).start(priority=1)
            return c

        @pl.when(nv_ref[b] == BM)
        def _():
            lax.fori_loop(0, BM // 2, pair, 0, unroll=DMA_UNROLL // 2)

        @pl.when(nv_ref[b] < BM)
        def _():
            lax.fori_loop(0, nv_ref[b], body, 0)

    @pl.when(b == nb - 1)
    def _():
        @pl.when(nu >= 2)
        def _():
            drain(nu - 2)
        drain(nu - 1)


def _moe_down(act, dst_slot, n_out_rows, wd, bd, sched):
    p, f = act.shape
    ne, _, d = wd.shape
    nb = p // BM
    grid_spec = pltpu.PrefetchScalarGridSpec(
        num_scalar_prefetch=7,
        grid=(nb,),
        in_specs=[pl.BlockSpec((1, 1, BM), lambda b, be, *_: (b, 0, 0), memory_space=pltpu.SMEM),
                  pl.BlockSpec((BM, f), lambda b, be, *_: (b, 0)),
                  pl.BlockSpec(memory_space=pl.ANY),
                  pl.BlockSpec((1, 1, d), lambda b, be, *_: (be[b], 0, 0))],
        out_specs=pl.BlockSpec(memory_space=pl.ANY),
        scratch_shapes=[pltpu.VMEM((2, BM * SLAB_PITCH, LANE), F32),
                        pltpu.VMEM((2, f, d), BF16),
                        pltpu.VMEM((2, _chunk_rows(d), d), F32),
                        pltpu.SemaphoreType.DMA((2,)), pltpu.SemaphoreType.DMA((2,))],
    )
    return pl.pallas_call(
        _moe_down_body,
        grid_spec=grid_spec,
        out_shape=jax.ShapeDtypeStruct((n_out_rows * SLAB_ROWS, LANE), F32),
        compiler_params=pltpu.CompilerParams(dimension_semantics=("arbitrary",),
                                             vmem_limit_bytes=48 * 1024 * 1024, has_side_effects=True),
        name="moe_down",
    )(sched["blk_e"], sched["n_used"], sched["wslot"], sched["wnext"], sched["wc0"], sched["wc1"],
      sched["nvalid"], dst_slot.reshape(nb, 1, BM), act, wd, bd.reshape(ne, 1, d))


def _final_body(y0_ref, y1_ref, y2_ref, y3_ref, rt_ref, x1_ref, ada_ref, g_ref, o_ref):
    tm = x1_ref.shape[0]
    rt = rt_ref[...]
    y_refs = (y0_ref, y1_ref, y2_ref, y3_ref)
    pieces = []
    for s in range(SLAB_ROWS):
        acc = rt[:, TOP_K:TOP_K + 1] * y_refs[0][pl.ds(s, tm, stride=SLAB_ROWS), :]
        for k in range(1, TOP_K):
            acc = acc + rt[:, TOP_K + k:TOP_K + k + 1] * y_refs[k][pl.ds(s, tm, stride=SLAB_ROWS), :]
        pieces.append(acc)
    y = jnp.concatenate(pieces, axis=1)
    gt_f = ada_ref[0, 5:6, :]
    o_ref[...] = x1_ref[...] + gt_f * (_rms(y) * g_ref[...])


def _final(ysc, rt, x1, ada3, g_post, seq):
    t, d = x1.shape
    tm = 256
    per_b = seq // tm
    nt = t // tm
    assert TOP_K == 4
    slot_spec = lambda k: pl.BlockSpec((tm * SLAB_ROWS, LANE), lambda i: (k * nt + i, 0))
    return pl.pallas_call(
        _final_body,
        grid=(nt,),
        in_specs=[slot_spec(0), slot_spec(1), slot_spec(2), slot_spec(3),
                  pl.BlockSpec((tm, LANE), lambda i: (i, 0)),
                  pl.BlockSpec((tm, d), lambda i: (i, 0)),
                  pl.BlockSpec((1, 6, d), lambda i: (i // per_b, 0, 0)),
                  pl.BlockSpec((1, d), lambda i: (0, 0))],
        out_specs=pl.BlockSpec((tm, d), lambda i: (i, 0)),
        out_shape=jax.ShapeDtypeStruct((t, d), F32),
        compiler_params=_cparams(("arbitrary",), 48),
        name="final",
    )(ysc, ysc, ysc, ysc, rt, x1, ada3, g_post.reshape(1, d))


def _mixer_ffn_layer(x2, ada3, bsz, seq, g_mix_pre, g_mix_post, g_ffn_pre, g_ffn_post, w_in, lb, g_rec_out,
                     rel_bias, w_branch_rec, w_branch_att, w_o, w_router, b_router, w_gate_up, b_gate_up,
                     w_down, b_down):
    t, d = x2.shape
    d_rec = w_branch_rec.shape[0]
    w_att = w_branch_att.shape[0]
    d_att = 3 * w_att
    widths = dict(q_r=d_rec, i_r=d_rec, zf_f=d_rec, zf_b=d_rec, z_o=d_rec, q_a=d_att, k_a=d_att, v_a=d_att,
                  zg_rec=d, zg_att=d)
    my_order = ("zg_rec", "zg_att", "q_r", "i_r", "zf_f", "zf_b", "z_o", "q_a", "k_a", "v_a")
    col, acc = {}, 0
    for name in my_order:
        col[name] = acc // LANE
        acc += widths[name]
    rot = acc - 2 * d

    proj = _inproj(x2, g_mix_pre, ada3, w_in, seq, rot)

    oi, qtf, qtb, utf, utb, df, db = _hgrn_a(proj, lb, col, t, d_rec)
    rec_o = _hgrn_c(oi, qtf, qtb, utf, utb, df, db, proj, g_rec_out, col, bsz, seq, d_rec)

    nums, stats = [], []
    for g, (window, dil) in enumerate(DIL_GROUPS):
        hs = slice(g * ATT_HEADS_PER_GROUP, (g + 1) * ATT_HEADS_PER_GROUP)
        bias = _band_bias(rel_bias[:, hs], window, dil)
        num, st = _attn_group(proj, bias, col, g, dil, bsz, seq)
        nums.append(num)
        stats.append(st)

    x1, h2s, logits = _merge(rec_o, nums, stats, proj, x2, ada3, g_mix_post, g_ffn_pre,
                            w_branch_rec.astype(BF16), w_branch_att.astype(BF16), w_o.astype(BF16),
                            w_router, b_router, col, seq)

    rt, cnt = _route(logits)
    ne = logits.shape[1]
    counts = cnt[0].astype(jnp.int32)
    top_idx = rt[:, 0:TOP_K].astype(jnp.int32)
    rank = rt[:, 2 * TOP_K:3 * TOP_K].astype(jnp.int32)
    padded = (counts + BM - 1) // BM * BM
    pends = jnp.cumsum(padded)
    pstarts = pends - padded
    experts = jnp.arange(ne, dtype=jnp.int32)
    pstart_sel = jnp.sum(jnp.where(top_idx[..., None] == experts, pstarts, 0), axis=-1)
    dest = (pstart_sel + rank).T.reshape(-1)
    p_rows = t * TOP_K + ne * BM
    nb = p_rows // BM
    blk_start = jnp.arange(nb, dtype=jnp.int32) * BM
    blk_e = jnp.minimum(jnp.sum((pends[None, :] <= blk_start[:, None]).astype(jnp.int32), axis=1), ne - 1)
    n_used = (pends[-1:] // BM).astype(jnp.int32)

    n_assign = t * TOP_K
    valid_end = pstarts + counts
    pad_lo = jnp.concatenate([valid_end, pends[-1:]]).astype(jnp.int32)
    pad_hi = jnp.concatenate([pends, jnp.full((1,), p_rows, jnp.int32)]).astype(jnp.int32)
    slot_assign = _invert_slots(dest, pad_lo, pad_hi, p_rows)
    blk_end = jnp.sum(jnp.where(blk_e[:, None] == experts, valid_end, 0), axis=-1)
    nvalid = jnp.clip(blk_end - blk_start, 0, BM).astype(jnp.int32)
    nvalid = jnp.where(jnp.arange(nb) < n_used[0], nvalid, 0)

    lookup = lambda table: jnp.sum(jnp.where(blk_e[:, None] == experts, table, 0), axis=-1)
    nonempty = padded > 0
    order = jnp.cumsum(nonempty.astype(jnp.int32)) - 1
    later = lax.cummin(jnp.where(nonempty, experts, ne)[::-1])[::-1]
    next_e = jnp.concatenate([later[1:], jnp.full((1,), ne, jnp.int32)])
    blk_next = lookup(next_e)
    has_next = (blk_next < ne) & (jnp.arange(nb) < n_used[0])
    k_in_run = jnp.arange(nb, dtype=jnp.int32) - lookup(pstarts // BM)
    n_in_run = jnp.maximum(lookup(padded // BM), 1)
    common = dict(blk_e=blk_e, n_used=n_used, nvalid=nvalid, wslot=lookup(order) % 2,
                  wnext=jnp.where(has_next, blk_next, blk_e))

    def schedule(w):
        nch = w.shape[1] // _chunk_rows(w.shape[2])
        s = dict(common, wc0=jnp.where(has_next, k_in_run * nch // n_in_run, 0),
                 wc1=jnp.where(has_next, (k_in_run + 1) * nch // n_in_run, 0))
        return {k: v.astype(jnp.int32) for k, v in s.items()}

    src_tok = jnp.maximum(slot_assign, 0) % t
    act = _moe_up(h2s, src_tok, w_gate_up, b_gate_up, schedule(w_gate_up))
    ysc = _moe_down(act, slot_assign, n_assign, w_down, b_down, schedule(w_down))
    return _final(ysc, rt, x1, ada3, g_ffn_post, seq)


def kernel(x, c, w_ada, b_ada, g_mix_pre, g_mix_post, g_ffn_pre, g_ffn_post, w_in, g_rec_out, w_branch_rec,
           w_branch_att, w_o, w_router, b_router, w_gate_up, b_gate_up, w_down, b_down, rec_lb_table, rel_bias):
    bsz, seq, d = x.shape
    depth = w_in.shape[0]
    lb_all = jnp.cumsum(jax.nn.softmax(rec_lb_table.astype(F32), axis=1), axis=1)
    x2 = x.reshape(bsz * seq, d)
    for layer in range(depth):
        ada3 = _ada(c, w_ada[layer], b_ada[layer]).reshape(bsz, 6, d)
        x2 = _mixer_ffn_layer(x2, ada3, bsz, seq, g_mix_pre[layer], g_mix_post[layer], g_ffn_pre[layer],
                              g_ffn_post[layer], w_in[layer], lb_all[:, layer], g_rec_out[layer], rel_bias,
                              w_branch_rec[layer], w_branch_att[layer], w_o[layer], w_router[layer],
                              b_router[layer], w_gate_up[layer], b_gate_up[layer], w_down[layer],
                              b_down[layer])
    return x2.reshape(bsz, seq, d)
```

```python
import functools
import math

import numpy as np
import jax
import jax.numpy as jnp
from jax import lax
from jax.experimental import pallas as pl
from jax.experimental.pallas import tpu as pltpu

F32 = jnp.float32
BF16 = jnp.bfloat16

LANE = 128
SUBLANE = 8
SLAB_ROWS = 16
SLAB_PITCH = 24

REC_HEAD_DIM = 128
REC_CHUNK = 64
ATT_HEAD_DIM = 128
ATT_HEADS_PER_GROUP = 4
ATT_BLOCK = 64
DIL_GROUPS = ((128, 1), (512, 4), (2048, 16))
NUM_BUCKETS = 32
MAX_DISTANCE = 1024
N_EXPERTS = 32
TOP_K = 4
SWIGLU_LIMIT = 7.0
SWIGLU_ALPHA = 1.702
RMS_EPS = 1e-6
NEG_INF = -1e30

N_LEVELS = 6
W_CHUNK_BYTES = 4 * 1024 * 1024
CAST_ROWS = 64
ROW_PATHS = 4
HGRN_UNROLL = 4
DMA_UNROLL = 8
ATT_UNROLL = 8
BM = 256

_NT = (((1,), (1,)), ((), ()))
_TN = (((0,), (0,)), ((), ()))


def _cparams(sem, vmem_mb):
    return pltpu.CompilerParams(dimension_semantics=sem, vmem_limit_bytes=vmem_mb * 1024 * 1024)


def _rms(x):
    return x * lax.rsqrt(jnp.mean(x * x, axis=-1, keepdims=True) + RMS_EPS)


def _ada_body(c_ref, w_ref, b_ref, o_ref):
    c = c_ref[...]
    cond = (c * jax.nn.sigmoid(c)).astype(BF16)
    o_ref[...] = jnp.dot(cond, w_ref[...].astype(BF16), preferred_element_type=F32) + b_ref[...]


def _ada(c, w, b):
    bsz, d = c.shape
    n = w.shape[1]
    tn = 1024
    cp = jnp.zeros((SUBLANE, d), F32).at[:bsz].set(c)
    out = pl.pallas_call(
        _ada_body,
        grid=(n // tn,),
        in_specs=[pl.BlockSpec((SUBLANE, d), lambda j: (0, 0)),
                  pl.BlockSpec((d, tn), lambda j: (0, j)),
                  pl.BlockSpec((1, tn), lambda j: (0, j))],
        out_specs=pl.BlockSpec((SUBLANE, tn), lambda j: (0, j)),
        out_shape=jax.ShapeDtypeStruct((SUBLANE, n), F32),
        compiler_params=_cparams(("arbitrary",), 40),
        name="ada",
    )(cp, w, b.reshape(1, n))
    return out[:bsz]


def _inproj_body(x_ref, g_ref, ada_ref, w_ref, o_ref, h_ref):
    @pl.when(pl.program_id(1) == 0)
    def _():
        half = x_ref.shape[0] // 2
        sh = ada_ref[0, 0:1, :]
        sc = ada_ref[0, 1:2, :]
        for r0 in (0, half):
            y = _rms(x_ref[r0:r0 + half, :]) * g_ref[...]
            h_ref[r0:r0 + half, :] = (y * (1.0 + sc) + sh).astype(BF16)

    o_ref[...] = jnp.dot(h_ref[...], w_ref[...].astype(BF16), preferred_element_type=F32)


def _inproj(x2, g, ada3, w_in, seq, rot):
    t, d = x2.shape
    n = w_in.shape[1]
    tm, tn = 2048, 512
    per_b = seq // tm
    nj = n // tn
    assert rot % tn == 0 and n % tn == 0 and seq % tm == 0
    return pl.pallas_call(
        _inproj_body,
        grid=(t // tm, nj),
        in_specs=[pl.BlockSpec((tm, d), lambda i, j: (i, 0), pipeline_mode=pl.Buffered(1)),
                  pl.BlockSpec((1, d), lambda i, j: (0, 0)),
                  pl.BlockSpec((1, 6, d), lambda i, j: (i // per_b, 0, 0)),
                  pl.BlockSpec((d, tn), lambda i, j: (0, (j + rot // tn) % nj))],
        out_specs=pl.BlockSpec((tm, tn), lambda i, j: (i, j)),
        out_shape=jax.ShapeDtypeStruct((t, n), F32),
        scratch_shapes=[pltpu.VMEM((tm, d), BF16)],
        compiler_params=_cparams(("arbitrary", "arbitrary"), 56),
        name="inproj",
    )(x2, g.reshape(1, d), ada3, w_in)


def _hgrn_consts():
    c = REC_CHUNK
    r = np.arange(c)[:, None]
    m = np.arange(c)[None, :]
    nw = N_LEVELS - 1
    wf = np.zeros(((nw + 2) * c, c), np.float32)
    wb = np.zeros(((nw + 2) * c, c), np.float32)
    mf = np.zeros((N_LEVELS + 1, c, c), np.float32)
    for lvl in range(N_LEVELS):
        s = 32 >> lvl
        m0 = (r // (2 * s)) * (2 * s) + s
        up = r >= m0
        if lvl < nw:
            wf[lvl * c:(lvl + 1) * c] = np.where(up, (m >= m0) & (m <= r), (m > r) & (m <= m0 - 1))
            wb[lvl * c:(lvl + 1) * c] = np.where(up, (m >= m0) & (m <= r - 1), (m >= r) & (m <= m0 - 1))
        i = np.arange(c)[:, None]
        j = np.arange(c)[None, :]
        mf[lvl] = (i // (2 * s) == j // (2 * s)) & (i % (2 * s) >= s) & (j % (2 * s) < s)
    mf[N_LEVELS] = np.eye(c)
    wf[nw * c:(nw + 1) * c] = m <= r
    wf[(nw + 1) * c:(nw + 2) * c] = m > r
    wb[nw * c:(nw + 1) * c] = m >= r
    wb[(nw + 1) * c:(nw + 2) * c] = m < r
    mfb = mf + np.transpose(mf, (0, 2, 1))
    mfb[N_LEVELS] = np.eye(c)
    up = np.zeros((N_LEVELS, c, LANE), np.float32)
    for lvl in range(N_LEVELS):
        s = 32 >> lvl
        up[lvl] = ((np.arange(c) % (2 * s)) >= s)[:, None]
    wf3 = np.concatenate([wf, wf, wf], axis=1)
    wb3 = np.concatenate([wb, wb, wb], axis=1)
    return (jnp.asarray(wf3, BF16), jnp.asarray(wb3, BF16), jnp.asarray(mfb, F32), jnp.asarray(up, F32),
            jnp.asarray(1.0 - up, F32))


def _split3(g):
    hi = g.astype(BF16)
    r1 = g - hi.astype(F32)
    mid = r1.astype(BF16)
    lo = (r1 - mid.astype(F32)).astype(BF16)
    return jnp.concatenate([hi, mid, lo], axis=0)


def _hgrn_a_body(q_ref, i_ref, zf_ref, zb_ref, lb_ref, wf_ref, wb_ref, mf_ref, up_ref, lo_ref,
                 oi_ref, qtf_ref, qtb_ref, utf_ref, utb_ref, df_ref, db_ref, *, cpb):
    c = REC_CHUNK
    dirs = ((zf_ref, wf_ref, None, qtf_ref, utf_ref, df_ref, 0, c - 1),
            (zb_ref, wb_ref, None, qtb_ref, utb_ref, db_ref, 1, 0))

    def chunk_group(cg, carry):
        cis = [cg * HGRN_UNROLL + u for u in range(HGRN_UNROLL)]
        rows = [pl.ds(pl.multiple_of(ci * c, c), c) for ci in cis]
        zqs = [q_ref[rw, :] for rw in rows]
        qs = [zq * jax.nn.sigmoid(zq) for zq in zqs]
        vbs = [i_ref[rw, :].astype(BF16) for rw in rows]
        units = [(u, d) for u in range(HGRN_UNROLL) for d in range(2)]
        nw = N_LEVELS - 1
        ks, es, fs = {}, {}, {}
        for u, d in units:
            z_ref, w_ref = dirs[d][0], dirs[d][1]
            lb = lb_ref[d:d + 1, :]
            f = lb + (1.0 - lb) * jax.nn.sigmoid(z_ref[rows[u], :])
            fs[u, d] = f
            ks[u, d] = 1.0 - f
            es[u, d] = jnp.exp(jnp.dot(w_ref[...], _split3(jnp.log(f)), preferred_element_type=F32))
        acc = [jnp.zeros((c, c), F32) for _ in range(HGRN_UNROLL)]
        for lvl in range(N_LEVELS + 1):
            for u in range(HGRN_UNROLL):
                if lvl < N_LEVELS:
                    up, lo = up_ref[lvl], lo_ref[lvl]
                    if lvl < nw:
                        ef = es[u, 0][lvl * c:(lvl + 1) * c]
                        eb = es[u, 1][lvl * c:(lvl + 1) * c]
                    else:
                        ef = fs[u, 0] * up + lo
                        eb = fs[u, 1] * lo + up
                    qa = jnp.concatenate([(qs[u] * (ef * up)).astype(BF16), (qs[u] * (eb * lo)).astype(BF16)],
                                         axis=1)
                    ka = jnp.concatenate([(ks[u, 0] * (ef * lo)).astype(BF16),
                                          (ks[u, 1] * (eb * up)).astype(BF16)], axis=1)
                else:
                    qa, ka = qs[u].astype(BF16), (ks[u, 0] + ks[u, 1]).astype(BF16)
                p = lax.dot_general(qa, ka, _NT, preferred_element_type=F32)
                acc[u] = acc[u] + p * mf_ref[lvl]
        for u, d in units:
            _, _, _, qt_ref, ut_ref, d_ref, _, drow = dirs[d]
            e = es[u, d]
            qt_ref[rows[u], :] = (qs[u] * e[nw * c:(nw + 1) * c]).astype(BF16)
            kt = (ks[u, d] * e[(nw + 1) * c:(nw + 2) * c]).astype(BF16)
            ut_ref[cis[u]] = lax.dot_general(vbs[u], kt, _TN, preferred_element_type=F32)
            d_ref[pl.ds(cis[u], 1), :] = e[nw * c + drow:nw * c + drow + 1]
        for u in range(HGRN_UNROLL):
            oi_ref[rows[u], :] = jnp.dot(acc[u].astype(BF16), vbs[u], preferred_element_type=F32)
        return carry

    lax.fori_loop(0, cpb // HGRN_UNROLL, chunk_group, 0)


def _hgrn_a(proj, lb, col, t, d_rec):
    heads = d_rec // REC_HEAD_DIM
    tq = 2048
    cpb = tq // REC_CHUNK
    nchunks = t // REC_CHUNK
    wf, wb, mf, up, lo = _hgrn_consts()
    hd = REC_HEAD_DIM

    def colspec(off):
        return pl.BlockSpec((tq, hd), lambda i, h: (i, off + h))

    full2 = lambda i, h: (0, 0)
    full3 = lambda i, h: (0, 0, 0)
    row_spec = pl.BlockSpec((tq, hd), lambda i, h: (i, h))
    u_spec = pl.BlockSpec((cpb, hd, hd), lambda i, h: (i, 0, h))
    d_spec = pl.BlockSpec((cpb, hd), lambda i, h: (i, h))
    return pl.pallas_call(
        functools.partial(_hgrn_a_body, cpb=cpb),
        grid=(t // tq, heads),
        in_specs=[colspec(col["q_r"]), colspec(col["i_r"]), colspec(col["zf_f"]), colspec(col["zf_b"]),
                  pl.BlockSpec((2, hd), lambda i, h: (0, h)),
                  pl.BlockSpec(wf.shape, full2), pl.BlockSpec(wb.shape, full2),
                  pl.BlockSpec(mf.shape, full3), pl.BlockSpec(up.shape, full3), pl.BlockSpec(lo.shape, full3)],
        out_specs=[row_spec, row_spec, row_spec, u_spec, u_spec, d_spec, d_spec],
        out_shape=[jax.ShapeDtypeStruct((t, d_rec), F32),
                   jax.ShapeDtypeStruct((t, d_rec), BF16),
                   jax.ShapeDtypeStruct((t, d_rec), BF16),
                   jax.ShapeDtypeStruct((nchunks, hd, d_rec), F32),
                   jax.ShapeDtypeStruct((nchunks, hd, d_rec), F32),
                   jax.ShapeDtypeStruct((nchunks, d_rec), F32),
                   jax.ShapeDtypeStruct((nchunks, d_rec), F32)],
        compiler_params=_cparams(("arbitrary", "arbitrary"), 32),
        name="hgrn_a",
    )(proj, proj, proj, proj, lb, wf, wb, mf, up, lo)


def _hgrn_c_body(oi_ref, qtf_ref, qtb_ref, utf_ref, utb_ref, df_ref, db_ref, z_ref, g_ref,
                 out_ref, acc_ref, accb_ref, *, nchunks):
    c = REC_CHUNK
    hd = REC_HEAD_DIM

    unroll = 4

    def step(i, carry):
        st_f, st_b = carry
        pending = []
        for u in range(unroll):
            nf = i * unroll + u
            nb = nchunks - 1 - nf
            rows_f = pl.ds(pl.multiple_of(nf * c, c), c)
            rows_b = pl.ds(pl.multiple_of(nb * c, c), c)
            of = lax.dot_general(qtf_ref[rows_f, :], st_f.astype(BF16), _NT, preferred_element_type=F32)
            ob = lax.dot_general(qtb_ref[rows_b, :], st_b.astype(BF16), _NT, preferred_element_type=F32)
            pending.append((rows_f, rows_b, oi_ref[rows_f, :] + of, ob))
            st_f = df_ref[pl.ds(nf, 1), :] * st_f + utf_ref[nf]
            st_b = db_ref[pl.ds(nb, 1), :] * st_b + utb_ref[nb]
        for rows_f, rows_b, vf, vb in pending:
            acc_ref[rows_f, :] = vf
            accb_ref[rows_b, :] = vb
        return st_f, st_b

    zero = jnp.zeros((hd, hd), F32)
    lax.fori_loop(0, nchunks // unroll, step, (zero, zero))

    o = _rms(acc_ref[...] + accb_ref[...])
    out_ref[...] = (o * g_ref[...] * jax.nn.sigmoid(z_ref[...])).astype(BF16)


def _hgrn_c(oi, qtf, qtb, utf, utb, df, db, proj, g_out, col, bsz, seq, d_rec):
    heads = d_rec // REC_HEAD_DIM
    hd = REC_HEAD_DIM
    nchunks = seq // REC_CHUNK
    row_spec = pl.BlockSpec((seq, hd), lambda b, h: (b, h))
    u_spec = pl.BlockSpec((nchunks, hd, hd), lambda b, h: (b, 0, h))
    d_spec = pl.BlockSpec((nchunks, hd), lambda b, h: (b, h))
    zo = col["z_o"]
    return pl.pallas_call(
        functools.partial(_hgrn_c_body, nchunks=nchunks),
        grid=(bsz, heads),
        in_specs=[row_spec, row_spec, row_spec, u_spec, u_spec, d_spec, d_spec,
                  pl.BlockSpec((seq, hd), lambda b, h: (b, zo + h)),
                  pl.BlockSpec((1, hd), lambda b, h: (0, h))],
        out_specs=row_spec,
        out_shape=jax.ShapeDtypeStruct((bsz * seq, d_rec), BF16),
        scratch_shapes=[pltpu.VMEM((seq, hd), F32), pltpu.VMEM((seq, hd), F32)],
        compiler_params=_cparams(("arbitrary", "arbitrary"), 48),
        name="hgrn_c",
    )(oi, qtf, qtb, utf, utb, df, db, proj, g_out.reshape(1, d_rec))


def _t5_bucket(rel):
    half_buckets = NUM_BUCKETS // 2
    ret = np.where(rel > 0, half_buckets, 0)
    n = np.abs(rel)
    max_exact = half_buckets // 2
    nf = np.maximum(n, 1).astype(np.float32)
    large = max_exact + (np.log(nf / np.float32(max_exact)) / np.float32(math.log(MAX_DISTANCE / max_exact))
                         * np.float32(half_buckets - max_exact)).astype(np.int32)
    large = np.minimum(large, half_buckets - 1)
    return ret + np.where(n < max_exact, n, large)


def _band_bias(rel_bias_g, window, dil):
    half = window // (2 * dil)
    q_off = np.arange(ATT_BLOCK)[:, None]
    rel = np.arange(3 * ATT_BLOCK)[None, :] - ATT_BLOCK - q_off
    onehot = (_t5_bucket(rel * dil)[..., None] == np.arange(NUM_BUCKETS)).astype(np.float32)
    bias = jnp.einsum("qkb,bh->hqk", jnp.asarray(onehot), rel_bias_g.astype(F32),
                      precision=lax.Precision.HIGHEST)
    return jnp.where(jnp.asarray(np.abs(rel) <= half)[None], bias, NEG_INF)


def _attn_body(q_ref, kp_ref, k_ref, kn_ref, vp_ref, v_ref, vn_ref, bias_ref,
               num_ref, st_ref, kc_ref, vc_ref, *, dil, tq, sub_len):
    blk = ATT_BLOCK
    nqb = tq // blk
    n = pl.program_id(1)
    scale = ATT_HEAD_DIM ** -0.5

    def sds(start, size):
        if dil == 1:
            return pl.ds(start, size)
        return pl.ds(start, size, stride=dil)

    cu = kc_ref.shape[0]
    qu = ATT_UNROLL // cu

    def deinterleave(r, j):
        kc_ref[j, 0:blk, :] = kp_ref[sds(r, blk), :].astype(BF16)
        kc_ref[j, blk:blk + tq, :] = k_ref[sds(r, tq), :].astype(BF16)
        kc_ref[j, blk + tq:2 * blk + tq, :] = kn_ref[sds(r, blk), :].astype(BF16)
        vc_ref[j, 0:blk, :] = vp_ref[sds(r, blk), :].astype(BF16)
        vc_ref[j, blk:blk + tq, :] = v_ref[sds(r, tq), :].astype(BF16)
        vc_ref[j, blk + tq:2 * blk + tq, :] = vn_ref[sds(r, blk), :].astype(BF16)

    def units(r0, qb0):
        us = [(j, u) for j in range(cu) for u in range(qu)]
        q0s = [pl.multiple_of((qb0 + u) * blk, blk) for _, u in us]
        rows = [sds(r0 + j + dil * q0, blk) for (j, _), q0 in zip(us, q0s)]
        lane = lax.broadcasted_iota(jnp.int32, (blk, LANE), 1)
        key_iota = lax.broadcasted_iota(jnp.int32, (1, 3 * blk), 1)
        bias = bias_ref[0]
        qs = [q_ref[rw, :].astype(BF16) for rw in rows]
        kws = [kc_ref[j, pl.ds(q0, 3 * blk), :] for (j, _), q0 in zip(us, q0s)]
        vws = [vc_ref[j, pl.ds(q0, 3 * blk), :] for (j, _), q0 in zip(us, q0s)]
        ss = [lax.dot_general(q, kw, _NT, preferred_element_type=F32) * scale for q, kw in zip(qs, kws)]
        valids = []
        for q0 in q0s:
            kpos = n * tq + q0 - blk + key_iota
            valids.append((kpos >= 0) & (kpos < sub_len))
        ss = [jnp.where(valid, s + bias, NEG_INF) for s, valid in zip(ss, valids)]
        ms = [jnp.max(s, axis=-1, keepdims=True) for s in ss]
        ps = [jnp.exp(s - m) for s, m in zip(ss, ms)]
        ls = [jnp.sum(p, axis=-1, keepdims=True) for p in ps]
        nums = [jnp.dot(p.astype(BF16), vw, preferred_element_type=F32) for p, vw in zip(ps, vws)]
        for rw, num, m, l in zip(rows, nums, ms, ls):
            num_ref[rw, :] = num
            st_ref[rw, :] = jnp.where(lane < LANE // 2, m, l)

    def class_group(rg, carry):
        r0 = rg * cu
        for j in range(cu):
            deinterleave(r0 + j, j)

        def qgroup(qg, carry2):
            units(r0, qg * qu)
            return carry2

        lax.fori_loop(0, nqb // qu, qgroup, 0)
        return carry

    lax.fori_loop(0, dil // cu, class_group, 0)


def _attn_group(proj, bias, col, g, dil, bsz, seq):
    tile = 2048
    tq = tile // dil
    halo = ATT_BLOCK * dil
    sub_len = seq // dil
    cu = ATT_UNROLL // min(tq // ATT_BLOCK, ATT_UNROLL)
    nh = ATT_HEADS_PER_GROUP
    hd = ATT_HEAD_DIM
    qc = col["q_a"] + g * nh
    kc = col["k_a"] + g * nh
    vc = col["v_a"] + g * nh
    tiles_b = seq // tile
    halos_b = seq // halo
    hpt = tile // halo

    own = lambda c: pl.BlockSpec((tile, hd), lambda b, n, h: (b * tiles_b + n, c + h))
    prev = lambda c: pl.BlockSpec(
        (halo, hd), lambda b, n, h: (b * halos_b + jnp.maximum(n * hpt - 1, 0), c + h))
    nxt = lambda c: pl.BlockSpec(
        (halo, hd), lambda b, n, h: (b * halos_b + jnp.minimum((n + 1) * hpt, halos_b - 1), c + h))
    t = bsz * seq
    return pl.pallas_call(
        functools.partial(_attn_body, dil=dil, tq=tq, sub_len=sub_len),
        grid=(bsz, tiles_b, nh),
        in_specs=[own(qc), prev(kc), own(kc), nxt(kc), prev(vc), own(vc), nxt(vc),
                  pl.BlockSpec((1,) + bias.shape[1:], lambda b, n, h: (h, 0, 0))],
        out_specs=[pl.BlockSpec((tile, hd), lambda b, n, h: (b * tiles_b + n, h)),
                   pl.BlockSpec((tile, LANE), lambda b, n, h: (b * tiles_b + n, h))],
        out_shape=[jax.ShapeDtypeStruct((t, nh * hd), F32), jax.ShapeDtypeStruct((t, nh * LANE), F32)],
        scratch_shapes=[pltpu.VMEM((cu, tq + 2 * ATT_BLOCK, hd), BF16),
                        pltpu.VMEM((cu, tq + 2 * ATT_BLOCK, hd), BF16)],
        compiler_params=_cparams(("arbitrary", "arbitrary", "arbitrary"), 32),
        name=f"attn_d{dil}",
    )(proj, proj, proj, proj, proj, proj, proj, bias)


def _merge_body(rec_ref, n0_ref, n1_ref, n2_ref, s0_ref, s1_ref, s2_ref, zgr_ref, zga_ref, x_ref,
                ada_ref, gpost_ref, gpre_ref, wbr_ref, wba_ref, wo_ref, wr_ref, br_ref,
                x1_ref, h2_ref, lg_ref):
    nh = ATT_HEADS_PER_GROUP
    hd = ATT_HEAD_DIM
    half = LANE // 2
    lane = lax.broadcasted_iota(jnp.int32, (rec_ref.shape[0], LANE), 1)
    heads = []
    for h in range(nh):
        cols = slice(h * hd, (h + 1) * hd)
        st = [s[:, cols] for s in (s0_ref, s1_ref, s2_ref)]
        top = jnp.maximum(jnp.maximum(st[0], st[1]), st[2])
        ws = [jnp.exp(s - top) for s in st]
        den = (ws[0] * pltpu.roll(st[0], half, 1) + ws[1] * pltpu.roll(st[1], half, 1)
               + ws[2] * pltpu.roll(st[2], half, 1))
        coef = [w / den for w in ws]
        coef = [jnp.where(lane < half, c, pltpu.roll(c, half, 1)) for c in coef]
        num = coef[0] * n0_ref[:, cols] + coef[1] * n1_ref[:, cols] + coef[2] * n2_ref[:, cols]
        heads.append(num.astype(BF16))
    att = jnp.concatenate(heads, axis=1)
    y_rec = jnp.dot(rec_ref[...], wbr_ref[...], preferred_element_type=F32)
    y_att = jnp.dot(att, wba_ref[...], preferred_element_type=F32)
    merged = jax.nn.sigmoid(zgr_ref[...]) * y_rec + jax.nn.sigmoid(zga_ref[...]) * y_att
    y = jnp.dot(merged.astype(BF16), wo_ref[...], preferred_element_type=F32)
    gt_m = ada_ref[0, 2:3, :]
    sh_f = ada_ref[0, 3:4, :]
    sc_f = ada_ref[0, 4:5, :]
    x1 = x_ref[...] + gt_m * (_rms(y) * gpost_ref[...])
    x1_ref[...] = x1
    h2 = _rms(x1) * gpre_ref[...] * (1.0 + sc_f) + sh_f
    tm = h2.shape[0]
    for s in range(SLAB_ROWS):
        h2_ref[pl.ds(s, tm, stride=SLAB_ROWS), :] = h2[:, s * LANE:(s + 1) * LANE]
    ne = lg_ref.shape[1]
    h_hi = h2.astype(BF16)
    h_lo = (h2 - h_hi.astype(F32)).astype(BF16)
    both = jnp.dot(h_hi, wr_ref[...], preferred_element_type=F32)
    cross = jnp.dot(h_lo, wr_ref[:, 0:ne], preferred_element_type=F32)
    lg_ref[...] = both[:, 0:ne] + both[:, ne:2 * ne] + cross + br_ref[...]


def _merge(rec_o, nums, stats, proj, x2, ada3, g_post, g_pre, wbr, wba, wo, w_router, b_router, col, seq):
    t, d = x2.shape
    tm = 256
    per_b = seq // tm
    d_rec = rec_o.shape[1]
    w_att = nums[0].shape[1]
    ne = w_router.shape[1]
    wr_hi = w_router.astype(BF16)
    wr_lo = (w_router - wr_hi.astype(F32)).astype(BF16)
    w_router = jnp.concatenate([wr_hi, wr_lo], axis=1)
    dl = d // LANE
    row = lambda w: pl.BlockSpec((tm, w), lambda i: (i, 0))
    const = lambda shape: pl.BlockSpec(shape, lambda i: (0,) * len(shape), pipeline_mode=pl.Buffered(1))
    zgr = col["zg_rec"] // dl
    zga = col["zg_att"] // dl
    return pl.pallas_call(
        _merge_body,
        grid=(t // tm,),
        in_specs=[row(d_rec), row(w_att), row(w_att), row(w_att), row(w_att), row(w_att), row(w_att),
                  pl.BlockSpec((tm, d), lambda i: (i, zgr)),
                  pl.BlockSpec((tm, d), lambda i: (i, zga)),
                  row(d),
                  pl.BlockSpec((1, 6, d), lambda i: (i // per_b, 0, 0)),
                  const((1, d)), const((1, d)),
                  const(wbr.shape), const(wba.shape), const(wo.shape), const(w_router.shape),
                  const((1, ne))],
        out_specs=[row(d), pl.BlockSpec((tm * SLAB_ROWS, LANE), lambda i: (i, 0)),
                   pl.BlockSpec((tm, ne), lambda i: (i, 0))],
        out_shape=[jax.ShapeDtypeStruct((t, d), F32), jax.ShapeDtypeStruct((t * SLAB_ROWS, LANE), F32),
                   jax.ShapeDtypeStruct((t, ne), F32)],
        compiler_params=_cparams(("arbitrary",), 56),
        name="merge",
    )(rec_o, nums[0], nums[1], nums[2], stats[0], stats[1], stats[2], proj, proj, x2, ada3,
      g_post.reshape(1, d), g_pre.reshape(1, d), wbr, wba, wo, w_router, b_router.reshape(1, ne))


def _route_body(lg_ref, tri_ref, rt_ref, cnt_ref, carry_ref):
    i = pl.program_id(0)
    tr, ne = lg_ref.shape

    @pl.when(i == 0)
    def _():
        carry_ref[...] = jnp.zeros_like(carry_ref)

    l = lg_ref[...]
    lane = lax.broadcasted_iota(jnp.int32, (tr, ne), 1).astype(F32)
    vals, sels, idxs = [], [], []
    for _ in range(TOP_K):
        m = jnp.max(l, axis=-1, keepdims=True)
        idx = jnp.min(jnp.where(l == m, lane, float(ne)), axis=-1, keepdims=True)
        sel = lane == idx
        vals.append(m)
        idxs.append(idx)
        sels.append(sel)
        l = jnp.where(sel, -jnp.inf, l)
    es = [jnp.exp(v - vals[0]) for v in vals]
    tot = es[0] + es[1] + es[2] + es[3]
    chosen = (sels[0] | sels[1] | sels[2] | sels[3]).astype(F32)
    prefix = jnp.dot(tri_ref[...], chosen.astype(BF16), preferred_element_type=F32) + carry_ref[0:1, :]
    out_lane = lax.broadcasted_iota(jnp.int32, (tr, LANE), 1)
    rt = jnp.zeros((tr, LANE), F32)
    for k in range(TOP_K):
        rank = jnp.sum(jnp.where(sels[k], prefix, 0.0), axis=-1, keepdims=True)
        rt = jnp.where(out_lane == k, idxs[k], rt)
        rt = jnp.where(out_lane == TOP_K + k, es[k] / tot, rt)
        rt = jnp.where(out_lane == 2 * TOP_K + k, rank, rt)
    rt_ref[...] = rt
    new = carry_ref[0:1, :] + jnp.sum(chosen, axis=0, keepdims=True)
    carry_ref[...] = jnp.broadcast_to(new, carry_ref.shape)
    cnt_ref[...] = carry_ref[...]


def _route(logits):
    t, ne = logits.shape
    tr = 512
    tri = jnp.asarray(np.tril(np.ones((tr, tr), np.float32), -1), BF16)
    return pl.pallas_call(
        _route_body,
        grid=(t // tr,),
        in_specs=[pl.BlockSpec((tr, ne), lambda i: (i, 0)),
                  pl.BlockSpec((tr, tr), lambda i: (0, 0))],
        out_specs=[pl.BlockSpec((tr, LANE), lambda i: (i, 0)),
                   pl.BlockSpec((SUBLANE, ne), lambda i: (0, 0))],
        out_shape=[jax.ShapeDtypeStruct((t, LANE), F32), jax.ShapeDtypeStruct((SUBLANE, ne), F32)],
        scratch_shapes=[pltpu.VMEM((SUBLANE, ne), F32)],
        compiler_params=_cparams(("arbitrary",), 32),
        name="route",
    )(logits, tri)


def _invert_body(pad_lo_ref, pad_hi_ref, dest_ref, out_ref, *, chunk):
    base = pl.program_id(0) * chunk

    @pl.when(pl.program_id(0) == 0)
    def _():
        def one_range(e, c):
            def fill(p, c2):
                out_ref[p] = -1
                return c2
            lax.fori_loop(pad_lo_ref[e], pad_hi_ref[e], fill, 0)
            return c
        lax.fori_loop(0, pad_lo_ref.shape[0], one_range, 0)

    def body(j, c):
        out_ref[dest_ref[0, 0, j]] = base + j
        return c
    lax.fori_loop(0, chunk, body, 0, unroll=DMA_UNROLL)


def _invert_slots(dest, pad_lo, pad_hi, p_rows):
    n = dest.shape[0]
    chunk = 8192
    nch = n // chunk
    grid_spec = pltpu.PrefetchScalarGridSpec(
        num_scalar_prefetch=2,
        grid=(nch,),
        in_specs=[pl.BlockSpec((1, 1, chunk), lambda i, lo, hi: (i, 0, 0), memory_space=pltpu.SMEM)],
        out_specs=pl.BlockSpec(memory_space=pltpu.SMEM),
    )
    return pl.pallas_call(
        functools.partial(_invert_body, chunk=chunk),
        grid_spec=grid_spec,
        out_shape=jax.ShapeDtypeStruct((p_rows,), jnp.int32),
        compiler_params=pltpu.CompilerParams(dimension_semantics=("arbitrary",)),
        name="invert_slots",
    )(pad_lo, pad_hi, dest.reshape(nch, 1, chunk))


def _for_row_count(nvalid, compute):
    q = BM // ROW_PATHS
    for i in range(1, ROW_PATHS + 1):
        lo, hi = (i - 1) * q, i * q
        cond = (nvalid > lo) & (nvalid <= hi) if i > 1 else (nvalid <= hi)

        @pl.when(cond)
        def _(m=hi):
            compute(m)


def _chunk_rows(ncols):
    return W_CHUNK_BYTES // (4 * ncols)


def _weight_stream(w_hbm, wbf, stage, wsem, e, slot, c0, c1, priority=0):
    kc = stage.shape[1]

    def copy(c):
        return pltpu.make_async_copy(w_hbm.at[e, pl.ds(pl.multiple_of(c * kc, kc), kc), :],
                                     stage.at[c % 2], wsem.at[c % 2])

    def prime():
        def body(c, carry):
            copy(c).start(priority=priority)
            return carry
        lax.fori_loop(c0, jnp.minimum(c0 + 2, c1), body, 0)

    def finish():
        def body(c, carry):
            copy(c).wait()
            buf = c % 2

            def cast(i, carry2):
                r = pl.multiple_of(i * CAST_ROWS, CAST_ROWS)
                wbf[slot, pl.ds(pl.multiple_of(c * kc, kc) + r, CAST_ROWS), :] = (
                    stage[buf, pl.ds(r, CAST_ROWS), :].astype(BF16))
                return carry2
            lax.fori_loop(0, kc // CAST_ROWS, cast, 0)

            @pl.when(c + 2 < c1)
            def _():
                copy(c + 2).start(priority=priority)
            return carry
        lax.fori_loop(c0, c1, body, 0)

    return prime, finish


def _moe_up_body(be_ref, nu_ref, ws_ref, wn_ref, wc0_ref, wc1_ref, nv_ref, idx0_ref, idxn_ref, h2s_ref, w_hbm,
                 bias_ref, o_ref, xbuf, wbf, stage, sem, wsem):
    b = pl.program_id(0)
    nu = nu_ref[0]
    f = o_ref.shape[1]
    slab = SLAB_ROWS
    nch = wbf.shape[1] // stage.shape[1]

    def row_start(idx_ref, slot, r):
        tok = idx_ref[0, 0, r]
        pltpu.make_async_copy(h2s_ref.at[pl.ds(pl.multiple_of(tok * slab, slab), slab), :],
                              xbuf.at[slot, pl.ds(pl.multiple_of(r * SLAB_PITCH, SUBLANE), slab), :],
                              sem.at[slot]).start()

    def wait_rows(slot):
        pltpu.make_async_copy(h2s_ref.at[pl.ds(0, BM * slab), :], xbuf.at[slot, pl.ds(0, BM * slab), :],
                              sem.at[slot]).wait()

    def issue(idx_ref, slot):
        def body(r, c):
            row_start(idx_ref, slot, r)
            return c
        lax.fori_loop(0, BM, body, 0, unroll=DMA_UNROLL)

    @pl.when(b == 0)
    def _():
        issue(idx0_ref, 0)
        prime0, finish0 = _weight_stream(w_hbm, wbf, stage, wsem, be_ref[0], ws_ref[0], 0, nch)
        prime0()
        finish0()

    @pl.when(b + 1 < nu)
    def _():
        issue(idxn_ref, (b + 1) % 2)

    @pl.when(b < nu)
    def _():
        wslot = ws_ref[b]
        prime, finish = _weight_stream(w_hbm, wbf, stage, wsem, wn_ref[b], 1 - wslot, wc0_ref[b], wc1_ref[b],
                                       priority=1)
        prime()
        slot = b % 2
        wait_rows(slot)

        def compute(m):
            x = jnp.concatenate([xbuf[slot, pl.ds(s, m, stride=SLAB_PITCH), :].astype(BF16) for s in range(slab)],
                                axis=1)
            half = f // 2
            for c0 in (0, half):
                gate = (jnp.dot(x, wbf[wslot, :, c0:c0 + half], preferred_element_type=F32)
                        + bias_ref[0, :, c0:c0 + half])
                up = (jnp.dot(x, wbf[wslot, :, f + c0:f + c0 + half], preferred_element_type=F32)
                      + bias_ref[0, :, f + c0:f + c0 + half])
                gate = jnp.minimum(gate, SWIGLU_LIMIT)
                up = jnp.clip(up, -SWIGLU_LIMIT, SWIGLU_LIMIT)
                o_ref[0:m, c0:c0 + half] = (gate * jax.nn.sigmoid(SWIGLU_ALPHA * gate) * (up + 1.0)).astype(BF16)
            if m < BM:
                o_ref[m:BM, :] = jnp.zeros((BM - m, f), BF16)

        _for_row_count(nv_ref[b], compute)
        finish()

    @pl.when(b >= nu)
    def _():
        o_ref[...] = jnp.zeros_like(o_ref)


def _moe_up(h2s, src_tok, wgu, bgu, sched):
    ne, d, f2 = wgu.shape
    f = f2 // 2
    p = src_tok.shape[0]
    nb = p // BM
    idx3 = src_tok.reshape(nb, 1, BM)
    smem_blk = lambda imap: pl.BlockSpec((1, 1, BM), imap, memory_space=pltpu.SMEM)
    grid_spec = pltpu.PrefetchScalarGridSpec(
        num_scalar_prefetch=7,
        grid=(nb,),
        in_specs=[smem_blk(lambda b, be, *_: (0, 0, 0)),
                  smem_blk(lambda b, be, *_: (jnp.minimum(b + 1, nb - 1), 0, 0)),
                  pl.BlockSpec(memory_space=pl.ANY),
                  pl.BlockSpec(memory_space=pl.ANY),
                  pl.BlockSpec((1, 1, f2), lambda b, be, *_: (be[b], 0, 0))],
        out_specs=pl.BlockSpec((BM, f), lambda b, be, *_: (b, 0)),
        scratch_shapes=[pltpu.VMEM((2, BM * SLAB_PITCH, LANE), F32),
                        pltpu.VMEM((2, d, f2), BF16),
                        pltpu.VMEM((2, _chunk_rows(f2), f2), F32),
                        pltpu.SemaphoreType.DMA((2,)), pltpu.SemaphoreType.DMA((2,))],
    )
    return pl.pallas_call(
        _moe_up_body,
        grid_spec=grid_spec,
        out_shape=jax.ShapeDtypeStruct((p, f), BF16),
        compiler_params=_cparams(("arbitrary",), 58),
        name="moe_up",
    )(sched["blk_e"], sched["n_used"], sched["wslot"], sched["wnext"], sched["wc0"], sched["wc1"],
      sched["nvalid"], idx3, idx3, h2s, wgu, bgu.reshape(ne, 1, f2))


def _moe_down_body(be_ref, nu_ref, ws_ref, wn_ref, wc0_ref, wc1_ref, nv_ref, dst_ref, a_ref, w_hbm, bias_ref,
                   ysc_ref, ybuf, wbf, stage, sem, wsem):
    b = pl.program_id(0)
    nb = pl.num_programs(0)
    nu = nu_ref[0]
    slab = SLAB_ROWS
    nch = wbf.shape[1] // stage.shape[1]

    @pl.when(b == 0)
    def _():
        prime0, finish0 = _weight_stream(w_hbm, wbf, stage, wsem, be_ref[0], ws_ref[0], 0, nch)
        prime0()
        finish0()

    def row_copy(slot, r, d):
        return pltpu.make_async_copy(ybuf.at[slot, pl.ds(pl.multiple_of(r * SLAB_PITCH, SUBLANE), slab), :],
                                     ysc_ref.at[pl.ds(pl.multiple_of(d * slab, slab), slab), :],
                                     sem.at[slot])

    def drain(step):
        slot = step % 2
        count = nv_ref[step]

        @pl.when(count == BM)
        def _():
            pltpu.make_async_copy(ybuf.at[slot, pl.ds(0, BM * slab), :], ysc_ref.at[pl.ds(0, BM * slab), :],
                                  sem.at[slot]).wait()

        @pl.when(count < BM)
        def _():
            def body(r, c):
                row_copy(slot, 0, 0).wait()
                return c
            lax.fori_loop(0, count, body, 0)

    @pl.when((b >= 2) & (b < nu))
    def _():
        drain(b - 2)

    @pl.when(b < nu)
    def _():
        slot = b % 2
        wslot = ws_ref[b]
        prime, finish = _weight_stream(w_hbm, wbf, stage, wsem, wn_ref[b], 1 - wslot, wc0_ref[b], wc1_ref[b])
        prime()
        def compute(m):
            y = jnp.dot(a_ref[0:m, :], wbf[wslot], preferred_element_type=F32) + bias_ref[0]
            for s in range(slab):
                ybuf[slot, pl.ds(s, m, stride=SLAB_PITCH), :] = y[:, s * LANE:(s + 1) * LANE]

        _for_row_count(nv_ref[b], compute)
        finish()

        def body(r, c):
            row_copy(slot, r, dst_ref[0, 0, r]).start()
            return c

        def pair(i, c):
            row_copy(slot, 2 * i, dst_ref[0, 0, 2 * i]).start(priority=0)
            row_copy(slot, 2 * i + 1, dst_ref[0, 0, 2 * i + 1]).start(priority=1)
            return c

        @pl.when(nv_ref[b] == BM)
        def _():
            lax.fori_loop(0, BM // 2, pair, 0, unroll=DMA_UNROLL // 2)

        @pl.when(nv_ref[b] < BM)
        def _():
            lax.fori_loop(0, nv_ref[b], body, 0)

    @pl.when(b == nb - 1)
    def _():
        @pl.when(nu >= 2)
        def _():
            drain(nu - 2)
        drain(nu - 1)


def _moe_down(act, dst_slot, n_out_rows, wd, bd, sched):
    p, f = act.shape
    ne, _, d = wd.shape
    nb = p // BM
    grid_spec = pltpu.PrefetchScalarGridSpec(
        num_scalar_prefetch=7,
        grid=(nb,),
        in_specs=[pl.BlockSpec((1, 1, BM), lambda b, be, *_: (b, 0, 0), memory_space=pltpu.SMEM),
                  pl.BlockSpec((BM, f), lambda b, be, *_: (b, 0)),
                  pl.BlockSpec(memory_space=pl.ANY),
                  pl.BlockSpec((1, 1, d), lambda b, be, *_: (be[b], 0, 0))],
        out_specs=pl.BlockSpec(memory_space=pl.ANY),
        scratch_shapes=[pltpu.VMEM((2, BM * SLAB_PITCH, LANE), F32),
                        pltpu.VMEM((2, f, d), BF16),
                        pltpu.VMEM((2, _chunk_rows(d), d), F32),
                        pltpu.SemaphoreType.DMA((2,)), pltpu.SemaphoreType.DMA((2,))],
    )
    return pl.pallas_call(
        _moe_down_body,
        grid_spec=grid_spec,
        out_shape=jax.ShapeDtypeStruct((n_out_rows * SLAB_ROWS, LANE), F32),
        compiler_params=pltpu.CompilerParams(dimension_semantics=("arbitrary",),
                                             vmem_limit_bytes=48 * 1024 * 1024, has_side_effects=True),
        name="moe_down",
    )(sched["blk_e"], sched["n_used"], sched["wslot"], sched["wnext"], sched["wc0"], sched["wc1"],
      sched["nvalid"], dst_slot.reshape(nb, 1, BM), act, wd, bd.reshape(ne, 1, d))


def _final_body(y0_ref, y1_ref, y2_ref, y3_ref, rt_ref, x1_ref, ada_ref, g_ref, o_ref):
    tm = x1_ref.shape[0]
    rt = rt_ref[...]
    y_refs = (y0_ref, y1_ref, y2_ref, y3_ref)
    pieces = []
    for s in range(SLAB_ROWS):
        acc = rt[:, TOP_K:TOP_K + 1] * y_refs[0][pl.ds(s, tm, stride=SLAB_ROWS), :]
        for k in range(1, TOP_K):
            acc = acc + rt[:, TOP_K + k:TOP_K + k + 1] * y_refs[k][pl.ds(s, tm, stride=SLAB_ROWS), :]
        pieces.append(acc)
    y = jnp.concatenate(pieces, axis=1)
    gt_f = ada_ref[0, 5:6, :]
    o_ref[...] = x1_ref[...] + gt_f * (_rms(y) * g_ref[...])


def _final(ysc, rt, x1, ada3, g_post, seq):
    t, d = x1.shape
    tm = 256
    per_b = seq // tm
    nt = t // tm
    assert TOP_K == 4
    slot_spec = lambda k: pl.BlockSpec((tm * SLAB_ROWS, LANE), lambda i: (k * nt + i, 0))
    return pl.pallas_call(
        _final_body,
        grid=(nt,),
        in_specs=[slot_spec(0), slot_spec(1), slot_spec(2), slot_spec(3),
                  pl.BlockSpec((tm, LANE), lambda i: (i, 0)),
                  pl.BlockSpec((tm, d), lambda i: (i, 0)),
                  pl.BlockSpec((1, 6, d), lambda i: (i // per_b, 0, 0)),
                  pl.BlockSpec((1, d), lambda i: (0, 0))],
        out_specs=pl.BlockSpec((tm, d), lambda i: (i, 0)),
        out_shape=jax.ShapeDtypeStruct((t, d), F32),
        compiler_params=_cparams(("arbitrary",), 48),
        name="final",
    )(ysc, ysc, ysc, ysc, rt, x1, ada3, g_post.reshape(1, d))


def _mixer_ffn_layer(x2, ada3, bsz, seq, g_mix_pre, g_mix_post, g_ffn_pre, g_ffn_post, w_in, lb, g_rec_out,
                     rel_bias, w_branch_rec, w_branch_att, w_o, w_router, b_router, w_gate_up, b_gate_up,
                     w_down, b_down):
    t, d = x2.shape
    d_rec = w_branch_rec.shape[0]
    w_att = w_branch_att.shape[0]
    d_att = 3 * w_att
    widths = dict(q_r=d_rec, i_r=d_rec, zf_f=d_rec, zf_b=d_rec, z_o=d_rec, q_a=d_att, k_a=d_att, v_a=d_att,
                  zg_rec=d, zg_att=d)
    my_order = ("zg_rec", "zg_att", "q_r", "i_r", "zf_f", "zf_b", "z_o", "q_a", "k_a", "v_a")
    col, acc = {}, 0
    for name in my_order:
        col[name] = acc // LANE
        acc += widths[name]
    rot = acc - 2 * d

    proj = _inproj(x2, g_mix_pre, ada3, w_in, seq, rot)

    oi, qtf, qtb, utf, utb, df, db = _hgrn_a(proj, lb, col, t, d_rec)
    rec_o = _hgrn_c(oi, qtf, qtb, utf, utb, df, db, proj, g_rec_out, col, bsz, seq, d_rec)

    nums, stats = [], []
    for g, (window, dil) in enumerate(DIL_GROUPS):
        hs = slice(g * ATT_HEADS_PER_GROUP, (g + 1) * ATT_HEADS_PER_GROUP)
        bias = _band_bias(rel_bias[:, hs], window, dil)
        num, st = _attn_group(proj, bias, col, g, dil, bsz, seq)
        nums.append(num)
        stats.append(st)

    x1, h2s, logits = _merge(rec_o, nums, stats, proj, x2, ada3, g_mix_post, g_ffn_pre,
                            w_branch_rec.astype(BF16), w_branch_att.astype(BF16), w_o.astype(BF16),
                            w_router, b_router, col, seq)

    rt, cnt = _route(logits)
    ne = logits.shape[1]
    counts = cnt[0].astype(jnp.int32)
    top_idx = rt[:, 0:TOP_K].astype(jnp.int32)
    rank = rt[:, 2 * TOP_K:3 * TOP_K].astype(jnp.int32)
    padded = (counts + BM - 1) // BM * BM
    pends = jnp.cumsum(padded)
    pstarts = pends - padded
    experts = jnp.arange(ne, dtype=jnp.int32)
    pstart_sel = jnp.sum(jnp.where(top_idx[..., None] == experts, pstarts, 0), axis=-1)
    dest = (pstart_sel + rank).T.reshape(-1)
    p_rows = t * TOP_K + ne * BM
    nb = p_rows // BM
    blk_start = jnp.arange(nb, dtype=jnp.int32) * BM
    blk_e = jnp.minimum(jnp.sum((pends[None, :] <= blk_start[:, None]).astype(jnp.int32), axis=1), ne - 1)
    n_used = (pends[-1:] // BM).astype(jnp.int32)

    n_assign = t * TOP_K
    valid_end = pstarts + counts
    pad_lo = jnp.concatenate([valid_end, pends[-1:]]).astype(jnp.int32)
    pad_hi = jnp.concatenate([pends, jnp.full((1,), p_rows, jnp.int32)]).astype(jnp.int32)
    slot_assign = _invert_slots(dest, pad_lo, pad_hi, p_rows)
    blk_end = jnp.sum(jnp.where(blk_e[:, None] == experts, valid_end, 0), axis=-1)
    nvalid = jnp.clip(blk_end - blk_start, 0, BM).astype(jnp.int32)
    nvalid = jnp.where(jnp.arange(nb) < n_used[0], nvalid, 0)

    lookup = lambda table: jnp.sum(jnp.where(blk_e[:, None] == experts, table, 0), axis=-1)
    nonempty = padded > 0
    order = jnp.cumsum(nonempty.astype(jnp.int32)) - 1
    later = lax.cummin(jnp.where(nonempty, experts, ne)[::-1])[::-1]
    next_e = jnp.concatenate([later[1:], jnp.full((1,), ne, jnp.int32)])
    blk_next = lookup(next_e)
    has_next = (blk_next < ne) & (jnp.arange(nb) < n_used[0])
    k_in_run = jnp.arange(nb, dtype=jnp.int32) - lookup(pstarts // BM)
    n_in_run = jnp.maximum(lookup(padded // BM), 1)
    common = dict(blk_e=blk_e, n_used=n_used, nvalid=nvalid, wslot=lookup(order) % 2,
                  wnext=jnp.where(has_next, blk_next, blk_e))

    def schedule(w):
        nch = w.shape[1] // _chunk_rows(w.shape[2])
        s = dict(common, wc0=jnp.where(has_next, k_in_run * nch // n_in_run, 0),
                 wc1=jnp.where(has_next, (k_in_run + 1) * nch // n_in_run, 0))
        return {k: v.astype(jnp.int32) for k, v in s.items()}

    src_tok = jnp.maximum(slot_assign, 0) % t
    act = _moe_up(h2s, src_tok, w_gate_up, b_gate_up, schedule(w_gate_up))
    ysc = _moe_down(act, slot_assign, n_assign, w_down, b_down, schedule(w_down))
    return _final(ysc, rt, x1, ada3, g_ffn_post, seq)


def kernel(x, c, w_ada, b_ada, g_mix_pre, g_mix_post, g_ffn_pre, g_ffn_post, w_in, g_rec_out, w_branch_rec,
           w_branch_att, w_o, w_router, b_router, w_gate_up, b_gate_up, w_down, b_down, rec_lb_table, rel_bias):
    bsz, seq, d = x.shape
    depth = w_in.shape[0]
    lb_all = jnp.cumsum(jax.nn.softmax(rec_lb_table.astype(F32), axis=1), axis=1)
    x2 = x.reshape(bsz * seq, d)
    for layer in range(depth):
        ada3 = _ada(c, w_ada[layer], b_ada[layer]).reshape(bsz, 6, d)
        x2 = _mixer_ffn_layer(x2, ada3, bsz, seq, g_mix_pre[layer], g_mix_post[layer], g_ffn_pre[layer],
                              g_ffn_post[layer], w_in[layer], lb_all[:, layer], g_rec_out[layer], rel_bias,
                              w_branch_rec[layer], w_branch_att[layer], w_o[layer], w_router[layer],
                              b_router[layer], w_gate_up[layer], b_gate_up[layer], w_down[layer],
                              b_down[layer])
    return x2.reshape(bsz, seq, d)
```

```python
import functools
import math

import numpy as np
import jax
import jax.numpy as jnp
from jax import lax
from jax.experimental import pallas as pl
from jax.experimental.pallas import tpu as pltpu

F32 = jnp.float32
BF16 = jnp.bfloat16

LANE = 128
SUBLANE = 8
SLAB_ROWS = 16
SLAB_PITCH = 24

REC_HEAD_DIM = 128
REC_CHUNK = 64
ATT_HEAD_DIM = 128
ATT_HEADS_PER_GROUP = 4
ATT_BLOCK = 64
DIL_GROUPS = ((128, 1), (512, 4), (2048, 16))
NUM_BUCKETS = 32
MAX_DISTANCE = 1024
N_EXPERTS = 32
TOP_K = 4
SWIGLU_LIMIT = 7.0
SWIGLU_ALPHA = 1.702
RMS_EPS = 1e-6
NEG_INF = -1e30

N_LEVELS = 6
W_CHUNK_BYTES = 4 * 1024 * 1024
CAST_ROWS = 64
ROW_PATHS = 4
HGRN_UNROLL = 4
DMA_UNROLL = 8
ATT_UNROLL = 8
BM = 256

_NT = (((1,), (1,)), ((), ()))
_TN = (((0,), (0,)), ((), ()))


def _cparams(sem, vmem_mb):
    return pltpu.CompilerParams(dimension_semantics=sem, vmem_limit_bytes=vmem_mb * 1024 * 1024)


def _rms(x):
    return x * lax.rsqrt(jnp.mean(x * x, axis=-1, keepdims=True) + RMS_EPS)


def _ada_body(c_ref, w_ref, b_ref, o_ref):
    c = c_ref[...]
    cond = (c * jax.nn.sigmoid(c)).astype(BF16)
    o_ref[...] = jnp.dot(cond, w_ref[...].astype(BF16), preferred_element_type=F32) + b_ref[...]


def _ada(c, w, b):
    bsz, d = c.shape
    n = w.shape[1]
    tn = 1024
    cp = jnp.zeros((SUBLANE, d), F32).at[:bsz].set(c)
    out = pl.pallas_call(
        _ada_body,
        grid=(n // tn,),
        in_specs=[pl.BlockSpec((SUBLANE, d), lambda j: (0, 0)),
                  pl.BlockSpec((d, tn), lambda j: (0, j)),
                  pl.BlockSpec((1, tn), lambda j: (0, j))],
        out_specs=pl.BlockSpec((SUBLANE, tn), lambda j: (0, j)),
        out_shape=jax.ShapeDtypeStruct((SUBLANE, n), F32),
        compiler_params=_cparams(("arbitrary",), 40),
        name="ada",
    )(cp, w, b.reshape(1, n))
    return out[:bsz]


def _inproj_body(x_ref, g_ref, ada_ref, w_ref, o_ref, h_ref):
    @pl.when(pl.program_id(1) == 0)
    def _():
        half = x_ref.shape[0] // 2
        sh = ada_ref[0, 0:1, :]
        sc = ada_ref[0, 1:2, :]
        for r0 in (0, half):
            y = _rms(x_ref[r0:r0 + half, :]) * g_ref[...]
            h_ref[r0:r0 + half, :] = (y * (1.0 + sc) + sh).astype(BF16)

    o_ref[...] = jnp.dot(h_ref[...], w_ref[...].astype(BF16), preferred_element_type=F32)


def _inproj(x2, g, ada3, w_in, seq, rot):
    t, d = x2.shape
    n = w_in.shape[1]
    tm, tn = 2048, 512
    per_b = seq // tm
    nj = n // tn
    assert rot % tn == 0 and n % tn == 0 and seq % tm == 0
    return pl.pallas_call(
        _inproj_body,
        grid=(t // tm, nj),
        in_specs=[pl.BlockSpec((tm, d), lambda i, j: (i, 0), pipeline_mode=pl.Buffered(1)),
                  pl.BlockSpec((1, d), lambda i, j: (0, 0)),
                  pl.BlockSpec((1, 6, d), lambda i, j: (i // per_b, 0, 0)),
                  pl.BlockSpec((d, tn), lambda i, j: (0, (j + rot // tn) % nj))],
        out_specs=pl.BlockSpec((tm, tn), lambda i, j: (i, j)),
        out_shape=jax.ShapeDtypeStruct((t, n), F32),
        scratch_shapes=[pltpu.VMEM((tm, d), BF16)],
        compiler_params=_cparams(("arbitrary", "arbitrary"), 56),
        name="inproj",
    )(x2, g.reshape(1, d), ada3, w_in)


def _hgrn_consts():
    c = REC_CHUNK
    r = np.arange(c)[:, None]
    m = np.arange(c)[None, :]
    nw = N_LEVELS - 1
    wf = np.zeros(((nw + 2) * c, c), np.float32)
    wb = np.zeros(((nw + 2) * c, c), np.float32)
    mf = np.zeros((N_LEVELS + 1, c, c), np.float32)
    for lvl in range(N_LEVELS):
        s = 32 >> lvl
        m0 = (r // (2 * s)) * (2 * s) + s
        up = r >= m0
        if lvl < nw:
            wf[lvl * c:(lvl + 1) * c] = np.where(up, (m >= m0) & (m <= r), (m > r) & (m <= m0 - 1))
            wb[lvl * c:(lvl + 1) * c] = np.where(up, (m >= m0) & (m <= r - 1), (m >= r) & (m <= m0 - 1))
        i = np.arange(c)[:, None]
        j = np.arange(c)[None, :]
        mf[lvl] = (i // (2 * s) == j // (2 * s)) & (i % (2 * s) >= s) & (j % (2 * s) < s)
    mf[N_LEVELS] = np.eye(c)
    wf[nw * c:(nw + 1) * c] = m <= r
    wf[(nw + 1) * c:(nw + 2) * c] = m > r
    wb[nw * c:(nw + 1) * c] = m >= r
    wb[(nw + 1) * c:(nw + 2) * c] = m < r
    mfb = mf + np.transpose(mf, (0, 2, 1))
    mfb[N_LEVELS] = np.eye(c)
    up = np.zeros((N_LEVELS, c, LANE), np.float32)
    for lvl in range(N_LEVELS):
        s = 32 >> lvl
        up[lvl] = ((np.arange(c) % (2 * s)) >= s)[:, None]
    wf3 = np.concatenate([wf, wf, wf], axis=1)
    wb3 = np.concatenate([wb, wb, wb], axis=1)
    return (jnp.asarray(wf3, BF16), jnp.asarray(wb3, BF16), jnp.asarray(mfb, F32), jnp.asarray(up, F32),
            jnp.asarray(1.0 - up, F32))


def _split3(g):
    hi = g.astype(BF16)
    r1 = g - hi.astype(F32)
    mid = r1.astype(BF16)
    lo = (r1 - mid.astype(F32)).astype(BF16)
    return jnp.concatenate([hi, mid, lo], axis=0)


def _hgrn_a_body(q_ref, i_ref, zf_ref, zb_ref, lb_ref, wf_ref, wb_ref, mf_ref, up_ref, lo_ref,
                 oi_ref, qtf_ref, qtb_ref, utf_ref, utb_ref, df_ref, db_ref, *, cpb):
    c = REC_CHUNK
    dirs = ((zf_ref, wf_ref, None, qtf_ref, utf_ref, df_ref, 0, c - 1),
            (zb_ref, wb_ref, None, qtb_ref, utb_ref, db_ref, 1, 0))

    def chunk_group(cg, carry):
        cis = [cg * HGRN_UNROLL + u for u in range(HGRN_UNROLL)]
        rows = [pl.ds(pl.multiple_of(ci * c, c), c) for ci in cis]
        zqs = [q_ref[rw, :] for rw in rows]
        qs = [zq * jax.nn.sigmoid(zq) for zq in zqs]
        vbs = [i_ref[rw, :].astype(BF16) for rw in rows]
        units = [(u, d) for u in range(HGRN_UNROLL) for d in range(2)]
        nw = N_LEVELS - 1
        ks, es, fs = {}, {}, {}
        for u, d in units:
            z_ref, w_ref = dirs[d][0], dirs[d][1]
            lb = lb_ref[d:d + 1, :]
            f = lb + (1.0 - lb) * jax.nn.sigmoid(z_ref[rows[u], :])
            fs[u, d] = f
            ks[u, d] = 1.0 - f
            es[u, d] = jnp.exp(jnp.dot(w_ref[...], _split3(jnp.log(f)), preferred_element_type=F32))
        acc = [jnp.zeros((c, c), F32) for _ in range(HGRN_UNROLL)]
        for lvl in range(N_LEVELS + 1):
            for u in range(HGRN_UNROLL):
                if lvl < N_LEVELS:
                    up, lo = up_ref[lvl], lo_ref[lvl]
                    if lvl < nw:
                        ef = es[u, 0][lvl * c:(lvl + 1) * c]
                        eb = es[u, 1][lvl * c:(lvl + 1) * c]
                    else:
                        ef = fs[u, 0] * up + lo
                        eb = fs[u, 1] * lo + up
                    qa = jnp.concatenate([(qs[u] * (ef * up)).astype(BF16), (qs[u] * (eb * lo)).astype(BF16)],
                                         axis=1)
                    ka = jnp.concatenate([(ks[u, 0] * (ef * lo)).astype(BF16),
                                          (ks[u, 1] * (eb * up)).astype(BF16)], axis=1)
                else:
                    qa, ka = qs[u].astype(BF16), (ks[u, 0] + ks[u, 1]).astype(BF16)
                p = lax.dot_general(qa, ka, _NT, preferred_element_type=F32)
                acc[u] = acc[u] + p * mf_ref[lvl]
        for u, d in units:
            _, _, _, qt_ref, ut_ref, d_ref, _, drow = dirs[d]
            e = es[u, d]
            qt_ref[rows[u], :] = (qs[u] * e[nw * c:(nw + 1) * c]).astype(BF16)
            kt = (ks[u, d] * e[(nw + 1) * c:(nw + 2) * c]).astype(BF16)
            ut_ref[cis[u]] = lax.dot_general(vbs[u], kt, _TN, preferred_element_type=F32)
            d_ref[pl.ds(cis[u], 1), :] = e[nw * c + drow:nw * c + drow + 1]
        for u in range(HGRN_UNROLL):
            oi_ref[rows[u], :] = jnp.dot(acc[u].astype(BF16), vbs[u], preferred_element_type=F32)
        return carry

    lax.fori_loop(0, cpb // HGRN_UNROLL, chunk_group, 0)


def _hgrn_a(proj, lb, col, t, d_rec):
    heads = d_rec // REC_HEAD_DIM
    tq = 2048
    cpb = tq // REC_CHUNK
    nchunks = t // REC_CHUNK
    wf, wb, mf, up, lo = _hgrn_consts()
    hd = REC_HEAD_DIM

    def colspec(off):
        return pl.BlockSpec((tq, hd), lambda i, h: (i, off + h))

    full2 = lambda i, h: (0, 0)
    full3 = lambda i, h: (0, 0, 0)
    row_spec = pl.BlockSpec((tq, hd), lambda i, h: (i, h))
    u_spec = pl.BlockSpec((cpb, hd, hd), lambda i, h: (i, 0, h))
    d_spec = pl.BlockSpec((cpb, hd), lambda i, h: (i, h))
    return pl.pallas_call(
        functools.partial(_hgrn_a_body, cpb=cpb),
        grid=(t // tq, heads),
        in_specs=[colspec(col["q_r"]), colspec(col["i_r"]), colspec(col["zf_f"]), colspec(col["zf_b"]),
                  pl.BlockSpec((2, hd), lambda i, h: (0, h)),
                  pl.BlockSpec(wf.shape, full2), pl.BlockSpec(wb.shape, full2),
                  pl.BlockSpec(mf.shape, full3), pl.BlockSpec(up.shape, full3), pl.BlockSpec(lo.shape, full3)],
        out_specs=[row_spec, row_spec, row_spec, u_spec, u_spec, d_spec, d_spec],
        out_shape=[jax.ShapeDtypeStruct((t, d_rec), F32),
                   jax.ShapeDtypeStruct((t, d_rec), BF16),
                   jax.ShapeDtypeStruct((t, d_rec), BF16),
                   jax.ShapeDtypeStruct((nchunks, hd, d_rec), F32),
                   jax.ShapeDtypeStruct((nchunks, hd, d_rec), F32),
                   jax.ShapeDtypeStruct((nchunks, d_rec), F32),
                   jax.ShapeDtypeStruct((nchunks, d_rec), F32)],
        compiler_params=_cparams(("arbitrary", "arbitrary"), 32),
        name="hgrn_a",
    )(proj, proj, proj, proj, lb, wf, wb, mf, up, lo)


def _hgrn_c_body(oi_ref, qtf_ref, qtb_ref, utf_ref, utb_ref, df_ref, db_ref, z_ref, g_ref,
                 out_ref, acc_ref, accb_ref, *, nchunks):
    c = REC_CHUNK
    hd = REC_HEAD_DIM

    unroll = 4

    def step(i, carry):
        st_f, st_b = carry
        pending = []
        for u in range(unroll):
            nf = i * unroll + u
            nb = nchunks - 1 - nf
            rows_f = pl.ds(pl.multiple_of(nf * c, c), c)
            rows_b = pl.ds(pl.multiple_of(nb * c, c), c)
            of = lax.dot_general(qtf_ref[rows_f, :], st_f.astype(BF16), _NT, preferred_element_type=F32)
            ob = lax.dot_general(qtb_ref[rows_b, :], st_b.astype(BF16), _NT, preferred_element_type=F32)
            pending.append((rows_f, rows_b, oi_ref[rows_f, :] + of, ob))
            st_f = df_ref[pl.ds(nf, 1), :] * st_f + utf_ref[nf]
            st_b = db_ref[pl.ds(nb, 1), :] * st_b + utb_ref[nb]
        for rows_f, rows_b, vf, vb in pending:
            acc_ref[rows_f, :] = vf
            accb_ref[rows_b, :] = vb
        return st_f, st_b

    zero = jnp.zeros((hd, hd), F32)
    lax.fori_loop(0, nchunks // unroll, step, (zero, zero))

    o = _rms(acc_ref[...] + accb_ref[...])
    out_ref[...] = (o * g_ref[...] * jax.nn.sigmoid(z_ref[...])).astype(BF16)


def _hgrn_c(oi, qtf, qtb, utf, utb, df, db, proj, g_out, col, bsz, seq, d_rec):
    heads = d_rec // REC_HEAD_DIM
    hd = REC_HEAD_DIM
    nchunks = seq // REC_CHUNK
    row_spec = pl.BlockSpec((seq, hd), lambda b, h: (b, h))
    u_spec = pl.BlockSpec((nchunks, hd, hd), lambda b, h: (b, 0, h))
    d_spec = pl.BlockSpec((nchunks, hd), lambda b, h: (b, h))
    zo = col["z_o"]
    return pl.pallas_call(
        functools.partial(_hgrn_c_body, nchunks=nchunks),
        grid=(bsz, heads),
        in_specs=[row_spec, row_spec, row_spec, u_spec, u_spec, d_spec, d_spec,
                  pl.BlockSpec((seq, hd), lambda b, h: (b, zo + h)),
                  pl.BlockSpec((1, hd), lambda b, h: (0, h))],
        out_specs=row_spec,
        out_shape=jax.ShapeDtypeStruct((bsz * seq, d_rec), BF16),
        scratch_shapes=[pltpu.VMEM((seq, hd), F32), pltpu.VMEM((seq, hd), F32)],
        compiler_params=_cparams(("arbitrary", "arbitrary"), 48),
        name="hgrn_c",
    )(oi, qtf, qtb, utf, utb, df, db, proj, g_out.reshape(1, d_rec))


def _t5_bucket(rel):
    half_buckets = NUM_BUCKETS // 2
    ret = np.where(rel > 0, half_buckets, 0)
    n = np.abs(rel)
    max_exact = half_buckets // 2
    nf = np.maximum(n, 1).astype(np.float32)
    large = max_exact + (np.log(nf / np.float32(max_exact)) / np.float32(math.log(MAX_DISTANCE / max_exact))
                         * np.float32(half_buckets - max_exact)).astype(np.int32)
    large = np.minimum(large, half_buckets - 1)
    return ret + np.where(n < max_exact, n, large)


def _band_bias(rel_bias_g, window, dil):
    half = window // (2 * dil)
    q_off = np.arange(ATT_BLOCK)[:, None]
    rel = np.arange(3 * ATT_BLOCK)[None, :] - ATT_BLOCK - q_off
    onehot = (_t5_bucket(rel * dil)[..., None] == np.arange(NUM_BUCKETS)).astype(np.float32)
    bias = jnp.einsum("qkb,bh->hqk", jnp.asarray(onehot), rel_bias_g.astype(F32),
                      precision=lax.Precision.HIGHEST)
    return jnp.where(jnp.asarray(np.abs(rel) <= half)[None], bias, NEG_INF)


def _attn_body(q_ref, kp_ref, k_ref, kn_ref, vp_ref, v_ref, vn_ref, bias_ref,
               num_ref, st_ref, kc_ref, vc_ref, *, dil, tq, sub_len):
    blk = ATT_BLOCK
    nqb = tq // blk
    n = pl.program_id(1)
    scale = ATT_HEAD_DIM ** -0.5

    def sds(start, size):
        if dil == 1:
            return pl.ds(start, size)
        return pl.ds(start, size, stride=dil)

    cu = kc_ref.shape[0]
    qu = ATT_UNROLL // cu

    def deinterleave(r, j):
        kc_ref[j, 0:blk, :] = kp_ref[sds(r, blk), :].astype(BF16)
        kc_ref[j, blk:blk + tq, :] = k_ref[sds(r, tq), :].astype(BF16)
        kc_ref[j, blk + tq:2 * blk + tq, :] = kn_ref[sds(r, blk), :].astype(BF16)
        vc_ref[j, 0:blk, :] = vp_ref[sds(r, blk), :].astype(BF16)
        vc_ref[j, blk:blk + tq, :] = v_ref[sds(r, tq), :].astype(BF16)
        vc_ref[j, blk + tq:2 * blk + tq, :] = vn_ref[sds(r, blk), :].astype(BF16)

    def units(r0, qb0):
        us = [(j, u) for j in range(cu) for u in range(qu)]
        q0s = [pl.multiple_of((qb0 + u) * blk, blk) for _, u in us]
        rows = [sds(r0 + j + dil * q0, blk) for (j, _), q0 in zip(us, q0s)]
        lane = lax.broadcasted_iota(jnp.int32, (blk, LANE), 1)
        key_iota = lax.broadcasted_iota(jnp.int32, (1, 3 * blk), 1)
        bias = bias_ref[0]
        qs = [q_ref[rw, :].astype(BF16) for rw in rows]
        kws = [kc_ref[j, pl.ds(q0, 3 * blk), :] for (j, _), q0 in zip(us, q0s)]
        vws = [vc_ref[j, pl.ds(q0, 3 * blk), :] for (j, _), q0 in zip(us, q0s)]
        ss = [lax.dot_general(q, kw, _NT, preferred_element_type=F32) * scale for q, kw in zip(qs, kws)]
        valids = []
        for q0 in q0s:
            kpos = n * tq + q0 - blk + key_iota
            valids.append((kpos >= 0) & (kpos < sub_len))
        ss = [jnp.where(valid, s + bias, NEG_INF) for s, valid in zip(ss, valids)]
        ms = [jnp.max(s, axis=-1, keepdims=True) for s in ss]
        ps = [jnp.exp(s - m) for s, m in zip(ss, ms)]
        ls = [jnp.sum(p, axis=-1, keepdims=True) for p in ps]
        nums = [jnp.dot(p.astype(BF16), vw, preferred_element_type=F32) for p, vw in zip(ps, vws)]
        for rw, num, m, l in zip(rows, nums, ms, ls):
            num_ref[rw, :] = num
            st_ref[rw, :] = jnp.where(lane < LANE // 2, m, l)

    def class_group(rg, carry):
        r0 = rg * cu
        for j in range(cu):
            deinterleave(r0 + j, j)

        def qgroup(qg, carry2):
            units(r0, qg * qu)
            return carry2

        lax.fori_loop(0, nqb // qu, qgroup, 0)
        return carry

    lax.fori_loop(0, dil // cu, class_group, 0)


def _attn_group(proj, bias, col, g, dil, bsz, seq):
    tile = 4096
    tq = tile // dil
    halo = ATT_BLOCK * dil
    sub_len = seq // dil
    cu = ATT_UNROLL // min(tq // ATT_BLOCK, ATT_UNROLL)
    nh = ATT_HEADS_PER_GROUP
    hd = ATT_HEAD_DIM
    qc = col["q_a"] + g * nh
    kc = col["k_a"] + g * nh
    vc = col["v_a"] + g * nh
    tiles_b = seq // tile
    halos_b = seq // halo
    hpt = tile // halo

    own = lambda c: pl.BlockSpec((tile, hd), lambda b, n, h: (b * tiles_b + n, c + h))
    prev = lambda c: pl.BlockSpec(
        (halo, hd), lambda b, n, h: (b * halos_b + jnp.maximum(n * hpt - 1, 0), c + h))
    nxt = lambda c: pl.BlockSpec(
        (halo, hd), lambda b, n, h: (b * halos_b + jnp.minimum((n + 1) * hpt, halos_b - 1), c + h))
    t = bsz * seq
    return pl.pallas_call(
        functools.partial(_attn_body, dil=dil, tq=tq, sub_len=sub_len),
        grid=(bsz, tiles_b, nh),
        in_specs=[own(qc), prev(kc), own(kc), nxt(kc), prev(vc), own(vc), nxt(vc),
                  pl.BlockSpec((1,) + bias.shape[1:], lambda b, n, h: (h, 0, 0))],
        out_specs=[pl.BlockSpec((tile, hd), lambda b, n, h: (b * tiles_b + n, h)),
                   pl.BlockSpec((tile, LANE), lambda b, n, h: (b * tiles_b + n, h))],
        out_shape=[jax.ShapeDtypeStruct((t, nh * hd), F32), jax.ShapeDtypeStruct((t, nh * LANE), F32)],
        scratch_shapes=[pltpu.VMEM((cu, tq + 2 * ATT_BLOCK, hd), BF16),
                        pltpu.VMEM((cu, tq + 2 * ATT_BLOCK, hd), BF16)],
        compiler_params=_cparams(("arbitrary", "arbitrary", "arbitrary"), 32),
        name=f"attn_d{dil}",
    )(proj, proj, proj, proj, proj, proj, proj, bias)


def _merge_body(rec_ref, n0_ref, n1_ref, n2_ref, s0_ref, s1_ref, s2_ref, zgr_ref, zga_ref, x_ref,
                ada_ref, gpost_ref, gpre_ref, wbr_ref, wba_ref, wo_ref, wr_ref, br_ref,
                x1_ref, h2_ref, lg_ref):
    nh = ATT_HEADS_PER_GROUP
    hd = ATT_HEAD_DIM
    half = LANE // 2
    lane = lax.broadcasted_iota(jnp.int32, (rec_ref.shape[0], LANE), 1)
    heads = []
    for h in range(nh):
        cols = slice(h * hd, (h + 1) * hd)
        st = [s[:, cols] for s in (s0_ref, s1_ref, s2_ref)]
        top = jnp.maximum(jnp.maximum(st[0], st[1]), st[2])
        ws = [jnp.exp(s - top) for s in st]
        den = (ws[0] * pltpu.roll(st[0], half, 1) + ws[1] * pltpu.roll(st[1], half, 1)
               + ws[2] * pltpu.roll(st[2], half, 1))
        coef = [w / den for w in ws]
        coef = [jnp.where(lane < half, c, pltpu.roll(c, half, 1)) for c in coef]
        num = coef[0] * n0_ref[:, cols] + coef[1] * n1_ref[:, cols] + coef[2] * n2_ref[:, cols]
        heads.append(num.astype(BF16))
    att = jnp.concatenate(heads, axis=1)
    y_rec = jnp.dot(rec_ref[...], wbr_ref[...], preferred_element_type=F32)
    y_att = jnp.dot(att, wba_ref[...], preferred_element_type=F32)
    merged = jax.nn.sigmoid(zgr_ref[...]) * y_rec + jax.nn.sigmoid(zga_ref[...]) * y_att
    y = jnp.dot(merged.astype(BF16), wo_ref[...], preferred_element_type=F32)
    gt_m = ada_ref[0, 2:3, :]
    sh_f = ada_ref[0, 3:4, :]
    sc_f = ada_ref[0, 4:5, :]
    x1 = x_ref[...] + gt_m * (_rms(y) * gpost_ref[...])
    x1_ref[...] = x1
    h2 = _rms(x1) * gpre_ref[...] * (1.0 + sc_f) + sh_f
    tm = h2.shape[0]
    for s in range(SLAB_ROWS):
        h2_ref[pl.ds(s, tm, stride=SLAB_ROWS), :] = h2[:, s * LANE:(s + 1) * LANE]
    ne = lg_ref.shape[1]
    h_hi = h2.astype(BF16)
    h_lo = (h2 - h_hi.astype(F32)).astype(BF16)
    both = jnp.dot(h_hi, wr_ref[...], preferred_element_type=F32)
    cross = jnp.dot(h_lo, wr_ref[:, 0:ne], preferred_element_type=F32)
    lg_ref[...] = both[:, 0:ne] + both[:, ne:2 * ne] + cross + br_ref[...]


def _merge(rec_o, nums, stats, proj, x2, ada3, g_post, g_pre, wbr, wba, wo, w_router, b_router, col, seq):
    t, d = x2.shape
    tm = 256
    per_b = seq // tm
    d_rec = rec_o.shape[1]
    w_att = nums[0].shape[1]
    ne = w_router.shape[1]
    wr_hi = w_router.astype(BF16)
    wr_lo = (w_router - wr_hi.astype(F32)).astype(BF16)
    w_router = jnp.concatenate([wr_hi, wr_lo], axis=1)
    dl = d // LANE
    row = lambda w: pl.BlockSpec((tm, w), lambda i: (i, 0))
    const = lambda shape: pl.BlockSpec(shape, lambda i: (0,) * len(shape), pipeline_mode=pl.Buffered(1))
    zgr = col["zg_rec"] // dl
    zga = col["zg_att"] // dl
    return pl.pallas_call(
        _merge_body,
        grid=(t // tm,),
        in_specs=[row(d_rec), row(w_att), row(w_att), row(w_att), row(w_att), row(w_att), row(w_att),
                  pl.BlockSpec((tm, d), lambda i: (i, zgr)),
                  pl.BlockSpec((tm, d), lambda i: (i, zga)),
                  row(d),
                  pl.BlockSpec((1, 6, d), lambda i: (i // per_b, 0, 0)),
                  const((1, d)), const((1, d)),
                  const(wbr.shape), const(wba.shape), const(wo.shape), const(w_router.shape),
                  const((1, ne))],
        out_specs=[row(d), pl.BlockSpec((tm * SLAB_ROWS, LANE), lambda i: (i, 0)),
                   pl.BlockSpec((tm, ne), lambda i: (i, 0))],
        out_shape=[jax.ShapeDtypeStruct((t, d), F32), jax.ShapeDtypeStruct((t * SLAB_ROWS, LANE), F32),
                   jax.ShapeDtypeStruct((t, ne), F32)],
        compiler_params=_cparams(("arbitrary",), 56),
        name="merge",
    )(rec_o, nums[0], nums[1], nums[2], stats[0], stats[1], stats[2], proj, proj, x2, ada3,
      g_post.reshape(1, d), g_pre.reshape(1, d), wbr, wba, wo, w_router, b_router.reshape(1, ne))


def _route_body(lg_ref, tri_ref, rt_ref, cnt_ref, carry_ref):
    i = pl.program_id(0)
    tr, ne = lg_ref.shape

    @pl.when(i == 0)
    def _():
        carry_ref[...] = jnp.zeros_like(carry_ref)

    l = lg_ref[...]
    lane = lax.broadcasted_iota(jnp.int32, (tr, ne), 1).astype(F32)
    vals, sels, idxs = [], [], []
    for _ in range(TOP_K):
        m = jnp.max(l, axis=-1, keepdims=True)
        idx = jnp.min(jnp.where(l == m, lane, float(ne)), axis=-1, keepdims=True)
        sel = lane == idx
        vals.append(m)
        idxs.append(idx)
        sels.append(sel)
        l = jnp.where(sel, -jnp.inf, l)
    es = [jnp.exp(v - vals[0]) for v in vals]
    tot = es[0] + es[1] + es[2] + es[3]
    chosen = (sels[0] | sels[1] | sels[2] | sels[3]).astype(F32)
    prefix = jnp.dot(tri_ref[...], chosen.astype(BF16), preferred_element_type=F32) + carry_ref[0:1, :]
    out_lane = lax.broadcasted_iota(jnp.int32, (tr, LANE), 1)
    rt = jnp.zeros((tr, LANE), F32)
    for k in range(TOP_K):
        rank = jnp.sum(jnp.where(sels[k], prefix, 0.0), axis=-1, keepdims=True)
        rt = jnp.where(out_lane == k, idxs[k], rt)
        rt = jnp.where(out_lane == TOP_K + k, es[k] / tot, rt)
        rt = jnp.where(out_lane == 2 * TOP_K + k, rank, rt)
    rt_ref[...] = rt
    new = carry_ref[0:1, :] + jnp.sum(chosen, axis=0, keepdims=True)
    carry_ref[...] = jnp.broadcast_to(new, carry_ref.shape)
    cnt_ref[...] = carry_ref[...]


def _route(logits):
    t, ne = logits.shape
    tr = 512
    tri = jnp.asarray(np.tril(np.ones((tr, tr), np.float32), -1), BF16)
    return pl.pallas_call(
        _route_body,
        grid=(t // tr,),
        in_specs=[pl.BlockSpec((tr, ne), lambda i: (i, 0)),
                  pl.BlockSpec((tr, tr), lambda i: (0, 0))],
        out_specs=[pl.BlockSpec((tr, LANE), lambda i: (i, 0)),
                   pl.BlockSpec((SUBLANE, ne), lambda i: (0, 0))],
        out_shape=[jax.ShapeDtypeStruct((t, LANE), F32), jax.ShapeDtypeStruct((SUBLANE, ne), F32)],
        scratch_shapes=[pltpu.VMEM((SUBLANE, ne), F32)],
        compiler_params=_cparams(("arbitrary",), 32),
        name="route",
    )(logits, tri)


def _invert_body(pad_lo_ref, pad_hi_ref, dest_ref, out_ref, *, chunk):
    base = pl.program_id(0) * chunk

    @pl.when(pl.program_id(0) == 0)
    def _():
        def one_range(e, c):
            def fill(p, c2):
                out_ref[p] = -1
                return c2
            lax.fori_loop(pad_lo_ref[e], pad_hi_ref[e], fill, 0)
            return c
        lax.fori_loop(0, pad_lo_ref.shape[0], one_range, 0)

    def body(j, c):
        out_ref[dest_ref[0, 0, j]] = base + j
        return c
    lax.fori_loop(0, chunk, body, 0, unroll=DMA_UNROLL)


def _invert_slots(dest, pad_lo, pad_hi, p_rows):
    n = dest.shape[0]
    chunk = 8192
    nch = n // chunk
    grid_spec = pltpu.PrefetchScalarGridSpec(
        num_scalar_prefetch=2,
        grid=(nch,),
        in_specs=[pl.BlockSpec((1, 1, chunk), lambda i, lo, hi: (i, 0, 0), memory_space=pltpu.SMEM)],
        out_specs=pl.BlockSpec(memory_space=pltpu.SMEM),
    )
    return pl.pallas_call(
        functools.partial(_invert_body, chunk=chunk),
        grid_spec=grid_spec,
        out_shape=jax.ShapeDtypeStruct((p_rows,), jnp.int32),
        compiler_params=pltpu.CompilerParams(dimension_semantics=("arbitrary",)),
        name="invert_slots",
    )(pad_lo, pad_hi, dest.reshape(nch, 1, chunk))


def _for_row_count(nvalid, compute):
    q = BM // ROW_PATHS
    for i in range(1, ROW_PATHS + 1):
        lo, hi = (i - 1) * q, i * q
        cond = (nvalid > lo) & (nvalid <= hi) if i > 1 else (nvalid <= hi)

        @pl.when(cond)
        def _(m=hi):
            compute(m)


def _chunk_rows(ncols):
    return W_CHUNK_BYTES // (4 * ncols)


def _weight_stream(w_hbm, wbf, stage, wsem, e, slot, c0, c1, priority=0):
    kc = stage.shape[1]

    def copy(c):
        return pltpu.make_async_copy(w_hbm.at[e, pl.ds(pl.multiple_of(c * kc, kc), kc), :],
                                     stage.at[c % 2], wsem.at[c % 2])

    def prime():
        def body(c, carry):
            copy(c).start(priority=priority)
            return carry
        lax.fori_loop(c0, jnp.minimum(c0 + 2, c1), body, 0)

    def finish():
        def body(c, carry):
            copy(c).wait()
            buf = c % 2

            def cast(i, carry2):
                r = pl.multiple_of(i * CAST_ROWS, CAST_ROWS)
                wbf[slot, pl.ds(pl.multiple_of(c * kc, kc) + r, CAST_ROWS), :] = (
                    stage[buf, pl.ds(r, CAST_ROWS), :].astype(BF16))
                return carry2
            lax.fori_loop(0, kc // CAST_ROWS, cast, 0)

            @pl.when(c + 2 < c1)
            def _():
                copy(c + 2).start(priority=priority)
            return carry
        lax.fori_loop(c0, c1, body, 0)

    return prime, finish


def _moe_up_body(be_ref, nu_ref, ws_ref, wn_ref, wc0_ref, wc1_ref, nv_ref, idx0_ref, idxn_ref, h2s_ref, w_hbm,
                 bias_ref, o_ref, xbuf, wbf, stage, sem, wsem):
    b = pl.program_id(0)
    nu = nu_ref[0]
    f = o_ref.shape[1]
    slab = SLAB_ROWS
    nch = wbf.shape[1] // stage.shape[1]

    def row_start(idx_ref, slot, r):
        tok = idx_ref[0, 0, r]
        pltpu.make_async_copy(h2s_ref.at[pl.ds(pl.multiple_of(tok * slab, slab), slab), :],
                              xbuf.at[slot, pl.ds(pl.multiple_of(r * SLAB_PITCH, SUBLANE), slab), :],
                              sem.at[slot]).start()

    def wait_rows(slot):
        pltpu.make_async_copy(h2s_ref.at[pl.ds(0, BM * slab), :], xbuf.at[slot, pl.ds(0, BM * slab), :],
                              sem.at[slot]).wait()

    def issue(idx_ref, slot):
        def body(r, c):
            row_start(idx_ref, slot, r)
            return c
        lax.fori_loop(0, BM, body, 0, unroll=DMA_UNROLL)

    @pl.when(b == 0)
    def _():
        issue(idx0_ref, 0)
        prime0, finish0 = _weight_stream(w_hbm, wbf, stage, wsem, be_ref[0], ws_ref[0], 0, nch)
        prime0()
        finish0()

    @pl.when(b + 1 < nu)
    def _():
        issue(idxn_ref, (b + 1) % 2)

    @pl.when(b < nu)
    def _():
        wslot = ws_ref[b]
        prime, finish = _weight_stream(w_hbm, wbf, stage, wsem, wn_ref[b], 1 - wslot, wc0_ref[b], wc1_ref[b],
                                       priority=1)
        prime()
        slot = b % 2
        wait_rows(slot)

        def compute(m):
            x = jnp.concatenate([xbuf[slot, pl.ds(s, m, stride=SLAB_PITCH), :].astype(BF16) for s in range(slab)],
                                axis=1)
            half = f // 2
            for c0 in (0, half):
                gate = (jnp.dot(x, wbf[wslot, :, c0:c0 + half], preferred_element_type=F32)
                        + bias_ref[0, :, c0:c0 + half])
                up = (jnp.dot(x, wbf[wslot, :, f + c0:f + c0 + half], preferred_element_type=F32)
                      + bias_ref[0, :, f + c0:f + c0 + half])
                gate = jnp.minimum(gate, SWIGLU_LIMIT)
                up = jnp.clip(up, -SWIGLU_LIMIT, SWIGLU_LIMIT)
                o_ref[0:m, c0:c0 + half] = (gate * jax.nn.sigmoid(SWIGLU_ALPHA * gate) * (up + 1.0)).astype(BF16)
            if m < BM:
                o_ref[m:BM, :] = jnp.zeros((BM - m, f), BF16)

        _for_row_count(nv_ref[b], compute)
        finish()

    @pl.when(b >= nu)
    def _():
        o_ref[...] = jnp.zeros_like(o_ref)


def _moe_up(h2s, src_tok, wgu, bgu, sched):
    ne, d, f2 = wgu.shape
    f = f2 // 2
    p = src_tok.shape[0]
    nb = p // BM
    idx3 = src_tok.reshape(nb, 1, BM)
    smem_blk = lambda imap: pl.BlockSpec((1, 1, BM), imap, memory_space=pltpu.SMEM)
    grid_spec = pltpu.PrefetchScalarGridSpec(
        num_scalar_prefetch=7,
        grid=(nb,),
        in_specs=[smem_blk(lambda b, be, *_: (0, 0, 0)),
                  smem_blk(lambda b, be, *_: (jnp.minimum(b + 1, nb - 1), 0, 0)),
                  pl.BlockSpec(memory_space=pl.ANY),
                  pl.BlockSpec(memory_space=pl.ANY),
                  pl.BlockSpec((1, 1, f2), lambda b, be, *_: (be[b], 0, 0))],
        out_specs=pl.BlockSpec((BM, f), lambda b, be, *_: (b, 0)),
        scratch_shapes=[pltpu.VMEM((2, BM * SLAB_PITCH, LANE), F32),
                        pltpu.VMEM((2, d, f2), BF16),
                        pltpu.VMEM((2, _chunk_rows(f2), f2), F32),
                        pltpu.SemaphoreType.DMA((2,)), pltpu.SemaphoreType.DMA((2,))],
    )
    return pl.pallas_call(
        _moe_up_body,
        grid_spec=grid_spec,
        out_shape=jax.ShapeDtypeStruct((p, f), BF16),
        compiler_params=_cparams(("arbitrary",), 58),
        name="moe_up",
    )(sched["blk_e"], sched["n_used"], sched["wslot"], sched["wnext"], sched["wc0"], sched["wc1"],
      sched["nvalid"], idx3, idx3, h2s, wgu, bgu.reshape(ne, 1, f2))


def _moe_down_body(be_ref, nu_ref, ws_ref, wn_ref, wc0_ref, wc1_ref, nv_ref, dst_ref, a_ref, w_hbm, bias_ref,
                   ysc_ref, ybuf, wbf, stage, sem, wsem):
    b = pl.program_id(0)
    nb = pl.num_programs(0)
    nu = nu_ref[0]
    slab = SLAB_ROWS
    nch = wbf.shape[1] // stage.shape[1]

    @pl.when(b == 0)
    def _():
        prime0, finish0 = _weight_stream(w_hbm, wbf, stage, wsem, be_ref[0], ws_ref[0], 0, nch)
        prime0()
        finish0()

    def row_copy(slot, r, d):
        return pltpu.make_async_copy(ybuf.at[slot, pl.ds(pl.multiple_of(r * SLAB_PITCH, SUBLANE), slab), :],
                                     ysc_ref.at[pl.ds(pl.multiple_of(d * slab, slab), slab), :],
                                     sem.at[slot])

    def drain(step):
        slot = step % 2
        count = nv_ref[step]

        @pl.when(count == BM)
        def _():
            pltpu.make_async_copy(ybuf.at[slot, pl.ds(0, BM * slab), :], ysc_ref.at[pl.ds(0, BM * slab), :],
                                  sem.at[slot]).wait()

        @pl.when(count < BM)
        def _():
            def body(r, c):
                row_copy(slot, 0, 0).wait()
                return c
            lax.fori_loop(0, count, body, 0)

    @pl.when((b >= 2) & (b < nu))
    def _():
        drain(b - 2)

    @pl.when(b < nu)
    def _():
        slot = b % 2
        wslot = ws_ref[b]
        prime, finish = _weight_stream(w_hbm, wbf, stage, wsem, wn_ref[b], 1 - wslot, wc0_ref[b], wc1_ref[b])
        prime()
        def compute(m):
            y = jnp.dot(a_ref[0:m, :], wbf[wslot], preferred_element_type=F32) + bias_ref[0]
            for s in range(slab):
                ybuf[slot, pl.ds(s, m, stride=SLAB_PITCH), :] = y[:, s * LANE:(s + 1) * LANE]

        _for_row_count(nv_ref[b], compute)
        finish()

        def body(r, c):
            row_copy(slot, r, dst_ref[0, 0, r]).start()
            return c

        def pair(i, c):
            row_copy(slot, 2 * i, dst_ref[0, 0, 2 * i]).start(priority=0)
            row_copy(slot, 2 * i + 1, dst_ref[0, 0, 2 * i + 1]).start(priority=1)
            return c

        @pl.when(nv_ref[b] == BM)
        def _():
            lax.fori_loop(0, BM // 2, pair, 0, unroll=DMA_UNROLL // 2)

        @pl.when(nv_ref[b] < BM)
        def _():
            lax.fori_loop(0, nv_ref[b], body, 0)

    @pl.when(b == nb - 1)
    def _():
        @pl.when(nu >= 2)
        def _():
            drain(nu - 2)
        drain(nu - 1)


def _moe_down(act, dst_slot, n_out_rows, wd, bd, sched):
    p, f = act.shape
    ne, _, d = wd.shape
    nb = p // BM
    grid_spec = pltpu.PrefetchScalarGridSpec(
        num_scalar_prefetch=7,
        grid=(nb,),
        in_specs=[pl.BlockSpec((1, 1, BM), lambda b, be, *_: (b, 0, 0), memory_space=pltpu.SMEM),
                  pl.BlockSpec((BM, f), lambda b, be, *_: (b, 0)),
                  pl.BlockSpec(memory_space=pl.ANY),
                  pl.BlockSpec((1, 1, d), lambda b, be, *_: (be[b], 0, 0))],
        out_specs=pl.BlockSpec(memory_space=pl.ANY),
        scratch_shapes=[pltpu.VMEM((2, BM * SLAB_PITCH, LANE), F32),
                        pltpu.VMEM((2, f, d), BF16),
                        pltpu.VMEM((2, _chunk_rows(d), d), F32),
                        pltpu.SemaphoreType.DMA((2,)), pltpu.SemaphoreType.DMA((2,))],
    )
    return pl.pallas_call(
        _moe_down_body,
        grid_spec=grid_spec,
        out_shape=jax.ShapeDtypeStruct((n_out_rows * SLAB_ROWS, LANE), F32),
        compiler_params=pltpu.CompilerParams(dimension_semantics=("arbitrary",),
                                             vmem_limit_bytes=48 * 1024 * 1024, has_side_effects=True),
        name="moe_down",
    )(sched["blk_e"], sched["n_used"], sched["wslot"], sched["wnext"], sched["wc0"], sched["wc1"],
      sched["nvalid"], dst_slot.reshape(nb, 1, BM), act, wd, bd.reshape(ne, 1, d))


def _final_body(y0_ref, y1_ref, y2_ref, y3_ref, rt_ref, x1_ref, ada_ref, g_ref, o_ref):
    tm = x1_ref.shape[0]
    rt = rt_ref[...]
    y_refs = (y0_ref, y1_ref, y2_ref, y3_ref)
    pieces = []
    for s in range(SLAB_ROWS):
        acc = rt[:, TOP_K:TOP_K + 1] * y_refs[0][pl.ds(s, tm, stride=SLAB_ROWS), :]
        for k in range(1, TOP_K):
            acc = acc + rt[:, TOP_K + k:TOP_K + k + 1] * y_refs[k][pl.ds(s, tm, stride=SLAB_ROWS), :]
        pieces.append(acc)
    y = jnp.concatenate(pieces, axis=1)
    gt_f = ada_ref[0, 5:6, :]
    o_ref[...] = x1_ref[...] + gt_f * (_rms(y) * g_ref[...])


def _final(ysc, rt, x1, ada3, g_post, seq):
    t, d = x1.shape
    tm = 256
    per_b = seq // tm
    nt = t // tm
    assert TOP_K == 4
    slot_spec = lambda k: pl.BlockSpec((tm * SLAB_ROWS, LANE), lambda i: (k * nt + i, 0))
    return pl.pallas_call(
        _final_body,
        grid=(nt,),
        in_specs=[slot_spec(0), slot_spec(1), slot_spec(2), slot_spec(3),
                  pl.BlockSpec((tm, LANE), lambda i: (i, 0)),
                  pl.BlockSpec((tm, d), lambda i: (i, 0)),
                  pl.BlockSpec((1, 6, d), lambda i: (i // per_b, 0, 0)),
                  pl.BlockSpec((1, d), lambda i: (0, 0))],
        out_specs=pl.BlockSpec((tm, d), lambda i: (i, 0)),
        out_shape=jax.ShapeDtypeStruct((t, d), F32),
        compiler_params=_cparams(("arbitrary",), 48),
        name="final",
    )(ysc, ysc, ysc, ysc, rt, x1, ada3, g_post.reshape(1, d))


def _mixer_ffn_layer(x2, ada3, bsz, seq, g_mix_pre, g_mix_post, g_ffn_pre, g_ffn_post, w_in, lb, g_rec_out,
                     rel_bias, w_branch_rec, w_branch_att, w_o, w_router, b_router, w_gate_up, b_gate_up,
                     w_down, b_down):
    t, d = x2.shape
    d_rec = w_branch_rec.shape[0]
    w_att = w_branch_att.shape[0]
    d_att = 3 * w_att
    widths = dict(q_r=d_rec, i_r=d_rec, zf_f=d_rec, zf_b=d_rec, z_o=d_rec, q_a=d_att, k_a=d_att, v_a=d_att,
                  zg_rec=d, zg_att=d)
    my_order = ("zg_rec", "zg_att", "q_r", "i_r", "zf_f", "zf_b", "z_o", "q_a", "k_a", "v_a")
    col, acc = {}, 0
    for name in my_order:
        col[name] = acc // LANE
        acc += widths[name]
    rot = acc - 2 * d

    proj = _inproj(x2, g_mix_pre, ada3, w_in, seq, rot)

    oi, qtf, qtb, utf, utb, df, db = _hgrn_a(proj, lb, col, t, d_rec)
    rec_o = _hgrn_c(oi, qtf, qtb, utf, utb, df, db, proj, g_rec_out, col, bsz, seq, d_rec)

    nums, stats = [], []
    for g, (window, dil) in enumerate(DIL_GROUPS):
        hs = slice(g * ATT_HEADS_PER_GROUP, (g + 1) * ATT_HEADS_PER_GROUP)
        bias = _band_bias(rel_bias[:, hs], window, dil)
        num, st = _attn_group(proj, bias, col, g, dil, bsz, seq)
        nums.append(num)
        stats.append(st)

    x1, h2s, logits = _merge(rec_o, nums, stats, proj, x2, ada3, g_mix_post, g_ffn_pre,
                            w_branch_rec.astype(BF16), w_branch_att.astype(BF16), w_o.astype(BF16),
                            w_router, b_router, col, seq)

    rt, cnt = _route(logits)
    ne = logits.shape[1]
    counts = cnt[0].astype(jnp.int32)
    top_idx = rt[:, 0:TOP_K].astype(jnp.int32)
    rank = rt[:, 2 * TOP_K:3 * TOP_K].astype(jnp.int32)
    padded = (counts + BM - 1) // BM * BM
    pends = jnp.cumsum(padded)
    pstarts = pends - padded
    experts = jnp.arange(ne, dtype=jnp.int32)
    pstart_sel = jnp.sum(jnp.where(top_idx[..., None] == experts, pstarts, 0), axis=-1)
    dest = (pstart_sel + rank).T.reshape(-1)
    p_rows = t * TOP_K + ne * BM
    nb = p_rows // BM
    blk_start = jnp.arange(nb, dtype=jnp.int32) * BM
    blk_e = jnp.minimum(jnp.sum((pends[None, :] <= blk_start[:, None]).astype(jnp.int32), axis=1), ne - 1)
    n_used = (pends[-1:] // BM).astype(jnp.int32)

    n_assign = t * TOP_K
    valid_end = pstarts + counts
    pad_lo = jnp.concatenate([valid_end, pends[-1:]]).astype(jnp.int32)
    pad_hi = jnp.concatenate([pends, jnp.full((1,), p_rows, jnp.int32)]).astype(jnp.int32)
    slot_assign = _invert_slots(dest, pad_lo, pad_hi, p_rows)
    blk_end = jnp.sum(jnp.where(blk_e[:, None] == experts, valid_end, 0), axis=-1)
    nvalid = jnp.clip(blk_end - blk_start, 0, BM).astype(jnp.int32)
    nvalid = jnp.where(jnp.arange(nb) < n_used[0], nvalid, 0)

    lookup = lambda table: jnp.sum(jnp.where(blk_e[:, None] == experts, table, 0), axis=-1)
    nonempty = padded > 0
    order = jnp.cumsum(nonempty.astype(jnp.int32)) - 1
    later = lax.cummin(jnp.where(nonempty, experts, ne)[::-1])[::-1]
    next_e = jnp.concatenate([later[1:], jnp.full((1,), ne, jnp.int32)])
    blk_next = lookup(next_e)
    has_next = (blk_next < ne) & (jnp.arange(nb) < n_used[0])
    k_in_run = jnp.arange(nb, dtype=jnp.int32) - lookup(pstarts // BM)
    n_in_run = jnp.maximum(lookup(padded // BM), 1)
    common = dict(blk_e=blk_e, n_used=n_used, nvalid=nvalid, wslot=lookup(order) % 2,
                  wnext=jnp.where(has_next, blk_next, blk_e))

    def schedule(w):
        nch = w.shape[1] // _chunk_rows(w.shape[2])
        s = dict(common, wc0=jnp.where(has_next, k_in_run * nch // n_in_run, 0),
                 wc1=jnp.where(has_next, (k_in_run + 1) * nch // n_in_run, 0))
        return {k: v.astype(jnp.int32) for k, v in s.items()}

    src_tok = jnp.maximum(slot_assign, 0) % t
    act = _moe_up(h2s, src_tok, w_gate_up, b_gate_up, schedule(w_gate_up))
    ysc = _moe_down(act, slot_assign, n_assign, w_down, b_down, schedule(w_down))
    return _final(ysc, rt, x1, ada3, g_ffn_post, seq)


def kernel(x, c, w_ada, b_ada, g_mix_pre, g_mix_post, g_ffn_pre, g_ffn_post, w_in, g_rec_out, w_branch_rec,
           w_branch_att, w_o, w_router, b_router, w_gate_up, b_gate_up, w_down, b_down, rec_lb_table, rel_bias):
    bsz, seq, d = x.shape
    depth = w_in.shape[0]
    lb_all = jnp.cumsum(jax.nn.softmax(rec_lb_table.astype(F32), axis=1), axis=1)
    x2 = x.reshape(bsz * seq, d)
    for layer in range(depth):
        ada3 = _ada(c, w_ada[layer], b_ada[layer]).reshape(bsz, 6, d)
        x2 = _mixer_ffn_layer(x2, ada3, bsz, seq, g_mix_pre[layer], g_mix_post[layer], g_ffn_pre[layer],
                              g_ffn_post[layer], w_in[layer], lb_all[:, layer], g_rec_out[layer], rel_bias,
                              w_branch_rec[layer], w_branch_att[layer], w_o[layer], w_router[layer],
                              b_router[layer], w_gate_up[layer], b_gate_up[layer], w_down[layer],
                              b_down[layer])
    return x2.reshape(bsz, seq, d)
```

```python
import functools
import math

import numpy as np
import jax
import jax.numpy as jnp
from jax import lax
from jax.experimental import pallas as pl
from jax.experimental.pallas import tpu as pltpu

F32 = jnp.float32
BF16 = jnp.bfloat16

LANE = 128
SUBLANE = 8
SLAB_ROWS = 16
SLAB_PITCH = 24

REC_HEAD_DIM = 128
REC_CHUNK = 64
ATT_HEAD_DIM = 128
ATT_HEADS_PER_GROUP = 4
ATT_BLOCK = 64
DIL_GROUPS = ((128, 1), (512, 4), (2048, 16))
NUM_BUCKETS = 32
MAX_DISTANCE = 1024
N_EXPERTS = 32
TOP_K = 4
SWIGLU_LIMIT = 7.0
SWIGLU_ALPHA = 1.702
RMS_EPS = 1e-6
NEG_INF = -1e30

N_LEVELS = 6
W_CHUNK_BYTES = 4 * 1024 * 1024
CAST_ROWS = 64
ROW_PATHS = 4
HGRN_UNROLL = 4
DMA_UNROLL = 8
ATT_UNROLL = 8
BM = 256

_NT = (((1,), (1,)), ((), ()))
_TN = (((0,), (0,)), ((), ()))


def _cparams(sem, vmem_mb):
    return pltpu.CompilerParams(dimension_semantics=sem, vmem_limit_bytes=vmem_mb * 1024 * 1024)


def _rms(x):
    return x * lax.rsqrt(jnp.mean(x * x, axis=-1, keepdims=True) + RMS_EPS)


def _ada_body(c_ref, w_ref, b_ref, o_ref):
    c = c_ref[...]
    cond = (c * jax.nn.sigmoid(c)).astype(BF16)
    o_ref[...] = jnp.dot(cond, w_ref[...].astype(BF16), preferred_element_type=F32) + b_ref[...]


def _ada(c, w, b):
    bsz, d = c.shape
    n = w.shape[1]
    tn = 1024
    cp = jnp.zeros((SUBLANE, d), F32).at[:bsz].set(c)
    out = pl.pallas_call(
        _ada_body,
        grid=(n // tn,),
        in_specs=[pl.BlockSpec((SUBLANE, d), lambda j: (0, 0)),
                  pl.BlockSpec((d, tn), lambda j: (0, j)),
                  pl.BlockSpec((1, tn), lambda j: (0, j))],
        out_specs=pl.BlockSpec((SUBLANE, tn), lambda j: (0, j)),
        out_shape=jax.ShapeDtypeStruct((SUBLANE, n), F32),
        compiler_params=_cparams(("arbitrary",), 40),
        name="ada",
    )(cp, w, b.reshape(1, n))
    return out[:bsz]


def _inproj_body(x_ref, g_ref, ada_ref, w_ref, o_ref, h_ref):
    @pl.when(pl.program_id(1) == 0)
    def _():
        half = x_ref.shape[0] // 2
        sh = ada_ref[0, 0:1, :]
        sc = ada_ref[0, 1:2, :]
        for r0 in (0, half):
            y = _rms(x_ref[r0:r0 + half, :]) * g_ref[...]
            h_ref[r0:r0 + half, :] = (y * (1.0 + sc) + sh).astype(BF16)

    o_ref[...] = jnp.dot(h_ref[...], w_ref[...].astype(BF16), preferred_element_type=F32)


def _inproj(x2, g, ada3, w_in, seq, rot):
    t, d = x2.shape
    n = w_in.shape[1]
    tm, tn = 2048, 512
    per_b = seq // tm
    nj = n // tn
    assert rot % tn == 0 and n % tn == 0 and seq % tm == 0
    return pl.pallas_call(
        _inproj_body,
        grid=(t // tm, nj),
        in_specs=[pl.BlockSpec((tm, d), lambda i, j: (i, 0), pipeline_mode=pl.Buffered(1)),
                  pl.BlockSpec((1, d), lambda i, j: (0, 0)),
                  pl.BlockSpec((1, 6, d), lambda i, j: (i // per_b, 0, 0)),
                  pl.BlockSpec((d, tn), lambda i, j: (0, (j + rot // tn) % nj))],
        out_specs=pl.BlockSpec((tm, tn), lambda i, j: (i, j)),
        out_shape=jax.ShapeDtypeStruct((t, n), F32),
        scratch_shapes=[pltpu.VMEM((tm, d), BF16)],
        compiler_params=_cparams(("arbitrary", "arbitrary"), 56),
        name="inproj",
    )(x2, g.reshape(1, d), ada3, w_in)


def _hgrn_consts():
    c = REC_CHUNK
    r = np.arange(c)[:, None]
    m = np.arange(c)[None, :]
    nw = N_LEVELS - 1
    wf = np.zeros(((nw + 2) * c, c), np.float32)
    wb = np.zeros(((nw + 2) * c, c), np.float32)
    mf = np.zeros((N_LEVELS + 1, c, c), np.float32)
    for lvl in range(N_LEVELS):
        s = 32 >> lvl
        m0 = (r // (2 * s)) * (2 * s) + s
        up = r >= m0
        if lvl < nw:
            wf[lvl * c:(lvl + 1) * c] = np.where(up, (m >= m0) & (m <= r), (m > r) & (m <= m0 - 1))
            wb[lvl * c:(lvl + 1) * c] = np.where(up, (m >= m0) & (m <= r - 1), (m >= r) & (m <= m0 - 1))
        i = np.arange(c)[:, None]
        j = np.arange(c)[None, :]
        mf[lvl] = (i // (2 * s) == j // (2 * s)) & (i % (2 * s) >= s) & (j % (2 * s) < s)
    mf[N_LEVELS] = np.eye(c)
    wf[nw * c:(nw + 1) * c] = m <= r
    wf[(nw + 1) * c:(nw + 2) * c] = m > r
    wb[nw * c:(nw + 1) * c] = m >= r
    wb[(nw + 1) * c:(nw + 2) * c] = m < r
    mfb = mf + np.transpose(mf, (0, 2, 1))
    mfb[N_LEVELS] = np.eye(c)
    up = np.zeros((N_LEVELS, c, LANE), np.float32)
    for lvl in range(N_LEVELS):
        s = 32 >> lvl
        up[lvl] = ((np.arange(c) % (2 * s)) >= s)[:, None]
    wf3 = np.concatenate([wf, wf, wf], axis=1)
    wb3 = np.concatenate([wb, wb, wb], axis=1)
    return (jnp.asarray(wf3, BF16), jnp.asarray(wb3, BF16), jnp.asarray(mfb, F32), jnp.asarray(up, F32),
            jnp.asarray(1.0 - up, F32))


def _split3(g):
    hi = g.astype(BF16)
    r1 = g - hi.astype(F32)
    mid = r1.astype(BF16)
    lo = (r1 - mid.astype(F32)).astype(BF16)
    return jnp.concatenate([hi, mid, lo], axis=0)


def _hgrn_a_body(q_ref, i_ref, zf_ref, zb_ref, lb_ref, wf_ref, wb_ref, mf_ref, up_ref, lo_ref,
                 oi_ref, qtf_ref, qtb_ref, utf_ref, utb_ref, df_ref, db_ref, *, cpb):
    c = REC_CHUNK
    dirs = ((zf_ref, wf_ref, None, qtf_ref, utf_ref, df_ref, 0, c - 1),
            (zb_ref, wb_ref, None, qtb_ref, utb_ref, db_ref, 1, 0))

    def chunk_group(cg, carry):
        cis = [cg * HGRN_UNROLL + u for u in range(HGRN_UNROLL)]
        rows = [pl.ds(pl.multiple_of(ci * c, c), c) for ci in cis]
        zqs = [q_ref[rw, :] for rw in rows]
        qs = [zq * jax.nn.sigmoid(zq) for zq in zqs]
        vbs = [i_ref[rw, :].astype(BF16) for rw in rows]
        units = [(u, d) for u in range(HGRN_UNROLL) for d in range(2)]
        nw = N_LEVELS - 1
        ks, es, fs = {}, {}, {}
        for u, d in units:
            z_ref, w_ref = dirs[d][0], dirs[d][1]
            lb = lb_ref[d:d + 1, :]
            f = lb + (1.0 - lb) * jax.nn.sigmoid(z_ref[rows[u], :])
            fs[u, d] = f
            ks[u, d] = 1.0 - f
            es[u, d] = jnp.exp(jnp.dot(w_ref[...], _split3(jnp.log(f)), preferred_element_type=F32))
        acc = [jnp.zeros((c, c), F32) for _ in range(HGRN_UNROLL)]
        for lvl in range(N_LEVELS + 1):
            for u in range(HGRN_UNROLL):
                if lvl < N_LEVELS:
                    up, lo = up_ref[lvl], lo_ref[lvl]
                    if lvl < nw:
                        ef = es[u, 0][lvl * c:(lvl + 1) * c]
                        eb = es[u, 1][lvl * c:(lvl + 1) * c]
                    else:
                        ef = fs[u, 0] * up + lo
                        eb = fs[u, 1] * lo + up
                    qa = jnp.concatenate([(qs[u] * (ef * up)).astype(BF16), (qs[u] * (eb * lo)).astype(BF16)],
                                         axis=1)
                    ka = jnp.concatenate([(ks[u, 0] * (ef * lo)).astype(BF16),
                                          (ks[u, 1] * (eb * up)).astype(BF16)], axis=1)
                else:
                    qa, ka = qs[u].astype(BF16), (ks[u, 0] + ks[u, 1]).astype(BF16)
                p = lax.dot_general(qa, ka, _NT, preferred_element_type=F32)
                acc[u] = acc[u] + p * mf_ref[lvl]
        for u, d in units:
            _, _, _, qt_ref, ut_ref, d_ref, _, drow = dirs[d]
            e = es[u, d]
            qt_ref[rows[u], :] = (qs[u] * e[nw * c:(nw + 1) * c]).astype(BF16)
            kt = (ks[u, d] * e[(nw + 1) * c:(nw + 2) * c]).astype(BF16)
            ut_ref[cis[u]] = lax.dot_general(vbs[u], kt, _TN, preferred_element_type=F32)
            d_ref[pl.ds(cis[u], 1), :] = e[nw * c + drow:nw * c + drow + 1]
        for u in range(HGRN_UNROLL):
            oi_ref[rows[u], :] = jnp.dot(acc[u].astype(BF16), vbs[u], preferred_element_type=F32)
        return carry

    lax.fori_loop(0, cpb // HGRN_UNROLL, chunk_group, 0)


def _hgrn_a(proj, lb, col, t, d_rec):
    heads = d_rec // REC_HEAD_DIM
    tq = 2048
    cpb = tq // REC_CHUNK
    nchunks = t // REC_CHUNK
    wf, wb, mf, up, lo = _hgrn_consts()
    hd = REC_HEAD_DIM

    def colspec(off):
        return pl.BlockSpec((tq, hd), lambda i, h: (i, off + h))

    full2 = lambda i, h: (0, 0)
    full3 = lambda i, h: (0, 0, 0)
    row_spec = pl.BlockSpec((tq, hd), lambda i, h: (i, h))
    u_spec = pl.BlockSpec((cpb, hd, hd), lambda i, h: (i, 0, h))
    d_spec = pl.BlockSpec((cpb, hd), lambda i, h: (i, h))
    return pl.pallas_call(
        functools.partial(_hgrn_a_body, cpb=cpb),
        grid=(t // tq, heads),
        in_specs=[colspec(col["q_r"]), colspec(col["i_r"]), colspec(col["zf_f"]), colspec(col["zf_b"]),
                  pl.BlockSpec((2, hd), lambda i, h: (0, h)),
                  pl.BlockSpec(wf.shape, full2), pl.BlockSpec(wb.shape, full2),
                  pl.BlockSpec(mf.shape, full3), pl.BlockSpec(up.shape, full3), pl.BlockSpec(lo.shape, full3)],
        out_specs=[row_spec, row_spec, row_spec, u_spec, u_spec, d_spec, d_spec],
        out_shape=[jax.ShapeDtypeStruct((t, d_rec), F32),
                   jax.ShapeDtypeStruct((t, d_rec), BF16),
                   jax.ShapeDtypeStruct((t, d_rec), BF16),
                   jax.ShapeDtypeStruct((nchunks, hd, d_rec), F32),
                   jax.ShapeDtypeStruct((nchunks, hd, d_rec), F32),
                   jax.ShapeDtypeStruct((nchunks, d_rec), F32),
                   jax.ShapeDtypeStruct((nchunks, d_rec), F32)],
        compiler_params=_cparams(("arbitrary", "arbitrary"), 32),
        name="hgrn_a",
    )(proj, proj, proj, proj, lb, wf, wb, mf, up, lo)


def _hgrn_c_body(oi_ref, qtf_ref, qtb_ref, utf_ref, utb_ref, df_ref, db_ref, z_ref, g_ref,
                 out_ref, acc_ref, accb_ref, *, nchunks):
    c = REC_CHUNK
    hd = REC_HEAD_DIM

    unroll = 4

    def step(i, carry):
        st_f, st_b = carry
        pending = []
        for u in range(unroll):
            nf = i * unroll + u
            nb = nchunks - 1 - nf
            rows_f = pl.ds(pl.multiple_of(nf * c, c), c)
            rows_b = pl.ds(pl.multiple_of(nb * c, c), c)
            of = lax.dot_general(qtf_ref[rows_f, :], st_f.astype(BF16), _NT, preferred_element_type=F32)
            ob = lax.dot_general(qtb_ref[rows_b, :], st_b.astype(BF16), _NT, preferred_element_type=F32)
            pending.append((rows_f, rows_b, oi_ref[rows_f, :] + of, ob))
            st_f = df_ref[pl.ds(nf, 1), :] * st_f + utf_ref[nf]
            st_b = db_ref[pl.ds(nb, 1), :] * st_b + utb_ref[nb]
        for rows_f, rows_b, vf, vb in pending:
            acc_ref[rows_f, :] = vf
            accb_ref[rows_b, :] = vb
        return st_f, st_b

    zero = jnp.zeros((hd, hd), F32)
    lax.fori_loop(0, nchunks // unroll, step, (zero, zero))

    o = _rms(acc_ref[...] + accb_ref[...])
    out_ref[...] = (o * g_ref[...] * jax.nn.sigmoid(z_ref[...])).astype(BF16)


def _hgrn_c(oi, qtf, qtb, utf, utb, df, db, proj, g_out, col, bsz, seq, d_rec):
    heads = d_rec // REC_HEAD_DIM
    hd = REC_HEAD_DIM
    nchunks = seq // REC_CHUNK
    row_spec = pl.BlockSpec((seq, hd), lambda b, h: (b, h))
    u_spec = pl.BlockSpec((nchunks, hd, hd), lambda b, h: (b, 0, h))
    d_spec = pl.BlockSpec((nchunks, hd), lambda b, h: (b, h))
    zo = col["z_o"]
    return pl.pallas_call(
        functools.partial(_hgrn_c_body, nchunks=nchunks),
        grid=(bsz, heads),
        in_specs=[row_spec, row_spec, row_spec, u_spec, u_spec, d_spec, d_spec,
                  pl.BlockSpec((seq, hd), lambda b, h: (b, zo + h)),
                  pl.BlockSpec((1, hd), lambda b, h: (0, h))],
        out_specs=row_spec,
        out_shape=jax.ShapeDtypeStruct((bsz * seq, d_rec), BF16),
        scratch_shapes=[pltpu.VMEM((seq, hd), F32), pltpu.VMEM((seq, hd), F32)],
        compiler_params=_cparams(("arbitrary", "arbitrary"), 48),
        name="hgrn_c",
    )(oi, qtf, qtb, utf, utb, df, db, proj, g_out.reshape(1, d_rec))


def _t5_bucket(rel):
    half_buckets = NUM_BUCKETS // 2
    ret = np.where(rel > 0, half_buckets, 0)
    n = np.abs(rel)
    max_exact = half_buckets // 2
    nf = np.maximum(n, 1).astype(np.float32)
    large = max_exact + (np.log(nf / np.float32(max_exact)) / np.float32(math.log(MAX_DISTANCE / max_exact))
                         * np.float32(half_buckets - max_exact)).astype(np.int32)
    large = np.minimum(large, half_buckets - 1)
    return ret + np.where(n < max_exact, n, large)


def _band_bias(rel_bias_g, window, dil):
    half = window // (2 * dil)
    q_off = np.arange(ATT_BLOCK)[:, None]
    rel = np.arange(3 * ATT_BLOCK)[None, :] - ATT_BLOCK - q_off
    onehot = (_t5_bucket(rel * dil)[..., None] == np.arange(NUM_BUCKETS)).astype(np.float32)
    bias = jnp.einsum("qkb,bh->hqk", jnp.asarray(onehot), rel_bias_g.astype(F32),
                      precision=lax.Precision.HIGHEST)
    return jnp.where(jnp.asarray(np.abs(rel) <= half)[None], bias, NEG_INF)


def _attn_body(q_ref, kp_ref, k_ref, kn_ref, vp_ref, v_ref, vn_ref, bias_ref,
               num_ref, st_ref, kc_ref, vc_ref, *, dil, tq, sub_len):
    blk = ATT_BLOCK
    nqb = tq // blk
    n = pl.program_id(1)
    scale = ATT_HEAD_DIM ** -0.5

    def sds(start, size):
        if dil == 1:
            return pl.ds(start, size)
        return pl.ds(start, size, stride=dil)

    cu = kc_ref.shape[0]
    qu = ATT_UNROLL // cu

    def deinterleave(r, j):
        kc_ref[j, 0:blk, :] = kp_ref[sds(r, blk), :].astype(BF16)
        kc_ref[j, blk:blk + tq, :] = k_ref[sds(r, tq), :].astype(BF16)
        kc_ref[j, blk + tq:2 * blk + tq, :] = kn_ref[sds(r, blk), :].astype(BF16)
        vc_ref[j, 0:blk, :] = vp_ref[sds(r, blk), :].astype(BF16)
        vc_ref[j, blk:blk + tq, :] = v_ref[sds(r, tq), :].astype(BF16)
        vc_ref[j, blk + tq:2 * blk + tq, :] = vn_ref[sds(r, blk), :].astype(BF16)

    def units(r0, qb0):
        us = [(j, u) for j in range(cu) for u in range(qu)]
        q0s = [pl.multiple_of((qb0 + u) * blk, blk) for _, u in us]
        rows = [sds(r0 + j + dil * q0, blk) for (j, _), q0 in zip(us, q0s)]
        lane = lax.broadcasted_iota(jnp.int32, (blk, LANE), 1)
        key_iota = lax.broadcasted_iota(jnp.int32, (1, 3 * blk), 1)
        bias = bias_ref[0]
        qs = [q_ref[rw, :].astype(BF16) for rw in rows]
        kws = [kc_ref[j, pl.ds(q0, 3 * blk), :] for (j, _), q0 in zip(us, q0s)]
        vws = [vc_ref[j, pl.ds(q0, 3 * blk), :] for (j, _), q0 in zip(us, q0s)]
        ss = [lax.dot_general(q, kw, _NT, preferred_element_type=F32) * scale for q, kw in zip(qs, kws)]
        valids = []
        for q0 in q0s:
            kpos = n * tq + q0 - blk + key_iota
            valids.append((kpos >= 0) & (kpos < sub_len))
        ss = [jnp.where(valid, s + bias, NEG_INF) for s, valid in zip(ss, valids)]
        ms = [jnp.max(s, axis=-1, keepdims=True) for s in ss]
        ps = [jnp.exp(s - m) for s, m in zip(ss, ms)]
        ls = [jnp.sum(p, axis=-1, keepdims=True) for p in ps]
        nums = [jnp.dot(p.astype(BF16), vw, preferred_element_type=F32) for p, vw in zip(ps, vws)]
        for rw, num, m, l in zip(rows, nums, ms, ls):
            num_ref[rw, :] = num
            st_ref[rw, :] = jnp.where(lane < LANE // 2, m, l)

    def class_group(rg, carry):
        r0 = rg * cu
        for j in range(cu):
            deinterleave(r0 + j, j)

        def qgroup(qg, carry2):
            units(r0, qg * qu)
            return carry2

        lax.fori_loop(0, nqb // qu, qgroup, 0)
        return carry

    lax.fori_loop(0, dil // cu, class_group, 0)


def _attn_group(proj, bias, col, g, dil, bsz, seq):
    tile = 4096
    tq = tile // dil
    halo = ATT_BLOCK * dil
    sub_len = seq // dil
    cu = ATT_UNROLL // min(tq // ATT_BLOCK, ATT_UNROLL)
    nh = ATT_HEADS_PER_GROUP
    hd = ATT_HEAD_DIM
    qc = col["q_a"] + g * nh
    kc = col["k_a"] + g * nh
    vc = col["v_a"] + g * nh
    tiles_b = seq // tile
    halos_b = seq // halo
    hpt = tile // halo

    own = lambda c: pl.BlockSpec((tile, hd), lambda b, n, h: (b * tiles_b + n, c + h))
    prev = lambda c: pl.BlockSpec(
        (halo, hd), lambda b, n, h: (b * halos_b + jnp.maximum(n * hpt - 1, 0), c + h))
    nxt = lambda c: pl.BlockSpec(
        (halo, hd), lambda b, n, h: (b * halos_b + jnp.minimum((n + 1) * hpt, halos_b - 1), c + h))
    t = bsz * seq
    return pl.pallas_call(
        functools.partial(_attn_body, dil=dil, tq=tq, sub_len=sub_len),
        grid=(bsz, tiles_b, nh),
        in_specs=[own(qc), prev(kc), own(kc), nxt(kc), prev(vc), own(vc), nxt(vc),
                  pl.BlockSpec((1,) + bias.shape[1:], lambda b, n, h: (h, 0, 0))],
        out_specs=[pl.BlockSpec((tile, hd), lambda b, n, h: (b * tiles_b + n, h)),
                   pl.BlockSpec((tile, LANE), lambda b, n, h: (b * tiles_b + n, h))],
        out_shape=[jax.ShapeDtypeStruct((t, nh * hd), F32), jax.ShapeDtypeStruct((t, nh * LANE), F32)],
        scratch_shapes=[pltpu.VMEM((cu, tq + 2 * ATT_BLOCK, hd), BF16),
                        pltpu.VMEM((cu, tq + 2 * ATT_BLOCK, hd), BF16)],
        compiler_params=_cparams(("arbitrary", "arbitrary", "arbitrary"), 32),
        name=f"attn_d{dil}",
    )(proj, proj, proj, proj, proj, proj, proj, bias)


def _merge_body(rec_ref, n0_ref, n1_ref, n2_ref, s0_ref, s1_ref, s2_ref, zgr_ref, zga_ref, x_ref,
                ada_ref, gpost_ref, gpre_ref, wbr_ref, wba_ref, wo_ref, wr_ref, br_ref,
                x1_ref, h2_ref, lg_ref):
    nh = ATT_HEADS_PER_GROUP
    hd = ATT_HEAD_DIM
    half = LANE // 2
    lane = lax.broadcasted_iota(jnp.int32, (rec_ref.shape[0], LANE), 1)
    heads = []
    for h in range(nh):
        cols = slice(h * hd, (h + 1) * hd)
        st = [s[:, cols] for s in (s0_ref, s1_ref, s2_ref)]
        top = jnp.maximum(jnp.maximum(st[0], st[1]), st[2])
        ws = [jnp.exp(s - top) for s in st]
        den = (ws[0] * pltpu.roll(st[0], half, 1) + ws[1] * pltpu.roll(st[1], half, 1)
               + ws[2] * pltpu.roll(st[2], half, 1))
        coef = [w / den for w in ws]
        coef = [jnp.where(lane < half, c, pltpu.roll(c, half, 1)) for c in coef]
        num = coef[0] * n0_ref[:, cols] + coef[1] * n1_ref[:, cols] + coef[2] * n2_ref[:, cols]
        heads.append(num.astype(BF16))
    att = jnp.concatenate(heads, axis=1)
    y_rec = jnp.dot(rec_ref[...], wbr_ref[...], preferred_element_type=F32)
    y_att = jnp.dot(att, wba_ref[...], preferred_element_type=F32)
    merged = jax.nn.sigmoid(zgr_ref[...]) * y_rec + jax.nn.sigmoid(zga_ref[...]) * y_att
    y = jnp.dot(merged.astype(BF16), wo_ref[...], preferred_element_type=F32)
    gt_m = ada_ref[0, 2:3, :]
    sh_f = ada_ref[0, 3:4, :]
    sc_f = ada_ref[0, 4:5, :]
    x1 = x_ref[...] + gt_m * (_rms(y) * gpost_ref[...])
    x1_ref[...] = x1
    h2 = _rms(x1) * gpre_ref[...] * (1.0 + sc_f) + sh_f
    tm = h2.shape[0]
    for s in range(SLAB_ROWS):
        h2_ref[pl.ds(s, tm, stride=SLAB_ROWS), :] = h2[:, s * LANE:(s + 1) * LANE]
    ne = lg_ref.shape[1]
    h_hi = h2.astype(BF16)
    h_lo = (h2 - h_hi.astype(F32)).astype(BF16)
    both = jnp.dot(h_hi, wr_ref[...], preferred_element_type=F32)
    cross = jnp.dot(h_lo, wr_ref[:, 0:ne], preferred_element_type=F32)
    lg_ref[...] = both[:, 0:ne] + both[:, ne:2 * ne] + cross + br_ref[...]


def _merge(rec_o, nums, stats, proj, x2, ada3, g_post, g_pre, wbr, wba, wo, w_router, b_router, col, seq):
    t, d = x2.shape
    tm = 256
    per_b = seq // tm
    d_rec = rec_o.shape[1]
    w_att = nums[0].shape[1]
    ne = w_router.shape[1]
    wr_hi = w_router.astype(BF16)
    wr_lo = (w_router - wr_hi.astype(F32)).astype(BF16)
    w_router = jnp.concatenate([wr_hi, wr_lo], axis=1)
    dl = d // LANE
    row = lambda w: pl.BlockSpec((tm, w), lambda i: (i, 0))
    const = lambda shape: pl.BlockSpec(shape, lambda i: (0,) * len(shape), pipeline_mode=pl.Buffered(1))
    zgr = col["zg_rec"] // dl
    zga = col["zg_att"] // dl
    return pl.pallas_call(
        _merge_body,
        grid=(t // tm,),
        in_specs=[row(d_rec), row(w_att), row(w_att), row(w_att), row(w_att), row(w_att), row(w_att),
                  pl.BlockSpec((tm, d), lambda i: (i, zgr)),
                  pl.BlockSpec((tm, d), lambda i: (i, zga)),
                  row(d),
                  pl.BlockSpec((1, 6, d), lambda i: (i // per_b, 0, 0)),
                  const((1, d)), const((1, d)),
                  const(wbr.shape), const(wba.shape), const(wo.shape), const(w_router.shape),
                  const((1, ne))],
        out_specs=[row(d), pl.BlockSpec((tm * SLAB_ROWS, LANE), lambda i: (i, 0)),
                   pl.BlockSpec((tm, ne), lambda i: (i, 0))],
        out_shape=[jax.ShapeDtypeStruct((t, d), F32), jax.ShapeDtypeStruct((t * SLAB_ROWS, LANE), F32),
                   jax.ShapeDtypeStruct((t, ne), F32)],
        compiler_params=_cparams(("arbitrary",), 56),
        name="merge",
    )(rec_o, nums[0], nums[1], nums[2], stats[0], stats[1], stats[2], proj, proj, x2, ada3,
      g_post.reshape(1, d), g_pre.reshape(1, d), wbr, wba, wo, w_router, b_router.reshape(1, ne))


def _route_body(lg_ref, tri_ref, rt_ref, cnt_ref, carry_ref):
    i = pl.program_id(0)
    tr, ne = lg_ref.shape

    @pl.when(i == 0)
    def _():
        carry_ref[...] = jnp.zeros_like(carry_ref)

    l = lg_ref[...]
    lane = lax.broadcasted_iota(jnp.int32, (tr, ne), 1).astype(F32)
    vals, sels, idxs = [], [], []
    for _ in range(TOP_K):
        m = jnp.max(l, axis=-1, keepdims=True)
        idx = jnp.min(jnp.where(l == m, lane, float(ne)), axis=-1, keepdims=True)
        sel = lane == idx
        vals.append(m)
        idxs.append(idx)
        sels.append(sel)
        l = jnp.where(sel, -jnp.inf, l)
    es = [jnp.exp(v - vals[0]) for v in vals]
    tot = es[0] + es[1] + es[2] + es[3]
    chosen = (sels[0] | sels[1] | sels[2] | sels[3]).astype(F32)
    prefix = jnp.dot(tri_ref[...], chosen.astype(BF16), preferred_element_type=F32) + carry_ref[0:1, :]
    out_lane = lax.broadcasted_iota(jnp.int32, (tr, LANE), 1)
    rt = jnp.zeros((tr, LANE), F32)
    for k in range(TOP_K):
        rank = jnp.sum(jnp.where(sels[k], prefix, 0.0), axis=-1, keepdims=True)
        rt = jnp.where(out_lane == k, idxs[k], rt)
        rt = jnp.where(out_lane == TOP_K + k, es[k] / tot, rt)
        rt = jnp.where(out_lane == 2 * TOP_K + k, rank, rt)
    rt_ref[...] = rt
    new = carry_ref[0:1, :] + jnp.sum(chosen, axis=0, keepdims=True)
    carry_ref[...] = jnp.broadcast_to(new, carry_ref.shape)
    cnt_ref[...] = carry_ref[...]


def _route(logits):
    t, ne = logits.shape
    tr = 512
    tri = jnp.asarray(np.tril(np.ones((tr, tr), np.float32), -1), BF16)
    return pl.pallas_call(
        _route_body,
        grid=(t // tr,),
        in_specs=[pl.BlockSpec((tr, ne), lambda i: (i, 0)),
                  pl.BlockSpec((tr, tr), lambda i: (0, 0))],
        out_specs=[pl.BlockSpec((tr, LANE), lambda i: (i, 0)),
                   pl.BlockSpec((SUBLANE, ne), lambda i: (0, 0))],
        out_shape=[jax.ShapeDtypeStruct((t, LANE), F32), jax.ShapeDtypeStruct((SUBLANE, ne), F32)],
        scratch_shapes=[pltpu.VMEM((SUBLANE, ne), F32)],
        compiler_params=_cparams(("arbitrary",), 32),
        name="route",
    )(logits, tri)


def _invert_body(pad_lo_ref, pad_hi_ref, dest_ref, out_ref, *, chunk):
    base = pl.program_id(0) * chunk

    @pl.when(pl.program_id(0) == 0)
    def _():
        def one_range(e, c):
            def fill(p, c2):
                out_ref[p] = -1
                return c2
            lax.fori_loop(pad_lo_ref[e], pad_hi_ref[e], fill, 0)
            return c
        lax.fori_loop(0, pad_lo_ref.shape[0], one_range, 0)

    def body(j, c):
        out_ref[dest_ref[0, 0, j]] = base + j
        return c
    lax.fori_loop(0, chunk, body, 0, unroll=DMA_UNROLL)


def _invert_slots(dest, pad_lo, pad_hi, p_rows):
    n = dest.shape[0]
    chunk = 8192
    nch = n // chunk
    grid_spec = pltpu.PrefetchScalarGridSpec(
        num_scalar_prefetch=2,
        grid=(nch,),
        in_specs=[pl.BlockSpec((1, 1, chunk), lambda i, lo, hi: (i, 0, 0), memory_space=pltpu.SMEM)],
        out_specs=pl.BlockSpec(memory_space=pltpu.SMEM),
    )
    return pl.pallas_call(
        functools.partial(_invert_body, chunk=chunk),
        grid_spec=grid_spec,
        out_shape=jax.ShapeDtypeStruct((p_rows,), jnp.int32),
        compiler_params=pltpu.CompilerParams(dimension_semantics=("arbitrary",)),
        name="invert_slots",
    )(pad_lo, pad_hi, dest.reshape(nch, 1, chunk))


def _for_row_count(nvalid, compute):
    q = BM // ROW_PATHS
    for i in range(1, ROW_PATHS + 1):
        lo, hi = (i - 1) * q, i * q
        cond = (nvalid > lo) & (nvalid <= hi) if i > 1 else (nvalid <= hi)

        @pl.when(cond)
        def _(m=hi):
            compute(m)


def _chunk_rows(ncols):
    return W_CHUNK_BYTES // (4 * ncols)


def _weight_stream(w_hbm, wbf, stage, wsem, e, slot, c0, c1, priority=0):
    kc = stage.shape[1]

    def copy(c):
        return pltpu.make_async_copy(w_hbm.at[e, pl.ds(pl.multiple_of(c * kc, kc), kc), :],
                                     stage.at[c % 2], wsem.at[c % 2])

    def prime():
        def body(c, carry):
            copy(c).start(priority=priority)
            return carry
        lax.fori_loop(c0, jnp.minimum(c0 + 2, c1), body, 0)

    def finish():
        def body(c, carry):
            copy(c).wait()
            buf = c % 2

            def cast(i, carry2):
                r = pl.multiple_of(i * CAST_ROWS, CAST_ROWS)
                wbf[slot, pl.ds(pl.multiple_of(c * kc, kc) + r, CAST_ROWS), :] = (
                    stage[buf, pl.ds(r, CAST_ROWS), :].astype(BF16))
                return carry2
            lax.fori_loop(0, kc // CAST_ROWS, cast, 0)

            @pl.when(c + 2 < c1)
            def _():
                copy(c + 2).start(priority=priority)
            return carry
        lax.fori_loop(c0, c1, body, 0)

    return prime, finish


def _moe_up_body(be_ref, nu_ref, ws_ref, wn_ref, wc0_ref, wc1_ref, nv_ref, idx0_ref, idxn_ref, h2s_ref, w_hbm,
                 bias_ref, o_ref, xbuf, wbf, stage, sem, wsem):
    b = pl.program_id(0)
    nu = nu_ref[0]
    f = o_ref.shape[1]
    slab = SLAB_ROWS
    nch = wbf.shape[1] // stage.shape[1]

    def row_start(idx_ref, slot, r):
        tok = idx_ref[0, 0, r]
        pltpu.make_async_copy(h2s_ref.at[pl.ds(pl.multiple_of(tok * slab, slab), slab), :],
                              xbuf.at[slot, pl.ds(pl.multiple_of(r * SLAB_PITCH, SUBLANE), slab), :],
                              sem.at[slot]).start()

    def wait_rows(slot):
        pltpu.make_async_copy(h2s_ref.at[pl.ds(0, BM * slab), :], xbuf.at[slot, pl.ds(0, BM * slab), :],
                              sem.at[slot]).wait()

    def issue(idx_ref, slot):
        def body(r, c):
            row_start(idx_ref, slot, r)
            return c
        lax.fori_loop(0, BM, body, 0, unroll=DMA_UNROLL)

    @pl.when(b == 0)
    def _():
        issue(idx0_ref, 0)
        prime0, finish0 = _weight_stream(w_hbm, wbf, stage, wsem, be_ref[0], ws_ref[0], 0, nch)
        prime0()
        finish0()

    def next_weights():
        return _weight_stream(w_hbm, wbf, stage, wsem, wn_ref[b], 1 - ws_ref[b], wc0_ref[b], wc1_ref[b],
                              priority=1)

    @pl.when(b < nu)
    def _():
        next_weights()[0]()

    @pl.when(b + 1 < nu)
    def _():
        issue(idxn_ref, (b + 1) % 2)

    @pl.when(b < nu)
    def _():
        wslot = ws_ref[b]
        finish = next_weights()[1]
        slot = b % 2
        wait_rows(slot)

        def compute(m):
            x = jnp.concatenate([xbuf[slot, pl.ds(s, m, stride=SLAB_PITCH), :].astype(BF16) for s in range(slab)],
                                axis=1)
            half = f // 2
            for c0 in (0, half):
                gate = (jnp.dot(x, wbf[wslot, :, c0:c0 + half], preferred_element_type=F32)
                        + bias_ref[0, :, c0:c0 + half])
                up = (jnp.dot(x, wbf[wslot, :, f + c0:f + c0 + half], preferred_element_type=F32)
                      + bias_ref[0, :, f + c0:f + c0 + half])
                gate = jnp.minimum(gate, SWIGLU_LIMIT)
                up = jnp.clip(up, -SWIGLU_LIMIT, SWIGLU_LIMIT)
                o_ref[0:m, c0:c0 + half] = (gate * jax.nn.sigmoid(SWIGLU_ALPHA * gate) * (up + 1.0)).astype(BF16)
            if m < BM:
                o_ref[m:BM, :] = jnp.zeros((BM - m, f), BF16)

        _for_row_count(nv_ref[b], compute)
        finish()

    @pl.when(b >= nu)
    def _():
        o_ref[...] = jnp.zeros_like(o_ref)


def _moe_up(h2s, src_tok, wgu, bgu, sched):
    ne, d, f2 = wgu.shape
    f = f2 // 2
    p = src_tok.shape[0]
    nb = p // BM
    idx3 = src_tok.reshape(nb, 1, BM)
    smem_blk = lambda imap: pl.BlockSpec((1, 1, BM), imap, memory_space=pltpu.SMEM)
    grid_spec = pltpu.PrefetchScalarGridSpec(
        num_scalar_prefetch=7,
        grid=(nb,),
        in_specs=[smem_blk(lambda b, be, *_: (0, 0, 0)),
                  smem_blk(lambda b, be, *_: (jnp.minimum(b + 1, nb - 1), 0, 0)),
                  pl.BlockSpec(memory_space=pl.ANY),
                  pl.BlockSpec(memory_space=pl.ANY),
                  pl.BlockSpec((1, 1, f2), lambda b, be, *_: (be[b], 0, 0))],
        out_specs=pl.BlockSpec((BM, f), lambda b, be, *_: (b, 0)),
        scratch_shapes=[pltpu.VMEM((2, BM * SLAB_PITCH, LANE), F32),
                        pltpu.VMEM((2, d, f2), BF16),
                        pltpu.VMEM((2, _chunk_rows(f2), f2), F32),
                        pltpu.SemaphoreType.DMA((2,)), pltpu.SemaphoreType.DMA((2,))],
    )
    return pl.pallas_call(
        _moe_up_body,
        grid_spec=grid_spec,
        out_shape=jax.ShapeDtypeStruct((p, f), BF16),
        compiler_params=_cparams(("arbitrary",), 58),
        name="moe_up",
    )(sched["blk_e"], sched["n_used"], sched["wslot"], sched["wnext"], sched["wc0"], sched["wc1"],
      sched["nvalid"], idx3, idx3, h2s, wgu, bgu.reshape(ne, 1, f2))


def _moe_down_body(be_ref, nu_ref, ws_ref, wn_ref, wc0_ref, wc1_ref, nv_ref, dst_ref, a_ref, w_hbm, bias_ref,
                   ysc_ref, ybuf, wbf, stage, sem, wsem):
    b = pl.program_id(0)
    nb = pl.num_programs(0)
    nu = nu_ref[0]
    slab = SLAB_ROWS
    nch = wbf.shape[1] // stage.shape[1]

    @pl.when(b == 0)
    def _():
        prime0, finish0 = _weight_stream(w_hbm, wbf, stage, wsem, be_ref[0], ws_ref[0], 0, nch)
        prime0()
        finish0()

    def row_copy(slot, r, d):
        return pltpu.make_async_copy(ybuf.at[slot, pl.ds(pl.multiple_of(r * SLAB_PITCH, SUBLANE), slab), :],
                                     ysc_ref.at[pl.ds(pl.multiple_of(d * slab, slab), slab), :],
                                     sem.at[slot])

    def drain(step):
        slot = step % 2
        count = nv_ref[step]

        @pl.when(count == BM)
        def _():
            pltpu.make_async_copy(ybuf.at[slot, pl.ds(0, BM * slab), :], ysc_ref.at[pl.ds(0, BM * slab), :],
                                  sem.at[slot]).wait()

        @pl.when(count < BM)
        def _():
            def body(r, c):
                row_copy(slot, 0, 0).wait()
                return c
            lax.fori_loop(0, count, body, 0)

    @pl.when((b >= 2) & (b < nu))
    def _():
        drain(b - 2)

    @pl.when(b < nu)
    def _():
        slot = b % 2
        wslot = ws_ref[b]
        prime, finish = _weight_stream(w_hbm, wbf, stage, wsem, wn_ref[b], 1 - wslot, wc0_ref[b], wc1_ref[b])
        prime()
        def compute(m):
            y = jnp.dot(a_ref[0:m, :], wbf[wslot], preferred_element_type=F32) + bias_ref[0]
            for s in range(slab):
                ybuf[slot, pl.ds(s, m, stride=SLAB_PITCH), :] = y[:, s * LANE:(s + 1) * LANE]

        _for_row_count(nv_ref[b], compute)
        finish()

        def body(r, c):
            row_copy(slot, r, dst_ref[0, 0, r]).start()
            return c

        def pair(i, c):
            row_copy(slot, 2 * i, dst_ref[0, 0, 2 * i]).start(priority=0)
            row_copy(slot, 2 * i + 1, dst_ref[0, 0, 2 * i + 1]).start(priority=1)
            return c

        @pl.when(nv_ref[b] == BM)
        def _():
            lax.fori_loop(0, BM // 2, pair, 0, unroll=DMA_UNROLL // 2)

        @pl.when(nv_ref[b] < BM)
        def _():
            lax.fori_loop(0, nv_ref[b], body, 0)

    @pl.when(b == nb - 1)
    def _():
        @pl.when(nu >= 2)
        def _():
            drain(nu - 2)
        drain(nu - 1)


def _moe_down(act, dst_slot, n_out_rows, wd, bd, sched):
    p, f = act.shape
    ne, _, d = wd.shape
    nb = p // BM
    grid_spec = pltpu.PrefetchScalarGridSpec(
        num_scalar_prefetch=7,
        grid=(nb,),
        in_specs=[pl.BlockSpec((1, 1, BM), lambda b, be, *_: (b, 0, 0), memory_space=pltpu.SMEM),
                  pl.BlockSpec((BM, f), lambda b, be, *_: (b, 0)),
                  pl.BlockSpec(memory_space=pl.ANY),
                  pl.BlockSpec((1, 1, d), lambda b, be, *_: (be[b], 0, 0))],
        out_specs=pl.BlockSpec(memory_space=pl.ANY),
        scratch_shapes=[pltpu.VMEM((2, BM * SLAB_PITCH, LANE), F32),
                        pltpu.VMEM((2, f, d), BF16),
                        pltpu.VMEM((2, _chunk_rows(d), d), F32),
                        pltpu.SemaphoreType.DMA((2,)), pltpu.SemaphoreType.DMA((2,))],
    )
    return pl.pallas_call(
        _moe_down_body,
        grid_spec=grid_spec,
        out_shape=jax.ShapeDtypeStruct((n_out_rows * SLAB_ROWS, LANE), F32),
        compiler_params=pltpu.CompilerParams(dimension_semantics=("arbitrary",),
                                             vmem_limit_bytes=48 * 1024 * 1024, has_side_effects=True),
        name="moe_down",
    )(sched["blk_e"], sched["n_used"], sched["wslot"], sched["wnext"], sched["wc0"], sched["wc1"],
      sched["nvalid"], dst_slot.reshape(nb, 1, BM), act, wd, bd.reshape(ne, 1, d))


def _final_body(y0_ref, y1_ref, y2_ref, y3_ref, rt_ref, x1_ref, ada_ref, g_ref, o_ref):
    tm = x1_ref.shape[0]
    rt = rt_ref[...]
    y_refs = (y0_ref, y1_ref, y2_ref, y3_ref)
    pieces = []
    for s in range(SLAB_ROWS):
        acc = rt[:, TOP_K:TOP_K + 1] * y_refs[0][pl.ds(s, tm, stride=SLAB_ROWS), :]
        for k in range(1, TOP_K):
            acc = acc + rt[:, TOP_K + k:TOP_K + k + 1] * y_refs[k][pl.ds(s, tm, stride=SLAB_ROWS), :]
        pieces.append(acc)
    y = jnp.concatenate(pieces, axis=1)
    gt_f = ada_ref[0, 5:6, :]
    o_ref[...] = x1_ref[...] + gt_f * (_rms(y) * g_ref[...])


def _final(ysc, rt, x1, ada3, g_post, seq):
    t, d = x1.shape
    tm = 256
    per_b = seq // tm
    nt = t // tm
    assert TOP_K == 4
    slot_spec = lambda k: pl.BlockSpec((tm * SLAB_ROWS, LANE), lambda i: (k * nt + i, 0))
    return pl.pallas_call(
        _final_body,
        grid=(nt,),
        in_specs=[slot_spec(0), slot_spec(1), slot_spec(2), slot_spec(3),
                  pl.BlockSpec((tm, LANE), lambda i: (i, 0)),
                  pl.BlockSpec((tm, d), lambda i: (i, 0)),
                  pl.BlockSpec((1, 6, d), lambda i: (i // per_b, 0, 0)),
                  pl.BlockSpec((1, d), lambda i: (0, 0))],
        out_specs=pl.BlockSpec((tm, d), lambda i: (i, 0)),
        out_shape=jax.ShapeDtypeStruct((t, d), F32),
        compiler_params=_cparams(("arbitrary",), 48),
        name="final",
    )(ysc, ysc, ysc, ysc, rt, x1, ada3, g_post.reshape(1, d))


def _mixer_ffn_layer(x2, ada3, bsz, seq, g_mix_pre, g_mix_post, g_ffn_pre, g_ffn_post, w_in, lb, g_rec_out,
                     rel_bias, w_branch_rec, w_branch_att, w_o, w_router, b_router, w_gate_up, b_gate_up,
                     w_down, b_down):
    t, d = x2.shape
    d_rec = w_branch_rec.shape[0]
    w_att = w_branch_att.shape[0]
    d_att = 3 * w_att
    widths = dict(q_r=d_rec, i_r=d_rec, zf_f=d_rec, zf_b=d_rec, z_o=d_rec, q_a=d_att, k_a=d_att, v_a=d_att,
                  zg_rec=d, zg_att=d)
    my_order = ("zg_rec", "zg_att", "q_r", "i_r", "zf_f", "zf_b", "z_o", "q_a", "k_a", "v_a")
    col, acc = {}, 0
    for name in my_order:
        col[name] = acc // LANE
        acc += widths[name]
    rot = acc - 2 * d

    proj = _inproj(x2, g_mix_pre, ada3, w_in, seq, rot)

    oi, qtf, qtb, utf, utb, df, db = _hgrn_a(proj, lb, col, t, d_rec)
    rec_o = _hgrn_c(oi, qtf, qtb, utf, utb, df, db, proj, g_rec_out, col, bsz, seq, d_rec)

    nums, stats = [], []
    for g, (window, dil) in enumerate(DIL_GROUPS):
        hs = slice(g * ATT_HEADS_PER_GROUP, (g + 1) * ATT_HEADS_PER_GROUP)
        bias = _band_bias(rel_bias[:, hs], window, dil)
        num, st = _attn_group(proj, bias, col, g, dil, bsz, seq)
        nums.append(num)
        stats.append(st)

    x1, h2s, logits = _merge(rec_o, nums, stats, proj, x2, ada3, g_mix_post, g_ffn_pre,
                            w_branch_rec.astype(BF16), w_branch_att.astype(BF16), w_o.astype(BF16),
                            w_router, b_router, col, seq)

    rt, cnt = _route(logits)
    ne = logits.shape[1]
    counts = cnt[0].astype(jnp.int32)
    top_idx = rt[:, 0:TOP_K].astype(jnp.int32)
    rank = rt[:, 2 * TOP_K:3 * TOP_K].astype(jnp.int32)
    padded = (counts + BM - 1) // BM * BM
    pends = jnp.cumsum(padded)
    pstarts = pends - padded
    experts = jnp.arange(ne, dtype=jnp.int32)
    pstart_sel = jnp.sum(jnp.where(top_idx[..., None] == experts, pstarts, 0), axis=-1)
    dest = (pstart_sel + rank).T.reshape(-1)
    p_rows = t * TOP_K + ne * BM
    nb = p_rows // BM
    blk_start = jnp.arange(nb, dtype=jnp.int32) * BM
    blk_e = jnp.minimum(jnp.sum((pends[None, :] <= blk_start[:, None]).astype(jnp.int32), axis=1), ne - 1)
    n_used = (pends[-1:] // BM).astype(jnp.int32)

    n_assign = t * TOP_K
    valid_end = pstarts + counts
    pad_lo = jnp.concatenate([valid_end, pends[-1:]]).astype(jnp.int32)
    pad_hi = jnp.concatenate([pends, jnp.full((1,), p_rows, jnp.int32)]).astype(jnp.int32)
    slot_assign = _invert_slots(dest, pad_lo, pad_hi, p_rows)
    blk_end = jnp.sum(jnp.where(blk_e[:, None] == experts, valid_end, 0), axis=-1)
    nvalid = jnp.clip(blk_end - blk_start, 0, BM).astype(jnp.int32)
    nvalid = jnp.where(jnp.arange(nb) < n_used[0], nvalid, 0)

    lookup = lambda table: jnp.sum(jnp.where(blk_e[:, None] == experts, table, 0), axis=-1)
    nonempty = padded > 0
    order = jnp.cumsum(nonempty.astype(jnp.int32)) - 1
    later = lax.cummin(jnp.where(nonempty, experts, ne)[::-1])[::-1]
    next_e = jnp.concatenate([later[1:], jnp.full((1,), ne, jnp.int32)])
    blk_next = lookup(next_e)
    has_next = (blk_next < ne) & (jnp.arange(nb) < n_used[0])
    k_in_run = jnp.arange(nb, dtype=jnp.int32) - lookup(pstarts // BM)
    n_in_run = jnp.maximum(lookup(padded // BM), 1)
    common = dict(blk_e=blk_e, n_used=n_used, nvalid=nvalid, wslot=lookup(order) % 2,
                  wnext=jnp.where(has_next, blk_next, blk_e))

    def schedule(w):
        nch = w.shape[1] // _chunk_rows(w.shape[2])
        s = dict(common, wc0=jnp.where(has_next, k_in_run * nch // n_in_run, 0),
                 wc1=jnp.where(has_next, (k_in_run + 1) * nch // n_in_run, 0))
        return {k: v.astype(jnp.int32) for k, v in s.items()}

    src_tok = jnp.maximum(slot_assign, 0) % t
    act = _moe_up(h2s, src_tok, w_gate_up, b_gate_up, schedule(w_gate_up))
    ysc = _moe_down(act, slot_assign, n_assign, w_down, b_down, schedule(w_down))
    return _final(ysc, rt, x1, ada3, g_ffn_post, seq)


def kernel(x, c, w_ada, b_ada, g_mix_pre, g_mix_post, g_ffn_pre, g_ffn_post, w_in, g_rec_out, w_branch_rec,
           w_branch_att, w_o, w_router, b_router, w_gate_up, b_gate_up, w_down, b_down, rec_lb_table, rel_bias):
    bsz, seq, d = x.shape
    depth = w_in.shape[0]
    lb_all = jnp.cumsum(jax.nn.softmax(rec_lb_table.astype(F32), axis=1), axis=1)
    x2 = x.reshape(bsz * seq, d)
    for layer in range(depth):
        ada3 = _ada(c, w_ada[layer], b_ada[layer]).reshape(bsz, 6, d)
        x2 = _mixer_ffn_layer(x2, ada3, bsz, seq, g_mix_pre[layer], g_mix_post[layer], g_ffn_pre[layer],
                              g_ffn_post[layer], w_in[layer], lb_all[:, layer], g_rec_out[layer], rel_bias,
                              w_branch_rec[layer], w_branch_att[layer], w_o[layer], w_router[layer],
                              b_router[layer], w_gate_up[layer], b_gate_up[layer], w_down[layer],
                              b_down[layer])
    return x2.reshape(bsz, seq, d)
```
